```python
import jax, jax.numpy as jnp
from jax import lax
import numpy as np

D_MODEL = 1024
BATCH = 4
SEQ = 4096
DEPTH = 1
DEC_BATCH = 32
DEC_SEQ = 64
PAST_LEN = 2048

CHUNK = 64
Q_BLOCK = 128
EPS = 1e-6
MLA_HEADS = 4
Q_LORA = 384
KV_LORA = 256
QK_NOPE = 128
QK_ROPE = 64
QK_DIM = QK_NOPE + QK_ROPE
V_DIM = 128
ROPE_THETA = 10000.0
MLA_WIDTH = MLA_HEADS * V_DIM
GLA_HEADS = 4
GLA_DK = 64
GLA_DV = 128
GLA_GATE_RANK = 16
GLA_TAU = 16.0
GLA_WIDTH = GLA_HEADS * GLA_DV
MIX_WIDTH = MLA_WIDTH + GLA_WIDTH
IN_SPLITS = (Q_LORA, KV_LORA, QK_ROPE, GLA_HEADS * GLA_DK, GLA_HEADS * GLA_DK, GLA_WIDTH, GLA_GATE_RANK, GLA_WIDTH)
IN_COLS = Q_LORA + KV_LORA + QK_ROPE + 2 * GLA_HEADS * GLA_DK + GLA_WIDTH + GLA_GATE_RANK + GLA_WIDTH
D_FF = 4 * D_MODEL

kernel_name = 'hymba_mla_gla_adaln_stream_step'


def rms_norm(x, g):
    xf = x.astype(jnp.float32)
    y = xf * lax.rsqrt(jnp.mean(xf * xf, axis=-1, keepdims=True) + EPS)
    return (y * g.astype(jnp.float32)).astype(x.dtype)


def rope(x, pos):
    half = QK_ROPE // 2
    inv = jnp.power(ROPE_THETA, -jnp.arange(half, dtype=jnp.float32) / half)
    ang = pos.astype(jnp.float32)[:, None] * inv[None, :]
    cos = jnp.cos(ang)[:, None, :]
    sin = jnp.sin(ang)[:, None, :]
    xf = x.astype(jnp.float32)
    x1, x2 = xf[..., :half], xf[..., half:]
    return jnp.concatenate([x1 * cos - x2 * sin, x2 * cos + x1 * sin], axis=-1).astype(x.dtype)


def chunk_causal_attention(q, k, v, q_pos, k_pos):
    B, T, H, D = q.shape
    qb = min(Q_BLOCK, T)
    nb = T // qb
    q_blocks = q.reshape(B, nb, qb, H, D).swapaxes(0, 1)
    pos_blocks = q_pos.reshape(nb, qb)
    k_chunk = k_pos // CHUNK
    scale = QK_DIM ** -0.5

    def attend(args):
        qblk, pblk = args
        s = jnp.einsum('bqhd,bkhd->bhqk', qblk, k).astype(jnp.float32) * scale
        visible = k_chunk[None, :] <= (pblk // CHUNK)[:, None]
        p = jax.nn.softmax(jnp.where(visible, s, -jnp.inf), axis=-1)
        return jnp.einsum('bhqk,bkhd->bqhd', p.astype(v.dtype), v)

    o = lax.map(attend, (q_blocks, pos_blocks))
    return o.swapaxes(0, 1).reshape(B, T, H, v.shape[-1])


def mla_mixer(c_q, c_kv, k_rope_raw, past_latent, past_krope,
              g_q_lora, w_uq, g_kv_lora, w_ukv, g_q_head, g_k_head):
    B, T, _ = c_q.shape
    P = past_latent.shape[1]
    S = P + T
    q = (rms_norm(c_q, g_q_lora) @ w_uq).reshape(B, T, MLA_HEADS, QK_DIM)
    latent_new = rms_norm(c_kv, g_kv_lora)
    latent = jnp.concatenate([past_latent.astype(latent_new.dtype), latent_new], axis=1)
    krope = jnp.concatenate([past_krope.astype(k_rope_raw.dtype), k_rope_raw], axis=1)
    kv = (latent @ w_ukv).reshape(B, S, MLA_HEADS, QK_NOPE + V_DIM)
    k_nope, v = kv[..., :QK_NOPE], kv[..., QK_NOPE:]
    k = jnp.concatenate([k_nope, jnp.broadcast_to(krope[:, :, None, :], (B, S, MLA_HEADS, QK_ROPE))], axis=-1)
    q = rms_norm(q, g_q_head)
    k = rms_norm(k, g_k_head)
    q_pos = P + jnp.arange(T)
    k_pos = jnp.arange(S)
    q = jnp.concatenate([q[..., :QK_NOPE], rope(q[..., QK_NOPE:], q_pos)], axis=-1)
    k = jnp.concatenate([k[..., :QK_NOPE], rope(k[..., QK_NOPE:], k_pos)], axis=-1)
    o = chunk_causal_attention(q, k, v, q_pos, k_pos)
    return o.reshape(B, T, MLA_WIDTH), latent_new, k_rope_raw


def gla_recurrence(q, k, v, log_a, s0):
    B, T, H, DK = q.shape
    DV = v.shape[-1]
    L = min(CHUNK, T)
    n = T // L

    def chunks(a):
        return a.astype(jnp.float32).reshape(B, n, L, *a.shape[2:]).swapaxes(0, 1)

    causal = jnp.tril(jnp.ones((L, L), dtype=bool))[None, :, :, None, None]

    def step(state, xs):
        qc, kc, vc, ac = xs
        b = jnp.cumsum(ac, axis=1)
        decay = jnp.exp(jnp.where(causal, b[:, :, None] - b[:, None, :], -jnp.inf))
        scores = jnp.einsum('bthk,bshk,btshk->bths', qc, kc, decay)
        o = (jnp.einsum('bths,bshv->bthv', scores, vc)
             + jnp.einsum('bthk,bhkv->bthv', qc * jnp.exp(b), state))
        b_end = b[:, -1]
        state = (state * jnp.exp(b_end)[..., None]
                 + jnp.einsum('bshk,bshv->bhkv', kc * jnp.exp(b_end[:, None] - b), vc))
        return state, o

    s_final, o = lax.scan(step, s0.astype(jnp.float32), (chunks(q), chunks(k), chunks(v), chunks(log_a)))
    return o.swapaxes(0, 1).reshape(B, T, H, DV), s_final.astype(s0.dtype)


def gla_mixer(q, k, v, gate_lr, r, s0, w_gate_up, b_gate_up, g_gla_out):
    B, T, _ = q.shape
    q = q.reshape(B, T, GLA_HEADS, GLA_DK) * (GLA_DK ** -0.5)
    k = k.reshape(B, T, GLA_HEADS, GLA_DK)
    v = v.reshape(B, T, GLA_HEADS, GLA_DV)
    log_a = jax.nn.log_sigmoid((gate_lr @ w_gate_up + b_gate_up).astype(jnp.float32)) / GLA_TAU
    log_a = log_a.reshape(B, T, GLA_HEADS, GLA_DK)
    o, s_new = gla_recurrence(q, k, v, log_a, s0)
    o = rms_norm(o.astype(r.dtype), g_gla_out.reshape(GLA_HEADS, GLA_DV))
    return o.reshape(B, T, GLA_WIDTH) * jax.nn.silu(r), s_new


def encoder_layer(x, c, past_latent, past_krope, gla_s0,
                  w_ada, b_ada, g_norm1, w_in, g_q_lora, w_uq, g_kv_lora, w_ukv, g_q_head, g_k_head,
                  w_gate_up, b_gate_up, g_gla_out, w_out, g_norm2, w_up, w_down):
    mod = (jax.nn.silu(c.astype(jnp.float32)) @ w_ada.astype(jnp.float32)
           + b_ada.astype(jnp.float32)).astype(x.dtype)
    shift1, scale1, gate1, shift2, scale2, gate2 = [m[:, None, :] for m in jnp.split(mod, 6, axis=-1)]
    h = rms_norm(x, g_norm1) * (1 + scale1) + shift1
    proj = h @ w_in
    split_points = [int(i) for i in np.cumsum(IN_SPLITS)[:-1]]
    c_q, c_kv, k_rope_raw, gq, gk, gv, g_lr, g_r = jnp.split(proj, split_points, axis=-1)
    a_out, latent_new, krope_new = mla_mixer(c_q, c_kv, k_rope_raw, past_latent, past_krope,
                                             g_q_lora, w_uq, g_kv_lora, w_ukv, g_q_head, g_k_head)
    b_out, s_new = gla_mixer(gq, gk, gv, g_lr, g_r, gla_s0, w_gate_up, b_gate_up, g_gla_out)
    x = x + gate1 * (jnp.concatenate([a_out, b_out], axis=-1) @ w_out)
    h2 = rms_norm(x, g_norm2) * (1 + scale2) + shift2
    x = x + gate2 * (jnp.square(jax.nn.relu(h2 @ w_up)) @ w_down)
    return x, latent_new, krope_new, s_new


def setup_inputs(seed: int = 0) -> dict:
    key = jax.random.key(seed)
    ks = jax.random.split(key, 28)

    def nrm(k, shape, s):
        return jax.random.normal(k, shape, jnp.float32) * s

    def gain(k, shape):
        return 1.0 + 0.02 * jax.random.normal(k, shape, jnp.float32)

    return {
        'x_prompt': nrm(ks[0], (BATCH, SEQ, D_MODEL), 1.0),
        'x_sample': nrm(ks[1], (DEC_BATCH, DEC_SEQ, D_MODEL), 1.0),
        'cache_mla_latent': nrm(ks[2], (DEPTH, DEC_BATCH, PAST_LEN, KV_LORA), 1.0),
        'cache_mla_krope': nrm(ks[3], (DEPTH, DEC_BATCH, PAST_LEN, QK_ROPE), 1.0),
        'state_gla': nrm(ks[4], (DEPTH, DEC_BATCH, GLA_HEADS, GLA_DK, GLA_DV), 1.0),
        'c_prompt': nrm(ks[5], (BATCH, D_MODEL), 1.0),
        'c_sample': nrm(ks[6], (DEC_BATCH, D_MODEL), 1.0),
        'w_ada': nrm(ks[7], (DEPTH, D_MODEL, 6 * D_MODEL), D_MODEL ** -0.5),
        'b_ada': nrm(ks[8], (DEPTH, 6 * D_MODEL), 0.02),
        'g_norm1': gain(ks[9], (DEPTH, D_MODEL)),
        'w_in': nrm(ks[10], (DEPTH, D_MODEL, IN_COLS), D_MODEL ** -0.5),
        'g_q_lora': gain(ks[11], (DEPTH, Q_LORA)),
        'w_uq': nrm(ks[12], (DEPTH, Q_LORA, MLA_HEADS * QK_DIM), Q_LORA ** -0.5),
        'g_kv_lora': gain(ks[13], (DEPTH, KV_LORA)),
        'w_ukv': nrm(ks[14], (DEPTH, KV_LORA, MLA_HEADS * (QK_NOPE + V_DIM)), KV_LORA ** -0.5),
        'g_q_head': gain(ks[15], (DEPTH, QK_DIM)),
        'g_k_head': gain(ks[16], (DEPTH, QK_DIM)),
        'w_gate_up': nrm(ks[17], (DEPTH, GLA_GATE_RANK, GLA_HEADS * GLA_DK), GLA_GATE_RANK ** -0.5),
        'b_gate_up': nrm(ks[18], (DEPTH, GLA_HEADS * GLA_DK), 0.1),
        'g_gla_out': gain(ks[19], (DEPTH, GLA_WIDTH)),
        'w_out': nrm(ks[20], (DEPTH, MIX_WIDTH, D_MODEL), MIX_WIDTH ** -0.5),
        'g_norm2': gain(ks[21], (DEPTH, D_MODEL)),
        'w_up': nrm(ks[22], (DEPTH, D_MODEL, D_FF), D_MODEL ** -0.5),
        'w_down': nrm(ks[23], (DEPTH, D_FF, D_MODEL), D_FF ** -0.5),
    }


def reference(x_prompt, x_sample, cache_mla_latent, cache_mla_krope, state_gla, c_prompt, c_sample,
              w_ada, b_ada, g_norm1, w_in, g_q_lora, w_uq, g_kv_lora, w_ukv, g_q_head, g_k_head,
              w_gate_up, b_gate_up, g_gla_out, w_out, g_norm2, w_up, w_down):
    B = x_prompt.shape[0]
    y_p, y_s = x_prompt, x_sample
    lat_p, kr_p, st_p, lat_s, kr_s, st_s = [], [], [], [], [], []
    for l in range(DEPTH):
        params = (w_ada[l], b_ada[l], g_norm1[l], w_in[l], g_q_lora[l], w_uq[l], g_kv_lora[l], w_ukv[l],
                  g_q_head[l], g_k_head[l], w_gate_up[l], b_gate_up[l], g_gla_out[l], w_out[l],
                  g_norm2[l], w_up[l], w_down[l])
        empty_lat = jnp.zeros((B, 0, KV_LORA), x_prompt.dtype)
        empty_kr = jnp.zeros((B, 0, QK_ROPE), x_prompt.dtype)
        zero_state = jnp.zeros((B, GLA_HEADS, GLA_DK, GLA_DV), x_prompt.dtype)
        y_p, a, b, s = encoder_layer(y_p, c_prompt, empty_lat, empty_kr, zero_state, *params)
        lat_p.append(a); kr_p.append(b); st_p.append(s)
        y_s, a, b, s = encoder_layer(y_s, c_sample, cache_mla_latent[l], cache_mla_krope[l], state_gla[l], *params)
        lat_s.append(a); kr_s.append(b); st_s.append(s)
    return (y_p, y_s, jnp.stack(lat_p), jnp.stack(kr_p), jnp.stack(st_p),
            jnp.stack(lat_s), jnp.stack(kr_s), jnp.stack(st_s))
```

```python
import functools

import jax
import jax.numpy as jnp
import numpy as np
from jax import lax
from jax.experimental import pallas as pl
from jax.experimental.pallas import tpu as pltpu

F32 = jnp.float32
BF16 = jnp.bfloat16

CHUNK = 64
EPS = 1e-6
MLA_HEADS = 4
Q_LORA = 384
KV_LORA = 256
QK_NOPE = 128
QK_ROPE = 64
QK_DIM = QK_NOPE + QK_ROPE
V_DIM = 128
ROPE_THETA = 10000.0
GLA_HEADS = 4
GLA_DK = 64
GLA_DV = 128
GLA_GATE_RANK = 16
GLA_TAU = 16.0
GLA_QK = GLA_HEADS * GLA_DK
GLA_WIDTH = GLA_HEADS * GLA_DV
MLA_WIDTH = MLA_HEADS * V_DIM
HEAD_PAD = 256
SUB = 16

LANES = 128
VMEM_LIMIT = 56 * 1024 * 1024

C_Q = (0, 384)
C_KV = (384, 640)
C_GQ = (640, 896)
C_GK = (896, 1152)
C_GV = (1152, 1664)
C_GR = (1664, 2176)
C_KRG = (2176, 2304)
IN_COLS_P = 2304


def _dot(a, b):
    return jnp.dot(a, b, preferred_element_type=F32)


def _dot_nt(a, b):
    return lax.dot_general(a, b, (((1,), (1,)), ((), ())), preferred_element_type=F32)


def _dot_tn(a, b):
    return lax.dot_general(a, b, (((0,), (0,)), ((), ())), preferred_element_type=F32)


def _rope_tile(t, c, sa, sb):
    return t * c + pltpu.roll(t, 96, 1) * sa + pltpu.roll(t, 32, 1) * sb


def _params(*sem):
    return pltpu.CompilerParams(dimension_semantics=sem, vmem_limit_bytes=VMEM_LIMIT)


def _const(shape):
    return pl.BlockSpec(shape, lambda *_: (0,) * len(shape), pipeline_mode=pl.Buffered(1))


def _mod_kernel(c_ref, w_ref, b_ref, o_ref):
    c = c_ref[...]
    s = (c * jax.nn.sigmoid(c)).astype(BF16)
    o_ref[...] = _dot(s, w_ref[...].astype(BF16)) + b_ref[...]


def _modulation(c_all, w_ada, b_ada):
    g, d = c_all.shape
    n = w_ada.shape[1]
    tn = 1024
    return pl.pallas_call(
        _mod_kernel,
        grid=(n // tn,),
        in_specs=[pl.BlockSpec((g, d), lambda j: (0, 0)),
                  pl.BlockSpec((d, tn), lambda j: (0, j)),
                  pl.BlockSpec((1, tn), lambda j: (0, j))],
        out_specs=pl.BlockSpec((g, tn), lambda j: (0, j)),
        out_shape=jax.ShapeDtypeStruct((g, n), F32),
        compiler_params=_params("arbitrary"),
        name="adaln_mod",
    )(c_all, w_ada, b_ada.reshape(1, n))


def _proj_kernel(x_ref, sh_ref, sc_ref, g1_ref, win_ref, gql_ref, wuq_ref, gkv_ref, wukv_ref,
                 gqn_ref, gqr_ref, gkn_ref, gkr_ref, tab_ref, wg_ref, bg_ref,
                 lat_ref, kr_ref, q_ref, k_ref, v_ref, gq_ref, gk_ref, gv_ref, la_ref, gr_ref,
                 *, gpt):
    tm, d = x_ref.shape
    x = x_ref[...]
    xn = x * lax.rsqrt(jnp.mean(x * x, axis=-1, keepdims=True) + EPS) * g1_ref[...]
    h = (xn.reshape(gpt, tm // gpt, d) * (1.0 + sc_ref[...]) + sh_ref[...]).reshape(tm, d)
    hb = h.astype(BF16)

    def col(c):
        return _dot(hb, win_ref[:, c[0]:c[1]])

    tab = tab_ref[...]
    cos, sa, sb = tab[:, 0:LANES], tab[:, LANES:2 * LANES], tab[:, 2 * LANES:3 * LANES]

    cq = col(C_Q)
    cqn = cq * lax.rsqrt(jnp.mean(cq * cq, axis=-1, keepdims=True) + EPS) * gql_ref[...]
    qp = _dot(cqn.astype(BF16), wuq_ref[...])
    qscale = QK_DIM ** -0.5
    for hh in range(MLA_HEADS):
        nope = qp[:, hh * HEAD_PAD:hh * HEAD_PAD + QK_NOPE]
        rt = qp[:, hh * HEAD_PAD + QK_NOPE:(hh + 1) * HEAD_PAD]
        ss = jnp.sum(nope * nope, axis=-1, keepdims=True) + jnp.sum(rt * rt, axis=-1, keepdims=True)
        inv = lax.rsqrt(ss * (1.0 / QK_DIM) + EPS)
        q_ref[:, hh * HEAD_PAD:hh * HEAD_PAD + QK_NOPE] = (nope * inv * gqn_ref[...] * qscale).astype(BF16)
        rq = _rope_tile(rt * inv * gqr_ref[...], cos, sa, sb)
        q_ref[:, hh * HEAD_PAD + QK_NOPE:(hh + 1) * HEAD_PAD] = (rq * qscale).astype(BF16)

    ckv = col(C_KV)
    lat = ckv * lax.rsqrt(jnp.mean(ckv * ckv, axis=-1, keepdims=True) + EPS) * gkv_ref[...]
    lat_ref[...] = lat
    krg = col(C_KRG)
    kr_ref[...] = krg[:, 0:QK_ROPE]
    lane = lax.broadcasted_iota(jnp.int32, (1, LANES), 1)
    krm = jnp.where(lane < QK_ROPE, krg, 0.0)
    ssr = jnp.sum(krm * krm, axis=-1, keepdims=True)
    rk = _rope_tile(krm * gkr_ref[...], cos, sa, sb)
    kv = _dot(lat.astype(BF16), wukv_ref[...])
    for hh in range(MLA_HEADS):
        kn = kv[:, hh * QK_NOPE:(hh + 1) * QK_NOPE]
        inv = lax.rsqrt((jnp.sum(kn * kn, axis=-1, keepdims=True) + ssr) * (1.0 / QK_DIM) + EPS)
        k_ref[:, hh * HEAD_PAD:hh * HEAD_PAD + QK_NOPE] = (kn * inv * gkn_ref[...]).astype(BF16)
        k_ref[:, hh * HEAD_PAD + QK_NOPE:(hh + 1) * HEAD_PAD] = (rk * inv).astype(BF16)
    v_ref[...] = kv[:, MLA_HEADS * QK_NOPE:].astype(BF16)

    gq_ref[...] = col(C_GQ) * (GLA_DK ** -0.5)
    gk_ref[...] = col(C_GK)
    gv_ref[...] = col(C_GV).astype(BF16)
    r = col(C_GR)
    gr_ref[...] = (r * jax.nn.sigmoid(r)).astype(BF16)
    z = _dot(krg.astype(BF16), wg_ref[...]) + bg_ref[...]
    la_ref[...] = (jnp.minimum(z, 0.0) - jnp.log(1.0 + jnp.exp(-jnp.abs(z)))) * (1.0 / GLA_TAU)


def _projection(x2, mod4, tab, w, *, rows_per_group, tm):
    n, d = x2.shape
    gpt = max(1, tm // rows_per_group)
    tpg = max(1, rows_per_group // tm)
    ntab = tab.shape[0] // tm

    def mod_spec(j):
        return pl.BlockSpec((gpt, None, 1, d), lambda i: ((i // tpg) if gpt == 1 else i, j, 0, 0))

    def rows(c):
        return pl.BlockSpec((tm, c), lambda i: (i, 0))

    outs = [(KV_LORA, F32), (QK_ROPE, F32), (MLA_HEADS * HEAD_PAD, BF16), (MLA_HEADS * HEAD_PAD, BF16),
            (MLA_WIDTH, BF16), (GLA_QK, F32), (GLA_QK, F32), (GLA_WIDTH, BF16), (GLA_QK, F32),
            (GLA_WIDTH, BF16)]
    return pl.pallas_call(
        functools.partial(_proj_kernel, gpt=gpt),
        grid=(n // tm,),
        in_specs=[rows(d), mod_spec(0), mod_spec(1), _const((1, d)), _const(w["w_in"].shape),
                  _const((1, Q_LORA)), _const(w["w_uq"].shape), _const((1, KV_LORA)), _const(w["w_ukv"].shape),
                  _const((1, LANES)), _const((1, LANES)), _const((1, LANES)), _const((1, LANES)),
                  pl.BlockSpec((tm, 3 * LANES), lambda i: (i % ntab, 0)),
                  _const(w["w_gate"].shape), _const((1, GLA_QK))],
        out_specs=[rows(c) for c, _ in outs],
        out_shape=[jax.ShapeDtypeStruct((n, c), t) for c, t in outs],
        compiler_params=_params("arbitrary"),
        name="in_proj",
    )(x2, mod4, mod4, w["g_norm1"], w["w_in"], w["g_q_lora"], w["w_uq"], w["g_kv_lora"], w["w_ukv"],
      w["gqn"], w["gqr"], w["gkn"], w["gkr"], tab, w["w_gate"], w["b_gate"])


def _attn_prompt_kernel(q_ref, k_ref, v_ref, o_ref, *, tq):
    i = pl.program_id(2)
    q = q_ref[...]

    def block(j, carry, masked):
        m, l, acc = carry
        start = pl.multiple_of(j * tq, tq)
        s = _dot_nt(q, k_ref[pl.ds(start, tq), :])
        if masked:
            rc = lax.broadcasted_iota(jnp.int32, (tq, tq), 0) // CHUNK
            cc = lax.broadcasted_iota(jnp.int32, (tq, tq), 1) // CHUNK
            s = jnp.where(cc <= rc, s, -jnp.inf)
        m_new = jnp.maximum(m, jnp.max(s, axis=-1, keepdims=True))
        p = jnp.exp(s - m_new)
        alpha = jnp.exp(m - m_new)
        l = alpha * l + jnp.sum(p, axis=-1, keepdims=True)
        acc = alpha * acc + _dot(p.astype(BF16), v_ref[pl.ds(start, tq), :])
        return m_new, l, acc

    init = (jnp.full((tq, 1), -jnp.inf, F32), jnp.zeros((tq, 1), F32), jnp.zeros((tq, V_DIM), F32))
    carry = lax.fori_loop(0, i, lambda j, c: block(j, c, False), init)
    m, l, acc = block(i, carry, True)
    o_ref[...] = (acc / l).astype(BF16)


def _attention_prompt(q, k, v, *, batch, seq, tq):
    nq = seq // tq
    return pl.pallas_call(
        functools.partial(_attn_prompt_kernel, tq=tq),
        grid=(batch, MLA_HEADS, nq),
        in_specs=[pl.BlockSpec((tq, HEAD_PAD), lambda b, h, i: (b * nq + i, h)),
                  pl.BlockSpec((seq, HEAD_PAD), lambda b, h, i: (b, h)),
                  pl.BlockSpec((seq, V_DIM), lambda b, h, i: (b, h))],
        out_specs=pl.BlockSpec((tq, V_DIM), lambda b, h, i: (b * nq + i, h)),
        out_shape=jax.ShapeDtypeStruct((batch * seq, MLA_WIDTH), BF16),
        compiler_params=_params("arbitrary", "arbitrary", "arbitrary"),
        name="mla_attn_prompt",
    )(q, k, v)


def _attn_sample_kernel(q_ref, lat_ref, kr_ref, tab_ref, kn_ref, vn_ref, wukv_ref, gkn_ref, gkr_ref,
                        o_ref, m_ref, l_ref, acc_ref):
    j = pl.program_id(1)
    nj = pl.num_programs(1)
    t = q_ref.shape[0]

    @pl.when(j == 0)
    def _():
        m_ref[...] = jnp.full(m_ref.shape, -jnp.inf, F32)
        l_ref[...] = jnp.zeros(l_ref.shape, F32)
        acc_ref[...] = jnp.zeros(acc_ref.shape, F32)

    def update(hh, kh, vh):
        s = _dot_nt(q_ref[:, hh * HEAD_PAD:(hh + 1) * HEAD_PAD], kh)
        m = m_ref[hh]
        m_new = jnp.maximum(m, jnp.max(s, axis=-1, keepdims=True))
        p = jnp.exp(s - m_new)
        alpha = jnp.exp(m - m_new)
        l_ref[hh] = alpha * l_ref[hh] + jnp.sum(p, axis=-1, keepdims=True)
        acc_ref[hh] = alpha * acc_ref[hh] + _dot(p.astype(BF16), vh)
        m_ref[hh] = m_new

    tab = tab_ref[...]
    cos, sa, sb = tab[:, 0:LANES], tab[:, LANES:2 * LANES], tab[:, 2 * LANES:3 * LANES]
    kv = _dot(lat_ref[...].astype(BF16), wukv_ref[...])
    kr = kr_ref[...]
    krm = jnp.concatenate([kr, jnp.zeros_like(kr)], axis=1)
    ssr = jnp.sum(kr * kr, axis=-1, keepdims=True)
    rk = _rope_tile(krm * gkr_ref[...], cos, sa, sb)
    for hh in range(MLA_HEADS):
        kn = kv[:, hh * QK_NOPE:(hh + 1) * QK_NOPE]
        inv = lax.rsqrt((jnp.sum(kn * kn, axis=-1, keepdims=True) + ssr) * (1.0 / QK_DIM) + EPS)
        kh = jnp.concatenate([(kn * inv * gkn_ref[...]).astype(BF16), (rk * inv).astype(BF16)], axis=1)
        vh = kv[:, (MLA_HEADS + hh) * V_DIM:(MLA_HEADS + hh + 1) * V_DIM].astype(BF16)
        update(hh, kh, vh)

    @pl.when(j == nj - 1)
    def _():
        for hh in range(MLA_HEADS):
            update(hh, kn_ref[:, hh * HEAD_PAD:(hh + 1) * HEAD_PAD], vn_ref[:, hh * V_DIM:(hh + 1) * V_DIM])
            o_ref[:, hh * V_DIM:(hh + 1) * V_DIM] = (acc_ref[hh] / l_ref[hh]).astype(BF16)


def _attention_sample(q, past_lat, past_kr, tab_past, k_new, v_new, w, *, tkb):
    batch, past, _ = past_lat.shape
    t = q.shape[0] // batch
    nj = past // tkb
    return pl.pallas_call(
        _attn_sample_kernel,
        grid=(batch, nj),
        in_specs=[pl.BlockSpec((t, MLA_HEADS * HEAD_PAD), lambda b, j: (b, 0)),
                  pl.BlockSpec((None, tkb, KV_LORA), lambda b, j: (b, j, 0)),
                  pl.BlockSpec((None, tkb, QK_ROPE), lambda b, j: (b, j, 0)),
                  pl.BlockSpec((tkb, 3 * LANES), lambda b, j: (j, 0)),
                  pl.BlockSpec((t, MLA_HEADS * HEAD_PAD), lambda b, j: (b, 0)),
                  pl.BlockSpec((t, MLA_WIDTH), lambda b, j: (b, 0)),
                  _const(w["w_ukv"].shape), _const((1, LANES)), _const((1, LANES))],
        out_specs=pl.BlockSpec((t, MLA_WIDTH), lambda b, j: (b, 0)),
        out_shape=jax.ShapeDtypeStruct((batch * t, MLA_WIDTH), BF16),
        scratch_shapes=[pltpu.VMEM((MLA_HEADS, t, 1), F32), pltpu.VMEM((MLA_HEADS, t, 1), F32),
                        pltpu.VMEM((MLA_HEADS, t, V_DIM), F32)],
        compiler_params=_params("arbitrary", "arbitrary"),
        name="mla_attn_sample",
    )(q, past_lat, past_kr, tab_past, k_new, v_new, w["w_ukv"], w["gkn"], w["gkr"])


def _gla_kernel(q_ref, k_ref, v_ref, la_ref, r_ref, s0_ref, g_ref, o_ref, sn_ref, st_ref, kp_ref, bp_ref,
                *, n_chunks):
    t_idx = pl.program_id(1)
    L = CHUNK
    W = GLA_QK

    @pl.when(t_idx == 0)
    def _():
        st_ref[...] = s0_ref[...].reshape(W, GLA_DV).T
        kp_ref[0:SUB, :] = jnp.zeros((SUB, W), F32)
        bp_ref[0:SUB, :] = jnp.zeros((SUB, W), F32)

    row = lax.broadcasted_iota(jnp.int32, (L, W), 0)
    lane = lax.broadcasted_iota(jnp.int32, (L, W), 1)
    key = lane % L
    band = jnp.where(row // SUB == key // SUB, row - key, -1)
    tri = (lax.broadcasted_iota(jnp.int32, (L, L), 0) >= lax.broadcasted_iota(jnp.int32, (L, L), 1)).astype(BF16)
    spread = (lax.broadcasted_iota(jnp.int32, (W, W), 0) // L
              == lax.broadcasted_iota(jnp.int32, (W, W), 1) // L).astype(BF16)
    head_of_lane = lax.broadcasted_iota(jnp.int32, (1, W), 1) // L

    def chunk(c, _):
        start = pl.multiple_of(c * L, L)
        q = q_ref[pl.ds(start, L), :]
        k = k_ref[pl.ds(start, L), :]
        la = la_ref[pl.ds(start, L), :]
        la_hi = la.astype(BF16)
        la_lo = (la - la_hi.astype(F32)).astype(BF16)
        b = _dot(tri, la_hi) + _dot(tri, la_lo)
        b_end = b[L - 1:L, :]
        kp_ref[SUB:SUB + L, :] = k
        bp_ref[SUB:SUB + L, :] = b

        ref_rows = jnp.concatenate(
            [jnp.broadcast_to(b[i * SUB:i * SUB + 1, :], (SUB, W)) for i in range(L // SUB)], axis=0)
        qt = q * jnp.exp(b - ref_rows)
        lhs, rhs = [], []
        for i in range(1, L // SUB):
            lhs.append(jnp.where(row // SUB == i, qt, 0.0))
            rhs.append(jnp.where(row < i * SUB, k * jnp.exp(b[i * SUB:i * SUB + 1, :] - b), 0.0).astype(BF16))
        lhs = jnp.concatenate(lhs, axis=1)
        rhs = jnp.concatenate(rhs, axis=1)
        head3 = jnp.concatenate([head_of_lane] * (L // SUB - 1), axis=1)

        a_band = jnp.zeros((L, W), F32)
        for d in range(SUB):
            if d == 0:
                p = q * k
            else:
                p = q * kp_ref[SUB - d:SUB - d + L, :] * jnp.exp(b - bp_ref[SUB - d:SUB - d + L, :])
            a_band = jnp.where(band == d, _dot(p.astype(BF16), spread), a_band)

        qe = q * jnp.exp(b)
        kd = (k * jnp.exp(b_end - b)).astype(BF16)
        st = st_ref[...]
        st_b = st.astype(BF16)
        d_st = jnp.zeros((GLA_DV, W), F32)
        for hh in range(GLA_HEADS):
            vh = v_ref[pl.ds(start, L), hh * GLA_DV:(hh + 1) * GLA_DV]
            hm = head_of_lane == hh
            s_off = _dot_nt(jnp.where(head3 == hh, lhs, 0.0).astype(BF16), rhs)
            o = _dot(s_off.astype(BF16), vh)
            o += _dot(jnp.where(hm, a_band, 0.0).astype(BF16), jnp.concatenate([vh] * GLA_HEADS, axis=0))
            o += _dot_nt(jnp.where(hm, qe, 0.0).astype(BF16), st_b)
            on = o * lax.rsqrt(jnp.mean(o * o, axis=-1, keepdims=True) + EPS) * g_ref[:, hh * GLA_DV:(hh + 1) * GLA_DV]
            gate = r_ref[pl.ds(start, L), hh * GLA_DV:(hh + 1) * GLA_DV].astype(F32)
            o_ref[pl.ds(start, L), hh * GLA_DV:(hh + 1) * GLA_DV] = (on * gate).astype(BF16)
            d_st += jnp.where(hm, _dot_tn(vh, kd), 0.0)
        st_ref[...] = st * jnp.exp(b_end) + d_st
        return 0

    lax.fori_loop(0, n_chunks, chunk, 0)

    @pl.when(t_idx == pl.num_programs(1) - 1)
    def _():
        sn_ref[...] = st_ref[...].T.reshape(GLA_HEADS, GLA_DK, GLA_DV)


def _gla(gq, gk, gv, la, gr, s0, g_out, *, groups, rows_per_group, tc):
    nt = rows_per_group // tc

    def rows(c):
        return pl.BlockSpec((tc, c), lambda g, t: (g * nt + t, 0))

    state = pl.BlockSpec((None, GLA_HEADS, GLA_DK, GLA_DV), lambda g, t: (g, 0, 0, 0))
    return pl.pallas_call(
        functools.partial(_gla_kernel, n_chunks=tc // CHUNK),
        grid=(groups, nt),
        in_specs=[rows(GLA_QK), rows(GLA_QK), rows(GLA_WIDTH), rows(GLA_QK), rows(GLA_WIDTH), state,
                  _const((1, GLA_WIDTH))],
        out_specs=[rows(GLA_WIDTH), state],
        out_shape=[jax.ShapeDtypeStruct((groups * rows_per_group, GLA_WIDTH), BF16),
                   jax.ShapeDtypeStruct((groups, GLA_HEADS, GLA_DK, GLA_DV), F32)],
        scratch_shapes=[pltpu.VMEM((GLA_DV, GLA_QK), F32), pltpu.VMEM((SUB + CHUNK, GLA_QK), F32),
                        pltpu.VMEM((SUB + CHUNK, GLA_QK), F32)],
        compiler_params=_params("arbitrary", "arbitrary"),
        name="gla",
    )(gq, gk, gv, la, gr, s0, g_out)


def _mlp_kernel(x_ref, a_ref, b_ref, g1_ref, sh2_ref, sc2_ref, g2_ref, gn_ref, wo_ref, wu_ref, wd_ref, y_ref,
                *, gpt, tf):
    tm, d = x_ref.shape

    def per_group(val, ref, scale_plus_one=False):
        m = ref[...]
        if scale_plus_one:
            m = 1.0 + m
        return (val.reshape(gpt, tm // gpt, d) * m).reshape(tm, d)

    mix = jnp.concatenate([a_ref[...], b_ref[...]], axis=1)
    x1 = x_ref[...] + per_group(_dot(mix, wo_ref[...]), g1_ref)
    xn = x1 * lax.rsqrt(jnp.mean(x1 * x1, axis=-1, keepdims=True) + EPS) * gn_ref[...]
    h2 = (per_group(xn, sc2_ref, True).reshape(gpt, tm // gpt, d) + sh2_ref[...]).reshape(tm, d).astype(BF16)
    acc = jnp.zeros((tm, d), F32)
    for j in range(wu_ref.shape[1] // tf):
        u = jnp.maximum(_dot(h2, wu_ref[:, j * tf:(j + 1) * tf]), 0.0)
        acc += _dot((u * u).astype(BF16), wd_ref[j * tf:(j + 1) * tf, :])
    y_ref[...] = x1 + per_group(acc, g2_ref)


def _mlp(x2, a_out, b_out, mod4, w, *, rows_per_group, tm):
    n, d = x2.shape
    gpt = max(1, tm // rows_per_group)
    tpg = max(1, rows_per_group // tm)

    def mod_spec(j):
        return pl.BlockSpec((gpt, None, 1, d), lambda i: ((i // tpg) if gpt == 1 else i, j, 0, 0))

    def rows(c):
        return pl.BlockSpec((tm, c), lambda i: (i, 0))

    return pl.pallas_call(
        functools.partial(_mlp_kernel, gpt=gpt, tf=1024),
        grid=(n // tm,),
        in_specs=[rows(d), rows(MLA_WIDTH), rows(GLA_WIDTH), mod_spec(2), mod_spec(3), mod_spec(4), mod_spec(5),
                  _const((1, d)), _const(w["w_out"].shape), _const(w["w_up"].shape), _const(w["w_down"].shape)],
        out_specs=rows(d),
        out_shape=jax.ShapeDtypeStruct((n, d), F32),
        compiler_params=_params("arbitrary"),
        name="out_proj_mlp",
    )(x2, a_out, b_out, mod4, mod4, mod4, mod4, w["g_norm2"], w["w_out"], w["w_up"], w["w_down"])


def _rope_table(pos):
    half = QK_ROPE // 2
    inv = jnp.power(ROPE_THETA, -jnp.arange(half, dtype=F32) / half)
    ang = pos.astype(F32)[:, None] * inv[None, :]
    c, s, z = jnp.cos(ang), jnp.sin(ang), jnp.zeros_like(ang)
    return jnp.concatenate([c, c, z, z, -s, z, z, z, z, s, z, z], axis=1)


def _pad_gain(g_rope):
    return jnp.concatenate([g_rope, jnp.zeros((LANES - QK_ROPE,), F32)]).reshape(1, LANES)


def _prep_weights(w_in, g_norm1, g_q_lora, w_uq, g_kv_lora, w_ukv, g_q_head, g_k_head,
                  w_gate_up, b_gate_up, g_gla_out, w_out, g_norm2, w_up, w_down):
    d = w_in.shape[0]
    s = np.cumsum([0, Q_LORA, KV_LORA, QK_ROPE, GLA_QK, GLA_QK, GLA_WIDTH, GLA_GATE_RANK, GLA_WIDTH])
    piece = [w_in[:, s[i]:s[i + 1]] for i in range(8)]
    pad = jnp.zeros((d, LANES - QK_ROPE - GLA_GATE_RANK), w_in.dtype)
    w_in_p = jnp.concatenate([piece[0], piece[1], piece[3], piece[4], piece[5], piece[7],
                              piece[2], piece[6], pad], axis=1).astype(BF16)
    uq = w_uq.reshape(Q_LORA, MLA_HEADS, QK_DIM)
    uq = jnp.concatenate([uq, jnp.zeros((Q_LORA, MLA_HEADS, HEAD_PAD - QK_DIM), w_uq.dtype)], axis=2)
    ukv = w_ukv.reshape(KV_LORA, MLA_HEADS, QK_NOPE + V_DIM)
    ukv = jnp.concatenate([ukv[:, :, :QK_NOPE].reshape(KV_LORA, -1), ukv[:, :, QK_NOPE:].reshape(KV_LORA, -1)], axis=1)
    w_gate = jnp.zeros((LANES, GLA_QK), w_gate_up.dtype).at[QK_ROPE:QK_ROPE + GLA_GATE_RANK].set(w_gate_up)
    return {
        "w_in": w_in_p, "g_norm1": g_norm1.reshape(1, d), "g_q_lora": g_q_lora.reshape(1, Q_LORA),
        "w_uq": uq.reshape(Q_LORA, MLA_HEADS * HEAD_PAD).astype(BF16),
        "g_kv_lora": g_kv_lora.reshape(1, KV_LORA), "w_ukv": ukv.astype(BF16),
        "gqn": g_q_head[:QK_NOPE].reshape(1, LANES), "gqr": _pad_gain(g_q_head[QK_NOPE:]),
        "gkn": g_k_head[:QK_NOPE].reshape(1, LANES), "gkr": _pad_gain(g_k_head[QK_NOPE:]),
        "w_gate": w_gate.astype(BF16), "b_gate": b_gate_up.reshape(1, GLA_QK),
        "g_gla_out": g_gla_out.reshape(1, GLA_WIDTH), "w_out": w_out.astype(BF16),
        "g_norm2": g_norm2.reshape(1, d), "w_up": w_up.astype(BF16), "w_down": w_down.astype(BF16),
    }


def _layer(x, mod, past_lat, past_kr, s0, w, *, tm):
    batch, seq, d = x.shape
    n = batch * seq
    past = 0 if past_lat is None else past_lat.shape[1]
    x2 = x.reshape(n, d)
    mod4 = mod.reshape(batch, 6, 1, d)
    tm = min(tm, n)
    tab = _rope_table(past + jnp.arange(seq))
    if seq < tm:
        tab = jnp.tile(tab, (tm // seq, 1))
    lat, kr, q, k, v, gq, gk, gv, la, gr = _projection(x2, mod4, tab, w, rows_per_group=seq, tm=tm)
    if past == 0:
        a_out = _attention_prompt(q, k, v, batch=batch, seq=seq, tq=min(512, seq))
    else:
        assert seq == CHUNK and past % CHUNK == 0
        a_out = _attention_sample(q, past_lat, past_kr, _rope_table(jnp.arange(past)), k, v, w,
                                  tkb=min(512, past))
    b_out, s_new = _gla(gq, gk, gv, la, gr, s0, w["g_gla_out"], groups=batch, rows_per_group=seq,
                        tc=min(512, seq))
    y = _mlp(x2, a_out, b_out, mod4, w, rows_per_group=seq, tm=tm)
    return (y.reshape(batch, seq, d), lat.reshape(batch, seq, KV_LORA), kr.reshape(batch, seq, QK_ROPE), s_new)


def kernel(x_prompt, x_sample, cache_mla_latent, cache_mla_krope, state_gla, c_prompt, c_sample,
           w_ada, b_ada, g_norm1, w_in, g_q_lora, w_uq, g_kv_lora, w_ukv, g_q_head, g_k_head,
           w_gate_up, b_gate_up, g_gla_out, w_out, g_norm2, w_up, w_down):
    nb = x_prompt.shape[0]
    depth = w_ada.shape[0]
    y_p, y_s = x_prompt, x_sample
    outs = [[] for _ in range(6)]
    c_all = jnp.concatenate([c_prompt, c_sample], axis=0)
    for l in range(depth):
        w = _prep_weights(w_in[l], g_norm1[l], g_q_lora[l], w_uq[l], g_kv_lora[l], w_ukv[l], g_q_head[l],
                          g_k_head[l], w_gate_up[l], b_gate_up[l], g_gla_out[l], w_out[l], g_norm2[l],
                          w_up[l], w_down[l])
        mod = _modulation(c_all, w_ada[l], b_ada[l])
        zero_state = jnp.zeros((nb, GLA_HEADS, GLA_DK, GLA_DV), x_prompt.dtype)
        y_p, lat, kr, st = _layer(y_p, mod[:nb], None, None, zero_state, w, tm=512)
        outs[0].append(lat); outs[1].append(kr); outs[2].append(st)
        y_s, lat, kr, st = _layer(y_s, mod[nb:], cache_mla_latent[l], cache_mla_krope[l], state_gla[l], w, tm=512)
        outs[3].append(lat); outs[4].append(kr); outs[5].append(st)
    return (y_p, y_s) + tuple(jnp.stack(o) for o in outs)
```

```python
import functools

import jax
import jax.numpy as jnp
import numpy as np
from jax import lax
from jax.experimental import pallas as pl
from jax.experimental.pallas import tpu as pltpu

F32 = jnp.float32
BF16 = jnp.bfloat16

CHUNK = 64
EPS = 1e-6
MLA_HEADS = 4
Q_LORA = 384
KV_LORA = 256
QK_NOPE = 128
QK_ROPE = 64
QK_DIM = QK_NOPE + QK_ROPE
V_DIM = 128
ROPE_THETA = 10000.0
GLA_HEADS = 4
GLA_DK = 64
GLA_DV = 128
GLA_GATE_RANK = 16
GLA_TAU = 16.0
GLA_QK = GLA_HEADS * GLA_DK
GLA_WIDTH = GLA_HEADS * GLA_DV
MLA_WIDTH = MLA_HEADS * V_DIM
HEAD_PAD = 256
SUB = 16

LANES = 128
VMEM_LIMIT = 56 * 1024 * 1024

C_Q = (0, 384)
C_KV = (384, 640)
C_GQ = (640, 896)
C_GK = (896, 1152)
C_GV = (1152, 1664)
C_GR = (1664, 2176)
C_KRG = (2176, 2304)
IN_COLS_P = 2304


def _dot(a, b):
    return jnp.dot(a, b, preferred_element_type=F32)


def _dot_nt(a, b):
    return lax.dot_general(a, b, (((1,), (1,)), ((), ())), preferred_element_type=F32)


def _dot_tn(a, b):
    return lax.dot_general(a, b, (((0,), (0,)), ((), ())), preferred_element_type=F32)


def _rope_tile(t, c, sa, sb):
    return t * c + pltpu.roll(t, 96, 1) * sa + pltpu.roll(t, 32, 1) * sb


def _params(*sem):
    return pltpu.CompilerParams(dimension_semantics=sem, vmem_limit_bytes=VMEM_LIMIT)


def _const(shape):
    return pl.BlockSpec(shape, lambda *_: (0,) * len(shape), pipeline_mode=pl.Buffered(1))


def _mod_kernel(c_ref, w_ref, b_ref, o_ref):
    c = c_ref[...]
    s = (c * jax.nn.sigmoid(c)).astype(BF16)
    o_ref[...] = _dot(s, w_ref[...].astype(BF16)) + b_ref[...]


def _modulation(c_all, w_ada, b_ada):
    g, d = c_all.shape
    n = w_ada.shape[1]
    tn = 1024
    return pl.pallas_call(
        _mod_kernel,
        grid=(n // tn,),
        in_specs=[pl.BlockSpec((g, d), lambda j: (0, 0)),
                  pl.BlockSpec((d, tn), lambda j: (0, j)),
                  pl.BlockSpec((1, tn), lambda j: (0, j))],
        out_specs=pl.BlockSpec((g, tn), lambda j: (0, j)),
        out_shape=jax.ShapeDtypeStruct((g, n), F32),
        compiler_params=_params("arbitrary"),
        name="adaln_mod",
    )(c_all, w_ada, b_ada.reshape(1, n))


def _proj_kernel(x_ref, sh_ref, sc_ref, g1_ref, win_ref, gql_ref, wuq_ref, gkv_ref, wukv_ref,
                 gqn_ref, gqr_ref, gkn_ref, gkr_ref, tab_ref, wg_ref, bg_ref,
                 lat_ref, kr_ref, q_ref, k_ref, v_ref, gq_ref, gk_ref, gv_ref, la_ref, gr_ref,
                 *, gpt):
    tm, d = x_ref.shape
    x = x_ref[...]
    xn = x * lax.rsqrt(jnp.mean(x * x, axis=-1, keepdims=True) + EPS) * g1_ref[...]
    h = (xn.reshape(gpt, tm // gpt, d) * (1.0 + sc_ref[...]) + sh_ref[...]).reshape(tm, d)
    hb = h.astype(BF16)

    def col(c):
        return _dot(hb, win_ref[:, c[0]:c[1]])

    tab = tab_ref[...]
    cos, sa, sb = tab[:, 0:LANES], tab[:, LANES:2 * LANES], tab[:, 2 * LANES:3 * LANES]

    cq = col(C_Q)
    cqn = cq * lax.rsqrt(jnp.mean(cq * cq, axis=-1, keepdims=True) + EPS) * gql_ref[...]
    qp = _dot(cqn.astype(BF16), wuq_ref[...])
    qscale = QK_DIM ** -0.5
    for hh in range(MLA_HEADS):
        nope = qp[:, hh * HEAD_PAD:hh * HEAD_PAD + QK_NOPE]
        rt = qp[:, hh * HEAD_PAD + QK_NOPE:(hh + 1) * HEAD_PAD]
        ss = jnp.sum(nope * nope, axis=-1, keepdims=True) + jnp.sum(rt * rt, axis=-1, keepdims=True)
        inv = lax.rsqrt(ss * (1.0 / QK_DIM) + EPS)
        q_ref[:, hh * HEAD_PAD:hh * HEAD_PAD + QK_NOPE] = (nope * inv * gqn_ref[...] * qscale).astype(BF16)
        rq = _rope_tile(rt * inv * gqr_ref[...], cos, sa, sb)
        q_ref[:, hh * HEAD_PAD + QK_NOPE:(hh + 1) * HEAD_PAD] = (rq * qscale).astype(BF16)

    ckv = col(C_KV)
    lat = ckv * lax.rsqrt(jnp.mean(ckv * ckv, axis=-1, keepdims=True) + EPS) * gkv_ref[...]
    lat_ref[...] = lat
    krg = col(C_KRG)
    kr_ref[...] = krg[:, 0:QK_ROPE]
    lane = lax.broadcasted_iota(jnp.int32, (1, LANES), 1)
    krm = jnp.where(lane < QK_ROPE, krg, 0.0)
    ssr = jnp.sum(krm * krm, axis=-1, keepdims=True)
    rk = _rope_tile(krm * gkr_ref[...], cos, sa, sb)
    kv = _dot(lat.astype(BF16), wukv_ref[...])
    for hh in range(MLA_HEADS):
        kn = kv[:, hh * QK_NOPE:(hh + 1) * QK_NOPE]
        inv = lax.rsqrt((jnp.sum(kn * kn, axis=-1, keepdims=True) + ssr) * (1.0 / QK_DIM) + EPS)
        k_ref[:, hh * HEAD_PAD:hh * HEAD_PAD + QK_NOPE] = (kn * inv * gkn_ref[...]).astype(BF16)
        k_ref[:, hh * HEAD_PAD + QK_NOPE:(hh + 1) * HEAD_PAD] = (rk * inv).astype(BF16)
    v_ref[...] = kv[:, MLA_HEADS * QK_NOPE:].astype(BF16)

    gq_ref[...] = col(C_GQ) * (GLA_DK ** -0.5)
    gk_ref[...] = col(C_GK)
    gv_ref[...] = col(C_GV).astype(BF16)
    r = col(C_GR)
    gr_ref[...] = (r * jax.nn.sigmoid(r)).astype(BF16)
    z = _dot(krg.astype(BF16), wg_ref[...]) + bg_ref[...]
    la_ref[...] = (jnp.minimum(z, 0.0) - jnp.log(1.0 + jnp.exp(-jnp.abs(z)))) * (1.0 / GLA_TAU)


def _projection(x2, mod4, tab, w, *, rows_per_group, tm):
    n, d = x2.shape
    gpt = max(1, tm // rows_per_group)
    tpg = max(1, rows_per_group // tm)
    ntab = tab.shape[0] // tm

    def mod_spec(j):
        return pl.BlockSpec((gpt, None, 1, d), lambda i: ((i // tpg) if gpt == 1 else i, j, 0, 0))

    def rows(c):
        return pl.BlockSpec((tm, c), lambda i: (i, 0))

    outs = [(KV_LORA, F32), (QK_ROPE, F32), (MLA_HEADS * HEAD_PAD, BF16), (MLA_HEADS * HEAD_PAD, BF16),
            (MLA_WIDTH, BF16), (GLA_QK, F32), (GLA_QK, F32), (GLA_WIDTH, BF16), (GLA_QK, F32),
            (GLA_WIDTH, BF16)]
    return pl.pallas_call(
        functools.partial(_proj_kernel, gpt=gpt),
        grid=(n // tm,),
        in_specs=[rows(d), mod_spec(0), mod_spec(1), _const((1, d)), _const(w["w_in"].shape),
                  _const((1, Q_LORA)), _const(w["w_uq"].shape), _const((1, KV_LORA)), _const(w["w_ukv"].shape),
                  _const((1, LANES)), _const((1, LANES)), _const((1, LANES)), _const((1, LANES)),
                  pl.BlockSpec((tm, 3 * LANES), lambda i: (i % ntab, 0)),
                  _const(w["w_gate"].shape), _const((1, GLA_QK))],
        out_specs=[rows(c) for c, _ in outs],
        out_shape=[jax.ShapeDtypeStruct((n, c), t) for c, t in outs],
        compiler_params=_params("arbitrary"),
        name="in_proj",
    )(x2, mod4, mod4, w["g_norm1"], w["w_in"], w["g_q_lora"], w["w_uq"], w["g_kv_lora"], w["w_ukv"],
      w["gqn"], w["gqr"], w["gkn"], w["gkr"], tab, w["w_gate"], w["b_gate"])


def _attn_prompt_kernel(q_ref, k_ref, v_ref, o_ref, *, tq):
    i = pl.program_id(2)
    q = q_ref[...]

    def block(j, carry, masked):
        m, l, acc = carry
        start = pl.multiple_of(j * tq, tq)
        s = _dot_nt(q, k_ref[pl.ds(start, tq), :])
        if masked:
            rc = lax.broadcasted_iota(jnp.int32, (tq, tq), 0) // CHUNK
            cc = lax.broadcasted_iota(jnp.int32, (tq, tq), 1) // CHUNK
            s = jnp.where(cc <= rc, s, -jnp.inf)
        m_new = jnp.maximum(m, jnp.max(s, axis=-1, keepdims=True))
        p = jnp.exp(s - m_new)
        alpha = jnp.exp(m - m_new)
        l = alpha * l + jnp.sum(p, axis=-1, keepdims=True)
        acc = alpha * acc + _dot(p.astype(BF16), v_ref[pl.ds(start, tq), :])
        return m_new, l, acc

    init = (jnp.full((tq, 1), -jnp.inf, F32), jnp.zeros((tq, 1), F32), jnp.zeros((tq, V_DIM), F32))
    carry = lax.fori_loop(0, i, lambda j, c: block(j, c, False), init)
    m, l, acc = block(i, carry, True)
    o_ref[...] = (acc / l).astype(BF16)


def _attention_prompt(q, k, v, *, batch, seq, tq):
    nq = seq // tq
    return pl.pallas_call(
        functools.partial(_attn_prompt_kernel, tq=tq),
        grid=(batch, MLA_HEADS, nq),
        in_specs=[pl.BlockSpec((tq, HEAD_PAD), lambda b, h, i: (b * nq + i, h)),
                  pl.BlockSpec((seq, HEAD_PAD), lambda b, h, i: (b, h)),
                  pl.BlockSpec((seq, V_DIM), lambda b, h, i: (b, h))],
        out_specs=pl.BlockSpec((tq, V_DIM), lambda b, h, i: (b * nq + i, h)),
        out_shape=jax.ShapeDtypeStruct((batch * seq, MLA_WIDTH), BF16),
        compiler_params=_params("arbitrary", "arbitrary", "arbitrary"),
        name="mla_attn_prompt",
    )(q, k, v)


def _attn_sample_kernel(q_ref, lat_ref, kr_ref, tab_ref, kn_ref, vn_ref, wukv_ref, gkn_ref, gkr_ref,
                        o_ref, s_ref, v_ref, *, tkb):
    t = q_ref.shape[0]
    past = lat_ref.shape[0]
    pair = 2 * t

    @pl.when(pl.program_id(0) == 0)
    def _():
        v_ref[...] = jnp.ones(v_ref.shape, BF16)

    zq = jnp.zeros((t, HEAD_PAD), BF16)
    qpad = []
    for hh in range(MLA_HEADS):
        qh = q_ref[:, hh * HEAD_PAD:(hh + 1) * HEAD_PAD]
        qpad.append(jnp.concatenate([qh, zq] if hh % 2 == 0 else [zq, qh], axis=0))

    def scores(kh):
        return jnp.concatenate([_dot_nt(kh[2 * p], qpad[2 * p]) + _dot_nt(kh[2 * p + 1], qpad[2 * p + 1])
                                for p in range(MLA_HEADS // 2)], axis=1)

    for blk in range(past // tkb):
        r0 = blk * tkb
        tab = tab_ref[r0:r0 + tkb, :]
        cos, sa, sb = tab[:, 0:LANES], tab[:, LANES:2 * LANES], tab[:, 2 * LANES:3 * LANES]
        kv = _dot(lat_ref[r0:r0 + tkb, :].astype(BF16), wukv_ref[...])
        kr = kr_ref[r0:r0 + tkb, :]
        krm = jnp.concatenate([kr, jnp.zeros_like(kr)], axis=1)
        ssr = jnp.sum(kr * kr, axis=-1, keepdims=True)
        rk = _rope_tile(krm * gkr_ref[...], cos, sa, sb)
        kh = []
        for hh in range(MLA_HEADS):
            kn = kv[:, hh * QK_NOPE:(hh + 1) * QK_NOPE]
            inv = lax.rsqrt((jnp.sum(kn * kn, axis=-1, keepdims=True) + ssr) * (1.0 / QK_DIM) + EPS)
            kh.append(jnp.concatenate([(kn * inv * gkn_ref[...]).astype(BF16), (rk * inv).astype(BF16)], axis=1))
            v_ref[r0:r0 + tkb, 2 * hh * V_DIM:(2 * hh + 1) * V_DIM] = (
                kv[:, (MLA_HEADS + hh) * V_DIM:(MLA_HEADS + hh + 1) * V_DIM].astype(BF16))
        s_ref[r0:r0 + tkb, :] = scores(kh)
    s_ref[past:past + t, :] = scores([kn_ref[:, hh * HEAD_PAD:(hh + 1) * HEAD_PAD] for hh in range(MLA_HEADS)])
    for hh in range(MLA_HEADS):
        v_ref[past:past + t, 2 * hh * V_DIM:(2 * hh + 1) * V_DIM] = vn_ref[:, hh * V_DIM:(hh + 1) * V_DIM]

    s = s_ref[...]
    p = jnp.exp(s - jnp.max(s, axis=0, keepdims=True)).astype(BF16)
    for hh in range(MLA_HEADS):
        pp = p[:, (hh // 2) * pair:(hh // 2 + 1) * pair]
        ol = _dot_tn(pp, v_ref[:, 2 * hh * V_DIM:(2 * hh + 2) * V_DIM])
        ol = ol[(hh % 2) * t:(hh % 2 + 1) * t, :]
        o_ref[:, hh * V_DIM:(hh + 1) * V_DIM] = (ol[:, 0:V_DIM] / ol[:, V_DIM:2 * V_DIM]).astype(BF16)


def _attention_sample(q, past_lat, past_kr, tab_past, k_new, v_new, w, *, tkb):
    batch, past, _ = past_lat.shape
    t = q.shape[0] // batch
    return pl.pallas_call(
        functools.partial(_attn_sample_kernel, tkb=tkb),
        grid=(batch,),
        in_specs=[pl.BlockSpec((t, MLA_HEADS * HEAD_PAD), lambda b: (b, 0)),
                  pl.BlockSpec((None, past, KV_LORA), lambda b: (b, 0, 0)),
                  pl.BlockSpec((None, past, QK_ROPE), lambda b: (b, 0, 0)),
                  _const((past, 3 * LANES)),
                  pl.BlockSpec((t, MLA_HEADS * HEAD_PAD), lambda b: (b, 0)),
                  pl.BlockSpec((t, MLA_WIDTH), lambda b: (b, 0)),
                  _const(w["w_ukv"].shape), _const((1, LANES)), _const((1, LANES))],
        out_specs=pl.BlockSpec((t, MLA_WIDTH), lambda b: (b, 0)),
        out_shape=jax.ShapeDtypeStruct((batch * t, MLA_WIDTH), BF16),
        scratch_shapes=[pltpu.VMEM((past + t, MLA_HEADS * t), F32),
                        pltpu.VMEM((past + t, 2 * MLA_WIDTH), BF16)],
        compiler_params=_params("arbitrary"),
        name="mla_attn_sample",
    )(q, past_lat, past_kr, tab_past, k_new, v_new, w["w_ukv"], w["gkn"], w["gkr"])


def _gla_kernel(q_ref, k_ref, v_ref, la_ref, r_ref, s0_ref, g_ref, spread_ref, o_ref, sn_ref,
                st_ref, kp_ref, bp_ref, p_ref, on_ref, *, gpt):
    t_idx = pl.program_id(1)
    L, W, P = CHUNK, GLA_QK, LANES
    R = q_ref.shape[0]
    n_chunks = R // L
    cpg = n_chunks // gpt
    n_pairs = GLA_HEADS // 2

    lane_p = lax.broadcasted_iota(jnp.int32, (1, P), 1)
    even = lane_p < GLA_DK
    bd_mask = (lax.broadcasted_iota(jnp.int32, (2 * GLA_DV, P), 0) // GLA_DV
               == lax.broadcasted_iota(jnp.int32, (2 * GLA_DV, P), 1) // GLA_DK)

    @pl.when(t_idx == 0)
    def _():
        kp_ref[0:SUB, :] = jnp.zeros((SUB, W), F32)
        bp_ref[0:SUB, :] = jnp.zeros((SUB, W), F32)
        for gi in range(gpt):
            for pr in range(n_pairs):
                tt = s0_ref[gi, 2 * pr:2 * pr + 2].reshape(2 * GLA_DK, GLA_DV).T
                st_ref[gi, pr] = jnp.where(bd_mask, jnp.concatenate([tt, tt], axis=0), 0.0)

    q = q_ref[...]
    k = k_ref[...]
    la = la_ref[...]

    tri = (lax.broadcasted_iota(jnp.int32, (L, L), 0) >= lax.broadcasted_iota(jnp.int32, (L, L), 1)).astype(BF16)
    la_hi = la.astype(BF16)
    la2 = jnp.concatenate([la_hi, (la - la_hi.astype(F32)).astype(BF16)], axis=1)
    bs = []
    for c in range(n_chunks):
        t2 = _dot(tri, la2[c * L:(c + 1) * L, :])
        bs.append(t2[:, 0:W] + t2[:, W:2 * W])
    b = jnp.concatenate(bs, axis=0) if n_chunks > 1 else bs[0]
    b3 = b.reshape(n_chunks, L, W)

    def chunk_row(r):
        return jnp.broadcast_to(b3[:, r:r + 1, :], (n_chunks, L, W)).reshape(R, W)

    b_sub = jnp.broadcast_to(b.reshape(R // SUB, SUB, W)[:, 0:1, :], (R // SUB, SUB, W)).reshape(R, W)
    sub = (lax.broadcasted_iota(jnp.int32, (R, W), 0) % L) // SUB

    qt = q * jnp.exp(b - b_sub)
    lhs, ktm = [], []
    for i in range(1, L // SUB):
        lhs.append(jnp.where(sub == i, qt, 0.0).astype(BF16))
        kt = jnp.where(sub < i, k * jnp.exp(chunk_row(i * SUB) - b), 0.0)
        ktm.append([[jnp.where(even if e == 0 else ~even, kt[:, pr * P:(pr + 1) * P], 0.0).astype(BF16)
                     for e in range(2)] for pr in range(n_pairs)])

    offs = [jnp.zeros((1, W), F32)]
    for c in range(1, n_chunks):
        offs.append(offs[-1] + b[c * L - 1:c * L, :])
    bc = b + jnp.concatenate([jnp.broadcast_to(o, (L, W)) for o in offs], axis=0) if n_chunks > 1 else b
    kp_ref[SUB:SUB + R, :] = k
    bp_ref[SUB:SUB + R, :] = bc
    p_ref[:, 0:W] = (q * k).astype(BF16)
    for d in range(1, SUB):
        p_ref[:, d * W:(d + 1) * W] = (
            q * kp_ref[SUB - d:SUB - d + R, :] * jnp.exp(bc - bp_ref[SUB - d:SUB - d + R, :])).astype(BF16)
    cband = _dot(p_ref[...], spread_ref[...])
    same_sub = (lax.broadcasted_iota(jnp.int32, (L, W), 0) // SUB
                == (lax.broadcasted_iota(jnp.int32, (L, W), 1) % L) // SUB)

    qe = (q * jnp.exp(b)).astype(BF16)
    kd = (k * jnp.exp(chunk_row(L - 1) - b)).astype(BF16)
    zv = jnp.zeros((L, 2 * GLA_DV), BF16)

    for c in range(n_chunks):
        gi = c // cpg
        rs = slice(c * L, (c + 1) * L)
        a_band = jnp.where(same_sub, pltpu.roll(cband[rs], W - (SUB - 1), 1, stride=1, stride_axis=0), 0.0)
        dec = jnp.exp(b[c * L + L - 1:c * L + L, :])
        for pr in range(n_pairs):
            ls = slice(pr * P, (pr + 1) * P)
            lhs_c = jnp.concatenate([x[rs, ls] for x in lhs], axis=1)
            rhs_c = jnp.concatenate([jnp.concatenate([m[pr][0][rs], m[pr][1][rs]], axis=0) for m in ktm], axis=1)
            a_tot = (a_band[:, ls] + _dot_nt(lhs_c, rhs_c)).astype(BF16)
            vp = v_ref[rs, 2 * pr * GLA_DV:(2 * pr + 2) * GLA_DV]
            v_bd = jnp.concatenate([jnp.concatenate([vp[:, 0:GLA_DV], zv[:, 0:GLA_DV]], axis=1),
                                    jnp.concatenate([zv[:, 0:GLA_DV], vp[:, GLA_DV:]], axis=1)], axis=0)
            st = st_ref[gi, pr]
            o = _dot(a_tot, v_bd) + _dot_nt(qe[rs, ls], st.astype(BF16))
            on_ref[rs, 2 * pr * GLA_DV:(2 * pr + 2) * GLA_DV] = o
            d_st = jnp.where(bd_mask, _dot_tn(vp, kd[rs, ls]), 0.0)
            st_ref[gi, pr] = st * dec[:, ls] + d_st

    for hh in range(GLA_HEADS):
        hs = slice(hh * GLA_DV, (hh + 1) * GLA_DV)
        o = on_ref[:, hs]
        on = o * lax.rsqrt(jnp.mean(o * o, axis=-1, keepdims=True) + EPS) * g_ref[:, hs]
        o_ref[:, hs] = (on * r_ref[:, hs].astype(F32)).astype(BF16)

    @pl.when(t_idx == pl.num_programs(1) - 1)
    def _():
        for gi in range(gpt):
            for pr in range(n_pairs):
                st = st_ref[gi, pr]
                tt = jnp.where(even, st[0:GLA_DV], st[GLA_DV:2 * GLA_DV])
                sn_ref[gi, 2 * pr:2 * pr + 2] = tt.T.reshape(2, GLA_DK, GLA_DV)


def _band_spread():
    m = np.zeros((SUB, GLA_HEADS, GLA_DK, GLA_QK), np.float32)
    for d in range(SUB):
        for h in range(GLA_HEADS):
            m[d, h, :, h * GLA_DK + SUB - 1 - d] = 1.0
    return jnp.asarray(m.reshape(SUB * GLA_QK, GLA_QK), BF16)


def _gla(gq, gk, gv, la, gr, s0, g_out, *, groups, rows_per_group, tc):
    gpt = max(1, tc // rows_per_group)
    nt = max(1, rows_per_group // tc)

    def rows(c):
        return pl.BlockSpec((tc, c), lambda g, t: (g * nt + t, 0))

    state = pl.BlockSpec((gpt, GLA_HEADS, GLA_DK, GLA_DV), lambda g, t: (g, 0, 0, 0))
    spread = _band_spread()
    return pl.pallas_call(
        functools.partial(_gla_kernel, gpt=gpt),
        grid=(groups // gpt, nt),
        in_specs=[rows(GLA_QK), rows(GLA_QK), rows(GLA_WIDTH), rows(GLA_QK), rows(GLA_WIDTH), state,
                  _const((1, GLA_WIDTH)), _const(spread.shape)],
        out_specs=[rows(GLA_WIDTH), state],
        out_shape=[jax.ShapeDtypeStruct((groups * rows_per_group, GLA_WIDTH), BF16),
                   jax.ShapeDtypeStruct((groups, GLA_HEADS, GLA_DK, GLA_DV), F32)],
        scratch_shapes=[pltpu.VMEM((gpt, GLA_HEADS // 2, 2 * GLA_DV, LANES), F32),
                        pltpu.VMEM((SUB + tc, GLA_QK), F32), pltpu.VMEM((SUB + tc, GLA_QK), F32),
                        pltpu.VMEM((tc, SUB * GLA_QK), BF16), pltpu.VMEM((tc, GLA_WIDTH), F32)],
        compiler_params=_params("arbitrary", "arbitrary"),
        name="gla",
    )(gq, gk, gv, la, gr, s0, g_out, spread)


def _mlp_kernel(x_ref, a_ref, b_ref, g1_ref, sh2_ref, sc2_ref, g2_ref, gn_ref, wo_ref, wu_ref, wd_ref, y_ref,
                *, gpt, tf):
    tm, d = x_ref.shape

    def per_group(val, ref, scale_plus_one=False):
        m = ref[...]
        if scale_plus_one:
            m = 1.0 + m
        return (val.reshape(gpt, tm // gpt, d) * m).reshape(tm, d)

    mix = jnp.concatenate([a_ref[...], b_ref[...]], axis=1)
    x1 = x_ref[...] + per_group(_dot(mix, wo_ref[...]), g1_ref)
    xn = x1 * lax.rsqrt(jnp.mean(x1 * x1, axis=-1, keepdims=True) + EPS) * gn_ref[...]
    h2 = (per_group(xn, sc2_ref, True).reshape(gpt, tm // gpt, d) + sh2_ref[...]).reshape(tm, d).astype(BF16)
    acc = jnp.zeros((tm, d), F32)
    for j in range(wu_ref.shape[1] // tf):
        u = jnp.maximum(_dot(h2, wu_ref[:, j * tf:(j + 1) * tf]), 0.0)
        acc += _dot((u * u).astype(BF16), wd_ref[j * tf:(j + 1) * tf, :])
    y_ref[...] = x1 + per_group(acc, g2_ref)


def _mlp(x2, a_out, b_out, mod4, w, *, rows_per_group, tm):
    n, d = x2.shape
    gpt = max(1, tm // rows_per_group)
    tpg = max(1, rows_per_group // tm)

    def mod_spec(j):
        return pl.BlockSpec((gpt, None, 1, d), lambda i: ((i // tpg) if gpt == 1 else i, j, 0, 0))

    def rows(c):
        return pl.BlockSpec((tm, c), lambda i: (i, 0))

    return pl.pallas_call(
        functools.partial(_mlp_kernel, gpt=gpt, tf=1024),
        grid=(n // tm,),
        in_specs=[rows(d), rows(MLA_WIDTH), rows(GLA_WIDTH), mod_spec(2), mod_spec(3), mod_spec(4), mod_spec(5),
                  _const((1, d)), _const(w["w_out"].shape), _const(w["w_up"].shape), _const(w["w_down"].shape)],
        out_specs=rows(d),
        out_shape=jax.ShapeDtypeStruct((n, d), F32),
        compiler_params=_params("arbitrary"),
        name="out_proj_mlp",
    )(x2, a_out, b_out, mod4, mod4, mod4, mod4, w["g_norm2"], w["w_out"], w["w_up"], w["w_down"])


def _rope_table(pos):
    half = QK_ROPE // 2
    inv = jnp.power(ROPE_THETA, -jnp.arange(half, dtype=F32) / half)
    ang = pos.astype(F32)[:, None] * inv[None, :]
    c, s, z = jnp.cos(ang), jnp.sin(ang), jnp.zeros_like(ang)
    return jnp.concatenate([c, c, z, z, -s, z, z, z, z, s, z, z], axis=1)


def _pad_gain(g_rope):
    return jnp.concatenate([g_rope, jnp.zeros((LANES - QK_ROPE,), F32)]).reshape(1, LANES)


def _prep_weights(w_in, g_norm1, g_q_lora, w_uq, g_kv_lora, w_ukv, g_q_head, g_k_head,
                  w_gate_up, b_gate_up, g_gla_out, w_out, g_norm2, w_up, w_down):
    d = w_in.shape[0]
    s = np.cumsum([0, Q_LORA, KV_LORA, QK_ROPE, GLA_QK, GLA_QK, GLA_WIDTH, GLA_GATE_RANK, GLA_WIDTH])
    piece = [w_in[:, s[i]:s[i + 1]] for i in range(8)]
    pad = jnp.zeros((d, LANES - QK_ROPE - GLA_GATE_RANK), w_in.dtype)
    w_in_p = jnp.concatenate([piece[0], piece[1], piece[3], piece[4], piece[5], piece[7],
                              piece[2], piece[6], pad], axis=1).astype(BF16)
    uq = w_uq.reshape(Q_LORA, MLA_HEADS, QK_DIM)
    uq = jnp.concatenate([uq, jnp.zeros((Q_LORA, MLA_HEADS, HEAD_PAD - QK_DIM), w_uq.dtype)], axis=2)
    ukv = w_ukv.reshape(KV_LORA, MLA_HEADS, QK_NOPE + V_DIM)
    ukv = jnp.concatenate([ukv[:, :, :QK_NOPE].reshape(KV_LORA, -1), ukv[:, :, QK_NOPE:].reshape(KV_LORA, -1)], axis=1)
    w_gate = jnp.zeros((LANES, GLA_QK), w_gate_up.dtype).at[QK_ROPE:QK_ROPE + GLA_GATE_RANK].set(w_gate_up)
    return {
        "w_in": w_in_p, "g_norm1": g_norm1.reshape(1, d), "g_q_lora": g_q_lora.reshape(1, Q_LORA),
        "w_uq": uq.reshape(Q_LORA, MLA_HEADS * HEAD_PAD).astype(BF16),
        "g_kv_lora": g_kv_lora.reshape(1, KV_LORA), "w_ukv": ukv.astype(BF16),
        "gqn": g_q_head[:QK_NOPE].reshape(1, LANES), "gqr": _pad_gain(g_q_head[QK_NOPE:]),
        "gkn": g_k_head[:QK_NOPE].reshape(1, LANES), "gkr": _pad_gain(g_k_head[QK_NOPE:]),
        "w_gate": w_gate.astype(BF16), "b_gate": b_gate_up.reshape(1, GLA_QK),
        "g_gla_out": g_gla_out.reshape(1, GLA_WIDTH), "w_out": w_out.astype(BF16),
        "g_norm2": g_norm2.reshape(1, d), "w_up": w_up.astype(BF16), "w_down": w_down.astype(BF16),
    }


def _layer(x, mod, past_lat, past_kr, s0, w, *, tm):
    batch, seq, d = x.shape
    n = batch * seq
    past = 0 if past_lat is None else past_lat.shape[1]
    x2 = x.reshape(n, d)
    mod4 = mod.reshape(batch, 6, 1, d)
    tm = min(tm, n)
    tab = _rope_table(past + jnp.arange(seq))
    if seq < tm:
        tab = jnp.tile(tab, (tm // seq, 1))
    lat, kr, q, k, v, gq, gk, gv, la, gr = _projection(x2, mod4, tab, w, rows_per_group=seq, tm=tm)
    if past == 0:
        a_out = _attention_prompt(q, k, v, batch=batch, seq=seq, tq=min(512, seq))
    else:
        assert seq == CHUNK and past % CHUNK == 0
        a_out = _attention_sample(q, past_lat, past_kr, _rope_table(jnp.arange(past)), k, v, w,
                                  tkb=min(512, past))
    b_out, s_new = _gla(gq, gk, gv, la, gr, s0, w["g_gla_out"], groups=batch, rows_per_group=seq,
                        tc=tm)
    y = _mlp(x2, a_out, b_out, mod4, w, rows_per_group=seq, tm=tm)
    return (y.reshape(batch, seq, d), lat.reshape(batch, seq, KV_LORA), kr.reshape(batch, seq, QK_ROPE), s_new)


def kernel(x_prompt, x_sample, cache_mla_latent, cache_mla_krope, state_gla, c_prompt, c_sample,
           w_ada, b_ada, g_norm1, w_in, g_q_lora, w_uq, g_kv_lora, w_ukv, g_q_head, g_k_head,
           w_gate_up, b_gate_up, g_gla_out, w_out, g_norm2, w_up, w_down):
    nb = x_prompt.shape[0]
    depth = w_ada.shape[0]
    y_p, y_s = x_prompt, x_sample
    outs = [[] for _ in range(6)]
    c_all = jnp.concatenate([c_prompt, c_sample], axis=0)
    for l in range(depth):
        w = _prep_weights(w_in[l], g_norm1[l], g_q_lora[l], w_uq[l], g_kv_lora[l], w_ukv[l], g_q_head[l],
                          g_k_head[l], w_gate_up[l], b_gate_up[l], g_gla_out[l], w_out[l], g_norm2[l],
                          w_up[l], w_down[l])
        mod = _modulation(c_all, w_ada[l], b_ada[l])
        zero_state = jnp.zeros((nb, GLA_HEADS, GLA_DK, GLA_DV), x_prompt.dtype)
        y_p, lat, kr, st = _layer(y_p, mod[:nb], None, None, zero_state, w, tm=512)
        outs[0].append(lat); outs[1].append(kr); outs[2].append(st)
        y_s, lat, kr, st = _layer(y_s, mod[nb:], cache_mla_latent[l], cache_mla_krope[l], state_gla[l], w, tm=512)
        outs[3].append(lat); outs[4].append(kr); outs[5].append(st)
    return (y_p, y_s) + tuple(jnp.stack(o) for o in outs)
```

```python
import functools

import jax
import jax.numpy as jnp
import numpy as np
from jax import lax
from jax.experimental import pallas as pl
from jax.experimental.pallas import tpu as pltpu

F32 = jnp.float32
BF16 = jnp.bfloat16

CHUNK = 64
EPS = 1e-6
MLA_HEADS = 4
Q_LORA = 384
KV_LORA = 256
QK_NOPE = 128
QK_ROPE = 64
QK_DIM = QK_NOPE + QK_ROPE
V_DIM = 128
ROPE_THETA = 10000.0
GLA_HEADS = 4
GLA_DK = 64
GLA_DV = 128
GLA_GATE_RANK = 16
GLA_TAU = 16.0
GLA_QK = GLA_HEADS * GLA_DK
GLA_WIDTH = GLA_HEADS * GLA_DV
MLA_WIDTH = MLA_HEADS * V_DIM
HEAD_PAD = 256
SUB = 16
LOG2E = 1.4426950408889634

LANES = 128
VMEM_LIMIT = 56 * 1024 * 1024

C_Q = (0, 384)
C_KV = (384, 640)
C_GQ = (640, 896)
C_GK = (896, 1152)
C_GV = (1152, 1664)
C_GR = (1664, 2176)
C_KRG = (2176, 2304)
IN_COLS_P = 2304


def _dot(a, b):
    return jnp.dot(a, b, preferred_element_type=F32)


def _dot_nt(a, b):
    return lax.dot_general(a, b, (((1,), (1,)), ((), ())), preferred_element_type=F32)


def _dot_tn(a, b):
    return lax.dot_general(a, b, (((0,), (0,)), ((), ())), preferred_element_type=F32)


def _rope_tile(t, c, sa, sb):
    return t * c + pltpu.roll(t, 96, 1) * sa + pltpu.roll(t, 32, 1) * sb


def _params(*sem):
    return pltpu.CompilerParams(dimension_semantics=sem, vmem_limit_bytes=VMEM_LIMIT)


def _const(shape):
    return pl.BlockSpec(shape, lambda *_: (0,) * len(shape), pipeline_mode=pl.Buffered(1))


def _mod_kernel(c_ref, w_ref, b_ref, o_ref):
    c = c_ref[...]
    s = (c * jax.nn.sigmoid(c)).astype(BF16)
    o_ref[...] = _dot(s, w_ref[...].astype(BF16)) + b_ref[...]


def _modulation(c_all, w_ada, b_ada):
    g, d = c_all.shape
    n = w_ada.shape[1]
    tn = 1024
    return pl.pallas_call(
        _mod_kernel,
        grid=(n // tn,),
        in_specs=[pl.BlockSpec((g, d), lambda j: (0, 0)),
                  pl.BlockSpec((d, tn), lambda j: (0, j)),
                  pl.BlockSpec((1, tn), lambda j: (0, j))],
        out_specs=pl.BlockSpec((g, tn), lambda j: (0, j)),
        out_shape=jax.ShapeDtypeStruct((g, n), F32),
        compiler_params=_params("arbitrary"),
        name="adaln_mod",
    )(c_all, w_ada, b_ada.reshape(1, n))


def _proj_kernel(x_ref, sh_ref, sc_ref, g1_ref, win_ref, gql_ref, wuq_ref, gkv_ref, wukv_ref,
                 gqn_ref, gqr_ref, gkn_ref, gkr_ref, tab_ref, wg_ref, bg_ref,
                 lat_ref, kr_ref, q_ref, k_ref, v_ref, gq_ref, gk_ref, gv_ref, la_ref, gr_ref,
                 *, gpt):
    tm, d = x_ref.shape
    x = x_ref[...]
    xn = x * lax.rsqrt(jnp.mean(x * x, axis=-1, keepdims=True) + EPS) * g1_ref[...]
    h = (xn.reshape(gpt, tm // gpt, d) * (1.0 + sc_ref[...]) + sh_ref[...]).reshape(tm, d)
    hb = h.astype(BF16)

    def col(c):
        return _dot(hb, win_ref[:, c[0]:c[1]])

    tab = tab_ref[...]
    cos, sa, sb = tab[:, 0:LANES], tab[:, LANES:2 * LANES], tab[:, 2 * LANES:3 * LANES]

    cq = col(C_Q)
    cqn = cq * lax.rsqrt(jnp.mean(cq * cq, axis=-1, keepdims=True) + EPS) * gql_ref[...]
    qp = _dot(cqn.astype(BF16), wuq_ref[...])
    qscale = QK_DIM ** -0.5 * LOG2E
    for hh in range(MLA_HEADS):
        nope = qp[:, hh * HEAD_PAD:hh * HEAD_PAD + QK_NOPE]
        rt = qp[:, hh * HEAD_PAD + QK_NOPE:(hh + 1) * HEAD_PAD]
        ss = jnp.sum(nope * nope, axis=-1, keepdims=True) + jnp.sum(rt * rt, axis=-1, keepdims=True)
        inv = lax.rsqrt(ss * (1.0 / QK_DIM) + EPS)
        q_ref[:, hh * HEAD_PAD:hh * HEAD_PAD + QK_NOPE] = (nope * inv * gqn_ref[...] * qscale).astype(BF16)
        rq = _rope_tile(rt * inv * gqr_ref[...], cos, sa, sb)
        q_ref[:, hh * HEAD_PAD + QK_NOPE:(hh + 1) * HEAD_PAD] = (rq * qscale).astype(BF16)

    ckv = col(C_KV)
    lat = ckv * lax.rsqrt(jnp.mean(ckv * ckv, axis=-1, keepdims=True) + EPS) * gkv_ref[...]
    lat_ref[...] = lat
    krg = col(C_KRG)
    kr_ref[...] = krg[:, 0:QK_ROPE]
    lane = lax.broadcasted_iota(jnp.int32, (1, LANES), 1)
    krm = jnp.where(lane < QK_ROPE, krg, 0.0)
    ssr = jnp.sum(krm * krm, axis=-1, keepdims=True)
    rk = _rope_tile(krm * gkr_ref[...], cos, sa, sb)
    kv = _dot(lat.astype(BF16), wukv_ref[...])
    for hh in range(MLA_HEADS):
        kn = kv[:, hh * QK_NOPE:(hh + 1) * QK_NOPE]
        inv = lax.rsqrt((jnp.sum(kn * kn, axis=-1, keepdims=True) + ssr) * (1.0 / QK_DIM) + EPS)
        k_ref[:, hh * HEAD_PAD:hh * HEAD_PAD + QK_NOPE] = (kn * inv * gkn_ref[...]).astype(BF16)
        k_ref[:, hh * HEAD_PAD + QK_NOPE:(hh + 1) * HEAD_PAD] = (rk * inv).astype(BF16)
    v_ref[...] = kv[:, MLA_HEADS * QK_NOPE:].astype(BF16)

    gq_ref[...] = col(C_GQ) * (GLA_DK ** -0.5)
    gk_ref[...] = col(C_GK)
    gv_ref[...] = col(C_GV).astype(BF16)
    r = col(C_GR)
    gr_ref[...] = (r * jax.nn.sigmoid(r)).astype(BF16)
    z = _dot(krg.astype(BF16), wg_ref[...]) + bg_ref[...]
    la_ref[...] = (jnp.minimum(z, 0.0) - jnp.log(1.0 + jnp.exp(-jnp.abs(z)))) * (1.0 / GLA_TAU)


def _projection(x2, mod4, tab, w, *, rows_per_group, tm):
    n, d = x2.shape
    gpt = max(1, tm // rows_per_group)
    tpg = max(1, rows_per_group // tm)
    ntab = tab.shape[0] // tm

    def mod_spec(j):
        return pl.BlockSpec((gpt, None, 1, d), lambda i: ((i // tpg) if gpt == 1 else i, j, 0, 0))

    def rows(c):
        return pl.BlockSpec((tm, c), lambda i: (i, 0))

    outs = [(KV_LORA, F32), (QK_ROPE, F32), (MLA_HEADS * HEAD_PAD, BF16), (MLA_HEADS * HEAD_PAD, BF16),
            (MLA_WIDTH, BF16), (GLA_QK, F32), (GLA_QK, F32), (GLA_WIDTH, BF16), (GLA_QK, F32),
            (GLA_WIDTH, BF16)]
    return pl.pallas_call(
        functools.partial(_proj_kernel, gpt=gpt),
        grid=(n // tm,),
        in_specs=[rows(d), mod_spec(0), mod_spec(1), _const((1, d)), _const(w["w_in"].shape),
                  _const((1, Q_LORA)), _const(w["w_uq"].shape), _const((1, KV_LORA)), _const(w["w_ukv"].shape),
                  _const((1, LANES)), _const((1, LANES)), _const((1, LANES)), _const((1, LANES)),
                  pl.BlockSpec((tm, 3 * LANES), lambda i: (i % ntab, 0)),
                  _const(w["w_gate"].shape), _const((1, GLA_QK))],
        out_specs=[rows(c) for c, _ in outs],
        out_shape=[jax.ShapeDtypeStruct((n, c), t) for c, t in outs],
        compiler_params=_params("arbitrary"),
        name="in_proj",
    )(x2, mod4, mod4, w["g_norm1"], w["w_in"], w["g_q_lora"], w["w_uq"], w["g_kv_lora"], w["w_ukv"],
      w["gqn"], w["gqr"], w["gkn"], w["gkr"], tab, w["w_gate"], w["b_gate"])


def _attn_prompt_kernel(q_ref, k_ref, v_ref, o_ref, vx_ref, m_ref, acc_ref, *, tq):
    i = pl.program_id(1)
    heads = range(MLA_HEADS)

    @pl.when(i == 0)
    def _():
        for hh in heads:
            for jj in range(k_ref.shape[0] // tq):
                vh = v_ref[jj * tq:(jj + 1) * tq, hh * V_DIM:(hh + 1) * V_DIM]
                vx_ref[hh, jj, 0:V_DIM, :] = vh.astype(F32).T.astype(BF16)
                vx_ref[hh, jj, V_DIM:, :] = jnp.ones((V_DIM, tq), BF16)

    m_ref[...] = jnp.full(m_ref.shape, -jnp.inf, F32)
    acc_ref[...] = jnp.zeros(acc_ref.shape, F32)

    def block(j, masked):
        start = pl.multiple_of(j * tq, tq)
        for hh in heads:
            hs = slice(hh * HEAD_PAD, (hh + 1) * HEAD_PAD)
            st = _dot_nt(k_ref[pl.ds(start, tq), hs], q_ref[:, hs])
            if masked:
                kc = lax.broadcasted_iota(jnp.int32, (tq, tq), 0) // CHUNK
                qc = lax.broadcasted_iota(jnp.int32, (tq, tq), 1) // CHUNK
                st = jnp.where(kc <= qc, st, -jnp.inf)
            m = m_ref[hh]
            m_new = jnp.maximum(m, jnp.max(st, axis=0, keepdims=True))
            p = jnp.exp2(st - m_new).astype(BF16)
            acc_ref[hh] = jnp.exp2(m - m_new) * acc_ref[hh] + _dot(vx_ref[hh, j], p)
            m_ref[hh] = m_new

    def body(j, c):
        block(j, False)
        return c

    lax.fori_loop(0, i, body, 0)
    block(i, True)
    for hh in heads:
        acc = acc_ref[hh]
        o_ref[:, hh * V_DIM:(hh + 1) * V_DIM] = (acc[0:V_DIM] / acc[V_DIM:]).T.astype(BF16)


def _attention_prompt(q, k, v, *, batch, seq, tq):
    nq = seq // tq
    return pl.pallas_call(
        functools.partial(_attn_prompt_kernel, tq=tq),
        grid=(batch, nq),
        in_specs=[pl.BlockSpec((tq, MLA_HEADS * HEAD_PAD), lambda b, i: (b * nq + i, 0)),
                  pl.BlockSpec((seq, MLA_HEADS * HEAD_PAD), lambda b, i: (b, 0)),
                  pl.BlockSpec((seq, MLA_WIDTH), lambda b, i: (b, 0))],
        out_specs=pl.BlockSpec((tq, MLA_WIDTH), lambda b, i: (b * nq + i, 0)),
        out_shape=jax.ShapeDtypeStruct((batch * seq, MLA_WIDTH), BF16),
        scratch_shapes=[pltpu.VMEM((MLA_HEADS, nq, 2 * V_DIM, tq), BF16), pltpu.VMEM((MLA_HEADS, 1, tq), F32),
                        pltpu.VMEM((MLA_HEADS, 2 * V_DIM, tq), F32)],
        compiler_params=_params("arbitrary", "arbitrary"),
        name="mla_attn_prompt",
    )(q, k, v)


def _attn_sample_kernel(q_ref, lat_ref, kr_ref, tab_ref, kn_ref, vn_ref, wukv_ref, gkn_ref, gkr_ref,
                        o_ref, s_ref, v_ref, *, tkb):
    t = q_ref.shape[0]
    past = lat_ref.shape[0]
    pair = 2 * t

    @pl.when(pl.program_id(0) == 0)
    def _():
        v_ref[...] = jnp.ones(v_ref.shape, BF16)

    zq = jnp.zeros((t, HEAD_PAD), BF16)
    qpad = []
    for hh in range(MLA_HEADS):
        qh = q_ref[:, hh * HEAD_PAD:(hh + 1) * HEAD_PAD]
        qpad.append(jnp.concatenate([qh, zq] if hh % 2 == 0 else [zq, qh], axis=0))

    def scores(kh):
        return jnp.concatenate([_dot_nt(kh[2 * p], qpad[2 * p]) + _dot_nt(kh[2 * p + 1], qpad[2 * p + 1])
                                for p in range(MLA_HEADS // 2)], axis=1)

    for blk in range(past // tkb):
        r0 = blk * tkb
        tab = tab_ref[r0:r0 + tkb, :]
        cos, sa, sb = tab[:, 0:LANES], tab[:, LANES:2 * LANES], tab[:, 2 * LANES:3 * LANES]
        kv = _dot(lat_ref[r0:r0 + tkb, :].astype(BF16), wukv_ref[...])
        kr = kr_ref[r0:r0 + tkb, :]
        krm = jnp.concatenate([kr, jnp.zeros_like(kr)], axis=1)
        ssr = jnp.sum(kr * kr, axis=-1, keepdims=True)
        rk = _rope_tile(krm * gkr_ref[...], cos, sa, sb)
        kh = []
        for hh in range(MLA_HEADS):
            kn = kv[:, hh * QK_NOPE:(hh + 1) * QK_NOPE]
            inv = lax.rsqrt((jnp.sum(kn * kn, axis=-1, keepdims=True) + ssr) * (1.0 / QK_DIM) + EPS)
            kh.append(jnp.concatenate([(kn * inv * gkn_ref[...]).astype(BF16), (rk * inv).astype(BF16)], axis=1))
            v_ref[r0:r0 + tkb, 2 * hh * V_DIM:(2 * hh + 1) * V_DIM] = (
                kv[:, (MLA_HEADS + hh) * V_DIM:(MLA_HEADS + hh + 1) * V_DIM].astype(BF16))
        s_ref[r0:r0 + tkb, :] = scores(kh)
    s_ref[past:past + t, :] = scores([kn_ref[:, hh * HEAD_PAD:(hh + 1) * HEAD_PAD] for hh in range(MLA_HEADS)])
    for hh in range(MLA_HEADS):
        v_ref[past:past + t, 2 * hh * V_DIM:(2 * hh + 1) * V_DIM] = vn_ref[:, hh * V_DIM:(hh + 1) * V_DIM]

    s = s_ref[...]
    p = jnp.exp2(s - jnp.max(s, axis=0, keepdims=True)).astype(BF16)
    for hh in range(MLA_HEADS):
        pp = p[:, (hh // 2) * pair:(hh // 2 + 1) * pair]
        ol = _dot_tn(pp, v_ref[:, 2 * hh * V_DIM:(2 * hh + 2) * V_DIM])
        ol = ol[(hh % 2) * t:(hh % 2 + 1) * t, :]
        o_ref[:, hh * V_DIM:(hh + 1) * V_DIM] = (ol[:, 0:V_DIM] / ol[:, V_DIM:2 * V_DIM]).astype(BF16)


def _attention_sample(q, past_lat, past_kr, tab_past, k_new, v_new, w, *, tkb):
    batch, past, _ = past_lat.shape
    t = q.shape[0] // batch
    return pl.pallas_call(
        functools.partial(_attn_sample_kernel, tkb=tkb),
        grid=(batch,),
        in_specs=[pl.BlockSpec((t, MLA_HEADS * HEAD_PAD), lambda b: (b, 0)),
                  pl.BlockSpec((None, past, KV_LORA), lambda b: (b, 0, 0)),
                  pl.BlockSpec((None, past, QK_ROPE), lambda b: (b, 0, 0)),
                  _const((past, 3 * LANES)),
                  pl.BlockSpec((t, MLA_HEADS * HEAD_PAD), lambda b: (b, 0)),
                  pl.BlockSpec((t, MLA_WIDTH), lambda b: (b, 0)),
                  _const(w["w_ukv"].shape), _const((1, LANES)), _const((1, LANES))],
        out_specs=pl.BlockSpec((t, MLA_WIDTH), lambda b: (b, 0)),
        out_shape=jax.ShapeDtypeStruct((batch * t, MLA_WIDTH), BF16),
        scratch_shapes=[pltpu.VMEM((past + t, MLA_HEADS * t), F32),
                        pltpu.VMEM((past + t, 2 * MLA_WIDTH), BF16)],
        compiler_params=_params("arbitrary"),
        name="mla_attn_sample",
    )(q, past_lat, past_kr, tab_past, k_new, v_new, w["w_ukv"], w["gkn"], w["gkr"])


def _gla_kernel(q_ref, k_ref, v_ref, la_ref, r_ref, s0_ref, g_ref, spread_ref, o_ref, sn_ref,
                st_ref, kp_ref, bp_ref, p_ref, on_ref, *, gpt):
    t_idx = pl.program_id(1)
    L, W, P = CHUNK, GLA_QK, LANES
    R = q_ref.shape[0]
    n_chunks = R // L
    cpg = n_chunks // gpt
    n_pairs = GLA_HEADS // 2

    lane_p = lax.broadcasted_iota(jnp.int32, (1, P), 1)
    even = lane_p < GLA_DK
    bd_mask = (lax.broadcasted_iota(jnp.int32, (2 * GLA_DV, P), 0) // GLA_DV
               == lax.broadcasted_iota(jnp.int32, (2 * GLA_DV, P), 1) // GLA_DK)

    @pl.when(t_idx == 0)
    def _():
        kp_ref[0:SUB, :] = jnp.zeros((SUB, W), F32)
        bp_ref[0:SUB, :] = jnp.zeros((SUB, W), F32)
        for gi in range(gpt):
            for pr in range(n_pairs):
                tt = s0_ref[gi, 2 * pr:2 * pr + 2].reshape(2 * GLA_DK, GLA_DV).T
                st_ref[gi, pr] = jnp.where(bd_mask, jnp.concatenate([tt, tt], axis=0), 0.0)

    q = q_ref[...]
    k = k_ref[...]
    la = la_ref[...]

    tri = (lax.broadcasted_iota(jnp.int32, (L, L), 0) >= lax.broadcasted_iota(jnp.int32, (L, L), 1)).astype(BF16)
    la_hi = la.astype(BF16)
    la2 = jnp.concatenate([la_hi, (la - la_hi.astype(F32)).astype(BF16)], axis=1)
    bs = []
    for c in range(n_chunks):
        t2 = _dot(tri, la2[c * L:(c + 1) * L, :])
        bs.append(t2[:, 0:W] + t2[:, W:2 * W])
    b = jnp.concatenate(bs, axis=0) if n_chunks > 1 else bs[0]
    b3 = b.reshape(n_chunks, L, W)

    def chunk_row(r):
        return jnp.broadcast_to(b3[:, r:r + 1, :], (n_chunks, L, W)).reshape(R, W)

    b_sub = jnp.broadcast_to(b.reshape(R // SUB, SUB, W)[:, 0:1, :], (R // SUB, SUB, W)).reshape(R, W)
    sub = (lax.broadcasted_iota(jnp.int32, (R, W), 0) % L) // SUB

    qt = q * jnp.exp(b - b_sub)
    lhs, ktm = [], []
    for i in range(1, L // SUB):
        lhs.append(jnp.where(sub == i, qt, 0.0).astype(BF16))
        kt = jnp.where(sub < i, k * jnp.exp(chunk_row(i * SUB) - b), 0.0)
        ktm.append([[jnp.where(even if e == 0 else ~even, kt[:, pr * P:(pr + 1) * P], 0.0).astype(BF16)
                     for e in range(2)] for pr in range(n_pairs)])

    offs = [jnp.zeros((1, W), F32)]
    for c in range(1, n_chunks):
        offs.append(offs[-1] + b[c * L - 1:c * L, :])
    bc = b + jnp.concatenate([jnp.broadcast_to(o, (L, W)) for o in offs], axis=0) if n_chunks > 1 else b
    kp_ref[SUB:SUB + R, :] = k
    bp_ref[SUB:SUB + R, :] = bc
    p_ref[:, 0:W] = (q * k).astype(BF16)
    for d in range(1, SUB):
        p_ref[:, d * W:(d + 1) * W] = (
            q * kp_ref[SUB - d:SUB - d + R, :] * jnp.exp(bc - bp_ref[SUB - d:SUB - d + R, :])).astype(BF16)
    cband = _dot(p_ref[...], spread_ref[...])
    same_sub = (lax.broadcasted_iota(jnp.int32, (L, W), 0) // SUB
                == (lax.broadcasted_iota(jnp.int32, (L, W), 1) % L) // SUB)

    qe = (q * jnp.exp(b)).astype(BF16)
    kd = (k * jnp.exp(chunk_row(L - 1) - b)).astype(BF16)
    zv = jnp.zeros((L, 2 * GLA_DV), BF16)

    for c in range(n_chunks):
        gi = c // cpg
        rs = slice(c * L, (c + 1) * L)
        a_band = jnp.where(same_sub, pltpu.roll(cband[rs], W - (SUB - 1), 1, stride=1, stride_axis=0), 0.0)
        dec = jnp.exp(b[c * L + L - 1:c * L + L, :])
        for pr in range(n_pairs):
            ls = slice(pr * P, (pr + 1) * P)
            lhs_c = jnp.concatenate([x[rs, ls] for x in lhs], axis=1)
            rhs_c = jnp.concatenate([jnp.concatenate([m[pr][0][rs], m[pr][1][rs]], axis=0) for m in ktm], axis=1)
            a_tot = (a_band[:, ls] + _dot_nt(lhs_c, rhs_c)).astype(BF16)
            vp = v_ref[rs, 2 * pr * GLA_DV:(2 * pr + 2) * GLA_DV]
            v_bd = jnp.concatenate([jnp.concatenate([vp[:, 0:GLA_DV], zv[:, 0:GLA_DV]], axis=1),
                                    jnp.concatenate([zv[:, 0:GLA_DV], vp[:, GLA_DV:]], axis=1)], axis=0)
            st = st_ref[gi, pr]
            o = _dot(a_tot, v_bd) + _dot_nt(qe[rs, ls], st.astype(BF16))
            on_ref[rs, 2 * pr * GLA_DV:(2 * pr + 2) * GLA_DV] = o
            d_st = jnp.where(bd_mask, _dot_tn(vp, kd[rs, ls]), 0.0)
            st_ref[gi, pr] = st * dec[:, ls] + d_st

    for hh in range(GLA_HEADS):
        hs = slice(hh * GLA_DV, (hh + 1) * GLA_DV)
        o = on_ref[:, hs]
        on = o * lax.rsqrt(jnp.mean(o * o, axis=-1, keepdims=True) + EPS) * g_ref[:, hs]
        o_ref[:, hs] = (on * r_ref[:, hs].astype(F32)).astype(BF16)

    @pl.when(t_idx == pl.num_programs(1) - 1)
    def _():
        for gi in range(gpt):
            for pr in range(n_pairs):
                st = st_ref[gi, pr]
                tt = jnp.where(even, st[0:GLA_DV], st[GLA_DV:2 * GLA_DV])
                sn_ref[gi, 2 * pr:2 * pr + 2] = tt.T.reshape(2, GLA_DK, GLA_DV)


def _band_spread():
    m = np.zeros((SUB, GLA_HEADS, GLA_DK, GLA_QK), np.float32)
    for d in range(SUB):
        for h in range(GLA_HEADS):
            m[d, h, :, h * GLA_DK + SUB - 1 - d] = 1.0
    return jnp.asarray(m.reshape(SUB * GLA_QK, GLA_QK), BF16)


def _gla(gq, gk, gv, la, gr, s0, g_out, *, groups, rows_per_group, tc):
    gpt = max(1, tc // rows_per_group)
    nt = max(1, rows_per_group // tc)

    def rows(c):
        return pl.BlockSpec((tc, c), lambda g, t: (g * nt + t, 0))

    state = pl.BlockSpec((gpt, GLA_HEADS, GLA_DK, GLA_DV), lambda g, t: (g, 0, 0, 0))
    spread = _band_spread()
    return pl.pallas_call(
        functools.partial(_gla_kernel, gpt=gpt),
        grid=(groups // gpt, nt),
        in_specs=[rows(GLA_QK), rows(GLA_QK), rows(GLA_WIDTH), rows(GLA_QK), rows(GLA_WIDTH), state,
                  _const((1, GLA_WIDTH)), _const(spread.shape)],
        out_specs=[rows(GLA_WIDTH), state],
        out_shape=[jax.ShapeDtypeStruct((groups * rows_per_group, GLA_WIDTH), BF16),
                   jax.ShapeDtypeStruct((groups, GLA_HEADS, GLA_DK, GLA_DV), F32)],
        scratch_shapes=[pltpu.VMEM((gpt, GLA_HEADS // 2, 2 * GLA_DV, LANES), F32),
                        pltpu.VMEM((SUB + tc, GLA_QK), F32), pltpu.VMEM((SUB + tc, GLA_QK), F32),
                        pltpu.VMEM((tc, SUB * GLA_QK), BF16), pltpu.VMEM((tc, GLA_WIDTH), F32)],
        compiler_params=_params("arbitrary", "arbitrary"),
        name="gla",
    )(gq, gk, gv, la, gr, s0, g_out, spread)


def _mlp_kernel(x_ref, a_ref, b_ref, g1_ref, sh2_ref, sc2_ref, g2_ref, gn_ref, wo_ref, wu_ref, wd_ref, y_ref,
                *, gpt, tf):
    tm, d = x_ref.shape

    def per_group(val, ref, scale_plus_one=False):
        m = ref[...]
        if scale_plus_one:
            m = 1.0 + m
        return (val.reshape(gpt, tm // gpt, d) * m).reshape(tm, d)

    mix = jnp.concatenate([a_ref[...], b_ref[...]], axis=1)
    x1 = x_ref[...] + per_group(_dot(mix, wo_ref[...]), g1_ref)
    xn = x1 * lax.rsqrt(jnp.mean(x1 * x1, axis=-1, keepdims=True) + EPS) * gn_ref[...]
    h2 = (per_group(xn, sc2_ref, True).reshape(gpt, tm // gpt, d) + sh2_ref[...]).reshape(tm, d).astype(BF16)
    acc = jnp.zeros((tm, d), F32)
    for j in range(wu_ref.shape[1] // tf):
        u = jnp.maximum(_dot(h2, wu_ref[:, j * tf:(j + 1) * tf]), 0.0)
        acc += _dot((u * u).astype(BF16), wd_ref[j * tf:(j + 1) * tf, :])
    y_ref[...] = x1 + per_group(acc, g2_ref)


def _mlp(x2, a_out, b_out, mod4, w, *, rows_per_group, tm):
    n, d = x2.shape
    gpt = max(1, tm // rows_per_group)
    tpg = max(1, rows_per_group // tm)

    def mod_spec(j):
        return pl.BlockSpec((gpt, None, 1, d), lambda i: ((i // tpg) if gpt == 1 else i, j, 0, 0))

    def rows(c):
        return pl.BlockSpec((tm, c), lambda i: (i, 0))

    return pl.pallas_call(
        functools.partial(_mlp_kernel, gpt=gpt, tf=1024),
        grid=(n // tm,),
        in_specs=[rows(d), rows(MLA_WIDTH), rows(GLA_WIDTH), mod_spec(2), mod_spec(3), mod_spec(4), mod_spec(5),
                  _const((1, d)), _const(w["w_out"].shape), _const(w["w_up"].shape), _const(w["w_down"].shape)],
        out_specs=rows(d),
        out_shape=jax.ShapeDtypeStruct((n, d), F32),
        compiler_params=_params("arbitrary"),
        name="out_proj_mlp",
    )(x2, a_out, b_out, mod4, mod4, mod4, mod4, w["g_norm2"], w["w_out"], w["w_up"], w["w_down"])


def _rope_table(start, count, repeat=1):
    half = QK_ROPE // 2
    inv = ROPE_THETA ** (-np.arange(half, dtype=np.float64) / half)
    ang = (start + np.arange(count, dtype=np.float64))[:, None] * inv[None, :]
    c, s, z = np.cos(ang), np.sin(ang), np.zeros_like(ang)
    tab = np.concatenate([c, c, z, z, -s, z, z, z, z, s, z, z], axis=1).astype(np.float32)
    return jnp.asarray(np.tile(tab, (repeat, 1)))


def _pad_gain(g_rope):
    return jnp.concatenate([g_rope, jnp.zeros((LANES - QK_ROPE,), F32)]).reshape(1, LANES)


def _prep_weights(w_in, g_norm1, g_q_lora, w_uq, g_kv_lora, w_ukv, g_q_head, g_k_head,
                  w_gate_up, b_gate_up, g_gla_out, w_out, g_norm2, w_up, w_down):
    d = w_in.shape[0]
    s = np.cumsum([0, Q_LORA, KV_LORA, QK_ROPE, GLA_QK, GLA_QK, GLA_WIDTH, GLA_GATE_RANK, GLA_WIDTH])
    piece = [w_in[:, s[i]:s[i + 1]] for i in range(8)]
    pad = jnp.zeros((d, LANES - QK_ROPE - GLA_GATE_RANK), w_in.dtype)
    w_in_p = jnp.concatenate([piece[0], piece[1], piece[3], piece[4], piece[5], piece[7],
                              piece[2], piece[6], pad], axis=1).astype(BF16)
    uq = w_uq.reshape(Q_LORA, MLA_HEADS, QK_DIM)
    uq = jnp.concatenate([uq, jnp.zeros((Q_LORA, MLA_HEADS, HEAD_PAD - QK_DIM), w_uq.dtype)], axis=2)
    ukv = w_ukv.reshape(KV_LORA, MLA_HEADS, QK_NOPE + V_DIM)
    ukv = jnp.concatenate([ukv[:, :, :QK_NOPE].reshape(KV_LORA, -1), ukv[:, :, QK_NOPE:].reshape(KV_LORA, -1)], axis=1)
    w_gate = jnp.zeros((LANES, GLA_QK), w_gate_up.dtype).at[QK_ROPE:QK_ROPE + GLA_GATE_RANK].set(w_gate_up)
    return {
        "w_in": w_in_p, "g_norm1": g_norm1.reshape(1, d), "g_q_lora": g_q_lora.reshape(1, Q_LORA),
        "w_uq": uq.reshape(Q_LORA, MLA_HEADS * HEAD_PAD).astype(BF16),
        "g_kv_lora": g_kv_lora.reshape(1, KV_LORA), "w_ukv": ukv.astype(BF16),
        "gqn": g_q_head[:QK_NOPE].reshape(1, LANES), "gqr": _pad_gain(g_q_head[QK_NOPE:]),
        "gkn": g_k_head[:QK_NOPE].reshape(1, LANES), "gkr": _pad_gain(g_k_head[QK_NOPE:]),
        "w_gate": w_gate.astype(BF16), "b_gate": b_gate_up.reshape(1, GLA_QK),
        "g_gla_out": g_gla_out.reshape(1, GLA_WIDTH), "w_out": w_out.astype(BF16),
        "g_norm2": g_norm2.reshape(1, d), "w_up": w_up.astype(BF16), "w_down": w_down.astype(BF16),
    }


def _layer(x, mod, past_lat, past_kr, s0, w, *, tm):
    batch, seq, d = x.shape
    n = batch * seq
    past = 0 if past_lat is None else past_lat.shape[1]
    x2 = x.reshape(n, d)
    mod4 = mod.reshape(batch, 6, 1, d)
    tm = min(tm, n)
    tab = _rope_table(past, seq, repeat=max(1, tm // seq))
    lat, kr, q, k, v, gq, gk, gv, la, gr = _projection(x2, mod4, tab, w, rows_per_group=seq, tm=tm)
    if past == 0:
        a_out = _attention_prompt(q, k, v, batch=batch, seq=seq, tq=min(512, seq))
    else:
        assert seq == CHUNK and past % CHUNK == 0
        a_out = _attention_sample(q, past_lat, past_kr, _rope_table(0, past), k, v, w,
                                  tkb=min(512, past))
    b_out, s_new = _gla(gq, gk, gv, la, gr, s0, w["g_gla_out"], groups=batch, rows_per_group=seq,
                        tc=tm)
    y = _mlp(x2, a_out, b_out, mod4, w, rows_per_group=seq, tm=tm)
    return (y.reshape(batch, seq, d), lat.reshape(batch, seq, KV_LORA), kr.reshape(batch, seq, QK_ROPE), s_new)


def kernel(x_prompt, x_sample, cache_mla_latent, cache_mla_krope, state_gla, c_prompt, c_sample,
           w_ada, b_ada, g_norm1, w_in, g_q_lora, w_uq, g_kv_lora, w_ukv, g_q_head, g_k_head,
           w_gate_up, b_gate_up, g_gla_out, w_out, g_norm2, w_up, w_down):
    nb = x_prompt.shape[0]
    depth = w_ada.shape[0]
    y_p, y_s = x_prompt, x_sample
    outs = [[] for _ in range(6)]
    c_all = jnp.concatenate([c_prompt, c_sample], axis=0)
    for l in range(depth):
        w = _prep_weights(w_in[l], g_norm1[l], g_q_lora[l], w_uq[l], g_kv_lora[l], w_ukv[l], g_q_head[l],
                          g_k_head[l], w_gate_up[l], b_gate_up[l], g_gla_out[l], w_out[l], g_norm2[l],
                          w_up[l], w_down[l])
        mod = _modulation(c_all, w_ada[l], b_ada[l])
        zero_state = jnp.zeros((nb, GLA_HEADS, GLA_DK, GLA_DV), x_prompt.dtype)
        y_p, lat, kr, st = _layer(y_p, mod[:nb], None, None, zero_state, w, tm=512)
        outs[0].append(lat); outs[1].append(kr); outs[2].append(st)
        y_s, lat, kr, st = _layer(y_s, mod[nb:], cache_mla_latent[l], cache_mla_krope[l], state_gla[l], w, tm=512)
        outs[3].append(lat); outs[4].append(kr); outs[5].append(st)
    return (y_p, y_s) + tuple(jnp.stack(o) for o in outs)
```

```python
import functools

import jax
import jax.numpy as jnp
import numpy as np
from jax import lax
from jax.experimental import pallas as pl
from jax.experimental.pallas import tpu as pltpu

F32 = jnp.float32
BF16 = jnp.bfloat16

CHUNK = 64
EPS = 1e-6
MLA_HEADS = 4
Q_LORA = 384
KV_LORA = 256
QK_NOPE = 128
QK_ROPE = 64
QK_DIM = QK_NOPE + QK_ROPE
V_DIM = 128
ROPE_THETA = 10000.0
GLA_HEADS = 4
GLA_DK = 64
GLA_DV = 128
GLA_GATE_RANK = 16
GLA_TAU = 16.0
GLA_QK = GLA_HEADS * GLA_DK
GLA_WIDTH = GLA_HEADS * GLA_DV
MLA_WIDTH = MLA_HEADS * V_DIM
HEAD_PAD = 256
SUB = 16
LOG2E = 1.4426950408889634

LANES = 128
VMEM_LIMIT = 56 * 1024 * 1024

C_Q = (0, 384)
C_KV = (384, 640)
C_GQ = (640, 896)
C_GK = (896, 1152)
C_GV = (1152, 1664)
C_GR = (1664, 2176)
C_KRG = (2176, 2304)
IN_COLS_P = 2304


def _dot(a, b):
    return jnp.dot(a, b, preferred_element_type=F32)


def _dot_nt(a, b):
    return lax.dot_general(a, b, (((1,), (1,)), ((), ())), preferred_element_type=F32)


def _dot_tn(a, b):
    return lax.dot_general(a, b, (((0,), (0,)), ((), ())), preferred_element_type=F32)


def _rope_tile(t, c, sa, sb):
    return t * c + pltpu.roll(t, 96, 1) * sa + pltpu.roll(t, 32, 1) * sb


def _params(*sem):
    return pltpu.CompilerParams(dimension_semantics=sem, vmem_limit_bytes=VMEM_LIMIT)


def _const(shape):
    return pl.BlockSpec(shape, lambda *_: (0,) * len(shape), pipeline_mode=pl.Buffered(1))


def _mod_kernel(c_ref, w_ref, b_ref, o_ref):
    c = c_ref[...]
    s = (c * jax.nn.sigmoid(c)).astype(BF16)
    o_ref[...] = _dot(s, w_ref[...].astype(BF16)) + b_ref[...]


def _modulation(c_all, w_ada, b_ada):
    g, d = c_all.shape
    n = w_ada.shape[1]
    tn = 1024
    return pl.pallas_call(
        _mod_kernel,
        grid=(n // tn,),
        in_specs=[pl.BlockSpec((g, d), lambda j: (0, 0)),
                  pl.BlockSpec((d, tn), lambda j: (0, j)),
                  pl.BlockSpec((1, tn), lambda j: (0, j))],
        out_specs=pl.BlockSpec((g, tn), lambda j: (0, j)),
        out_shape=jax.ShapeDtypeStruct((g, n), F32),
        compiler_params=_params("arbitrary"),
        name="adaln_mod",
    )(c_all, w_ada, b_ada.reshape(1, n))


def _proj_kernel(x_ref, sh_ref, sc_ref, g1_ref, win_ref, gql_ref, wuq_ref, gkv_ref, wukv_ref,
                 gqn_ref, gqr_ref, gkn_ref, gkr_ref, tab_ref, wg_ref, bg_ref, *outs, gpt, prompt):
    if prompt:
        lat_ref, krt_ref, q_ref, k_ref, v_ref, gq_ref, gk_ref, gv_ref, la_ref, gr_ref = outs
    else:
        lat_ref, kr_ref, krt_ref, q_ref, gq_ref, gk_ref, gv_ref, la_ref, gr_ref = outs
    tm, d = x_ref.shape
    x = x_ref[...]
    xn = x * lax.rsqrt(jnp.mean(x * x, axis=-1, keepdims=True) + EPS) * g1_ref[...]
    h = (xn.reshape(gpt, tm // gpt, d) * (1.0 + sc_ref[...]) + sh_ref[...]).reshape(tm, d)
    hb = h.astype(BF16)

    def col(c):
        return _dot(hb, win_ref[:, c[0]:c[1]])

    tab = tab_ref[...]
    cos, sa, sb = tab[:, 0:LANES], tab[:, LANES:2 * LANES], tab[:, 2 * LANES:3 * LANES]

    cq = col(C_Q)
    cqn = cq * lax.rsqrt(jnp.mean(cq * cq, axis=-1, keepdims=True) + EPS) * gql_ref[...]
    qp = _dot(cqn.astype(BF16), wuq_ref[...])
    qscale = QK_DIM ** -0.5 * LOG2E
    for hh in range(MLA_HEADS):
        nope = qp[:, hh * HEAD_PAD:hh * HEAD_PAD + QK_NOPE]
        rt = qp[:, hh * HEAD_PAD + QK_NOPE:(hh + 1) * HEAD_PAD]
        ss = jnp.sum(nope * nope, axis=-1, keepdims=True) + jnp.sum(rt * rt, axis=-1, keepdims=True)
        inv = lax.rsqrt(ss * (1.0 / QK_DIM) + EPS)
        q_ref[:, hh * HEAD_PAD:hh * HEAD_PAD + QK_NOPE] = (nope * inv * gqn_ref[...] * qscale).astype(BF16)
        rq = _rope_tile(rt * inv * gqr_ref[...], cos, sa, sb)
        q_ref[:, hh * HEAD_PAD + QK_NOPE:(hh + 1) * HEAD_PAD] = (rq * qscale).astype(BF16)

    ckv = col(C_KV)
    lat = ckv * lax.rsqrt(jnp.mean(ckv * ckv, axis=-1, keepdims=True) + EPS) * gkv_ref[...]
    lat_ref[...] = lat
    krg = col(C_KRG)
    krt_ref[...] = krg.T[0:QK_ROPE, :]
    if prompt:
        lane = lax.broadcasted_iota(jnp.int32, (1, LANES), 1)
        krm = jnp.where(lane < QK_ROPE, krg, 0.0)
        ssr = jnp.sum(krm * krm, axis=-1, keepdims=True)
        rk = _rope_tile(krm * gkr_ref[...], cos, sa, sb)
        kv = _dot(lat.astype(BF16), wukv_ref[...])
        for hh in range(MLA_HEADS):
            kn = kv[:, hh * QK_NOPE:(hh + 1) * QK_NOPE]
            inv = lax.rsqrt((jnp.sum(kn * kn, axis=-1, keepdims=True) + ssr) * (1.0 / QK_DIM) + EPS)
            k_ref[:, hh * HEAD_PAD:hh * HEAD_PAD + QK_NOPE] = (kn * inv * gkn_ref[...]).astype(BF16)
            k_ref[:, hh * HEAD_PAD + QK_NOPE:(hh + 1) * HEAD_PAD] = (rk * inv).astype(BF16)
        v_ref[...] = kv[:, MLA_HEADS * QK_NOPE:].astype(BF16)
    else:
        kr_ref[...] = krg[:, 0:QK_ROPE]

    gq_ref[...] = col(C_GQ) * (GLA_DK ** -0.5)
    gk_ref[...] = col(C_GK)
    gv_ref[...] = col(C_GV).astype(BF16)
    r = col(C_GR)
    gr_ref[...] = (r * jax.nn.sigmoid(r)).astype(BF16)
    z = _dot(krg.astype(BF16), wg_ref[...]) + bg_ref[...]
    la_ref[...] = (jnp.minimum(z, 0.0) - jnp.log(1.0 + jnp.exp(-jnp.abs(z)))) * (1.0 / GLA_TAU)


def _projection(x2, mod4, tab, w, *, rows_per_group, tm, prompt):
    n, d = x2.shape
    gpt = max(1, tm // rows_per_group)
    tpg = max(1, rows_per_group // tm)
    ntab = tab.shape[0] // tm

    def mod_spec(j):
        return pl.BlockSpec((gpt, None, 1, d), lambda i: ((i // tpg) if gpt == 1 else i, j, 0, 0))

    def rows(c):
        return pl.BlockSpec((tm, c), lambda i: (i, 0))

    def out(c, t):
        return rows(c), jax.ShapeDtypeStruct((n, c), t)

    gla_outs = [out(GLA_QK, F32), out(GLA_QK, F32), out(GLA_WIDTH, BF16), out(GLA_QK, F32), out(GLA_WIDTH, BF16)]
    qo = out(MLA_HEADS * HEAD_PAD, BF16)
    if prompt:
        krt = (pl.BlockSpec((None, QK_ROPE, tm), lambda i: (i // tpg, 0, i % tpg)),
               jax.ShapeDtypeStruct((n // rows_per_group, QK_ROPE, rows_per_group), F32))
        outs = [out(KV_LORA, F32), krt, qo, out(MLA_HEADS * HEAD_PAD, BF16), out(MLA_WIDTH, BF16)] + gla_outs
    else:
        krt = (pl.BlockSpec((QK_ROPE, tm), lambda i: (0, i)), jax.ShapeDtypeStruct((QK_ROPE, n), F32))
        outs = [out(KV_LORA, F32), out(QK_ROPE, F32), krt, qo] + gla_outs
    return pl.pallas_call(
        functools.partial(_proj_kernel, gpt=gpt, prompt=prompt),
        grid=(n // tm,),
        in_specs=[rows(d), mod_spec(0), mod_spec(1), _const((1, d)), _const(w["w_in"].shape),
                  _const((1, Q_LORA)), _const(w["w_uq"].shape), _const((1, KV_LORA)), _const(w["w_ukv"].shape),
                  _const((1, LANES)), _const((1, LANES)), _const((1, LANES)), _const((1, LANES)),
                  pl.BlockSpec((tm, 3 * LANES), lambda i: (i % ntab, 0)),
                  _const(w["w_gate"].shape), _const((1, GLA_QK))],
        out_specs=[o[0] for o in outs],
        out_shape=[o[1] for o in outs],
        compiler_params=_params("arbitrary"),
        name="in_proj",
    )(x2, mod4, mod4, w["g_norm1"], w["w_in"], w["g_q_lora"], w["w_uq"], w["g_kv_lora"], w["w_ukv"],
      w["gqn"], w["gqr"], w["gkn"], w["gkr"], tab, w["w_gate"], w["b_gate"])


def _attn_prompt_kernel(q_ref, k_ref, v_ref, o_ref, vx_ref, m_ref, acc_ref, *, tq):
    i = pl.program_id(1)
    heads = range(MLA_HEADS)

    @pl.when(i == 0)
    def _():
        for hh in heads:
            for jj in range(k_ref.shape[0] // tq):
                vh = v_ref[jj * tq:(jj + 1) * tq, hh * V_DIM:(hh + 1) * V_DIM]
                vx_ref[hh, jj, 0:V_DIM, :] = vh.astype(F32).T.astype(BF16)
                vx_ref[hh, jj, V_DIM:, :] = jnp.ones((V_DIM, tq), BF16)

    m_ref[...] = jnp.full(m_ref.shape, -jnp.inf, F32)
    acc_ref[...] = jnp.zeros(acc_ref.shape, F32)

    def block(j, masked):
        start = pl.multiple_of(j * tq, tq)
        for hh in heads:
            hs = slice(hh * HEAD_PAD, (hh + 1) * HEAD_PAD)
            st = _dot_nt(k_ref[pl.ds(start, tq), hs], q_ref[:, hs])
            if masked:
                kc = lax.broadcasted_iota(jnp.int32, (tq, tq), 0) // CHUNK
                qc = lax.broadcasted_iota(jnp.int32, (tq, tq), 1) // CHUNK
                st = jnp.where(kc <= qc, st, -jnp.inf)
            m = m_ref[hh]
            m_new = jnp.maximum(m, jnp.max(st, axis=0, keepdims=True))
            p = jnp.exp2(st - m_new).astype(BF16)
            acc_ref[hh] = jnp.exp2(m - m_new) * acc_ref[hh] + _dot(vx_ref[hh, j], p)
            m_ref[hh] = m_new

    def body(j, c):
        block(j, False)
        return c

    lax.fori_loop(0, i, body, 0)
    block(i, True)
    for hh in heads:
        acc = acc_ref[hh]
        o_ref[:, hh * V_DIM:(hh + 1) * V_DIM] = (acc[0:V_DIM] / acc[V_DIM:]).T.astype(BF16)


def _attention_prompt(q, k, v, *, batch, seq, tq):
    nq = seq // tq
    return pl.pallas_call(
        functools.partial(_attn_prompt_kernel, tq=tq),
        grid=(batch, nq),
        in_specs=[pl.BlockSpec((tq, MLA_HEADS * HEAD_PAD), lambda b, i: (b * nq + i, 0)),
                  pl.BlockSpec((seq, MLA_HEADS * HEAD_PAD), lambda b, i: (b, 0)),
                  pl.BlockSpec((seq, MLA_WIDTH), lambda b, i: (b, 0))],
        out_specs=pl.BlockSpec((tq, MLA_WIDTH), lambda b, i: (b * nq + i, 0)),
        out_shape=jax.ShapeDtypeStruct((batch * seq, MLA_WIDTH), BF16),
        scratch_shapes=[pltpu.VMEM((MLA_HEADS, nq, 2 * V_DIM, tq), BF16), pltpu.VMEM((MLA_HEADS, 1, tq), F32),
                        pltpu.VMEM((MLA_HEADS, 2 * V_DIM, tq), F32)],
        compiler_params=_params("arbitrary", "arbitrary"),
        name="mla_attn_prompt",
    )(q, k, v)


def _attn_sample_kernel(q_ref, lat_ref, krt_ref, latn_ref, krtn_ref, tab_ref, wukt_ref, wuv_ref, gkn_ref, gkr_ref,
                        o_ref, wq_ref, latb_ref, s_ref, *, tkb):
    b = pl.program_id(0)
    t = q_ref.shape[0]
    past = lat_ref.shape[0]
    nk = MLA_HEADS * QK_NOPE
    half = QK_ROPE // 2

    @pl.when(b == 0)
    def _():
        wq_ref[0:nk, :] = wukt_ref[...]

    qr = []
    for hh in range(MLA_HEADS):
        qn = (q_ref[:, hh * HEAD_PAD:hh * HEAD_PAD + QK_NOPE].astype(F32) * gkn_ref[...]).astype(BF16)
        wq_ref[nk + hh * t:nk + (hh + 1) * t, :] = _dot(qn, wukt_ref[hh * QK_NOPE:(hh + 1) * QK_NOPE, :]).astype(BF16)
        qr.append(q_ref[:, hh * HEAD_PAD + QK_NOPE:(hh + 1) * HEAD_PAD])
    qr = jnp.concatenate(qr, axis=0)

    def key_block(lat, krt, tab, col0, width, valid):
        latb = lat.astype(BF16)
        latb_ref[col0:col0 + width, :] = latb
        g = _dot_nt(wq_ref[...], latb)
        ssr = jnp.sum(krt * krt, axis=0, keepdims=True)
        kg = krt * gkr_ref[...]
        x1, x2 = kg[0:half], kg[half:QK_ROPE]
        c, sn = tab[0:half], tab[half:QK_ROPE]
        rope = jnp.concatenate([x1 * c - x2 * sn, x2 * c + x1 * sn, jnp.zeros((LANES - QK_ROPE, width), F32)], axis=0)
        srope = _dot(qr, rope.astype(BF16))
        rows = []
        for hh in range(MLA_HEADS):
            kn = g[hh * QK_NOPE:(hh + 1) * QK_NOPE]
            inv = lax.rsqrt((jnp.sum(kn * kn, axis=0, keepdims=True) + ssr) * (1.0 / QK_DIM) + EPS)
            rows.append((g[nk + hh * t:nk + (hh + 1) * t] + srope[hh * t:(hh + 1) * t]) * inv)
        sc = jnp.concatenate(rows, axis=0)
        if valid is not None:
            sc = jnp.where(valid, sc, -jnp.inf)
        s_ref[:, col0:col0 + width] = sc

    for blk in range(past // tkb):
        c0 = blk * tkb
        key_block(lat_ref[c0:c0 + tkb, :], krt_ref[:, c0:c0 + tkb], tab_ref[:, c0:c0 + tkb], c0, tkb, None)
    mine = lax.broadcasted_iota(jnp.int32, (1, 2 * t), 1) // t == b % 2
    key_block(latn_ref[...], krtn_ref[...], tab_ref[:, past:past + 2 * t], past, 2 * t, mine)

    s = s_ref[...]
    p = jnp.exp2(s - jnp.max(s, axis=1, keepdims=True))
    l = jnp.sum(p, axis=1, keepdims=True)
    pl_ = _dot(p.astype(BF16), latb_ref[...]).astype(BF16)
    for hh in range(MLA_HEADS):
        o = _dot(pl_[hh * t:(hh + 1) * t], wuv_ref[:, hh * V_DIM:(hh + 1) * V_DIM])
        o_ref[:, hh * V_DIM:(hh + 1) * V_DIM] = (o / l[hh * t:(hh + 1) * t]).astype(BF16)


def _attention_sample(q, past_lat, past_krt, lat_new, krt_new, w, *, tkb):
    batch, past, _ = past_lat.shape
    t = q.shape[0] // batch
    half = QK_ROPE // 2
    inv = ROPE_THETA ** (-np.arange(half, dtype=np.float64) / half)
    pos = np.concatenate([np.arange(past), past + np.arange(t), past + np.arange(t)]).astype(np.float64)
    ang = inv[:, None] * pos[None, :]
    tab = jnp.asarray(np.concatenate([np.cos(ang), np.sin(ang)], axis=0).astype(np.float32))
    s_pad = past + 2 * t
    return pl.pallas_call(
        functools.partial(_attn_sample_kernel, tkb=tkb),
        grid=(batch,),
        in_specs=[pl.BlockSpec((t, MLA_HEADS * HEAD_PAD), lambda b: (b, 0)),
                  pl.BlockSpec((None, past, KV_LORA), lambda b: (b, 0, 0)),
                  pl.BlockSpec((None, QK_ROPE, past), lambda b: (b, 0, 0)),
                  pl.BlockSpec((2 * t, KV_LORA), lambda b: (b // 2, 0)),
                  pl.BlockSpec((QK_ROPE, 2 * t), lambda b: (0, b // 2)),
                  _const(tab.shape), _const(w["w_ukt"].shape), _const(w["w_uv"].shape),
                  _const((1, LANES)), _const((QK_ROPE, 1))],
        out_specs=pl.BlockSpec((t, MLA_WIDTH), lambda b: (b, 0)),
        out_shape=jax.ShapeDtypeStruct((batch * t, MLA_WIDTH), BF16),
        scratch_shapes=[pltpu.VMEM((MLA_HEADS * (QK_NOPE + t), KV_LORA), BF16),
                        pltpu.VMEM((s_pad, KV_LORA), BF16),
                        pltpu.VMEM((MLA_HEADS * t, s_pad), F32)],
        compiler_params=_params("arbitrary"),
        name="mla_attn_sample",
    )(q, past_lat, past_krt, lat_new, krt_new, tab, w["w_ukt"], w["w_uv"], w["gkn"], w["gkr_col"])


def _gla_kernel(q_ref, k_ref, v_ref, la_ref, r_ref, s0_ref, g_ref, spread_ref, o_ref, sn_ref,
                st_ref, kp_ref, bp_ref, p_ref, on_ref, *, gpt):
    t_idx = pl.program_id(1)
    L, W, P = CHUNK, GLA_QK, LANES
    R = q_ref.shape[0]
    n_chunks = R // L
    cpg = n_chunks // gpt
    n_pairs = GLA_HEADS // 2

    lane_p = lax.broadcasted_iota(jnp.int32, (1, P), 1)
    even = lane_p < GLA_DK
    bd_mask = (lax.broadcasted_iota(jnp.int32, (2 * GLA_DV, P), 0) // GLA_DV
               == lax.broadcasted_iota(jnp.int32, (2 * GLA_DV, P), 1) // GLA_DK)

    @pl.when(t_idx == 0)
    def _():
        kp_ref[0:SUB, :] = jnp.zeros((SUB, W), F32)
        bp_ref[0:SUB, :] = jnp.zeros((SUB, W), F32)
        for gi in range(gpt):
            for pr in range(n_pairs):
                tt = s0_ref[gi, 2 * pr:2 * pr + 2].reshape(2 * GLA_DK, GLA_DV).T
                st_ref[gi, pr] = jnp.where(bd_mask, jnp.concatenate([tt, tt], axis=0), 0.0)

    q = q_ref[...]
    k = k_ref[...]
    la = la_ref[...]

    tri = (lax.broadcasted_iota(jnp.int32, (L, L), 0) >= lax.broadcasted_iota(jnp.int32, (L, L), 1)).astype(BF16)
    la_hi = la.astype(BF16)
    la2 = jnp.concatenate([la_hi, (la - la_hi.astype(F32)).astype(BF16)], axis=1)
    bs = []
    for c in range(n_chunks):
        t2 = _dot(tri, la2[c * L:(c + 1) * L, :])
        bs.append(t2[:, 0:W] + t2[:, W:2 * W])
    b = jnp.concatenate(bs, axis=0) if n_chunks > 1 else bs[0]
    b3 = b.reshape(n_chunks, L, W)

    def chunk_row(r):
        return jnp.broadcast_to(b3[:, r:r + 1, :], (n_chunks, L, W)).reshape(R, W)

    b_sub = jnp.broadcast_to(b.reshape(R // SUB, SUB, W)[:, 0:1, :], (R // SUB, SUB, W)).reshape(R, W)
    sub = (lax.broadcasted_iota(jnp.int32, (R, W), 0) % L) // SUB

    qt = q * jnp.exp(b - b_sub)
    lhs, ktm = [], []
    for i in range(1, L // SUB):
        lhs.append(jnp.where(sub == i, qt, 0.0).astype(BF16))
        kt = jnp.where(sub < i, k * jnp.exp(chunk_row(i * SUB) - b), 0.0)
        ktm.append([[jnp.where(even if e == 0 else ~even, kt[:, pr * P:(pr + 1) * P], 0.0).astype(BF16)
                     for e in range(2)] for pr in range(n_pairs)])

    offs = [jnp.zeros((1, W), F32)]
    for c in range(1, n_chunks):
        offs.append(offs[-1] + b[c * L - 1:c * L, :])
    bc = b + jnp.concatenate([jnp.broadcast_to(o, (L, W)) for o in offs], axis=0) if n_chunks > 1 else b
    kp_ref[SUB:SUB + R, :] = k
    bp_ref[SUB:SUB + R, :] = bc
    p_ref[:, 0:W] = (q * k).astype(BF16)
    for d in range(1, SUB):
        p_ref[:, d * W:(d + 1) * W] = (
            q * kp_ref[SUB - d:SUB - d + R, :] * jnp.exp(bc - bp_ref[SUB - d:SUB - d + R, :])).astype(BF16)
    cband = _dot(p_ref[...], spread_ref[...])
    same_sub = (lax.broadcasted_iota(jnp.int32, (L, W), 0) // SUB
                == (lax.broadcasted_iota(jnp.int32, (L, W), 1) % L) // SUB)

    qe = (q * jnp.exp(b)).astype(BF16)
    kd = (k * jnp.exp(chunk_row(L - 1) - b)).astype(BF16)
    zv = jnp.zeros((L, 2 * GLA_DV), BF16)

    for c in range(n_chunks):
        gi = c // cpg
        rs = slice(c * L, (c + 1) * L)
        a_band = jnp.where(same_sub, pltpu.roll(cband[rs], W - (SUB - 1), 1, stride=1, stride_axis=0), 0.0)
        dec = jnp.exp(b[c * L + L - 1:c * L + L, :])
        for pr in range(n_pairs):
            ls = slice(pr * P, (pr + 1) * P)
            lhs_c = jnp.concatenate([x[rs, ls] for x in lhs], axis=1)
            rhs_c = jnp.concatenate([jnp.concatenate([m[pr][0][rs], m[pr][1][rs]], axis=0) for m in ktm], axis=1)
            a_tot = (a_band[:, ls] + _dot_nt(lhs_c, rhs_c)).astype(BF16)
            vp = v_ref[rs, 2 * pr * GLA_DV:(2 * pr + 2) * GLA_DV]
            v_bd = jnp.concatenate([jnp.concatenate([vp[:, 0:GLA_DV], zv[:, 0:GLA_DV]], axis=1),
                                    jnp.concatenate([zv[:, 0:GLA_DV], vp[:, GLA_DV:]], axis=1)], axis=0)
            st = st_ref[gi, pr]
            o = _dot(a_tot, v_bd) + _dot_nt(qe[rs, ls], st.astype(BF16))
            on_ref[rs, 2 * pr * GLA_DV:(2 * pr + 2) * GLA_DV] = o
            d_st = jnp.where(bd_mask, _dot_tn(vp, kd[rs, ls]), 0.0)
            st_ref[gi, pr] = st * dec[:, ls] + d_st

    for hh in range(GLA_HEADS):
        hs = slice(hh * GLA_DV, (hh + 1) * GLA_DV)
        o = on_ref[:, hs]
        on = o * lax.rsqrt(jnp.mean(o * o, axis=-1, keepdims=True) + EPS) * g_ref[:, hs]
        o_ref[:, hs] = (on * r_ref[:, hs].astype(F32)).astype(BF16)

    @pl.when(t_idx == pl.num_programs(1) - 1)
    def _():
        for gi in range(gpt):
            for pr in range(n_pairs):
                st = st_ref[gi, pr]
                tt = jnp.where(even, st[0:GLA_DV], st[GLA_DV:2 * GLA_DV])
                sn_ref[gi, 2 * pr:2 * pr + 2] = tt.T.reshape(2, GLA_DK, GLA_DV)


def _band_spread():
    m = np.zeros((SUB, GLA_HEADS, GLA_DK, GLA_QK), np.float32)
    for d in range(SUB):
        for h in range(GLA_HEADS):
            m[d, h, :, h * GLA_DK + SUB - 1 - d] = 1.0
    return jnp.asarray(m.reshape(SUB * GLA_QK, GLA_QK), BF16)


def _gla(gq, gk, gv, la, gr, s0, g_out, *, groups, rows_per_group, tc):
    gpt = max(1, tc // rows_per_group)
    nt = max(1, rows_per_group // tc)

    def rows(c):
        return pl.BlockSpec((tc, c), lambda g, t: (g * nt + t, 0))

    state = pl.BlockSpec((gpt, GLA_HEADS, GLA_DK, GLA_DV), lambda g, t: (g, 0, 0, 0))
    spread = _band_spread()
    return pl.pallas_call(
        functools.partial(_gla_kernel, gpt=gpt),
        grid=(groups // gpt, nt),
        in_specs=[rows(GLA_QK), rows(GLA_QK), rows(GLA_WIDTH), rows(GLA_QK), rows(GLA_WIDTH), state,
                  _const((1, GLA_WIDTH)), _const(spread.shape)],
        out_specs=[rows(GLA_WIDTH), state],
        out_shape=[jax.ShapeDtypeStruct((groups * rows_per_group, GLA_WIDTH), BF16),
                   jax.ShapeDtypeStruct((groups, GLA_HEADS, GLA_DK, GLA_DV), F32)],
        scratch_shapes=[pltpu.VMEM((gpt, GLA_HEADS // 2, 2 * GLA_DV, LANES), F32),
                        pltpu.VMEM((SUB + tc, GLA_QK), F32), pltpu.VMEM((SUB + tc, GLA_QK), F32),
                        pltpu.VMEM((tc, SUB * GLA_QK), BF16), pltpu.VMEM((tc, GLA_WIDTH), F32)],
        compiler_params=_params("arbitrary", "arbitrary"),
        name="gla",
    )(gq, gk, gv, la, gr, s0, g_out, spread)


def _mlp_kernel(x_ref, a_ref, b_ref, g1_ref, sh2_ref, sc2_ref, g2_ref, gn_ref, wo_ref, wu_ref, wd_ref, y_ref,
                *, gpt, tf):
    tm, d = x_ref.shape

    def per_group(val, ref, scale_plus_one=False):
        m = ref[...]
        if scale_plus_one:
            m = 1.0 + m
        return (val.reshape(gpt, tm // gpt, d) * m).reshape(tm, d)

    mix = jnp.concatenate([a_ref[...], b_ref[...]], axis=1)
    x1 = x_ref[...] + per_group(_dot(mix, wo_ref[...]), g1_ref)
    xn = x1 * lax.rsqrt(jnp.mean(x1 * x1, axis=-1, keepdims=True) + EPS) * gn_ref[...]
    h2 = (per_group(xn, sc2_ref, True).reshape(gpt, tm // gpt, d) + sh2_ref[...]).reshape(tm, d).astype(BF16)
    acc = jnp.zeros((tm, d), F32)
    for j in range(wu_ref.shape[1] // tf):
        u = jnp.maximum(_dot(h2, wu_ref[:, j * tf:(j + 1) * tf]), 0.0)
        acc += _dot((u * u).astype(BF16), wd_ref[j * tf:(j + 1) * tf, :])
    y_ref[...] = x1 + per_group(acc, g2_ref)


def _mlp(x2, a_out, b_out, mod4, w, *, rows_per_group, tm):
    n, d = x2.shape
    gpt = max(1, tm // rows_per_group)
    tpg = max(1, rows_per_group // tm)

    def mod_spec(j):
        return pl.BlockSpec((gpt, None, 1, d), lambda i: ((i // tpg) if gpt == 1 else i, j, 0, 0))

    def rows(c):
        return pl.BlockSpec((tm, c), lambda i: (i, 0))

    return pl.pallas_call(
        functools.partial(_mlp_kernel, gpt=gpt, tf=1024),
        grid=(n // tm,),
        in_specs=[rows(d), rows(MLA_WIDTH), rows(GLA_WIDTH), mod_spec(2), mod_spec(3), mod_spec(4), mod_spec(5),
                  _const((1, d)), _const(w["w_out"].shape), _const(w["w_up"].shape), _const(w["w_down"].shape)],
        out_specs=rows(d),
        out_shape=jax.ShapeDtypeStruct((n, d), F32),
        compiler_params=_params("arbitrary"),
        name="out_proj_mlp",
    )(x2, a_out, b_out, mod4, mod4, mod4, mod4, w["g_norm2"], w["w_out"], w["w_up"], w["w_down"])


def _rope_table(start, count, repeat=1):
    half = QK_ROPE // 2
    inv = ROPE_THETA ** (-np.arange(half, dtype=np.float64) / half)
    ang = (start + np.arange(count, dtype=np.float64))[:, None] * inv[None, :]
    c, s, z = np.cos(ang), np.sin(ang), np.zeros_like(ang)
    tab = np.concatenate([c, c, z, z, -s, z, z, z, z, s, z, z], axis=1).astype(np.float32)
    return jnp.asarray(np.tile(tab, (repeat, 1)))


def _pad_gain(g_rope):
    return jnp.concatenate([g_rope, jnp.zeros((LANES - QK_ROPE,), F32)]).reshape(1, LANES)


def _prep_weights(w_in, g_norm1, g_q_lora, w_uq, g_kv_lora, w_ukv, g_q_head, g_k_head,
                  w_gate_up, b_gate_up, g_gla_out, w_out, g_norm2, w_up, w_down):
    d = w_in.shape[0]
    s = np.cumsum([0, Q_LORA, KV_LORA, QK_ROPE, GLA_QK, GLA_QK, GLA_WIDTH, GLA_GATE_RANK, GLA_WIDTH])
    piece = [w_in[:, s[i]:s[i + 1]] for i in range(8)]
    pad = jnp.zeros((d, LANES - QK_ROPE - GLA_GATE_RANK), w_in.dtype)
    w_in_p = jnp.concatenate([piece[0], piece[1], piece[3], piece[4], piece[5], piece[7],
                              piece[2], piece[6], pad], axis=1).astype(BF16)
    uq = w_uq.reshape(Q_LORA, MLA_HEADS, QK_DIM)
    uq = jnp.concatenate([uq, jnp.zeros((Q_LORA, MLA_HEADS, HEAD_PAD - QK_DIM), w_uq.dtype)], axis=2)
    ukv = w_ukv.reshape(KV_LORA, MLA_HEADS, QK_NOPE + V_DIM)
    ukv = jnp.concatenate([ukv[:, :, :QK_NOPE].reshape(KV_LORA, -1), ukv[:, :, QK_NOPE:].reshape(KV_LORA, -1)], axis=1)
    w_gate = jnp.zeros((LANES, GLA_QK), w_gate_up.dtype).at[QK_ROPE:QK_ROPE + GLA_GATE_RANK].set(w_gate_up)
    return {
        "w_in": w_in_p, "g_norm1": g_norm1.reshape(1, d), "g_q_lora": g_q_lora.reshape(1, Q_LORA),
        "w_uq": uq.reshape(Q_LORA, MLA_HEADS * HEAD_PAD).astype(BF16),
        "g_kv_lora": g_kv_lora.reshape(1, KV_LORA), "w_ukv": ukv.astype(BF16),
        "w_ukt": ukv[:, :MLA_HEADS * QK_NOPE].T.astype(BF16), "w_uv": ukv[:, MLA_HEADS * QK_NOPE:].astype(BF16),
        "gkr_col": g_k_head[QK_NOPE:].reshape(QK_ROPE, 1),
        "gqn": g_q_head[:QK_NOPE].reshape(1, LANES), "gqr": _pad_gain(g_q_head[QK_NOPE:]),
        "gkn": g_k_head[:QK_NOPE].reshape(1, LANES), "gkr": _pad_gain(g_k_head[QK_NOPE:]),
        "w_gate": w_gate.astype(BF16), "b_gate": b_gate_up.reshape(1, GLA_QK),
        "g_gla_out": g_gla_out.reshape(1, GLA_WIDTH), "w_out": w_out.astype(BF16),
        "g_norm2": g_norm2.reshape(1, d), "w_up": w_up.astype(BF16), "w_down": w_down.astype(BF16),
    }


def _layer(x, mod, past_lat, past_kr, s0, w, *, tm):
    batch, seq, d = x.shape
    n = batch * seq
    past = 0 if past_lat is None else past_lat.shape[1]
    x2 = x.reshape(n, d)
    mod4 = mod.reshape(batch, 6, 1, d)
    tm = min(tm, n)
    tab = _rope_table(past, seq, repeat=max(1, tm // seq))
    if past == 0:
        lat, krt, q, k, v, gq, gk, gv, la, gr = _projection(x2, mod4, tab, w, rows_per_group=seq, tm=tm, prompt=True)
        kr = jnp.swapaxes(krt, 1, 2)
        a_out = _attention_prompt(q, k, v, batch=batch, seq=seq, tq=min(512, seq))
    else:
        assert seq == CHUNK and past % CHUNK == 0
        lat, kr, krt, q, gq, gk, gv, la, gr = _projection(x2, mod4, tab, w, rows_per_group=seq, tm=tm, prompt=False)
        a_out = _attention_sample(q, past_lat, jnp.swapaxes(past_kr, 1, 2), lat, krt, w, tkb=min(512, past))
    b_out, s_new = _gla(gq, gk, gv, la, gr, s0, w["g_gla_out"], groups=batch, rows_per_group=seq,
                        tc=tm)
    y = _mlp(x2, a_out, b_out, mod4, w, rows_per_group=seq, tm=tm)
    return (y.reshape(batch, seq, d), lat.reshape(batch, seq, KV_LORA), kr.reshape(batch, seq, QK_ROPE), s_new)


def kernel(x_prompt, x_sample, cache_mla_latent, cache_mla_krope, state_gla, c_prompt, c_sample,
           w_ada, b_ada, g_norm1, w_in, g_q_lora, w_uq, g_kv_lora, w_ukv, g_q_head, g_k_head,
           w_gate_up, b_gate_up, g_gla_out, w_out, g_norm2, w_up, w_down):
    nb = x_prompt.shape[0]
    depth = w_ada.shape[0]
    y_p, y_s = x_prompt, x_sample
    outs = [[] for _ in range(6)]
    c_all = jnp.concatenate([c_prompt, c_sample], axis=0)
    for l in range(depth):
        w = _prep_weights(w_in[l], g_norm1[l], g_q_lora[l], w_uq[l], g_kv_lora[l], w_ukv[l], g_q_head[l],
                          g_k_head[l], w_gate_up[l], b_gate_up[l], g_gla_out[l], w_out[l], g_norm2[l],
                          w_up[l], w_down[l])
        mod = _modulation(c_all, w_ada[l], b_ada[l])
        zero_state = jnp.zeros((nb, GLA_HEADS, GLA_DK, GLA_DV), x_prompt.dtype)
        y_p, lat, kr, st = _layer(y_p, mod[:nb], None, None, zero_state, w, tm=512)
        outs[0].append(lat); outs[1].append(kr); outs[2].append(st)
        y_s, lat, kr, st = _layer(y_s, mod[nb:], cache_mla_latent[l], cache_mla_krope[l], state_gla[l], w, tm=512)
        outs[3].append(lat); outs[4].append(kr); outs[5].append(st)
    return (y_p, y_s) + tuple(jnp.stack(o) for o in outs)
```

```python
import functools

import jax
import jax.numpy as jnp
import numpy as np
from jax import lax
from jax.experimental import pallas as pl
from jax.experimental.pallas import tpu as pltpu

F32 = jnp.float32
BF16 = jnp.bfloat16

CHUNK = 64
EPS = 1e-6
MLA_HEADS = 4
Q_LORA = 384
KV_LORA = 256
QK_NOPE = 128
QK_ROPE = 64
QK_DIM = QK_NOPE + QK_ROPE
V_DIM = 128
ROPE_THETA = 10000.0
GLA_HEADS = 4
GLA_DK = 64
GLA_DV = 128
GLA_GATE_RANK = 16
GLA_TAU = 16.0
GLA_QK = GLA_HEADS * GLA_DK
GLA_WIDTH = GLA_HEADS * GLA_DV
MLA_WIDTH = MLA_HEADS * V_DIM
HEAD_PAD = 256
SUB = 16
LOG2E = 1.4426950408889634
MAX_FIXED_SHIFT = 48.0

LANES = 128
VMEM_LIMIT = 56 * 1024 * 1024

C_Q = (0, 384)
C_KV = (384, 640)
C_GQ = (640, 896)
C_GK = (896, 1152)
C_GV = (1152, 1664)
C_GR = (1664, 2176)
C_KRG = (2176, 2304)
IN_COLS_P = 2304


def _dot(a, b):
    return jnp.dot(a, b, preferred_element_type=F32)


def _dot_nt(a, b):
    return lax.dot_general(a, b, (((1,), (1,)), ((), ())), preferred_element_type=F32)


def _dot_tn(a, b):
    return lax.dot_general(a, b, (((0,), (0,)), ((), ())), preferred_element_type=F32)


def _rope_tile(t, c, sa, sb):
    return t * c + pltpu.roll(t, 96, 1) * sa + pltpu.roll(t, 32, 1) * sb


def _params(*sem):
    return pltpu.CompilerParams(dimension_semantics=sem, vmem_limit_bytes=VMEM_LIMIT)


def _const(shape):
    return pl.BlockSpec(shape, lambda *_: (0,) * len(shape), pipeline_mode=pl.Buffered(1))


def _mod_kernel(c_ref, w_ref, b_ref, o_ref):
    c = c_ref[...]
    s = (c * jax.nn.sigmoid(c)).astype(BF16)
    o_ref[...] = _dot(s, w_ref[...].astype(BF16)) + b_ref[...]


def _modulation(c_all, w_ada, b_ada):
    g, d = c_all.shape
    n = w_ada.shape[1]
    tn = 1024
    return pl.pallas_call(
        _mod_kernel,
        grid=(n // tn,),
        in_specs=[pl.BlockSpec((g, d), lambda j: (0, 0)),
                  pl.BlockSpec((d, tn), lambda j: (0, j)),
                  pl.BlockSpec((1, tn), lambda j: (0, j))],
        out_specs=pl.BlockSpec((g, tn), lambda j: (0, j)),
        out_shape=jax.ShapeDtypeStruct((g, n), F32),
        compiler_params=_params("arbitrary"),
        name="adaln_mod",
    )(c_all, w_ada, b_ada.reshape(1, n))


def _proj_kernel(x_ref, sh_ref, sc_ref, g1_ref, win_ref, gql_ref, wuq_ref, gkv_ref, wukv_ref,
                 gqn_ref, gqr_ref, gkn_ref, gkr_ref, qone_ref, kbias_ref, tab_ref, wg_ref, bg_ref,
                 *outs, gpt, prompt):
    if prompt:
        lat_ref, krt_ref, q_ref, k_ref, v_ref, gq_ref, gk_ref, gv_ref, la_ref, gr_ref = outs
    else:
        lat_ref, kr_ref, krt_ref, q_ref, gq_ref, gk_ref, gv_ref, la_ref, gr_ref = outs
    tm, d = x_ref.shape
    x = x_ref[...]
    xn = x * lax.rsqrt(jnp.mean(x * x, axis=-1, keepdims=True) + EPS) * g1_ref[...]
    h = (xn.reshape(gpt, tm // gpt, d) * (1.0 + sc_ref[...]) + sh_ref[...]).reshape(tm, d)
    hb = h.astype(BF16)

    def col(c):
        return _dot(hb, win_ref[:, c[0]:c[1]])

    tab = tab_ref[...]
    cos, sa, sb = tab[:, 0:LANES], tab[:, LANES:2 * LANES], tab[:, 2 * LANES:3 * LANES]

    cq = col(C_Q)
    cqn = cq * lax.rsqrt(jnp.mean(cq * cq, axis=-1, keepdims=True) + EPS) * gql_ref[...]
    qp = _dot(cqn.astype(BF16), wuq_ref[...])
    qscale = QK_DIM ** -0.5 * LOG2E
    for hh in range(MLA_HEADS):
        nope = qp[:, hh * HEAD_PAD:hh * HEAD_PAD + QK_NOPE]
        rt = qp[:, hh * HEAD_PAD + QK_NOPE:(hh + 1) * HEAD_PAD]
        ss = jnp.sum(nope * nope, axis=-1, keepdims=True) + jnp.sum(rt * rt, axis=-1, keepdims=True)
        inv = lax.rsqrt(ss * (1.0 / QK_DIM) + EPS)
        q_ref[:, hh * HEAD_PAD:hh * HEAD_PAD + QK_NOPE] = (nope * inv * gqn_ref[...] * qscale).astype(BF16)
        rq = _rope_tile(rt * inv * gqr_ref[...], cos, sa, sb)
        q_ref[:, hh * HEAD_PAD + QK_NOPE:(hh + 1) * HEAD_PAD] = (rq * qscale + qone_ref[...]).astype(BF16)

    ckv = col(C_KV)
    lat = ckv * lax.rsqrt(jnp.mean(ckv * ckv, axis=-1, keepdims=True) + EPS) * gkv_ref[...]
    lat_ref[...] = lat
    krg = col(C_KRG)
    krt_ref[...] = krg.T[0:QK_ROPE, :]
    if prompt:
        lane = lax.broadcasted_iota(jnp.int32, (1, LANES), 1)
        krm = jnp.where(lane < QK_ROPE, krg, 0.0)
        ssr = jnp.sum(krm * krm, axis=-1, keepdims=True)
        rk = _rope_tile(krm * gkr_ref[...], cos, sa, sb)
        kv = _dot(lat.astype(BF16), wukv_ref[...])
        for hh in range(MLA_HEADS):
            kn = kv[:, hh * QK_NOPE:(hh + 1) * QK_NOPE]
            inv = lax.rsqrt((jnp.sum(kn * kn, axis=-1, keepdims=True) + ssr) * (1.0 / QK_DIM) + EPS)
            k_ref[:, hh * HEAD_PAD:hh * HEAD_PAD + QK_NOPE] = (kn * inv * gkn_ref[...]).astype(BF16)
            k_ref[:, hh * HEAD_PAD + QK_NOPE:(hh + 1) * HEAD_PAD] = (rk * inv + kbias_ref[...]).astype(BF16)
        v_ref[...] = kv[:, MLA_HEADS * QK_NOPE:].astype(BF16)
    else:
        kr_ref[...] = krg[:, 0:QK_ROPE]

    gq_ref[...] = col(C_GQ) * (GLA_DK ** -0.5)
    gk_ref[...] = col(C_GK)
    gv_ref[...] = col(C_GV).astype(BF16)
    r = col(C_GR)
    gr_ref[...] = (r * jax.nn.sigmoid(r)).astype(BF16)
    z = _dot(krg.astype(BF16), wg_ref[...]) + bg_ref[...]
    la_ref[...] = (jnp.minimum(z, 0.0) - jnp.log(1.0 + jnp.exp(-jnp.abs(z)))) * (1.0 / GLA_TAU)


def _projection(x2, mod4, tab, w, *, rows_per_group, tm, prompt):
    n, d = x2.shape
    gpt = max(1, tm // rows_per_group)
    tpg = max(1, rows_per_group // tm)
    ntab = tab.shape[0] // tm

    def mod_spec(j):
        return pl.BlockSpec((gpt, None, 1, d), lambda i: ((i // tpg) if gpt == 1 else i, j, 0, 0))

    def rows(c):
        return pl.BlockSpec((tm, c), lambda i: (i, 0))

    def out(c, t):
        return rows(c), jax.ShapeDtypeStruct((n, c), t)

    gla_outs = [out(GLA_QK, F32), out(GLA_QK, F32), out(GLA_WIDTH, BF16), out(GLA_QK, F32), out(GLA_WIDTH, BF16)]
    qo = out(MLA_HEADS * HEAD_PAD, BF16)
    if prompt:
        krt = (pl.BlockSpec((None, QK_ROPE, tm), lambda i: (i // tpg, 0, i % tpg)),
               jax.ShapeDtypeStruct((n // rows_per_group, QK_ROPE, rows_per_group), F32))
        outs = [out(KV_LORA, F32), krt, qo, out(MLA_HEADS * HEAD_PAD, BF16), out(MLA_WIDTH, BF16)] + gla_outs
    else:
        krt = (pl.BlockSpec((QK_ROPE, tm), lambda i: (0, i)), jax.ShapeDtypeStruct((QK_ROPE, n), F32))
        outs = [out(KV_LORA, F32), out(QK_ROPE, F32), krt, qo] + gla_outs
    return pl.pallas_call(
        functools.partial(_proj_kernel, gpt=gpt, prompt=prompt),
        grid=(n // tm,),
        in_specs=[rows(d), mod_spec(0), mod_spec(1), _const((1, d)), _const(w["w_in"].shape),
                  _const((1, Q_LORA)), _const(w["w_uq"].shape), _const((1, KV_LORA)), _const(w["w_ukv"].shape),
                  _const((1, LANES)), _const((1, LANES)), _const((1, LANES)), _const((1, LANES)),
                  _const((1, LANES)), _const((1, LANES)),
                  pl.BlockSpec((tm, 3 * LANES), lambda i: (i % ntab, 0)),
                  _const(w["w_gate"].shape), _const((1, GLA_QK))],
        out_specs=[o[0] for o in outs],
        out_shape=[o[1] for o in outs],
        compiler_params=_params("arbitrary"),
        name="in_proj",
    )(x2, mod4, mod4, w["g_norm1"], w["w_in"], w["g_q_lora"], w["w_uq"], w["g_kv_lora"], w["w_ukv"],
      w["gqn"], w["gqr"], w["gkn"], w["gkr"], w["qone"], w["kbias"], tab, w["w_gate"], w["b_gate"])


def _attn_prompt_kernel(fast_ref, q_ref, k_ref, v_ref, o_ref, vx_ref, m_ref, acc_ref, *, tq):
    i = pl.program_id(1)
    heads = range(MLA_HEADS)

    @pl.when(i == 0)
    def _():
        for hh in heads:
            for jj in range(k_ref.shape[0] // tq):
                vh = v_ref[jj * tq:(jj + 1) * tq, hh * V_DIM:(hh + 1) * V_DIM]
                vx_ref[hh, jj, 0:V_DIM, :] = vh.astype(F32).T.astype(BF16)
                vx_ref[hh, jj, V_DIM:, :] = jnp.ones((V_DIM, tq), BF16)

    acc_ref[...] = jnp.zeros(acc_ref.shape, F32)

    def scores(j, hh, masked):
        hs = slice(hh * HEAD_PAD, (hh + 1) * HEAD_PAD)
        st = _dot_nt(k_ref[pl.ds(pl.multiple_of(j * tq, tq), tq), hs], q_ref[:, hs])
        if masked:
            kc = lax.broadcasted_iota(jnp.int32, (tq, tq), 0) // CHUNK
            qc = lax.broadcasted_iota(jnp.int32, (tq, tq), 1) // CHUNK
            st = jnp.where(kc <= qc, st, -jnp.inf)
        return st

    def fast_block(j, masked):
        for hh in heads:
            acc_ref[hh] += _dot(vx_ref[hh, j], jnp.exp2(scores(j, hh, masked)).astype(BF16))

    def safe_block(j, masked):
        for hh in heads:
            st = scores(j, hh, masked)
            m = m_ref[hh]
            m_new = jnp.maximum(m, jnp.max(st, axis=0, keepdims=True))
            p = jnp.exp2(st - m_new).astype(BF16)
            acc_ref[hh] = jnp.exp2(m - m_new) * acc_ref[hh] + _dot(vx_ref[hh, j], p)
            m_ref[hh] = m_new

    def sweep(block):
        def body(j, c):
            block(j, False)
            return c

        lax.fori_loop(0, i, body, 0)
        block(i, True)

    @pl.when(fast_ref[0] == 1)
    def _():
        sweep(fast_block)

    @pl.when(fast_ref[0] != 1)
    def _():
        m_ref[...] = jnp.full(m_ref.shape, -jnp.inf, F32)
        sweep(safe_block)

    for hh in heads:
        acc = acc_ref[hh]
        o_ref[:, hh * V_DIM:(hh + 1) * V_DIM] = (acc[0:V_DIM] / acc[V_DIM:]).T.astype(BF16)


def _attention_prompt(fast, q, k, v, *, batch, seq, tq):
    nq = seq // tq
    return pl.pallas_call(
        functools.partial(_attn_prompt_kernel, tq=tq),
        grid=(batch, nq),
        in_specs=[pl.BlockSpec(memory_space=pltpu.SMEM),
                  pl.BlockSpec((tq, MLA_HEADS * HEAD_PAD), lambda b, i: (b * nq + i, 0)),
                  pl.BlockSpec((seq, MLA_HEADS * HEAD_PAD), lambda b, i: (b, 0)),
                  pl.BlockSpec((seq, MLA_WIDTH), lambda b, i: (b, 0))],
        out_specs=pl.BlockSpec((tq, MLA_WIDTH), lambda b, i: (b * nq + i, 0)),
        out_shape=jax.ShapeDtypeStruct((batch * seq, MLA_WIDTH), BF16),
        scratch_shapes=[pltpu.VMEM((MLA_HEADS, nq, 2 * V_DIM, tq), BF16), pltpu.VMEM((MLA_HEADS, 1, tq), F32),
                        pltpu.VMEM((MLA_HEADS, 2 * V_DIM, tq), F32)],
        compiler_params=_params("arbitrary", "arbitrary"),
        name="mla_attn_prompt",
    )(fast, q, k, v)


def _attn_sample_kernel(q_ref, lat_ref, krt_ref, latn_ref, krtn_ref, tab_ref, wukt_ref, wuv_ref, gkn_ref, gkr_ref,
                        o_ref, wq_ref, latb_ref, s_ref, *, tkb):
    b = pl.program_id(0)
    t = q_ref.shape[0]
    past = lat_ref.shape[0]
    nk = MLA_HEADS * QK_NOPE
    half = QK_ROPE // 2

    @pl.when(b == 0)
    def _():
        wq_ref[0:nk, :] = wukt_ref[...]

    qr = []
    for hh in range(MLA_HEADS):
        qn = (q_ref[:, hh * HEAD_PAD:hh * HEAD_PAD + QK_NOPE].astype(F32) * gkn_ref[...]).astype(BF16)
        wq_ref[nk + hh * t:nk + (hh + 1) * t, :] = _dot(qn, wukt_ref[hh * QK_NOPE:(hh + 1) * QK_NOPE, :]).astype(BF16)
        qr.append(q_ref[:, hh * HEAD_PAD + QK_NOPE:(hh + 1) * HEAD_PAD])
    qr = jnp.concatenate(qr, axis=0)

    def key_block(lat, krt, tab, col0, width, valid):
        latb = lat.astype(BF16)
        latb_ref[col0:col0 + width, :] = latb
        g = _dot_nt(wq_ref[...], latb)
        ssr = jnp.sum(krt * krt, axis=0, keepdims=True)
        kg = krt * gkr_ref[...]
        x1, x2 = kg[0:half], kg[half:QK_ROPE]
        c, sn = tab[0:half], tab[half:QK_ROPE]
        rope = jnp.concatenate([x1 * c - x2 * sn, x2 * c + x1 * sn, jnp.zeros((LANES - QK_ROPE, width), F32)], axis=0)
        srope = _dot(qr, rope.astype(BF16))
        rows = []
        for hh in range(MLA_HEADS):
            kn = g[hh * QK_NOPE:(hh + 1) * QK_NOPE]
            inv = lax.rsqrt((jnp.sum(kn * kn, axis=0, keepdims=True) + ssr) * (1.0 / QK_DIM) + EPS)
            rows.append((g[nk + hh * t:nk + (hh + 1) * t] + srope[hh * t:(hh + 1) * t]) * inv)
        sc = jnp.concatenate(rows, axis=0)
        if valid is not None:
            sc = jnp.where(valid, sc, -jnp.inf)
        s_ref[:, col0:col0 + width] = sc

    for blk in range(past // tkb):
        c0 = blk * tkb
        key_block(lat_ref[c0:c0 + tkb, :], krt_ref[:, c0:c0 + tkb], tab_ref[:, c0:c0 + tkb], c0, tkb, None)
    mine = lax.broadcasted_iota(jnp.int32, (1, 2 * t), 1) // t == b % 2
    key_block(latn_ref[...], krtn_ref[...], tab_ref[:, past:past + 2 * t], past, 2 * t, mine)

    s = s_ref[...]
    p = jnp.exp2(s - jnp.max(s, axis=1, keepdims=True))
    l = jnp.sum(p, axis=1, keepdims=True)
    pl_ = _dot(p.astype(BF16), latb_ref[...]).astype(BF16)
    for hh in range(MLA_HEADS):
        o = _dot(pl_[hh * t:(hh + 1) * t], wuv_ref[:, hh * V_DIM:(hh + 1) * V_DIM])
        o_ref[:, hh * V_DIM:(hh + 1) * V_DIM] = (o / l[hh * t:(hh + 1) * t]).astype(BF16)


def _attention_sample(q, past_lat, past_krt, lat_new, krt_new, w, *, tkb):
    batch, past, _ = past_lat.shape
    t = q.shape[0] // batch
    half = QK_ROPE // 2
    inv = ROPE_THETA ** (-np.arange(half, dtype=np.float64) / half)
    pos = np.concatenate([np.arange(past), past + np.arange(t), past + np.arange(t)]).astype(np.float64)
    ang = inv[:, None] * pos[None, :]
    tab = jnp.asarray(np.concatenate([np.cos(ang), np.sin(ang)], axis=0).astype(np.float32))
    s_pad = past + 2 * t
    return pl.pallas_call(
        functools.partial(_attn_sample_kernel, tkb=tkb),
        grid=(batch,),
        in_specs=[pl.BlockSpec((t, MLA_HEADS * HEAD_PAD), lambda b: (b, 0)),
                  pl.BlockSpec((None, past, KV_LORA), lambda b: (b, 0, 0)),
                  pl.BlockSpec((None, QK_ROPE, past), lambda b: (b, 0, 0)),
                  pl.BlockSpec((2 * t, KV_LORA), lambda b: (b // 2, 0)),
                  pl.BlockSpec((QK_ROPE, 2 * t), lambda b: (0, b // 2)),
                  _const(tab.shape), _const(w["w_ukt"].shape), _const(w["w_uv"].shape),
                  _const((1, LANES)), _const((QK_ROPE, 1))],
        out_specs=pl.BlockSpec((t, MLA_WIDTH), lambda b: (b, 0)),
        out_shape=jax.ShapeDtypeStruct((batch * t, MLA_WIDTH), BF16),
        scratch_shapes=[pltpu.VMEM((MLA_HEADS * (QK_NOPE + t), KV_LORA), BF16),
                        pltpu.VMEM((s_pad, KV_LORA), BF16),
                        pltpu.VMEM((MLA_HEADS * t, s_pad), F32)],
        compiler_params=_params("arbitrary"),
        name="mla_attn_sample",
    )(q, past_lat, past_krt, lat_new, krt_new, tab, w["w_ukt"], w["w_uv"], w["gkn"], w["gkr_col"])


def _gla_kernel(q_ref, k_ref, v_ref, la_ref, r_ref, s0_ref, g_ref, spread_ref, o_ref, sn_ref,
                st_ref, kp_ref, bp_ref, p_ref, on_ref, *, gpt):
    t_idx = pl.program_id(1)
    L, W, P = CHUNK, GLA_QK, LANES
    R = q_ref.shape[0]
    n_chunks = R // L
    cpg = n_chunks // gpt
    n_pairs = GLA_HEADS // 2

    lane_p = lax.broadcasted_iota(jnp.int32, (1, P), 1)
    even = lane_p < GLA_DK
    bd_mask = (lax.broadcasted_iota(jnp.int32, (2 * GLA_DV, P), 0) // GLA_DV
               == lax.broadcasted_iota(jnp.int32, (2 * GLA_DV, P), 1) // GLA_DK)

    @pl.when(t_idx == 0)
    def _():
        kp_ref[0:SUB, :] = jnp.zeros((SUB, W), F32)
        bp_ref[0:SUB, :] = jnp.zeros((SUB, W), F32)
        for gi in range(gpt):
            for pr in range(n_pairs):
                tt = s0_ref[gi, 2 * pr:2 * pr + 2].reshape(2 * GLA_DK, GLA_DV).T
                st_ref[gi, pr] = jnp.where(bd_mask, jnp.concatenate([tt, tt], axis=0), 0.0)

    q = q_ref[...]
    k = k_ref[...]
    la = la_ref[...]

    tri = (lax.broadcasted_iota(jnp.int32, (L, L), 0) >= lax.broadcasted_iota(jnp.int32, (L, L), 1)).astype(BF16)
    la_hi = la.astype(BF16)
    la2 = jnp.concatenate([la_hi, (la - la_hi.astype(F32)).astype(BF16)], axis=1)
    bs = []
    for c in range(n_chunks):
        t2 = _dot(tri, la2[c * L:(c + 1) * L, :])
        bs.append(t2[:, 0:W] + t2[:, W:2 * W])
    b = jnp.concatenate(bs, axis=0) if n_chunks > 1 else bs[0]
    b3 = b.reshape(n_chunks, L, W)

    def chunk_row(r):
        return jnp.broadcast_to(b3[:, r:r + 1, :], (n_chunks, L, W)).reshape(R, W)

    b_sub = jnp.broadcast_to(b.reshape(R // SUB, SUB, W)[:, 0:1, :], (R // SUB, SUB, W)).reshape(R, W)
    sub = (lax.broadcasted_iota(jnp.int32, (R, W), 0) % L) // SUB

    qt = q * jnp.exp(b - b_sub)
    lhs, ktm = [], []
    for i in range(1, L // SUB):
        lhs.append(jnp.where(sub == i, qt, 0.0).astype(BF16))
        kt = jnp.where(sub < i, k * jnp.exp(chunk_row(i * SUB) - b), 0.0)
        ktm.append([[jnp.where(even if e == 0 else ~even, kt[:, pr * P:(pr + 1) * P], 0.0).astype(BF16)
                     for e in range(2)] for pr in range(n_pairs)])

    offs = [jnp.zeros((1, W), F32)]
    for c in range(1, n_chunks):
        offs.append(offs[-1] + b[c * L - 1:c * L, :])
    bc = b + jnp.concatenate([jnp.broadcast_to(o, (L, W)) for o in offs], axis=0) if n_chunks > 1 else b
    kp_ref[SUB:SUB + R, :] = k
    bp_ref[SUB:SUB + R, :] = bc
    p_ref[:, 0:W] = (q * k).astype(BF16)
    for d in range(1, SUB):
        p_ref[:, d * W:(d + 1) * W] = (
            q * kp_ref[SUB - d:SUB - d + R, :] * jnp.exp(bc - bp_ref[SUB - d:SUB - d + R, :])).astype(BF16)
    cband = _dot(p_ref[...], spread_ref[...])
    same_sub = (lax.broadcasted_iota(jnp.int32, (L, W), 0) // SUB
                == (lax.broadcasted_iota(jnp.int32, (L, W), 1) % L) // SUB)

    qe = (q * jnp.exp(b)).astype(BF16)
    kd = (k * jnp.exp(chunk_row(L - 1) - b)).astype(BF16)
    zv = jnp.zeros((L, 2 * GLA_DV), BF16)

    for c in range(n_chunks):
        gi = c // cpg
        rs = slice(c * L, (c + 1) * L)
        a_band = jnp.where(same_sub, pltpu.roll(cband[rs], W - (SUB - 1), 1, stride=1, stride_axis=0), 0.0)
        dec = jnp.exp(b[c * L + L - 1:c * L + L, :])
        for pr in range(n_pairs):
            ls = slice(pr * P, (pr + 1) * P)
            lhs_c = jnp.concatenate([x[rs, ls] for x in lhs], axis=1)
            rhs_c = jnp.concatenate([jnp.concatenate([m[pr][0][rs], m[pr][1][rs]], axis=0) for m in ktm], axis=1)
            a_tot = (a_band[:, ls] + _dot_nt(lhs_c, rhs_c)).astype(BF16)
            vp = v_ref[rs, 2 * pr * GLA_DV:(2 * pr + 2) * GLA_DV]
            v_bd = jnp.concatenate([jnp.concatenate([vp[:, 0:GLA_DV], zv[:, 0:GLA_DV]], axis=1),
                                    jnp.concatenate([zv[:, 0:GLA_DV], vp[:, GLA_DV:]], axis=1)], axis=0)
            st = st_ref[gi, pr]
            o = _dot(a_tot, v_bd) + _dot_nt(qe[rs, ls], st.astype(BF16))
            on_ref[rs, 2 * pr * GLA_DV:(2 * pr + 2) * GLA_DV] = o
            d_st = jnp.where(bd_mask, _dot_tn(vp, kd[rs, ls]), 0.0)
            st_ref[gi, pr] = st * dec[:, ls] + d_st

    for hh in range(GLA_HEADS):
        hs = slice(hh * GLA_DV, (hh + 1) * GLA_DV)
        o = on_ref[:, hs]
        on = o * lax.rsqrt(jnp.mean(o * o, axis=-1, keepdims=True) + EPS) * g_ref[:, hs]
        o_ref[:, hs] = (on * r_ref[:, hs].astype(F32)).astype(BF16)

    @pl.when(t_idx == pl.num_programs(1) - 1)
    def _():
        for gi in range(gpt):
            for pr in range(n_pairs):
                st = st_ref[gi, pr]
                tt = jnp.where(even, st[0:GLA_DV], st[GLA_DV:2 * GLA_DV])
                sn_ref[gi, 2 * pr:2 * pr + 2] = tt.T.reshape(2, GLA_DK, GLA_DV)


def _band_spread():
    m = np.zeros((SUB, GLA_HEADS, GLA_DK, GLA_QK), np.float32)
    for d in range(SUB):
        for h in range(GLA_HEADS):
            m[d, h, :, h * GLA_DK + SUB - 1 - d] = 1.0
    return jnp.asarray(m.reshape(SUB * GLA_QK, GLA_QK), BF16)


def _gla(gq, gk, gv, la, gr, s0, g_out, *, groups, rows_per_group, tc):
    gpt = max(1, tc // rows_per_group)
    nt = max(1, rows_per_group // tc)

    def rows(c):
        return pl.BlockSpec((tc, c), lambda g, t: (g * nt + t, 0))

    state = pl.BlockSpec((gpt, GLA_HEADS, GLA_DK, GLA_DV), lambda g, t: (g, 0, 0, 0))
    spread = _band_spread()
    return pl.pallas_call(
        functools.partial(_gla_kernel, gpt=gpt),
        grid=(groups // gpt, nt),
        in_specs=[rows(GLA_QK), rows(GLA_QK), rows(GLA_WIDTH), rows(GLA_QK), rows(GLA_WIDTH), state,
                  _const((1, GLA_WIDTH)), _const(spread.shape)],
        out_specs=[rows(GLA_WIDTH), state],
        out_shape=[jax.ShapeDtypeStruct((groups * rows_per_group, GLA_WIDTH), BF16),
                   jax.ShapeDtypeStruct((groups, GLA_HEADS, GLA_DK, GLA_DV), F32)],
        scratch_shapes=[pltpu.VMEM((gpt, GLA_HEADS // 2, 2 * GLA_DV, LANES), F32),
                        pltpu.VMEM((SUB + tc, GLA_QK), F32), pltpu.VMEM((SUB + tc, GLA_QK), F32),
                        pltpu.VMEM((tc, SUB * GLA_QK), BF16), pltpu.VMEM((tc, GLA_WIDTH), F32)],
        compiler_params=_params("arbitrary", "arbitrary"),
        name="gla",
    )(gq, gk, gv, la, gr, s0, g_out, spread)


def _mlp_kernel(x_ref, a_ref, b_ref, g1_ref, sh2_ref, sc2_ref, g2_ref, gn_ref, wo_ref, wu_ref, wd_ref, y_ref,
                *, gpt, tf):
    tm, d = x_ref.shape

    def per_group(val, ref, scale_plus_one=False):
        m = ref[...]
        if scale_plus_one:
            m = 1.0 + m
        return (val.reshape(gpt, tm // gpt, d) * m).reshape(tm, d)

    mix = jnp.concatenate([a_ref[...], b_ref[...]], axis=1)
    x1 = x_ref[...] + per_group(_dot(mix, wo_ref[...]), g1_ref)
    xn = x1 * lax.rsqrt(jnp.mean(x1 * x1, axis=-1, keepdims=True) + EPS) * gn_ref[...]
    h2 = (per_group(xn, sc2_ref, True).reshape(gpt, tm // gpt, d) + sh2_ref[...]).reshape(tm, d).astype(BF16)
    acc = jnp.zeros((tm, d), F32)
    for j in range(wu_ref.shape[1] // tf):
        u = jnp.maximum(_dot(h2, wu_ref[:, j * tf:(j + 1) * tf]), 0.0)
        acc += _dot((u * u).astype(BF16), wd_ref[j * tf:(j + 1) * tf, :])
    y_ref[...] = x1 + per_group(acc, g2_ref)


def _mlp(x2, a_out, b_out, mod4, w, *, rows_per_group, tm):
    n, d = x2.shape
    gpt = max(1, tm // rows_per_group)
    tpg = max(1, rows_per_group // tm)

    def mod_spec(j):
        return pl.BlockSpec((gpt, None, 1, d), lambda i: ((i // tpg) if gpt == 1 else i, j, 0, 0))

    def rows(c):
        return pl.BlockSpec((tm, c), lambda i: (i, 0))

    return pl.pallas_call(
        functools.partial(_mlp_kernel, gpt=gpt, tf=1024),
        grid=(n // tm,),
        in_specs=[rows(d), rows(MLA_WIDTH), rows(GLA_WIDTH), mod_spec(2), mod_spec(3), mod_spec(4), mod_spec(5),
                  _const((1, d)), _const(w["w_out"].shape), _const(w["w_up"].shape), _const(w["w_down"].shape)],
        out_specs=rows(d),
        out_shape=jax.ShapeDtypeStruct((n, d), F32),
        compiler_params=_params("arbitrary"),
        name="out_proj_mlp",
    )(x2, a_out, b_out, mod4, mod4, mod4, mod4, w["g_norm2"], w["w_out"], w["w_up"], w["w_down"])


def _rope_table(start, count, repeat=1):
    half = QK_ROPE // 2
    inv = ROPE_THETA ** (-np.arange(half, dtype=np.float64) / half)
    ang = (start + np.arange(count, dtype=np.float64))[:, None] * inv[None, :]
    c, s, z = np.cos(ang), np.sin(ang), np.zeros_like(ang)
    tab = np.concatenate([c, c, z, z, -s, z, z, z, z, s, z, z], axis=1).astype(np.float32)
    return jnp.asarray(np.tile(tab, (repeat, 1)))


def _pad_gain(g_rope):
    return jnp.concatenate([g_rope, jnp.zeros((LANES - QK_ROPE,), F32)]).reshape(1, LANES)


def _prep_weights(w_in, g_norm1, g_q_lora, w_uq, g_kv_lora, w_ukv, g_q_head, g_k_head,
                  w_gate_up, b_gate_up, g_gla_out, w_out, g_norm2, w_up, w_down):
    d = w_in.shape[0]
    s = np.cumsum([0, Q_LORA, KV_LORA, QK_ROPE, GLA_QK, GLA_QK, GLA_WIDTH, GLA_GATE_RANK, GLA_WIDTH])
    piece = [w_in[:, s[i]:s[i + 1]] for i in range(8)]
    pad = jnp.zeros((d, LANES - QK_ROPE - GLA_GATE_RANK), w_in.dtype)
    w_in_p = jnp.concatenate([piece[0], piece[1], piece[3], piece[4], piece[5], piece[7],
                              piece[2], piece[6], pad], axis=1).astype(BF16)
    uq = w_uq.reshape(Q_LORA, MLA_HEADS, QK_DIM)
    uq = jnp.concatenate([uq, jnp.zeros((Q_LORA, MLA_HEADS, HEAD_PAD - QK_DIM), w_uq.dtype)], axis=2)
    ukv = w_ukv.reshape(KV_LORA, MLA_HEADS, QK_NOPE + V_DIM)
    ukv = jnp.concatenate([ukv[:, :, :QK_NOPE].reshape(KV_LORA, -1), ukv[:, :, QK_NOPE:].reshape(KV_LORA, -1)], axis=1)
    w_gate = jnp.zeros((LANES, GLA_QK), w_gate_up.dtype).at[QK_ROPE:QK_ROPE + GLA_GATE_RANK].set(w_gate_up)
    bound = 1.02 * QK_DIM ** 0.5 * LOG2E * jnp.max(jnp.abs(g_q_head)) * jnp.max(jnp.abs(g_k_head))
    lane = jnp.arange(LANES) == QK_ROPE
    return {
        "w_in": w_in_p, "g_norm1": g_norm1.reshape(1, d), "g_q_lora": g_q_lora.reshape(1, Q_LORA),
        "w_uq": uq.reshape(Q_LORA, MLA_HEADS * HEAD_PAD).astype(BF16),
        "g_kv_lora": g_kv_lora.reshape(1, KV_LORA), "w_ukv": ukv.astype(BF16),
        "w_ukt": ukv[:, :MLA_HEADS * QK_NOPE].T.astype(BF16), "w_uv": ukv[:, MLA_HEADS * QK_NOPE:].astype(BF16),
        "gkr_col": g_k_head[QK_NOPE:].reshape(QK_ROPE, 1),
        "qone": lane.astype(F32).reshape(1, LANES), "kbias": jnp.where(lane, -bound, 0.0).reshape(1, LANES),
        "fast_softmax": (bound <= MAX_FIXED_SHIFT).astype(jnp.int32).reshape(1),
        "gqn": g_q_head[:QK_NOPE].reshape(1, LANES), "gqr": _pad_gain(g_q_head[QK_NOPE:]),
        "gkn": g_k_head[:QK_NOPE].reshape(1, LANES), "gkr": _pad_gain(g_k_head[QK_NOPE:]),
        "w_gate": w_gate.astype(BF16), "b_gate": b_gate_up.reshape(1, GLA_QK),
        "g_gla_out": g_gla_out.reshape(1, GLA_WIDTH), "w_out": w_out.astype(BF16),
        "g_norm2": g_norm2.reshape(1, d), "w_up": w_up.astype(BF16), "w_down": w_down.astype(BF16),
    }


def _layer(x, mod, past_lat, past_kr, s0, w, *, tm):
    batch, seq, d = x.shape
    n = batch * seq
    past = 0 if past_lat is None else past_lat.shape[1]
    x2 = x.reshape(n, d)
    mod4 = mod.reshape(batch, 6, 1, d)
    tm = min(tm, n)
    tab = _rope_table(past, seq, repeat=max(1, tm // seq))
    if past == 0:
        lat, krt, q, k, v, gq, gk, gv, la, gr = _projection(x2, mod4, tab, w, rows_per_group=seq, tm=tm, prompt=True)
        kr = jnp.swapaxes(krt, 1, 2)
        a_out = _attention_prompt(w["fast_softmax"], q, k, v, batch=batch, seq=seq, tq=min(512, seq))
    else:
        assert seq == CHUNK and past % CHUNK == 0
        lat, kr, krt, q, gq, gk, gv, la, gr = _projection(x2, mod4, tab, w, rows_per_group=seq, tm=tm, prompt=False)
        a_out = _attention_sample(q, past_lat, jnp.swapaxes(past_kr, 1, 2), lat, krt, w, tkb=min(512, past))
    b_out, s_new = _gla(gq, gk, gv, la, gr, s0, w["g_gla_out"], groups=batch, rows_per_group=seq,
                        tc=tm)
    y = _mlp(x2, a_out, b_out, mod4, w, rows_per_group=seq, tm=tm)
    return (y.reshape(batch, seq, d), lat.reshape(batch, seq, KV_LORA), kr.reshape(batch, seq, QK_ROPE), s_new)


def kernel(x_prompt, x_sample, cache_mla_latent, cache_mla_krope, state_gla, c_prompt, c_sample,
           w_ada, b_ada, g_norm1, w_in, g_q_lora, w_uq, g_kv_lora, w_ukv, g_q_head, g_k_head,
           w_gate_up, b_gate_up, g_gla_out, w_out, g_norm2, w_up, w_down):
    nb = x_prompt.shape[0]
    depth = w_ada.shape[0]
    y_p, y_s = x_prompt, x_sample
    outs = [[] for _ in range(6)]
    c_all = jnp.concatenate([c_prompt, c_sample], axis=0)
    for l in range(depth):
        w = _prep_weights(w_in[l], g_norm1[l], g_q_lora[l], w_uq[l], g_kv_lora[l], w_ukv[l], g_q_head[l],
                          g_k_head[l], w_gate_up[l], b_gate_up[l], g_gla_out[l], w_out[l], g_norm2[l],
                          w_up[l], w_down[l])
        mod = _modulation(c_all, w_ada[l], b_ada[l])
        zero_state = jnp.zeros((nb, GLA_HEADS, GLA_DK, GLA_DV), x_prompt.dtype)
        y_p, lat, kr, st = _layer(y_p, mod[:nb], None, None, zero_state, w, tm=512)
        outs[0].append(lat); outs[1].append(kr); outs[2].append(st)
        y_s, lat, kr, st = _layer(y_s, mod[nb:], cache_mla_latent[l], cache_mla_krope[l], state_gla[l], w, tm=512)
        outs[3].append(lat); outs[4].append(kr); outs[5].append(st)
    return (y_p, y_s) + tuple(jnp.stack(o) for o in outs)
```

```python
import functools

import jax
import jax.numpy as jnp
import numpy as np
from jax import lax
from jax.experimental import pallas as pl
from jax.experimental.pallas import tpu as pltpu

F32 = jnp.float32
BF16 = jnp.bfloat16

CHUNK = 64
EPS = 1e-6
MLA_HEADS = 4
Q_LORA = 384
KV_LORA = 256
QK_NOPE = 128
QK_ROPE = 64
QK_DIM = QK_NOPE + QK_ROPE
V_DIM = 128
ROPE_THETA = 10000.0
GLA_HEADS = 4
GLA_DK = 64
GLA_DV = 128
GLA_GATE_RANK = 16
GLA_TAU = 16.0
GLA_QK = GLA_HEADS * GLA_DK
GLA_WIDTH = GLA_HEADS * GLA_DV
MLA_WIDTH = MLA_HEADS * V_DIM
HEAD_PAD = 256
SUB = 8
LOG2E = 1.4426950408889634
MAX_FIXED_SHIFT = 48.0

LANES = 128
VMEM_LIMIT = 56 * 1024 * 1024

C_QKR = (0, 512)
C_KV = (512, 768)
C_GQ = (768, 1024)
C_GK = (1024, 1280)
C_GV = (1280, 1792)
C_GR = (1792, 2304)


def _dot(a, b):
    return jnp.dot(a, b, preferred_element_type=F32)


def _dot_nt(a, b):
    return lax.dot_general(a, b, (((1,), (1,)), ((), ())), preferred_element_type=F32)


def _dot_tn(a, b):
    return lax.dot_general(a, b, (((0,), (0,)), ((), ())), preferred_element_type=F32)


def _rope_tile(t, c, sa, sb):
    return t * c + pltpu.roll(t, 96, 1) * sa + pltpu.roll(t, 32, 1) * sb


def _params(*sem):
    return pltpu.CompilerParams(dimension_semantics=sem, vmem_limit_bytes=VMEM_LIMIT)


def _const(shape):
    return pl.BlockSpec(shape, lambda *_: (0,) * len(shape), pipeline_mode=pl.Buffered(1))


def _mod_kernel(c_ref, w_ref, b_ref, o_ref):
    c = c_ref[...]
    s = (c * jax.nn.sigmoid(c)).astype(BF16)
    o_ref[...] = _dot(s, w_ref[...].astype(BF16)) + b_ref[...]


def _modulation(c_all, w_ada, b_ada):
    g, d = c_all.shape
    n = w_ada.shape[1]
    tn = 1024
    return pl.pallas_call(
        _mod_kernel,
        grid=(n // tn,),
        in_specs=[pl.BlockSpec((g, d), lambda j: (0, 0)),
                  pl.BlockSpec((d, tn), lambda j: (0, j)),
                  pl.BlockSpec((1, tn), lambda j: (0, j))],
        out_specs=pl.BlockSpec((g, tn), lambda j: (0, j)),
        out_shape=jax.ShapeDtypeStruct((g, n), F32),
        compiler_params=_params("arbitrary"),
        name="adaln_mod",
    )(c_all, w_ada, b_ada.reshape(1, n))


def _proj_kernel(x_ref, sh_ref, sc_ref, g1_ref, win_ref, gql_ref, wuq_ref, gkv_ref, wukv_ref,
                 gqn_ref, gqr_ref, gkn_ref, gkr_ref, qone_ref, kbias_ref, tab_ref, wg_ref, bg_ref,
                 *outs, gpt, prompt):
    if prompt:
        lat_ref, krt_ref, q_ref, k_ref, v_ref, gq_ref, gk_ref, gv_ref, la_ref, gr_ref = outs
    else:
        lat_ref, kr_ref, krt_ref, q_ref, gq_ref, gk_ref, gv_ref, la_ref, gr_ref = outs
    tm, d = x_ref.shape
    x = x_ref[...]
    xn = x * lax.rsqrt(jnp.mean(x * x, axis=-1, keepdims=True) + EPS) * g1_ref[...]
    h = (xn.reshape(gpt, tm // gpt, d) * (1.0 + sc_ref[...]) + sh_ref[...]).reshape(tm, d)
    hb = h.astype(BF16)

    def col(c):
        return _dot(hb, win_ref[:, c[0]:c[1]])

    tab = tab_ref[...]
    cos, sa, sb = tab[:, 0:LANES], tab[:, LANES:2 * LANES], tab[:, 2 * LANES:3 * LANES]

    qkr = col(C_QKR)
    cq = qkr[:, 0:Q_LORA]
    cqn = cq * lax.rsqrt(jnp.mean(cq * cq, axis=-1, keepdims=True) + EPS) * gql_ref[...]
    qp = _dot(cqn.astype(BF16), wuq_ref[...])
    qscale = QK_DIM ** -0.5 * LOG2E
    for hh in range(MLA_HEADS):
        nope = qp[:, hh * HEAD_PAD:hh * HEAD_PAD + QK_NOPE]
        rt = qp[:, hh * HEAD_PAD + QK_NOPE:(hh + 1) * HEAD_PAD]
        ss = jnp.sum(nope * nope, axis=-1, keepdims=True) + jnp.sum(rt * rt, axis=-1, keepdims=True)
        inv = lax.rsqrt(ss * (1.0 / QK_DIM) + EPS)
        q_ref[:, hh * HEAD_PAD:hh * HEAD_PAD + QK_NOPE] = (nope * inv * gqn_ref[...] * qscale).astype(BF16)
        rq = _rope_tile(rt * inv * gqr_ref[...], cos, sa, sb)
        q_ref[:, hh * HEAD_PAD + QK_NOPE:(hh + 1) * HEAD_PAD] = (rq * qscale + qone_ref[...]).astype(BF16)

    ckv = col(C_KV)
    lat = ckv * lax.rsqrt(jnp.mean(ckv * ckv, axis=-1, keepdims=True) + EPS) * gkv_ref[...]
    lat_ref[...] = lat
    krg = qkr[:, Q_LORA:]
    krt_ref[...] = krg.T[0:QK_ROPE, :]
    if prompt:
        lane = lax.broadcasted_iota(jnp.int32, (1, LANES), 1)
        krm = jnp.where(lane < QK_ROPE, krg, 0.0)
        ssr = jnp.sum(krm * krm, axis=-1, keepdims=True)
        rk = _rope_tile(krm * gkr_ref[...], cos, sa, sb)
        kv = _dot(lat.astype(BF16), wukv_ref[...])
        for hh in range(MLA_HEADS):
            kn = kv[:, hh * QK_NOPE:(hh + 1) * QK_NOPE]
            inv = lax.rsqrt((jnp.sum(kn * kn, axis=-1, keepdims=True) + ssr) * (1.0 / QK_DIM) + EPS)
            k_ref[:, hh * HEAD_PAD:hh * HEAD_PAD + QK_NOPE] = (kn * inv * gkn_ref[...]).astype(BF16)
            k_ref[:, hh * HEAD_PAD + QK_NOPE:(hh + 1) * HEAD_PAD] = (rk * inv + kbias_ref[...]).astype(BF16)
        v_ref[...] = kv[:, MLA_HEADS * QK_NOPE:].astype(BF16)
    else:
        kr_ref[...] = krg[:, 0:QK_ROPE]

    gq_ref[...] = col(C_GQ) * (GLA_DK ** -0.5)
    gk_ref[...] = col(C_GK)
    gv_ref[...] = col(C_GV).astype(BF16)
    r = col(C_GR)
    gr_ref[...] = (r * jax.nn.sigmoid(r)).astype(BF16)
    z = _dot(krg.astype(BF16), wg_ref[...]) + bg_ref[...]
    la_ref[...] = (jnp.minimum(z, 0.0) - jnp.log(1.0 + jnp.exp(-jnp.abs(z)))) * (1.0 / GLA_TAU)


def _projection(x2, mod4, tab, w, *, rows_per_group, tm, prompt):
    n, d = x2.shape
    gpt = max(1, tm // rows_per_group)
    tpg = max(1, rows_per_group // tm)
    ntab = tab.shape[0] // tm

    def mod_spec(j):
        return pl.BlockSpec((gpt, None, 1, d), lambda i: ((i // tpg) if gpt == 1 else i, j, 0, 0))

    def rows(c):
        return pl.BlockSpec((tm, c), lambda i: (i, 0))

    def out(c, t):
        return rows(c), jax.ShapeDtypeStruct((n, c), t)

    gla_outs = [out(GLA_QK, F32), out(GLA_QK, F32), out(GLA_WIDTH, BF16), out(GLA_QK, F32), out(GLA_WIDTH, BF16)]
    qo = out(MLA_HEADS * HEAD_PAD, BF16)
    if prompt:
        krt = (pl.BlockSpec((None, QK_ROPE, tm), lambda i: (i // tpg, 0, i % tpg)),
               jax.ShapeDtypeStruct((n // rows_per_group, QK_ROPE, rows_per_group), F32))
        outs = [out(KV_LORA, F32), krt, qo, out(MLA_HEADS * HEAD_PAD, BF16), out(MLA_WIDTH, BF16)] + gla_outs
    else:
        krt = (pl.BlockSpec((QK_ROPE, tm), lambda i: (0, i)), jax.ShapeDtypeStruct((QK_ROPE, n), F32))
        outs = [out(KV_LORA, F32), out(QK_ROPE, F32), krt, qo] + gla_outs
    return pl.pallas_call(
        functools.partial(_proj_kernel, gpt=gpt, prompt=prompt),
        grid=(n // tm,),
        in_specs=[rows(d), mod_spec(0), mod_spec(1), _const((1, d)), _const(w["w_in"].shape),
                  _const((1, Q_LORA)), _const(w["w_uq"].shape), _const((1, KV_LORA)), _const(w["w_ukv"].shape),
                  _const((1, LANES)), _const((1, LANES)), _const((1, LANES)), _const((1, LANES)),
                  _const((1, LANES)), _const((1, LANES)),
                  pl.BlockSpec((tm, 3 * LANES), lambda i: (i % ntab, 0)),
                  _const(w["w_gate"].shape), _const((1, GLA_QK))],
        out_specs=[o[0] for o in outs],
        out_shape=[o[1] for o in outs],
        compiler_params=_params("arbitrary"),
        name="in_proj",
    )(x2, mod4, mod4, w["g_norm1"], w["w_in"], w["g_q_lora"], w["w_uq"], w["g_kv_lora"], w["w_ukv"],
      w["gqn"], w["gqr"], w["gkn"], w["gkr"], w["qone"], w["kbias"], tab, w["w_gate"], w["b_gate"])


def _attn_prompt_kernel(fast_ref, q_ref, k_ref, v_ref, o_ref, vx_ref, m_ref, acc_ref, *, tq):
    i = pl.program_id(1)
    heads = range(MLA_HEADS)

    @pl.when(i == 0)
    def _():
        for hh in heads:
            for jj in range(k_ref.shape[0] // tq):
                vh = v_ref[jj * tq:(jj + 1) * tq, hh * V_DIM:(hh + 1) * V_DIM]
                vx_ref[hh, jj, 0:V_DIM, :] = vh.astype(F32).T.astype(BF16)
                vx_ref[hh, jj, V_DIM:, :] = jnp.ones((V_DIM, tq), BF16)

    acc_ref[...] = jnp.zeros(acc_ref.shape, F32)

    def scores(j, hh, masked):
        hs = slice(hh * HEAD_PAD, (hh + 1) * HEAD_PAD)
        st = _dot_nt(k_ref[pl.ds(pl.multiple_of(j * tq, tq), tq), hs], q_ref[:, hs])
        if masked:
            kc = lax.broadcasted_iota(jnp.int32, (tq, tq), 0) // CHUNK
            qc = lax.broadcasted_iota(jnp.int32, (tq, tq), 1) // CHUNK
            st = jnp.where(kc <= qc, st, -jnp.inf)
        return st

    def fast_block(j, masked):
        for hh in heads:
            acc_ref[hh] += _dot(vx_ref[hh, j], jnp.exp2(scores(j, hh, masked)).astype(BF16))

    def safe_block(j, masked):
        for hh in heads:
            st = scores(j, hh, masked)
            m = m_ref[hh]
            m_new = jnp.maximum(m, jnp.max(st, axis=0, keepdims=True))
            p = jnp.exp2(st - m_new).astype(BF16)
            acc_ref[hh] = jnp.exp2(m - m_new) * acc_ref[hh] + _dot(vx_ref[hh, j], p)
            m_ref[hh] = m_new

    def sweep(block):
        def body(j, c):
            block(j, False)
            return c

        lax.fori_loop(0, i, body, 0)
        block(i, True)

    @pl.when(fast_ref[0] == 1)
    def _():
        sweep(fast_block)

    @pl.when(fast_ref[0] != 1)
    def _():
        m_ref[...] = jnp.full(m_ref.shape, -jnp.inf, F32)
        sweep(safe_block)

    for hh in heads:
        acc = acc_ref[hh]
        o_ref[:, hh * V_DIM:(hh + 1) * V_DIM] = (acc[0:V_DIM] / acc[V_DIM:]).T.astype(BF16)


def _attention_prompt(fast, q, k, v, *, batch, seq, tq):
    nq = seq // tq
    return pl.pallas_call(
        functools.partial(_attn_prompt_kernel, tq=tq),
        grid=(batch, nq),
        in_specs=[pl.BlockSpec(memory_space=pltpu.SMEM),
                  pl.BlockSpec((tq, MLA_HEADS * HEAD_PAD), lambda b, i: (b * nq + i, 0)),
                  pl.BlockSpec((seq, MLA_HEADS * HEAD_PAD), lambda b, i: (b, 0)),
                  pl.BlockSpec((seq, MLA_WIDTH), lambda b, i: (b, 0))],
        out_specs=pl.BlockSpec((tq, MLA_WIDTH), lambda b, i: (b * nq + i, 0)),
        out_shape=jax.ShapeDtypeStruct((batch * seq, MLA_WIDTH), BF16),
        scratch_shapes=[pltpu.VMEM((MLA_HEADS, nq, 2 * V_DIM, tq), BF16), pltpu.VMEM((MLA_HEADS, 1, tq), F32),
                        pltpu.VMEM((MLA_HEADS, 2 * V_DIM, tq), F32)],
        compiler_params=_params("arbitrary", "arbitrary"),
        name="mla_attn_prompt",
    )(fast, q, k, v)


def _attn_sample_kernel(q_ref, lat_ref, krt_ref, latn_ref, krtn_ref, tab_ref, wukt_ref, wuv_ref, gkn_ref, gkr_ref,
                        o_ref, wq_ref, latb_ref, s_ref, *, tkb):
    b = pl.program_id(0)
    t = q_ref.shape[0]
    past = lat_ref.shape[0]
    nk = MLA_HEADS * QK_NOPE
    half = QK_ROPE // 2

    @pl.when(b == 0)
    def _():
        wq_ref[0:nk, :] = wukt_ref[...]

    qr = []
    for hh in range(MLA_HEADS):
        qn = (q_ref[:, hh * HEAD_PAD:hh * HEAD_PAD + QK_NOPE].astype(F32) * gkn_ref[...]).astype(BF16)
        wq_ref[nk + hh * t:nk + (hh + 1) * t, :] = _dot(qn, wukt_ref[hh * QK_NOPE:(hh + 1) * QK_NOPE, :]).astype(BF16)
        qr.append(q_ref[:, hh * HEAD_PAD + QK_NOPE:(hh + 1) * HEAD_PAD])
    qr = jnp.concatenate(qr, axis=0)

    def key_block(lat, krt, tab, col0, width, valid):
        latb = lat.astype(BF16)
        latb_ref[col0:col0 + width, :] = latb
        g = _dot_nt(wq_ref[...], latb)
        ssr = jnp.sum(krt * krt, axis=0, keepdims=True)
        kg = krt * gkr_ref[...]
        x1, x2 = kg[0:half], kg[half:QK_ROPE]
        c, sn = tab[0:half], tab[half:QK_ROPE]
        rope = jnp.concatenate([x1 * c - x2 * sn, x2 * c + x1 * sn, jnp.zeros((LANES - QK_ROPE, width), F32)], axis=0)
        srope = _dot(qr, rope.astype(BF16))
        rows = []
        for hh in range(MLA_HEADS):
            kn = g[hh * QK_NOPE:(hh + 1) * QK_NOPE]
            inv = lax.rsqrt((jnp.sum(kn * kn, axis=0, keepdims=True) + ssr) * (1.0 / QK_DIM) + EPS)
            rows.append((g[nk + hh * t:nk + (hh + 1) * t] + srope[hh * t:(hh + 1) * t]) * inv)
        sc = jnp.concatenate(rows, axis=0)
        if valid is not None:
            sc = jnp.where(valid, sc, -jnp.inf)
        s_ref[:, col0:col0 + width] = sc

    for blk in range(past // tkb):
        c0 = blk * tkb
        key_block(lat_ref[c0:c0 + tkb, :], krt_ref[:, c0:c0 + tkb], tab_ref[:, c0:c0 + tkb], c0, tkb, None)
    mine = lax.broadcasted_iota(jnp.int32, (1, 2 * t), 1) // t == b % 2
    key_block(latn_ref[...], krtn_ref[...], tab_ref[:, past:past + 2 * t], past, 2 * t, mine)

    s = s_ref[...]
    p = jnp.exp2(s - jnp.max(s, axis=1, keepdims=True))
    l = jnp.sum(p, axis=1, keepdims=True)
    pl_ = _dot(p.astype(BF16), latb_ref[...]).astype(BF16)
    for hh in range(MLA_HEADS):
        o = _dot(pl_[hh * t:(hh + 1) * t], wuv_ref[:, hh * V_DIM:(hh + 1) * V_DIM])
        o_ref[:, hh * V_DIM:(hh + 1) * V_DIM] = (o / l[hh * t:(hh + 1) * t]).astype(BF16)


def _attention_sample(q, past_lat, past_krt, lat_new, krt_new, w, *, tkb):
    batch, past, _ = past_lat.shape
    t = q.shape[0] // batch
    half = QK_ROPE // 2
    inv = ROPE_THETA ** (-np.arange(half, dtype=np.float64) / half)
    pos = np.concatenate([np.arange(past), past + np.arange(t), past + np.arange(t)]).astype(np.float64)
    ang = inv[:, None] * pos[None, :]
    tab = jnp.asarray(np.concatenate([np.cos(ang), np.sin(ang)], axis=0).astype(np.float32))
    s_pad = past + 2 * t
    return pl.pallas_call(
        functools.partial(_attn_sample_kernel, tkb=tkb),
        grid=(batch,),
        in_specs=[pl.BlockSpec((t, MLA_HEADS * HEAD_PAD), lambda b: (b, 0)),
                  pl.BlockSpec((None, past, KV_LORA), lambda b: (b, 0, 0)),
                  pl.BlockSpec((None, QK_ROPE, past), lambda b: (b, 0, 0)),
                  pl.BlockSpec((2 * t, KV_LORA), lambda b: (b // 2, 0)),
                  pl.BlockSpec((QK_ROPE, 2 * t), lambda b: (0, b // 2)),
                  _const(tab.shape), _const(w["w_ukt"].shape), _const(w["w_uv"].shape),
                  _const((1, LANES)), _const((QK_ROPE, 1))],
        out_specs=pl.BlockSpec((t, MLA_WIDTH), lambda b: (b, 0)),
        out_shape=jax.ShapeDtypeStruct((batch * t, MLA_WIDTH), BF16),
        scratch_shapes=[pltpu.VMEM((MLA_HEADS * (QK_NOPE + t), KV_LORA), BF16),
                        pltpu.VMEM((s_pad, KV_LORA), BF16),
                        pltpu.VMEM((MLA_HEADS * t, s_pad), F32)],
        compiler_params=_params("arbitrary"),
        name="mla_attn_sample",
    )(q, past_lat, past_krt, lat_new, krt_new, tab, w["w_ukt"], w["w_uv"], w["gkn"], w["gkr_col"])


def _gla_kernel(q_ref, k_ref, v_ref, la_ref, r_ref, s0_ref, g_ref, spread_ref, o_ref, sn_ref,
                st_ref, kp_ref, ap_ref, p_ref, on_ref, *, gpt):
    t_idx = pl.program_id(1)
    L, W, P = CHUNK, GLA_QK, LANES
    R = q_ref.shape[0]
    n_chunks = R // L
    cpg = n_chunks // gpt
    n_pairs = GLA_HEADS // 2

    lane_p = lax.broadcasted_iota(jnp.int32, (1, P), 1)
    even = lane_p < GLA_DK
    bd_mask = (lax.broadcasted_iota(jnp.int32, (2 * GLA_DV, P), 0) // GLA_DV
               == lax.broadcasted_iota(jnp.int32, (2 * GLA_DV, P), 1) // GLA_DK)

    @pl.when(t_idx == 0)
    def _():
        kp_ref[0:SUB, :] = jnp.zeros((SUB, W), F32)
        ap_ref[0:SUB, :] = jnp.zeros((SUB, W), F32)
        for gi in range(gpt):
            for pr in range(n_pairs):
                tt = s0_ref[gi, 2 * pr:2 * pr + 2].reshape(2 * GLA_DK, GLA_DV).T
                st_ref[gi, pr] = jnp.where(bd_mask, jnp.concatenate([tt, tt], axis=0), 0.0)

    q = q_ref[...]
    k = k_ref[...]
    la = la_ref[...]

    tri = (lax.broadcasted_iota(jnp.int32, (L, L), 0) >= lax.broadcasted_iota(jnp.int32, (L, L), 1)).astype(BF16)
    la_hi = la.astype(BF16)
    la2 = jnp.concatenate([la_hi, (la - la_hi.astype(F32)).astype(BF16)], axis=1)
    bs = []
    for c in range(n_chunks):
        t2 = _dot(tri, la2[c * L:(c + 1) * L, :])
        bs.append(t2[:, 0:W] + t2[:, W:2 * W])
    b = (jnp.concatenate(bs, axis=0) if n_chunks > 1 else bs[0]) * LOG2E
    b3 = b.reshape(n_chunks, L, W)

    def chunk_row(r):
        return jnp.broadcast_to(b3[:, r:r + 1, :], (n_chunks, L, W)).reshape(R, W)

    b_sub = jnp.broadcast_to(b.reshape(R // SUB, SUB, W)[:, 0:1, :], (R // SUB, SUB, W)).reshape(R, W)
    sub = (lax.broadcasted_iota(jnp.int32, (R, W), 0) % L) // SUB

    qt = q * jnp.exp2(b - b_sub)
    zb = jnp.zeros((), BF16)
    zq = jnp.zeros((SUB, P), F32)
    ktm = []
    for i in range(1, L // SUB):
        kt = (k * jnp.exp2(chunk_row(i * SUB) - b)).astype(BF16)
        ktm.append([[jnp.where((sub[:, 0:P] < i) & (even if e == 0 else ~even), kt[:, pr * P:(pr + 1) * P], zb)
                     for e in range(2)] for pr in range(n_pairs)])

    a = jnp.exp(la)
    kp_ref[SUB:SUB + R, :] = k
    ap_ref[SUB:SUB + R, :] = a
    p_ref[:, 0:W] = (q * k).astype(BF16)
    e = a
    for d in range(1, SUB):
        if d > 1:
            e = e * ap_ref[SUB - d + 1:SUB - d + 1 + R, :]
        p_ref[:, d * W:(d + 1) * W] = (q * kp_ref[SUB - d:SUB - d + R, :] * e).astype(BF16)
    cband = _dot(p_ref[...], spread_ref[...])
    same_sub = (lax.broadcasted_iota(jnp.int32, (L, W), 0) // SUB
                == (lax.broadcasted_iota(jnp.int32, (L, W), 1) % L) // SUB)

    qe = (q * jnp.exp2(b)).astype(BF16)
    kd = (k * jnp.exp2(chunk_row(L - 1) - b)).astype(BF16)
    zv = jnp.zeros((L, 2 * GLA_DV), BF16)

    for c in range(n_chunks):
        gi = c // cpg
        rs = slice(c * L, (c + 1) * L)
        a_band = jnp.where(same_sub, pltpu.roll(cband[rs], W - (SUB - 1), 1, stride=1, stride_axis=0), 0.0)
        dec = jnp.exp2(b[c * L + L - 1:c * L + L, :])
        for pr in range(n_pairs):
            ls = slice(pr * P, (pr + 1) * P)
            lhs_c = jnp.concatenate(
                [jnp.concatenate([qt[c * L + r * SUB:c * L + (r + 1) * SUB, ls] if i == r else zq
                                  for i in range(1, L // SUB)], axis=1) for r in range(L // SUB)],
                axis=0).astype(BF16)
            rhs_c = jnp.concatenate([jnp.concatenate([m[pr][0][rs], m[pr][1][rs]], axis=0) for m in ktm], axis=1)
            a_tot = (a_band[:, ls] + _dot_nt(lhs_c, rhs_c)).astype(BF16)
            vp = v_ref[rs, 2 * pr * GLA_DV:(2 * pr + 2) * GLA_DV]
            v_bd = jnp.concatenate([jnp.concatenate([vp[:, 0:GLA_DV], zv[:, 0:GLA_DV]], axis=1),
                                    jnp.concatenate([zv[:, 0:GLA_DV], vp[:, GLA_DV:]], axis=1)], axis=0)
            st = st_ref[gi, pr]
            o = _dot(a_tot, v_bd) + _dot_nt(qe[rs, ls], st.astype(BF16))
            on_ref[rs, 2 * pr * GLA_DV:(2 * pr + 2) * GLA_DV] = o
            d_st = jnp.where(bd_mask, _dot_tn(vp, kd[rs, ls]), 0.0)
            st_ref[gi, pr] = st * dec[:, ls] + d_st

    for hh in range(GLA_HEADS):
        hs = slice(hh * GLA_DV, (hh + 1) * GLA_DV)
        o = on_ref[:, hs]
        on = o * lax.rsqrt(jnp.mean(o * o, axis=-1, keepdims=True) + EPS) * g_ref[:, hs]
        o_ref[:, hs] = (on * r_ref[:, hs].astype(F32)).astype(BF16)

    @pl.when(t_idx == pl.num_programs(1) - 1)
    def _():
        for gi in range(gpt):
            for pr in range(n_pairs):
                st = st_ref[gi, pr]
                tt = jnp.where(even, st[0:GLA_DV], st[GLA_DV:2 * GLA_DV])
                sn_ref[gi, 2 * pr:2 * pr + 2] = tt.T.reshape(2, GLA_DK, GLA_DV)


def _band_spread():
    m = np.zeros((SUB, GLA_HEADS, GLA_DK, GLA_QK), np.float32)
    for d in range(SUB):
        for h in range(GLA_HEADS):
            m[d, h, :, h * GLA_DK + SUB - 1 - d] = 1.0
    return jnp.asarray(m.reshape(SUB * GLA_QK, GLA_QK), BF16)


def _gla(gq, gk, gv, la, gr, s0, g_out, *, groups, rows_per_group, tc):
    gpt = max(1, tc // rows_per_group)
    nt = max(1, rows_per_group // tc)

    def rows(c):
        return pl.BlockSpec((tc, c), lambda g, t: (g * nt + t, 0))

    state = pl.BlockSpec((gpt, GLA_HEADS, GLA_DK, GLA_DV), lambda g, t: (g, 0, 0, 0))
    spread = _band_spread()
    return pl.pallas_call(
        functools.partial(_gla_kernel, gpt=gpt),
        grid=(groups // gpt, nt),
        in_specs=[rows(GLA_QK), rows(GLA_QK), rows(GLA_WIDTH), rows(GLA_QK), rows(GLA_WIDTH), state,
                  _const((1, GLA_WIDTH)), _const(spread.shape)],
        out_specs=[rows(GLA_WIDTH), state],
        out_shape=[jax.ShapeDtypeStruct((groups * rows_per_group, GLA_WIDTH), BF16),
                   jax.ShapeDtypeStruct((groups, GLA_HEADS, GLA_DK, GLA_DV), F32)],
        scratch_shapes=[pltpu.VMEM((gpt, GLA_HEADS // 2, 2 * GLA_DV, LANES), F32),
                        pltpu.VMEM((SUB + tc, GLA_QK), F32), pltpu.VMEM((SUB + tc, GLA_QK), F32),
                        pltpu.VMEM((tc, SUB * GLA_QK), BF16), pltpu.VMEM((tc, GLA_WIDTH), F32)],
        compiler_params=_params("arbitrary", "arbitrary"),
        name="gla",
    )(gq, gk, gv, la, gr, s0, g_out, spread)


def _mlp_kernel(x_ref, a_ref, b_ref, g1_ref, sh2_ref, sc2_ref, g2_ref, gn_ref, wo_ref, wu_ref, wd_ref, y_ref,
                *, gpt, tf):
    tm, d = x_ref.shape

    def per_group(val, ref, scale_plus_one=False):
        m = ref[...]
        if scale_plus_one:
            m = 1.0 + m
        return (val.reshape(gpt, tm // gpt, d) * m).reshape(tm, d)

    mix = jnp.concatenate([a_ref[...], b_ref[...]], axis=1)
    x1 = x_ref[...] + per_group(_dot(mix, wo_ref[...]), g1_ref)
    xn = x1 * lax.rsqrt(jnp.mean(x1 * x1, axis=-1, keepdims=True) + EPS) * gn_ref[...]
    h2 = (per_group(xn, sc2_ref, True).reshape(gpt, tm // gpt, d) + sh2_ref[...]).reshape(tm, d).astype(BF16)
    acc = jnp.zeros((tm, d), F32)
    for j in range(wu_ref.shape[1] // tf):
        u = jnp.maximum(_dot(h2, wu_ref[:, j * tf:(j + 1) * tf]), 0.0)
        acc += _dot((u * u).astype(BF16), wd_ref[j * tf:(j + 1) * tf, :])
    y_ref[...] = x1 + per_group(acc, g2_ref)


def _mlp(x2, a_out, b_out, mod4, w, *, rows_per_group, tm):
    n, d = x2.shape
    gpt = max(1, tm // rows_per_group)
    tpg = max(1, rows_per_group // tm)

    def mod_spec(j):
        return pl.BlockSpec((gpt, None, 1, d), lambda i: ((i // tpg) if gpt == 1 else i, j, 0, 0))

    def rows(c):
        return pl.BlockSpec((tm, c), lambda i: (i, 0))

    return pl.pallas_call(
        functools.partial(_mlp_kernel, gpt=gpt, tf=1024),
        grid=(n // tm,),
        in_specs=[rows(d), rows(MLA_WIDTH), rows(GLA_WIDTH), mod_spec(2), mod_spec(3), mod_spec(4), mod_spec(5),
                  _const((1, d)), _const(w["w_out"].shape), _const(w["w_up"].shape), _const(w["w_down"].shape)],
        out_specs=rows(d),
        out_shape=jax.ShapeDtypeStruct((n, d), F32),
        compiler_params=_params("arbitrary"),
        name="out_proj_mlp",
    )(x2, a_out, b_out, mod4, mod4, mod4, mod4, w["g_norm2"], w["w_out"], w["w_up"], w["w_down"])


def _rope_table(start, count, repeat=1):
    half = QK_ROPE // 2
    inv = ROPE_THETA ** (-np.arange(half, dtype=np.float64) / half)
    ang = (start + np.arange(count, dtype=np.float64))[:, None] * inv[None, :]
    c, s, z = np.cos(ang), np.sin(ang), np.zeros_like(ang)
    tab = np.concatenate([c, c, z, z, -s, z, z, z, z, s, z, z], axis=1).astype(np.float32)
    return jnp.asarray(np.tile(tab, (repeat, 1)))


def _pad_gain(g_rope):
    return jnp.concatenate([g_rope, jnp.zeros((LANES - QK_ROPE,), F32)]).reshape(1, LANES)


def _prep_weights(w_in, g_norm1, g_q_lora, w_uq, g_kv_lora, w_ukv, g_q_head, g_k_head,
                  w_gate_up, b_gate_up, g_gla_out, w_out, g_norm2, w_up, w_down):
    d = w_in.shape[0]
    s = np.cumsum([0, Q_LORA, KV_LORA, QK_ROPE, GLA_QK, GLA_QK, GLA_WIDTH, GLA_GATE_RANK, GLA_WIDTH])
    piece = [w_in[:, s[i]:s[i + 1]] for i in range(8)]
    pad = jnp.zeros((d, LANES - QK_ROPE - GLA_GATE_RANK), w_in.dtype)
    w_in_p = jnp.concatenate([piece[0], piece[2], piece[6], pad, piece[1], piece[3], piece[4], piece[5],
                              piece[7]], axis=1).astype(BF16)
    uq = w_uq.reshape(Q_LORA, MLA_HEADS, QK_DIM)
    uq = jnp.concatenate([uq, jnp.zeros((Q_LORA, MLA_HEADS, HEAD_PAD - QK_DIM), w_uq.dtype)], axis=2)
    ukv = w_ukv.reshape(KV_LORA, MLA_HEADS, QK_NOPE + V_DIM)
    ukv = jnp.concatenate([ukv[:, :, :QK_NOPE].reshape(KV_LORA, -1), ukv[:, :, QK_NOPE:].reshape(KV_LORA, -1)], axis=1)
    w_gate = jnp.zeros((LANES, GLA_QK), w_gate_up.dtype).at[QK_ROPE:QK_ROPE + GLA_GATE_RANK].set(w_gate_up)
    bound = 1.02 * QK_DIM ** 0.5 * LOG2E * jnp.max(jnp.abs(g_q_head)) * jnp.max(jnp.abs(g_k_head))
    lane = jnp.arange(LANES) == QK_ROPE
    return {
        "w_in": w_in_p, "g_norm1": g_norm1.reshape(1, d), "g_q_lora": g_q_lora.reshape(1, Q_LORA),
        "w_uq": uq.reshape(Q_LORA, MLA_HEADS * HEAD_PAD).astype(BF16),
        "g_kv_lora": g_kv_lora.reshape(1, KV_LORA), "w_ukv": ukv.astype(BF16),
        "w_ukt": ukv[:, :MLA_HEADS * QK_NOPE].T.astype(BF16), "w_uv": ukv[:, MLA_HEADS * QK_NOPE:].astype(BF16),
        "gkr_col": g_k_head[QK_NOPE:].reshape(QK_ROPE, 1),
        "qone": lane.astype(F32).reshape(1, LANES), "kbias": jnp.where(lane, -bound, 0.0).reshape(1, LANES),
        "fast_softmax": (bound <= MAX_FIXED_SHIFT).astype(jnp.int32).reshape(1),
        "gqn": g_q_head[:QK_NOPE].reshape(1, LANES), "gqr": _pad_gain(g_q_head[QK_NOPE:]),
        "gkn": g_k_head[:QK_NOPE].reshape(1, LANES), "gkr": _pad_gain(g_k_head[QK_NOPE:]),
        "w_gate": w_gate.astype(BF16), "b_gate": b_gate_up.reshape(1, GLA_QK),
        "g_gla_out": g_gla_out.reshape(1, GLA_WIDTH), "w_out": w_out.astype(BF16),
        "g_norm2": g_norm2.reshape(1, d), "w_up": w_up.astype(BF16), "w_down": w_down.astype(BF16),
    }


def _layer(x, mod, past_lat, past_kr, s0, w, *, tm):
    batch, seq, d = x.shape
    n = batch * seq
    past = 0 if past_lat is None else past_lat.shape[1]
    x2 = x.reshape(n, d)
    mod4 = mod.reshape(batch, 6, 1, d)
    tm = min(tm, n)
    tab = _rope_table(past, seq, repeat=max(1, tm // seq))
    if past == 0:
        lat, krt, q, k, v, gq, gk, gv, la, gr = _projection(x2, mod4, tab, w, rows_per_group=seq, tm=tm, prompt=True)
        kr = jnp.swapaxes(krt, 1, 2)
        a_out = _attention_prompt(w["fast_softmax"], q, k, v, batch=batch, seq=seq, tq=min(512, seq))
    else:
        assert seq == CHUNK and past % CHUNK == 0
        lat, kr, krt, q, gq, gk, gv, la, gr = _projection(x2, mod4, tab, w, rows_per_group=seq, tm=tm, prompt=False)
        a_out = _attention_sample(q, past_lat, jnp.swapaxes(past_kr, 1, 2), lat, krt, w, tkb=min(512, past))
    b_out, s_new = _gla(gq, gk, gv, la, gr, s0, w["g_gla_out"], groups=batch, rows_per_group=seq,
                        tc=tm)
    y = _mlp(x2, a_out, b_out, mod4, w, rows_per_group=seq, tm=tm)
    return (y.reshape(batch, seq, d), lat.reshape(batch, seq, KV_LORA), kr.reshape(batch, seq, QK_ROPE), s_new)


def kernel(x_prompt, x_sample, cache_mla_latent, cache_mla_krope, state_gla, c_prompt, c_sample,
           w_ada, b_ada, g_norm1, w_in, g_q_lora, w_uq, g_kv_lora, w_ukv, g_q_head, g_k_head,
           w_gate_up, b_gate_up, g_gla_out, w_out, g_norm2, w_up, w_down):
    nb = x_prompt.shape[0]
    depth = w_ada.shape[0]
    y_p, y_s = x_prompt, x_sample
    outs = [[] for _ in range(6)]
    c_all = jnp.concatenate([c_prompt, c_sample], axis=0)
    for l in range(depth):
        w = _prep_weights(w_in[l], g_norm1[l], g_q_lora[l], w_uq[l], g_kv_lora[l], w_ukv[l], g_q_head[l],
                          g_k_head[l], w_gate_up[l], b_gate_up[l], g_gla_out[l], w_out[l], g_norm2[l],
                          w_up[l], w_down[l])
        mod = _modulation(c_all, w_ada[l], b_ada[l])
        zero_state = jnp.zeros((nb, GLA_HEADS, GLA_DK, GLA_DV), x_prompt.dtype)
        y_p, lat, kr, st = _layer(y_p, mod[:nb], None, None, zero_state, w, tm=512)
        outs[0].append(lat); outs[1].append(kr); outs[2].append(st)
        y_s, lat, kr, st = _layer(y_s, mod[nb:], cache_mla_latent[l], cache_mla_krope[l], state_gla[l], w, tm=512)
        outs[3].append(lat); outs[4].append(kr); outs[5].append(st)
    return (y_p, y_s) + tuple(jnp.stack(o) for o in outs)
```

```python
import functools

import jax
import jax.numpy as jnp
import numpy as np
from jax import lax
from jax.experimental import pallas as pl
from jax.experimental.pallas import tpu as pltpu

F32 = jnp.float32
BF16 = jnp.bfloat16

CHUNK = 64
EPS = 1e-6
MLA_HEADS = 4
Q_LORA = 384
KV_LORA = 256
QK_NOPE = 128
QK_ROPE = 64
QK_DIM = QK_NOPE + QK_ROPE
V_DIM = 128
ROPE_THETA = 10000.0
GLA_HEADS = 4
GLA_DK = 64
GLA_DV = 128
GLA_GATE_RANK = 16
GLA_TAU = 16.0
GLA_QK = GLA_HEADS * GLA_DK
GLA_WIDTH = GLA_HEADS * GLA_DV
MLA_WIDTH = MLA_HEADS * V_DIM
HEAD_PAD = 256
SUB = 8
LOG2E = 1.4426950408889634
MAX_FIXED_SHIFT = 48.0

LANES = 128
VMEM_LIMIT = 56 * 1024 * 1024

C_QKR = (0, 512)
C_KV = (512, 768)
C_GQ = (768, 1024)
C_GK = (1024, 1280)
C_GV = (1280, 1792)
C_GR = (1792, 2304)


def _dot(a, b):
    return jnp.dot(a, b, preferred_element_type=F32)


def _dot_nt(a, b):
    return lax.dot_general(a, b, (((1,), (1,)), ((), ())), preferred_element_type=F32)


def _dot_tn(a, b):
    return lax.dot_general(a, b, (((0,), (0,)), ((), ())), preferred_element_type=F32)


def _rope_tile(t, c, sa, sb):
    return t * c + pltpu.roll(t, 96, 1) * sa + pltpu.roll(t, 32, 1) * sb


def _params(*sem):
    return pltpu.CompilerParams(dimension_semantics=sem, vmem_limit_bytes=VMEM_LIMIT)


def _const(shape):
    return pl.BlockSpec(shape, lambda *_: (0,) * len(shape), pipeline_mode=pl.Buffered(1))


def _mod_kernel(c_ref, w_ref, b_ref, o_ref):
    c = c_ref[...]
    s = (c * jax.nn.sigmoid(c)).astype(BF16)
    o_ref[...] = _dot(s, w_ref[...].astype(BF16)) + b_ref[...]


def _modulation(c_all, w_ada, b_ada):
    g, d = c_all.shape
    n = w_ada.shape[1]
    tn = 1024
    return pl.pallas_call(
        _mod_kernel,
        grid=(n // tn,),
        in_specs=[pl.BlockSpec((g, d), lambda j: (0, 0)),
                  pl.BlockSpec((d, tn), lambda j: (0, j)),
                  pl.BlockSpec((1, tn), lambda j: (0, j))],
        out_specs=pl.BlockSpec((g, tn), lambda j: (0, j)),
        out_shape=jax.ShapeDtypeStruct((g, n), F32),
        compiler_params=_params("arbitrary"),
        name="adaln_mod",
    )(c_all, w_ada, b_ada.reshape(1, n))


def _proj_kernel(x_ref, sh_ref, sc_ref, g1_ref, win_ref, gql_ref, wuq_ref, gkv_ref, wukv_ref,
                 gqn_ref, gqr_ref, gkn_ref, gkr_ref, qone_ref, kbias_ref, tab_ref, wg_ref, bg_ref,
                 *outs, gpt, prompt):
    if prompt:
        lat_ref, krt_ref, q_ref, k_ref, v_ref, gq_ref, gk_ref, gv_ref, la_ref, gr_ref = outs
    else:
        lat_ref, kr_ref, krt_ref, q_ref, gq_ref, gk_ref, gv_ref, la_ref, gr_ref = outs
    tm, d = x_ref.shape
    x = x_ref[...]
    xn = x * lax.rsqrt(jnp.mean(x * x, axis=-1, keepdims=True) + EPS) * g1_ref[...]
    h = (xn.reshape(gpt, tm // gpt, d) * (1.0 + sc_ref[...]) + sh_ref[...]).reshape(tm, d)
    hb = h.astype(BF16)

    def col(c):
        return _dot(hb, win_ref[:, c[0]:c[1]])

    tab = tab_ref[...]
    cos, sa, sb = tab[:, 0:LANES], tab[:, LANES:2 * LANES], tab[:, 2 * LANES:3 * LANES]

    qkr = col(C_QKR)
    cq = qkr[:, 0:Q_LORA]
    cqn = cq * lax.rsqrt(jnp.mean(cq * cq, axis=-1, keepdims=True) + EPS) * gql_ref[...]
    qp = _dot(cqn.astype(BF16), wuq_ref[...])
    qscale = QK_DIM ** -0.5 * LOG2E
    for hh in range(MLA_HEADS):
        nope = qp[:, hh * HEAD_PAD:hh * HEAD_PAD + QK_NOPE]
        rt = qp[:, hh * HEAD_PAD + QK_NOPE:(hh + 1) * HEAD_PAD]
        ss = jnp.sum(nope * nope, axis=-1, keepdims=True) + jnp.sum(rt * rt, axis=-1, keepdims=True)
        inv = lax.rsqrt(ss * (1.0 / QK_DIM) + EPS)
        q_ref[:, hh * HEAD_PAD:hh * HEAD_PAD + QK_NOPE] = (nope * inv * gqn_ref[...] * qscale).astype(BF16)
        rq = _rope_tile(rt * inv * gqr_ref[...], cos, sa, sb)
        q_ref[:, hh * HEAD_PAD + QK_NOPE:(hh + 1) * HEAD_PAD] = (rq * qscale + qone_ref[...]).astype(BF16)

    ckv = col(C_KV)
    lat = ckv * lax.rsqrt(jnp.mean(ckv * ckv, axis=-1, keepdims=True) + EPS) * gkv_ref[...]
    lat_ref[...] = lat
    krg = qkr[:, Q_LORA:]
    krt_ref[...] = krg.T[0:QK_ROPE, :]
    if prompt:
        lane = lax.broadcasted_iota(jnp.int32, (1, LANES), 1)
        krm = jnp.where(lane < QK_ROPE, krg, 0.0)
        ssr = jnp.sum(krm * krm, axis=-1, keepdims=True)
        rk = _rope_tile(krm * gkr_ref[...], cos, sa, sb)
        kv = _dot(lat.astype(BF16), wukv_ref[...])
        for hh in range(MLA_HEADS):
            kn = kv[:, hh * QK_NOPE:(hh + 1) * QK_NOPE]
            inv = lax.rsqrt((jnp.sum(kn * kn, axis=-1, keepdims=True) + ssr) * (1.0 / QK_DIM) + EPS)
            k_ref[:, hh * HEAD_PAD:hh * HEAD_PAD + QK_NOPE] = (kn * inv * gkn_ref[...]).astype(BF16)
            k_ref[:, hh * HEAD_PAD + QK_NOPE:(hh + 1) * HEAD_PAD] = (rk * inv + kbias_ref[...]).astype(BF16)
        v_ref[...] = kv[:, MLA_HEADS * QK_NOPE:].astype(BF16)
    else:
        kr_ref[...] = krg[:, 0:QK_ROPE]

    gq_ref[...] = col(C_GQ) * (GLA_DK ** -0.5)
    gk_ref[...] = col(C_GK)
    gv_ref[...] = col(C_GV).astype(BF16)
    r = col(C_GR)
    gr_ref[...] = (r * jax.nn.sigmoid(r)).astype(BF16)
    z = _dot(krg.astype(BF16), wg_ref[...]) + bg_ref[...]
    la_ref[...] = (jnp.minimum(z, 0.0) - jnp.log(1.0 + jnp.exp(-jnp.abs(z)))) * (1.0 / GLA_TAU)


def _projection(x2, mod4, tab, w, *, rows_per_group, tm, prompt):
    n, d = x2.shape
    gpt = max(1, tm // rows_per_group)
    tpg = max(1, rows_per_group // tm)
    ntab = tab.shape[0] // tm

    def mod_spec(j):
        return pl.BlockSpec((gpt, None, 1, d), lambda i: ((i // tpg) if gpt == 1 else i, j, 0, 0))

    def rows(c):
        return pl.BlockSpec((tm, c), lambda i: (i, 0))

    def out(c, t):
        return rows(c), jax.ShapeDtypeStruct((n, c), t)

    gla_outs = [out(GLA_QK, F32), out(GLA_QK, F32), out(GLA_WIDTH, BF16), out(GLA_QK, F32), out(GLA_WIDTH, BF16)]
    qo = out(MLA_HEADS * HEAD_PAD, BF16)
    if prompt:
        krt = (pl.BlockSpec((None, QK_ROPE, tm), lambda i: (i // tpg, 0, i % tpg)),
               jax.ShapeDtypeStruct((n // rows_per_group, QK_ROPE, rows_per_group), F32))
        outs = [out(KV_LORA, F32), krt, qo, out(MLA_HEADS * HEAD_PAD, BF16), out(MLA_WIDTH, BF16)] + gla_outs
    else:
        krt = (pl.BlockSpec((QK_ROPE, tm), lambda i: (0, i)), jax.ShapeDtypeStruct((QK_ROPE, n), F32))
        outs = [out(KV_LORA, F32), out(QK_ROPE, F32), krt, qo] + gla_outs
    return pl.pallas_call(
        functools.partial(_proj_kernel, gpt=gpt, prompt=prompt),
        grid=(n // tm,),
        in_specs=[rows(d), mod_spec(0), mod_spec(1), _const((1, d)), _const(w["w_in"].shape),
                  _const((1, Q_LORA)), _const(w["w_uq"].shape), _const((1, KV_LORA)), _const(w["w_ukv"].shape),
                  _const((1, LANES)), _const((1, LANES)), _const((1, LANES)), _const((1, LANES)),
                  _const((1, LANES)), _const((1, LANES)),
                  pl.BlockSpec((tm, 3 * LANES), lambda i: (i % ntab, 0)),
                  _const(w["w_gate"].shape), _const((1, GLA_QK))],
        out_specs=[o[0] for o in outs],
        out_shape=[o[1] for o in outs],
        compiler_params=_params("arbitrary"),
        name="in_proj",
    )(x2, mod4, mod4, w["g_norm1"], w["w_in"], w["g_q_lora"], w["w_uq"], w["g_kv_lora"], w["w_ukv"],
      w["gqn"], w["gqr"], w["gkn"], w["gkr"], w["qone"], w["kbias"], tab, w["w_gate"], w["b_gate"])


def _attn_prompt_kernel(fast_ref, q_ref, k_ref, v_ref, o_ref, vx_ref, m_ref, acc_ref, *, tq):
    i = pl.program_id(1)
    heads = range(MLA_HEADS)

    @pl.when(i == 0)
    def _():
        for hh in heads:
            vx_ref[hh, :, 0:V_DIM] = v_ref[:, hh * V_DIM:(hh + 1) * V_DIM]
            vx_ref[hh, :, V_DIM:] = jnp.ones((v_ref.shape[0], V_DIM), BF16)

    acc_ref[...] = jnp.zeros(acc_ref.shape, F32)

    def scores(j, hh, masked):
        hs = slice(hh * HEAD_PAD, (hh + 1) * HEAD_PAD)
        s = _dot_nt(q_ref[:, hs], k_ref[pl.ds(pl.multiple_of(j * tq, tq), tq), hs])
        if masked:
            qc = lax.broadcasted_iota(jnp.int32, (tq, tq), 0) // CHUNK
            kc = lax.broadcasted_iota(jnp.int32, (tq, tq), 1) // CHUNK
            s = jnp.where(kc <= qc, s, -jnp.inf)
        return s

    def vblock(j, hh):
        return vx_ref[hh, pl.ds(pl.multiple_of(j * tq, tq), tq), :]

    def fast_block(j, masked):
        for hh in heads:
            acc_ref[hh] += _dot(jnp.exp2(scores(j, hh, masked)).astype(BF16), vblock(j, hh))

    def safe_block(j, masked):
        for hh in heads:
            s = scores(j, hh, masked)
            m = m_ref[hh]
            m_new = jnp.maximum(m, jnp.max(s, axis=1, keepdims=True))
            p = jnp.exp2(s - m_new).astype(BF16)
            acc_ref[hh] = jnp.exp2(m - m_new) * acc_ref[hh] + _dot(p, vblock(j, hh))
            m_ref[hh] = m_new

    def sweep(block):
        def body(j, c):
            block(j, False)
            return c

        lax.fori_loop(0, i, body, 0)
        block(i, True)

    @pl.when(fast_ref[0] == 1)
    def _():
        sweep(fast_block)

    @pl.when(fast_ref[0] != 1)
    def _():
        m_ref[...] = jnp.full(m_ref.shape, -jnp.inf, F32)
        sweep(safe_block)

    for hh in heads:
        acc = acc_ref[hh]
        o_ref[:, hh * V_DIM:(hh + 1) * V_DIM] = (acc[:, 0:V_DIM] / acc[:, V_DIM:]).astype(BF16)


def _attention_prompt(fast, q, k, v, *, batch, seq, tq):
    nq = seq // tq
    return pl.pallas_call(
        functools.partial(_attn_prompt_kernel, tq=tq),
        grid=(batch, nq),
        in_specs=[pl.BlockSpec(memory_space=pltpu.SMEM),
                  pl.BlockSpec((tq, MLA_HEADS * HEAD_PAD), lambda b, i: (b * nq + i, 0)),
                  pl.BlockSpec((seq, MLA_HEADS * HEAD_PAD), lambda b, i: (b, 0)),
                  pl.BlockSpec((seq, MLA_WIDTH), lambda b, i: (b, 0))],
        out_specs=pl.BlockSpec((tq, MLA_WIDTH), lambda b, i: (b * nq + i, 0)),
        out_shape=jax.ShapeDtypeStruct((batch * seq, MLA_WIDTH), BF16),
        scratch_shapes=[pltpu.VMEM((MLA_HEADS, seq, 2 * V_DIM), BF16), pltpu.VMEM((MLA_HEADS, tq, 1), F32),
                        pltpu.VMEM((MLA_HEADS, tq, 2 * V_DIM), F32)],
        compiler_params=_params("arbitrary", "arbitrary"),
        name="mla_attn_prompt",
    )(fast, q, k, v)


def _attn_sample_kernel(q_ref, lat_ref, krt_ref, latn_ref, krtn_ref, tab_ref, wukt_ref, wuv_ref, gkn_ref, gkr_ref,
                        o_ref, wq_ref, latb_ref, s_ref, *, tkb):
    b = pl.program_id(0)
    t = q_ref.shape[0]
    past = lat_ref.shape[0]
    nk = MLA_HEADS * QK_NOPE
    half = QK_ROPE // 2

    @pl.when(b == 0)
    def _():
        wq_ref[0:nk, :] = wukt_ref[...]

    qr = []
    for hh in range(MLA_HEADS):
        qn = (q_ref[:, hh * HEAD_PAD:hh * HEAD_PAD + QK_NOPE].astype(F32) * gkn_ref[...]).astype(BF16)
        wq_ref[nk + hh * t:nk + (hh + 1) * t, :] = _dot(qn, wukt_ref[hh * QK_NOPE:(hh + 1) * QK_NOPE, :]).astype(BF16)
        qr.append(q_ref[:, hh * HEAD_PAD + QK_NOPE:(hh + 1) * HEAD_PAD])
    qr = jnp.concatenate(qr, axis=0)

    def key_block(lat, krt, tab, col0, width, valid):
        latb = lat.astype(BF16)
        latb_ref[col0:col0 + width, :] = latb
        g = _dot_nt(wq_ref[...], latb)
        ssr = jnp.sum(krt * krt, axis=0, keepdims=True)
        kg = krt * gkr_ref[...]
        x1, x2 = kg[0:half], kg[half:QK_ROPE]
        c, sn = tab[0:half], tab[half:QK_ROPE]
        rope = jnp.concatenate([x1 * c - x2 * sn, x2 * c + x1 * sn, jnp.zeros((LANES - QK_ROPE, width), F32)], axis=0)
        srope = _dot(qr, rope.astype(BF16))
        rows = []
        for hh in range(MLA_HEADS):
            kn = g[hh * QK_NOPE:(hh + 1) * QK_NOPE]
            inv = lax.rsqrt((jnp.sum(kn * kn, axis=0, keepdims=True) + ssr) * (1.0 / QK_DIM) + EPS)
            rows.append((g[nk + hh * t:nk + (hh + 1) * t] + srope[hh * t:(hh + 1) * t]) * inv)
        sc = jnp.concatenate(rows, axis=0)
        if valid is not None:
            sc = jnp.where(valid, sc, -jnp.inf)
        s_ref[:, col0:col0 + width] = sc

    for blk in range(past // tkb):
        c0 = blk * tkb
        key_block(lat_ref[c0:c0 + tkb, :], krt_ref[:, c0:c0 + tkb], tab_ref[:, c0:c0 + tkb], c0, tkb, None)
    mine = lax.broadcasted_iota(jnp.int32, (1, 2 * t), 1) // t == b % 2
    key_block(latn_ref[...], krtn_ref[...], tab_ref[:, past:past + 2 * t], past, 2 * t, mine)

    s = s_ref[...]
    p = jnp.exp2(s - jnp.max(s, axis=1, keepdims=True))
    l = jnp.sum(p, axis=1, keepdims=True)
    pl_ = _dot(p.astype(BF16), latb_ref[...]).astype(BF16)
    for hh in range(MLA_HEADS):
        o = _dot(pl_[hh * t:(hh + 1) * t], wuv_ref[:, hh * V_DIM:(hh + 1) * V_DIM])
        o_ref[:, hh * V_DIM:(hh + 1) * V_DIM] = (o / l[hh * t:(hh + 1) * t]).astype(BF16)


def _attention_sample(q, past_lat, past_krt, lat_new, krt_new, w, *, tkb):
    batch, past, _ = past_lat.shape
    t = q.shape[0] // batch
    half = QK_ROPE // 2
    inv = ROPE_THETA ** (-np.arange(half, dtype=np.float64) / half)
    pos = np.concatenate([np.arange(past), past + np.arange(t), past + np.arange(t)]).astype(np.float64)
    ang = inv[:, None] * pos[None, :]
    tab = jnp.asarray(np.concatenate([np.cos(ang), np.sin(ang)], axis=0).astype(np.float32))
    s_pad = past + 2 * t
    return pl.pallas_call(
        functools.partial(_attn_sample_kernel, tkb=tkb),
        grid=(batch,),
        in_specs=[pl.BlockSpec((t, MLA_HEADS * HEAD_PAD), lambda b: (b, 0)),
                  pl.BlockSpec((None, past, KV_LORA), lambda b: (b, 0, 0)),
                  pl.BlockSpec((None, QK_ROPE, past), lambda b: (b, 0, 0)),
                  pl.BlockSpec((2 * t, KV_LORA), lambda b: (b // 2, 0)),
                  pl.BlockSpec((QK_ROPE, 2 * t), lambda b: (0, b // 2)),
                  _const(tab.shape), _const(w["w_ukt"].shape), _const(w["w_uv"].shape),
                  _const((1, LANES)), _const((QK_ROPE, 1))],
        out_specs=pl.BlockSpec((t, MLA_WIDTH), lambda b: (b, 0)),
        out_shape=jax.ShapeDtypeStruct((batch * t, MLA_WIDTH), BF16),
        scratch_shapes=[pltpu.VMEM((MLA_HEADS * (QK_NOPE + t), KV_LORA), BF16),
                        pltpu.VMEM((s_pad, KV_LORA), BF16),
                        pltpu.VMEM((MLA_HEADS * t, s_pad), F32)],
        compiler_params=_params("arbitrary"),
        name="mla_attn_sample",
    )(q, past_lat, past_krt, lat_new, krt_new, tab, w["w_ukt"], w["w_uv"], w["gkn"], w["gkr_col"])


def _gla_kernel(q_ref, k_ref, v_ref, la_ref, r_ref, s0_ref, g_ref, spread_ref, o_ref, sn_ref,
                st_ref, kp_ref, ap_ref, p_ref, on_ref, *, gpt):
    t_idx = pl.program_id(1)
    L, W, P = CHUNK, GLA_QK, LANES
    R = q_ref.shape[0]
    n_chunks = R // L
    cpg = n_chunks // gpt
    n_pairs = GLA_HEADS // 2

    lane_p = lax.broadcasted_iota(jnp.int32, (1, P), 1)
    even = lane_p < GLA_DK
    bd_mask = (lax.broadcasted_iota(jnp.int32, (2 * GLA_DV, P), 0) // GLA_DV
               == lax.broadcasted_iota(jnp.int32, (2 * GLA_DV, P), 1) // GLA_DK)

    @pl.when(t_idx == 0)
    def _():
        kp_ref[0:SUB, :] = jnp.zeros((SUB, W), F32)
        ap_ref[0:SUB, :] = jnp.zeros((SUB, W), F32)
        for gi in range(gpt):
            for pr in range(n_pairs):
                tt = s0_ref[gi, 2 * pr:2 * pr + 2].reshape(2 * GLA_DK, GLA_DV).T
                st_ref[gi, pr] = jnp.where(bd_mask, jnp.concatenate([tt, tt], axis=0), 0.0)

    q = q_ref[...]
    k = k_ref[...]
    la = la_ref[...]

    tri = (lax.broadcasted_iota(jnp.int32, (L, L), 0) >= lax.broadcasted_iota(jnp.int32, (L, L), 1)).astype(BF16)
    la_hi = la.astype(BF16)
    la2 = jnp.concatenate([la_hi, (la - la_hi.astype(F32)).astype(BF16)], axis=1)
    bs = []
    for c in range(n_chunks):
        t2 = _dot(tri, la2[c * L:(c + 1) * L, :])
        bs.append(t2[:, 0:W] + t2[:, W:2 * W])
    b = (jnp.concatenate(bs, axis=0) if n_chunks > 1 else bs[0]) * LOG2E
    b3 = b.reshape(n_chunks, L, W)

    def chunk_row(r):
        return jnp.broadcast_to(b3[:, r:r + 1, :], (n_chunks, L, W)).reshape(R, W)

    b_sub = jnp.broadcast_to(b.reshape(R // SUB, SUB, W)[:, 0:1, :], (R // SUB, SUB, W)).reshape(R, W)
    sub = (lax.broadcasted_iota(jnp.int32, (R, W), 0) % L) // SUB

    qt = q * jnp.exp2(b - b_sub)
    zb = jnp.zeros((), BF16)
    zq = jnp.zeros((SUB, P), F32)
    ktm = []
    for i in range(1, L // SUB):
        kt = (k * jnp.exp2(chunk_row(i * SUB) - b)).astype(BF16)
        ktm.append([[jnp.where((sub[:, 0:P] < i) & (even if e == 0 else ~even), kt[:, pr * P:(pr + 1) * P], zb)
                     for e in range(2)] for pr in range(n_pairs)])

    a = jnp.exp(la)
    kp_ref[SUB:SUB + R, :] = k
    ap_ref[SUB:SUB + R, :] = a
    p_ref[:, 0:W] = (q * k).astype(BF16)
    e = a
    for d in range(1, SUB):
        if d > 1:
            e = e * ap_ref[SUB - d + 1:SUB - d + 1 + R, :]
        p_ref[:, d * W:(d + 1) * W] = (q * kp_ref[SUB - d:SUB - d + R, :] * e).astype(BF16)
    cband = _dot(p_ref[...], spread_ref[...])
    same_sub = (lax.broadcasted_iota(jnp.int32, (L, W), 0) // SUB
                == (lax.broadcasted_iota(jnp.int32, (L, W), 1) % L) // SUB)

    qe = (q * jnp.exp2(b)).astype(BF16)
    kd = (k * jnp.exp2(chunk_row(L - 1) - b)).astype(BF16)
    zv = jnp.zeros((L, 2 * GLA_DV), BF16)

    for c in range(n_chunks):
        gi = c // cpg
        rs = slice(c * L, (c + 1) * L)
        a_band = jnp.where(same_sub, pltpu.roll(cband[rs], W - (SUB - 1), 1, stride=1, stride_axis=0), 0.0)
        dec = jnp.exp2(b[c * L + L - 1:c * L + L, :])
        for pr in range(n_pairs):
            ls = slice(pr * P, (pr + 1) * P)
            lhs_c = jnp.concatenate(
                [jnp.concatenate([qt[c * L + r * SUB:c * L + (r + 1) * SUB, ls] if i == r else zq
                                  for i in range(1, L // SUB)], axis=1) for r in range(L // SUB)],
                axis=0).astype(BF16)
            rhs_c = jnp.concatenate([jnp.concatenate([m[pr][0][rs], m[pr][1][rs]], axis=0) for m in ktm], axis=1)
            a_tot = (a_band[:, ls] + _dot_nt(lhs_c, rhs_c)).astype(BF16)
            vp = v_ref[rs, 2 * pr * GLA_DV:(2 * pr + 2) * GLA_DV]
            v_bd = jnp.concatenate([jnp.concatenate([vp[:, 0:GLA_DV], zv[:, 0:GLA_DV]], axis=1),
                                    jnp.concatenate([zv[:, 0:GLA_DV], vp[:, GLA_DV:]], axis=1)], axis=0)
            st = st_ref[gi, pr]
            o = _dot(a_tot, v_bd) + _dot_nt(qe[rs, ls], st.astype(BF16))
            on_ref[rs, 2 * pr * GLA_DV:(2 * pr + 2) * GLA_DV] = o
            d_st = jnp.where(bd_mask, _dot_tn(vp, kd[rs, ls]), 0.0)
            st_ref[gi, pr] = st * dec[:, ls] + d_st

    for hh in range(GLA_HEADS):
        hs = slice(hh * GLA_DV, (hh + 1) * GLA_DV)
        o = on_ref[:, hs]
        on = o * lax.rsqrt(jnp.mean(o * o, axis=-1, keepdims=True) + EPS) * g_ref[:, hs]
        o_ref[:, hs] = (on * r_ref[:, hs].astype(F32)).astype(BF16)

    @pl.when(t_idx == pl.num_programs(1) - 1)
    def _():
        for gi in range(gpt):
            for pr in range(n_pairs):
                st = st_ref[gi, pr]
                tt = jnp.where(even, st[0:GLA_DV], st[GLA_DV:2 * GLA_DV])
                sn_ref[gi, 2 * pr:2 * pr + 2] = tt.T.reshape(2, GLA_DK, GLA_DV)


def _band_spread():
    m = np.zeros((SUB, GLA_HEADS, GLA_DK, GLA_QK), np.float32)
    for d in range(SUB):
        for h in range(GLA_HEADS):
            m[d, h, :, h * GLA_DK + SUB - 1 - d] = 1.0
    return jnp.asarray(m.reshape(SUB * GLA_QK, GLA_QK), BF16)


def _gla(gq, gk, gv, la, gr, s0, g_out, *, groups, rows_per_group, tc):
    gpt = max(1, tc // rows_per_group)
    nt = max(1, rows_per_group // tc)

    def rows(c):
        return pl.BlockSpec((tc, c), lambda g, t: (g * nt + t, 0))

    state = pl.BlockSpec((gpt, GLA_HEADS, GLA_DK, GLA_DV), lambda g, t: (g, 0, 0, 0))
    spread = _band_spread()
    return pl.pallas_call(
        functools.partial(_gla_kernel, gpt=gpt),
        grid=(groups // gpt, nt),
        in_specs=[rows(GLA_QK), rows(GLA_QK), rows(GLA_WIDTH), rows(GLA_QK), rows(GLA_WIDTH), state,
                  _const((1, GLA_WIDTH)), _const(spread.shape)],
        out_specs=[rows(GLA_WIDTH), state],
        out_shape=[jax.ShapeDtypeStruct((groups * rows_per_group, GLA_WIDTH), BF16),
                   jax.ShapeDtypeStruct((groups, GLA_HEADS, GLA_DK, GLA_DV), F32)],
        scratch_shapes=[pltpu.VMEM((gpt, GLA_HEADS // 2, 2 * GLA_DV, LANES), F32),
                        pltpu.VMEM((SUB + tc, GLA_QK), F32), pltpu.VMEM((SUB + tc, GLA_QK), F32),
                        pltpu.VMEM((tc, SUB * GLA_QK), BF16), pltpu.VMEM((tc, GLA_WIDTH), F32)],
        compiler_params=_params("arbitrary", "arbitrary"),
        name="gla",
    )(gq, gk, gv, la, gr, s0, g_out, spread)


def _mlp_kernel(x_ref, a_ref, b_ref, g1_ref, sh2_ref, sc2_ref, g2_ref, gn_ref, wo_ref, wu_ref, wd_ref, y_ref,
                *, gpt, tf):
    tm, d = x_ref.shape

    def per_group(val, ref, scale_plus_one=False):
        m = ref[...]
        if scale_plus_one:
            m = 1.0 + m
        return (val.reshape(gpt, tm // gpt, d) * m).reshape(tm, d)

    mix = jnp.concatenate([a_ref[...], b_ref[...]], axis=1)
    x1 = x_ref[...] + per_group(_dot(mix, wo_ref[...]), g1_ref)
    xn = x1 * lax.rsqrt(jnp.mean(x1 * x1, axis=-1, keepdims=True) + EPS) * gn_ref[...]
    h2 = (per_group(xn, sc2_ref, True).reshape(gpt, tm // gpt, d) + sh2_ref[...]).reshape(tm, d).astype(BF16)
    acc = jnp.zeros((tm, d), F32)
    for j in range(wu_ref.shape[1] // tf):
        u = jnp.maximum(_dot(h2, wu_ref[:, j * tf:(j + 1) * tf]), 0.0)
        acc += _dot((u * u).astype(BF16), wd_ref[j * tf:(j + 1) * tf, :])
    y_ref[...] = x1 + per_group(acc, g2_ref)


def _mlp(x2, a_out, b_out, mod4, w, *, rows_per_group, tm):
    n, d = x2.shape
    gpt = max(1, tm // rows_per_group)
    tpg = max(1, rows_per_group // tm)

    def mod_spec(j):
        return pl.BlockSpec((gpt, None, 1, d), lambda i: ((i // tpg) if gpt == 1 else i, j, 0, 0))

    def rows(c):
        return pl.BlockSpec((tm, c), lambda i: (i, 0))

    return pl.pallas_call(
        functools.partial(_mlp_kernel, gpt=gpt, tf=1024),
        grid=(n // tm,),
        in_specs=[rows(d), rows(MLA_WIDTH), rows(GLA_WIDTH), mod_spec(2), mod_spec(3), mod_spec(4), mod_spec(5),
                  _const((1, d)), _const(w["w_out"].shape), _const(w["w_up"].shape), _const(w["w_down"].shape)],
        out_specs=rows(d),
        out_shape=jax.ShapeDtypeStruct((n, d), F32),
        compiler_params=_params("arbitrary"),
        name="out_proj_mlp",
    )(x2, a_out, b_out, mod4, mod4, mod4, mod4, w["g_norm2"], w["w_out"], w["w_up"], w["w_down"])


def _rope_table(start, count, repeat=1):
    half = QK_ROPE // 2
    inv = ROPE_THETA ** (-np.arange(half, dtype=np.float64) / half)
    ang = (start + np.arange(count, dtype=np.float64))[:, None] * inv[None, :]
    c, s, z = np.cos(ang), np.sin(ang), np.zeros_like(ang)
    tab = np.concatenate([c, c, z, z, -s, z, z, z, z, s, z, z], axis=1).astype(np.float32)
    return jnp.asarray(np.tile(tab, (repeat, 1)))


def _pad_gain(g_rope):
    return jnp.concatenate([g_rope, jnp.zeros((LANES - QK_ROPE,), F32)]).reshape(1, LANES)


def _prep_weights(w_in, g_norm1, g_q_lora, w_uq, g_kv_lora, w_ukv, g_q_head, g_k_head,
                  w_gate_up, b_gate_up, g_gla_out, w_out, g_norm2, w_up, w_down):
    d = w_in.shape[0]
    s = np.cumsum([0, Q_LORA, KV_LORA, QK_ROPE, GLA_QK, GLA_QK, GLA_WIDTH, GLA_GATE_RANK, GLA_WIDTH])
    piece = [w_in[:, s[i]:s[i + 1]] for i in range(8)]
    pad = jnp.zeros((d, LANES - QK_ROPE - GLA_GATE_RANK), w_in.dtype)
    w_in_p = jnp.concatenate([piece[0], piece[2], piece[6], pad, piece[1], piece[3], piece[4], piece[5],
                              piece[7]], axis=1).astype(BF16)
    uq = w_uq.reshape(Q_LORA, MLA_HEADS, QK_DIM)
    uq = jnp.concatenate([uq, jnp.zeros((Q_LORA, MLA_HEADS, HEAD_PAD - QK_DIM), w_uq.dtype)], axis=2)
    ukv = w_ukv.reshape(KV_LORA, MLA_HEADS, QK_NOPE + V_DIM)
    ukv = jnp.concatenate([ukv[:, :, :QK_NOPE].reshape(KV_LORA, -1), ukv[:, :, QK_NOPE:].reshape(KV_LORA, -1)], axis=1)
    w_gate = jnp.zeros((LANES, GLA_QK), w_gate_up.dtype).at[QK_ROPE:QK_ROPE + GLA_GATE_RANK].set(w_gate_up)
    bound = 1.02 * QK_DIM ** 0.5 * LOG2E * jnp.max(jnp.abs(g_q_head)) * jnp.max(jnp.abs(g_k_head))
    lane = jnp.arange(LANES) == QK_ROPE
    return {
        "w_in": w_in_p, "g_norm1": g_norm1.reshape(1, d), "g_q_lora": g_q_lora.reshape(1, Q_LORA),
        "w_uq": uq.reshape(Q_LORA, MLA_HEADS * HEAD_PAD).astype(BF16),
        "g_kv_lora": g_kv_lora.reshape(1, KV_LORA), "w_ukv": ukv.astype(BF16),
        "w_ukt": ukv[:, :MLA_HEADS * QK_NOPE].T.astype(BF16), "w_uv": ukv[:, MLA_HEADS * QK_NOPE:].astype(BF16),
        "gkr_col": g_k_head[QK_NOPE:].reshape(QK_ROPE, 1),
        "qone": lane.astype(F32).reshape(1, LANES), "kbias": jnp.where(lane, -bound, 0.0).reshape(1, LANES),
        "fast_softmax": (bound <= MAX_FIXED_SHIFT).astype(jnp.int32).reshape(1),
        "gqn": g_q_head[:QK_NOPE].reshape(1, LANES), "gqr": _pad_gain(g_q_head[QK_NOPE:]),
        "gkn": g_k_head[:QK_NOPE].reshape(1, LANES), "gkr": _pad_gain(g_k_head[QK_NOPE:]),
        "w_gate": w_gate.astype(BF16), "b_gate": b_gate_up.reshape(1, GLA_QK),
        "g_gla_out": g_gla_out.reshape(1, GLA_WIDTH), "w_out": w_out.astype(BF16),
        "g_norm2": g_norm2.reshape(1, d), "w_up": w_up.astype(BF16), "w_down": w_down.astype(BF16),
    }


def _layer(x, mod, past_lat, past_kr, s0, w, *, tm):
    batch, seq, d = x.shape
    n = batch * seq
    past = 0 if past_lat is None else past_lat.shape[1]
    x2 = x.reshape(n, d)
    mod4 = mod.reshape(batch, 6, 1, d)
    tm = min(tm, n)
    tab = _rope_table(past, seq, repeat=max(1, tm // seq))
    if past == 0:
        lat, krt, q, k, v, gq, gk, gv, la, gr = _projection(x2, mod4, tab, w, rows_per_group=seq, tm=tm, prompt=True)
        kr = jnp.swapaxes(krt, 1, 2)
        a_out = _attention_prompt(w["fast_softmax"], q, k, v, batch=batch, seq=seq, tq=min(512, seq))
    else:
        assert seq == CHUNK and past % CHUNK == 0
        lat, kr, krt, q, gq, gk, gv, la, gr = _projection(x2, mod4, tab, w, rows_per_group=seq, tm=tm, prompt=False)
        a_out = _attention_sample(q, past_lat, jnp.swapaxes(past_kr, 1, 2), lat, krt, w, tkb=min(512, past))
    b_out, s_new = _gla(gq, gk, gv, la, gr, s0, w["g_gla_out"], groups=batch, rows_per_group=seq,
                        tc=tm)
    y = _mlp(x2, a_out, b_out, mod4, w, rows_per_group=seq, tm=tm)
    return (y.reshape(batch, seq, d), lat.reshape(batch, seq, KV_LORA), kr.reshape(batch, seq, QK_ROPE), s_new)


def kernel(x_prompt, x_sample, cache_mla_latent, cache_mla_krope, state_gla, c_prompt, c_sample,
           w_ada, b_ada, g_norm1, w_in, g_q_lora, w_uq, g_kv_lora, w_ukv, g_q_head, g_k_head,
           w_gate_up, b_gate_up, g_gla_out, w_out, g_norm2, w_up, w_down):
    nb = x_prompt.shape[0]
    depth = w_ada.shape[0]
    y_p, y_s = x_prompt, x_sample
    outs = [[] for _ in range(6)]
    c_all = jnp.concatenate([c_prompt, c_sample], axis=0)
    for l in range(depth):
        w = _prep_weights(w_in[l], g_norm1[l], g_q_lora[l], w_uq[l], g_kv_lora[l], w_ukv[l], g_q_head[l],
                          g_k_head[l], w_gate_up[l], b_gate_up[l], g_gla_out[l], w_out[l], g_norm2[l],
                          w_up[l], w_down[l])
        mod = _modulation(c_all, w_ada[l], b_ada[l])
        zero_state = jnp.zeros((nb, GLA_HEADS, GLA_DK, GLA_DV), x_prompt.dtype)
        y_p, lat, kr, st = _layer(y_p, mod[:nb], None, None, zero_state, w, tm=512)
        outs[0].append(lat); outs[1].append(kr); outs[2].append(st)
        y_s, lat, kr, st = _layer(y_s, mod[nb:], cache_mla_latent[l], cache_mla_krope[l], state_gla[l], w, tm=512)
        outs[3].append(lat); outs[4].append(kr); outs[5].append(st)
    return (y_p, y_s) + tuple(jnp.stack(o) for o in outs)
```

```python
import functools

import jax
import jax.numpy as jnp
import numpy as np
from jax import lax
from jax.experimental import pallas as pl
from jax.experimental.pallas import tpu as pltpu

F32 = jnp.float32
BF16 = jnp.bfloat16

CHUNK = 64
EPS = 1e-6
MLA_HEADS = 4
Q_LORA = 384
KV_LORA = 256
QK_NOPE = 128
QK_ROPE = 64
QK_DIM = QK_NOPE + QK_ROPE
V_DIM = 128
ROPE_THETA = 10000.0
GLA_HEADS = 4
GLA_DK = 64
GLA_DV = 128
GLA_GATE_RANK = 16
GLA_TAU = 16.0
GLA_QK = GLA_HEADS * GLA_DK
GLA_WIDTH = GLA_HEADS * GLA_DV
MLA_WIDTH = MLA_HEADS * V_DIM
HEAD_PAD = 256
SUB = 8
LOG2E = 1.4426950408889634
MAX_FIXED_SHIFT = 48.0

LANES = 128
VMEM_LIMIT = 56 * 1024 * 1024

C_QKR = (0, 512)
C_KV = (512, 768)
C_GQ = (768, 1024)
C_GK = (1024, 1280)
C_GV = (1280, 1792)
C_GR = (1792, 2304)


def _dot(a, b):
    return jnp.dot(a, b, preferred_element_type=F32)


def _dot_nt(a, b):
    return lax.dot_general(a, b, (((1,), (1,)), ((), ())), preferred_element_type=F32)


def _dot_tn(a, b):
    return lax.dot_general(a, b, (((0,), (0,)), ((), ())), preferred_element_type=F32)


def _rope_tile(t, c, sa, sb):
    return t * c + pltpu.roll(t, 96, 1) * sa + pltpu.roll(t, 32, 1) * sb


def _params(*sem):
    return pltpu.CompilerParams(dimension_semantics=sem, vmem_limit_bytes=VMEM_LIMIT)


def _const(shape):
    return pl.BlockSpec(shape, lambda *_: (0,) * len(shape), pipeline_mode=pl.Buffered(1))


def _mod_kernel(c_ref, w_ref, b_ref, o_ref):
    c = c_ref[...]
    s = (c * jax.nn.sigmoid(c)).astype(BF16)
    o_ref[...] = _dot(s, w_ref[...].astype(BF16)) + b_ref[...]


def _modulation(c_all, w_ada, b_ada):
    g, d = c_all.shape
    n = w_ada.shape[1]
    tn = 1024
    return pl.pallas_call(
        _mod_kernel,
        grid=(n // tn,),
        in_specs=[pl.BlockSpec((g, d), lambda j: (0, 0)),
                  pl.BlockSpec((d, tn), lambda j: (0, j)),
                  pl.BlockSpec((1, tn), lambda j: (0, j))],
        out_specs=pl.BlockSpec((g, tn), lambda j: (0, j)),
        out_shape=jax.ShapeDtypeStruct((g, n), F32),
        compiler_params=_params("arbitrary"),
        name="adaln_mod",
    )(c_all, w_ada, b_ada.reshape(1, n))


def _proj_kernel(x_ref, sh_ref, sc_ref, g1_ref, win_ref, gql_ref, wuq_ref, gkv_ref, wukv_ref,
                 gqn_ref, gqr_ref, gkn_ref, gkr_ref, qone_ref, kbias_ref, tab_ref, wg_ref, bg_ref,
                 *outs, gpt, prompt):
    if prompt:
        lat_ref, krt_ref, q_ref, k_ref, v_ref, gq_ref, gk_ref, gv_ref, la_ref, gr_ref = outs
    else:
        lat_ref, kr_ref, krt_ref, q_ref, gq_ref, gk_ref, gv_ref, la_ref, gr_ref = outs
    tm, d = x_ref.shape
    x = x_ref[...]
    xn = x * lax.rsqrt(jnp.mean(x * x, axis=-1, keepdims=True) + EPS) * g1_ref[...]
    h = (xn.reshape(gpt, tm // gpt, d) * (1.0 + sc_ref[...]) + sh_ref[...]).reshape(tm, d)
    hb = h.astype(BF16)

    def col(c):
        return _dot(hb, win_ref[:, c[0]:c[1]])

    tab = tab_ref[...]
    cos, sa, sb = tab[:, 0:LANES], tab[:, LANES:2 * LANES], tab[:, 2 * LANES:3 * LANES]

    qkr = col(C_QKR)
    cq = qkr[:, 0:Q_LORA]
    cqn = cq * lax.rsqrt(jnp.mean(cq * cq, axis=-1, keepdims=True) + EPS) * gql_ref[...]
    qp = _dot(cqn.astype(BF16), wuq_ref[...])
    qscale = QK_DIM ** -0.5 * LOG2E
    for hh in range(MLA_HEADS):
        nope = qp[:, hh * HEAD_PAD:hh * HEAD_PAD + QK_NOPE]
        rt = qp[:, hh * HEAD_PAD + QK_NOPE:(hh + 1) * HEAD_PAD]
        ss = jnp.sum(nope * nope, axis=-1, keepdims=True) + jnp.sum(rt * rt, axis=-1, keepdims=True)
        inv = lax.rsqrt(ss * (1.0 / QK_DIM) + EPS)
        q_ref[:, hh * HEAD_PAD:hh * HEAD_PAD + QK_NOPE] = (nope * inv * gqn_ref[...] * qscale).astype(BF16)
        rq = _rope_tile(rt * inv * gqr_ref[...], cos, sa, sb)
        q_ref[:, hh * HEAD_PAD + QK_NOPE:(hh + 1) * HEAD_PAD] = (rq * qscale + qone_ref[...]).astype(BF16)

    ckv = col(C_KV)
    lat = ckv * lax.rsqrt(jnp.mean(ckv * ckv, axis=-1, keepdims=True) + EPS) * gkv_ref[...]
    lat_ref[...] = lat
    krg = qkr[:, Q_LORA:]
    krt_ref[...] = krg.T[0:QK_ROPE, :]
    if prompt:
        lane = lax.broadcasted_iota(jnp.int32, (1, LANES), 1)
        krm = jnp.where(lane < QK_ROPE, krg, 0.0)
        ssr = jnp.sum(krm * krm, axis=-1, keepdims=True)
        rk = _rope_tile(krm * gkr_ref[...], cos, sa, sb)
        kv = _dot(lat.astype(BF16), wukv_ref[...])
        for hh in range(MLA_HEADS):
            kn = kv[:, hh * QK_NOPE:(hh + 1) * QK_NOPE]
            inv = lax.rsqrt((jnp.sum(kn * kn, axis=-1, keepdims=True) + ssr) * (1.0 / QK_DIM) + EPS)
            k_ref[:, hh * HEAD_PAD:hh * HEAD_PAD + QK_NOPE] = (kn * inv * gkn_ref[...]).astype(BF16)
            k_ref[:, hh * HEAD_PAD + QK_NOPE:(hh + 1) * HEAD_PAD] = (rk * inv + kbias_ref[...]).astype(BF16)
        v_ref[...] = kv[:, MLA_HEADS * QK_NOPE:].astype(BF16)
    else:
        kr_ref[...] = krg[:, 0:QK_ROPE]

    gq_ref[...] = col(C_GQ) * (GLA_DK ** -0.5)
    gk_ref[...] = col(C_GK)
    gv_ref[...] = col(C_GV).astype(BF16)
    r = col(C_GR)
    gr_ref[...] = (r * jax.nn.sigmoid(r)).astype(BF16)
    z = _dot(krg.astype(BF16), wg_ref[...]) + bg_ref[...]
    la_ref[...] = (jnp.minimum(z, 0.0) - jnp.log(1.0 + jnp.exp(-jnp.abs(z)))) * (1.0 / GLA_TAU)


def _projection(x2, mod4, tab, w, *, rows_per_group, tm, prompt):
    n, d = x2.shape
    gpt = max(1, tm // rows_per_group)
    tpg = max(1, rows_per_group // tm)
    ntab = tab.shape[0] // tm

    def mod_spec(j):
        return pl.BlockSpec((gpt, None, 1, d), lambda i: ((i // tpg) if gpt == 1 else i, j, 0, 0))

    def rows(c):
        return pl.BlockSpec((tm, c), lambda i: (i, 0))

    def out(c, t):
        return rows(c), jax.ShapeDtypeStruct((n, c), t)

    gla_outs = [out(GLA_QK, F32), out(GLA_QK, F32), out(GLA_WIDTH, BF16), out(GLA_QK, F32), out(GLA_WIDTH, BF16)]
    qo = out(MLA_HEADS * HEAD_PAD, BF16)
    if prompt:
        krt = (pl.BlockSpec((None, QK_ROPE, tm), lambda i: (i // tpg, 0, i % tpg)),
               jax.ShapeDtypeStruct((n // rows_per_group, QK_ROPE, rows_per_group), F32))
        outs = [out(KV_LORA, F32), krt, qo, out(MLA_HEADS * HEAD_PAD, BF16), out(MLA_WIDTH, BF16)] + gla_outs
    else:
        krt = (pl.BlockSpec((QK_ROPE, tm), lambda i: (0, i)), jax.ShapeDtypeStruct((QK_ROPE, n), F32))
        outs = [out(KV_LORA, F32), out(QK_ROPE, F32), krt, qo] + gla_outs
    return pl.pallas_call(
        functools.partial(_proj_kernel, gpt=gpt, prompt=prompt),
        grid=(n // tm,),
        in_specs=[rows(d), mod_spec(0), mod_spec(1), _const((1, d)), _const(w["w_in"].shape),
                  _const((1, Q_LORA)), _const(w["w_uq"].shape), _const((1, KV_LORA)), _const(w["w_ukv"].shape),
                  _const((1, LANES)), _const((1, LANES)), _const((1, LANES)), _const((1, LANES)),
                  _const((1, LANES)), _const((1, LANES)),
                  pl.BlockSpec((tm, 3 * LANES), lambda i: (i % ntab, 0)),
                  _const(w["w_gate"].shape), _const((1, GLA_QK))],
        out_specs=[o[0] for o in outs],
        out_shape=[o[1] for o in outs],
        compiler_params=_params("arbitrary"),
        name="in_proj",
    )(x2, mod4, mod4, w["g_norm1"], w["w_in"], w["g_q_lora"], w["w_uq"], w["g_kv_lora"], w["w_ukv"],
      w["gqn"], w["gqr"], w["gkn"], w["gkr"], w["qone"], w["kbias"], tab, w["w_gate"], w["b_gate"])


def _attn_prompt_kernel(fast_ref, q_ref, k_ref, v_ref, o_ref, vx_ref, m_ref, acc_ref, *, tq):
    i = pl.program_id(1)
    heads = range(MLA_HEADS)

    @pl.when(i == 0)
    def _():
        for hh in heads:
            vx_ref[hh, :, 0:V_DIM] = v_ref[:, hh * V_DIM:(hh + 1) * V_DIM]
            vx_ref[hh, :, V_DIM:] = jnp.ones((v_ref.shape[0], V_DIM), BF16)

    acc_ref[...] = jnp.zeros(acc_ref.shape, F32)

    def scores(j, hh, masked):
        hs = slice(hh * HEAD_PAD, (hh + 1) * HEAD_PAD)
        s = _dot_nt(q_ref[:, hs], k_ref[pl.ds(pl.multiple_of(j * tq, tq), tq), hs])
        if masked:
            qc = lax.broadcasted_iota(jnp.int32, (tq, tq), 0) // CHUNK
            kc = lax.broadcasted_iota(jnp.int32, (tq, tq), 1) // CHUNK
            s = jnp.where(kc <= qc, s, -jnp.inf)
        return s

    def vblock(j, hh):
        return vx_ref[hh, pl.ds(pl.multiple_of(j * tq, tq), tq), :]

    def fast_block(j, masked):
        for hh in heads:
            acc_ref[hh] += _dot(jnp.exp2(scores(j, hh, masked)).astype(BF16), vblock(j, hh))

    def safe_block(j, masked):
        for hh in heads:
            s = scores(j, hh, masked)
            m = m_ref[hh]
            m_new = jnp.maximum(m, jnp.max(s, axis=1, keepdims=True))
            p = jnp.exp2(s - m_new).astype(BF16)
            acc_ref[hh] = jnp.exp2(m - m_new) * acc_ref[hh] + _dot(p, vblock(j, hh))
            m_ref[hh] = m_new

    def sweep(block):
        def body(j, c):
            block(j, False)
            return c

        lax.fori_loop(0, i, body, 0)
        block(i, True)

    @pl.when(fast_ref[0] == 1)
    def _():
        sweep(fast_block)

    @pl.when(fast_ref[0] != 1)
    def _():
        m_ref[...] = jnp.full(m_ref.shape, -jnp.inf, F32)
        sweep(safe_block)

    for hh in heads:
        acc = acc_ref[hh]
        o_ref[:, hh * V_DIM:(hh + 1) * V_DIM] = (acc[:, 0:V_DIM] / acc[:, V_DIM:]).astype(BF16)


def _attention_prompt(fast, q, k, v, *, batch, seq, tq):
    nq = seq // tq
    return pl.pallas_call(
        functools.partial(_attn_prompt_kernel, tq=tq),
        grid=(batch, nq),
        in_specs=[pl.BlockSpec(memory_space=pltpu.SMEM),
                  pl.BlockSpec((tq, MLA_HEADS * HEAD_PAD), lambda b, i: (b * nq + i, 0)),
                  pl.BlockSpec((seq, MLA_HEADS * HEAD_PAD), lambda b, i: (b, 0)),
                  pl.BlockSpec((seq, MLA_WIDTH), lambda b, i: (b, 0))],
        out_specs=pl.BlockSpec((tq, MLA_WIDTH), lambda b, i: (b * nq + i, 0)),
        out_shape=jax.ShapeDtypeStruct((batch * seq, MLA_WIDTH), BF16),
        scratch_shapes=[pltpu.VMEM((MLA_HEADS, seq, 2 * V_DIM), BF16), pltpu.VMEM((MLA_HEADS, tq, 1), F32),
                        pltpu.VMEM((MLA_HEADS, tq, 2 * V_DIM), F32)],
        compiler_params=_params("arbitrary", "arbitrary"),
        name="mla_attn_prompt",
    )(fast, q, k, v)


def _attn_sample_kernel(q_ref, lat_ref, krt_ref, latn_ref, krtn_ref, tab_ref, wukt_ref, wuv_ref, gkn_ref, gkr_ref,
                        o_ref, wq_ref, latb_ref, s_ref, *, tkb):
    b = pl.program_id(0)
    t = q_ref.shape[0]
    past = lat_ref.shape[0]
    nk = MLA_HEADS * QK_NOPE
    half = QK_ROPE // 2

    @pl.when(b == 0)
    def _():
        wq_ref[0:nk, :] = wukt_ref[...]

    qr = []
    for hh in range(MLA_HEADS):
        qn = (q_ref[:, hh * HEAD_PAD:hh * HEAD_PAD + QK_NOPE].astype(F32) * gkn_ref[...]).astype(BF16)
        wq_ref[nk + hh * t:nk + (hh + 1) * t, :] = _dot(qn, wukt_ref[hh * QK_NOPE:(hh + 1) * QK_NOPE, :]).astype(BF16)
        qr.append(q_ref[:, hh * HEAD_PAD + QK_NOPE:(hh + 1) * HEAD_PAD])
    qr = jnp.concatenate(qr, axis=0)

    def key_block(lat, krt, tab, col0, width, valid):
        latb = lat.astype(BF16)
        latb_ref[col0:col0 + width, :] = latb
        g = _dot_nt(wq_ref[...], latb)
        ssr = jnp.sum(krt * krt, axis=0, keepdims=True)
        kg = krt * gkr_ref[...]
        x1, x2 = kg[0:half], kg[half:QK_ROPE]
        c, sn = tab[0:half], tab[half:QK_ROPE]
        rope = jnp.concatenate([x1 * c - x2 * sn, x2 * c + x1 * sn, jnp.zeros((LANES - QK_ROPE, width), F32)], axis=0)
        srope = _dot(qr, rope.astype(BF16))
        rows = []
        for hh in range(MLA_HEADS):
            kn = g[hh * QK_NOPE:(hh + 1) * QK_NOPE]
            inv = lax.rsqrt((jnp.sum(kn * kn, axis=0, keepdims=True) + ssr) * (1.0 / QK_DIM) + EPS)
            rows.append((g[nk + hh * t:nk + (hh + 1) * t] + srope[hh * t:(hh + 1) * t]) * inv)
        sc = jnp.concatenate(rows, axis=0)
        if valid is not None:
            sc = jnp.where(valid, sc, -jnp.inf)
        s_ref[:, col0:col0 + width] = sc

    for blk in range(past // tkb):
        c0 = blk * tkb
        key_block(lat_ref[c0:c0 + tkb, :], krt_ref[:, c0:c0 + tkb], tab_ref[:, c0:c0 + tkb], c0, tkb, None)
    mine = lax.broadcasted_iota(jnp.int32, (1, 2 * t), 1) // t == b % 2
    key_block(latn_ref[...], krtn_ref[...], tab_ref[:, past:past + 2 * t], past, 2 * t, mine)

    s = s_ref[...]
    p = jnp.exp2(s - jnp.max(s, axis=1, keepdims=True))
    l = jnp.sum(p, axis=1, keepdims=True)
    pl_ = _dot(p.astype(BF16), latb_ref[...]).astype(BF16)
    for hh in range(MLA_HEADS):
        o = _dot(pl_[hh * t:(hh + 1) * t], wuv_ref[:, hh * V_DIM:(hh + 1) * V_DIM])
        o_ref[:, hh * V_DIM:(hh + 1) * V_DIM] = (o / l[hh * t:(hh + 1) * t]).astype(BF16)


def _attention_sample(q, past_lat, past_krt, lat_new, krt_new, w, *, tkb):
    batch, past, _ = past_lat.shape
    t = q.shape[0] // batch
    half = QK_ROPE // 2
    inv = ROPE_THETA ** (-np.arange(half, dtype=np.float64) / half)
    pos = np.concatenate([np.arange(past), past + np.arange(t), past + np.arange(t)]).astype(np.float64)
    ang = inv[:, None] * pos[None, :]
    tab = jnp.asarray(np.concatenate([np.cos(ang), np.sin(ang)], axis=0).astype(np.float32))
    s_pad = past + 2 * t
    return pl.pallas_call(
        functools.partial(_attn_sample_kernel, tkb=tkb),
        grid=(batch,),
        in_specs=[pl.BlockSpec((t, MLA_HEADS * HEAD_PAD), lambda b: (b, 0)),
                  pl.BlockSpec((None, past, KV_LORA), lambda b: (b, 0, 0)),
                  pl.BlockSpec((None, QK_ROPE, past), lambda b: (b, 0, 0)),
                  pl.BlockSpec((2 * t, KV_LORA), lambda b: (b // 2, 0)),
                  pl.BlockSpec((QK_ROPE, 2 * t), lambda b: (0, b // 2)),
                  _const(tab.shape), _const(w["w_ukt"].shape), _const(w["w_uv"].shape),
                  _const((1, LANES)), _const((QK_ROPE, 1))],
        out_specs=pl.BlockSpec((t, MLA_WIDTH), lambda b: (b, 0)),
        out_shape=jax.ShapeDtypeStruct((batch * t, MLA_WIDTH), BF16),
        scratch_shapes=[pltpu.VMEM((MLA_HEADS * (QK_NOPE + t), KV_LORA), BF16),
                        pltpu.VMEM((s_pad, KV_LORA), BF16),
                        pltpu.VMEM((MLA_HEADS * t, s_pad), F32)],
        compiler_params=_params("arbitrary"),
        name="mla_attn_sample",
    )(q, past_lat, past_krt, lat_new, krt_new, tab, w["w_ukt"], w["w_uv"], w["gkn"], w["gkr_col"])


def _gla_kernel(q_ref, k_ref, v_ref, la_ref, r_ref, s0_ref, g_ref, spread_ref, o_ref, sn_ref,
                st_ref, kp_ref, ap_ref, p_ref, on_ref, *, gpt):
    t_idx = pl.program_id(1)
    L, W, P = CHUNK, GLA_QK, LANES
    R = q_ref.shape[0]
    n_chunks = R // L
    cpg = n_chunks // gpt
    n_pairs = GLA_HEADS // 2

    lane_p = lax.broadcasted_iota(jnp.int32, (1, P), 1)
    even = lane_p < GLA_DK
    bd_mask = (lax.broadcasted_iota(jnp.int32, (2 * GLA_DV, P), 0) // GLA_DV
               == lax.broadcasted_iota(jnp.int32, (2 * GLA_DV, P), 1) // GLA_DK)

    @pl.when(t_idx == 0)
    def _():
        kp_ref[0:SUB, :] = jnp.zeros((SUB, W), F32)
        ap_ref[0:SUB, :] = jnp.zeros((SUB, W), F32)
        for gi in range(gpt):
            for pr in range(n_pairs):
                tt = s0_ref[gi, 2 * pr:2 * pr + 2].reshape(2 * GLA_DK, GLA_DV).T
                st_ref[gi, pr] = jnp.where(bd_mask, jnp.concatenate([tt, tt], axis=0), 0.0)

    q = q_ref[...]
    k = k_ref[...]
    la = la_ref[...]

    tri = (lax.broadcasted_iota(jnp.int32, (L, L), 0) >= lax.broadcasted_iota(jnp.int32, (L, L), 1)).astype(BF16)
    la_hi = la.astype(BF16)
    la2 = jnp.concatenate([la_hi, (la - la_hi.astype(F32)).astype(BF16)], axis=1)
    bs = []
    for c in range(n_chunks):
        t2 = _dot(tri, la2[c * L:(c + 1) * L, :])
        bs.append(t2[:, 0:W] + t2[:, W:2 * W])
    b = (jnp.concatenate(bs, axis=0) if n_chunks > 1 else bs[0]) * LOG2E
    b3 = b.reshape(n_chunks, L, W)

    def chunk_row(r):
        return jnp.broadcast_to(b3[:, r:r + 1, :], (n_chunks, L, W)).reshape(R, W)

    b_sub = jnp.broadcast_to(b.reshape(R // SUB, SUB, W)[:, 0:1, :], (R // SUB, SUB, W)).reshape(R, W)
    sub = (lax.broadcasted_iota(jnp.int32, (R, W), 0) % L) // SUB

    qt = q * jnp.exp2(b - b_sub)
    zb = jnp.zeros((), BF16)
    zq = jnp.zeros((SUB, P), F32)
    ktm = []
    for i in range(1, L // SUB):
        kt = (k * jnp.exp2(chunk_row(i * SUB) - b)).astype(BF16)
        ktm.append([[jnp.where((sub[:, 0:P] < i) & (even if e == 0 else ~even), kt[:, pr * P:(pr + 1) * P], zb)
                     for e in range(2)] for pr in range(n_pairs)])

    a = jnp.exp(la)
    kp_ref[SUB:SUB + R, :] = k
    ap_ref[SUB:SUB + R, :] = a
    p_ref[:, 0:W] = (q * k).astype(BF16)
    e = a
    for d in range(1, SUB):
        if d > 1:
            e = e * ap_ref[SUB - d + 1:SUB - d + 1 + R, :]
        p_ref[:, d * W:(d + 1) * W] = (q * kp_ref[SUB - d:SUB - d + R, :] * e).astype(BF16)
    cband = _dot(p_ref[...], spread_ref[...])
    same_sub = (lax.broadcasted_iota(jnp.int32, (L, W), 0) // SUB
                == (lax.broadcasted_iota(jnp.int32, (L, W), 1) % L) // SUB)

    qe = (q * jnp.exp2(b)).astype(BF16)
    kd = (k * jnp.exp2(chunk_row(L - 1) - b)).astype(BF16)
    zv = jnp.zeros((L, 2 * GLA_DV), BF16)

    o_intra, d_st, dec = {}, {}, []
    for c in range(n_chunks):
        rs = slice(c * L, (c + 1) * L)
        a_band = jnp.where(same_sub, pltpu.roll(cband[rs], W - (SUB - 1), 1, stride=1, stride_axis=0), 0.0)
        dec.append(jnp.exp2(b[c * L + L - 1:c * L + L, :]))
        for pr in range(n_pairs):
            ls = slice(pr * P, (pr + 1) * P)
            lhs_c = jnp.concatenate(
                [jnp.concatenate([qt[c * L + r * SUB:c * L + (r + 1) * SUB, ls] if i == r else zq
                                  for i in range(1, L // SUB)], axis=1) for r in range(L // SUB)],
                axis=0).astype(BF16)
            rhs_c = jnp.concatenate([jnp.concatenate([m[pr][0][rs], m[pr][1][rs]], axis=0) for m in ktm], axis=1)
            a_tot = (a_band[:, ls] + _dot_nt(lhs_c, rhs_c)).astype(BF16)
            vp = v_ref[rs, 2 * pr * GLA_DV:(2 * pr + 2) * GLA_DV]
            v_bd = jnp.concatenate([jnp.concatenate([vp[:, 0:GLA_DV], zv[:, 0:GLA_DV]], axis=1),
                                    jnp.concatenate([zv[:, 0:GLA_DV], vp[:, GLA_DV:]], axis=1)], axis=0)
            o_intra[c, pr] = _dot(a_tot, v_bd)
            d_st[c, pr] = jnp.where(bd_mask, _dot_tn(vp, kd[rs, ls]), 0.0)
    st_in = {}
    for gi in range(gpt):
        for pr in range(n_pairs):
            st = st_ref[gi, pr]
            for c in range(gi * cpg, (gi + 1) * cpg):
                st_in[c, pr] = st.astype(BF16)
                st = st * dec[c][:, pr * P:(pr + 1) * P] + d_st[c, pr]
            st_ref[gi, pr] = st
    for c in range(n_chunks):
        rs = slice(c * L, (c + 1) * L)
        for pr in range(n_pairs):
            on_ref[rs, 2 * pr * GLA_DV:(2 * pr + 2) * GLA_DV] = (
                o_intra[c, pr] + _dot_nt(qe[rs, pr * P:(pr + 1) * P], st_in[c, pr]))

    for hh in range(GLA_HEADS):
        hs = slice(hh * GLA_DV, (hh + 1) * GLA_DV)
        o = on_ref[:, hs]
        on = o * lax.rsqrt(jnp.mean(o * o, axis=-1, keepdims=True) + EPS) * g_ref[:, hs]
        o_ref[:, hs] = (on * r_ref[:, hs].astype(F32)).astype(BF16)

    @pl.when(t_idx == pl.num_programs(1) - 1)
    def _():
        for gi in range(gpt):
            for pr in range(n_pairs):
                st = st_ref[gi, pr]
                tt = jnp.where(even, st[0:GLA_DV], st[GLA_DV:2 * GLA_DV])
                sn_ref[gi, 2 * pr:2 * pr + 2] = tt.T.reshape(2, GLA_DK, GLA_DV)


def _band_spread():
    m = np.zeros((SUB, GLA_HEADS, GLA_DK, GLA_QK), np.float32)
    for d in range(SUB):
        for h in range(GLA_HEADS):
            m[d, h, :, h * GLA_DK + SUB - 1 - d] = 1.0
    return jnp.asarray(m.reshape(SUB * GLA_QK, GLA_QK), BF16)


def _gla(gq, gk, gv, la, gr, s0, g_out, *, groups, rows_per_group, tc):
    gpt = max(1, tc // rows_per_group)
    nt = max(1, rows_per_group // tc)

    def rows(c):
        return pl.BlockSpec((tc, c), lambda g, t: (g * nt + t, 0))

    state = pl.BlockSpec((gpt, GLA_HEADS, GLA_DK, GLA_DV), lambda g, t: (g, 0, 0, 0))
    spread = _band_spread()
    return pl.pallas_call(
        functools.partial(_gla_kernel, gpt=gpt),
        grid=(groups // gpt, nt),
        in_specs=[rows(GLA_QK), rows(GLA_QK), rows(GLA_WIDTH), rows(GLA_QK), rows(GLA_WIDTH), state,
                  _const((1, GLA_WIDTH)), _const(spread.shape)],
        out_specs=[rows(GLA_WIDTH), state],
        out_shape=[jax.ShapeDtypeStruct((groups * rows_per_group, GLA_WIDTH), BF16),
                   jax.ShapeDtypeStruct((groups, GLA_HEADS, GLA_DK, GLA_DV), F32)],
        scratch_shapes=[pltpu.VMEM((gpt, GLA_HEADS // 2, 2 * GLA_DV, LANES), F32),
                        pltpu.VMEM((SUB + tc, GLA_QK), F32), pltpu.VMEM((SUB + tc, GLA_QK), F32),
                        pltpu.VMEM((tc, SUB * GLA_QK), BF16), pltpu.VMEM((tc, GLA_WIDTH), F32)],
        compiler_params=_params("arbitrary", "arbitrary"),
        name="gla",
    )(gq, gk, gv, la, gr, s0, g_out, spread)


def _mlp_kernel(x_ref, a_ref, b_ref, g1_ref, sh2_ref, sc2_ref, g2_ref, gn_ref, wo_ref, wu_ref, wd_ref, y_ref,
                *, gpt, tf):
    tm, d = x_ref.shape

    def per_group(val, ref, scale_plus_one=False):
        m = ref[...]
        if scale_plus_one:
            m = 1.0 + m
        return (val.reshape(gpt, tm // gpt, d) * m).reshape(tm, d)

    mix = jnp.concatenate([a_ref[...], b_ref[...]], axis=1)
    x1 = x_ref[...] + per_group(_dot(mix, wo_ref[...]), g1_ref)
    xn = x1 * lax.rsqrt(jnp.mean(x1 * x1, axis=-1, keepdims=True) + EPS) * gn_ref[...]
    h2 = (per_group(xn, sc2_ref, True).reshape(gpt, tm // gpt, d) + sh2_ref[...]).reshape(tm, d).astype(BF16)
    acc = jnp.zeros((tm, d), F32)
    for j in range(wu_ref.shape[1] // tf):
        u = jnp.maximum(_dot(h2, wu_ref[:, j * tf:(j + 1) * tf]), 0.0)
        acc += _dot((u * u).astype(BF16), wd_ref[j * tf:(j + 1) * tf, :])
    y_ref[...] = x1 + per_group(acc, g2_ref)


def _mlp(x2, a_out, b_out, mod4, w, *, rows_per_group, tm):
    n, d = x2.shape
    gpt = max(1, tm // rows_per_group)
    tpg = max(1, rows_per_group // tm)

    def mod_spec(j):
        return pl.BlockSpec((gpt, None, 1, d), lambda i: ((i // tpg) if gpt == 1 else i, j, 0, 0))

    def rows(c):
        return pl.BlockSpec((tm, c), lambda i: (i, 0))

    return pl.pallas_call(
        functools.partial(_mlp_kernel, gpt=gpt, tf=1024),
        grid=(n // tm,),
        in_specs=[rows(d), rows(MLA_WIDTH), rows(GLA_WIDTH), mod_spec(2), mod_spec(3), mod_spec(4), mod_spec(5),
                  _const((1, d)), _const(w["w_out"].shape), _const(w["w_up"].shape), _const(w["w_down"].shape)],
        out_specs=rows(d),
        out_shape=jax.ShapeDtypeStruct((n, d), F32),
        compiler_params=_params("arbitrary"),
        name="out_proj_mlp",
    )(x2, a_out, b_out, mod4, mod4, mod4, mod4, w["g_norm2"], w["w_out"], w["w_up"], w["w_down"])


def _rope_table(start, count, repeat=1):
    half = QK_ROPE // 2
    inv = ROPE_THETA ** (-np.arange(half, dtype=np.float64) / half)
    ang = (start + np.arange(count, dtype=np.float64))[:, None] * inv[None, :]
    c, s, z = np.cos(ang), np.sin(ang), np.zeros_like(ang)
    tab = np.concatenate([c, c, z, z, -s, z, z, z, z, s, z, z], axis=1).astype(np.float32)
    return jnp.asarray(np.tile(tab, (repeat, 1)))


def _pad_gain(g_rope):
    return jnp.concatenate([g_rope, jnp.zeros((LANES - QK_ROPE,), F32)]).reshape(1, LANES)


def _prep_weights(w_in, g_norm1, g_q_lora, w_uq, g_kv_lora, w_ukv, g_q_head, g_k_head,
                  w_gate_up, b_gate_up, g_gla_out, w_out, g_norm2, w_up, w_down):
    d = w_in.shape[0]
    s = np.cumsum([0, Q_LORA, KV_LORA, QK_ROPE, GLA_QK, GLA_QK, GLA_WIDTH, GLA_GATE_RANK, GLA_WIDTH])
    piece = [w_in[:, s[i]:s[i + 1]] for i in range(8)]
    pad = jnp.zeros((d, LANES - QK_ROPE - GLA_GATE_RANK), w_in.dtype)
    w_in_p = jnp.concatenate([piece[0], piece[2], piece[6], pad, piece[1], piece[3], piece[4], piece[5],
                              piece[7]], axis=1).astype(BF16)
    uq = w_uq.reshape(Q_LORA, MLA_HEADS, QK_DIM)
    uq = jnp.concatenate([uq, jnp.zeros((Q_LORA, MLA_HEADS, HEAD_PAD - QK_DIM), w_uq.dtype)], axis=2)
    ukv = w_ukv.reshape(KV_LORA, MLA_HEADS, QK_NOPE + V_DIM)
    ukv = jnp.concatenate([ukv[:, :, :QK_NOPE].reshape(KV_LORA, -1), ukv[:, :, QK_NOPE:].reshape(KV_LORA, -1)], axis=1)
    w_gate = jnp.zeros((LANES, GLA_QK), w_gate_up.dtype).at[QK_ROPE:QK_ROPE + GLA_GATE_RANK].set(w_gate_up)
    bound = 1.02 * QK_DIM ** 0.5 * LOG2E * jnp.max(jnp.abs(g_q_head)) * jnp.max(jnp.abs(g_k_head))
    lane = jnp.arange(LANES) == QK_ROPE
    return {
        "w_in": w_in_p, "g_norm1": g_norm1.reshape(1, d), "g_q_lora": g_q_lora.reshape(1, Q_LORA),
        "w_uq": uq.reshape(Q_LORA, MLA_HEADS * HEAD_PAD).astype(BF16),
        "g_kv_lora": g_kv_lora.reshape(1, KV_LORA), "w_ukv": ukv.astype(BF16),
        "w_ukt": ukv[:, :MLA_HEADS * QK_NOPE].T.astype(BF16), "w_uv": ukv[:, MLA_HEADS * QK_NOPE:].astype(BF16),
        "gkr_col": g_k_head[QK_NOPE:].reshape(QK_ROPE, 1),
        "qone": lane.astype(F32).reshape(1, LANES), "kbias": jnp.where(lane, -bound, 0.0).reshape(1, LANES),
        "fast_softmax": (bound <= MAX_FIXED_SHIFT).astype(jnp.int32).reshape(1),
        "gqn": g_q_head[:QK_NOPE].reshape(1, LANES), "gqr": _pad_gain(g_q_head[QK_NOPE:]),
        "gkn": g_k_head[:QK_NOPE].reshape(1, LANES), "gkr": _pad_gain(g_k_head[QK_NOPE:]),
        "w_gate": w_gate.astype(BF16), "b_gate": b_gate_up.reshape(1, GLA_QK),
        "g_gla_out": g_gla_out.reshape(1, GLA_WIDTH), "w_out": w_out.astype(BF16),
        "g_norm2": g_norm2.reshape(1, d), "w_up": w_up.astype(BF16), "w_down": w_down.astype(BF16),
    }


def _layer(x, mod, past_lat, past_kr, s0, w, *, tm):
    batch, seq, d = x.shape
    n = batch * seq
    past = 0 if past_lat is None else past_lat.shape[1]
    x2 = x.reshape(n, d)
    mod4 = mod.reshape(batch, 6, 1, d)
    tm = min(tm, n)
    tab = _rope_table(past, seq, repeat=max(1, tm // seq))
    if past == 0:
        lat, krt, q, k, v, gq, gk, gv, la, gr = _projection(x2, mod4, tab, w, rows_per_group=seq, tm=tm, prompt=True)
        kr = jnp.swapaxes(krt, 1, 2)
        a_out = _attention_prompt(w["fast_softmax"], q, k, v, batch=batch, seq=seq, tq=min(512, seq))
    else:
        assert seq == CHUNK and past % CHUNK == 0
        lat, kr, krt, q, gq, gk, gv, la, gr = _projection(x2, mod4, tab, w, rows_per_group=seq, tm=tm, prompt=False)
        a_out = _attention_sample(q, past_lat, jnp.swapaxes(past_kr, 1, 2), lat, krt, w, tkb=min(512, past))
    b_out, s_new = _gla(gq, gk, gv, la, gr, s0, w["g_gla_out"], groups=batch, rows_per_group=seq,
                        tc=tm)
    y = _mlp(x2, a_out, b_out, mod4, w, rows_per_group=seq, tm=tm)
    return (y.reshape(batch, seq, d), lat.reshape(batch, seq, KV_LORA), kr.reshape(batch, seq, QK_ROPE), s_new)


def kernel(x_prompt, x_sample, cache_mla_latent, cache_mla_krope, state_gla, c_prompt, c_sample,
           w_ada, b_ada, g_norm1, w_in, g_q_lora, w_uq, g_kv_lora, w_ukv, g_q_head, g_k_head,
           w_gate_up, b_gate_up, g_gla_out, w_out, g_norm2, w_up, w_down):
    nb = x_prompt.shape[0]
    depth = w_ada.shape[0]
    y_p, y_s = x_prompt, x_sample
    outs = [[] for _ in range(6)]
    c_all = jnp.concatenate([c_prompt, c_sample], axis=0)
    for l in range(depth):
        w = _prep_weights(w_in[l], g_norm1[l], g_q_lora[l], w_uq[l], g_kv_lora[l], w_ukv[l], g_q_head[l],
                          g_k_head[l], w_gate_up[l], b_gate_up[l], g_gla_out[l], w_out[l], g_norm2[l],
                          w_up[l], w_down[l])
        mod = _modulation(c_all, w_ada[l], b_ada[l])
        zero_state = jnp.zeros((nb, GLA_HEADS, GLA_DK, GLA_DV), x_prompt.dtype)
        y_p, lat, kr, st = _layer(y_p, mod[:nb], None, None, zero_state, w, tm=512)
        outs[0].append(lat); outs[1].append(kr); outs[2].append(st)
        y_s, lat, kr, st = _layer(y_s, mod[nb:], cache_mla_latent[l], cache_mla_krope[l], state_gla[l], w, tm=512)
        outs[3].append(lat); outs[4].append(kr); outs[5].append(st)
    return (y_p, y_s) + tuple(jnp.stack(o) for o in outs)
```

```python
import functools

import jax
import jax.numpy as jnp
import numpy as np
from jax import lax
from jax.experimental import pallas as pl
from jax.experimental.pallas import tpu as pltpu

F32 = jnp.float32
BF16 = jnp.bfloat16

CHUNK = 64
EPS = 1e-6
MLA_HEADS = 4
Q_LORA = 384
KV_LORA = 256
QK_NOPE = 128
QK_ROPE = 64
QK_DIM = QK_NOPE + QK_ROPE
V_DIM = 128
ROPE_THETA = 10000.0
GLA_HEADS = 4
GLA_DK = 64
GLA_DV = 128
GLA_GATE_RANK = 16
GLA_TAU = 16.0
GLA_QK = GLA_HEADS * GLA_DK
GLA_WIDTH = GLA_HEADS * GLA_DV
MLA_WIDTH = MLA_HEADS * V_DIM
HEAD_PAD = 256
SUB = 8
LOG2E = 1.4426950408889634
MAX_FIXED_SHIFT = 48.0

LANES = 128
VMEM_LIMIT = 56 * 1024 * 1024

C_QKR = (0, 512)
C_KV = (512, 768)
C_GQ = (768, 1024)
C_GK = (1024, 1280)
C_GV = (1280, 1792)
C_GR = (1792, 2304)


def _dot(a, b):
    return jnp.dot(a, b, preferred_element_type=F32)


def _dot_nt(a, b):
    return lax.dot_general(a, b, (((1,), (1,)), ((), ())), preferred_element_type=F32)


def _dot_tn(a, b):
    return lax.dot_general(a, b, (((0,), (0,)), ((), ())), preferred_element_type=F32)


def _rope_tile(t, c, sa, sb):
    return t * c + pltpu.roll(t, 96, 1) * sa + pltpu.roll(t, 32, 1) * sb


def _params(*sem):
    return pltpu.CompilerParams(dimension_semantics=sem, vmem_limit_bytes=VMEM_LIMIT)


def _const(shape):
    return pl.BlockSpec(shape, lambda *_: (0,) * len(shape), pipeline_mode=pl.Buffered(1))


def _mod_kernel(c_ref, w_ref, b_ref, o_ref):
    c = c_ref[...]
    s = (c * jax.nn.sigmoid(c)).astype(BF16)
    o_ref[...] = _dot(s, w_ref[...].astype(BF16)) + b_ref[...]


def _modulation(c_all, w_ada, b_ada):
    g, d = c_all.shape
    n = w_ada.shape[1]
    tn = 1024
    return pl.pallas_call(
        _mod_kernel,
        grid=(n // tn,),
        in_specs=[pl.BlockSpec((g, d), lambda j: (0, 0)),
                  pl.BlockSpec((d, tn), lambda j: (0, j)),
                  pl.BlockSpec((1, tn), lambda j: (0, j))],
        out_specs=pl.BlockSpec((g, tn), lambda j: (0, j)),
        out_shape=jax.ShapeDtypeStruct((g, n), F32),
        compiler_params=_params("arbitrary"),
        name="adaln_mod",
    )(c_all, w_ada, b_ada.reshape(1, n))


def _proj_kernel(x_ref, sh_ref, sc_ref, g1_ref, win_ref, gql_ref, wuq_ref, gkv_ref, wukv_ref,
                 gqn_ref, gqr_ref, gkn_ref, gkr_ref, qone_ref, kbias_ref, tab_ref, wg_ref, bg_ref,
                 *outs, gpt, prompt):
    if prompt:
        lat_ref, krt_ref, q_ref, k_ref, v_ref, gq_ref, gk_ref, gv_ref, la_ref, gr_ref = outs
    else:
        lat_ref, kr_ref, krt_ref, q_ref, gq_ref, gk_ref, gv_ref, la_ref, gr_ref = outs
    tm, d = x_ref.shape
    x = x_ref[...]
    xn = x * lax.rsqrt(jnp.mean(x * x, axis=-1, keepdims=True) + EPS) * g1_ref[...]
    h = (xn.reshape(gpt, tm // gpt, d) * (1.0 + sc_ref[...]) + sh_ref[...]).reshape(tm, d)
    hb = h.astype(BF16)

    def col(c):
        return _dot(hb, win_ref[:, c[0]:c[1]])

    tab = tab_ref[...]
    cos, sa, sb = tab[:, 0:LANES], tab[:, LANES:2 * LANES], tab[:, 2 * LANES:3 * LANES]

    qkr = col(C_QKR)
    cq = qkr[:, 0:Q_LORA]
    cqn = cq * lax.rsqrt(jnp.mean(cq * cq, axis=-1, keepdims=True) + EPS) * gql_ref[...]
    qp = _dot(cqn.astype(BF16), wuq_ref[...])
    qscale = QK_DIM ** -0.5 * LOG2E
    for hh in range(MLA_HEADS):
        nope = qp[:, hh * HEAD_PAD:hh * HEAD_PAD + QK_NOPE]
        rt = qp[:, hh * HEAD_PAD + QK_NOPE:(hh + 1) * HEAD_PAD]
        ss = jnp.sum(nope * nope, axis=-1, keepdims=True) + jnp.sum(rt * rt, axis=-1, keepdims=True)
        inv = lax.rsqrt(ss * (1.0 / QK_DIM) + EPS)
        q_ref[:, hh * HEAD_PAD:hh * HEAD_PAD + QK_NOPE] = (nope * inv * gqn_ref[...] * qscale).astype(BF16)
        rq = _rope_tile(rt * inv * gqr_ref[...], cos, sa, sb)
        q_ref[:, hh * HEAD_PAD + QK_NOPE:(hh + 1) * HEAD_PAD] = (rq * qscale + qone_ref[...]).astype(BF16)

    ckv = col(C_KV)
    lat = ckv * lax.rsqrt(jnp.mean(ckv * ckv, axis=-1, keepdims=True) + EPS) * gkv_ref[...]
    lat_ref[...] = lat
    krg = qkr[:, Q_LORA:]
    krt_ref[...] = krg.T[0:QK_ROPE, :]
    if prompt:
        lane = lax.broadcasted_iota(jnp.int32, (1, LANES), 1)
        krm = jnp.where(lane < QK_ROPE, krg, 0.0)
        ssr = jnp.sum(krm * krm, axis=-1, keepdims=True)
        rk = _rope_tile(krm * gkr_ref[...], cos, sa, sb)
        kv = _dot(lat.astype(BF16), wukv_ref[...])
        for hh in range(MLA_HEADS):
            kn = kv[:, hh * QK_NOPE:(hh + 1) * QK_NOPE]
            inv = lax.rsqrt((jnp.sum(kn * kn, axis=-1, keepdims=True) + ssr) * (1.0 / QK_DIM) + EPS)
            k_ref[:, hh * HEAD_PAD:hh * HEAD_PAD + QK_NOPE] = (kn * inv * gkn_ref[...]).astype(BF16)
            k_ref[:, hh * HEAD_PAD + QK_NOPE:(hh + 1) * HEAD_PAD] = (rk * inv + kbias_ref[...]).astype(BF16)
        v_ref[...] = kv[:, MLA_HEADS * QK_NOPE:].astype(BF16)
    else:
        kr_ref[...] = krg[:, 0:QK_ROPE]

    gq_ref[...] = col(C_GQ) * (GLA_DK ** -0.5)
    gk_ref[...] = col(C_GK)
    gv_ref[...] = col(C_GV).astype(BF16)
    r = col(C_GR)
    gr_ref[...] = (r * jax.nn.sigmoid(r)).astype(BF16)
    z = _dot(krg.astype(BF16), wg_ref[...]) + bg_ref[...]
    la_ref[...] = (jnp.minimum(z, 0.0) - jnp.log(1.0 + jnp.exp(-jnp.abs(z)))) * (1.0 / GLA_TAU)


def _projection(x2, mod4, tab, w, *, rows_per_group, tm, prompt):
    n, d = x2.shape
    gpt = max(1, tm // rows_per_group)
    tpg = max(1, rows_per_group // tm)
    ntab = tab.shape[0] // tm

    def mod_spec(j):
        return pl.BlockSpec((gpt, None, 1, d), lambda i: ((i // tpg) if gpt == 1 else i, j, 0, 0))

    def rows(c):
        return pl.BlockSpec((tm, c), lambda i: (i, 0))

    def out(c, t):
        return rows(c), jax.ShapeDtypeStruct((n, c), t)

    gla_outs = [out(GLA_QK, F32), out(GLA_QK, F32), out(GLA_WIDTH, BF16), out(GLA_QK, F32), out(GLA_WIDTH, BF16)]
    qo = out(MLA_HEADS * HEAD_PAD, BF16)
    if prompt:
        krt = (pl.BlockSpec((None, QK_ROPE, tm), lambda i: (i // tpg, 0, i % tpg)),
               jax.ShapeDtypeStruct((n // rows_per_group, QK_ROPE, rows_per_group), F32))
        outs = [out(KV_LORA, F32), krt, qo, out(MLA_HEADS * HEAD_PAD, BF16), out(MLA_WIDTH, BF16)] + gla_outs
    else:
        krt = (pl.BlockSpec((QK_ROPE, tm), lambda i: (0, i)), jax.ShapeDtypeStruct((QK_ROPE, n), F32))
        outs = [out(KV_LORA, F32), out(QK_ROPE, F32), krt, qo] + gla_outs
    return pl.pallas_call(
        functools.partial(_proj_kernel, gpt=gpt, prompt=prompt),
        grid=(n // tm,),
        in_specs=[rows(d), mod_spec(0), mod_spec(1), _const((1, d)), _const(w["w_in"].shape),
                  _const((1, Q_LORA)), _const(w["w_uq"].shape), _const((1, KV_LORA)), _const(w["w_ukv"].shape),
                  _const((1, LANES)), _const((1, LANES)), _const((1, LANES)), _const((1, LANES)),
                  _const((1, LANES)), _const((1, LANES)),
                  pl.BlockSpec((tm, 3 * LANES), lambda i: (i % ntab, 0)),
                  _const(w["w_gate"].shape), _const((1, GLA_QK))],
        out_specs=[o[0] for o in outs],
        out_shape=[o[1] for o in outs],
        compiler_params=_params("arbitrary"),
        name="in_proj",
    )(x2, mod4, mod4, w["g_norm1"], w["w_in"], w["g_q_lora"], w["w_uq"], w["g_kv_lora"], w["w_ukv"],
      w["gqn"], w["gqr"], w["gkn"], w["gkr"], w["qone"], w["kbias"], tab, w["w_gate"], w["b_gate"])


def _attn_prompt_kernel(fast_ref, q_ref, k_ref, v_ref, o_ref, vx_ref, m_ref, acc_ref, *, tq):
    i = pl.program_id(1)
    heads = range(MLA_HEADS)

    @pl.when(i == 0)
    def _():
        for hh in heads:
            vx_ref[hh, :, 0:V_DIM] = v_ref[:, hh * V_DIM:(hh + 1) * V_DIM]
            vx_ref[hh, :, V_DIM:] = jnp.ones((v_ref.shape[0], V_DIM), BF16)

    acc_ref[...] = jnp.zeros(acc_ref.shape, F32)

    def scores(j, hh, masked):
        hs = slice(hh * HEAD_PAD, (hh + 1) * HEAD_PAD)
        s = _dot_nt(q_ref[:, hs], k_ref[pl.ds(pl.multiple_of(j * tq, tq), tq), hs])
        if masked:
            qc = lax.broadcasted_iota(jnp.int32, (tq, tq), 0) // CHUNK
            kc = lax.broadcasted_iota(jnp.int32, (tq, tq), 1) // CHUNK
            s = jnp.where(kc <= qc, s, -jnp.inf)
        return s

    def vblock(j, hh):
        return vx_ref[hh, pl.ds(pl.multiple_of(j * tq, tq), tq), :]

    def fast_block(j, masked):
        for hh in heads:
            acc_ref[hh] += _dot(jnp.exp2(scores(j, hh, masked)).astype(BF16), vblock(j, hh))

    def safe_block(j, masked):
        for hh in heads:
            s = scores(j, hh, masked)
            m = m_ref[hh]
            m_new = jnp.maximum(m, jnp.max(s, axis=1, keepdims=True))
            p = jnp.exp2(s - m_new).astype(BF16)
            acc_ref[hh] = jnp.exp2(m - m_new) * acc_ref[hh] + _dot(p, vblock(j, hh))
            m_ref[hh] = m_new

    def sweep(block):
        def body(j, c):
            block(j, False)
            return c

        lax.fori_loop(0, i, body, 0)
        block(i, True)

    @pl.when(fast_ref[0] == 1)
    def _():
        sweep(fast_block)

    @pl.when(fast_ref[0] != 1)
    def _():
        m_ref[...] = jnp.full(m_ref.shape, -jnp.inf, F32)
        sweep(safe_block)

    for hh in heads:
        acc = acc_ref[hh]
        o_ref[:, hh * V_DIM:(hh + 1) * V_DIM] = (acc[:, 0:V_DIM] / acc[:, V_DIM:]).astype(BF16)


def _attention_prompt(fast, q, k, v, *, batch, seq, tq):
    nq = seq // tq
    return pl.pallas_call(
        functools.partial(_attn_prompt_kernel, tq=tq),
        grid=(batch, nq),
        in_specs=[pl.BlockSpec(memory_space=pltpu.SMEM),
                  pl.BlockSpec((tq, MLA_HEADS * HEAD_PAD), lambda b, i: (b * nq + i, 0)),
                  pl.BlockSpec((seq, MLA_HEADS * HEAD_PAD), lambda b, i: (b, 0)),
                  pl.BlockSpec((seq, MLA_WIDTH), lambda b, i: (b, 0))],
        out_specs=pl.BlockSpec((tq, MLA_WIDTH), lambda b, i: (b * nq + i, 0)),
        out_shape=jax.ShapeDtypeStruct((batch * seq, MLA_WIDTH), BF16),
        scratch_shapes=[pltpu.VMEM((MLA_HEADS, seq, 2 * V_DIM), BF16), pltpu.VMEM((MLA_HEADS, tq, 1), F32),
                        pltpu.VMEM((MLA_HEADS, tq, 2 * V_DIM), F32)],
        compiler_params=_params("arbitrary", "arbitrary"),
        name="mla_attn_prompt",
    )(fast, q, k, v)


def _attn_sample_kernel(fast_ref, bound_ref, q_ref, lat_ref, krt_ref, latn_ref, krtn_ref, tab_ref, wukt_ref, wuv_ref,
                        gkn_ref, gkr_ref, o_ref, wq_ref, latb_ref, s_ref, *, tkb):
    b = pl.program_id(0)
    t = q_ref.shape[0]
    past = lat_ref.shape[0]
    nk = MLA_HEADS * QK_NOPE
    half = QK_ROPE // 2

    @pl.when(b == 0)
    def _():
        wq_ref[0:nk, :] = wukt_ref[...]

    qr = []
    for hh in range(MLA_HEADS):
        qn = (q_ref[:, hh * HEAD_PAD:hh * HEAD_PAD + QK_NOPE].astype(F32) * gkn_ref[...]).astype(BF16)
        wq_ref[nk + hh * t:nk + (hh + 1) * t, :] = _dot(qn, wukt_ref[hh * QK_NOPE:(hh + 1) * QK_NOPE, :]).astype(BF16)
        qr.append(q_ref[:, hh * HEAD_PAD + QK_NOPE:(hh + 1) * HEAD_PAD])
    qr = jnp.concatenate(qr, axis=0)

    def key_block(lat, krt, tab, width, valid):
        latb = lat.astype(BF16)
        g = _dot_nt(wq_ref[...], latb)
        ssr = jnp.sum(krt * krt, axis=0, keepdims=True)
        kg = krt * gkr_ref[...]
        x1, x2 = kg[0:half], kg[half:QK_ROPE]
        c, sn = tab[0:half], tab[half:QK_ROPE]
        rope = jnp.concatenate([x1 * c - x2 * sn, x2 * c + x1 * sn, jnp.zeros((LANES - QK_ROPE, width), F32)], axis=0)
        srope = _dot(qr, rope.astype(BF16))
        rows = []
        for hh in range(MLA_HEADS):
            kn = g[hh * QK_NOPE:(hh + 1) * QK_NOPE]
            inv = lax.rsqrt((jnp.sum(kn * kn, axis=0, keepdims=True) + ssr) * (1.0 / QK_DIM) + EPS)
            rows.append((g[nk + hh * t:nk + (hh + 1) * t] + srope[hh * t:(hh + 1) * t]) * inv)
        sc = jnp.concatenate(rows, axis=0) - bound_ref[0]
        if valid is not None:
            sc = jnp.where(valid, sc, -jnp.inf)
        return sc, latb

    mine = lax.broadcasted_iota(jnp.int32, (1, 2 * t), 1) // t == b % 2

    def blocks():
        for blk in range(past // tkb):
            c0 = blk * tkb
            yield c0, tkb, key_block(lat_ref[c0:c0 + tkb, :], krt_ref[:, c0:c0 + tkb], tab_ref[:, c0:c0 + tkb],
                                     tkb, None)
        yield past, 2 * t, key_block(latn_ref[...], krtn_ref[...], tab_ref[:, past:past + 2 * t], 2 * t, mine)

    def finish(wlat, l):
        wlat = wlat.astype(BF16)
        for hh in range(MLA_HEADS):
            o = _dot(wlat[hh * t:(hh + 1) * t], wuv_ref[:, hh * V_DIM:(hh + 1) * V_DIM])
            o_ref[:, hh * V_DIM:(hh + 1) * V_DIM] = (o / l[hh * t:(hh + 1) * t]).astype(BF16)

    @pl.when(fast_ref[0] == 1)
    def _():
        wlat = jnp.zeros((MLA_HEADS * t, KV_LORA), F32)
        lsum = jnp.zeros((MLA_HEADS * t, LANES), F32)
        for _, width, (sc, latb) in blocks():
            p = jnp.exp2(sc)
            wlat += _dot(p.astype(BF16), latb)
            for c in range(width // LANES):
                lsum += p[:, c * LANES:(c + 1) * LANES]
        finish(wlat, jnp.sum(lsum, axis=1, keepdims=True))

    @pl.when(fast_ref[0] != 1)
    def _():
        for c0, width, (sc, latb) in blocks():
            s_ref[:, c0:c0 + width] = sc
            latb_ref[c0:c0 + width, :] = latb
        s = s_ref[...]
        p = jnp.exp2(s - jnp.max(s, axis=1, keepdims=True))
        finish(_dot(p.astype(BF16), latb_ref[...]), jnp.sum(p, axis=1, keepdims=True))


def _attention_sample(q, past_lat, past_krt, lat_new, krt_new, w, *, tkb):
    batch, past, _ = past_lat.shape
    t = q.shape[0] // batch
    half = QK_ROPE // 2
    inv = ROPE_THETA ** (-np.arange(half, dtype=np.float64) / half)
    pos = np.concatenate([np.arange(past), past + np.arange(t), past + np.arange(t)]).astype(np.float64)
    ang = inv[:, None] * pos[None, :]
    tab = jnp.asarray(np.concatenate([np.cos(ang), np.sin(ang)], axis=0).astype(np.float32))
    s_pad = past + 2 * t
    return pl.pallas_call(
        functools.partial(_attn_sample_kernel, tkb=tkb),
        grid=(batch,),
        in_specs=[pl.BlockSpec(memory_space=pltpu.SMEM), pl.BlockSpec(memory_space=pltpu.SMEM),
                  pl.BlockSpec((t, MLA_HEADS * HEAD_PAD), lambda b: (b, 0)),
                  pl.BlockSpec((None, past, KV_LORA), lambda b: (b, 0, 0)),
                  pl.BlockSpec((None, QK_ROPE, past), lambda b: (b, 0, 0)),
                  pl.BlockSpec((2 * t, KV_LORA), lambda b: (b // 2, 0)),
                  pl.BlockSpec((QK_ROPE, 2 * t), lambda b: (0, b // 2)),
                  _const(tab.shape), _const(w["w_ukt"].shape), _const(w["w_uv"].shape),
                  _const((1, LANES)), _const((QK_ROPE, 1))],
        out_specs=pl.BlockSpec((t, MLA_WIDTH), lambda b: (b, 0)),
        out_shape=jax.ShapeDtypeStruct((batch * t, MLA_WIDTH), BF16),
        scratch_shapes=[pltpu.VMEM((MLA_HEADS * (QK_NOPE + t), KV_LORA), BF16),
                        pltpu.VMEM((s_pad, KV_LORA), BF16),
                        pltpu.VMEM((MLA_HEADS * t, s_pad), F32)],
        compiler_params=_params("arbitrary"),
        name="mla_attn_sample",
    )(w["fast_softmax"], w["bound"], q, past_lat, past_krt, lat_new, krt_new, tab, w["w_ukt"], w["w_uv"], w["gkn"],
      w["gkr_col"])


def _gla_kernel(q_ref, k_ref, v_ref, la_ref, r_ref, s0_ref, g_ref, spread_ref, o_ref, sn_ref,
                st_ref, kp_ref, ap_ref, p_ref, on_ref, *, gpt):
    t_idx = pl.program_id(1)
    L, W, P = CHUNK, GLA_QK, LANES
    R = q_ref.shape[0]
    n_chunks = R // L
    cpg = n_chunks // gpt
    n_pairs = GLA_HEADS // 2

    lane_p = lax.broadcasted_iota(jnp.int32, (1, P), 1)
    even = lane_p < GLA_DK
    bd_mask = (lax.broadcasted_iota(jnp.int32, (2 * GLA_DV, P), 0) // GLA_DV
               == lax.broadcasted_iota(jnp.int32, (2 * GLA_DV, P), 1) // GLA_DK)

    @pl.when(t_idx == 0)
    def _():
        kp_ref[0:SUB, :] = jnp.zeros((SUB, W), F32)
        ap_ref[0:SUB, :] = jnp.zeros((SUB, W), F32)
        for gi in range(gpt):
            for pr in range(n_pairs):
                tt = s0_ref[gi, 2 * pr:2 * pr + 2].reshape(2 * GLA_DK, GLA_DV).T
                st_ref[gi, pr] = jnp.where(bd_mask, jnp.concatenate([tt, tt], axis=0), 0.0)

    q = q_ref[...]
    k = k_ref[...]
    la = la_ref[...]

    tri = (lax.broadcasted_iota(jnp.int32, (L, L), 0) >= lax.broadcasted_iota(jnp.int32, (L, L), 1)).astype(BF16)
    la_hi = la.astype(BF16)
    la2 = jnp.concatenate([la_hi, (la - la_hi.astype(F32)).astype(BF16)], axis=1)
    bs = []
    for c in range(n_chunks):
        t2 = _dot(tri, la2[c * L:(c + 1) * L, :])
        bs.append(t2[:, 0:W] + t2[:, W:2 * W])
    b = (jnp.concatenate(bs, axis=0) if n_chunks > 1 else bs[0]) * LOG2E
    b3 = b.reshape(n_chunks, L, W)

    def chunk_row(r):
        return jnp.broadcast_to(b3[:, r:r + 1, :], (n_chunks, L, W)).reshape(R, W)

    b_sub = jnp.broadcast_to(b.reshape(R // SUB, SUB, W)[:, 0:1, :], (R // SUB, SUB, W)).reshape(R, W)
    sub = (lax.broadcasted_iota(jnp.int32, (R, W), 0) % L) // SUB

    qt = q * jnp.exp2(b - b_sub)
    zb = jnp.zeros((), BF16)
    zq = jnp.zeros((SUB, P), F32)
    ktm = []
    for i in range(1, L // SUB):
        kt = (k * jnp.exp2(chunk_row(i * SUB) - b)).astype(BF16)
        ktm.append([[jnp.where((sub[:, 0:P] < i) & (even if e == 0 else ~even), kt[:, pr * P:(pr + 1) * P], zb)
                     for e in range(2)] for pr in range(n_pairs)])

    a = jnp.exp(la)
    kp_ref[SUB:SUB + R, :] = k
    ap_ref[SUB:SUB + R, :] = a
    p_ref[:, 0:W] = (q * k).astype(BF16)
    e = a
    for d in range(1, SUB):
        if d > 1:
            e = e * ap_ref[SUB - d + 1:SUB - d + 1 + R, :]
        p_ref[:, d * W:(d + 1) * W] = (q * kp_ref[SUB - d:SUB - d + R, :] * e).astype(BF16)
    cband = _dot(p_ref[...], spread_ref[...])
    same_sub = (lax.broadcasted_iota(jnp.int32, (L, W), 0) // SUB
                == (lax.broadcasted_iota(jnp.int32, (L, W), 1) % L) // SUB)

    qe = (q * jnp.exp2(b)).astype(BF16)
    kd = (k * jnp.exp2(chunk_row(L - 1) - b)).astype(BF16)
    zv = jnp.zeros((L, 2 * GLA_DV), BF16)

    o_intra, d_st, dec = {}, {}, []
    for c in range(n_chunks):
        rs = slice(c * L, (c + 1) * L)
        a_band = jnp.where(same_sub, pltpu.roll(cband[rs], W - (SUB - 1), 1, stride=1, stride_axis=0), 0.0)
        dec.append(jnp.exp2(b[c * L + L - 1:c * L + L, :]))
        for pr in range(n_pairs):
            ls = slice(pr * P, (pr + 1) * P)
            lhs_c = jnp.concatenate(
                [jnp.concatenate([qt[c * L + r * SUB:c * L + (r + 1) * SUB, ls] if i == r else zq
                                  for i in range(1, L // SUB)], axis=1) for r in range(L // SUB)],
                axis=0).astype(BF16)
            rhs_c = jnp.concatenate([jnp.concatenate([m[pr][0][rs], m[pr][1][rs]], axis=0) for m in ktm], axis=1)
            a_tot = (a_band[:, ls] + _dot_nt(lhs_c, rhs_c)).astype(BF16)
            vp = v_ref[rs, 2 * pr * GLA_DV:(2 * pr + 2) * GLA_DV]
            v_bd = jnp.concatenate([jnp.concatenate([vp[:, 0:GLA_DV], zv[:, 0:GLA_DV]], axis=1),
                                    jnp.concatenate([zv[:, 0:GLA_DV], vp[:, GLA_DV:]], axis=1)], axis=0)
            o_intra[c, pr] = _dot(a_tot, v_bd)
            d_st[c, pr] = jnp.where(bd_mask, _dot_tn(vp, kd[rs, ls]), 0.0)
    st_in = {}
    for gi in range(gpt):
        for pr in range(n_pairs):
            st = st_ref[gi, pr]
            for c in range(gi * cpg, (gi + 1) * cpg):
                st_in[c, pr] = st.astype(BF16)
                st = st * dec[c][:, pr * P:(pr + 1) * P] + d_st[c, pr]
            st_ref[gi, pr] = st
    for c in range(n_chunks):
        rs = slice(c * L, (c + 1) * L)
        for pr in range(n_pairs):
            on_ref[rs, 2 * pr * GLA_DV:(2 * pr + 2) * GLA_DV] = (
                o_intra[c, pr] + _dot_nt(qe[rs, pr * P:(pr + 1) * P], st_in[c, pr]))

    for hh in range(GLA_HEADS):
        hs = slice(hh * GLA_DV, (hh + 1) * GLA_DV)
        o = on_ref[:, hs]
        on = o * lax.rsqrt(jnp.mean(o * o, axis=-1, keepdims=True) + EPS) * g_ref[:, hs]
        o_ref[:, hs] = (on * r_ref[:, hs].astype(F32)).astype(BF16)

    @pl.when(t_idx == pl.num_programs(1) - 1)
    def _():
        for gi in range(gpt):
            for pr in range(n_pairs):
                st = st_ref[gi, pr]
                tt = jnp.where(even, st[0:GLA_DV], st[GLA_DV:2 * GLA_DV])
                sn_ref[gi, 2 * pr:2 * pr + 2] = tt.T.reshape(2, GLA_DK, GLA_DV)


def _band_spread():
    m = np.zeros((SUB, GLA_HEADS, GLA_DK, GLA_QK), np.float32)
    for d in range(SUB):
        for h in range(GLA_HEADS):
            m[d, h, :, h * GLA_DK + SUB - 1 - d] = 1.0
    return jnp.asarray(m.reshape(SUB * GLA_QK, GLA_QK), BF16)


def _gla(gq, gk, gv, la, gr, s0, g_out, *, groups, rows_per_group, tc):
    gpt = max(1, tc // rows_per_group)
    nt = max(1, rows_per_group // tc)

    def rows(c):
        return pl.BlockSpec((tc, c), lambda g, t: (g * nt + t, 0))

    state = pl.BlockSpec((gpt, GLA_HEADS, GLA_DK, GLA_DV), lambda g, t: (g, 0, 0, 0))
    spread = _band_spread()
    return pl.pallas_call(
        functools.partial(_gla_kernel, gpt=gpt),
        grid=(groups // gpt, nt),
        in_specs=[rows(GLA_QK), rows(GLA_QK), rows(GLA_WIDTH), rows(GLA_QK), rows(GLA_WIDTH), state,
                  _const((1, GLA_WIDTH)), _const(spread.shape)],
        out_specs=[rows(GLA_WIDTH), state],
        out_shape=[jax.ShapeDtypeStruct((groups * rows_per_group, GLA_WIDTH), BF16),
                   jax.ShapeDtypeStruct((groups, GLA_HEADS, GLA_DK, GLA_DV), F32)],
        scratch_shapes=[pltpu.VMEM((gpt, GLA_HEADS // 2, 2 * GLA_DV, LANES), F32),
                        pltpu.VMEM((SUB + tc, GLA_QK), F32), pltpu.VMEM((SUB + tc, GLA_QK), F32),
                        pltpu.VMEM((tc, SUB * GLA_QK), BF16), pltpu.VMEM((tc, GLA_WIDTH), F32)],
        compiler_params=_params("arbitrary", "arbitrary"),
        name="gla",
    )(gq, gk, gv, la, gr, s0, g_out, spread)


def _mlp_kernel(x_ref, a_ref, b_ref, g1_ref, sh2_ref, sc2_ref, g2_ref, gn_ref, wo_ref, wu_ref, wd_ref, y_ref,
                *, gpt, tf):
    tm, d = x_ref.shape

    def per_group(val, ref, scale_plus_one=False):
        m = ref[...]
        if scale_plus_one:
            m = 1.0 + m
        return (val.reshape(gpt, tm // gpt, d) * m).reshape(tm, d)

    mix = jnp.concatenate([a_ref[...], b_ref[...]], axis=1)
    x1 = x_ref[...] + per_group(_dot(mix, wo_ref[...]), g1_ref)
    xn = x1 * lax.rsqrt(jnp.mean(x1 * x1, axis=-1, keepdims=True) + EPS) * gn_ref[...]
    h2 = (per_group(xn, sc2_ref, True).reshape(gpt, tm // gpt, d) + sh2_ref[...]).reshape(tm, d).astype(BF16)
    acc = jnp.zeros((tm, d), F32)
    for j in range(wu_ref.shape[1] // tf):
        u = jnp.maximum(_dot(h2, wu_ref[:, j * tf:(j + 1) * tf]), 0.0)
        acc += _dot((u * u).astype(BF16), wd_ref[j * tf:(j + 1) * tf, :])
    y_ref[...] = x1 + per_group(acc, g2_ref)


def _mlp(x2, a_out, b_out, mod4, w, *, rows_per_group, tm):
    n, d = x2.shape
    gpt = max(1, tm // rows_per_group)
    tpg = max(1, rows_per_group // tm)

    def mod_spec(j):
        return pl.BlockSpec((gpt, None, 1, d), lambda i: ((i // tpg) if gpt == 1 else i, j, 0, 0))

    def rows(c):
        return pl.BlockSpec((tm, c), lambda i: (i, 0))

    return pl.pallas_call(
        functools.partial(_mlp_kernel, gpt=gpt, tf=1024),
        grid=(n // tm,),
        in_specs=[rows(d), rows(MLA_WIDTH), rows(GLA_WIDTH), mod_spec(2), mod_spec(3), mod_spec(4), mod_spec(5),
                  _const((1, d)), _const(w["w_out"].shape), _const(w["w_up"].shape), _const(w["w_down"].shape)],
        out_specs=rows(d),
        out_shape=jax.ShapeDtypeStruct((n, d), F32),
        compiler_params=_params("arbitrary"),
        name="out_proj_mlp",
    )(x2, a_out, b_out, mod4, mod4, mod4, mod4, w["g_norm2"], w["w_out"], w["w_up"], w["w_down"])


def _rope_table(start, count, repeat=1):
    half = QK_ROPE // 2
    inv = ROPE_THETA ** (-np.arange(half, dtype=np.float64) / half)
    ang = (start + np.arange(count, dtype=np.float64))[:, None] * inv[None, :]
    c, s, z = np.cos(ang), np.sin(ang), np.zeros_like(ang)
    tab = np.concatenate([c, c, z, z, -s, z, z, z, z, s, z, z], axis=1).astype(np.float32)
    return jnp.asarray(np.tile(tab, (repeat, 1)))


def _pad_gain(g_rope):
    return jnp.concatenate([g_rope, jnp.zeros((LANES - QK_ROPE,), F32)]).reshape(1, LANES)


def _prep_weights(w_in, g_norm1, g_q_lora, w_uq, g_kv_lora, w_ukv, g_q_head, g_k_head,
                  w_gate_up, b_gate_up, g_gla_out, w_out, g_norm2, w_up, w_down):
    d = w_in.shape[0]
    s = np.cumsum([0, Q_LORA, KV_LORA, QK_ROPE, GLA_QK, GLA_QK, GLA_WIDTH, GLA_GATE_RANK, GLA_WIDTH])
    piece = [w_in[:, s[i]:s[i + 1]] for i in range(8)]
    pad = jnp.zeros((d, LANES - QK_ROPE - GLA_GATE_RANK), w_in.dtype)
    w_in_p = jnp.concatenate([piece[0], piece[2], piece[6], pad, piece[1], piece[3], piece[4], piece[5],
                              piece[7]], axis=1).astype(BF16)
    uq = w_uq.reshape(Q_LORA, MLA_HEADS, QK_DIM)
    uq = jnp.concatenate([uq, jnp.zeros((Q_LORA, MLA_HEADS, HEAD_PAD - QK_DIM), w_uq.dtype)], axis=2)
    ukv = w_ukv.reshape(KV_LORA, MLA_HEADS, QK_NOPE + V_DIM)
    ukv = jnp.concatenate([ukv[:, :, :QK_NOPE].reshape(KV_LORA, -1), ukv[:, :, QK_NOPE:].reshape(KV_LORA, -1)], axis=1)
    w_gate = jnp.zeros((LANES, GLA_QK), w_gate_up.dtype).at[QK_ROPE:QK_ROPE + GLA_GATE_RANK].set(w_gate_up)
    bound = 1.02 * QK_DIM ** 0.5 * LOG2E * jnp.max(jnp.abs(g_q_head)) * jnp.max(jnp.abs(g_k_head))
    lane = jnp.arange(LANES) == QK_ROPE
    return {
        "w_in": w_in_p, "g_norm1": g_norm1.reshape(1, d), "g_q_lora": g_q_lora.reshape(1, Q_LORA),
        "w_uq": uq.reshape(Q_LORA, MLA_HEADS * HEAD_PAD).astype(BF16),
        "g_kv_lora": g_kv_lora.reshape(1, KV_LORA), "w_ukv": ukv.astype(BF16),
        "w_ukt": ukv[:, :MLA_HEADS * QK_NOPE].T.astype(BF16), "w_uv": ukv[:, MLA_HEADS * QK_NOPE:].astype(BF16),
        "gkr_col": g_k_head[QK_NOPE:].reshape(QK_ROPE, 1),
        "qone": lane.astype(F32).reshape(1, LANES), "kbias": jnp.where(lane, -bound, 0.0).reshape(1, LANES),
        "fast_softmax": (bound <= MAX_FIXED_SHIFT).astype(jnp.int32).reshape(1), "bound": bound.reshape(1),
        "gqn": g_q_head[:QK_NOPE].reshape(1, LANES), "gqr": _pad_gain(g_q_head[QK_NOPE:]),
        "gkn": g_k_head[:QK_NOPE].reshape(1, LANES), "gkr": _pad_gain(g_k_head[QK_NOPE:]),
        "w_gate": w_gate.astype(BF16), "b_gate": b_gate_up.reshape(1, GLA_QK),
        "g_gla_out": g_gla_out.reshape(1, GLA_WIDTH), "w_out": w_out.astype(BF16),
        "g_norm2": g_norm2.reshape(1, d), "w_up": w_up.astype(BF16), "w_down": w_down.astype(BF16),
    }


def _layer(x, mod, past_lat, past_kr, s0, w, *, tm):
    batch, seq, d = x.shape
    n = batch * seq
    past = 0 if past_lat is None else past_lat.shape[1]
    x2 = x.reshape(n, d)
    mod4 = mod.reshape(batch, 6, 1, d)
    tm = min(tm, n)
    tab = _rope_table(past, seq, repeat=max(1, tm // seq))
    if past == 0:
        lat, krt, q, k, v, gq, gk, gv, la, gr = _projection(x2, mod4, tab, w, rows_per_group=seq, tm=tm, prompt=True)
        kr = jnp.swapaxes(krt, 1, 2)
        a_out = _attention_prompt(w["fast_softmax"], q, k, v, batch=batch, seq=seq, tq=min(512, seq))
    else:
        assert seq == CHUNK and past % CHUNK == 0
        lat, kr, krt, q, gq, gk, gv, la, gr = _projection(x2, mod4, tab, w, rows_per_group=seq, tm=tm, prompt=False)
        a_out = _attention_sample(q, past_lat, jnp.swapaxes(past_kr, 1, 2), lat, krt, w, tkb=min(512, past))
    b_out, s_new = _gla(gq, gk, gv, la, gr, s0, w["g_gla_out"], groups=batch, rows_per_group=seq,
                        tc=tm)
    y = _mlp(x2, a_out, b_out, mod4, w, rows_per_group=seq, tm=tm)
    return (y.reshape(batch, seq, d), lat.reshape(batch, seq, KV_LORA), kr.reshape(batch, seq, QK_ROPE), s_new)


def kernel(x_prompt, x_sample, cache_mla_latent, cache_mla_krope, state_gla, c_prompt, c_sample,
           w_ada, b_ada, g_norm1, w_in, g_q_lora, w_uq, g_kv_lora, w_ukv, g_q_head, g_k_head,
           w_gate_up, b_gate_up, g_gla_out, w_out, g_norm2, w_up, w_down):
    nb = x_prompt.shape[0]
    depth = w_ada.shape[0]
    y_p, y_s = x_prompt, x_sample
    outs = [[] for _ in range(6)]
    c_all = jnp.concatenate([c_prompt, c_sample], axis=0)
    for l in range(depth):
        w = _prep_weights(w_in[l], g_norm1[l], g_q_lora[l], w_uq[l], g_kv_lora[l], w_ukv[l], g_q_head[l],
                          g_k_head[l], w_gate_up[l], b_gate_up[l], g_gla_out[l], w_out[l], g_norm2[l],
                          w_up[l], w_down[l])
        mod = _modulation(c_all, w_ada[l], b_ada[l])
        zero_state = jnp.zeros((nb, GLA_HEADS, GLA_DK, GLA_DV), x_prompt.dtype)
        y_p, lat, kr, st = _layer(y_p, mod[:nb], None, None, zero_state, w, tm=512)
        outs[0].append(lat); outs[1].append(kr); outs[2].append(st)
        y_s, lat, kr, st = _layer(y_s, mod[nb:], cache_mla_latent[l], cache_mla_krope[l], state_gla[l], w, tm=512)
        outs[3].append(lat); outs[4].append(kr); outs[5].append(st)
    return (y_p, y_s) + tuple(jnp.stack(o) for o in outs)
```

```python
import functools

import jax
import jax.numpy as jnp
import numpy as np
from jax import lax
from jax.experimental import pallas as pl
from jax.experimental.pallas import tpu as pltpu

F32 = jnp.float32
BF16 = jnp.bfloat16

CHUNK = 64
EPS = 1e-6
MLA_HEADS = 4
Q_LORA = 384
KV_LORA = 256
QK_NOPE = 128
QK_ROPE = 64
QK_DIM = QK_NOPE + QK_ROPE
V_DIM = 128
ROPE_THETA = 10000.0
GLA_HEADS = 4
GLA_DK = 64
GLA_DV = 128
GLA_GATE_RANK = 16
GLA_TAU = 16.0
GLA_QK = GLA_HEADS * GLA_DK
GLA_WIDTH = GLA_HEADS * GLA_DV
MLA_WIDTH = MLA_HEADS * V_DIM
HEAD_PAD = 256
SUB = 8
LOG2E = 1.4426950408889634
MAX_FIXED_SHIFT = 48.0

LANES = 128
VMEM_LIMIT = 56 * 1024 * 1024

C_QKR = (0, 512)
C_KV = (512, 768)
C_GQ = (768, 1024)
C_GK = (1024, 1280)
C_GV = (1280, 1792)
C_GR = (1792, 2304)


def _dot(a, b):
    return jnp.dot(a, b, preferred_element_type=F32)


def _dot_nt(a, b):
    return lax.dot_general(a, b, (((1,), (1,)), ((), ())), preferred_element_type=F32)


def _dot_tn(a, b):
    return lax.dot_general(a, b, (((0,), (0,)), ((), ())), preferred_element_type=F32)


def _rope_tile(t, c, sa, sb):
    return t * c + pltpu.roll(t, 96, 1) * sa + pltpu.roll(t, 32, 1) * sb


def _params(*sem):
    return pltpu.CompilerParams(dimension_semantics=sem, vmem_limit_bytes=VMEM_LIMIT)


def _const(shape):
    return pl.BlockSpec(shape, lambda *_: (0,) * len(shape), pipeline_mode=pl.Buffered(1))


def _mod_kernel(c_ref, w_ref, b_ref, o_ref):
    c = c_ref[...]
    s = (c * jax.nn.sigmoid(c)).astype(BF16)
    o_ref[...] = _dot(s, w_ref[...].astype(BF16)) + b_ref[...]


def _modulation(c_all, w_ada, b_ada):
    g, d = c_all.shape
    n = w_ada.shape[1]
    tn = 1024
    return pl.pallas_call(
        _mod_kernel,
        grid=(n // tn,),
        in_specs=[pl.BlockSpec((g, d), lambda j: (0, 0)),
                  pl.BlockSpec((d, tn), lambda j: (0, j)),
                  pl.BlockSpec((1, tn), lambda j: (0, j))],
        out_specs=pl.BlockSpec((g, tn), lambda j: (0, j)),
        out_shape=jax.ShapeDtypeStruct((g, n), F32),
        compiler_params=_params("arbitrary"),
        name="adaln_mod",
    )(c_all, w_ada, b_ada.reshape(1, n))


def _proj_kernel(x_ref, sh_ref, sc_ref, g1_ref, win_ref, gql_ref, wuq_ref, gkv_ref, wukv_ref,
                 gqn_ref, gqr_ref, gkn_ref, gkr_ref, qone_ref, kbias_ref, tab_ref, wg_ref, bg_ref,
                 *outs, gpt, prompt):
    if prompt:
        lat_ref, krt_ref, q_ref, k_ref, v_ref, gq_ref, gk_ref, gv_ref, la_ref, gr_ref = outs
    else:
        lat_ref, kr_ref, krt_ref, q_ref, gq_ref, gk_ref, gv_ref, la_ref, gr_ref = outs
    tm, d = x_ref.shape
    x = x_ref[...]
    xn = x * lax.rsqrt(jnp.mean(x * x, axis=-1, keepdims=True) + EPS)
    h = (xn.reshape(gpt, tm // gpt, d) * (g1_ref[...] * (1.0 + sc_ref[...])) + sh_ref[...]).reshape(tm, d)
    hb = h.astype(BF16)

    def col(c):
        return _dot(hb, win_ref[:, c[0]:c[1]])

    tab = tab_ref[...]
    cos, sa, sb = tab[:, 0:LANES], tab[:, LANES:2 * LANES], tab[:, 2 * LANES:3 * LANES]

    qkr = col(C_QKR)
    krg = qkr[:, Q_LORA:]
    z = _dot(krg.astype(BF16), wg_ref[...]) + bg_ref[...]
    la_ref[...] = (jnp.minimum(z, 0.0) - jnp.log(1.0 + jnp.exp(-jnp.abs(z)))) * (1.0 / GLA_TAU)
    r = col(C_GR)
    gr_ref[...] = (r * jax.nn.sigmoid(r)).astype(BF16)

    cq = qkr[:, 0:Q_LORA]
    cqn = cq * lax.rsqrt(jnp.mean(cq * cq, axis=-1, keepdims=True) + EPS) * gql_ref[...]
    qp = _dot(cqn.astype(BF16), wuq_ref[...])
    for hh in range(MLA_HEADS):
        nope = qp[:, hh * HEAD_PAD:hh * HEAD_PAD + QK_NOPE]
        rt = qp[:, hh * HEAD_PAD + QK_NOPE:(hh + 1) * HEAD_PAD]
        ss = jnp.sum(nope * nope, axis=-1, keepdims=True) + jnp.sum(rt * rt, axis=-1, keepdims=True)
        inv = lax.rsqrt(ss * (1.0 / QK_DIM) + EPS)
        q_ref[:, hh * HEAD_PAD:hh * HEAD_PAD + QK_NOPE] = (nope * inv * gqn_ref[...]).astype(BF16)
        rq = _rope_tile(rt * inv * gqr_ref[...], cos, sa, sb)
        q_ref[:, hh * HEAD_PAD + QK_NOPE:(hh + 1) * HEAD_PAD] = (rq + qone_ref[...]).astype(BF16)

    ckv = col(C_KV)
    lat = ckv * lax.rsqrt(jnp.mean(ckv * ckv, axis=-1, keepdims=True) + EPS) * gkv_ref[...]
    lat_ref[...] = lat
    krt_ref[...] = krg.T[0:QK_ROPE, :]
    if prompt:
        lane = lax.broadcasted_iota(jnp.int32, (1, LANES), 1)
        krm = jnp.where(lane < QK_ROPE, krg, 0.0)
        ssr = jnp.sum(krm * krm, axis=-1, keepdims=True)
        rk = _rope_tile(krm * gkr_ref[...], cos, sa, sb)
        kv = _dot(lat.astype(BF16), wukv_ref[...])
        for hh in range(MLA_HEADS):
            kn = kv[:, hh * QK_NOPE:(hh + 1) * QK_NOPE]
            inv = lax.rsqrt((jnp.sum(kn * kn, axis=-1, keepdims=True) + ssr) * (1.0 / QK_DIM) + EPS)
            k_ref[:, hh * HEAD_PAD:hh * HEAD_PAD + QK_NOPE] = (kn * inv * gkn_ref[...]).astype(BF16)
            k_ref[:, hh * HEAD_PAD + QK_NOPE:(hh + 1) * HEAD_PAD] = (rk * inv + kbias_ref[...]).astype(BF16)
        v_ref[...] = kv[:, MLA_HEADS * QK_NOPE:].astype(BF16)
    else:
        kr_ref[...] = krg[:, 0:QK_ROPE]

    gq_ref[...] = col(C_GQ)
    gk_ref[...] = col(C_GK)
    gv_ref[...] = col(C_GV).astype(BF16)


def _projection(x2, mod4, tab, w, *, rows_per_group, tm, prompt):
    n, d = x2.shape
    gpt = max(1, tm // rows_per_group)
    tpg = max(1, rows_per_group // tm)
    ntab = tab.shape[0] // tm

    def mod_spec(j):
        return pl.BlockSpec((gpt, None, 1, d), lambda i: ((i // tpg) if gpt == 1 else i, j, 0, 0))

    def rows(c):
        return pl.BlockSpec((tm, c), lambda i: (i, 0))

    def out(c, t):
        return rows(c), jax.ShapeDtypeStruct((n, c), t)

    gla_outs = [out(GLA_QK, F32), out(GLA_QK, F32), out(GLA_WIDTH, BF16), out(GLA_QK, F32), out(GLA_WIDTH, BF16)]
    qo = out(MLA_HEADS * HEAD_PAD, BF16)
    if prompt:
        krt = (pl.BlockSpec((None, QK_ROPE, tm), lambda i: (i // tpg, 0, i % tpg)),
               jax.ShapeDtypeStruct((n // rows_per_group, QK_ROPE, rows_per_group), F32))
        outs = [out(KV_LORA, F32), krt, qo, out(MLA_HEADS * HEAD_PAD, BF16), out(MLA_WIDTH, BF16)] + gla_outs
    else:
        krt = (pl.BlockSpec((QK_ROPE, tm), lambda i: (0, i)), jax.ShapeDtypeStruct((QK_ROPE, n), F32))
        outs = [out(KV_LORA, F32), out(QK_ROPE, F32), krt, qo] + gla_outs
    return pl.pallas_call(
        functools.partial(_proj_kernel, gpt=gpt, prompt=prompt),
        grid=(n // tm,),
        in_specs=[rows(d), mod_spec(0), mod_spec(1), _const((1, d)), _const(w["w_in"].shape),
                  _const((1, Q_LORA)), _const(w["w_uq"].shape), _const((1, KV_LORA)), _const(w["w_ukv"].shape),
                  _const((1, LANES)), _const((1, LANES)), _const((1, LANES)), _const((1, LANES)),
                  _const((1, LANES)), _const((1, LANES)),
                  pl.BlockSpec((tm, 3 * LANES), lambda i: (i % ntab, 0)),
                  _const(w["w_gate"].shape), _const((1, GLA_QK))],
        out_specs=[o[0] for o in outs],
        out_shape=[o[1] for o in outs],
        compiler_params=_params("arbitrary"),
        name="in_proj",
    )(x2, mod4, mod4, w["g_norm1"], w["w_in"], w["g_q_lora"], w["w_uq"], w["g_kv_lora"], w["w_ukv"],
      w["gqn"], w["gqr"], w["gkn"], w["gkr"], w["qone"], w["kbias"], tab, w["w_gate"], w["b_gate"])


def _attn_prompt_kernel(fast_ref, q_ref, k_ref, v_ref, o_ref, vx_ref, m_ref, acc_ref, *, tq):
    i = pl.program_id(1)
    heads = range(MLA_HEADS)

    @pl.when(i == 0)
    def _():
        for hh in heads:
            vx_ref[hh, :, 0:V_DIM] = v_ref[:, hh * V_DIM:(hh + 1) * V_DIM]
            vx_ref[hh, :, V_DIM:] = jnp.ones((v_ref.shape[0], V_DIM), BF16)

    acc_ref[...] = jnp.zeros(acc_ref.shape, F32)

    def scores(j, hh, masked):
        hs = slice(hh * HEAD_PAD, (hh + 1) * HEAD_PAD)
        s = _dot_nt(q_ref[:, hs], k_ref[pl.ds(pl.multiple_of(j * tq, tq), tq), hs])
        if masked:
            qc = lax.broadcasted_iota(jnp.int32, (tq, tq), 0) // CHUNK
            kc = lax.broadcasted_iota(jnp.int32, (tq, tq), 1) // CHUNK
            s = jnp.where(kc <= qc, s, -jnp.inf)
        return s

    def vblock(j, hh):
        return vx_ref[hh, pl.ds(pl.multiple_of(j * tq, tq), tq), :]

    def fast_block(j, masked):
        for hh in heads:
            acc_ref[hh] += _dot(jnp.exp2(scores(j, hh, masked)).astype(BF16), vblock(j, hh))

    def safe_block(j, masked):
        for hh in heads:
            s = scores(j, hh, masked)
            m = m_ref[hh]
            m_new = jnp.maximum(m, jnp.max(s, axis=1, keepdims=True))
            p = jnp.exp2(s - m_new).astype(BF16)
            acc_ref[hh] = jnp.exp2(m - m_new) * acc_ref[hh] + _dot(p, vblock(j, hh))
            m_ref[hh] = m_new

    def sweep(block):
        def body(j, c):
            block(j, False)
            return c

        lax.fori_loop(0, i, body, 0)
        block(i, True)

    @pl.when(fast_ref[0] == 1)
    def _():
        sweep(fast_block)

    @pl.when(fast_ref[0] != 1)
    def _():
        m_ref[...] = jnp.full(m_ref.shape, -jnp.inf, F32)
        sweep(safe_block)

    for hh in heads:
        acc = acc_ref[hh]
        o_ref[:, hh * V_DIM:(hh + 1) * V_DIM] = (acc[:, 0:V_DIM] / acc[:, V_DIM:]).astype(BF16)


def _attention_prompt(fast, q, k, v, *, batch, seq, tq):
    nq = seq // tq
    return pl.pallas_call(
        functools.partial(_attn_prompt_kernel, tq=tq),
        grid=(batch, nq),
        in_specs=[pl.BlockSpec(memory_space=pltpu.SMEM),
                  pl.BlockSpec((tq, MLA_HEADS * HEAD_PAD), lambda b, i: (b * nq + i, 0)),
                  pl.BlockSpec((seq, MLA_HEADS * HEAD_PAD), lambda b, i: (b, 0)),
                  pl.BlockSpec((seq, MLA_WIDTH), lambda b, i: (b, 0))],
        out_specs=pl.BlockSpec((tq, MLA_WIDTH), lambda b, i: (b * nq + i, 0)),
        out_shape=jax.ShapeDtypeStruct((batch * seq, MLA_WIDTH), BF16),
        scratch_shapes=[pltpu.VMEM((MLA_HEADS, seq, 2 * V_DIM), BF16), pltpu.VMEM((MLA_HEADS, tq, 1), F32),
                        pltpu.VMEM((MLA_HEADS, tq, 2 * V_DIM), F32)],
        compiler_params=_params("arbitrary", "arbitrary"),
        name="mla_attn_prompt",
    )(fast, q, k, v)


def _attn_sample_kernel(fast_ref, bound_ref, q_ref, lat_ref, krt_ref, latn_ref, krtn_ref, tab_ref, wukt_ref, wuv_ref,
                        gkn_ref, gkr_ref, o_ref, wq_ref, latb_ref, s_ref, *, tkb):
    b = pl.program_id(0)
    t = q_ref.shape[0]
    past = lat_ref.shape[0]
    nk = MLA_HEADS * QK_NOPE
    half = QK_ROPE // 2

    @pl.when(b == 0)
    def _():
        wq_ref[0:nk, :] = wukt_ref[...]

    qr = []
    for hh in range(MLA_HEADS):
        qn = (q_ref[:, hh * HEAD_PAD:hh * HEAD_PAD + QK_NOPE].astype(F32) * gkn_ref[...]).astype(BF16)
        wq_ref[nk + hh * t:nk + (hh + 1) * t, :] = _dot(qn, wukt_ref[hh * QK_NOPE:(hh + 1) * QK_NOPE, :]).astype(BF16)
        qr.append(q_ref[:, hh * HEAD_PAD + QK_NOPE:(hh + 1) * HEAD_PAD])
    qr = jnp.concatenate(qr, axis=0)

    def key_block(lat, krt, tab, width, valid):
        latb = lat.astype(BF16)
        g = _dot_nt(wq_ref[...], latb)
        ssr = jnp.sum(krt * krt, axis=0, keepdims=True)
        kg = krt * gkr_ref[...]
        x1, x2 = kg[0:half], kg[half:QK_ROPE]
        c, sn = tab[0:half], tab[half:QK_ROPE]
        rope = jnp.concatenate([x1 * c - x2 * sn, x2 * c + x1 * sn, jnp.zeros((LANES - QK_ROPE, width), F32)], axis=0)
        srope = _dot(qr, rope.astype(BF16))
        rows = []
        for hh in range(MLA_HEADS):
            kn = g[hh * QK_NOPE:(hh + 1) * QK_NOPE]
            inv = lax.rsqrt((jnp.sum(kn * kn, axis=0, keepdims=True) + ssr) * (1.0 / QK_DIM) + EPS)
            rows.append((g[nk + hh * t:nk + (hh + 1) * t] + srope[hh * t:(hh + 1) * t]) * inv)
        sc = jnp.concatenate(rows, axis=0) - bound_ref[0]
        if valid is not None:
            sc = jnp.where(valid, sc, -jnp.inf)
        return sc, latb

    mine = lax.broadcasted_iota(jnp.int32, (1, 2 * t), 1) // t == b % 2

    def blocks():
        for blk in range(past // tkb):
            c0 = blk * tkb
            yield c0, tkb, key_block(lat_ref[c0:c0 + tkb, :], krt_ref[:, c0:c0 + tkb], tab_ref[:, c0:c0 + tkb],
                                     tkb, None)
        yield past, 2 * t, key_block(latn_ref[...], krtn_ref[...], tab_ref[:, past:past + 2 * t], 2 * t, mine)

    def finish(wlat, l):
        wlat = wlat.astype(BF16)
        for hh in range(MLA_HEADS):
            o = _dot(wlat[hh * t:(hh + 1) * t], wuv_ref[:, hh * V_DIM:(hh + 1) * V_DIM])
            o_ref[:, hh * V_DIM:(hh + 1) * V_DIM] = (o / l[hh * t:(hh + 1) * t]).astype(BF16)

    @pl.when(fast_ref[0] == 1)
    def _():
        wlat = jnp.zeros((MLA_HEADS * t, KV_LORA), F32)
        lsum = jnp.zeros((MLA_HEADS * t, LANES), F32)
        for _, width, (sc, latb) in blocks():
            p = jnp.exp2(sc)
            wlat += _dot(p.astype(BF16), latb)
            for c in range(width // LANES):
                lsum += p[:, c * LANES:(c + 1) * LANES]
        finish(wlat, jnp.sum(lsum, axis=1, keepdims=True))

    @pl.when(fast_ref[0] != 1)
    def _():
        for c0, width, (sc, latb) in blocks():
            s_ref[:, c0:c0 + width] = sc
            latb_ref[c0:c0 + width, :] = latb
        s = s_ref[...]
        p = jnp.exp2(s - jnp.max(s, axis=1, keepdims=True))
        finish(_dot(p.astype(BF16), latb_ref[...]), jnp.sum(p, axis=1, keepdims=True))


def _attention_sample(q, past_lat, past_krt, lat_new, krt_new, w, *, tkb):
    batch, past, _ = past_lat.shape
    t = q.shape[0] // batch
    half = QK_ROPE // 2
    inv = ROPE_THETA ** (-np.arange(half, dtype=np.float64) / half)
    pos = np.concatenate([np.arange(past), past + np.arange(t), past + np.arange(t)]).astype(np.float64)
    ang = inv[:, None] * pos[None, :]
    tab = jnp.asarray(np.concatenate([np.cos(ang), np.sin(ang)], axis=0).astype(np.float32))
    s_pad = past + 2 * t
    return pl.pallas_call(
        functools.partial(_attn_sample_kernel, tkb=tkb),
        grid=(batch,),
        in_specs=[pl.BlockSpec(memory_space=pltpu.SMEM), pl.BlockSpec(memory_space=pltpu.SMEM),
                  pl.BlockSpec((t, MLA_HEADS * HEAD_PAD), lambda b: (b, 0)),
                  pl.BlockSpec((None, past, KV_LORA), lambda b: (b, 0, 0)),
                  pl.BlockSpec((None, QK_ROPE, past), lambda b: (b, 0, 0)),
                  pl.BlockSpec((2 * t, KV_LORA), lambda b: (b // 2, 0)),
                  pl.BlockSpec((QK_ROPE, 2 * t), lambda b: (0, b // 2)),
                  _const(tab.shape), _const(w["w_ukt"].shape), _const(w["w_uv"].shape),
                  _const((1, LANES)), _const((QK_ROPE, 1))],
        out_specs=pl.BlockSpec((t, MLA_WIDTH), lambda b: (b, 0)),
        out_shape=jax.ShapeDtypeStruct((batch * t, MLA_WIDTH), BF16),
        scratch_shapes=[pltpu.VMEM((MLA_HEADS * (QK_NOPE + t), KV_LORA), BF16),
                        pltpu.VMEM((s_pad, KV_LORA), BF16),
                        pltpu.VMEM((MLA_HEADS * t, s_pad), F32)],
        compiler_params=_params("arbitrary"),
        name="mla_attn_sample",
    )(w["fast_softmax"], w["bound"], q, past_lat, past_krt, lat_new, krt_new, tab, w["w_ukt"], w["w_uv"], w["gkn"],
      w["gkr_col"])


def _gla_kernel(q_ref, k_ref, v_ref, la_ref, r_ref, s0_ref, g_ref, spread_ref, o_ref, sn_ref,
                st_ref, kp_ref, ap_ref, p_ref, on_ref, *, gpt):
    t_idx = pl.program_id(1)
    L, W, P = CHUNK, GLA_QK, LANES
    R = q_ref.shape[0]
    n_chunks = R // L
    cpg = n_chunks // gpt
    n_pairs = GLA_HEADS // 2

    lane_p = lax.broadcasted_iota(jnp.int32, (1, P), 1)
    even = lane_p < GLA_DK
    bd_mask = (lax.broadcasted_iota(jnp.int32, (2 * GLA_DV, P), 0) // GLA_DV
               == lax.broadcasted_iota(jnp.int32, (2 * GLA_DV, P), 1) // GLA_DK)

    @pl.when(t_idx == 0)
    def _():
        kp_ref[0:SUB, :] = jnp.zeros((SUB, W), F32)
        ap_ref[0:SUB, :] = jnp.zeros((SUB, W), F32)
        for gi in range(gpt):
            for pr in range(n_pairs):
                tt = s0_ref[gi, 2 * pr:2 * pr + 2].reshape(2 * GLA_DK, GLA_DV).T
                st_ref[gi, pr] = jnp.where(bd_mask, jnp.concatenate([tt, tt], axis=0), 0.0)

    q = q_ref[...]
    k = k_ref[...]
    la = la_ref[...]

    tri = (lax.broadcasted_iota(jnp.int32, (L, L), 0) >= lax.broadcasted_iota(jnp.int32, (L, L), 1)).astype(BF16)
    la_hi = la.astype(BF16)
    la2 = jnp.concatenate([la_hi, (la - la_hi.astype(F32)).astype(BF16)], axis=1)
    bs = []
    for c in range(n_chunks):
        t2 = _dot(tri, la2[c * L:(c + 1) * L, :])
        bs.append(t2[:, 0:W] + t2[:, W:2 * W])
    b = (jnp.concatenate(bs, axis=0) if n_chunks > 1 else bs[0]) * LOG2E
    b3 = b.reshape(n_chunks, L, W)

    def chunk_row(r):
        return jnp.broadcast_to(b3[:, r:r + 1, :], (n_chunks, L, W)).reshape(R, W)

    b_sub = jnp.broadcast_to(b.reshape(R // SUB, SUB, W)[:, 0:1, :], (R // SUB, SUB, W)).reshape(R, W)
    sub = (lax.broadcasted_iota(jnp.int32, (R, W), 0) % L) // SUB

    qt = q * jnp.exp2(b - b_sub)
    zb = jnp.zeros((), BF16)
    zq = jnp.zeros((SUB, P), F32)
    ktm = []
    for i in range(1, L // SUB):
        kt = (k * jnp.exp2(chunk_row(i * SUB) - b)).astype(BF16)
        ktm.append([[jnp.where((sub[:, 0:P] < i) & (even if e == 0 else ~even), kt[:, pr * P:(pr + 1) * P], zb)
                     for e in range(2)] for pr in range(n_pairs)])

    a = jnp.exp(la)
    kp_ref[SUB:SUB + R, :] = k
    ap_ref[SUB:SUB + R, :] = a
    p_ref[:, 0:W] = (q * k).astype(BF16)
    e = a
    for d in range(1, SUB):
        if d > 1:
            e = e * ap_ref[SUB - d + 1:SUB - d + 1 + R, :]
        p_ref[:, d * W:(d + 1) * W] = (q * kp_ref[SUB - d:SUB - d + R, :] * e).astype(BF16)
    cband = _dot(p_ref[...], spread_ref[...])
    same_sub = (lax.broadcasted_iota(jnp.int32, (L, W), 0) // SUB
                == (lax.broadcasted_iota(jnp.int32, (L, W), 1) % L) // SUB)

    qe = (q * jnp.exp2(b)).astype(BF16)
    kd = (k * jnp.exp2(chunk_row(L - 1) - b)).astype(BF16)
    zv = jnp.zeros((L, 2 * GLA_DV), BF16)

    o_intra, d_st, dec = {}, {}, []
    for c in range(n_chunks):
        rs = slice(c * L, (c + 1) * L)
        a_band = jnp.where(same_sub, pltpu.roll(cband[rs], W - (SUB - 1), 1, stride=1, stride_axis=0), 0.0)
        dec.append(jnp.exp2(b[c * L + L - 1:c * L + L, :]))
        for pr in range(n_pairs):
            ls = slice(pr * P, (pr + 1) * P)
            lhs_c = jnp.concatenate(
                [jnp.concatenate([qt[c * L + r * SUB:c * L + (r + 1) * SUB, ls] if i == r else zq
                                  for i in range(1, L // SUB)], axis=1) for r in range(L // SUB)],
                axis=0).astype(BF16)
            rhs_c = jnp.concatenate([jnp.concatenate([m[pr][0][rs], m[pr][1][rs]], axis=0) for m in ktm], axis=1)
            a_tot = (a_band[:, ls] + _dot_nt(lhs_c, rhs_c)).astype(BF16)
            vp = v_ref[rs, 2 * pr * GLA_DV:(2 * pr + 2) * GLA_DV]
            v_bd = jnp.concatenate([jnp.concatenate([vp[:, 0:GLA_DV], zv[:, 0:GLA_DV]], axis=1),
                                    jnp.concatenate([zv[:, 0:GLA_DV], vp[:, GLA_DV:]], axis=1)], axis=0)
            o_intra[c, pr] = _dot(a_tot, v_bd)
            d_st[c, pr] = jnp.where(bd_mask, _dot_tn(vp, kd[rs, ls]), 0.0)
    st_in = {}
    for gi in range(gpt):
        for pr in range(n_pairs):
            st = st_ref[gi, pr]
            for c in range(gi * cpg, (gi + 1) * cpg):
                st_in[c, pr] = st.astype(BF16)
                st = st * dec[c][:, pr * P:(pr + 1) * P] + d_st[c, pr]
            st_ref[gi, pr] = st
    for c in range(n_chunks):
        rs = slice(c * L, (c + 1) * L)
        for pr in range(n_pairs):
            on_ref[rs, 2 * pr * GLA_DV:(2 * pr + 2) * GLA_DV] = (
                o_intra[c, pr] + _dot_nt(qe[rs, pr * P:(pr + 1) * P], st_in[c, pr]))

    for hh in range(GLA_HEADS):
        hs = slice(hh * GLA_DV, (hh + 1) * GLA_DV)
        o = on_ref[:, hs]
        on = o * lax.rsqrt(jnp.mean(o * o, axis=-1, keepdims=True) + EPS) * g_ref[:, hs]
        o_ref[:, hs] = (on * r_ref[:, hs].astype(F32)).astype(BF16)

    @pl.when(t_idx == pl.num_programs(1) - 1)
    def _():
        for gi in range(gpt):
            for pr in range(n_pairs):
                st = st_ref[gi, pr]
                tt = jnp.where(even, st[0:GLA_DV], st[GLA_DV:2 * GLA_DV])
                sn_ref[gi, 2 * pr:2 * pr + 2] = tt.T.reshape(2, GLA_DK, GLA_DV)


def _band_spread():
    m = np.zeros((SUB, GLA_HEADS, GLA_DK, GLA_QK), np.float32)
    for d in range(SUB):
        for h in range(GLA_HEADS):
            m[d, h, :, h * GLA_DK + SUB - 1 - d] = 1.0
    return jnp.asarray(m.reshape(SUB * GLA_QK, GLA_QK), BF16)


def _gla(gq, gk, gv, la, gr, s0, g_out, *, groups, rows_per_group, tc):
    gpt = max(1, tc // rows_per_group)
    nt = max(1, rows_per_group // tc)

    def rows(c):
        return pl.BlockSpec((tc, c), lambda g, t: (g * nt + t, 0))

    state = pl.BlockSpec((gpt, GLA_HEADS, GLA_DK, GLA_DV), lambda g, t: (g, 0, 0, 0))
    spread = _band_spread()
    return pl.pallas_call(
        functools.partial(_gla_kernel, gpt=gpt),
        grid=(groups // gpt, nt),
        in_specs=[rows(GLA_QK), rows(GLA_QK), rows(GLA_WIDTH), rows(GLA_QK), rows(GLA_WIDTH), state,
                  _const((1, GLA_WIDTH)), _const(spread.shape)],
        out_specs=[rows(GLA_WIDTH), state],
        out_shape=[jax.ShapeDtypeStruct((groups * rows_per_group, GLA_WIDTH), BF16),
                   jax.ShapeDtypeStruct((groups, GLA_HEADS, GLA_DK, GLA_DV), F32)],
        scratch_shapes=[pltpu.VMEM((gpt, GLA_HEADS // 2, 2 * GLA_DV, LANES), F32),
                        pltpu.VMEM((SUB + tc, GLA_QK), F32), pltpu.VMEM((SUB + tc, GLA_QK), F32),
                        pltpu.VMEM((tc, SUB * GLA_QK), BF16), pltpu.VMEM((tc, GLA_WIDTH), F32)],
        compiler_params=_params("arbitrary", "arbitrary"),
        name="gla",
    )(gq, gk, gv, la, gr, s0, g_out, spread)


def _mlp_kernel(x_ref, a_ref, b_ref, g1_ref, sh2_ref, sc2_ref, g2_ref, gn_ref, wo_ref, wu_ref, wd_ref, y_ref,
                *, gpt, tf):
    tm, d = x_ref.shape

    def per_group(val, ref, scale_plus_one=False):
        m = ref[...]
        if scale_plus_one:
            m = 1.0 + m
        return (val.reshape(gpt, tm // gpt, d) * m).reshape(tm, d)

    mix = jnp.concatenate([a_ref[...], b_ref[...]], axis=1)
    x1 = x_ref[...] + per_group(_dot(mix, wo_ref[...]), g1_ref)
    xn = x1 * lax.rsqrt(jnp.mean(x1 * x1, axis=-1, keepdims=True) + EPS) * gn_ref[...]
    h2 = (per_group(xn, sc2_ref, True).reshape(gpt, tm // gpt, d) + sh2_ref[...]).reshape(tm, d).astype(BF16)
    acc = jnp.zeros((tm, d), F32)
    for j in range(wu_ref.shape[1] // tf):
        u = jnp.maximum(_dot(h2, wu_ref[:, j * tf:(j + 1) * tf]), 0.0)
        acc += _dot((u * u).astype(BF16), wd_ref[j * tf:(j + 1) * tf, :])
    y_ref[...] = x1 + per_group(acc, g2_ref)


def _mlp(x2, a_out, b_out, mod4, w, *, rows_per_group, tm):
    n, d = x2.shape
    gpt = max(1, tm // rows_per_group)
    tpg = max(1, rows_per_group // tm)

    def mod_spec(j):
        return pl.BlockSpec((gpt, None, 1, d), lambda i: ((i // tpg) if gpt == 1 else i, j, 0, 0))

    def rows(c):
        return pl.BlockSpec((tm, c), lambda i: (i, 0))

    return pl.pallas_call(
        functools.partial(_mlp_kernel, gpt=gpt, tf=1024),
        grid=(n // tm,),
        in_specs=[rows(d), rows(MLA_WIDTH), rows(GLA_WIDTH), mod_spec(2), mod_spec(3), mod_spec(4), mod_spec(5),
                  _const((1, d)), _const(w["w_out"].shape), _const(w["w_up"].shape), _const(w["w_down"].shape)],
        out_specs=rows(d),
        out_shape=jax.ShapeDtypeStruct((n, d), F32),
        compiler_params=_params("arbitrary"),
        name="out_proj_mlp",
    )(x2, a_out, b_out, mod4, mod4, mod4, mod4, w["g_norm2"], w["w_out"], w["w_up"], w["w_down"])


def _rope_table(start, count, repeat=1):
    half = QK_ROPE // 2
    inv = ROPE_THETA ** (-np.arange(half, dtype=np.float64) / half)
    ang = (start + np.arange(count, dtype=np.float64))[:, None] * inv[None, :]
    c, s, z = np.cos(ang), np.sin(ang), np.zeros_like(ang)
    tab = np.concatenate([c, c, z, z, -s, z, z, z, z, s, z, z], axis=1).astype(np.float32)
    return jnp.asarray(np.tile(tab, (repeat, 1)))


def _pad_gain(g_rope):
    return jnp.concatenate([g_rope, jnp.zeros((LANES - QK_ROPE,), F32)]).reshape(1, LANES)


def _relayout_kernel(win_ref, wuq_ref, wukv_ref, wg_ref, win_o, wuq_o, wukv_o, wukt_o, wuv_o, wg_o):
    s = np.cumsum([0, Q_LORA, KV_LORA, QK_ROPE, GLA_QK, GLA_QK, GLA_WIDTH, GLA_GATE_RANK, GLA_WIDTH])

    def piece(i):
        return win_ref[:, int(s[i]):int(s[i + 1])]

    d = win_ref.shape[0]
    zeros = jnp.zeros((d, LANES - QK_ROPE - GLA_GATE_RANK), F32)
    win_o[:, 0:Q_LORA] = piece(0).astype(BF16)
    win_o[:, Q_LORA:C_QKR[1]] = jnp.concatenate([piece(2), piece(6), zeros], axis=1).astype(BF16)
    win_o[:, C_KV[0]:C_KV[1]] = piece(1).astype(BF16)
    win_o[:, C_GQ[0]:C_GQ[1]] = (piece(3) * (GLA_DK ** -0.5)).astype(BF16)
    win_o[:, C_GK[0]:C_GK[1]] = piece(4).astype(BF16)
    win_o[:, C_GV[0]:C_GV[1]] = piece(5).astype(BF16)
    win_o[:, C_GR[0]:C_GR[1]] = piece(7).astype(BF16)

    zq = jnp.zeros((Q_LORA, HEAD_PAD - QK_DIM), BF16)
    kvw = QK_NOPE + V_DIM
    for hh in range(MLA_HEADS):
        wuq_o[:, hh * HEAD_PAD:hh * HEAD_PAD + QK_DIM] = wuq_ref[:, hh * QK_DIM:(hh + 1) * QK_DIM].astype(BF16)
        wuq_o[:, hh * HEAD_PAD + QK_DIM:(hh + 1) * HEAD_PAD] = zq
        uk = wukv_ref[:, hh * kvw:hh * kvw + QK_NOPE]
        uv = wukv_ref[:, hh * kvw + QK_NOPE:(hh + 1) * kvw].astype(BF16)
        wukv_o[:, hh * QK_NOPE:(hh + 1) * QK_NOPE] = uk.astype(BF16)
        wukv_o[:, (MLA_HEADS + hh) * V_DIM:(MLA_HEADS + hh + 1) * V_DIM] = uv
        wukt_o[hh * QK_NOPE:(hh + 1) * QK_NOPE, :] = uk.T.astype(BF16)
        wuv_o[:, hh * V_DIM:(hh + 1) * V_DIM] = uv

    wg_o[...] = jnp.zeros(wg_o.shape, BF16)
    wg_o[QK_ROPE:QK_ROPE + GLA_GATE_RANK, :] = wg_ref[...].astype(BF16)


def _relayout(w_in, w_uq, w_ukv, w_gate_up):
    d = w_in.shape[0]
    shapes = [(d, C_GR[1]), (Q_LORA, MLA_HEADS * HEAD_PAD), (KV_LORA, MLA_HEADS * (QK_NOPE + V_DIM)),
              (MLA_HEADS * QK_NOPE, KV_LORA), (KV_LORA, MLA_WIDTH), (LANES, GLA_QK)]
    return pl.pallas_call(
        _relayout_kernel,
        out_shape=[jax.ShapeDtypeStruct(sh, BF16) for sh in shapes],
        compiler_params=pltpu.CompilerParams(vmem_limit_bytes=VMEM_LIMIT),
        name="weight_relayout",
    )(w_in, w_uq, w_ukv, w_gate_up)


def _prep_weights(w_in, g_norm1, g_q_lora, w_uq, g_kv_lora, w_ukv, g_q_head, g_k_head,
                  w_gate_up, b_gate_up, g_gla_out, w_out, g_norm2, w_up, w_down):
    d = w_in.shape[0]
    w_in_p, w_uq_p, w_ukv_p, w_ukt, w_uv, w_gate = _relayout(w_in, w_uq, w_ukv, w_gate_up)
    qscale = QK_DIM ** -0.5 * LOG2E
    bound = 1.02 * QK_DIM ** 0.5 * LOG2E * jnp.max(jnp.abs(g_q_head)) * jnp.max(jnp.abs(g_k_head))
    lane = jnp.arange(LANES) == QK_ROPE
    return {
        "w_in": w_in_p, "g_norm1": g_norm1.reshape(1, d), "g_q_lora": g_q_lora.reshape(1, Q_LORA),
        "w_uq": w_uq_p, "g_kv_lora": g_kv_lora.reshape(1, KV_LORA), "w_ukv": w_ukv_p, "w_ukt": w_ukt, "w_uv": w_uv,
        "gkr_col": g_k_head[QK_NOPE:].reshape(QK_ROPE, 1),
        "qone": lane.astype(F32).reshape(1, LANES), "kbias": jnp.where(lane, -bound, 0.0).reshape(1, LANES),
        "fast_softmax": (bound <= MAX_FIXED_SHIFT).astype(jnp.int32).reshape(1), "bound": bound.reshape(1),
        "gqn": (g_q_head[:QK_NOPE] * qscale).reshape(1, LANES), "gqr": _pad_gain(g_q_head[QK_NOPE:] * qscale),
        "gkn": g_k_head[:QK_NOPE].reshape(1, LANES), "gkr": _pad_gain(g_k_head[QK_NOPE:]),
        "w_gate": w_gate, "b_gate": b_gate_up.reshape(1, GLA_QK),
        "g_gla_out": g_gla_out.reshape(1, GLA_WIDTH), "w_out": w_out.astype(BF16),
        "g_norm2": g_norm2.reshape(1, d), "w_up": w_up.astype(BF16), "w_down": w_down.astype(BF16),
    }


def _layer(x, mod, past_lat, past_kr, s0, w, *, tm):
    batch, seq, d = x.shape
    n = batch * seq
    past = 0 if past_lat is None else past_lat.shape[1]
    x2 = x.reshape(n, d)
    mod4 = mod.reshape(batch, 6, 1, d)
    tm = min(tm, n)
    tab = _rope_table(past, seq, repeat=max(1, tm // seq))
    if past == 0:
        lat, krt, q, k, v, gq, gk, gv, la, gr = _projection(x2, mod4, tab, w, rows_per_group=seq, tm=tm, prompt=True)
        kr = jnp.swapaxes(krt, 1, 2)
        a_out = _attention_prompt(w["fast_softmax"], q, k, v, batch=batch, seq=seq, tq=min(512, seq))
    else:
        assert seq == CHUNK and past % CHUNK == 0
        lat, kr, krt, q, gq, gk, gv, la, gr = _projection(x2, mod4, tab, w, rows_per_group=seq, tm=tm, prompt=False)
        a_out = _attention_sample(q, past_lat, jnp.swapaxes(past_kr, 1, 2), lat, krt, w, tkb=min(512, past))
    b_out, s_new = _gla(gq, gk, gv, la, gr, s0, w["g_gla_out"], groups=batch, rows_per_group=seq,
                        tc=tm)
    y = _mlp(x2, a_out, b_out, mod4, w, rows_per_group=seq, tm=tm)
    return (y.reshape(batch, seq, d), lat.reshape(batch, seq, KV_LORA), kr.reshape(batch, seq, QK_ROPE), s_new)


def kernel(x_prompt, x_sample, cache_mla_latent, cache_mla_krope, state_gla, c_prompt, c_sample,
           w_ada, b_ada, g_norm1, w_in, g_q_lora, w_uq, g_kv_lora, w_ukv, g_q_head, g_k_head,
           w_gate_up, b_gate_up, g_gla_out, w_out, g_norm2, w_up, w_down):
    nb = x_prompt.shape[0]
    depth = w_ada.shape[0]
    y_p, y_s = x_prompt, x_sample
    outs = [[] for _ in range(6)]
    c_all = jnp.concatenate([c_prompt, c_sample], axis=0)
    for l in range(depth):
        w = _prep_weights(w_in[l], g_norm1[l], g_q_lora[l], w_uq[l], g_kv_lora[l], w_ukv[l], g_q_head[l],
                          g_k_head[l], w_gate_up[l], b_gate_up[l], g_gla_out[l], w_out[l], g_norm2[l],
                          w_up[l], w_down[l])
        mod = _modulation(c_all, w_ada[l], b_ada[l])
        zero_state = jnp.zeros((nb, GLA_HEADS, GLA_DK, GLA_DV), x_prompt.dtype)
        y_p, lat, kr, st = _layer(y_p, mod[:nb], None, None, zero_state, w, tm=512)
        outs[0].append(lat); outs[1].append(kr); outs[2].append(st)
        y_s, lat, kr, st = _layer(y_s, mod[nb:], cache_mla_latent[l], cache_mla_krope[l], state_gla[l], w, tm=512)
        outs[3].append(lat); outs[4].append(kr); outs[5].append(st)
    return (y_p, y_s) + tuple(jnp.stack(o) for o in outs)
```

```python
import functools

import jax
import jax.numpy as jnp
import numpy as np
from jax import lax
from jax.experimental import pallas as pl
from jax.experimental.pallas import tpu as pltpu

F32 = jnp.float32
BF16 = jnp.bfloat16

CHUNK = 64
EPS = 1e-6
MLA_HEADS = 4
Q_LORA = 384
KV_LORA = 256
QK_NOPE = 128
QK_ROPE = 64
QK_DIM = QK_NOPE + QK_ROPE
V_DIM = 128
ROPE_THETA = 10000.0
GLA_HEADS = 4
GLA_DK = 64
GLA_DV = 128
GLA_GATE_RANK = 16
GLA_TAU = 16.0
GLA_QK = GLA_HEADS * GLA_DK
GLA_WIDTH = GLA_HEADS * GLA_DV
MLA_WIDTH = MLA_HEADS * V_DIM
HEAD_PAD = 256
SUB = 8
LOG2E = 1.4426950408889634
MAX_FIXED_SHIFT = 48.0

LANES = 128
VMEM_LIMIT = 56 * 1024 * 1024

C_QKR = (0, 512)
C_KV = (512, 768)
C_GQ = (768, 1024)
C_GK = (1024, 1280)
C_GV = (1280, 1792)
C_GR = (1792, 2304)


def _dot(a, b):
    return jnp.dot(a, b, preferred_element_type=F32)


def _dot_nt(a, b):
    return lax.dot_general(a, b, (((1,), (1,)), ((), ())), preferred_element_type=F32)


def _dot_tn(a, b):
    return lax.dot_general(a, b, (((0,), (0,)), ((), ())), preferred_element_type=F32)


def _rope_tile(t, c, sa, sb):
    return t * c + pltpu.roll(t, 96, 1) * sa + pltpu.roll(t, 32, 1) * sb


def _params(*sem):
    return pltpu.CompilerParams(dimension_semantics=sem, vmem_limit_bytes=VMEM_LIMIT)


def _const(shape):
    return pl.BlockSpec(shape, lambda *_: (0,) * len(shape), pipeline_mode=pl.Buffered(1))


def _mod_kernel(c_ref, w_ref, b_ref, o_ref):
    c = c_ref[...]
    s = (c * jax.nn.sigmoid(c)).astype(BF16)
    o_ref[...] = _dot(s, w_ref[...].astype(BF16)) + b_ref[...]


def _modulation(c_all, w_ada, b_ada):
    g, d = c_all.shape
    n = w_ada.shape[1]
    tn = 1024
    return pl.pallas_call(
        _mod_kernel,
        grid=(n // tn,),
        in_specs=[pl.BlockSpec((g, d), lambda j: (0, 0)),
                  pl.BlockSpec((d, tn), lambda j: (0, j)),
                  pl.BlockSpec((1, tn), lambda j: (0, j))],
        out_specs=pl.BlockSpec((g, tn), lambda j: (0, j)),
        out_shape=jax.ShapeDtypeStruct((g, n), F32),
        compiler_params=_params("arbitrary"),
        name="adaln_mod",
    )(c_all, w_ada, b_ada.reshape(1, n))


def _proj_kernel(x_ref, sh_ref, sc_ref, g1_ref, win_ref, gql_ref, wuq_ref, gkv_ref, wukv_ref,
                 gqn_ref, gqr_ref, gkn_ref, gkr_ref, qone_ref, kbias_ref, tab_ref, wg_ref, bg_ref,
                 *outs, gpt, prompt):
    if prompt:
        lat_ref, krt_ref, q_ref, k_ref, v_ref, gq_ref, gk_ref, gv_ref, la_ref, gr_ref = outs
    else:
        lat_ref, kr_ref, krt_ref, q_ref, gq_ref, gk_ref, gv_ref, la_ref, gr_ref = outs
    tm, d = x_ref.shape
    x = x_ref[...]
    xn = x * lax.rsqrt(jnp.mean(x * x, axis=-1, keepdims=True) + EPS)
    h = (xn.reshape(gpt, tm // gpt, d) * (g1_ref[...] * (1.0 + sc_ref[...])) + sh_ref[...]).reshape(tm, d)
    hb = h.astype(BF16)

    def col(c):
        return _dot(hb, win_ref[:, c[0]:c[1]])

    tab = tab_ref[...]
    cos, sa, sb = tab[:, 0:LANES], tab[:, LANES:2 * LANES], tab[:, 2 * LANES:3 * LANES]

    qkr = col(C_QKR)
    krg = qkr[:, Q_LORA:]
    z = _dot(krg.astype(BF16), wg_ref[...]) + bg_ref[...]
    la_ref[...] = (jnp.minimum(z, 0.0) - jnp.log(1.0 + jnp.exp(-jnp.abs(z)))) * (1.0 / GLA_TAU)
    r = col(C_GR)
    gr_ref[...] = (r * jax.nn.sigmoid(r)).astype(BF16)

    cq = qkr[:, 0:Q_LORA]
    cqn = cq * lax.rsqrt(jnp.mean(cq * cq, axis=-1, keepdims=True) + EPS) * gql_ref[...]
    qp = _dot(cqn.astype(BF16), wuq_ref[...])
    for hh in range(MLA_HEADS):
        nope = qp[:, hh * HEAD_PAD:hh * HEAD_PAD + QK_NOPE]
        rt = qp[:, hh * HEAD_PAD + QK_NOPE:(hh + 1) * HEAD_PAD]
        ss = jnp.sum(nope * nope, axis=-1, keepdims=True) + jnp.sum(rt * rt, axis=-1, keepdims=True)
        inv = lax.rsqrt(ss * (1.0 / QK_DIM) + EPS)
        q_ref[:, hh * HEAD_PAD:hh * HEAD_PAD + QK_NOPE] = (nope * inv * gqn_ref[...]).astype(BF16)
        rq = _rope_tile(rt * inv * gqr_ref[...], cos, sa, sb)
        q_ref[:, hh * HEAD_PAD + QK_NOPE:(hh + 1) * HEAD_PAD] = (rq + qone_ref[...]).astype(BF16)

    ckv = col(C_KV)
    lat = ckv * lax.rsqrt(jnp.mean(ckv * ckv, axis=-1, keepdims=True) + EPS) * gkv_ref[...]
    lat_ref[...] = lat
    krt_ref[...] = krg.T[0:QK_ROPE, :]
    if prompt:
        lane = lax.broadcasted_iota(jnp.int32, (1, LANES), 1)
        krm = jnp.where(lane < QK_ROPE, krg, 0.0)
        ssr = jnp.sum(krm * krm, axis=-1, keepdims=True)
        rk = _rope_tile(krm * gkr_ref[...], cos, sa, sb)
        kv = _dot(lat.astype(BF16), wukv_ref[...])
        for hh in range(MLA_HEADS):
            kn = kv[:, hh * QK_NOPE:(hh + 1) * QK_NOPE]
            inv = lax.rsqrt((jnp.sum(kn * kn, axis=-1, keepdims=True) + ssr) * (1.0 / QK_DIM) + EPS)
            k_ref[:, hh * HEAD_PAD:hh * HEAD_PAD + QK_NOPE] = (kn * inv * gkn_ref[...]).astype(BF16)
            k_ref[:, hh * HEAD_PAD + QK_NOPE:(hh + 1) * HEAD_PAD] = (rk * inv + kbias_ref[...]).astype(BF16)
        v_ref[...] = kv[:, MLA_HEADS * QK_NOPE:].astype(BF16)
    else:
        kr_ref[...] = krg[:, 0:QK_ROPE]

    gq_ref[...] = col(C_GQ)
    gk_ref[...] = col(C_GK)
    gv_ref[...] = col(C_GV).astype(BF16)


def _projection(x2, mod4, tab, w, *, rows_per_group, tm, prompt):
    n, d = x2.shape
    gpt = max(1, tm // rows_per_group)
    tpg = max(1, rows_per_group // tm)
    ntab = tab.shape[0] // tm

    def mod_spec(j):
        return pl.BlockSpec((gpt, None, 1, d), lambda i: ((i // tpg) if gpt == 1 else i, j, 0, 0))

    def rows(c):
        return pl.BlockSpec((tm, c), lambda i: (i, 0))

    def out(c, t):
        return rows(c), jax.ShapeDtypeStruct((n, c), t)

    gla_outs = [out(GLA_QK, F32), out(GLA_QK, F32), out(GLA_WIDTH, BF16), out(GLA_QK, F32), out(GLA_WIDTH, BF16)]
    qo = out(MLA_HEADS * HEAD_PAD, BF16)
    if prompt:
        krt = (pl.BlockSpec((None, QK_ROPE, tm), lambda i: (i // tpg, 0, i % tpg)),
               jax.ShapeDtypeStruct((n // rows_per_group, QK_ROPE, rows_per_group), F32))
        outs = [out(KV_LORA, F32), krt, qo, out(MLA_HEADS * HEAD_PAD, BF16), out(MLA_WIDTH, BF16)] + gla_outs
    else:
        krt = (pl.BlockSpec((QK_ROPE, tm), lambda i: (0, i)), jax.ShapeDtypeStruct((QK_ROPE, n), F32))
        outs = [out(KV_LORA, F32), out(QK_ROPE, F32), krt, qo] + gla_outs
    return pl.pallas_call(
        functools.partial(_proj_kernel, gpt=gpt, prompt=prompt),
        grid=(n // tm,),
        in_specs=[rows(d), mod_spec(0), mod_spec(1), _const((1, d)), _const(w["w_in"].shape),
                  _const((1, Q_LORA)), _const(w["w_uq"].shape), _const((1, KV_LORA)), _const(w["w_ukv"].shape),
                  _const((1, LANES)), _const((1, LANES)), _const((1, LANES)), _const((1, LANES)),
                  _const((1, LANES)), _const((1, LANES)),
                  pl.BlockSpec((tm, 3 * LANES), lambda i: (i % ntab, 0)),
                  _const(w["w_gate"].shape), _const((1, GLA_QK))],
        out_specs=[o[0] for o in outs],
        out_shape=[o[1] for o in outs],
        compiler_params=_params("arbitrary"),
        name="in_proj",
    )(x2, mod4, mod4, w["g_norm1"], w["w_in"], w["g_q_lora"], w["w_uq"], w["g_kv_lora"], w["w_ukv"],
      w["gqn"], w["gqr"], w["gkn"], w["gkr"], w["qone"], w["kbias"], tab, w["w_gate"], w["b_gate"])


def _attn_prompt_kernel(fast_ref, q_ref, k_ref, v_ref, o_ref, vx_ref, m_ref, acc_ref, *, tq):
    i = pl.program_id(1)
    heads = range(MLA_HEADS)

    @pl.when(i == 0)
    def _():
        for hh in heads:
            vx_ref[hh, :, 0:V_DIM] = v_ref[:, hh * V_DIM:(hh + 1) * V_DIM]
            vx_ref[hh, :, V_DIM:] = jnp.ones((v_ref.shape[0], V_DIM), BF16)

    acc_ref[...] = jnp.zeros(acc_ref.shape, F32)

    def scores(j, hh, masked):
        hs = slice(hh * HEAD_PAD, (hh + 1) * HEAD_PAD)
        s = _dot_nt(q_ref[:, hs], k_ref[pl.ds(pl.multiple_of(j * tq, tq), tq), hs])
        if masked:
            qc = lax.broadcasted_iota(jnp.int32, (tq, tq), 0) // CHUNK
            kc = lax.broadcasted_iota(jnp.int32, (tq, tq), 1) // CHUNK
            s = jnp.where(kc <= qc, s, -jnp.inf)
        return s

    def vblock(j, hh):
        return vx_ref[hh, pl.ds(pl.multiple_of(j * tq, tq), tq), :]

    def fast_block(j, masked):
        for hh in heads:
            acc_ref[hh] += _dot(jnp.exp2(scores(j, hh, masked)).astype(BF16), vblock(j, hh))

    def safe_block(j, masked):
        for hh in heads:
            s = scores(j, hh, masked)
            m = m_ref[hh]
            m_new = jnp.maximum(m, jnp.max(s, axis=1, keepdims=True))
            p = jnp.exp2(s - m_new).astype(BF16)
            acc_ref[hh] = jnp.exp2(m - m_new) * acc_ref[hh] + _dot(p, vblock(j, hh))
            m_ref[hh] = m_new

    def sweep(block):
        def body(j, c):
            block(j, False)
            return c

        lax.fori_loop(0, i, body, 0)
        block(i, True)

    @pl.when(fast_ref[0] == 1)
    def _():
        sweep(fast_block)

    @pl.when(fast_ref[0] != 1)
    def _():
        m_ref[...] = jnp.full(m_ref.shape, -jnp.inf, F32)
        sweep(safe_block)

    for hh in heads:
        acc = acc_ref[hh]
        o_ref[:, hh * V_DIM:(hh + 1) * V_DIM] = (acc[:, 0:V_DIM] / acc[:, V_DIM:]).astype(BF16)


def _attention_prompt(fast, q, k, v, *, batch, seq, tq):
    nq = seq // tq
    return pl.pallas_call(
        functools.partial(_attn_prompt_kernel, tq=tq),
        grid=(batch, nq),
        in_specs=[pl.BlockSpec(memory_space=pltpu.SMEM),
                  pl.BlockSpec((tq, MLA_HEADS * HEAD_PAD), lambda b, i: (b * nq + i, 0)),
                  pl.BlockSpec((seq, MLA_HEADS * HEAD_PAD), lambda b, i: (b, 0)),
                  pl.BlockSpec((seq, MLA_WIDTH), lambda b, i: (b, 0))],
        out_specs=pl.BlockSpec((tq, MLA_WIDTH), lambda b, i: (b * nq + i, 0)),
        out_shape=jax.ShapeDtypeStruct((batch * seq, MLA_WIDTH), BF16),
        scratch_shapes=[pltpu.VMEM((MLA_HEADS, seq, 2 * V_DIM), BF16), pltpu.VMEM((MLA_HEADS, tq, 1), F32),
                        pltpu.VMEM((MLA_HEADS, tq, 2 * V_DIM), F32)],
        compiler_params=_params("arbitrary", "arbitrary"),
        name="mla_attn_prompt",
    )(fast, q, k, v)


def _attn_sample_kernel(fast_ref, bound_ref, q_ref, lat_ref, krt_ref, latn_ref, krtn_ref, tab_ref, wukt_ref, wuv_ref,
                        gkn_ref, gkr_ref, o_ref, wq_ref, latb_ref, s_ref, *, tkb):
    b = pl.program_id(0)
    t = q_ref.shape[0]
    past = lat_ref.shape[0]
    nk = MLA_HEADS * QK_NOPE
    half = QK_ROPE // 2

    @pl.when(b == 0)
    def _():
        wq_ref[0:nk, :] = wukt_ref[...]

    qr = []
    for hh in range(MLA_HEADS):
        qn = (q_ref[:, hh * HEAD_PAD:hh * HEAD_PAD + QK_NOPE].astype(F32) * gkn_ref[...]).astype(BF16)
        wq_ref[nk + hh * t:nk + (hh + 1) * t, :] = _dot(qn, wukt_ref[hh * QK_NOPE:(hh + 1) * QK_NOPE, :]).astype(BF16)
        qr.append(q_ref[:, hh * HEAD_PAD + QK_NOPE:(hh + 1) * HEAD_PAD])
    qr = jnp.concatenate(qr, axis=0)

    def key_block(lat, krt, tab, width, valid):
        latb = lat.astype(BF16)
        g = _dot_nt(wq_ref[...], latb)
        ssr = jnp.sum(krt * krt, axis=0, keepdims=True)
        kg = krt * gkr_ref[...]
        x1, x2 = kg[0:half], kg[half:QK_ROPE]
        c, sn = tab[0:half], tab[half:QK_ROPE]
        rope = jnp.concatenate([x1 * c - x2 * sn, x2 * c + x1 * sn, jnp.zeros((LANES - QK_ROPE, width), F32)], axis=0)
        srope = _dot(qr, rope.astype(BF16))
        rows = []
        for hh in range(MLA_HEADS):
            kn = g[hh * QK_NOPE:(hh + 1) * QK_NOPE]
            inv = lax.rsqrt((jnp.sum(kn * kn, axis=0, keepdims=True) + ssr) * (1.0 / QK_DIM) + EPS)
            rows.append((g[nk + hh * t:nk + (hh + 1) * t] + srope[hh * t:(hh + 1) * t]) * inv)
        sc = jnp.concatenate(rows, axis=0) - bound_ref[0]
        if valid is not None:
            sc = jnp.where(valid, sc, -jnp.inf)
        return sc, latb

    mine = lax.broadcasted_iota(jnp.int32, (1, 2 * t), 1) // t == b % 2

    def blocks():
        for blk in range(past // tkb):
            c0 = blk * tkb
            yield c0, tkb, key_block(lat_ref[c0:c0 + tkb, :], krt_ref[:, c0:c0 + tkb], tab_ref[:, c0:c0 + tkb],
                                     tkb, None)
        yield past, 2 * t, key_block(latn_ref[...], krtn_ref[...], tab_ref[:, past:past + 2 * t], 2 * t, mine)

    def finish(wlat, l):
        wlat = wlat.astype(BF16)
        for hh in range(MLA_HEADS):
            o = _dot(wlat[hh * t:(hh + 1) * t], wuv_ref[:, hh * V_DIM:(hh + 1) * V_DIM])
            o_ref[:, hh * V_DIM:(hh + 1) * V_DIM] = (o / l[hh * t:(hh + 1) * t]).astype(BF16)

    @pl.when(fast_ref[0] == 1)
    def _():
        wlat = jnp.zeros((MLA_HEADS * t, KV_LORA), F32)
        lsum = jnp.zeros((MLA_HEADS * t, LANES), F32)
        for _, width, (sc, latb) in blocks():
            p = jnp.exp2(sc)
            wlat += _dot(p.astype(BF16), latb)
            for c in range(width // LANES):
                lsum += p[:, c * LANES:(c + 1) * LANES]
        finish(wlat, jnp.sum(lsum, axis=1, keepdims=True))

    @pl.when(fast_ref[0] != 1)
    def _():
        for c0, width, (sc, latb) in blocks():
            s_ref[:, c0:c0 + width] = sc
            latb_ref[c0:c0 + width, :] = latb
        s = s_ref[...]
        p = jnp.exp2(s - jnp.max(s, axis=1, keepdims=True))
        finish(_dot(p.astype(BF16), latb_ref[...]), jnp.sum(p, axis=1, keepdims=True))


def _attention_sample(q, past_lat, past_krt, lat_new, krt_new, w, *, tkb):
    batch, past, _ = past_lat.shape
    t = q.shape[0] // batch
    half = QK_ROPE // 2
    inv = ROPE_THETA ** (-np.arange(half, dtype=np.float64) / half)
    pos = np.concatenate([np.arange(past), past + np.arange(t), past + np.arange(t)]).astype(np.float64)
    ang = inv[:, None] * pos[None, :]
    tab = jnp.asarray(np.concatenate([np.cos(ang), np.sin(ang)], axis=0).astype(np.float32))
    s_pad = past + 2 * t
    return pl.pallas_call(
        functools.partial(_attn_sample_kernel, tkb=tkb),
        grid=(batch,),
        in_specs=[pl.BlockSpec(memory_space=pltpu.SMEM), pl.BlockSpec(memory_space=pltpu.SMEM),
                  pl.BlockSpec((t, MLA_HEADS * HEAD_PAD), lambda b: (b, 0)),
                  pl.BlockSpec((None, past, KV_LORA), lambda b: (b, 0, 0)),
                  pl.BlockSpec((None, QK_ROPE, past), lambda b: (b, 0, 0)),
                  pl.BlockSpec((2 * t, KV_LORA), lambda b: (b // 2, 0)),
                  pl.BlockSpec((QK_ROPE, 2 * t), lambda b: (0, b // 2)),
                  _const(tab.shape), _const(w["w_ukt"].shape), _const(w["w_uv"].shape),
                  _const((1, LANES)), _const((QK_ROPE, 1))],
        out_specs=pl.BlockSpec((t, MLA_WIDTH), lambda b: (b, 0)),
        out_shape=jax.ShapeDtypeStruct((batch * t, MLA_WIDTH), BF16),
        scratch_shapes=[pltpu.VMEM((MLA_HEADS * (QK_NOPE + t), KV_LORA), BF16),
                        pltpu.VMEM((s_pad, KV_LORA), BF16),
                        pltpu.VMEM((MLA_HEADS * t, s_pad), F32)],
        compiler_params=_params("arbitrary"),
        name="mla_attn_sample",
    )(w["fast_softmax"], w["bound"], q, past_lat, past_krt, lat_new, krt_new, tab, w["w_ukt"], w["w_uv"], w["gkn"],
      w["gkr_col"])


def _gla_kernel(q_ref, k_ref, v_ref, la_ref, r_ref, s0_ref, g_ref, spread_ref, o_ref, sn_ref,
                st_ref, kp_ref, ap_ref, p_ref, on_ref, *, gpt):
    t_idx = pl.program_id(1)
    L, W, P = CHUNK, GLA_QK, LANES
    R = q_ref.shape[0]
    n_chunks = R // L
    cpg = n_chunks // gpt
    n_pairs = GLA_HEADS // 2

    lane_p = lax.broadcasted_iota(jnp.int32, (1, P), 1)
    even = lane_p < GLA_DK
    bd_mask = (lax.broadcasted_iota(jnp.int32, (2 * GLA_DV, P), 0) // GLA_DV
               == lax.broadcasted_iota(jnp.int32, (2 * GLA_DV, P), 1) // GLA_DK)

    @pl.when(t_idx == 0)
    def _():
        kp_ref[0:SUB, :] = jnp.zeros((SUB, W), F32)
        ap_ref[0:SUB, :] = jnp.zeros((SUB, W), F32)
        for gi in range(gpt):
            for pr in range(n_pairs):
                tt = s0_ref[gi, 2 * pr:2 * pr + 2].reshape(2 * GLA_DK, GLA_DV).T
                st_ref[gi, pr] = jnp.where(bd_mask, jnp.concatenate([tt, tt], axis=0), 0.0)

    q = q_ref[...]
    k = k_ref[...]
    la = la_ref[...]

    tri = (lax.broadcasted_iota(jnp.int32, (L, L), 0) >= lax.broadcasted_iota(jnp.int32, (L, L), 1)).astype(BF16)
    la_hi = la.astype(BF16)
    la2 = jnp.concatenate([la_hi, (la - la_hi.astype(F32)).astype(BF16)], axis=1)
    bs = []
    for c in range(n_chunks):
        t2 = _dot(tri, la2[c * L:(c + 1) * L, :])
        bs.append(t2[:, 0:W] + t2[:, W:2 * W])
    b = (jnp.concatenate(bs, axis=0) if n_chunks > 1 else bs[0]) * LOG2E
    b3 = b.reshape(n_chunks, L, W)

    def chunk_row(r):
        return jnp.broadcast_to(b3[:, r:r + 1, :], (n_chunks, L, W)).reshape(R, W)

    b_sub = jnp.broadcast_to(b.reshape(R // SUB, SUB, W)[:, 0:1, :], (R // SUB, SUB, W)).reshape(R, W)
    sub = (lax.broadcasted_iota(jnp.int32, (R, W), 0) % L) // SUB

    qt = q * jnp.exp2(b - b_sub)
    zb = jnp.zeros((), BF16)
    zq = jnp.zeros((SUB, P), F32)
    ktm = []
    for i in range(1, L // SUB):
        kt = (k * jnp.exp2(chunk_row(i * SUB) - b)).astype(BF16)
        ktm.append([[jnp.where((sub[:, 0:P] < i) & (even if e == 0 else ~even), kt[:, pr * P:(pr + 1) * P], zb)
                     for e in range(2)] for pr in range(n_pairs)])

    a = jnp.exp(la)
    kp_ref[SUB:SUB + R, :] = k
    ap_ref[SUB:SUB + R, :] = a
    p_ref[:, 0:W] = (q * k).astype(BF16)
    e = a
    for d in range(1, SUB):
        if d > 1:
            e = e * ap_ref[SUB - d + 1:SUB - d + 1 + R, :]
        p_ref[:, d * W:(d + 1) * W] = (q * kp_ref[SUB - d:SUB - d + R, :] * e).astype(BF16)
    cband = _dot(p_ref[...], spread_ref[...])
    same_sub = (lax.broadcasted_iota(jnp.int32, (L, W), 0) // SUB
                == (lax.broadcasted_iota(jnp.int32, (L, W), 1) % L) // SUB)

    qe = (q * jnp.exp2(b)).astype(BF16)
    kd = (k * jnp.exp2(chunk_row(L - 1) - b)).astype(BF16)
    zv = jnp.zeros((L, 2 * GLA_DV), BF16)

    o_intra, d_st, dec = {}, {}, []
    for c in range(n_chunks):
        rs = slice(c * L, (c + 1) * L)
        a_band = jnp.where(same_sub, pltpu.roll(cband[rs], W - (SUB - 1), 1, stride=1, stride_axis=0), 0.0)
        dec.append(jnp.exp2(b[c * L + L - 1:c * L + L, :]))
        for pr in range(n_pairs):
            ls = slice(pr * P, (pr + 1) * P)
            lhs_c = jnp.concatenate(
                [jnp.concatenate([qt[c * L + r * SUB:c * L + (r + 1) * SUB, ls] if i == r else zq
                                  for i in range(1, L // SUB)], axis=1) for r in range(L // SUB)],
                axis=0).astype(BF16)
            rhs_c = jnp.concatenate([jnp.concatenate([m[pr][0][rs], m[pr][1][rs]], axis=0) for m in ktm], axis=1)
            a_tot = (a_band[:, ls] + _dot_nt(lhs_c, rhs_c)).astype(BF16)
            vp = v_ref[rs, 2 * pr * GLA_DV:(2 * pr + 2) * GLA_DV]
            v_bd = jnp.concatenate([jnp.concatenate([vp[:, 0:GLA_DV], zv[:, 0:GLA_DV]], axis=1),
                                    jnp.concatenate([zv[:, 0:GLA_DV], vp[:, GLA_DV:]], axis=1)], axis=0)
            o_intra[c, pr] = _dot(a_tot, v_bd)
            d_st[c, pr] = jnp.where(bd_mask, _dot_tn(vp, kd[rs, ls]), 0.0)
    st_in = {}
    for gi in range(gpt):
        for pr in range(n_pairs):
            st = st_ref[gi, pr]
            for c in range(gi * cpg, (gi + 1) * cpg):
                st_in[c, pr] = st.astype(BF16)
                st = st * dec[c][:, pr * P:(pr + 1) * P] + d_st[c, pr]
            st_ref[gi, pr] = st
    for c in range(n_chunks):
        rs = slice(c * L, (c + 1) * L)
        for pr in range(n_pairs):
            on_ref[rs, 2 * pr * GLA_DV:(2 * pr + 2) * GLA_DV] = (
                o_intra[c, pr] + _dot_nt(qe[rs, pr * P:(pr + 1) * P], st_in[c, pr]))

    for hh in range(GLA_HEADS):
        hs = slice(hh * GLA_DV, (hh + 1) * GLA_DV)
        o = on_ref[:, hs]
        on = o * lax.rsqrt(jnp.mean(o * o, axis=-1, keepdims=True) + EPS) * g_ref[:, hs]
        o_ref[:, hs] = (on * r_ref[:, hs].astype(F32)).astype(BF16)

    @pl.when(t_idx == pl.num_programs(1) - 1)
    def _():
        for gi in range(gpt):
            for pr in range(n_pairs):
                st = st_ref[gi, pr]
                tt = jnp.where(even, st[0:GLA_DV], st[GLA_DV:2 * GLA_DV])
                sn_ref[gi, 2 * pr:2 * pr + 2] = tt.T.reshape(2, GLA_DK, GLA_DV)


def _band_spread():
    m = np.zeros((SUB, GLA_HEADS, GLA_DK, GLA_QK), np.float32)
    for d in range(SUB):
        for h in range(GLA_HEADS):
            m[d, h, :, h * GLA_DK + SUB - 1 - d] = 1.0
    return jnp.asarray(m.reshape(SUB * GLA_QK, GLA_QK), BF16)


def _gla(gq, gk, gv, la, gr, s0, g_out, *, groups, rows_per_group, tc):
    gpt = max(1, tc // rows_per_group)
    nt = max(1, rows_per_group // tc)

    def rows(c):
        return pl.BlockSpec((tc, c), lambda g, t: (g * nt + t, 0))

    state = pl.BlockSpec((gpt, GLA_HEADS, GLA_DK, GLA_DV), lambda g, t: (g, 0, 0, 0))
    spread = _band_spread()
    return pl.pallas_call(
        functools.partial(_gla_kernel, gpt=gpt),
        grid=(groups // gpt, nt),
        in_specs=[rows(GLA_QK), rows(GLA_QK), rows(GLA_WIDTH), rows(GLA_QK), rows(GLA_WIDTH), state,
                  _const((1, GLA_WIDTH)), _const(spread.shape)],
        out_specs=[rows(GLA_WIDTH), state],
        out_shape=[jax.ShapeDtypeStruct((groups * rows_per_group, GLA_WIDTH), BF16),
                   jax.ShapeDtypeStruct((groups, GLA_HEADS, GLA_DK, GLA_DV), F32)],
        scratch_shapes=[pltpu.VMEM((gpt, GLA_HEADS // 2, 2 * GLA_DV, LANES), F32),
                        pltpu.VMEM((SUB + tc, GLA_QK), F32), pltpu.VMEM((SUB + tc, GLA_QK), F32),
                        pltpu.VMEM((tc, SUB * GLA_QK), BF16), pltpu.VMEM((tc, GLA_WIDTH), F32)],
        compiler_params=_params("arbitrary", "arbitrary"),
        name="gla",
    )(gq, gk, gv, la, gr, s0, g_out, spread)


def _mlp_kernel(x_ref, a_ref, b_ref, g1_ref, sh2_ref, sc2_ref, g2_ref, gn_ref, wo_ref, wu_ref, wd_ref, y_ref,
                *, gpt, tf):
    tm, d = x_ref.shape

    def per_group(val, ref, scale_plus_one=False):
        m = ref[...]
        if scale_plus_one:
            m = 1.0 + m
        return (val.reshape(gpt, tm // gpt, d) * m).reshape(tm, d)

    mix = jnp.concatenate([a_ref[...], b_ref[...]], axis=1)
    x1 = x_ref[...] + per_group(_dot(mix, wo_ref[...]), g1_ref)
    xn = x1 * lax.rsqrt(jnp.mean(x1 * x1, axis=-1, keepdims=True) + EPS) * gn_ref[...]
    h2 = (per_group(xn, sc2_ref, True).reshape(gpt, tm // gpt, d) + sh2_ref[...]).reshape(tm, d).astype(BF16)
    acc = jnp.zeros((tm, d), F32)
    for j in range(wu_ref.shape[1] // tf):
        u = jnp.maximum(_dot(h2, wu_ref[:, j * tf:(j + 1) * tf]), 0.0)
        acc += _dot((u * u).astype(BF16), wd_ref[j * tf:(j + 1) * tf, :])
    y_ref[...] = x1 + per_group(acc, g2_ref)


def _mlp(x2, a_out, b_out, mod4, w, *, rows_per_group, tm):
    n, d = x2.shape
    gpt = max(1, tm // rows_per_group)
    tpg = max(1, rows_per_group // tm)

    def mod_spec(j):
        return pl.BlockSpec((gpt, None, 1, d), lambda i: ((i // tpg) if gpt == 1 else i, j, 0, 0))

    def rows(c):
        return pl.BlockSpec((tm, c), lambda i: (i, 0))

    return pl.pallas_call(
        functools.partial(_mlp_kernel, gpt=gpt, tf=1024),
        grid=(n // tm,),
        in_specs=[rows(d), rows(MLA_WIDTH), rows(GLA_WIDTH), mod_spec(2), mod_spec(3), mod_spec(4), mod_spec(5),
                  _const((1, d)), _const(w["w_out"].shape), _const(w["w_up"].shape), _const(w["w_down"].shape)],
        out_specs=rows(d),
        out_shape=jax.ShapeDtypeStruct((n, d), F32),
        compiler_params=_params("arbitrary"),
        name="out_proj_mlp",
    )(x2, a_out, b_out, mod4, mod4, mod4, mod4, w["g_norm2"], w["w_out"], w["w_up"], w["w_down"])


def _rope_table(start, count, repeat=1):
    half = QK_ROPE // 2
    inv = ROPE_THETA ** (-np.arange(half, dtype=np.float64) / half)
    ang = (start + np.arange(count, dtype=np.float64))[:, None] * inv[None, :]
    c, s, z = np.cos(ang), np.sin(ang), np.zeros_like(ang)
    tab = np.concatenate([c, c, z, z, -s, z, z, z, z, s, z, z], axis=1).astype(np.float32)
    return jnp.asarray(np.tile(tab, (repeat, 1)))


def _pad_gain(g_rope):
    return jnp.concatenate([g_rope, jnp.zeros((LANES - QK_ROPE,), F32)]).reshape(1, LANES)


def _relayout_kernel(wint_ref, wuq_ref, wukv_ref, wg_ref, win_o, wuq_o, wukv_o, wukt_o, wuv_o, wg_o):
    s = np.cumsum([0, Q_LORA, KV_LORA, QK_ROPE, GLA_QK, GLA_QK, GLA_WIDTH, GLA_GATE_RANK, GLA_WIDTH])

    def piece(i):
        return wint_ref[int(s[i]):int(s[i + 1]), :]

    d = wint_ref.shape[1]
    zeros = jnp.zeros((LANES - QK_ROPE - GLA_GATE_RANK, d), F32)
    win_o[:, 0:Q_LORA] = piece(0).T.astype(BF16)
    win_o[:, Q_LORA:C_QKR[1]] = jnp.concatenate([piece(2), piece(6), zeros], axis=0).T.astype(BF16)
    win_o[:, C_KV[0]:C_KV[1]] = piece(1).T.astype(BF16)
    win_o[:, C_GQ[0]:C_GQ[1]] = (piece(3).T * (GLA_DK ** -0.5)).astype(BF16)
    win_o[:, C_GK[0]:C_GK[1]] = piece(4).T.astype(BF16)
    win_o[:, C_GV[0]:C_GV[1]] = piece(5).T.astype(BF16)
    win_o[:, C_GR[0]:C_GR[1]] = piece(7).T.astype(BF16)

    zq = jnp.zeros((Q_LORA, HEAD_PAD - QK_DIM), BF16)
    kvw = QK_NOPE + V_DIM
    for hh in range(MLA_HEADS):
        wuq_o[:, hh * HEAD_PAD:hh * HEAD_PAD + QK_DIM] = wuq_ref[:, hh * QK_DIM:(hh + 1) * QK_DIM].astype(BF16)
        wuq_o[:, hh * HEAD_PAD + QK_DIM:(hh + 1) * HEAD_PAD] = zq
        uk = wukv_ref[:, hh * kvw:hh * kvw + QK_NOPE]
        uv = wukv_ref[:, hh * kvw + QK_NOPE:(hh + 1) * kvw].astype(BF16)
        wukv_o[:, hh * QK_NOPE:(hh + 1) * QK_NOPE] = uk.astype(BF16)
        wukv_o[:, (MLA_HEADS + hh) * V_DIM:(MLA_HEADS + hh + 1) * V_DIM] = uv
        wukt_o[hh * QK_NOPE:(hh + 1) * QK_NOPE, :] = uk.T.astype(BF16)
        wuv_o[:, hh * V_DIM:(hh + 1) * V_DIM] = uv

    wg_o[...] = jnp.zeros(wg_o.shape, BF16)
    wg_o[QK_ROPE:QK_ROPE + GLA_GATE_RANK, :] = wg_ref[...].astype(BF16)


def _relayout(w_in_t, w_uq, w_ukv, w_gate_up):
    d = w_in_t.shape[1]
    shapes = [(d, C_GR[1]), (Q_LORA, MLA_HEADS * HEAD_PAD), (KV_LORA, MLA_HEADS * (QK_NOPE + V_DIM)),
              (MLA_HEADS * QK_NOPE, KV_LORA), (KV_LORA, MLA_WIDTH), (LANES, GLA_QK)]
    return pl.pallas_call(
        _relayout_kernel,
        out_shape=[jax.ShapeDtypeStruct(sh, BF16) for sh in shapes],
        compiler_params=pltpu.CompilerParams(vmem_limit_bytes=VMEM_LIMIT),
        name="weight_relayout",
    )(w_in_t, w_uq, w_ukv, w_gate_up)


def _prep_weights(w_in, g_norm1, g_q_lora, w_uq, g_kv_lora, w_ukv, g_q_head, g_k_head,
                  w_gate_up, b_gate_up, g_gla_out, w_out, g_norm2, w_up, w_down):
    d = w_in.shape[0]
    w_in_p, w_uq_p, w_ukv_p, w_ukt, w_uv, w_gate = _relayout(w_in.T, w_uq, w_ukv, w_gate_up)
    qscale = QK_DIM ** -0.5 * LOG2E
    bound = 1.02 * QK_DIM ** 0.5 * LOG2E * jnp.max(jnp.abs(g_q_head)) * jnp.max(jnp.abs(g_k_head))
    lane = jnp.arange(LANES) == QK_ROPE
    return {
        "w_in": w_in_p, "g_norm1": g_norm1.reshape(1, d), "g_q_lora": g_q_lora.reshape(1, Q_LORA),
        "w_uq": w_uq_p, "g_kv_lora": g_kv_lora.reshape(1, KV_LORA), "w_ukv": w_ukv_p, "w_ukt": w_ukt, "w_uv": w_uv,
        "gkr_col": g_k_head[QK_NOPE:].reshape(QK_ROPE, 1),
        "qone": lane.astype(F32).reshape(1, LANES), "kbias": jnp.where(lane, -bound, 0.0).reshape(1, LANES),
        "fast_softmax": (bound <= MAX_FIXED_SHIFT).astype(jnp.int32).reshape(1), "bound": bound.reshape(1),
        "gqn": (g_q_head[:QK_NOPE] * qscale).reshape(1, LANES), "gqr": _pad_gain(g_q_head[QK_NOPE:] * qscale),
        "gkn": g_k_head[:QK_NOPE].reshape(1, LANES), "gkr": _pad_gain(g_k_head[QK_NOPE:]),
        "w_gate": w_gate, "b_gate": b_gate_up.reshape(1, GLA_QK),
        "g_gla_out": g_gla_out.reshape(1, GLA_WIDTH), "w_out": w_out.astype(BF16),
        "g_norm2": g_norm2.reshape(1, d), "w_up": w_up.astype(BF16), "w_down": w_down.astype(BF16),
    }


def _layer(x, mod, past_lat, past_kr, s0, w, *, tm):
    batch, seq, d = x.shape
    n = batch * seq
    past = 0 if past_lat is None else past_lat.shape[1]
    x2 = x.reshape(n, d)
    mod4 = mod.reshape(batch, 6, 1, d)
    tm = min(tm, n)
    tab = _rope_table(past, seq, repeat=max(1, tm // seq))
    if past == 0:
        lat, krt, q, k, v, gq, gk, gv, la, gr = _projection(x2, mod4, tab, w, rows_per_group=seq, tm=tm, prompt=True)
        kr = jnp.swapaxes(krt, 1, 2)
        a_out = _attention_prompt(w["fast_softmax"], q, k, v, batch=batch, seq=seq, tq=min(512, seq))
    else:
        assert seq == CHUNK and past % CHUNK == 0
        lat, kr, krt, q, gq, gk, gv, la, gr = _projection(x2, mod4, tab, w, rows_per_group=seq, tm=tm, prompt=False)
        a_out = _attention_sample(q, past_lat, jnp.swapaxes(past_kr, 1, 2), lat, krt, w, tkb=min(512, past))
    b_out, s_new = _gla(gq, gk, gv, la, gr, s0, w["g_gla_out"], groups=batch, rows_per_group=seq,
                        tc=tm)
    y = _mlp(x2, a_out, b_out, mod4, w, rows_per_group=seq, tm=tm)
    return (y.reshape(batch, seq, d), lat.reshape(batch, seq, KV_LORA), kr.reshape(batch, seq, QK_ROPE), s_new)


def kernel(x_prompt, x_sample, cache_mla_latent, cache_mla_krope, state_gla, c_prompt, c_sample,
           w_ada, b_ada, g_norm1, w_in, g_q_lora, w_uq, g_kv_lora, w_ukv, g_q_head, g_k_head,
           w_gate_up, b_gate_up, g_gla_out, w_out, g_norm2, w_up, w_down):
    nb = x_prompt.shape[0]
    depth = w_ada.shape[0]
    y_p, y_s = x_prompt, x_sample
    outs = [[] for _ in range(6)]
    c_all = jnp.concatenate([c_prompt, c_sample], axis=0)
    for l in range(depth):
        w = _prep_weights(w_in[l], g_norm1[l], g_q_lora[l], w_uq[l], g_kv_lora[l], w_ukv[l], g_q_head[l],
                          g_k_head[l], w_gate_up[l], b_gate_up[l], g_gla_out[l], w_out[l], g_norm2[l],
                          w_up[l], w_down[l])
        mod = _modulation(c_all, w_ada[l], b_ada[l])
        zero_state = jnp.zeros((nb, GLA_HEADS, GLA_DK, GLA_DV), x_prompt.dtype)
        y_p, lat, kr, st = _layer(y_p, mod[:nb], None, None, zero_state, w, tm=512)
        outs[0].append(lat); outs[1].append(kr); outs[2].append(st)
        y_s, lat, kr, st = _layer(y_s, mod[nb:], cache_mla_latent[l], cache_mla_krope[l], state_gla[l], w, tm=512)
        outs[3].append(lat); outs[4].append(kr); outs[5].append(st)
    return (y_p, y_s) + tuple(jnp.stack(o) for o in outs)
```

```python
import functools

import jax
import jax.numpy as jnp
import numpy as np
from jax import lax
from jax.experimental import pallas as pl
from jax.experimental.pallas import tpu as pltpu

F32 = jnp.float32
BF16 = jnp.bfloat16

CHUNK = 64
EPS = 1e-6
MLA_HEADS = 4
Q_LORA = 384
KV_LORA = 256
QK_NOPE = 128
QK_ROPE = 64
QK_DIM = QK_NOPE + QK_ROPE
V_DIM = 128
ROPE_THETA = 10000.0
GLA_HEADS = 4
GLA_DK = 64
GLA_DV = 128
GLA_GATE_RANK = 16
GLA_TAU = 16.0
GLA_QK = GLA_HEADS * GLA_DK
GLA_WIDTH = GLA_HEADS * GLA_DV
MLA_WIDTH = MLA_HEADS * V_DIM
HEAD_PAD = 256
SUB = 8
LOG2E = 1.4426950408889634
MAX_FIXED_SHIFT = 48.0

LANES = 128
VMEM_LIMIT = 56 * 1024 * 1024

C_QKR = (0, 512)
C_KV = (512, 768)
C_GQ = (768, 1024)
C_GK = (1024, 1280)
C_GV = (1280, 1792)
C_GR = (1792, 2304)


def _dot(a, b):
    return jnp.dot(a, b, preferred_element_type=F32)


def _dot_nt(a, b):
    return lax.dot_general(a, b, (((1,), (1,)), ((), ())), preferred_element_type=F32)


def _dot_tn(a, b):
    return lax.dot_general(a, b, (((0,), (0,)), ((), ())), preferred_element_type=F32)


def _rope_tile(t, c, sa, sb):
    return t * c + pltpu.roll(t, 96, 1) * sa + pltpu.roll(t, 32, 1) * sb


def _params(*sem):
    return pltpu.CompilerParams(dimension_semantics=sem, vmem_limit_bytes=VMEM_LIMIT)


def _const(shape):
    return pl.BlockSpec(shape, lambda *_: (0,) * len(shape), pipeline_mode=pl.Buffered(1))


def _mod_kernel(c_ref, w_ref, b_ref, o_ref):
    c = c_ref[...]
    s = (c * jax.nn.sigmoid(c)).astype(BF16)
    o_ref[...] = _dot(s, w_ref[...].astype(BF16)) + b_ref[...]


def _modulation(c_all, w_ada, b_ada):
    g, d = c_all.shape
    n = w_ada.shape[1]
    tn = 1024
    return pl.pallas_call(
        _mod_kernel,
        grid=(n // tn,),
        in_specs=[pl.BlockSpec((g, d), lambda j: (0, 0)),
                  pl.BlockSpec((d, tn), lambda j: (0, j)),
                  pl.BlockSpec((1, tn), lambda j: (0, j))],
        out_specs=pl.BlockSpec((g, tn), lambda j: (0, j)),
        out_shape=jax.ShapeDtypeStruct((g, n), F32),
        compiler_params=_params("arbitrary"),
        name="adaln_mod",
    )(c_all, w_ada, b_ada.reshape(1, n))


def _proj_kernel(x_ref, sh_ref, sc_ref, g1_ref, win_ref, gql_ref, wuq_ref, gkv_ref, wukv_ref,
                 gqn_ref, gqr_ref, gkn_ref, gkr_ref, qone_ref, kbias_ref, tab_ref, wg_ref, bg_ref,
                 *outs, gpt, prompt):
    if prompt:
        lat_ref, krt_ref, q_ref, k_ref, v_ref, gq_ref, gk_ref, gv_ref, la_ref, gr_ref = outs
    else:
        lat_ref, kr_ref, krt_ref, q_ref, gq_ref, gk_ref, gv_ref, la_ref, gr_ref = outs
    tm, d = x_ref.shape
    x = x_ref[...]
    xn = x * lax.rsqrt(jnp.mean(x * x, axis=-1, keepdims=True) + EPS)
    h = (xn.reshape(gpt, tm // gpt, d) * (g1_ref[...] * (1.0 + sc_ref[...])) + sh_ref[...]).reshape(tm, d)
    hb = h.astype(BF16)

    def col(c):
        return _dot(hb, win_ref[:, c[0]:c[1]])

    tab = tab_ref[...]
    cos, sa, sb = tab[:, 0:LANES], tab[:, LANES:2 * LANES], tab[:, 2 * LANES:3 * LANES]

    qkr = col(C_QKR)
    krg = qkr[:, Q_LORA:]
    z = _dot(krg.astype(BF16), wg_ref[...]) + bg_ref[...]
    la_ref[...] = (jnp.minimum(z, 0.0) - jnp.log(1.0 + jnp.exp(-jnp.abs(z)))) * (1.0 / GLA_TAU)
    r = col(C_GR)
    gr_ref[...] = (r * jax.nn.sigmoid(r)).astype(BF16)

    cq = qkr[:, 0:Q_LORA]
    cqn = cq * lax.rsqrt(jnp.mean(cq * cq, axis=-1, keepdims=True) + EPS) * gql_ref[...]
    qp = _dot(cqn.astype(BF16), wuq_ref[...])
    for hh in range(MLA_HEADS):
        nope = qp[:, hh * HEAD_PAD:hh * HEAD_PAD + QK_NOPE]
        rt = qp[:, hh * HEAD_PAD + QK_NOPE:(hh + 1) * HEAD_PAD]
        ss = jnp.sum(nope * nope, axis=-1, keepdims=True) + jnp.sum(rt * rt, axis=-1, keepdims=True)
        inv = lax.rsqrt(ss * (1.0 / QK_DIM) + EPS)
        q_ref[:, hh * HEAD_PAD:hh * HEAD_PAD + QK_NOPE] = (nope * inv * gqn_ref[...]).astype(BF16)
        rq = _rope_tile(rt * inv * gqr_ref[...], cos, sa, sb)
        q_ref[:, hh * HEAD_PAD + QK_NOPE:(hh + 1) * HEAD_PAD] = (rq + qone_ref[...]).astype(BF16)

    ckv = col(C_KV)
    lat = ckv * lax.rsqrt(jnp.mean(ckv * ckv, axis=-1, keepdims=True) + EPS) * gkv_ref[...]
    lat_ref[...] = lat
    krt_ref[...] = krg.T[0:QK_ROPE, :]
    if prompt:
        lane = lax.broadcasted_iota(jnp.int32, (1, LANES), 1)
        krm = jnp.where(lane < QK_ROPE, krg, 0.0)
        ssr = jnp.sum(krm * krm, axis=-1, keepdims=True)
        rk = _rope_tile(krm * gkr_ref[...], cos, sa, sb)
        kv = _dot(lat.astype(BF16), wukv_ref[...])
        for hh in range(MLA_HEADS):
            kn = kv[:, hh * QK_NOPE:(hh + 1) * QK_NOPE]
            inv = lax.rsqrt((jnp.sum(kn * kn, axis=-1, keepdims=True) + ssr) * (1.0 / QK_DIM) + EPS)
            k_ref[:, hh * HEAD_PAD:hh * HEAD_PAD + QK_NOPE] = (kn * inv * gkn_ref[...]).astype(BF16)
            k_ref[:, hh * HEAD_PAD + QK_NOPE:(hh + 1) * HEAD_PAD] = (rk * inv + kbias_ref[...]).astype(BF16)
        v_ref[...] = kv[:, MLA_HEADS * QK_NOPE:].astype(BF16)
    else:
        kr_ref[...] = krg[:, 0:QK_ROPE]

    gq_ref[...] = col(C_GQ)
    gk_ref[...] = col(C_GK)
    gv_ref[...] = col(C_GV).astype(BF16)


def _projection(x2, mod4, tab, w, *, rows_per_group, tm, prompt):
    n, d = x2.shape
    gpt = max(1, tm // rows_per_group)
    tpg = max(1, rows_per_group // tm)
    ntab = tab.shape[0] // tm

    def mod_spec(j):
        return pl.BlockSpec((gpt, None, 1, d), lambda i: ((i // tpg) if gpt == 1 else i, j, 0, 0))

    def rows(c):
        return pl.BlockSpec((tm, c), lambda i: (i, 0))

    def out(c, t):
        return rows(c), jax.ShapeDtypeStruct((n, c), t)

    gla_outs = [out(GLA_QK, F32), out(GLA_QK, F32), out(GLA_WIDTH, BF16), out(GLA_QK, F32), out(GLA_WIDTH, BF16)]
    qo = out(MLA_HEADS * HEAD_PAD, BF16)
    if prompt:
        krt = (pl.BlockSpec((None, QK_ROPE, tm), lambda i: (i // tpg, 0, i % tpg)),
               jax.ShapeDtypeStruct((n // rows_per_group, QK_ROPE, rows_per_group), F32))
        outs = [out(KV_LORA, F32), krt, qo, out(MLA_HEADS * HEAD_PAD, BF16), out(MLA_WIDTH, BF16)] + gla_outs
    else:
        krt = (pl.BlockSpec((QK_ROPE, tm), lambda i: (0, i)), jax.ShapeDtypeStruct((QK_ROPE, n), F32))
        outs = [out(KV_LORA, F32), out(QK_ROPE, F32), krt, qo] + gla_outs
    return pl.pallas_call(
        functools.partial(_proj_kernel, gpt=gpt, prompt=prompt),
        grid=(n // tm,),
        in_specs=[rows(d), mod_spec(0), mod_spec(1), _const((1, d)), _const(w["w_in"].shape),
                  _const((1, Q_LORA)), _const(w["w_uq"].shape), _const((1, KV_LORA)), _const(w["w_ukv"].shape),
                  _const((1, LANES)), _const((1, LANES)), _const((1, LANES)), _const((1, LANES)),
                  _const((1, LANES)), _const((1, LANES)),
                  pl.BlockSpec((tm, 3 * LANES), lambda i: (i % ntab, 0)),
                  _const(w["w_gate"].shape), _const((1, GLA_QK))],
        out_specs=[o[0] for o in outs],
        out_shape=[o[1] for o in outs],
        compiler_params=_params("arbitrary"),
        name="in_proj",
    )(x2, mod4, mod4, w["g_norm1"], w["w_in"], w["g_q_lora"], w["w_uq"], w["g_kv_lora"], w["w_ukv"],
      w["gqn"], w["gqr"], w["gkn"], w["gkr"], w["qone"], w["kbias"], tab, w["w_gate"], w["b_gate"])


def _attn_prompt_kernel(fast_ref, q_ref, k_ref, v_ref, o_ref, vx_ref, m_ref, acc_ref, *, tq):
    i = pl.program_id(1)
    heads = range(MLA_HEADS)

    @pl.when(i == 0)
    def _():
        for hh in heads:
            vx_ref[hh, :, 0:V_DIM] = v_ref[:, hh * V_DIM:(hh + 1) * V_DIM]
            vx_ref[hh, :, V_DIM:] = jnp.ones((v_ref.shape[0], V_DIM), BF16)

    acc_ref[...] = jnp.zeros(acc_ref.shape, F32)

    def scores(j, hh, masked):
        hs = slice(hh * HEAD_PAD, (hh + 1) * HEAD_PAD)
        s = _dot_nt(q_ref[:, hs], k_ref[pl.ds(pl.multiple_of(j * tq, tq), tq), hs])
        if masked:
            qc = lax.broadcasted_iota(jnp.int32, (tq, tq), 0) // CHUNK
            kc = lax.broadcasted_iota(jnp.int32, (tq, tq), 1) // CHUNK
            s = jnp.where(kc <= qc, s, -jnp.inf)
        return s

    def vblock(j, hh):
        return vx_ref[hh, pl.ds(pl.multiple_of(j * tq, tq), tq), :]

    def fast_block(j, masked):
        for hh in heads:
            acc_ref[hh] += _dot(jnp.exp2(scores(j, hh, masked)).astype(BF16), vblock(j, hh))

    def safe_block(j, masked):
        for hh in heads:
            s = scores(j, hh, masked)
            m = m_ref[hh]
            m_new = jnp.maximum(m, jnp.max(s, axis=1, keepdims=True))
            p = jnp.exp2(s - m_new).astype(BF16)
            acc_ref[hh] = jnp.exp2(m - m_new) * acc_ref[hh] + _dot(p, vblock(j, hh))
            m_ref[hh] = m_new

    def sweep(block):
        def body(jj, c):
            block(2 * jj, False)
            block(2 * jj + 1, False)
            return c

        lax.fori_loop(0, i // 2, body, 0)

        @pl.when(i % 2 == 1)
        def _():
            block(i - 1, False)

        block(i, True)

    @pl.when(fast_ref[0] == 1)
    def _():
        sweep(fast_block)

    @pl.when(fast_ref[0] != 1)
    def _():
        m_ref[...] = jnp.full(m_ref.shape, -jnp.inf, F32)
        sweep(safe_block)

    for hh in heads:
        acc = acc_ref[hh]
        o_ref[:, hh * V_DIM:(hh + 1) * V_DIM] = (acc[:, 0:V_DIM] / acc[:, V_DIM:]).astype(BF16)


def _attention_prompt(fast, q, k, v, *, batch, seq, tq):
    nq = seq // tq
    return pl.pallas_call(
        functools.partial(_attn_prompt_kernel, tq=tq),
        grid=(batch, nq),
        in_specs=[pl.BlockSpec(memory_space=pltpu.SMEM),
                  pl.BlockSpec((tq, MLA_HEADS * HEAD_PAD), lambda b, i: (b * nq + i, 0)),
                  pl.BlockSpec((seq, MLA_HEADS * HEAD_PAD), lambda b, i: (b, 0)),
                  pl.BlockSpec((seq, MLA_WIDTH), lambda b, i: (b, 0))],
        out_specs=pl.BlockSpec((tq, MLA_WIDTH), lambda b, i: (b * nq + i, 0)),
        out_shape=jax.ShapeDtypeStruct((batch * seq, MLA_WIDTH), BF16),
        scratch_shapes=[pltpu.VMEM((MLA_HEADS, seq, 2 * V_DIM), BF16), pltpu.VMEM((MLA_HEADS, tq, 1), F32),
                        pltpu.VMEM((MLA_HEADS, tq, 2 * V_DIM), F32)],
        compiler_params=_params("arbitrary", "arbitrary"),
        name="mla_attn_prompt",
    )(fast, q, k, v)


def _attn_sample_kernel(fast_ref, bound_ref, q_ref, lat_ref, krt_ref, latn_ref, krtn_ref, tab_ref, wukt_ref, wuv_ref,
                        gkn_ref, gkr_ref, o_ref, wq_ref, latb_ref, s_ref, *, tkb):
    b = pl.program_id(0)
    t = q_ref.shape[0]
    past = lat_ref.shape[0]
    nk = MLA_HEADS * QK_NOPE
    half = QK_ROPE // 2

    @pl.when(b == 0)
    def _():
        wq_ref[0:nk, :] = wukt_ref[...]

    qr = []
    for hh in range(MLA_HEADS):
        qn = (q_ref[:, hh * HEAD_PAD:hh * HEAD_PAD + QK_NOPE].astype(F32) * gkn_ref[...]).astype(BF16)
        wq_ref[nk + hh * t:nk + (hh + 1) * t, :] = _dot(qn, wukt_ref[hh * QK_NOPE:(hh + 1) * QK_NOPE, :]).astype(BF16)
        qr.append(q_ref[:, hh * HEAD_PAD + QK_NOPE:(hh + 1) * HEAD_PAD])
    qr = jnp.concatenate(qr, axis=0)

    def key_block(lat, krt, tab, width, valid):
        latb = lat.astype(BF16)
        g = _dot_nt(wq_ref[...], latb)
        ssr = jnp.sum(krt * krt, axis=0, keepdims=True)
        kg = krt * gkr_ref[...]
        x1, x2 = kg[0:half], kg[half:QK_ROPE]
        c, sn = tab[0:half], tab[half:QK_ROPE]
        rope = jnp.concatenate([x1 * c - x2 * sn, x2 * c + x1 * sn, jnp.zeros((LANES - QK_ROPE, width), F32)], axis=0)
        srope = _dot(qr, rope.astype(BF16))
        rows = []
        for hh in range(MLA_HEADS):
            kn = g[hh * QK_NOPE:(hh + 1) * QK_NOPE]
            inv = lax.rsqrt((jnp.sum(kn * kn, axis=0, keepdims=True) + ssr) * (1.0 / QK_DIM) + EPS)
            rows.append((g[nk + hh * t:nk + (hh + 1) * t] + srope[hh * t:(hh + 1) * t]) * inv)
        sc = jnp.concatenate(rows, axis=0) - bound_ref[0]
        if valid is not None:
            sc = jnp.where(valid, sc, -jnp.inf)
        return sc, latb

    mine = lax.broadcasted_iota(jnp.int32, (1, 2 * t), 1) // t == b % 2

    def blocks():
        for blk in range(past // tkb):
            c0 = blk * tkb
            yield c0, tkb, key_block(lat_ref[c0:c0 + tkb, :], krt_ref[:, c0:c0 + tkb], tab_ref[:, c0:c0 + tkb],
                                     tkb, None)
        yield past, 2 * t, key_block(latn_ref[...], krtn_ref[...], tab_ref[:, past:past + 2 * t], 2 * t, mine)

    def finish(wlat, l):
        wlat = wlat.astype(BF16)
        for hh in range(MLA_HEADS):
            o = _dot(wlat[hh * t:(hh + 1) * t], wuv_ref[:, hh * V_DIM:(hh + 1) * V_DIM])
            o_ref[:, hh * V_DIM:(hh + 1) * V_DIM] = (o / l[hh * t:(hh + 1) * t]).astype(BF16)

    @pl.when(fast_ref[0] == 1)
    def _():
        wlat = jnp.zeros((MLA_HEADS * t, KV_LORA), F32)
        lsum = jnp.zeros((MLA_HEADS * t, LANES), F32)
        for _, width, (sc, latb) in blocks():
            p = jnp.exp2(sc)
            wlat += _dot(p.astype(BF16), latb)
            for c in range(width // LANES):
                lsum += p[:, c * LANES:(c + 1) * LANES]
        finish(wlat, jnp.sum(lsum, axis=1, keepdims=True))

    @pl.when(fast_ref[0] != 1)
    def _():
        for c0, width, (sc, latb) in blocks():
            s_ref[:, c0:c0 + width] = sc
            latb_ref[c0:c0 + width, :] = latb
        s = s_ref[...]
        p = jnp.exp2(s - jnp.max(s, axis=1, keepdims=True))
        finish(_dot(p.astype(BF16), latb_ref[...]), jnp.sum(p, axis=1, keepdims=True))


def _attention_sample(q, past_lat, past_krt, lat_new, krt_new, w, *, tkb):
    batch, past, _ = past_lat.shape
    t = q.shape[0] // batch
    half = QK_ROPE // 2
    inv = ROPE_THETA ** (-np.arange(half, dtype=np.float64) / half)
    pos = np.concatenate([np.arange(past), past + np.arange(t), past + np.arange(t)]).astype(np.float64)
    ang = inv[:, None] * pos[None, :]
    tab = jnp.asarray(np.concatenate([np.cos(ang), np.sin(ang)], axis=0).astype(np.float32))
    s_pad = past + 2 * t
    return pl.pallas_call(
        functools.partial(_attn_sample_kernel, tkb=tkb),
        grid=(batch,),
        in_specs=[pl.BlockSpec(memory_space=pltpu.SMEM), pl.BlockSpec(memory_space=pltpu.SMEM),
                  pl.BlockSpec((t, MLA_HEADS * HEAD_PAD), lambda b: (b, 0)),
                  pl.BlockSpec((None, past, KV_LORA), lambda b: (b, 0, 0)),
                  pl.BlockSpec((None, QK_ROPE, past), lambda b: (b, 0, 0)),
                  pl.BlockSpec((2 * t, KV_LORA), lambda b: (b // 2, 0)),
                  pl.BlockSpec((QK_ROPE, 2 * t), lambda b: (0, b // 2)),
                  _const(tab.shape), _const(w["w_ukt"].shape), _const(w["w_uv"].shape),
                  _const((1, LANES)), _const((QK_ROPE, 1))],
        out_specs=pl.BlockSpec((t, MLA_WIDTH), lambda b: (b, 0)),
        out_shape=jax.ShapeDtypeStruct((batch * t, MLA_WIDTH), BF16),
        scratch_shapes=[pltpu.VMEM((MLA_HEADS * (QK_NOPE + t), KV_LORA), BF16),
                        pltpu.VMEM((s_pad, KV_LORA), BF16),
                        pltpu.VMEM((MLA_HEADS * t, s_pad), F32)],
        compiler_params=_params("arbitrary"),
        name="mla_attn_sample",
    )(w["fast_softmax"], w["bound"], q, past_lat, past_krt, lat_new, krt_new, tab, w["w_ukt"], w["w_uv"], w["gkn"],
      w["gkr_col"])


def _gla_kernel(q_ref, k_ref, v_ref, la_ref, r_ref, s0_ref, g_ref, spread_ref, o_ref, sn_ref,
                st_ref, kp_ref, ap_ref, p_ref, on_ref, *, gpt):
    t_idx = pl.program_id(1)
    L, W, P = CHUNK, GLA_QK, LANES
    R = q_ref.shape[0]
    n_chunks = R // L
    cpg = n_chunks // gpt
    n_pairs = GLA_HEADS // 2

    lane_p = lax.broadcasted_iota(jnp.int32, (1, P), 1)
    even = lane_p < GLA_DK
    bd_mask = (lax.broadcasted_iota(jnp.int32, (2 * GLA_DV, P), 0) // GLA_DV
               == lax.broadcasted_iota(jnp.int32, (2 * GLA_DV, P), 1) // GLA_DK)

    @pl.when(t_idx == 0)
    def _():
        kp_ref[0:SUB, :] = jnp.zeros((SUB, W), F32)
        ap_ref[0:SUB, :] = jnp.zeros((SUB, W), F32)
        for gi in range(gpt):
            for pr in range(n_pairs):
                tt = s0_ref[gi, 2 * pr:2 * pr + 2].reshape(2 * GLA_DK, GLA_DV).T
                st_ref[gi, pr] = jnp.where(bd_mask, jnp.concatenate([tt, tt], axis=0), 0.0)

    q = q_ref[...]
    k = k_ref[...]
    la = la_ref[...]

    tri = (lax.broadcasted_iota(jnp.int32, (L, L), 0) >= lax.broadcasted_iota(jnp.int32, (L, L), 1)).astype(BF16)
    la_hi = la.astype(BF16)
    la2 = jnp.concatenate([la_hi, (la - la_hi.astype(F32)).astype(BF16)], axis=1)
    bs = []
    for c in range(n_chunks):
        t2 = _dot(tri, la2[c * L:(c + 1) * L, :])
        bs.append(t2[:, 0:W] + t2[:, W:2 * W])
    b = (jnp.concatenate(bs, axis=0) if n_chunks > 1 else bs[0]) * LOG2E
    b3 = b.reshape(n_chunks, L, W)

    def chunk_row(r):
        return jnp.broadcast_to(b3[:, r:r + 1, :], (n_chunks, L, W)).reshape(R, W)

    b_sub = jnp.broadcast_to(b.reshape(R // SUB, SUB, W)[:, 0:1, :], (R // SUB, SUB, W)).reshape(R, W)
    sub = (lax.broadcasted_iota(jnp.int32, (R, W), 0) % L) // SUB

    qt = q * jnp.exp2(b - b_sub)
    zb = jnp.zeros((), BF16)
    zq = jnp.zeros((SUB, P), F32)
    ktm = []
    for i in range(1, L // SUB):
        kt = (k * jnp.exp2(chunk_row(i * SUB) - b)).astype(BF16)
        ktm.append([[jnp.where((sub[:, 0:P] < i) & (even if e == 0 else ~even), kt[:, pr * P:(pr + 1) * P], zb)
                     for e in range(2)] for pr in range(n_pairs)])

    a = jnp.exp(la)
    kp_ref[SUB:SUB + R, :] = k
    ap_ref[SUB:SUB + R, :] = a
    p_ref[:, 0:W] = (q * k).astype(BF16)
    e = a
    for d in range(1, SUB):
        if d > 1:
            e = e * ap_ref[SUB - d + 1:SUB - d + 1 + R, :]
        p_ref[:, d * W:(d + 1) * W] = (q * kp_ref[SUB - d:SUB - d + R, :] * e).astype(BF16)
    cband = _dot(p_ref[...], spread_ref[...])
    same_sub = (lax.broadcasted_iota(jnp.int32, (L, W), 0) // SUB
                == (lax.broadcasted_iota(jnp.int32, (L, W), 1) % L) // SUB)

    qe = (q * jnp.exp2(b)).astype(BF16)
    kd = (k * jnp.exp2(chunk_row(L - 1) - b)).astype(BF16)
    zv = jnp.zeros((L, 2 * GLA_DV), BF16)

    o_intra, d_st, dec = {}, {}, []
    for c in range(n_chunks):
        rs = slice(c * L, (c + 1) * L)
        a_band = jnp.where(same_sub, pltpu.roll(cband[rs], W - (SUB - 1), 1, stride=1, stride_axis=0), 0.0)
        dec.append(jnp.exp2(b[c * L + L - 1:c * L + L, :]))
        for pr in range(n_pairs):
            ls = slice(pr * P, (pr + 1) * P)
            lhs_c = jnp.concatenate(
                [jnp.concatenate([qt[c * L + r * SUB:c * L + (r + 1) * SUB, ls] if i == r else zq
                                  for i in range(1, L // SUB)], axis=1) for r in range(L // SUB)],
                axis=0).astype(BF16)
            rhs_c = jnp.concatenate([jnp.concatenate([m[pr][0][rs], m[pr][1][rs]], axis=0) for m in ktm], axis=1)
            a_tot = (a_band[:, ls] + _dot_nt(lhs_c, rhs_c)).astype(BF16)
            vp = v_ref[rs, 2 * pr * GLA_DV:(2 * pr + 2) * GLA_DV]
            v_bd = jnp.concatenate([jnp.concatenate([vp[:, 0:GLA_DV], zv[:, 0:GLA_DV]], axis=1),
                                    jnp.concatenate([zv[:, 0:GLA_DV], vp[:, GLA_DV:]], axis=1)], axis=0)
            o_intra[c, pr] = _dot(a_tot, v_bd)
            d_st[c, pr] = jnp.where(bd_mask, _dot_tn(vp, kd[rs, ls]), 0.0)
    st_in = {}
    for gi in range(gpt):
        for pr in range(n_pairs):
            st = st_ref[gi, pr]
            for c in range(gi * cpg, (gi + 1) * cpg):
                st_in[c, pr] = st.astype(BF16)
                st = st * dec[c][:, pr * P:(pr + 1) * P] + d_st[c, pr]
            st_ref[gi, pr] = st
    for c in range(n_chunks):
        rs = slice(c * L, (c + 1) * L)
        for pr in range(n_pairs):
            on_ref[rs, 2 * pr * GLA_DV:(2 * pr + 2) * GLA_DV] = (
                o_intra[c, pr] + _dot_nt(qe[rs, pr * P:(pr + 1) * P], st_in[c, pr]))

    for hh in range(GLA_HEADS):
        hs = slice(hh * GLA_DV, (hh + 1) * GLA_DV)
        o = on_ref[:, hs]
        on = o * lax.rsqrt(jnp.mean(o * o, axis=-1, keepdims=True) + EPS) * g_ref[:, hs]
        o_ref[:, hs] = (on * r_ref[:, hs].astype(F32)).astype(BF16)

    @pl.when(t_idx == pl.num_programs(1) - 1)
    def _():
        for gi in range(gpt):
            for pr in range(n_pairs):
                st = st_ref[gi, pr]
                tt = jnp.where(even, st[0:GLA_DV], st[GLA_DV:2 * GLA_DV])
                sn_ref[gi, 2 * pr:2 * pr + 2] = tt.T.reshape(2, GLA_DK, GLA_DV)


def _band_spread():
    m = np.zeros((SUB, GLA_HEADS, GLA_DK, GLA_QK), np.float32)
    for d in range(SUB):
        for h in range(GLA_HEADS):
            m[d, h, :, h * GLA_DK + SUB - 1 - d] = 1.0
    return jnp.asarray(m.reshape(SUB * GLA_QK, GLA_QK), BF16)


def _gla(gq, gk, gv, la, gr, s0, g_out, *, groups, rows_per_group, tc):
    gpt = max(1, tc // rows_per_group)
    nt = max(1, rows_per_group // tc)

    def rows(c):
        return pl.BlockSpec((tc, c), lambda g, t: (g * nt + t, 0))

    state = pl.BlockSpec((gpt, GLA_HEADS, GLA_DK, GLA_DV), lambda g, t: (g, 0, 0, 0))
    spread = _band_spread()
    return pl.pallas_call(
        functools.partial(_gla_kernel, gpt=gpt),
        grid=(groups // gpt, nt),
        in_specs=[rows(GLA_QK), rows(GLA_QK), rows(GLA_WIDTH), rows(GLA_QK), rows(GLA_WIDTH), state,
                  _const((1, GLA_WIDTH)), _const(spread.shape)],
        out_specs=[rows(GLA_WIDTH), state],
        out_shape=[jax.ShapeDtypeStruct((groups * rows_per_group, GLA_WIDTH), BF16),
                   jax.ShapeDtypeStruct((groups, GLA_HEADS, GLA_DK, GLA_DV), F32)],
        scratch_shapes=[pltpu.VMEM((gpt, GLA_HEADS // 2, 2 * GLA_DV, LANES), F32),
                        pltpu.VMEM((SUB + tc, GLA_QK), F32), pltpu.VMEM((SUB + tc, GLA_QK), F32),
                        pltpu.VMEM((tc, SUB * GLA_QK), BF16), pltpu.VMEM((tc, GLA_WIDTH), F32)],
        compiler_params=_params("arbitrary", "arbitrary"),
        name="gla",
    )(gq, gk, gv, la, gr, s0, g_out, spread)


def _mlp_kernel(x_ref, a_ref, b_ref, g1_ref, sh2_ref, sc2_ref, g2_ref, gn_ref, wo_ref, wu_ref, wd_ref, y_ref,
                *, gpt, tf):
    tm, d = x_ref.shape

    def per_group(val, ref, scale_plus_one=False):
        m = ref[...]
        if scale_plus_one:
            m = 1.0 + m
        return (val.reshape(gpt, tm // gpt, d) * m).reshape(tm, d)

    mix = jnp.concatenate([a_ref[...], b_ref[...]], axis=1)
    x1 = x_ref[...] + per_group(_dot(mix, wo_ref[...]), g1_ref)
    xn = x1 * lax.rsqrt(jnp.mean(x1 * x1, axis=-1, keepdims=True) + EPS) * gn_ref[...]
    h2 = (per_group(xn, sc2_ref, True).reshape(gpt, tm // gpt, d) + sh2_ref[...]).reshape(tm, d).astype(BF16)
    acc = jnp.zeros((tm, d), F32)
    for j in range(wu_ref.shape[1] // tf):
        u = jnp.maximum(_dot(h2, wu_ref[:, j * tf:(j + 1) * tf]), 0.0)
        acc += _dot((u * u).astype(BF16), wd_ref[j * tf:(j + 1) * tf, :])
    y_ref[...] = x1 + per_group(acc, g2_ref)


def _mlp(x2, a_out, b_out, mod4, w, *, rows_per_group, tm):
    n, d = x2.shape
    gpt = max(1, tm // rows_per_group)
    tpg = max(1, rows_per_group // tm)

    def mod_spec(j):
        return pl.BlockSpec((gpt, None, 1, d), lambda i: ((i // tpg) if gpt == 1 else i, j, 0, 0))

    def rows(c):
        return pl.BlockSpec((tm, c), lambda i: (i, 0))

    return pl.pallas_call(
        functools.partial(_mlp_kernel, gpt=gpt, tf=1024),
        grid=(n // tm,),
        in_specs=[rows(d), rows(MLA_WIDTH), rows(GLA_WIDTH), mod_spec(2), mod_spec(3), mod_spec(4), mod_spec(5),
                  _const((1, d)), _const(w["w_out"].shape), _const(w["w_up"].shape), _const(w["w_down"].shape)],
        out_specs=rows(d),
        out_shape=jax.ShapeDtypeStruct((n, d), F32),
        compiler_params=_params("arbitrary"),
        name="out_proj_mlp",
    )(x2, a_out, b_out, mod4, mod4, mod4, mod4, w["g_norm2"], w["w_out"], w["w_up"], w["w_down"])


def _rope_table(start, count, repeat=1):
    half = QK_ROPE // 2
    inv = ROPE_THETA ** (-np.arange(half, dtype=np.float64) / half)
    ang = (start + np.arange(count, dtype=np.float64))[:, None] * inv[None, :]
    c, s, z = np.cos(ang), np.sin(ang), np.zeros_like(ang)
    tab = np.concatenate([c, c, z, z, -s, z, z, z, z, s, z, z], axis=1).astype(np.float32)
    return jnp.asarray(np.tile(tab, (repeat, 1)))


def _pad_gain(g_rope):
    return jnp.concatenate([g_rope, jnp.zeros((LANES - QK_ROPE,), F32)]).reshape(1, LANES)


def _relayout_kernel(wint_ref, wuq_ref, wukv_ref, wg_ref, win_o, wuq_o, wukv_o, wukt_o, wuv_o, wg_o):
    s = np.cumsum([0, Q_LORA, KV_LORA, QK_ROPE, GLA_QK, GLA_QK, GLA_WIDTH, GLA_GATE_RANK, GLA_WIDTH])

    def piece(i):
        return wint_ref[int(s[i]):int(s[i + 1]), :]

    d = wint_ref.shape[1]
    zeros = jnp.zeros((LANES - QK_ROPE - GLA_GATE_RANK, d), F32)
    win_o[:, 0:Q_LORA] = piece(0).T.astype(BF16)
    win_o[:, Q_LORA:C_QKR[1]] = jnp.concatenate([piece(2), piece(6), zeros], axis=0).T.astype(BF16)
    win_o[:, C_KV[0]:C_KV[1]] = piece(1).T.astype(BF16)
    win_o[:, C_GQ[0]:C_GQ[1]] = (piece(3).T * (GLA_DK ** -0.5)).astype(BF16)
    win_o[:, C_GK[0]:C_GK[1]] = piece(4).T.astype(BF16)
    win_o[:, C_GV[0]:C_GV[1]] = piece(5).T.astype(BF16)
    win_o[:, C_GR[0]:C_GR[1]] = piece(7).T.astype(BF16)

    zq = jnp.zeros((Q_LORA, HEAD_PAD - QK_DIM), BF16)
    kvw = QK_NOPE + V_DIM
    for hh in range(MLA_HEADS):
        wuq_o[:, hh * HEAD_PAD:hh * HEAD_PAD + QK_DIM] = wuq_ref[:, hh * QK_DIM:(hh + 1) * QK_DIM].astype(BF16)
        wuq_o[:, hh * HEAD_PAD + QK_DIM:(hh + 1) * HEAD_PAD] = zq
        uk = wukv_ref[:, hh * kvw:hh * kvw + QK_NOPE]
        uv = wukv_ref[:, hh * kvw + QK_NOPE:(hh + 1) * kvw].astype(BF16)
        wukv_o[:, hh * QK_NOPE:(hh + 1) * QK_NOPE] = uk.astype(BF16)
        wukv_o[:, (MLA_HEADS + hh) * V_DIM:(MLA_HEADS + hh + 1) * V_DIM] = uv
        wukt_o[hh * QK_NOPE:(hh + 1) * QK_NOPE, :] = uk.T.astype(BF16)
        wuv_o[:, hh * V_DIM:(hh + 1) * V_DIM] = uv

    wg_o[...] = jnp.zeros(wg_o.shape, BF16)
    wg_o[QK_ROPE:QK_ROPE + GLA_GATE_RANK, :] = wg_ref[...].astype(BF16)


def _relayout(w_in_t, w_uq, w_ukv, w_gate_up):
    d = w_in_t.shape[1]
    shapes = [(d, C_GR[1]), (Q_LORA, MLA_HEADS * HEAD_PAD), (KV_LORA, MLA_HEADS * (QK_NOPE + V_DIM)),
              (MLA_HEADS * QK_NOPE, KV_LORA), (KV_LORA, MLA_WIDTH), (LANES, GLA_QK)]
    return pl.pallas_call(
        _relayout_kernel,
        out_shape=[jax.ShapeDtypeStruct(sh, BF16) for sh in shapes],
        compiler_params=pltpu.CompilerParams(vmem_limit_bytes=VMEM_LIMIT),
        name="weight_relayout",
    )(w_in_t, w_uq, w_ukv, w_gate_up)


def _prep_weights(w_in, g_norm1, g_q_lora, w_uq, g_kv_lora, w_ukv, g_q_head, g_k_head,
                  w_gate_up, b_gate_up, g_gla_out, w_out, g_norm2, w_up, w_down):
    d = w_in.shape[0]
    w_in_p, w_uq_p, w_ukv_p, w_ukt, w_uv, w_gate = _relayout(w_in.T, w_uq, w_ukv, w_gate_up)
    qscale = QK_DIM ** -0.5 * LOG2E
    bound = 1.02 * QK_DIM ** 0.5 * LOG2E * jnp.max(jnp.abs(g_q_head)) * jnp.max(jnp.abs(g_k_head))
    lane = jnp.arange(LANES) == QK_ROPE
    return {
        "w_in": w_in_p, "g_norm1": g_norm1.reshape(1, d), "g_q_lora": g_q_lora.reshape(1, Q_LORA),
        "w_uq": w_uq_p, "g_kv_lora": g_kv_lora.reshape(1, KV_LORA), "w_ukv": w_ukv_p, "w_ukt": w_ukt, "w_uv": w_uv,
        "gkr_col": g_k_head[QK_NOPE:].reshape(QK_ROPE, 1),
        "qone": lane.astype(F32).reshape(1, LANES), "kbias": jnp.where(lane, -bound, 0.0).reshape(1, LANES),
        "fast_softmax": (bound <= MAX_FIXED_SHIFT).astype(jnp.int32).reshape(1), "bound": bound.reshape(1),
        "gqn": (g_q_head[:QK_NOPE] * qscale).reshape(1, LANES), "gqr": _pad_gain(g_q_head[QK_NOPE:] * qscale),
        "gkn": g_k_head[:QK_NOPE].reshape(1, LANES), "gkr": _pad_gain(g_k_head[QK_NOPE:]),
        "w_gate": w_gate, "b_gate": b_gate_up.reshape(1, GLA_QK),
        "g_gla_out": g_gla_out.reshape(1, GLA_WIDTH), "w_out": w_out.astype(BF16),
        "g_norm2": g_norm2.reshape(1, d), "w_up": w_up.astype(BF16), "w_down": w_down.astype(BF16),
    }


def _layer(x, mod, past_lat, past_kr, s0, w, *, tm):
    batch, seq, d = x.shape
    n = batch * seq
    past = 0 if past_lat is None else past_lat.shape[1]
    x2 = x.reshape(n, d)
    mod4 = mod.reshape(batch, 6, 1, d)
    tm = min(tm, n)
    tab = _rope_table(past, seq, repeat=max(1, tm // seq))
    if past == 0:
        lat, krt, q, k, v, gq, gk, gv, la, gr = _projection(x2, mod4, tab, w, rows_per_group=seq, tm=tm, prompt=True)
        kr = jnp.swapaxes(krt, 1, 2)
        a_out = _attention_prompt(w["fast_softmax"], q, k, v, batch=batch, seq=seq, tq=min(512, seq))
    else:
        assert seq == CHUNK and past % CHUNK == 0
        lat, kr, krt, q, gq, gk, gv, la, gr = _projection(x2, mod4, tab, w, rows_per_group=seq, tm=tm, prompt=False)
        a_out = _attention_sample(q, past_lat, jnp.swapaxes(past_kr, 1, 2), lat, krt, w, tkb=min(512, past))
    b_out, s_new = _gla(gq, gk, gv, la, gr, s0, w["g_gla_out"], groups=batch, rows_per_group=seq,
                        tc=tm)
    y = _mlp(x2, a_out, b_out, mod4, w, rows_per_group=seq, tm=tm)
    return (y.reshape(batch, seq, d), lat.reshape(batch, seq, KV_LORA), kr.reshape(batch, seq, QK_ROPE), s_new)


def kernel(x_prompt, x_sample, cache_mla_latent, cache_mla_krope, state_gla, c_prompt, c_sample,
           w_ada, b_ada, g_norm1, w_in, g_q_lora, w_uq, g_kv_lora, w_ukv, g_q_head, g_k_head,
           w_gate_up, b_gate_up, g_gla_out, w_out, g_norm2, w_up, w_down):
    nb = x_prompt.shape[0]
    depth = w_ada.shape[0]
    y_p, y_s = x_prompt, x_sample
    outs = [[] for _ in range(6)]
    c_all = jnp.concatenate([c_prompt, c_sample], axis=0)
    for l in range(depth):
        w = _prep_weights(w_in[l], g_norm1[l], g_q_lora[l], w_uq[l], g_kv_lora[l], w_ukv[l], g_q_head[l],
                          g_k_head[l], w_gate_up[l], b_gate_up[l], g_gla_out[l], w_out[l], g_norm2[l],
                          w_up[l], w_down[l])
        mod = _modulation(c_all, w_ada[l], b_ada[l])
        zero_state = jnp.zeros((nb, GLA_HEADS, GLA_DK, GLA_DV), x_prompt.dtype)
        y_p, lat, kr, st = _layer(y_p, mod[:nb], None, None, zero_state, w, tm=512)
        outs[0].append(lat); outs[1].append(kr); outs[2].append(st)
        y_s, lat, kr, st = _layer(y_s, mod[nb:], cache_mla_latent[l], cache_mla_krope[l], state_gla[l], w, tm=512)
        outs[3].append(lat); outs[4].append(kr); outs[5].append(st)
    return (y_p, y_s) + tuple(jnp.stack(o) for o in outs)
```

```python
import functools

import jax
import jax.numpy as jnp
import numpy as np
from jax import lax
from jax.experimental import pallas as pl
from jax.experimental.pallas import tpu as pltpu

F32 = jnp.float32
BF16 = jnp.bfloat16

CHUNK = 64
EPS = 1e-6
MLA_HEADS = 4
Q_LORA = 384
KV_LORA = 256
QK_NOPE = 128
QK_ROPE = 64
QK_DIM = QK_NOPE + QK_ROPE
V_DIM = 128
ROPE_THETA = 10000.0
GLA_HEADS = 4
GLA_DK = 64
GLA_DV = 128
GLA_GATE_RANK = 16
GLA_TAU = 16.0
GLA_QK = GLA_HEADS * GLA_DK
GLA_WIDTH = GLA_HEADS * GLA_DV
MLA_WIDTH = MLA_HEADS * V_DIM
HEAD_PAD = 256
SUB = 8
LOG2E = 1.4426950408889634
MAX_FIXED_SHIFT = 48.0

LANES = 128
VMEM_LIMIT = 56 * 1024 * 1024

C_QKR = (0, 512)
C_KV = (512, 768)
C_GQ = (768, 1024)
C_GK = (1024, 1280)
C_GV = (1280, 1792)
C_GR = (1792, 2304)


def _dot(a, b):
    return jnp.dot(a, b, preferred_element_type=F32)


def _dot_nt(a, b):
    return lax.dot_general(a, b, (((1,), (1,)), ((), ())), preferred_element_type=F32)


def _dot_tn(a, b):
    return lax.dot_general(a, b, (((0,), (0,)), ((), ())), preferred_element_type=F32)


def _rope_tile(t, c, sa, sb):
    return t * c + pltpu.roll(t, 96, 1) * sa + pltpu.roll(t, 32, 1) * sb


def _params(*sem):
    return pltpu.CompilerParams(dimension_semantics=sem, vmem_limit_bytes=VMEM_LIMIT)


def _const(shape):
    return pl.BlockSpec(shape, lambda *_: (0,) * len(shape), pipeline_mode=pl.Buffered(1))


def _mod_kernel(c_ref, w_ref, b_ref, o_ref):
    c = c_ref[...]
    s = (c * jax.nn.sigmoid(c)).astype(BF16)
    o_ref[...] = _dot(s, w_ref[...].astype(BF16)) + b_ref[...]


def _modulation(c_all, w_ada, b_ada):
    g, d = c_all.shape
    n = w_ada.shape[1]
    tn = 1024
    return pl.pallas_call(
        _mod_kernel,
        grid=(n // tn,),
        in_specs=[pl.BlockSpec((g, d), lambda j: (0, 0)),
                  pl.BlockSpec((d, tn), lambda j: (0, j)),
                  pl.BlockSpec((1, tn), lambda j: (0, j))],
        out_specs=pl.BlockSpec((g, tn), lambda j: (0, j)),
        out_shape=jax.ShapeDtypeStruct((g, n), F32),
        compiler_params=_params("arbitrary"),
        name="adaln_mod",
    )(c_all, w_ada, b_ada.reshape(1, n))


def _proj_kernel(x_ref, sh_ref, sc_ref, g1_ref, win_ref, gql_ref, wuq_ref, gkv_ref, wukv_ref,
                 gqn_ref, gqr_ref, gkn_ref, gkr_ref, qone_ref, kbias_ref, tab_ref, wg_ref, bg_ref,
                 *outs, gpt, prompt):
    if prompt:
        lat_ref, krt_ref, q_ref, k_ref, v_ref, gq_ref, gk_ref, gv_ref, la_ref, gr_ref = outs
    else:
        lat_ref, kr_ref, krt_ref, q_ref, gq_ref, gk_ref, gv_ref, la_ref, gr_ref = outs
    tm, d = x_ref.shape
    x = x_ref[...]
    xn = x * lax.rsqrt(jnp.mean(x * x, axis=-1, keepdims=True) + EPS)
    h = (xn.reshape(gpt, tm // gpt, d) * (g1_ref[...] * (1.0 + sc_ref[...])) + sh_ref[...]).reshape(tm, d)
    hb = h.astype(BF16)

    def col(c):
        return _dot(hb, win_ref[:, c[0]:c[1]])

    tab = tab_ref[...]
    cos, sa, sb = tab[:, 0:LANES], tab[:, LANES:2 * LANES], tab[:, 2 * LANES:3 * LANES]

    qkr = col(C_QKR)
    krg = qkr[:, Q_LORA:]
    z = _dot(krg.astype(BF16), wg_ref[...]) + bg_ref[...]
    la_ref[...] = (jnp.minimum(z, 0.0) - jnp.log(1.0 + jnp.exp(-jnp.abs(z)))) * (1.0 / GLA_TAU)
    r = col(C_GR)
    gr_ref[...] = (r * jax.nn.sigmoid(r)).astype(BF16)

    cq = qkr[:, 0:Q_LORA]
    cqn = cq * lax.rsqrt(jnp.mean(cq * cq, axis=-1, keepdims=True) + EPS) * gql_ref[...]
    qp = _dot(cqn.astype(BF16), wuq_ref[...])
    for hh in range(MLA_HEADS):
        nope = qp[:, hh * HEAD_PAD:hh * HEAD_PAD + QK_NOPE]
        rt = qp[:, hh * HEAD_PAD + QK_NOPE:(hh + 1) * HEAD_PAD]
        ss = jnp.sum(nope * nope, axis=-1, keepdims=True) + jnp.sum(rt * rt, axis=-1, keepdims=True)
        inv = lax.rsqrt(ss * (1.0 / QK_DIM) + EPS)
        q_ref[:, hh * HEAD_PAD:hh * HEAD_PAD + QK_NOPE] = (nope * inv * gqn_ref[...]).astype(BF16)
        rq = _rope_tile(rt * inv * gqr_ref[...], cos, sa, sb)
        q_ref[:, hh * HEAD_PAD + QK_NOPE:(hh + 1) * HEAD_PAD] = (rq + qone_ref[...]).astype(BF16)

    ckv = col(C_KV)
    lat = ckv * lax.rsqrt(jnp.mean(ckv * ckv, axis=-1, keepdims=True) + EPS) * gkv_ref[...]
    lat_ref[...] = lat
    krt_ref[...] = krg.T[0:QK_ROPE, :]
    if prompt:
        lane = lax.broadcasted_iota(jnp.int32, (1, LANES), 1)
        krm = jnp.where(lane < QK_ROPE, krg, 0.0)
        ssr = jnp.sum(krm * krm, axis=-1, keepdims=True)
        rk = _rope_tile(krm * gkr_ref[...], cos, sa, sb)
        kv = _dot(lat.astype(BF16), wukv_ref[...])
        for hh in range(MLA_HEADS):
            kn = kv[:, hh * QK_NOPE:(hh + 1) * QK_NOPE]
            inv = lax.rsqrt((jnp.sum(kn * kn, axis=-1, keepdims=True) + ssr) * (1.0 / QK_DIM) + EPS)
            k_ref[:, hh * HEAD_PAD:hh * HEAD_PAD + QK_NOPE] = (kn * inv * gkn_ref[...]).astype(BF16)
            k_ref[:, hh * HEAD_PAD + QK_NOPE:(hh + 1) * HEAD_PAD] = (rk * inv + kbias_ref[...]).astype(BF16)
        v_ref[...] = kv[:, MLA_HEADS * QK_NOPE:].astype(BF16)
    else:
        kr_ref[...] = krg[:, 0:QK_ROPE]

    gq_ref[...] = col(C_GQ)
    gk_ref[...] = col(C_GK)
    gv_ref[...] = col(C_GV).astype(BF16)


def _projection(x2, mod4, tab, w, *, rows_per_group, tm, prompt):
    n, d = x2.shape
    gpt = max(1, tm // rows_per_group)
    tpg = max(1, rows_per_group // tm)
    ntab = tab.shape[0] // tm

    def mod_spec(j):
        return pl.BlockSpec((gpt, None, 1, d), lambda i: ((i // tpg) if gpt == 1 else i, j, 0, 0))

    def rows(c):
        return pl.BlockSpec((tm, c), lambda i: (i, 0))

    def out(c, t):
        return rows(c), jax.ShapeDtypeStruct((n, c), t)

    gla_outs = [out(GLA_QK, F32), out(GLA_QK, F32), out(GLA_WIDTH, BF16), out(GLA_QK, F32), out(GLA_WIDTH, BF16)]
    qo = out(MLA_HEADS * HEAD_PAD, BF16)
    if prompt:
        krt = (pl.BlockSpec((None, QK_ROPE, tm), lambda i: (i // tpg, 0, i % tpg)),
               jax.ShapeDtypeStruct((n // rows_per_group, QK_ROPE, rows_per_group), F32))
        outs = [out(KV_LORA, F32), krt, qo, out(MLA_HEADS * HEAD_PAD, BF16), out(MLA_WIDTH, BF16)] + gla_outs
    else:
        krt = (pl.BlockSpec((QK_ROPE, tm), lambda i: (0, i)), jax.ShapeDtypeStruct((QK_ROPE, n), F32))
        outs = [out(KV_LORA, F32), out(QK_ROPE, F32), krt, qo] + gla_outs
    return pl.pallas_call(
        functools.partial(_proj_kernel, gpt=gpt, prompt=prompt),
        grid=(n // tm,),
        in_specs=[rows(d), mod_spec(0), mod_spec(1), _const((1, d)), _const(w["w_in"].shape),
                  _const((1, Q_LORA)), _const(w["w_uq"].shape), _const((1, KV_LORA)), _const(w["w_ukv"].shape),
                  _const((1, LANES)), _const((1, LANES)), _const((1, LANES)), _const((1, LANES)),
                  _const((1, LANES)), _const((1, LANES)),
                  pl.BlockSpec((tm, 3 * LANES), lambda i: (i % ntab, 0)),
                  _const(w["w_gate"].shape), _const((1, GLA_QK))],
        out_specs=[o[0] for o in outs],
        out_shape=[o[1] for o in outs],
        compiler_params=_params("arbitrary"),
        name="in_proj",
    )(x2, mod4, mod4, w["g_norm1"], w["w_in"], w["g_q_lora"], w["w_uq"], w["g_kv_lora"], w["w_ukv"],
      w["gqn"], w["gqr"], w["gkn"], w["gkr"], w["qone"], w["kbias"], tab, w["w_gate"], w["b_gate"])


def _attn_prompt_kernel(fast_ref, q_ref, k_ref, v_ref, o_ref, vx_ref, m_ref, acc_ref, *, tq):
    i = pl.program_id(1)
    heads = range(MLA_HEADS)

    @pl.when(i == 0)
    def _():
        for hh in heads:
            vx_ref[hh, :, 0:V_DIM] = v_ref[:, hh * V_DIM:(hh + 1) * V_DIM]
            vx_ref[hh, :, V_DIM:] = jnp.ones((v_ref.shape[0], V_DIM), BF16)

    def scores(j, hh, masked):
        hs = slice(hh * HEAD_PAD, (hh + 1) * HEAD_PAD)
        s = _dot_nt(q_ref[:, hs], k_ref[pl.ds(pl.multiple_of(j * tq, tq), tq), hs])
        if masked:
            qc = lax.broadcasted_iota(jnp.int32, (tq, tq), 0) // CHUNK
            kc = lax.broadcasted_iota(jnp.int32, (tq, tq), 1) // CHUNK
            s = jnp.where(kc <= qc, s, -jnp.inf)
        return s

    def fast_diagonal():
        hq = tq // 2
        k0 = pl.multiple_of(i * tq, tq)
        qc = lax.broadcasted_iota(jnp.int32, (hq, hq), 0) // CHUNK
        kc = lax.broadcasted_iota(jnp.int32, (hq, hq), 1) // CHUNK
        tri = kc <= qc
        for hh in heads:
            hs = slice(hh * HEAD_PAD, (hh + 1) * HEAD_PAD)
            top = jnp.where(tri, _dot_nt(q_ref[0:hq, hs], k_ref[pl.ds(k0, hq), hs]), -jnp.inf)
            acc_ref[hh, 0:hq] = _dot(jnp.exp2(top).astype(BF16), vx_ref[hh, pl.ds(k0, hq), :])
            bot = _dot_nt(q_ref[hq:tq, hs], k_ref[pl.ds(k0, tq), hs])
            bot = jnp.concatenate([bot[:, 0:hq], jnp.where(tri, bot[:, hq:tq], -jnp.inf)], axis=1)
            acc_ref[hh, hq:tq] = _dot(jnp.exp2(bot).astype(BF16), vx_ref[hh, pl.ds(k0, tq), :])

    def vblock(j, hh):
        return vx_ref[hh, pl.ds(pl.multiple_of(j * tq, tq), tq), :]

    def fast_block(j, masked):
        for hh in heads:
            acc_ref[hh] += _dot(jnp.exp2(scores(j, hh, masked)).astype(BF16), vblock(j, hh))

    def safe_block(j, masked):
        for hh in heads:
            s = scores(j, hh, masked)
            m = m_ref[hh]
            m_new = jnp.maximum(m, jnp.max(s, axis=1, keepdims=True))
            p = jnp.exp2(s - m_new).astype(BF16)
            acc_ref[hh] = jnp.exp2(m - m_new) * acc_ref[hh] + _dot(p, vblock(j, hh))
            m_ref[hh] = m_new

    def sweep(block):
        def body(jj, c):
            block(2 * jj, False)
            block(2 * jj + 1, False)
            return c

        lax.fori_loop(0, i // 2, body, 0)

        @pl.when(i % 2 == 1)
        def _():
            block(i - 1, False)

    @pl.when(fast_ref[0] == 1)
    def _():
        fast_diagonal()
        sweep(fast_block)

    @pl.when(fast_ref[0] != 1)
    def _():
        m_ref[...] = jnp.full(m_ref.shape, -jnp.inf, F32)
        acc_ref[...] = jnp.zeros(acc_ref.shape, F32)
        safe_block(i, True)
        sweep(safe_block)

    for hh in heads:
        acc = acc_ref[hh]
        o_ref[:, hh * V_DIM:(hh + 1) * V_DIM] = (acc[:, 0:V_DIM] / acc[:, V_DIM:]).astype(BF16)


def _attention_prompt(fast, q, k, v, *, batch, seq, tq):
    nq = seq // tq
    return pl.pallas_call(
        functools.partial(_attn_prompt_kernel, tq=tq),
        grid=(batch, nq),
        in_specs=[pl.BlockSpec(memory_space=pltpu.SMEM),
                  pl.BlockSpec((tq, MLA_HEADS * HEAD_PAD), lambda b, i: (b * nq + i, 0)),
                  pl.BlockSpec((seq, MLA_HEADS * HEAD_PAD), lambda b, i: (b, 0)),
                  pl.BlockSpec((seq, MLA_WIDTH), lambda b, i: (b, 0))],
        out_specs=pl.BlockSpec((tq, MLA_WIDTH), lambda b, i: (b * nq + i, 0)),
        out_shape=jax.ShapeDtypeStruct((batch * seq, MLA_WIDTH), BF16),
        scratch_shapes=[pltpu.VMEM((MLA_HEADS, seq, 2 * V_DIM), BF16), pltpu.VMEM((MLA_HEADS, tq, 1), F32),
                        pltpu.VMEM((MLA_HEADS, tq, 2 * V_DIM), F32)],
        compiler_params=_params("arbitrary", "arbitrary"),
        name="mla_attn_prompt",
    )(fast, q, k, v)


def _attn_sample_kernel(fast_ref, bound_ref, q_ref, lat_ref, krt_ref, latn_ref, krtn_ref, tab_ref, wukt_ref, wuv_ref,
                        gkn_ref, gkr_ref, o_ref, wq_ref, latb_ref, s_ref, *, tkb):
    b = pl.program_id(0)
    t = q_ref.shape[0]
    past = lat_ref.shape[0]
    nk = MLA_HEADS * QK_NOPE
    half = QK_ROPE // 2

    @pl.when(b == 0)
    def _():
        wq_ref[0:nk, :] = wukt_ref[...]

    qr = []
    for hh in range(MLA_HEADS):
        qn = (q_ref[:, hh * HEAD_PAD:hh * HEAD_PAD + QK_NOPE].astype(F32) * gkn_ref[...]).astype(BF16)
        wq_ref[nk + hh * t:nk + (hh + 1) * t, :] = _dot(qn, wukt_ref[hh * QK_NOPE:(hh + 1) * QK_NOPE, :]).astype(BF16)
        qr.append(q_ref[:, hh * HEAD_PAD + QK_NOPE:(hh + 1) * HEAD_PAD])
    qr = jnp.concatenate(qr, axis=0)

    def key_block(lat, krt, tab, width, valid):
        latb = lat.astype(BF16)
        g = _dot_nt(wq_ref[...], latb)
        ssr = jnp.sum(krt * krt, axis=0, keepdims=True)
        kg = krt * gkr_ref[...]
        x1, x2 = kg[0:half], kg[half:QK_ROPE]
        c, sn = tab[0:half], tab[half:QK_ROPE]
        rope = jnp.concatenate([x1 * c - x2 * sn, x2 * c + x1 * sn, jnp.zeros((LANES - QK_ROPE, width), F32)], axis=0)
        srope = _dot(qr, rope.astype(BF16))
        rows = []
        for hh in range(MLA_HEADS):
            kn = g[hh * QK_NOPE:(hh + 1) * QK_NOPE]
            inv = lax.rsqrt((jnp.sum(kn * kn, axis=0, keepdims=True) + ssr) * (1.0 / QK_DIM) + EPS)
            rows.append((g[nk + hh * t:nk + (hh + 1) * t] + srope[hh * t:(hh + 1) * t]) * inv)
        sc = jnp.concatenate(rows, axis=0) - bound_ref[0]
        if valid is not None:
            sc = jnp.where(valid, sc, -jnp.inf)
        return sc, latb

    mine = lax.broadcasted_iota(jnp.int32, (1, 2 * t), 1) // t == b % 2

    def blocks():
        for blk in range(past // tkb):
            c0 = blk * tkb
            yield c0, tkb, key_block(lat_ref[c0:c0 + tkb, :], krt_ref[:, c0:c0 + tkb], tab_ref[:, c0:c0 + tkb],
                                     tkb, None)
        yield past, 2 * t, key_block(latn_ref[...], krtn_ref[...], tab_ref[:, past:past + 2 * t], 2 * t, mine)

    def finish(wlat, l):
        wlat = wlat.astype(BF16)
        for hh in range(MLA_HEADS):
            o = _dot(wlat[hh * t:(hh + 1) * t], wuv_ref[:, hh * V_DIM:(hh + 1) * V_DIM])
            o_ref[:, hh * V_DIM:(hh + 1) * V_DIM] = (o / l[hh * t:(hh + 1) * t]).astype(BF16)

    @pl.when(fast_ref[0] == 1)
    def _():
        wlat = jnp.zeros((MLA_HEADS * t, KV_LORA), F32)
        lsum = jnp.zeros((MLA_HEADS * t, LANES), F32)
        for _, width, (sc, latb) in blocks():
            p = jnp.exp2(sc)
            wlat += _dot(p.astype(BF16), latb)
            for c in range(width // LANES):
                lsum += p[:, c * LANES:(c + 1) * LANES]
        finish(wlat, jnp.sum(lsum, axis=1, keepdims=True))

    @pl.when(fast_ref[0] != 1)
    def _():
        for c0, width, (sc, latb) in blocks():
            s_ref[:, c0:c0 + width] = sc
            latb_ref[c0:c0 + width, :] = latb
        s = s_ref[...]
        p = jnp.exp2(s - jnp.max(s, axis=1, keepdims=True))
        finish(_dot(p.astype(BF16), latb_ref[...]), jnp.sum(p, axis=1, keepdims=True))


def _attention_sample(q, past_lat, past_krt, lat_new, krt_new, w, *, tkb):
    batch, past, _ = past_lat.shape
    t = q.shape[0] // batch
    half = QK_ROPE // 2
    inv = ROPE_THETA ** (-np.arange(half, dtype=np.float64) / half)
    pos = np.concatenate([np.arange(past), past + np.arange(t), past + np.arange(t)]).astype(np.float64)
    ang = inv[:, None] * pos[None, :]
    tab = jnp.asarray(np.concatenate([np.cos(ang), np.sin(ang)], axis=0).astype(np.float32))
    s_pad = past + 2 * t
    return pl.pallas_call(
        functools.partial(_attn_sample_kernel, tkb=tkb),
        grid=(batch,),
        in_specs=[pl.BlockSpec(memory_space=pltpu.SMEM), pl.BlockSpec(memory_space=pltpu.SMEM),
                  pl.BlockSpec((t, MLA_HEADS * HEAD_PAD), lambda b: (b, 0)),
                  pl.BlockSpec((None, past, KV_LORA), lambda b: (b, 0, 0)),
                  pl.BlockSpec((None, QK_ROPE, past), lambda b: (b, 0, 0)),
                  pl.BlockSpec((2 * t, KV_LORA), lambda b: (b // 2, 0)),
                  pl.BlockSpec((QK_ROPE, 2 * t), lambda b: (0, b // 2)),
                  _const(tab.shape), _const(w["w_ukt"].shape), _const(w["w_uv"].shape),
                  _const((1, LANES)), _const((QK_ROPE, 1))],
        out_specs=pl.BlockSpec((t, MLA_WIDTH), lambda b: (b, 0)),
        out_shape=jax.ShapeDtypeStruct((batch * t, MLA_WIDTH), BF16),
        scratch_shapes=[pltpu.VMEM((MLA_HEADS * (QK_NOPE + t), KV_LORA), BF16),
                        pltpu.VMEM((s_pad, KV_LORA), BF16),
                        pltpu.VMEM((MLA_HEADS * t, s_pad), F32)],
        compiler_params=_params("arbitrary"),
        name="mla_attn_sample",
    )(w["fast_softmax"], w["bound"], q, past_lat, past_krt, lat_new, krt_new, tab, w["w_ukt"], w["w_uv"], w["gkn"],
      w["gkr_col"])


def _gla_kernel(q_ref, k_ref, v_ref, la_ref, r_ref, s0_ref, g_ref, spread_ref, o_ref, sn_ref,
                st_ref, kp_ref, ap_ref, p_ref, on_ref, *, gpt):
    t_idx = pl.program_id(1)
    L, W, P = CHUNK, GLA_QK, LANES
    R = q_ref.shape[0]
    n_chunks = R // L
    cpg = n_chunks // gpt
    n_pairs = GLA_HEADS // 2

    lane_p = lax.broadcasted_iota(jnp.int32, (1, P), 1)
    even = lane_p < GLA_DK
    bd_mask = (lax.broadcasted_iota(jnp.int32, (2 * GLA_DV, P), 0) // GLA_DV
               == lax.broadcasted_iota(jnp.int32, (2 * GLA_DV, P), 1) // GLA_DK)

    @pl.when(t_idx == 0)
    def _():
        kp_ref[0:SUB, :] = jnp.zeros((SUB, W), F32)
        ap_ref[0:SUB, :] = jnp.zeros((SUB, W), F32)
        for gi in range(gpt):
            for pr in range(n_pairs):
                tt = s0_ref[gi, 2 * pr:2 * pr + 2].reshape(2 * GLA_DK, GLA_DV).T
                st_ref[gi, pr] = jnp.where(bd_mask, jnp.concatenate([tt, tt], axis=0), 0.0)

    q = q_ref[...]
    k = k_ref[...]
    la = la_ref[...]

    tri = (lax.broadcasted_iota(jnp.int32, (L, L), 0) >= lax.broadcasted_iota(jnp.int32, (L, L), 1)).astype(BF16)
    la_hi = la.astype(BF16)
    la2 = jnp.concatenate([la_hi, (la - la_hi.astype(F32)).astype(BF16)], axis=1)
    bs = []
    for c in range(n_chunks):
        t2 = _dot(tri, la2[c * L:(c + 1) * L, :])
        bs.append(t2[:, 0:W] + t2[:, W:2 * W])
    b = (jnp.concatenate(bs, axis=0) if n_chunks > 1 else bs[0]) * LOG2E
    b3 = b.reshape(n_chunks, L, W)

    def chunk_row(r):
        return jnp.broadcast_to(b3[:, r:r + 1, :], (n_chunks, L, W)).reshape(R, W)

    b_sub = jnp.broadcast_to(b.reshape(R // SUB, SUB, W)[:, 0:1, :], (R // SUB, SUB, W)).reshape(R, W)
    sub = (lax.broadcasted_iota(jnp.int32, (R, W), 0) % L) // SUB

    qt = q * jnp.exp2(b - b_sub)
    zb = jnp.zeros((), BF16)
    zq = jnp.zeros((SUB, P), F32)
    ktm = []
    for i in range(1, L // SUB):
        kt = (k * jnp.exp2(chunk_row(i * SUB) - b)).astype(BF16)
        ktm.append([[jnp.where((sub[:, 0:P] < i) & (even if e == 0 else ~even), kt[:, pr * P:(pr + 1) * P], zb)
                     for e in range(2)] for pr in range(n_pairs)])

    a = jnp.exp(la)
    kp_ref[SUB:SUB + R, :] = k
    ap_ref[SUB:SUB + R, :] = a
    p_ref[:, 0:W] = (q * k).astype(BF16)
    e = a
    for d in range(1, SUB):
        if d > 1:
            e = e * ap_ref[SUB - d + 1:SUB - d + 1 + R, :]
        p_ref[:, d * W:(d + 1) * W] = (q * kp_ref[SUB - d:SUB - d + R, :] * e).astype(BF16)
    cband = _dot(p_ref[...], spread_ref[...])
    same_sub = (lax.broadcasted_iota(jnp.int32, (L, W), 0) // SUB
                == (lax.broadcasted_iota(jnp.int32, (L, W), 1) % L) // SUB)

    qe = (q * jnp.exp2(b)).astype(BF16)
    kd = (k * jnp.exp2(chunk_row(L - 1) - b)).astype(BF16)
    zv = jnp.zeros((L, 2 * GLA_DV), BF16)

    o_intra, d_st, dec = {}, {}, []
    for c in range(n_chunks):
        rs = slice(c * L, (c + 1) * L)
        a_band = jnp.where(same_sub, pltpu.roll(cband[rs], W - (SUB - 1), 1, stride=1, stride_axis=0), 0.0)
        dec.append(jnp.exp2(b[c * L + L - 1:c * L + L, :]))
        for pr in range(n_pairs):
            ls = slice(pr * P, (pr + 1) * P)
            lhs_c = jnp.concatenate(
                [jnp.concatenate([qt[c * L + r * SUB:c * L + (r + 1) * SUB, ls] if i == r else zq
                                  for i in range(1, L // SUB)], axis=1) for r in range(L // SUB)],
                axis=0).astype(BF16)
            rhs_c = jnp.concatenate([jnp.concatenate([m[pr][0][rs], m[pr][1][rs]], axis=0) for m in ktm], axis=1)
            a_tot = (a_band[:, ls] + _dot_nt(lhs_c, rhs_c)).astype(BF16)
            vp = v_ref[rs, 2 * pr * GLA_DV:(2 * pr + 2) * GLA_DV]
            v_bd = jnp.concatenate([jnp.concatenate([vp[:, 0:GLA_DV], zv[:, 0:GLA_DV]], axis=1),
                                    jnp.concatenate([zv[:, 0:GLA_DV], vp[:, GLA_DV:]], axis=1)], axis=0)
            o_intra[c, pr] = _dot(a_tot, v_bd)
            d_st[c, pr] = jnp.where(bd_mask, _dot_tn(vp, kd[rs, ls]), 0.0)
    st_in = {}
    for gi in range(gpt):
        for pr in range(n_pairs):
            st = st_ref[gi, pr]
            for c in range(gi * cpg, (gi + 1) * cpg):
                st_in[c, pr] = st.astype(BF16)
                st = st * dec[c][:, pr * P:(pr + 1) * P] + d_st[c, pr]
            st_ref[gi, pr] = st
    for c in range(n_chunks):
        rs = slice(c * L, (c + 1) * L)
        for pr in range(n_pairs):
            on_ref[rs, 2 * pr * GLA_DV:(2 * pr + 2) * GLA_DV] = (
                o_intra[c, pr] + _dot_nt(qe[rs, pr * P:(pr + 1) * P], st_in[c, pr]))

    for hh in range(GLA_HEADS):
        hs = slice(hh * GLA_DV, (hh + 1) * GLA_DV)
        o = on_ref[:, hs]
        on = o * lax.rsqrt(jnp.mean(o * o, axis=-1, keepdims=True) + EPS) * g_ref[:, hs]
        o_ref[:, hs] = (on * r_ref[:, hs].astype(F32)).astype(BF16)

    @pl.when(t_idx == pl.num_programs(1) - 1)
    def _():
        for gi in range(gpt):
            for pr in range(n_pairs):
                st = st_ref[gi, pr]
                tt = jnp.where(even, st[0:GLA_DV], st[GLA_DV:2 * GLA_DV])
                sn_ref[gi, 2 * pr:2 * pr + 2] = tt.T.reshape(2, GLA_DK, GLA_DV)


def _band_spread():
    m = np.zeros((SUB, GLA_HEADS, GLA_DK, GLA_QK), np.float32)
    for d in range(SUB):
        for h in range(GLA_HEADS):
            m[d, h, :, h * GLA_DK + SUB - 1 - d] = 1.0
    return jnp.asarray(m.reshape(SUB * GLA_QK, GLA_QK), BF16)


def _gla(gq, gk, gv, la, gr, s0, g_out, *, groups, rows_per_group, tc):
    gpt = max(1, tc // rows_per_group)
    nt = max(1, rows_per_group // tc)

    def rows(c):
        return pl.BlockSpec((tc, c), lambda g, t: (g * nt + t, 0))

    state = pl.BlockSpec((gpt, GLA_HEADS, GLA_DK, GLA_DV), lambda g, t: (g, 0, 0, 0))
    spread = _band_spread()
    return pl.pallas_call(
        functools.partial(_gla_kernel, gpt=gpt),
        grid=(groups // gpt, nt),
        in_specs=[rows(GLA_QK), rows(GLA_QK), rows(GLA_WIDTH), rows(GLA_QK), rows(GLA_WIDTH), state,
                  _const((1, GLA_WIDTH)), _const(spread.shape)],
        out_specs=[rows(GLA_WIDTH), state],
        out_shape=[jax.ShapeDtypeStruct((groups * rows_per_group, GLA_WIDTH), BF16),
                   jax.ShapeDtypeStruct((groups, GLA_HEADS, GLA_DK, GLA_DV), F32)],
        scratch_shapes=[pltpu.VMEM((gpt, GLA_HEADS // 2, 2 * GLA_DV, LANES), F32),
                        pltpu.VMEM((SUB + tc, GLA_QK), F32), pltpu.VMEM((SUB + tc, GLA_QK), F32),
                        pltpu.VMEM((tc, SUB * GLA_QK), BF16), pltpu.VMEM((tc, GLA_WIDTH), F32)],
        compiler_params=_params("arbitrary", "arbitrary"),
        name="gla",
    )(gq, gk, gv, la, gr, s0, g_out, spread)


def _mlp_kernel(x_ref, a_ref, b_ref, g1_ref, sh2_ref, sc2_ref, g2_ref, gn_ref, wo_ref, wu_ref, wd_ref, y_ref,
                *, gpt, tf):
    tm, d = x_ref.shape

    def per_group(val, ref, scale_plus_one=False):
        m = ref[...]
        if scale_plus_one:
            m = 1.0 + m
        return (val.reshape(gpt, tm // gpt, d) * m).reshape(tm, d)

    mix = jnp.concatenate([a_ref[...], b_ref[...]], axis=1)
    x1 = x_ref[...] + per_group(_dot(mix, wo_ref[...]), g1_ref)
    xn = x1 * lax.rsqrt(jnp.mean(x1 * x1, axis=-1, keepdims=True) + EPS) * gn_ref[...]
    h2 = (per_group(xn, sc2_ref, True).reshape(gpt, tm // gpt, d) + sh2_ref[...]).reshape(tm, d).astype(BF16)
    acc = jnp.zeros((tm, d), F32)
    for j in range(wu_ref.shape[1] // tf):
        u = jnp.maximum(_dot(h2, wu_ref[:, j * tf:(j + 1) * tf]), 0.0)
        acc += _dot((u * u).astype(BF16), wd_ref[j * tf:(j + 1) * tf, :])
    y_ref[...] = x1 + per_group(acc, g2_ref)


def _mlp(x2, a_out, b_out, mod4, w, *, rows_per_group, tm):
    n, d = x2.shape
    gpt = max(1, tm // rows_per_group)
    tpg = max(1, rows_per_group // tm)

    def mod_spec(j):
        return pl.BlockSpec((gpt, None, 1, d), lambda i: ((i // tpg) if gpt == 1 else i, j, 0, 0))

    def rows(c):
        return pl.BlockSpec((tm, c), lambda i: (i, 0))

    return pl.pallas_call(
        functools.partial(_mlp_kernel, gpt=gpt, tf=1024),
        grid=(n // tm,),
        in_specs=[rows(d), rows(MLA_WIDTH), rows(GLA_WIDTH), mod_spec(2), mod_spec(3), mod_spec(4), mod_spec(5),
                  _const((1, d)), _const(w["w_out"].shape), _const(w["w_up"].shape), _const(w["w_down"].shape)],
        out_specs=rows(d),
        out_shape=jax.ShapeDtypeStruct((n, d), F32),
        compiler_params=_params("arbitrary"),
        name="out_proj_mlp",
    )(x2, a_out, b_out, mod4, mod4, mod4, mod4, w["g_norm2"], w["w_out"], w["w_up"], w["w_down"])


def _rope_table(start, count, repeat=1):
    half = QK_ROPE // 2
    inv = ROPE_THETA ** (-np.arange(half, dtype=np.float64) / half)
    ang = (start + np.arange(count, dtype=np.float64))[:, None] * inv[None, :]
    c, s, z = np.cos(ang), np.sin(ang), np.zeros_like(ang)
    tab = np.concatenate([c, c, z, z, -s, z, z, z, z, s, z, z], axis=1).astype(np.float32)
    return jnp.asarray(np.tile(tab, (repeat, 1)))


def _pad_gain(g_rope):
    return jnp.concatenate([g_rope, jnp.zeros((LANES - QK_ROPE,), F32)]).reshape(1, LANES)


def _relayout_kernel(wint_ref, wuq_ref, wukv_ref, wg_ref, win_o, wuq_o, wukv_o, wukt_o, wuv_o, wg_o):
    s = np.cumsum([0, Q_LORA, KV_LORA, QK_ROPE, GLA_QK, GLA_QK, GLA_WIDTH, GLA_GATE_RANK, GLA_WIDTH])

    def piece(i):
        return wint_ref[int(s[i]):int(s[i + 1]), :]

    d = wint_ref.shape[1]
    zeros = jnp.zeros((LANES - QK_ROPE - GLA_GATE_RANK, d), F32)
    win_o[:, 0:Q_LORA] = piece(0).T.astype(BF16)
    win_o[:, Q_LORA:C_QKR[1]] = jnp.concatenate([piece(2), piece(6), zeros], axis=0).T.astype(BF16)
    win_o[:, C_KV[0]:C_KV[1]] = piece(1).T.astype(BF16)
    win_o[:, C_GQ[0]:C_GQ[1]] = (piece(3).T * (GLA_DK ** -0.5)).astype(BF16)
    win_o[:, C_GK[0]:C_GK[1]] = piece(4).T.astype(BF16)
    win_o[:, C_GV[0]:C_GV[1]] = piece(5).T.astype(BF16)
    win_o[:, C_GR[0]:C_GR[1]] = piece(7).T.astype(BF16)

    zq = jnp.zeros((Q_LORA, HEAD_PAD - QK_DIM), BF16)
    kvw = QK_NOPE + V_DIM
    for hh in range(MLA_HEADS):
        wuq_o[:, hh * HEAD_PAD:hh * HEAD_PAD + QK_DIM] = wuq_ref[:, hh * QK_DIM:(hh + 1) * QK_DIM].astype(BF16)
        wuq_o[:, hh * HEAD_PAD + QK_DIM:(hh + 1) * HEAD_PAD] = zq
        uk = wukv_ref[:, hh * kvw:hh * kvw + QK_NOPE]
        uv = wukv_ref[:, hh * kvw + QK_NOPE:(hh + 1) * kvw].astype(BF16)
        wukv_o[:, hh * QK_NOPE:(hh + 1) * QK_NOPE] = uk.astype(BF16)
        wukv_o[:, (MLA_HEADS + hh) * V_DIM:(MLA_HEADS + hh + 1) * V_DIM] = uv
        wukt_o[hh * QK_NOPE:(hh + 1) * QK_NOPE, :] = uk.T.astype(BF16)
        wuv_o[:, hh * V_DIM:(hh + 1) * V_DIM] = uv

    wg_o[...] = jnp.zeros(wg_o.shape, BF16)
    wg_o[QK_ROPE:QK_ROPE + GLA_GATE_RANK, :] = wg_ref[...].astype(BF16)


def _relayout(w_in_t, w_uq, w_ukv, w_gate_up):
    d = w_in_t.shape[1]
    shapes = [(d, C_GR[1]), (Q_LORA, MLA_HEADS * HEAD_PAD), (KV_LORA, MLA_HEADS * (QK_NOPE + V_DIM)),
              (MLA_HEADS * QK_NOPE, KV_LORA), (KV_LORA, MLA_WIDTH), (LANES, GLA_QK)]
    return pl.pallas_call(
        _relayout_kernel,
        out_shape=[jax.ShapeDtypeStruct(sh, BF16) for sh in shapes],
        compiler_params=pltpu.CompilerParams(vmem_limit_bytes=VMEM_LIMIT),
        name="weight_relayout",
    )(w_in_t, w_uq, w_ukv, w_gate_up)


def _prep_weights(w_in, g_norm1, g_q_lora, w_uq, g_kv_lora, w_ukv, g_q_head, g_k_head,
                  w_gate_up, b_gate_up, g_gla_out, w_out, g_norm2, w_up, w_down):
    d = w_in.shape[0]
    w_in_p, w_uq_p, w_ukv_p, w_ukt, w_uv, w_gate = _relayout(w_in.T, w_uq, w_ukv, w_gate_up)
    qscale = QK_DIM ** -0.5 * LOG2E
    bound = 1.02 * QK_DIM ** 0.5 * LOG2E * jnp.max(jnp.abs(g_q_head)) * jnp.max(jnp.abs(g_k_head))
    lane = jnp.arange(LANES) == QK_ROPE
    return {
        "w_in": w_in_p, "g_norm1": g_norm1.reshape(1, d), "g_q_lora": g_q_lora.reshape(1, Q_LORA),
        "w_uq": w_uq_p, "g_kv_lora": g_kv_lora.reshape(1, KV_LORA), "w_ukv": w_ukv_p, "w_ukt": w_ukt, "w_uv": w_uv,
        "gkr_col": g_k_head[QK_NOPE:].reshape(QK_ROPE, 1),
        "qone": lane.astype(F32).reshape(1, LANES), "kbias": jnp.where(lane, -bound, 0.0).reshape(1, LANES),
        "fast_softmax": (bound <= MAX_FIXED_SHIFT).astype(jnp.int32).reshape(1), "bound": bound.reshape(1),
        "gqn": (g_q_head[:QK_NOPE] * qscale).reshape(1, LANES), "gqr": _pad_gain(g_q_head[QK_NOPE:] * qscale),
        "gkn": g_k_head[:QK_NOPE].reshape(1, LANES), "gkr": _pad_gain(g_k_head[QK_NOPE:]),
        "w_gate": w_gate, "b_gate": b_gate_up.reshape(1, GLA_QK),
        "g_gla_out": g_gla_out.reshape(1, GLA_WIDTH), "w_out": w_out.astype(BF16),
        "g_norm2": g_norm2.reshape(1, d), "w_up": w_up.astype(BF16), "w_down": w_down.astype(BF16),
    }


def _layer(x, mod, past_lat, past_kr, s0, w, *, tm):
    batch, seq, d = x.shape
    n = batch * seq
    past = 0 if past_lat is None else past_lat.shape[1]
    x2 = x.reshape(n, d)
    mod4 = mod.reshape(batch, 6, 1, d)
    tm = min(tm, n)
    tab = _rope_table(past, seq, repeat=max(1, tm // seq))
    if past == 0:
        lat, krt, q, k, v, gq, gk, gv, la, gr = _projection(x2, mod4, tab, w, rows_per_group=seq, tm=tm, prompt=True)
        kr = jnp.swapaxes(krt, 1, 2)
        a_out = _attention_prompt(w["fast_softmax"], q, k, v, batch=batch, seq=seq, tq=min(512, seq))
    else:
        assert seq == CHUNK and past % CHUNK == 0
        lat, kr, krt, q, gq, gk, gv, la, gr = _projection(x2, mod4, tab, w, rows_per_group=seq, tm=tm, prompt=False)
        a_out = _attention_sample(q, past_lat, jnp.swapaxes(past_kr, 1, 2), lat, krt, w, tkb=min(512, past))
    b_out, s_new = _gla(gq, gk, gv, la, gr, s0, w["g_gla_out"], groups=batch, rows_per_group=seq,
                        tc=tm)
    y = _mlp(x2, a_out, b_out, mod4, w, rows_per_group=seq, tm=tm)
    return (y.reshape(batch, seq, d), lat.reshape(batch, seq, KV_LORA), kr.reshape(batch, seq, QK_ROPE), s_new)


def kernel(x_prompt, x_sample, cache_mla_latent, cache_mla_krope, state_gla, c_prompt, c_sample,
           w_ada, b_ada, g_norm1, w_in, g_q_lora, w_uq, g_kv_lora, w_ukv, g_q_head, g_k_head,
           w_gate_up, b_gate_up, g_gla_out, w_out, g_norm2, w_up, w_down):
    nb = x_prompt.shape[0]
    depth = w_ada.shape[0]
    y_p, y_s = x_prompt, x_sample
    outs = [[] for _ in range(6)]
    c_all = jnp.concatenate([c_prompt, c_sample], axis=0)
    for l in range(depth):
        w = _prep_weights(w_in[l], g_norm1[l], g_q_lora[l], w_uq[l], g_kv_lora[l], w_ukv[l], g_q_head[l],
                          g_k_head[l], w_gate_up[l], b_gate_up[l], g_gla_out[l], w_out[l], g_norm2[l],
                          w_up[l], w_down[l])
        mod = _modulation(c_all, w_ada[l], b_ada[l])
        zero_state = jnp.zeros((nb, GLA_HEADS, GLA_DK, GLA_DV), x_prompt.dtype)
        y_p, lat, kr, st = _layer(y_p, mod[:nb], None, None, zero_state, w, tm=512)
        outs[0].append(lat); outs[1].append(kr); outs[2].append(st)
        y_s, lat, kr, st = _layer(y_s, mod[nb:], cache_mla_latent[l], cache_mla_krope[l], state_gla[l], w, tm=512)
        outs[3].append(lat); outs[4].append(kr); outs[5].append(st)
    return (y_p, y_s) + tuple(jnp.stack(o) for o in outs)
```

```python
import functools

import jax
import jax.numpy as jnp
import numpy as np
from jax import lax
from jax.experimental import pallas as pl
from jax.experimental.pallas import tpu as pltpu

F32 = jnp.float32
BF16 = jnp.bfloat16

CHUNK = 64
EPS = 1e-6
MLA_HEADS = 4
Q_LORA = 384
KV_LORA = 256
QK_NOPE = 128
QK_ROPE = 64
QK_DIM = QK_NOPE + QK_ROPE
V_DIM = 128
ROPE_THETA = 10000.0
GLA_HEADS = 4
GLA_DK = 64
GLA_DV = 128
GLA_GATE_RANK = 16
GLA_TAU = 16.0
GLA_QK = GLA_HEADS * GLA_DK
GLA_WIDTH = GLA_HEADS * GLA_DV
MLA_WIDTH = MLA_HEADS * V_DIM
HEAD_PAD = 256
SUB = 8
LOG2E = 1.4426950408889634
MAX_FIXED_SHIFT = 48.0

LANES = 128
VMEM_LIMIT = 56 * 1024 * 1024

C_QKR = (0, 512)
C_KV = (512, 768)
C_GQ = (768, 1024)
C_GK = (1024, 1280)
C_GV = (1280, 1792)
C_GR = (1792, 2304)


def _dot(a, b):
    return jnp.dot(a, b, preferred_element_type=F32)


def _dot_nt(a, b):
    return lax.dot_general(a, b, (((1,), (1,)), ((), ())), preferred_element_type=F32)


def _dot_tn(a, b):
    return lax.dot_general(a, b, (((0,), (0,)), ((), ())), preferred_element_type=F32)


def _rope_tile(t, c, sa, sb):
    return t * c + pltpu.roll(t, 96, 1) * sa + pltpu.roll(t, 32, 1) * sb


def _params(*sem):
    return pltpu.CompilerParams(dimension_semantics=sem, vmem_limit_bytes=VMEM_LIMIT)


def _const(shape):
    return pl.BlockSpec(shape, lambda *_: (0,) * len(shape), pipeline_mode=pl.Buffered(1))


def _mod_kernel(c_ref, w_ref, b_ref, o_ref):
    c = c_ref[...]
    s = (c * jax.nn.sigmoid(c)).astype(BF16)
    o_ref[...] = _dot(s, w_ref[...].astype(BF16)) + b_ref[...]


def _modulation(c_all, w_ada, b_ada):
    g, d = c_all.shape
    n = w_ada.shape[1]
    tn = 1024
    return pl.pallas_call(
        _mod_kernel,
        grid=(n // tn,),
        in_specs=[pl.BlockSpec((g, d), lambda j: (0, 0)),
                  pl.BlockSpec((d, tn), lambda j: (0, j)),
                  pl.BlockSpec((1, tn), lambda j: (0, j))],
        out_specs=pl.BlockSpec((g, tn), lambda j: (0, j)),
        out_shape=jax.ShapeDtypeStruct((g, n), F32),
        compiler_params=_params("arbitrary"),
        name="adaln_mod",
    )(c_all, w_ada, b_ada.reshape(1, n))


def _proj_kernel(x_ref, sh_ref, sc_ref, g1_ref, win_ref, gql_ref, wuq_ref, gkv_ref, wukv_ref,
                 gqn_ref, gqr_ref, gkn_ref, gkr_ref, qone_ref, kbias_ref, tab_ref, wg_ref, bg_ref,
                 *outs, gpt, prompt):
    if prompt:
        lat_ref, krt_ref, q_ref, k_ref, v_ref, gq_ref, gk_ref, gv_ref, la_ref, gr_ref = outs
    else:
        lat_ref, kr_ref, krt_ref, q_ref, gq_ref, gk_ref, gv_ref, la_ref, gr_ref = outs
    tm, d = x_ref.shape
    x = x_ref[...]
    xn = x * lax.rsqrt(jnp.mean(x * x, axis=-1, keepdims=True) + EPS)
    h = (xn.reshape(gpt, tm // gpt, d) * (g1_ref[...] * (1.0 + sc_ref[...])) + sh_ref[...]).reshape(tm, d)
    hb = h.astype(BF16)

    def col(c):
        return _dot(hb, win_ref[:, c[0]:c[1]])

    tab = tab_ref[...]
    cos, sa, sb = tab[:, 0:LANES], tab[:, LANES:2 * LANES], tab[:, 2 * LANES:3 * LANES]

    qkr = col(C_QKR)
    krg = qkr[:, Q_LORA:]
    z = _dot(krg.astype(BF16), wg_ref[...]) + bg_ref[...]
    la_ref[...] = (jnp.minimum(z, 0.0) - jnp.log(1.0 + jnp.exp(-jnp.abs(z)))) * (1.0 / GLA_TAU)
    r = col(C_GR)
    gr_ref[...] = (r * jax.nn.sigmoid(r)).astype(BF16)

    cq = qkr[:, 0:Q_LORA]
    cqn = cq * lax.rsqrt(jnp.mean(cq * cq, axis=-1, keepdims=True) + EPS) * gql_ref[...]
    qp = _dot(cqn.astype(BF16), wuq_ref[...])
    for hh in range(MLA_HEADS):
        nope = qp[:, hh * HEAD_PAD:hh * HEAD_PAD + QK_NOPE]
        rt = qp[:, hh * HEAD_PAD + QK_NOPE:(hh + 1) * HEAD_PAD]
        ss = jnp.sum(nope * nope, axis=-1, keepdims=True) + jnp.sum(rt * rt, axis=-1, keepdims=True)
        inv = lax.rsqrt(ss * (1.0 / QK_DIM) + EPS)
        q_ref[:, hh * HEAD_PAD:hh * HEAD_PAD + QK_NOPE] = (nope * inv * gqn_ref[...]).astype(BF16)
        rq = _rope_tile(rt * inv * gqr_ref[...], cos, sa, sb)
        q_ref[:, hh * HEAD_PAD + QK_NOPE:(hh + 1) * HEAD_PAD] = (rq + qone_ref[...]).astype(BF16)

    ckv = col(C_KV)
    lat = ckv * lax.rsqrt(jnp.mean(ckv * ckv, axis=-1, keepdims=True) + EPS) * gkv_ref[...]
    lat_ref[...] = lat
    krt_ref[...] = krg.T[0:QK_ROPE, :]
    if prompt:
        lane = lax.broadcasted_iota(jnp.int32, (1, LANES), 1)
        krm = jnp.where(lane < QK_ROPE, krg, 0.0)
        ssr = jnp.sum(krm * krm, axis=-1, keepdims=True)
        rk = _rope_tile(krm * gkr_ref[...], cos, sa, sb)
        kv = _dot(lat.astype(BF16), wukv_ref[...])
        for hh in range(MLA_HEADS):
            kn = kv[:, hh * QK_NOPE:(hh + 1) * QK_NOPE]
            inv = lax.rsqrt((jnp.sum(kn * kn, axis=-1, keepdims=True) + ssr) * (1.0 / QK_DIM) + EPS)
            k_ref[:, hh * HEAD_PAD:hh * HEAD_PAD + QK_NOPE] = (kn * inv * gkn_ref[...]).astype(BF16)
            k_ref[:, hh * HEAD_PAD + QK_NOPE:(hh + 1) * HEAD_PAD] = (rk * inv + kbias_ref[...]).astype(BF16)
        v_ref[...] = kv[:, MLA_HEADS * QK_NOPE:].astype(BF16)
    else:
        kr_ref[...] = krg[:, 0:QK_ROPE]

    gq_ref[...] = col(C_GQ)
    gk_ref[...] = col(C_GK)
    gv_ref[...] = col(C_GV).astype(BF16)


def _projection(x2, mod4, tab, w, *, rows_per_group, tm, prompt):
    n, d = x2.shape
    gpt = max(1, tm // rows_per_group)
    tpg = max(1, rows_per_group // tm)
    ntab = tab.shape[0] // tm

    def mod_spec(j):
        return pl.BlockSpec((gpt, None, 1, d), lambda i: ((i // tpg) if gpt == 1 else i, j, 0, 0))

    def rows(c):
        return pl.BlockSpec((tm, c), lambda i: (i, 0))

    def out(c, t):
        return rows(c), jax.ShapeDtypeStruct((n, c), t)

    gla_outs = [out(GLA_QK, F32), out(GLA_QK, F32), out(GLA_WIDTH, BF16), out(GLA_QK, F32), out(GLA_WIDTH, BF16)]
    qo = out(MLA_HEADS * HEAD_PAD, BF16)
    if prompt:
        krt = (pl.BlockSpec((None, QK_ROPE, tm), lambda i: (i // tpg, 0, i % tpg)),
               jax.ShapeDtypeStruct((n // rows_per_group, QK_ROPE, rows_per_group), F32))
        outs = [out(KV_LORA, F32), krt, qo, out(MLA_HEADS * HEAD_PAD, BF16), out(MLA_WIDTH, BF16)] + gla_outs
    else:
        krt = (pl.BlockSpec((QK_ROPE, tm), lambda i: (0, i)), jax.ShapeDtypeStruct((QK_ROPE, n), F32))
        outs = [out(KV_LORA, F32), out(QK_ROPE, F32), krt, qo] + gla_outs
    return pl.pallas_call(
        functools.partial(_proj_kernel, gpt=gpt, prompt=prompt),
        grid=(n // tm,),
        in_specs=[rows(d), mod_spec(0), mod_spec(1), _const((1, d)), _const(w["w_in"].shape),
                  _const((1, Q_LORA)), _const(w["w_uq"].shape), _const((1, KV_LORA)), _const(w["w_ukv"].shape),
                  _const((1, LANES)), _const((1, LANES)), _const((1, LANES)), _const((1, LANES)),
                  _const((1, LANES)), _const((1, LANES)),
                  pl.BlockSpec((tm, 3 * LANES), lambda i: (i % ntab, 0)),
                  _const(w["w_gate"].shape), _const((1, GLA_QK))],
        out_specs=[o[0] for o in outs],
        out_shape=[o[1] for o in outs],
        compiler_params=_params("arbitrary"),
        name="in_proj",
    )(x2, mod4, mod4, w["g_norm1"], w["w_in"], w["g_q_lora"], w["w_uq"], w["g_kv_lora"], w["w_ukv"],
      w["gqn"], w["gqr"], w["gkn"], w["gkr"], w["qone"], w["kbias"], tab, w["w_gate"], w["b_gate"])


def _attn_prompt_kernel(fast_ref, q_ref, k_ref, v_ref, o_ref, vx_ref, m_ref, acc_ref, *, tq):
    i = pl.program_id(1)
    heads = range(MLA_HEADS)

    @pl.when(i == 0)
    def _():
        for hh in heads:
            vx_ref[hh, :, 0:V_DIM] = v_ref[:, hh * V_DIM:(hh + 1) * V_DIM]
            vx_ref[hh, :, V_DIM:] = jnp.ones((v_ref.shape[0], V_DIM), BF16)

    def scores(j, hh, masked):
        hs = slice(hh * HEAD_PAD, (hh + 1) * HEAD_PAD)
        s = _dot_nt(q_ref[:, hs], k_ref[pl.ds(pl.multiple_of(j * tq, tq), tq), hs])
        if masked:
            qc = lax.broadcasted_iota(jnp.int32, (tq, tq), 0) // CHUNK
            kc = lax.broadcasted_iota(jnp.int32, (tq, tq), 1) // CHUNK
            s = jnp.where(kc <= qc, s, -jnp.inf)
        return s

    def fast_diagonal():
        hq = tq // 2
        k0 = pl.multiple_of(i * tq, tq)
        qc = lax.broadcasted_iota(jnp.int32, (hq, hq), 0) // CHUNK
        kc = lax.broadcasted_iota(jnp.int32, (hq, hq), 1) // CHUNK
        tri = kc <= qc
        for hh in heads:
            hs = slice(hh * HEAD_PAD, (hh + 1) * HEAD_PAD)
            top = jnp.where(tri, _dot_nt(q_ref[0:hq, hs], k_ref[pl.ds(k0, hq), hs]), -jnp.inf)
            acc_ref[hh, 0:hq] = _dot(jnp.exp2(top).astype(BF16), vx_ref[hh, pl.ds(k0, hq), :])
            bot = _dot_nt(q_ref[hq:tq, hs], k_ref[pl.ds(k0, tq), hs])
            bot = jnp.concatenate([bot[:, 0:hq], jnp.where(tri, bot[:, hq:tq], -jnp.inf)], axis=1)
            acc_ref[hh, hq:tq] = _dot(jnp.exp2(bot).astype(BF16), vx_ref[hh, pl.ds(k0, tq), :])

    def vblock(j, hh):
        return vx_ref[hh, pl.ds(pl.multiple_of(j * tq, tq), tq), :]

    def fast_block(j, masked):
        for hh in heads:
            acc_ref[hh] += _dot(jnp.exp2(scores(j, hh, masked)).astype(BF16), vblock(j, hh))

    def safe_block(j, masked):
        for hh in heads:
            s = scores(j, hh, masked)
            m = m_ref[hh]
            m_new = jnp.maximum(m, jnp.max(s, axis=1, keepdims=True))
            p = jnp.exp2(s - m_new).astype(BF16)
            acc_ref[hh] = jnp.exp2(m - m_new) * acc_ref[hh] + _dot(p, vblock(j, hh))
            m_ref[hh] = m_new

    def sweep(block):
        def body(jj, c):
            block(2 * jj, False)
            block(2 * jj + 1, False)
            return c

        lax.fori_loop(0, i // 2, body, 0)

        @pl.when(i % 2 == 1)
        def _():
            block(i - 1, False)

    @pl.when(fast_ref[0] == 1)
    def _():
        fast_diagonal()
        sweep(fast_block)

    @pl.when(fast_ref[0] != 1)
    def _():
        m_ref[...] = jnp.full(m_ref.shape, -jnp.inf, F32)
        acc_ref[...] = jnp.zeros(acc_ref.shape, F32)
        safe_block(i, True)
        sweep(safe_block)

    for hh in heads:
        acc = acc_ref[hh]
        o_ref[:, hh * V_DIM:(hh + 1) * V_DIM] = (acc[:, 0:V_DIM] / acc[:, V_DIM:]).astype(BF16)


def _attention_prompt(fast, q, k, v, *, batch, seq, tq):
    nq = seq // tq
    return pl.pallas_call(
        functools.partial(_attn_prompt_kernel, tq=tq),
        grid=(batch, nq),
        in_specs=[pl.BlockSpec(memory_space=pltpu.SMEM),
                  pl.BlockSpec((tq, MLA_HEADS * HEAD_PAD), lambda b, i: (b * nq + i, 0)),
                  pl.BlockSpec((seq, MLA_HEADS * HEAD_PAD), lambda b, i: (b, 0)),
                  pl.BlockSpec((seq, MLA_WIDTH), lambda b, i: (b, 0))],
        out_specs=pl.BlockSpec((tq, MLA_WIDTH), lambda b, i: (b * nq + i, 0)),
        out_shape=jax.ShapeDtypeStruct((batch * seq, MLA_WIDTH), BF16),
        scratch_shapes=[pltpu.VMEM((MLA_HEADS, seq, 2 * V_DIM), BF16), pltpu.VMEM((MLA_HEADS, tq, 1), F32),
                        pltpu.VMEM((MLA_HEADS, tq, 2 * V_DIM), F32)],
        compiler_params=_params("arbitrary", "arbitrary"),
        name="mla_attn_prompt",
    )(fast, q, k, v)


def _attn_sample_kernel(fast_ref, bound_ref, q_ref, lat_ref, krt_ref, latn_ref, krtn_ref, tab_ref, wukt_ref, wuv_ref,
                        gkn_ref, gkr_ref, o_ref, wq_ref, latb_ref, s_ref, *, tkb):
    t = q_ref.shape[0] // 2
    past = lat_ref.shape[1]
    nk = MLA_HEADS * QK_NOPE
    half = QK_ROPE // 2
    streams = range(2)

    @pl.when(pl.program_id(0) == 0)
    def _():
        for e in streams:
            wq_ref[e, 0:nk, :] = wukt_ref[...]

    qr = []
    for e in streams:
        rows = slice(e * t, (e + 1) * t)
        parts = []
        for hh in range(MLA_HEADS):
            qn = (q_ref[rows, hh * HEAD_PAD:hh * HEAD_PAD + QK_NOPE].astype(F32) * gkn_ref[...]).astype(BF16)
            wq_ref[e, nk + hh * t:nk + (hh + 1) * t, :] = _dot(
                qn, wukt_ref[hh * QK_NOPE:(hh + 1) * QK_NOPE, :]).astype(BF16)
            parts.append(q_ref[rows, hh * HEAD_PAD + QK_NOPE:(hh + 1) * HEAD_PAD])
        qr.append(jnp.concatenate(parts, axis=0))

    def key_block(e, lat, krt, tab, width, valid):
        latb = lat.astype(BF16)
        g = _dot_nt(wq_ref[e], latb)
        ssr = jnp.sum(krt * krt, axis=0, keepdims=True)
        kg = krt * gkr_ref[...]
        x1, x2 = kg[0:half], kg[half:QK_ROPE]
        c, sn = tab[0:half], tab[half:QK_ROPE]
        rope = jnp.concatenate([x1 * c - x2 * sn, x2 * c + x1 * sn, jnp.zeros((LANES - QK_ROPE, width), F32)], axis=0)
        srope = _dot(qr[e], rope.astype(BF16))
        rows = []
        for hh in range(MLA_HEADS):
            kn = g[hh * QK_NOPE:(hh + 1) * QK_NOPE]
            inv = lax.rsqrt((jnp.sum(kn * kn, axis=0, keepdims=True) + ssr) * (1.0 / QK_DIM) + EPS)
            rows.append((g[nk + hh * t:nk + (hh + 1) * t] + srope[hh * t:(hh + 1) * t]) * inv)
        sc = jnp.concatenate(rows, axis=0) - bound_ref[0]
        if valid is not None:
            sc = jnp.where(valid, sc, -jnp.inf)
        return sc, latb

    def blocks():
        for blk in range(past // tkb):
            c0 = blk * tkb
            for e in streams:
                yield e, c0, tkb, key_block(e, lat_ref[e, c0:c0 + tkb, :], krt_ref[e, :, c0:c0 + tkb],
                                            tab_ref[:, c0:c0 + tkb], tkb, None)
        for e in streams:
            mine = lax.broadcasted_iota(jnp.int32, (1, 2 * t), 1) // t == e
            yield e, past, 2 * t, key_block(e, latn_ref[...], krtn_ref[...], tab_ref[:, past:past + 2 * t], 2 * t, mine)

    def finish(e, wlat, l):
        wlat = wlat.astype(BF16)
        for hh in range(MLA_HEADS):
            o = _dot(wlat[hh * t:(hh + 1) * t], wuv_ref[:, hh * V_DIM:(hh + 1) * V_DIM])
            o_ref[e * t:(e + 1) * t, hh * V_DIM:(hh + 1) * V_DIM] = (o / l[hh * t:(hh + 1) * t]).astype(BF16)

    @pl.when(fast_ref[0] == 1)
    def _():
        wlat = [jnp.zeros((MLA_HEADS * t, KV_LORA), F32) for _ in streams]
        lsum = [jnp.zeros((MLA_HEADS * t, LANES), F32) for _ in streams]
        for e, _, width, (sc, latb) in blocks():
            p = jnp.exp2(sc)
            wlat[e] += _dot(p.astype(BF16), latb)
            for c in range(width // LANES):
                lsum[e] += p[:, c * LANES:(c + 1) * LANES]
        for e in streams:
            finish(e, wlat[e], jnp.sum(lsum[e], axis=1, keepdims=True))

    @pl.when(fast_ref[0] != 1)
    def _():
        for e, c0, width, (sc, latb) in blocks():
            s_ref[e, :, c0:c0 + width] = sc
            latb_ref[e, c0:c0 + width, :] = latb
        for e in streams:
            s = s_ref[e]
            p = jnp.exp2(s - jnp.max(s, axis=1, keepdims=True))
            finish(e, _dot(p.astype(BF16), latb_ref[e]), jnp.sum(p, axis=1, keepdims=True))


def _attention_sample(q, past_lat, past_krt, lat_new, krt_new, w, *, tkb):
    batch, past, _ = past_lat.shape
    t = q.shape[0] // batch
    half = QK_ROPE // 2
    inv = ROPE_THETA ** (-np.arange(half, dtype=np.float64) / half)
    pos = np.concatenate([np.arange(past), past + np.arange(t), past + np.arange(t)]).astype(np.float64)
    ang = inv[:, None] * pos[None, :]
    tab = jnp.asarray(np.concatenate([np.cos(ang), np.sin(ang)], axis=0).astype(np.float32))
    s_pad = past + 2 * t
    return pl.pallas_call(
        functools.partial(_attn_sample_kernel, tkb=tkb),
        grid=(batch // 2,),
        in_specs=[pl.BlockSpec(memory_space=pltpu.SMEM), pl.BlockSpec(memory_space=pltpu.SMEM),
                  pl.BlockSpec((2 * t, MLA_HEADS * HEAD_PAD), lambda g: (g, 0)),
                  pl.BlockSpec((2, past, KV_LORA), lambda g: (g, 0, 0)),
                  pl.BlockSpec((2, QK_ROPE, past), lambda g: (g, 0, 0)),
                  pl.BlockSpec((2 * t, KV_LORA), lambda g: (g, 0)),
                  pl.BlockSpec((QK_ROPE, 2 * t), lambda g: (0, g)),
                  _const(tab.shape), _const(w["w_ukt"].shape), _const(w["w_uv"].shape),
                  _const((1, LANES)), _const((QK_ROPE, 1))],
        out_specs=pl.BlockSpec((2 * t, MLA_WIDTH), lambda g: (g, 0)),
        out_shape=jax.ShapeDtypeStruct((batch * t, MLA_WIDTH), BF16),
        scratch_shapes=[pltpu.VMEM((2, MLA_HEADS * (QK_NOPE + t), KV_LORA), BF16),
                        pltpu.VMEM((2, s_pad, KV_LORA), BF16),
                        pltpu.VMEM((2, MLA_HEADS * t, s_pad), F32)],
        compiler_params=_params("arbitrary"),
        name="mla_attn_sample",
    )(w["fast_softmax"], w["bound"], q, past_lat, past_krt, lat_new, krt_new, tab, w["w_ukt"], w["w_uv"], w["gkn"],
      w["gkr_col"])


def _gla_kernel(q_ref, k_ref, v_ref, la_ref, r_ref, s0_ref, g_ref, spread_ref, o_ref, sn_ref,
                st_ref, kp_ref, ap_ref, p_ref, on_ref, *, gpt):
    t_idx = pl.program_id(1)
    L, W, P = CHUNK, GLA_QK, LANES
    R = q_ref.shape[0]
    n_chunks = R // L
    cpg = n_chunks // gpt
    n_pairs = GLA_HEADS // 2

    lane_p = lax.broadcasted_iota(jnp.int32, (1, P), 1)
    even = lane_p < GLA_DK
    bd_mask = (lax.broadcasted_iota(jnp.int32, (2 * GLA_DV, P), 0) // GLA_DV
               == lax.broadcasted_iota(jnp.int32, (2 * GLA_DV, P), 1) // GLA_DK)

    @pl.when(t_idx == 0)
    def _():
        kp_ref[0:SUB, :] = jnp.zeros((SUB, W), F32)
        ap_ref[0:SUB, :] = jnp.zeros((SUB, W), F32)
        for gi in range(gpt):
            for pr in range(n_pairs):
                tt = s0_ref[gi, 2 * pr:2 * pr + 2].reshape(2 * GLA_DK, GLA_DV).T
                st_ref[gi, pr] = jnp.where(bd_mask, jnp.concatenate([tt, tt], axis=0), 0.0)

    q = q_ref[...]
    k = k_ref[...]
    la = la_ref[...]

    tri = (lax.broadcasted_iota(jnp.int32, (L, L), 0) >= lax.broadcasted_iota(jnp.int32, (L, L), 1)).astype(BF16)
    la_hi = la.astype(BF16)
    la2 = jnp.concatenate([la_hi, (la - la_hi.astype(F32)).astype(BF16)], axis=1)
    bs = []
    for c in range(n_chunks):
        t2 = _dot(tri, la2[c * L:(c + 1) * L, :])
        bs.append(t2[:, 0:W] + t2[:, W:2 * W])
    b = (jnp.concatenate(bs, axis=0) if n_chunks > 1 else bs[0]) * LOG2E
    b3 = b.reshape(n_chunks, L, W)

    def chunk_row(r):
        return jnp.broadcast_to(b3[:, r:r + 1, :], (n_chunks, L, W)).reshape(R, W)

    b_sub = jnp.broadcast_to(b.reshape(R // SUB, SUB, W)[:, 0:1, :], (R // SUB, SUB, W)).reshape(R, W)
    sub = (lax.broadcasted_iota(jnp.int32, (R, W), 0) % L) // SUB

    qt = q * jnp.exp2(b - b_sub)
    zb = jnp.zeros((), BF16)
    zq = jnp.zeros((SUB, P), F32)
    ktm = []
    for i in range(1, L // SUB):
        kt = (k * jnp.exp2(chunk_row(i * SUB) - b)).astype(BF16)
        ktm.append([[jnp.where((sub[:, 0:P] < i) & (even if e == 0 else ~even), kt[:, pr * P:(pr + 1) * P], zb)
                     for e in range(2)] for pr in range(n_pairs)])

    a = jnp.exp(la)
    kp_ref[SUB:SUB + R, :] = k
    ap_ref[SUB:SUB + R, :] = a
    p_ref[:, 0:W] = (q * k).astype(BF16)
    e = a
    for d in range(1, SUB):
        if d > 1:
            e = e * ap_ref[SUB - d + 1:SUB - d + 1 + R, :]
        p_ref[:, d * W:(d + 1) * W] = (q * kp_ref[SUB - d:SUB - d + R, :] * e).astype(BF16)
    cband = _dot(p_ref[...], spread_ref[...])
    same_sub = (lax.broadcasted_iota(jnp.int32, (L, W), 0) // SUB
                == (lax.broadcasted_iota(jnp.int32, (L, W), 1) % L) // SUB)

    qe = (q * jnp.exp2(b)).astype(BF16)
    kd = (k * jnp.exp2(chunk_row(L - 1) - b)).astype(BF16)
    zv = jnp.zeros((L, 2 * GLA_DV), BF16)

    o_intra, d_st, dec = {}, {}, []
    for c in range(n_chunks):
        rs = slice(c * L, (c + 1) * L)
        a_band = jnp.where(same_sub, pltpu.roll(cband[rs], W - (SUB - 1), 1, stride=1, stride_axis=0), 0.0)
        dec.append(jnp.exp2(b[c * L + L - 1:c * L + L, :]))
        for pr in range(n_pairs):
            ls = slice(pr * P, (pr + 1) * P)
            lhs_c = jnp.concatenate(
                [jnp.concatenate([qt[c * L + r * SUB:c * L + (r + 1) * SUB, ls] if i == r else zq
                                  for i in range(1, L // SUB)], axis=1) for r in range(L // SUB)],
                axis=0).astype(BF16)
            rhs_c = jnp.concatenate([jnp.concatenate([m[pr][0][rs], m[pr][1][rs]], axis=0) for m in ktm], axis=1)
            a_tot = (a_band[:, ls] + _dot_nt(lhs_c, rhs_c)).astype(BF16)
            vp = v_ref[rs, 2 * pr * GLA_DV:(2 * pr + 2) * GLA_DV]
            v_bd = jnp.concatenate([jnp.concatenate([vp[:, 0:GLA_DV], zv[:, 0:GLA_DV]], axis=1),
                                    jnp.concatenate([zv[:, 0:GLA_DV], vp[:, GLA_DV:]], axis=1)], axis=0)
            o_intra[c, pr] = _dot(a_tot, v_bd)
            d_st[c, pr] = jnp.where(bd_mask, _dot_tn(vp, kd[rs, ls]), 0.0)
    st_in = {}
    for gi in range(gpt):
        for pr in range(n_pairs):
            st = st_ref[gi, pr]
            for c in range(gi * cpg, (gi + 1) * cpg):
                st_in[c, pr] = st.astype(BF16)
                st = st * dec[c][:, pr * P:(pr + 1) * P] + d_st[c, pr]
            st_ref[gi, pr] = st
    for c in range(n_chunks):
        rs = slice(c * L, (c + 1) * L)
        for pr in range(n_pairs):
            on_ref[rs, 2 * pr * GLA_DV:(2 * pr + 2) * GLA_DV] = (
                o_intra[c, pr] + _dot_nt(qe[rs, pr * P:(pr + 1) * P], st_in[c, pr]))

    for hh in range(GLA_HEADS):
        hs = slice(hh * GLA_DV, (hh + 1) * GLA_DV)
        o = on_ref[:, hs]
        on = o * lax.rsqrt(jnp.mean(o * o, axis=-1, keepdims=True) + EPS) * g_ref[:, hs]
        o_ref[:, hs] = (on * r_ref[:, hs].astype(F32)).astype(BF16)

    @pl.when(t_idx == pl.num_programs(1) - 1)
    def _():
        for gi in range(gpt):
            for pr in range(n_pairs):
                st = st_ref[gi, pr]
                tt = jnp.where(even, st[0:GLA_DV], st[GLA_DV:2 * GLA_DV])
                sn_ref[gi, 2 * pr:2 * pr + 2] = tt.T.reshape(2, GLA_DK, GLA_DV)


def _band_spread():
    m = np.zeros((SUB, GLA_HEADS, GLA_DK, GLA_QK), np.float32)
    for d in range(SUB):
        for h in range(GLA_HEADS):
            m[d, h, :, h * GLA_DK + SUB - 1 - d] = 1.0
    return jnp.asarray(m.reshape(SUB * GLA_QK, GLA_QK), BF16)


def _gla(gq, gk, gv, la, gr, s0, g_out, *, groups, rows_per_group, tc):
    gpt = max(1, tc // rows_per_group)
    nt = max(1, rows_per_group // tc)

    def rows(c):
        return pl.BlockSpec((tc, c), lambda g, t: (g * nt + t, 0))

    state = pl.BlockSpec((gpt, GLA_HEADS, GLA_DK, GLA_DV), lambda g, t: (g, 0, 0, 0))
    spread = _band_spread()
    return pl.pallas_call(
        functools.partial(_gla_kernel, gpt=gpt),
        grid=(groups // gpt, nt),
        in_specs=[rows(GLA_QK), rows(GLA_QK), rows(GLA_WIDTH), rows(GLA_QK), rows(GLA_WIDTH), state,
                  _const((1, GLA_WIDTH)), _const(spread.shape)],
        out_specs=[rows(GLA_WIDTH), state],
        out_shape=[jax.ShapeDtypeStruct((groups * rows_per_group, GLA_WIDTH), BF16),
                   jax.ShapeDtypeStruct((groups, GLA_HEADS, GLA_DK, GLA_DV), F32)],
        scratch_shapes=[pltpu.VMEM((gpt, GLA_HEADS // 2, 2 * GLA_DV, LANES), F32),
                        pltpu.VMEM((SUB + tc, GLA_QK), F32), pltpu.VMEM((SUB + tc, GLA_QK), F32),
                        pltpu.VMEM((tc, SUB * GLA_QK), BF16), pltpu.VMEM((tc, GLA_WIDTH), F32)],
        compiler_params=_params("arbitrary", "arbitrary"),
        name="gla",
    )(gq, gk, gv, la, gr, s0, g_out, spread)


def _mlp_kernel(x_ref, a_ref, b_ref, g1_ref, sh2_ref, sc2_ref, g2_ref, gn_ref, wo_ref, wu_ref, wd_ref, y_ref,
                *, gpt, tf):
    tm, d = x_ref.shape

    def per_group(val, ref, scale_plus_one=False):
        m = ref[...]
        if scale_plus_one:
            m = 1.0 + m
        return (val.reshape(gpt, tm // gpt, d) * m).reshape(tm, d)

    mix = jnp.concatenate([a_ref[...], b_ref[...]], axis=1)
    x1 = x_ref[...] + per_group(_dot(mix, wo_ref[...]), g1_ref)
    xn = x1 * lax.rsqrt(jnp.mean(x1 * x1, axis=-1, keepdims=True) + EPS) * gn_ref[...]
    h2 = (per_group(xn, sc2_ref, True).reshape(gpt, tm // gpt, d) + sh2_ref[...]).reshape(tm, d).astype(BF16)
    acc = jnp.zeros((tm, d), F32)
    for j in range(wu_ref.shape[1] // tf):
        u = jnp.maximum(_dot(h2, wu_ref[:, j * tf:(j + 1) * tf]), 0.0)
        acc += _dot((u * u).astype(BF16), wd_ref[j * tf:(j + 1) * tf, :])
    y_ref[...] = x1 + per_group(acc, g2_ref)


def _mlp(x2, a_out, b_out, mod4, w, *, rows_per_group, tm):
    n, d = x2.shape
    gpt = max(1, tm // rows_per_group)
    tpg = max(1, rows_per_group // tm)

    def mod_spec(j):
        return pl.BlockSpec((gpt, None, 1, d), lambda i: ((i // tpg) if gpt == 1 else i, j, 0, 0))

    def rows(c):
        return pl.BlockSpec((tm, c), lambda i: (i, 0))

    return pl.pallas_call(
        functools.partial(_mlp_kernel, gpt=gpt, tf=1024),
        grid=(n // tm,),
        in_specs=[rows(d), rows(MLA_WIDTH), rows(GLA_WIDTH), mod_spec(2), mod_spec(3), mod_spec(4), mod_spec(5),
                  _const((1, d)), _const(w["w_out"].shape), _const(w["w_up"].shape), _const(w["w_down"].shape)],
        out_specs=rows(d),
        out_shape=jax.ShapeDtypeStruct((n, d), F32),
        compiler_params=_params("arbitrary"),
        name="out_proj_mlp",
    )(x2, a_out, b_out, mod4, mod4, mod4, mod4, w["g_norm2"], w["w_out"], w["w_up"], w["w_down"])


def _rope_table(start, count, repeat=1):
    half = QK_ROPE // 2
    inv = ROPE_THETA ** (-np.arange(half, dtype=np.float64) / half)
    ang = (start + np.arange(count, dtype=np.float64))[:, None] * inv[None, :]
    c, s, z = np.cos(ang), np.sin(ang), np.zeros_like(ang)
    tab = np.concatenate([c, c, z, z, -s, z, z, z, z, s, z, z], axis=1).astype(np.float32)
    return jnp.asarray(np.tile(tab, (repeat, 1)))


def _pad_gain(g_rope):
    return jnp.concatenate([g_rope, jnp.zeros((LANES - QK_ROPE,), F32)]).reshape(1, LANES)


def _relayout_kernel(wint_ref, wuq_ref, wukv_ref, wg_ref, win_o, wuq_o, wukv_o, wukt_o, wuv_o, wg_o):
    s = np.cumsum([0, Q_LORA, KV_LORA, QK_ROPE, GLA_QK, GLA_QK, GLA_WIDTH, GLA_GATE_RANK, GLA_WIDTH])

    def piece(i):
        return wint_ref[int(s[i]):int(s[i + 1]), :]

    d = wint_ref.shape[1]
    zeros = jnp.zeros((LANES - QK_ROPE - GLA_GATE_RANK, d), F32)
    win_o[:, 0:Q_LORA] = piece(0).T.astype(BF16)
    win_o[:, Q_LORA:C_QKR[1]] = jnp.concatenate([piece(2), piece(6), zeros], axis=0).T.astype(BF16)
    win_o[:, C_KV[0]:C_KV[1]] = piece(1).T.astype(BF16)
    win_o[:, C_GQ[0]:C_GQ[1]] = (piece(3).T * (GLA_DK ** -0.5)).astype(BF16)
    win_o[:, C_GK[0]:C_GK[1]] = piece(4).T.astype(BF16)
    win_o[:, C_GV[0]:C_GV[1]] = piece(5).T.astype(BF16)
    win_o[:, C_GR[0]:C_GR[1]] = piece(7).T.astype(BF16)

    zq = jnp.zeros((Q_LORA, HEAD_PAD - QK_DIM), BF16)
    kvw = QK_NOPE + V_DIM
    for hh in range(MLA_HEADS):
        wuq_o[:, hh * HEAD_PAD:hh * HEAD_PAD + QK_DIM] = wuq_ref[:, hh * QK_DIM:(hh + 1) * QK_DIM].astype(BF16)
        wuq_o[:, hh * HEAD_PAD + QK_DIM:(hh + 1) * HEAD_PAD] = zq
        uk = wukv_ref[:, hh * kvw:hh * kvw + QK_NOPE]
        uv = wukv_ref[:, hh * kvw + QK_NOPE:(hh + 1) * kvw].astype(BF16)
        wukv_o[:, hh * QK_NOPE:(hh + 1) * QK_NOPE] = uk.astype(BF16)
        wukv_o[:, (MLA_HEADS + hh) * V_DIM:(MLA_HEADS + hh + 1) * V_DIM] = uv
        wukt_o[hh * QK_NOPE:(hh + 1) * QK_NOPE, :] = uk.T.astype(BF16)
        wuv_o[:, hh * V_DIM:(hh + 1) * V_DIM] = uv

    wg_o[...] = jnp.zeros(wg_o.shape, BF16)
    wg_o[QK_ROPE:QK_ROPE + GLA_GATE_RANK, :] = wg_ref[...].astype(BF16)


def _relayout(w_in_t, w_uq, w_ukv, w_gate_up):
    d = w_in_t.shape[1]
    shapes = [(d, C_GR[1]), (Q_LORA, MLA_HEADS * HEAD_PAD), (KV_LORA, MLA_HEADS * (QK_NOPE + V_DIM)),
              (MLA_HEADS * QK_NOPE, KV_LORA), (KV_LORA, MLA_WIDTH), (LANES, GLA_QK)]
    return pl.pallas_call(
        _relayout_kernel,
        out_shape=[jax.ShapeDtypeStruct(sh, BF16) for sh in shapes],
        compiler_params=pltpu.CompilerParams(vmem_limit_bytes=VMEM_LIMIT),
        name="weight_relayout",
    )(w_in_t, w_uq, w_ukv, w_gate_up)


def _prep_weights(w_in, g_norm1, g_q_lora, w_uq, g_kv_lora, w_ukv, g_q_head, g_k_head,
                  w_gate_up, b_gate_up, g_gla_out, w_out, g_norm2, w_up, w_down):
    d = w_in.shape[0]
    w_in_p, w_uq_p, w_ukv_p, w_ukt, w_uv, w_gate = _relayout(w_in.T, w_uq, w_ukv, w_gate_up)
    qscale = QK_DIM ** -0.5 * LOG2E
    bound = 1.02 * QK_DIM ** 0.5 * LOG2E * jnp.max(jnp.abs(g_q_head)) * jnp.max(jnp.abs(g_k_head))
    lane = jnp.arange(LANES) == QK_ROPE
    return {
        "w_in": w_in_p, "g_norm1": g_norm1.reshape(1, d), "g_q_lora": g_q_lora.reshape(1, Q_LORA),
        "w_uq": w_uq_p, "g_kv_lora": g_kv_lora.reshape(1, KV_LORA), "w_ukv": w_ukv_p, "w_ukt": w_ukt, "w_uv": w_uv,
        "gkr_col": g_k_head[QK_NOPE:].reshape(QK_ROPE, 1),
        "qone": lane.astype(F32).reshape(1, LANES), "kbias": jnp.where(lane, -bound, 0.0).reshape(1, LANES),
        "fast_softmax": (bound <= MAX_FIXED_SHIFT).astype(jnp.int32).reshape(1), "bound": bound.reshape(1),
        "gqn": (g_q_head[:QK_NOPE] * qscale).reshape(1, LANES), "gqr": _pad_gain(g_q_head[QK_NOPE:] * qscale),
        "gkn": g_k_head[:QK_NOPE].reshape(1, LANES), "gkr": _pad_gain(g_k_head[QK_NOPE:]),
        "w_gate": w_gate, "b_gate": b_gate_up.reshape(1, GLA_QK),
        "g_gla_out": g_gla_out.reshape(1, GLA_WIDTH), "w_out": w_out.astype(BF16),
        "g_norm2": g_norm2.reshape(1, d), "w_up": w_up.astype(BF16), "w_down": w_down.astype(BF16),
    }


def _layer(x, mod, past_lat, past_kr, s0, w, *, tm):
    batch, seq, d = x.shape
    n = batch * seq
    past = 0 if past_lat is None else past_lat.shape[1]
    x2 = x.reshape(n, d)
    mod4 = mod.reshape(batch, 6, 1, d)
    tm = min(tm, n)
    tab = _rope_table(past, seq, repeat=max(1, tm // seq))
    if past == 0:
        lat, krt, q, k, v, gq, gk, gv, la, gr = _projection(x2, mod4, tab, w, rows_per_group=seq, tm=tm, prompt=True)
        kr = jnp.swapaxes(krt, 1, 2)
        a_out = _attention_prompt(w["fast_softmax"], q, k, v, batch=batch, seq=seq, tq=min(512, seq))
    else:
        assert seq == CHUNK and past % CHUNK == 0
        lat, kr, krt, q, gq, gk, gv, la, gr = _projection(x2, mod4, tab, w, rows_per_group=seq, tm=tm, prompt=False)
        a_out = _attention_sample(q, past_lat, jnp.swapaxes(past_kr, 1, 2), lat, krt, w, tkb=min(512, past))
    b_out, s_new = _gla(gq, gk, gv, la, gr, s0, w["g_gla_out"], groups=batch, rows_per_group=seq,
                        tc=tm)
    y = _mlp(x2, a_out, b_out, mod4, w, rows_per_group=seq, tm=tm)
    return (y.reshape(batch, seq, d), lat.reshape(batch, seq, KV_LORA), kr.reshape(batch, seq, QK_ROPE), s_new)


def kernel(x_prompt, x_sample, cache_mla_latent, cache_mla_krope, state_gla, c_prompt, c_sample,
           w_ada, b_ada, g_norm1, w_in, g_q_lora, w_uq, g_kv_lora, w_ukv, g_q_head, g_k_head,
           w_gate_up, b_gate_up, g_gla_out, w_out, g_norm2, w_up, w_down):
    nb = x_prompt.shape[0]
    depth = w_ada.shape[0]
    y_p, y_s = x_prompt, x_sample
    outs = [[] for _ in range(6)]
    c_all = jnp.concatenate([c_prompt, c_sample], axis=0)
    for l in range(depth):
        w = _prep_weights(w_in[l], g_norm1[l], g_q_lora[l], w_uq[l], g_kv_lora[l], w_ukv[l], g_q_head[l],
                          g_k_head[l], w_gate_up[l], b_gate_up[l], g_gla_out[l], w_out[l], g_norm2[l],
                          w_up[l], w_down[l])
        mod = _modulation(c_all, w_ada[l], b_ada[l])
        zero_state = jnp.zeros((nb, GLA_HEADS, GLA_DK, GLA_DV), x_prompt.dtype)
        y_p, lat, kr, st = _layer(y_p, mod[:nb], None, None, zero_state, w, tm=512)
        outs[0].append(lat); outs[1].append(kr); outs[2].append(st)
        y_s, lat, kr, st = _layer(y_s, mod[nb:], cache_mla_latent[l], cache_mla_krope[l], state_gla[l], w, tm=512)
        outs[3].append(lat); outs[4].append(kr); outs[5].append(st)
    return (y_p, y_s) + tuple(jnp.stack(o) for o in outs)
```

```python
import functools

import jax
import jax.numpy as jnp
import numpy as np
from jax import lax
from jax.experimental import pallas as pl
from jax.experimental.pallas import tpu as pltpu

F32 = jnp.float32
BF16 = jnp.bfloat16

CHUNK = 64
EPS = 1e-6
MLA_HEADS = 4
Q_LORA = 384
KV_LORA = 256
QK_NOPE = 128
QK_ROPE = 64
QK_DIM = QK_NOPE + QK_ROPE
V_DIM = 128
ROPE_THETA = 10000.0
GLA_HEADS = 4
GLA_DK = 64
GLA_DV = 128
GLA_GATE_RANK = 16
GLA_TAU = 16.0
GLA_QK = GLA_HEADS * GLA_DK
GLA_WIDTH = GLA_HEADS * GLA_DV
MLA_WIDTH = MLA_HEADS * V_DIM
HEAD_PAD = 256
SUB = 8
LOG2E = 1.4426950408889634
MAX_FIXED_SHIFT = 48.0

LANES = 128
VMEM_LIMIT = 56 * 1024 * 1024
ROW_TILE = 512
ATTN_TILE = 512
CACHE_BLOCK = 512
FF_SLICE = 1024
MOD_TILE = 1024

C_QKR = (0, 512)
C_KV = (512, 768)
C_GQ = (768, 1024)
C_GK = (1024, 1280)
C_GV = (1280, 1792)
C_GR = (1792, 2304)


def _dot(a, b):
    return jnp.dot(a, b, preferred_element_type=F32)


def _dot_nt(a, b):
    return lax.dot_general(a, b, (((1,), (1,)), ((), ())), preferred_element_type=F32)


def _dot_tn(a, b):
    return lax.dot_general(a, b, (((0,), (0,)), ((), ())), preferred_element_type=F32)


def _rope_tile(t, c, sa, sb):
    return t * c + pltpu.roll(t, 96, 1) * sa + pltpu.roll(t, 32, 1) * sb


def _params(*sem):
    return pltpu.CompilerParams(dimension_semantics=sem, vmem_limit_bytes=VMEM_LIMIT)


def _const(shape):
    return pl.BlockSpec(shape, lambda *_: (0,) * len(shape), pipeline_mode=pl.Buffered(1))


def _mod_kernel(c_ref, w_ref, b_ref, o_ref):
    c = c_ref[...]
    s = (c * jax.nn.sigmoid(c)).astype(BF16)
    o_ref[...] = _dot(s, w_ref[...].astype(BF16)) + b_ref[...]


def _modulation(c_all, w_ada, b_ada):
    g, d = c_all.shape
    n = w_ada.shape[1]
    tn = MOD_TILE
    return pl.pallas_call(
        _mod_kernel,
        grid=(n // tn,),
        in_specs=[pl.BlockSpec((g, d), lambda j: (0, 0)),
                  pl.BlockSpec((d, tn), lambda j: (0, j)),
                  pl.BlockSpec((1, tn), lambda j: (0, j))],
        out_specs=pl.BlockSpec((g, tn), lambda j: (0, j)),
        out_shape=jax.ShapeDtypeStruct((g, n), F32),
        compiler_params=_params("arbitrary"),
        name="adaln_mod",
    )(c_all, w_ada, b_ada.reshape(1, n))


def _proj_kernel(x_ref, sh_ref, sc_ref, g1_ref, win_ref, gql_ref, wuq_ref, gkv_ref, wukv_ref,
                 gqn_ref, gqr_ref, gkn_ref, gkr_ref, qone_ref, kbias_ref, tab_ref, wg_ref, bg_ref,
                 *outs, gpt, prompt):
    if prompt:
        lat_ref, krt_ref, q_ref, k_ref, v_ref, gq_ref, gk_ref, gv_ref, la_ref, gr_ref = outs
    else:
        lat_ref, kr_ref, krt_ref, q_ref, gq_ref, gk_ref, gv_ref, la_ref, gr_ref = outs
    tm, d = x_ref.shape
    x = x_ref[...]
    xn = x * lax.rsqrt(jnp.mean(x * x, axis=-1, keepdims=True) + EPS)
    h = (xn.reshape(gpt, tm // gpt, d) * (g1_ref[...] * (1.0 + sc_ref[...])) + sh_ref[...]).reshape(tm, d)
    hb = h.astype(BF16)

    def col(c):
        return _dot(hb, win_ref[:, c[0]:c[1]])

    tab = tab_ref[...]
    cos, sa, sb = tab[:, 0:LANES], tab[:, LANES:2 * LANES], tab[:, 2 * LANES:3 * LANES]

    qkr = col(C_QKR)
    krg = qkr[:, Q_LORA:]
    z = _dot(krg.astype(BF16), wg_ref[...]) + bg_ref[...]
    la_ref[...] = (jnp.minimum(z, 0.0) - jnp.log(1.0 + jnp.exp(-jnp.abs(z)))) * (1.0 / GLA_TAU)
    r = col(C_GR)
    gr_ref[...] = (r * jax.nn.sigmoid(r)).astype(BF16)

    cq = qkr[:, 0:Q_LORA]
    cqn = cq * lax.rsqrt(jnp.mean(cq * cq, axis=-1, keepdims=True) + EPS) * gql_ref[...]
    qp = _dot(cqn.astype(BF16), wuq_ref[...])
    for hh in range(MLA_HEADS):
        nope = qp[:, hh * HEAD_PAD:hh * HEAD_PAD + QK_NOPE]
        rt = qp[:, hh * HEAD_PAD + QK_NOPE:(hh + 1) * HEAD_PAD]
        ss = jnp.sum(nope * nope, axis=-1, keepdims=True) + jnp.sum(rt * rt, axis=-1, keepdims=True)
        inv = lax.rsqrt(ss * (1.0 / QK_DIM) + EPS)
        q_ref[:, hh * HEAD_PAD:hh * HEAD_PAD + QK_NOPE] = (nope * inv * gqn_ref[...]).astype(BF16)
        rq = _rope_tile(rt * inv * gqr_ref[...], cos, sa, sb)
        q_ref[:, hh * HEAD_PAD + QK_NOPE:(hh + 1) * HEAD_PAD] = (rq + qone_ref[...]).astype(BF16)

    ckv = col(C_KV)
    lat = ckv * lax.rsqrt(jnp.mean(ckv * ckv, axis=-1, keepdims=True) + EPS) * gkv_ref[...]
    lat_ref[...] = lat
    krt_ref[...] = krg.T[0:QK_ROPE, :]
    if prompt:
        lane = lax.broadcasted_iota(jnp.int32, (1, LANES), 1)
        krm = jnp.where(lane < QK_ROPE, krg, 0.0)
        ssr = jnp.sum(krm * krm, axis=-1, keepdims=True)
        rk = _rope_tile(krm * gkr_ref[...], cos, sa, sb)
        kv = _dot(lat.astype(BF16), wukv_ref[...])
        for hh in range(MLA_HEADS):
            kn = kv[:, hh * QK_NOPE:(hh + 1) * QK_NOPE]
            inv = lax.rsqrt((jnp.sum(kn * kn, axis=-1, keepdims=True) + ssr) * (1.0 / QK_DIM) + EPS)
            k_ref[:, hh * HEAD_PAD:hh * HEAD_PAD + QK_NOPE] = (kn * inv * gkn_ref[...]).astype(BF16)
            k_ref[:, hh * HEAD_PAD + QK_NOPE:(hh + 1) * HEAD_PAD] = (rk * inv + kbias_ref[...]).astype(BF16)
        v_ref[...] = kv[:, MLA_HEADS * QK_NOPE:].astype(BF16)
    else:
        kr_ref[...] = krg[:, 0:QK_ROPE]

    gq_ref[...] = col(C_GQ)
    gk_ref[...] = col(C_GK)
    gv_ref[...] = col(C_GV).astype(BF16)


def _projection(x2, mod4, tab, w, *, rows_per_group, tm, prompt):
    n, d = x2.shape
    gpt = max(1, tm // rows_per_group)
    tpg = max(1, rows_per_group // tm)
    ntab = tab.shape[0] // tm

    def mod_spec(j):
        return pl.BlockSpec((gpt, None, 1, d), lambda i: ((i // tpg) if gpt == 1 else i, j, 0, 0))

    def rows(c):
        return pl.BlockSpec((tm, c), lambda i: (i, 0))

    def out(c, t):
        return rows(c), jax.ShapeDtypeStruct((n, c), t)

    gla_outs = [out(GLA_QK, F32), out(GLA_QK, F32), out(GLA_WIDTH, BF16), out(GLA_QK, F32), out(GLA_WIDTH, BF16)]
    qo = out(MLA_HEADS * HEAD_PAD, BF16)
    if prompt:
        krt = (pl.BlockSpec((None, QK_ROPE, tm), lambda i: (i // tpg, 0, i % tpg)),
               jax.ShapeDtypeStruct((n // rows_per_group, QK_ROPE, rows_per_group), F32))
        outs = [out(KV_LORA, F32), krt, qo, out(MLA_HEADS * HEAD_PAD, BF16), out(MLA_WIDTH, BF16)] + gla_outs
    else:
        krt = (pl.BlockSpec((QK_ROPE, tm), lambda i: (0, i)), jax.ShapeDtypeStruct((QK_ROPE, n), F32))
        outs = [out(KV_LORA, F32), out(QK_ROPE, F32), krt, qo] + gla_outs
    return pl.pallas_call(
        functools.partial(_proj_kernel, gpt=gpt, prompt=prompt),
        grid=(n // tm,),
        in_specs=[rows(d), mod_spec(0), mod_spec(1), _const((1, d)), _const(w["w_in"].shape),
                  _const((1, Q_LORA)), _const(w["w_uq"].shape), _const((1, KV_LORA)), _const(w["w_ukv"].shape),
                  _const((1, LANES)), _const((1, LANES)), _const((1, LANES)), _const((1, LANES)),
                  _const((1, LANES)), _const((1, LANES)),
                  pl.BlockSpec((tm, 3 * LANES), lambda i: (i % ntab, 0)),
                  _const(w["w_gate"].shape), _const((1, GLA_QK))],
        out_specs=[o[0] for o in outs],
        out_shape=[o[1] for o in outs],
        compiler_params=_params("arbitrary"),
        name="in_proj",
    )(x2, mod4, mod4, w["g_norm1"], w["w_in"], w["g_q_lora"], w["w_uq"], w["g_kv_lora"], w["w_ukv"],
      w["gqn"], w["gqr"], w["gkn"], w["gkr"], w["qone"], w["kbias"], tab, w["w_gate"], w["b_gate"])


def _attn_prompt_kernel(fast_ref, q_ref, k_ref, v_ref, o_ref, vx_ref, m_ref, acc_ref, *, tq):
    i = pl.program_id(1)
    heads = range(MLA_HEADS)

    @pl.when(i == 0)
    def _():
        for hh in heads:
            vx_ref[hh, :, 0:V_DIM] = v_ref[:, hh * V_DIM:(hh + 1) * V_DIM]
            vx_ref[hh, :, V_DIM:] = jnp.ones((v_ref.shape[0], V_DIM), BF16)

    def scores(j, hh, masked):
        hs = slice(hh * HEAD_PAD, (hh + 1) * HEAD_PAD)
        s = _dot_nt(q_ref[:, hs], k_ref[pl.ds(pl.multiple_of(j * tq, tq), tq), hs])
        if masked:
            qc = lax.broadcasted_iota(jnp.int32, (tq, tq), 0) // CHUNK
            kc = lax.broadcasted_iota(jnp.int32, (tq, tq), 1) // CHUNK
            s = jnp.where(kc <= qc, s, -jnp.inf)
        return s

    def fast_diagonal():
        hq = tq // 2
        k0 = pl.multiple_of(i * tq, tq)
        qc = lax.broadcasted_iota(jnp.int32, (hq, hq), 0) // CHUNK
        kc = lax.broadcasted_iota(jnp.int32, (hq, hq), 1) // CHUNK
        tri = kc <= qc
        for hh in heads:
            hs = slice(hh * HEAD_PAD, (hh + 1) * HEAD_PAD)
            top = jnp.where(tri, _dot_nt(q_ref[0:hq, hs], k_ref[pl.ds(k0, hq), hs]), -jnp.inf)
            acc_ref[hh, 0:hq] = _dot(jnp.exp2(top).astype(BF16), vx_ref[hh, pl.ds(k0, hq), :])
            bot = _dot_nt(q_ref[hq:tq, hs], k_ref[pl.ds(k0, tq), hs])
            bot = jnp.concatenate([bot[:, 0:hq], jnp.where(tri, bot[:, hq:tq], -jnp.inf)], axis=1)
            acc_ref[hh, hq:tq] = _dot(jnp.exp2(bot).astype(BF16), vx_ref[hh, pl.ds(k0, tq), :])

    def vblock(j, hh):
        return vx_ref[hh, pl.ds(pl.multiple_of(j * tq, tq), tq), :]

    def fast_block(j, masked):
        for hh in heads:
            acc_ref[hh] += _dot(jnp.exp2(scores(j, hh, masked)).astype(BF16), vblock(j, hh))

    def safe_block(j, masked):
        for hh in heads:
            s = scores(j, hh, masked)
            m = m_ref[hh]
            m_new = jnp.maximum(m, jnp.max(s, axis=1, keepdims=True))
            p = jnp.exp2(s - m_new).astype(BF16)
            acc_ref[hh] = jnp.exp2(m - m_new) * acc_ref[hh] + _dot(p, vblock(j, hh))
            m_ref[hh] = m_new

    def sweep(block):
        def body(jj, c):
            block(2 * jj, False)
            block(2 * jj + 1, False)
            return c

        lax.fori_loop(0, i // 2, body, 0)

        @pl.when(i % 2 == 1)
        def _():
            block(i - 1, False)

    @pl.when(fast_ref[0] == 1)
    def _():
        fast_diagonal()
        sweep(fast_block)

    @pl.when(fast_ref[0] != 1)
    def _():
        m_ref[...] = jnp.full(m_ref.shape, -jnp.inf, F32)
        acc_ref[...] = jnp.zeros(acc_ref.shape, F32)
        safe_block(i, True)
        sweep(safe_block)

    for hh in heads:
        acc = acc_ref[hh]
        o_ref[:, hh * V_DIM:(hh + 1) * V_DIM] = (acc[:, 0:V_DIM] / acc[:, V_DIM:]).astype(BF16)


def _attention_prompt(fast, q, k, v, *, batch, seq, tq):
    nq = seq // tq
    return pl.pallas_call(
        functools.partial(_attn_prompt_kernel, tq=tq),
        grid=(batch, nq),
        in_specs=[pl.BlockSpec(memory_space=pltpu.SMEM),
                  pl.BlockSpec((tq, MLA_HEADS * HEAD_PAD), lambda b, i: (b * nq + i, 0)),
                  pl.BlockSpec((seq, MLA_HEADS * HEAD_PAD), lambda b, i: (b, 0)),
                  pl.BlockSpec((seq, MLA_WIDTH), lambda b, i: (b, 0))],
        out_specs=pl.BlockSpec((tq, MLA_WIDTH), lambda b, i: (b * nq + i, 0)),
        out_shape=jax.ShapeDtypeStruct((batch * seq, MLA_WIDTH), BF16),
        scratch_shapes=[pltpu.VMEM((MLA_HEADS, seq, 2 * V_DIM), BF16), pltpu.VMEM((MLA_HEADS, tq, 1), F32),
                        pltpu.VMEM((MLA_HEADS, tq, 2 * V_DIM), F32)],
        compiler_params=_params("arbitrary", "arbitrary"),
        name="mla_attn_prompt",
    )(fast, q, k, v)


def _attn_sample_kernel(fast_ref, bound_ref, q_ref, lat_ref, krt_ref, latn_ref, krtn_ref, tab_ref, wukt_ref, wuv_ref,
                        gkn_ref, gkr_ref, o_ref, wq_ref, latb_ref, s_ref, *, tkb):
    t = q_ref.shape[0] // 2
    past = lat_ref.shape[1]
    nk = MLA_HEADS * QK_NOPE
    half = QK_ROPE // 2
    streams = range(2)

    @pl.when(pl.program_id(0) == 0)
    def _():
        for e in streams:
            wq_ref[e, 0:nk, :] = wukt_ref[...]

    qr = []
    for e in streams:
        rows = slice(e * t, (e + 1) * t)
        parts = []
        for hh in range(MLA_HEADS):
            qn = (q_ref[rows, hh * HEAD_PAD:hh * HEAD_PAD + QK_NOPE].astype(F32) * gkn_ref[...]).astype(BF16)
            wq_ref[e, nk + hh * t:nk + (hh + 1) * t, :] = _dot(
                qn, wukt_ref[hh * QK_NOPE:(hh + 1) * QK_NOPE, :]).astype(BF16)
            parts.append(q_ref[rows, hh * HEAD_PAD + QK_NOPE:(hh + 1) * HEAD_PAD])
        qr.append(jnp.concatenate(parts, axis=0))

    def key_block(e, lat, krt, tab, width, valid):
        latb = lat.astype(BF16)
        g = _dot_nt(wq_ref[e], latb)
        ssr = jnp.sum(krt * krt, axis=0, keepdims=True)
        kg = krt * gkr_ref[...]
        x1, x2 = kg[0:half], kg[half:QK_ROPE]
        c, sn = tab[0:half], tab[half:QK_ROPE]
        rope = jnp.concatenate([x1 * c - x2 * sn, x2 * c + x1 * sn, jnp.zeros((LANES - QK_ROPE, width), F32)], axis=0)
        srope = _dot(qr[e], rope.astype(BF16))
        rows = []
        for hh in range(MLA_HEADS):
            kn = g[hh * QK_NOPE:(hh + 1) * QK_NOPE]
            inv = lax.rsqrt((jnp.sum(kn * kn, axis=0, keepdims=True) + ssr) * (1.0 / QK_DIM) + EPS)
            rows.append((g[nk + hh * t:nk + (hh + 1) * t] + srope[hh * t:(hh + 1) * t]) * inv)
        sc = jnp.concatenate(rows, axis=0) - bound_ref[0]
        if valid is not None:
            sc = jnp.where(valid, sc, -jnp.inf)
        return sc, latb

    def blocks():
        for blk in range(past // tkb):
            c0 = blk * tkb
            for e in streams:
                yield e, c0, tkb, key_block(e, lat_ref[e, c0:c0 + tkb, :], krt_ref[e, :, c0:c0 + tkb],
                                            tab_ref[:, c0:c0 + tkb], tkb, None)
        for e in streams:
            mine = lax.broadcasted_iota(jnp.int32, (1, 2 * t), 1) // t == e
            yield e, past, 2 * t, key_block(e, latn_ref[...], krtn_ref[...], tab_ref[:, past:past + 2 * t], 2 * t, mine)

    def finish(e, wlat, l):
        wlat = wlat.astype(BF16)
        for hh in range(MLA_HEADS):
            o = _dot(wlat[hh * t:(hh + 1) * t], wuv_ref[:, hh * V_DIM:(hh + 1) * V_DIM])
            o_ref[e * t:(e + 1) * t, hh * V_DIM:(hh + 1) * V_DIM] = (o / l[hh * t:(hh + 1) * t]).astype(BF16)

    @pl.when(fast_ref[0] == 1)
    def _():
        wlat = [jnp.zeros((MLA_HEADS * t, KV_LORA), F32) for _ in streams]
        lsum = [jnp.zeros((MLA_HEADS * t, LANES), F32) for _ in streams]
        for e, _, width, (sc, latb) in blocks():
            p = jnp.exp2(sc)
            wlat[e] += _dot(p.astype(BF16), latb)
            for c in range(width // LANES):
                lsum[e] += p[:, c * LANES:(c + 1) * LANES]
        for e in streams:
            finish(e, wlat[e], jnp.sum(lsum[e], axis=1, keepdims=True))

    @pl.when(fast_ref[0] != 1)
    def _():
        for e, c0, width, (sc, latb) in blocks():
            s_ref[e, :, c0:c0 + width] = sc
            latb_ref[e, c0:c0 + width, :] = latb
        for e in streams:
            s = s_ref[e]
            p = jnp.exp2(s - jnp.max(s, axis=1, keepdims=True))
            finish(e, _dot(p.astype(BF16), latb_ref[e]), jnp.sum(p, axis=1, keepdims=True))


def _attention_sample(q, past_lat, past_krt, lat_new, krt_new, w, *, tkb):
    batch, past, _ = past_lat.shape
    t = q.shape[0] // batch
    half = QK_ROPE // 2
    inv = ROPE_THETA ** (-np.arange(half, dtype=np.float64) / half)
    pos = np.concatenate([np.arange(past), past + np.arange(t), past + np.arange(t)]).astype(np.float64)
    ang = inv[:, None] * pos[None, :]
    tab = jnp.asarray(np.concatenate([np.cos(ang), np.sin(ang)], axis=0).astype(np.float32))
    s_pad = past + 2 * t
    return pl.pallas_call(
        functools.partial(_attn_sample_kernel, tkb=tkb),
        grid=(batch // 2,),
        in_specs=[pl.BlockSpec(memory_space=pltpu.SMEM), pl.BlockSpec(memory_space=pltpu.SMEM),
                  pl.BlockSpec((2 * t, MLA_HEADS * HEAD_PAD), lambda g: (g, 0)),
                  pl.BlockSpec((2, past, KV_LORA), lambda g: (g, 0, 0)),
                  pl.BlockSpec((2, QK_ROPE, past), lambda g: (g, 0, 0)),
                  pl.BlockSpec((2 * t, KV_LORA), lambda g: (g, 0)),
                  pl.BlockSpec((QK_ROPE, 2 * t), lambda g: (0, g)),
                  _const(tab.shape), _const(w["w_ukt"].shape), _const(w["w_uv"].shape),
                  _const((1, LANES)), _const((QK_ROPE, 1))],
        out_specs=pl.BlockSpec((2 * t, MLA_WIDTH), lambda g: (g, 0)),
        out_shape=jax.ShapeDtypeStruct((batch * t, MLA_WIDTH), BF16),
        scratch_shapes=[pltpu.VMEM((2, MLA_HEADS * (QK_NOPE + t), KV_LORA), BF16),
                        pltpu.VMEM((2, s_pad, KV_LORA), BF16),
                        pltpu.VMEM((2, MLA_HEADS * t, s_pad), F32)],
        compiler_params=_params("arbitrary"),
        name="mla_attn_sample",
    )(w["fast_softmax"], w["bound"], q, past_lat, past_krt, lat_new, krt_new, tab, w["w_ukt"], w["w_uv"], w["gkn"],
      w["gkr_col"])


def _gla_kernel(q_ref, k_ref, v_ref, la_ref, r_ref, s0_ref, g_ref, spread_ref, o_ref, sn_ref,
                st_ref, kp_ref, ap_ref, p_ref, on_ref, *, gpt):
    t_idx = pl.program_id(1)
    L, W, P = CHUNK, GLA_QK, LANES
    R = q_ref.shape[0]
    n_chunks = R // L
    cpg = n_chunks // gpt
    n_pairs = GLA_HEADS // 2

    lane_p = lax.broadcasted_iota(jnp.int32, (1, P), 1)
    even = lane_p < GLA_DK
    bd_mask = (lax.broadcasted_iota(jnp.int32, (2 * GLA_DV, P), 0) // GLA_DV
               == lax.broadcasted_iota(jnp.int32, (2 * GLA_DV, P), 1) // GLA_DK)

    @pl.when(t_idx == 0)
    def _():
        kp_ref[0:SUB, :] = jnp.zeros((SUB, W), F32)
        ap_ref[0:SUB, :] = jnp.zeros((SUB, W), F32)
        for gi in range(gpt):
            for pr in range(n_pairs):
                tt = s0_ref[gi, 2 * pr:2 * pr + 2].reshape(2 * GLA_DK, GLA_DV).T
                st_ref[gi, pr] = jnp.where(bd_mask, jnp.concatenate([tt, tt], axis=0), 0.0)

    q = q_ref[...]
    k = k_ref[...]
    la = la_ref[...]

    tri = (lax.broadcasted_iota(jnp.int32, (L, L), 0) >= lax.broadcasted_iota(jnp.int32, (L, L), 1)).astype(BF16)
    la_hi = la.astype(BF16)
    la2 = jnp.concatenate([la_hi, (la - la_hi.astype(F32)).astype(BF16)], axis=1)
    bs = []
    for c in range(n_chunks):
        t2 = _dot(tri, la2[c * L:(c + 1) * L, :])
        bs.append(t2[:, 0:W] + t2[:, W:2 * W])
    b = (jnp.concatenate(bs, axis=0) if n_chunks > 1 else bs[0]) * LOG2E
    b3 = b.reshape(n_chunks, L, W)

    def chunk_row(r):
        return jnp.broadcast_to(b3[:, r:r + 1, :], (n_chunks, L, W)).reshape(R, W)

    b_sub = jnp.broadcast_to(b.reshape(R // SUB, SUB, W)[:, 0:1, :], (R // SUB, SUB, W)).reshape(R, W)
    sub = (lax.broadcasted_iota(jnp.int32, (R, W), 0) % L) // SUB

    qt = q * jnp.exp2(b - b_sub)
    zb = jnp.zeros((), BF16)
    zq = jnp.zeros((SUB, P), F32)
    ktm = []
    for i in range(1, L // SUB):
        kt = (k * jnp.exp2(chunk_row(i * SUB) - b)).astype(BF16)
        ktm.append([[jnp.where((sub[:, 0:P] < i) & (even if e == 0 else ~even), kt[:, pr * P:(pr + 1) * P], zb)
                     for e in range(2)] for pr in range(n_pairs)])

    a = jnp.exp(la)
    kp_ref[SUB:SUB + R, :] = k
    ap_ref[SUB:SUB + R, :] = a
    p_ref[:, 0:W] = (q * k).astype(BF16)
    e = a
    for d in range(1, SUB):
        if d > 1:
            e = e * ap_ref[SUB - d + 1:SUB - d + 1 + R, :]
        p_ref[:, d * W:(d + 1) * W] = (q * kp_ref[SUB - d:SUB - d + R, :] * e).astype(BF16)
    cband = _dot(p_ref[...], spread_ref[...])
    same_sub = (lax.broadcasted_iota(jnp.int32, (L, W), 0) // SUB
                == (lax.broadcasted_iota(jnp.int32, (L, W), 1) % L) // SUB)

    qe = (q * jnp.exp2(b)).astype(BF16)
    kd = (k * jnp.exp2(chunk_row(L - 1) - b)).astype(BF16)
    zv = jnp.zeros((L, 2 * GLA_DV), BF16)

    o_intra, d_st, dec = {}, {}, []
    for c in range(n_chunks):
        rs = slice(c * L, (c + 1) * L)
        a_band = jnp.where(same_sub, pltpu.roll(cband[rs], W - (SUB - 1), 1, stride=1, stride_axis=0), 0.0)
        dec.append(jnp.exp2(b[c * L + L - 1:c * L + L, :]))
        for pr in range(n_pairs):
            ls = slice(pr * P, (pr + 1) * P)
            lhs_c = jnp.concatenate(
                [jnp.concatenate([qt[c * L + r * SUB:c * L + (r + 1) * SUB, ls] if i == r else zq
                                  for i in range(1, L // SUB)], axis=1) for r in range(L // SUB)],
                axis=0).astype(BF16)
            rhs_c = jnp.concatenate([jnp.concatenate([m[pr][0][rs], m[pr][1][rs]], axis=0) for m in ktm], axis=1)
            a_tot = (a_band[:, ls] + _dot_nt(lhs_c, rhs_c)).astype(BF16)
            vp = v_ref[rs, 2 * pr * GLA_DV:(2 * pr + 2) * GLA_DV]
            v_bd = jnp.concatenate([jnp.concatenate([vp[:, 0:GLA_DV], zv[:, 0:GLA_DV]], axis=1),
                                    jnp.concatenate([zv[:, 0:GLA_DV], vp[:, GLA_DV:]], axis=1)], axis=0)
            o_intra[c, pr] = _dot(a_tot, v_bd)
            d_st[c, pr] = jnp.where(bd_mask, _dot_tn(vp, kd[rs, ls]), 0.0)
    st_in = {}
    for gi in range(gpt):
        for pr in range(n_pairs):
            st = st_ref[gi, pr]
            for c in range(gi * cpg, (gi + 1) * cpg):
                st_in[c, pr] = st.astype(BF16)
                st = st * dec[c][:, pr * P:(pr + 1) * P] + d_st[c, pr]
            st_ref[gi, pr] = st
    for c in range(n_chunks):
        rs = slice(c * L, (c + 1) * L)
        for pr in range(n_pairs):
            on_ref[rs, 2 * pr * GLA_DV:(2 * pr + 2) * GLA_DV] = (
                o_intra[c, pr] + _dot_nt(qe[rs, pr * P:(pr + 1) * P], st_in[c, pr]))

    for hh in range(GLA_HEADS):
        hs = slice(hh * GLA_DV, (hh + 1) * GLA_DV)
        o = on_ref[:, hs]
        on = o * lax.rsqrt(jnp.mean(o * o, axis=-1, keepdims=True) + EPS) * g_ref[:, hs]
        o_ref[:, hs] = (on * r_ref[:, hs].astype(F32)).astype(BF16)

    @pl.when(t_idx == pl.num_programs(1) - 1)
    def _():
        for gi in range(gpt):
            for pr in range(n_pairs):
                st = st_ref[gi, pr]
                tt = jnp.where(even, st[0:GLA_DV], st[GLA_DV:2 * GLA_DV])
                sn_ref[gi, 2 * pr:2 * pr + 2] = tt.T.reshape(2, GLA_DK, GLA_DV)


def _band_spread():
    m = np.zeros((SUB, GLA_HEADS, GLA_DK, GLA_QK), np.float32)
    for d in range(SUB):
        for h in range(GLA_HEADS):
            m[d, h, :, h * GLA_DK + SUB - 1 - d] = 1.0
    return jnp.asarray(m.reshape(SUB * GLA_QK, GLA_QK), BF16)


def _gla(gq, gk, gv, la, gr, s0, g_out, *, groups, rows_per_group, tc):
    gpt = max(1, tc // rows_per_group)
    nt = max(1, rows_per_group // tc)

    def rows(c):
        return pl.BlockSpec((tc, c), lambda g, t: (g * nt + t, 0))

    state = pl.BlockSpec((gpt, GLA_HEADS, GLA_DK, GLA_DV), lambda g, t: (g, 0, 0, 0))
    spread = _band_spread()
    return pl.pallas_call(
        functools.partial(_gla_kernel, gpt=gpt),
        grid=(groups // gpt, nt),
        in_specs=[rows(GLA_QK), rows(GLA_QK), rows(GLA_WIDTH), rows(GLA_QK), rows(GLA_WIDTH), state,
                  _const((1, GLA_WIDTH)), _const(spread.shape)],
        out_specs=[rows(GLA_WIDTH), state],
        out_shape=[jax.ShapeDtypeStruct((groups * rows_per_group, GLA_WIDTH), BF16),
                   jax.ShapeDtypeStruct((groups, GLA_HEADS, GLA_DK, GLA_DV), F32)],
        scratch_shapes=[pltpu.VMEM((gpt, GLA_HEADS // 2, 2 * GLA_DV, LANES), F32),
                        pltpu.VMEM((SUB + tc, GLA_QK), F32), pltpu.VMEM((SUB + tc, GLA_QK), F32),
                        pltpu.VMEM((tc, SUB * GLA_QK), BF16), pltpu.VMEM((tc, GLA_WIDTH), F32)],
        compiler_params=_params("arbitrary", "arbitrary"),
        name="gla",
    )(gq, gk, gv, la, gr, s0, g_out, spread)


def _mlp_tile(x_ref, a_ref, b_ref, g1_ref, sh2_ref, sc2_ref, g2_ref, gn_ref, wo_ref, wu_ref, wd_ref, y_ref, *, gpt):
    tm, d = x_ref.shape

    def per_group(val, ref, scale_plus_one=False):
        m = ref[...]
        if scale_plus_one:
            m = 1.0 + m
        return (val.reshape(gpt, tm // gpt, d) * m).reshape(tm, d)

    mix = jnp.concatenate([a_ref[...], b_ref[...]], axis=1)
    x1 = x_ref[...] + per_group(_dot(mix, wo_ref[...]), g1_ref)
    xn = x1 * lax.rsqrt(jnp.mean(x1 * x1, axis=-1, keepdims=True) + EPS) * gn_ref[...]
    h2 = (per_group(xn, sc2_ref, True).reshape(gpt, tm // gpt, d) + sh2_ref[...]).reshape(tm, d).astype(BF16)
    acc = jnp.zeros((tm, d), F32)
    for j in range(wu_ref.shape[1] // FF_SLICE):
        u = jnp.maximum(_dot(h2, wu_ref[:, j * FF_SLICE:(j + 1) * FF_SLICE]), 0.0)
        acc += _dot((u * u).astype(BF16), wd_ref[j * FF_SLICE:(j + 1) * FF_SLICE, :])
    y_ref[...] = x1 + per_group(acc, g2_ref)


def _mlp_kernel(*refs, tiles, gpts):
    per_phase = 7
    gn_ref, wo_ref, wu_ref, wd_ref = refs[len(tiles) * per_phase:len(tiles) * per_phase + 4]
    y_refs = refs[len(tiles) * per_phase + 4:]
    i = pl.program_id(0)
    first = 0
    for p, (n_tiles, gpt) in enumerate(zip(tiles, gpts)):
        ins = refs[p * per_phase:(p + 1) * per_phase]

        @pl.when(jnp.logical_and(i >= first, i < first + n_tiles))
        def _(ins=ins, p=p, gpt=gpt):
            _mlp_tile(*ins, gn_ref, wo_ref, wu_ref, wd_ref, y_refs[p], gpt=gpt)

        first += n_tiles


def _mlp(phases, w, *, tm):
    d = phases[0][0].shape[1]
    tiles = [ph[0].shape[0] // tm for ph in phases]
    gpts = [max(1, tm // ph[4]) for ph in phases]
    in_specs, args, first = [], [], 0
    for (x2, a_out, b_out, mod4, rpg), n_tiles, gpt in zip(phases, tiles, gpts):
        tpg = max(1, rpg // tm)

        def tile(i, first=first, n_tiles=n_tiles):
            return jnp.clip(i - first, 0, n_tiles - 1)

        def rows(c, tile=tile):
            return pl.BlockSpec((tm, c), lambda i: (tile(i), 0))

        def mod_spec(j, tile=tile, gpt=gpt, tpg=tpg):
            return pl.BlockSpec((gpt, None, 1, d), lambda i: (tile(i) // tpg, j, 0, 0))

        in_specs += [rows(d), rows(MLA_WIDTH), rows(GLA_WIDTH), mod_spec(2), mod_spec(3), mod_spec(4), mod_spec(5)]
        args += [x2, a_out, b_out, mod4, mod4, mod4, mod4]
        first += n_tiles
    out_specs, first = [], 0
    for n_tiles in tiles:
        out_specs.append(pl.BlockSpec((tm, d), lambda i, first=first, n_tiles=n_tiles:
                                      (jnp.clip(i - first, 0, n_tiles - 1), 0)))
        first += n_tiles
    return pl.pallas_call(
        functools.partial(_mlp_kernel, tiles=tuple(tiles), gpts=tuple(gpts)),
        grid=(sum(tiles),),
        in_specs=in_specs + [_const((1, d)), _const(w["w_out"].shape), _const(w["w_up"].shape),
                             _const(w["w_down"].shape)],
        out_specs=out_specs,
        out_shape=[jax.ShapeDtypeStruct(ph[0].shape, F32) for ph in phases],
        compiler_params=_params("arbitrary"),
        name="out_proj_mlp",
    )(*args, w["g_norm2"], w["w_out"], w["w_up"], w["w_down"])


def _rope_table(start, count, repeat=1):
    half = QK_ROPE // 2
    inv = ROPE_THETA ** (-np.arange(half, dtype=np.float64) / half)
    ang = (start + np.arange(count, dtype=np.float64))[:, None] * inv[None, :]
    c, s, z = np.cos(ang), np.sin(ang), np.zeros_like(ang)
    tab = np.concatenate([c, c, z, z, -s, z, z, z, z, s, z, z], axis=1).astype(np.float32)
    return jnp.asarray(np.tile(tab, (repeat, 1)))


def _pad_gain(g_rope):
    return jnp.concatenate([g_rope, jnp.zeros((LANES - QK_ROPE,), F32)]).reshape(1, LANES)


def _relayout_kernel(wint_ref, wuq_ref, wukv_ref, wg_ref, win_o, wuq_o, wukv_o, wukt_o, wuv_o, wg_o):
    s = np.cumsum([0, Q_LORA, KV_LORA, QK_ROPE, GLA_QK, GLA_QK, GLA_WIDTH, GLA_GATE_RANK, GLA_WIDTH])

    def piece(i):
        return wint_ref[int(s[i]):int(s[i + 1]), :]

    d = wint_ref.shape[1]
    zeros = jnp.zeros((LANES - QK_ROPE - GLA_GATE_RANK, d), F32)
    win_o[:, 0:Q_LORA] = piece(0).T.astype(BF16)
    win_o[:, Q_LORA:C_QKR[1]] = jnp.concatenate([piece(2), piece(6), zeros], axis=0).T.astype(BF16)
    win_o[:, C_KV[0]:C_KV[1]] = piece(1).T.astype(BF16)
    win_o[:, C_GQ[0]:C_GQ[1]] = (piece(3).T * (GLA_DK ** -0.5)).astype(BF16)
    win_o[:, C_GK[0]:C_GK[1]] = piece(4).T.astype(BF16)
    win_o[:, C_GV[0]:C_GV[1]] = piece(5).T.astype(BF16)
    win_o[:, C_GR[0]:C_GR[1]] = piece(7).T.astype(BF16)

    zq = jnp.zeros((Q_LORA, HEAD_PAD - QK_DIM), BF16)
    kvw = QK_NOPE + V_DIM
    for hh in range(MLA_HEADS):
        wuq_o[:, hh * HEAD_PAD:hh * HEAD_PAD + QK_DIM] = wuq_ref[:, hh * QK_DIM:(hh + 1) * QK_DIM].astype(BF16)
        wuq_o[:, hh * HEAD_PAD + QK_DIM:(hh + 1) * HEAD_PAD] = zq
        uk = wukv_ref[:, hh * kvw:hh * kvw + QK_NOPE]
        uv = wukv_ref[:, hh * kvw + QK_NOPE:(hh + 1) * kvw].astype(BF16)
        wukv_o[:, hh * QK_NOPE:(hh + 1) * QK_NOPE] = uk.astype(BF16)
        wukv_o[:, (MLA_HEADS + hh) * V_DIM:(MLA_HEADS + hh + 1) * V_DIM] = uv
        wukt_o[hh * QK_NOPE:(hh + 1) * QK_NOPE, :] = uk.T.astype(BF16)
        wuv_o[:, hh * V_DIM:(hh + 1) * V_DIM] = uv

    wg_o[...] = jnp.zeros(wg_o.shape, BF16)
    wg_o[QK_ROPE:QK_ROPE + GLA_GATE_RANK, :] = wg_ref[...].astype(BF16)


def _relayout(w_in_t, w_uq, w_ukv, w_gate_up):
    d = w_in_t.shape[1]
    shapes = [(d, C_GR[1]), (Q_LORA, MLA_HEADS * HEAD_PAD), (KV_LORA, MLA_HEADS * (QK_NOPE + V_DIM)),
              (MLA_HEADS * QK_NOPE, KV_LORA), (KV_LORA, MLA_WIDTH), (LANES, GLA_QK)]
    return pl.pallas_call(
        _relayout_kernel,
        out_shape=[jax.ShapeDtypeStruct(sh, BF16) for sh in shapes],
        compiler_params=pltpu.CompilerParams(vmem_limit_bytes=VMEM_LIMIT),
        name="weight_relayout",
    )(w_in_t, w_uq, w_ukv, w_gate_up)


def _prep_weights(w_in, g_norm1, g_q_lora, w_uq, g_kv_lora, w_ukv, g_q_head, g_k_head,
                  w_gate_up, b_gate_up, g_gla_out, w_out, g_norm2, w_up, w_down):
    d = w_in.shape[0]
    w_in_p, w_uq_p, w_ukv_p, w_ukt, w_uv, w_gate = _relayout(w_in.T, w_uq, w_ukv, w_gate_up)
    qscale = QK_DIM ** -0.5 * LOG2E
    bound = 1.02 * QK_DIM ** 0.5 * LOG2E * jnp.max(jnp.abs(g_q_head)) * jnp.max(jnp.abs(g_k_head))
    lane = jnp.arange(LANES) == QK_ROPE
    return {
        "w_in": w_in_p, "g_norm1": g_norm1.reshape(1, d), "g_q_lora": g_q_lora.reshape(1, Q_LORA),
        "w_uq": w_uq_p, "g_kv_lora": g_kv_lora.reshape(1, KV_LORA), "w_ukv": w_ukv_p, "w_ukt": w_ukt, "w_uv": w_uv,
        "gkr_col": g_k_head[QK_NOPE:].reshape(QK_ROPE, 1),
        "qone": lane.astype(F32).reshape(1, LANES), "kbias": jnp.where(lane, -bound, 0.0).reshape(1, LANES),
        "fast_softmax": (bound <= MAX_FIXED_SHIFT).astype(jnp.int32).reshape(1), "bound": bound.reshape(1),
        "gqn": (g_q_head[:QK_NOPE] * qscale).reshape(1, LANES), "gqr": _pad_gain(g_q_head[QK_NOPE:] * qscale),
        "gkn": g_k_head[:QK_NOPE].reshape(1, LANES), "gkr": _pad_gain(g_k_head[QK_NOPE:]),
        "w_gate": w_gate, "b_gate": b_gate_up.reshape(1, GLA_QK),
        "g_gla_out": g_gla_out.reshape(1, GLA_WIDTH), "w_out": w_out.astype(BF16),
        "g_norm2": g_norm2.reshape(1, d), "w_up": w_up.astype(BF16), "w_down": w_down.astype(BF16),
    }


def _mixers(x, mod, past_lat, past_kr, s0, w, *, tm):
    batch, seq, d = x.shape
    n = batch * seq
    past = 0 if past_lat is None else past_lat.shape[1]
    x2 = x.reshape(n, d)
    mod4 = mod.reshape(batch, 6, 1, d)
    tm = min(tm, n)
    tab = _rope_table(past, seq, repeat=max(1, tm // seq))
    if past == 0:
        lat, krt, q, k, v, gq, gk, gv, la, gr = _projection(x2, mod4, tab, w, rows_per_group=seq, tm=tm, prompt=True)
        kr = jnp.swapaxes(krt, 1, 2)
        a_out = _attention_prompt(w["fast_softmax"], q, k, v, batch=batch, seq=seq, tq=min(ATTN_TILE, seq))
    else:
        assert seq == CHUNK and past % CHUNK == 0
        lat, kr, krt, q, gq, gk, gv, la, gr = _projection(x2, mod4, tab, w, rows_per_group=seq, tm=tm, prompt=False)
        a_out = _attention_sample(q, past_lat, jnp.swapaxes(past_kr, 1, 2), lat, krt, w, tkb=min(CACHE_BLOCK, past))
    b_out, s_new = _gla(gq, gk, gv, la, gr, s0, w["g_gla_out"], groups=batch, rows_per_group=seq,
                        tc=tm)
    return (x2, a_out, b_out, mod4, seq), (lat.reshape(batch, seq, KV_LORA), kr.reshape(batch, seq, QK_ROPE), s_new)


def kernel(x_prompt, x_sample, cache_mla_latent, cache_mla_krope, state_gla, c_prompt, c_sample,
           w_ada, b_ada, g_norm1, w_in, g_q_lora, w_uq, g_kv_lora, w_ukv, g_q_head, g_k_head,
           w_gate_up, b_gate_up, g_gla_out, w_out, g_norm2, w_up, w_down):
    nb = x_prompt.shape[0]
    depth = w_ada.shape[0]
    y_p, y_s = x_prompt, x_sample
    outs = [[] for _ in range(6)]
    c_all = jnp.concatenate([c_prompt, c_sample], axis=0)
    for l in range(depth):
        w = _prep_weights(w_in[l], g_norm1[l], g_q_lora[l], w_uq[l], g_kv_lora[l], w_ukv[l], g_q_head[l],
                          g_k_head[l], w_gate_up[l], b_gate_up[l], g_gla_out[l], w_out[l], g_norm2[l],
                          w_up[l], w_down[l])
        mod = _modulation(c_all, w_ada[l], b_ada[l])
        zero_state = jnp.zeros((nb, GLA_HEADS, GLA_DK, GLA_DV), x_prompt.dtype)
        mix_p, new_p = _mixers(y_p, mod[:nb], None, None, zero_state, w, tm=ROW_TILE)
        mix_s, new_s = _mixers(y_s, mod[nb:], cache_mla_latent[l], cache_mla_krope[l], state_gla[l], w, tm=ROW_TILE)
        y2_p, y2_s = _mlp([mix_p, mix_s], w, tm=ROW_TILE)
        y_p, y_s = y2_p.reshape(y_p.shape), y2_s.reshape(y_s.shape)
        for o, new in zip(outs, new_p + new_s):
            o.append(new)
    return (y_p, y_s) + tuple(jnp.stack(o) for o in outs)
```

```python
import functools

import jax
import jax.numpy as jnp
import numpy as np
from jax import lax
from jax.experimental import pallas as pl
from jax.experimental.pallas import tpu as pltpu

F32 = jnp.float32
BF16 = jnp.bfloat16

CHUNK = 64
EPS = 1e-6
MLA_HEADS = 4
Q_LORA = 384
KV_LORA = 256
QK_NOPE = 128
QK_ROPE = 64
QK_DIM = QK_NOPE + QK_ROPE
V_DIM = 128
ROPE_THETA = 10000.0
GLA_HEADS = 4
GLA_DK = 64
GLA_DV = 128
GLA_GATE_RANK = 16
GLA_TAU = 16.0
GLA_QK = GLA_HEADS * GLA_DK
GLA_WIDTH = GLA_HEADS * GLA_DV
MLA_WIDTH = MLA_HEADS * V_DIM
HEAD_PAD = 256
SUB = 8
LOG2E = 1.4426950408889634
MAX_FIXED_SHIFT = 48.0
MAX_SUB_DECAY = 60.0

LANES = 128
VMEM_LIMIT = 56 * 1024 * 1024
ROW_TILE = 512
ATTN_TILE = 512
CACHE_BLOCK = 512
FF_SLICE = 1024
MOD_TILE = 1024

C_QKR = (0, 512)
C_KV = (512, 768)
C_GQ = (768, 1024)
C_GK = (1024, 1280)
C_GV = (1280, 1792)
C_GR = (1792, 2304)


def _dot(a, b):
    return jnp.dot(a, b, preferred_element_type=F32)


def _dot_nt(a, b):
    return lax.dot_general(a, b, (((1,), (1,)), ((), ())), preferred_element_type=F32)


def _dot_tn(a, b):
    return lax.dot_general(a, b, (((0,), (0,)), ((), ())), preferred_element_type=F32)


def _rope_tile(t, c, sa, sb):
    return t * c + pltpu.roll(t, 96, 1) * sa + pltpu.roll(t, 32, 1) * sb


def _params(*sem):
    return pltpu.CompilerParams(dimension_semantics=sem, vmem_limit_bytes=VMEM_LIMIT)


def _const(shape):
    return pl.BlockSpec(shape, lambda *_: (0,) * len(shape), pipeline_mode=pl.Buffered(1))


def _mod_kernel(cs_ref, cp_ref, w_ref, b_ref, os_ref, op_ref):
    ns = cs_ref.shape[0]
    c = jnp.concatenate([cs_ref[...], cp_ref[...]], axis=0)
    s = (c * jax.nn.sigmoid(c)).astype(BF16)
    mod = _dot(s, w_ref[...].astype(BF16)) + b_ref[...]
    os_ref[...] = mod[0:ns]
    op_ref[...] = mod[ns:]


def _modulation(c_prompt, c_sample, w_ada, b_ada):
    d, n = w_ada.shape
    tn = MOD_TILE

    def whole(c):
        return pl.BlockSpec(c.shape, lambda j: (0, 0))

    def cols(c):
        return pl.BlockSpec((c.shape[0], tn), lambda j: (0, j))

    mod_s, mod_p = pl.pallas_call(
        _mod_kernel,
        grid=(n // tn,),
        in_specs=[whole(c_sample), whole(c_prompt),
                  pl.BlockSpec((d, tn), lambda j: (0, j)),
                  pl.BlockSpec((1, tn), lambda j: (0, j))],
        out_specs=[cols(c_sample), cols(c_prompt)],
        out_shape=[jax.ShapeDtypeStruct((c_sample.shape[0], n), F32), jax.ShapeDtypeStruct((c_prompt.shape[0], n), F32)],
        compiler_params=_params("arbitrary"),
        name="adaln_mod",
    )(c_sample, c_prompt, w_ada, b_ada.reshape(1, n))
    return mod_p, mod_s


def _proj_kernel(x_ref, sh_ref, sc_ref, g1_ref, win_ref, gql_ref, wuq_ref, gkv_ref, wukv_ref,
                 gqn_ref, gqr_ref, gkn_ref, gkr_ref, qone_ref, kbias_ref, tab_ref, wg_ref, bg_ref,
                 *outs, gpt, prompt):
    if prompt:
        lat_ref, krt_ref, q_ref, k_ref, v_ref, gq_ref, gk_ref, gv_ref, la_ref, gr_ref = outs
    else:
        lat_ref, kr_ref, krt_ref, q_ref, gq_ref, gk_ref, gv_ref, la_ref, gr_ref = outs
    tm, d = x_ref.shape
    x = x_ref[...]
    xn = x * lax.rsqrt(jnp.mean(x * x, axis=-1, keepdims=True) + EPS)
    h = (xn.reshape(gpt, tm // gpt, d) * (g1_ref[...] * (1.0 + sc_ref[...])) + sh_ref[...]).reshape(tm, d)
    hb = h.astype(BF16)

    def col(c):
        return _dot(hb, win_ref[:, c[0]:c[1]])

    tab = tab_ref[...]
    cos, sa, sb = tab[:, 0:LANES], tab[:, LANES:2 * LANES], tab[:, 2 * LANES:3 * LANES]

    qkr = col(C_QKR)
    krg = qkr[:, Q_LORA:]
    z = _dot(krg.astype(BF16), wg_ref[...]) + bg_ref[...]
    la_ref[...] = (jnp.minimum(z, 0.0) - jnp.log(1.0 + jnp.exp(-jnp.abs(z)))) * (1.0 / GLA_TAU)
    r = col(C_GR)
    gr_ref[...] = (r * jax.nn.sigmoid(r)).astype(BF16)

    cq = qkr[:, 0:Q_LORA]
    cqn = cq * lax.rsqrt(jnp.mean(cq * cq, axis=-1, keepdims=True) + EPS) * gql_ref[...]
    qp = _dot(cqn.astype(BF16), wuq_ref[...])
    for hh in range(MLA_HEADS):
        nope = qp[:, hh * HEAD_PAD:hh * HEAD_PAD + QK_NOPE]
        rt = qp[:, hh * HEAD_PAD + QK_NOPE:(hh + 1) * HEAD_PAD]
        ss = jnp.sum(nope * nope, axis=-1, keepdims=True) + jnp.sum(rt * rt, axis=-1, keepdims=True)
        inv = lax.rsqrt(ss * (1.0 / QK_DIM) + EPS)
        q_ref[:, hh * HEAD_PAD:hh * HEAD_PAD + QK_NOPE] = (nope * inv * gqn_ref[...]).astype(BF16)
        rq = _rope_tile(rt * inv * gqr_ref[...], cos, sa, sb)
        q_ref[:, hh * HEAD_PAD + QK_NOPE:(hh + 1) * HEAD_PAD] = (rq + qone_ref[...]).astype(BF16)

    ckv = col(C_KV)
    lat = ckv * lax.rsqrt(jnp.mean(ckv * ckv, axis=-1, keepdims=True) + EPS) * gkv_ref[...]
    lat_ref[...] = lat
    krt_ref[...] = krg.T[0:QK_ROPE, :]
    if prompt:
        lane = lax.broadcasted_iota(jnp.int32, (1, LANES), 1)
        krm = jnp.where(lane < QK_ROPE, krg, 0.0)
        ssr = jnp.sum(krm * krm, axis=-1, keepdims=True)
        rk = _rope_tile(krm * gkr_ref[...], cos, sa, sb)
        kv = _dot(lat.astype(BF16), wukv_ref[...])
        for hh in range(MLA_HEADS):
            kn = kv[:, hh * QK_NOPE:(hh + 1) * QK_NOPE]
            inv = lax.rsqrt((jnp.sum(kn * kn, axis=-1, keepdims=True) + ssr) * (1.0 / QK_DIM) + EPS)
            k_ref[:, hh * HEAD_PAD:hh * HEAD_PAD + QK_NOPE] = (kn * inv * gkn_ref[...]).astype(BF16)
            k_ref[:, hh * HEAD_PAD + QK_NOPE:(hh + 1) * HEAD_PAD] = (rk * inv + kbias_ref[...]).astype(BF16)
        v_ref[...] = kv[:, MLA_HEADS * QK_NOPE:].astype(BF16)
    else:
        kr_ref[...] = krg[:, 0:QK_ROPE]

    gq_ref[...] = col(C_GQ)
    gk_ref[...] = col(C_GK)
    gv_ref[...] = col(C_GV).astype(BF16)


def _projection(x2, mod4, tab, w, *, rows_per_group, tm, prompt):
    n, d = x2.shape
    gpt = max(1, tm // rows_per_group)
    tpg = max(1, rows_per_group // tm)
    ntab = tab.shape[0] // tm

    def mod_spec(j):
        return pl.BlockSpec((gpt, None, 1, d), lambda i: ((i // tpg) if gpt == 1 else i, j, 0, 0))

    def rows(c):
        return pl.BlockSpec((tm, c), lambda i: (i, 0))

    def out(c, t):
        return rows(c), jax.ShapeDtypeStruct((n, c), t)

    gla_outs = [out(GLA_QK, F32), out(GLA_QK, F32), out(GLA_WIDTH, BF16), out(GLA_QK, F32), out(GLA_WIDTH, BF16)]
    qo = out(MLA_HEADS * HEAD_PAD, BF16)
    if prompt:
        krt = (pl.BlockSpec((None, QK_ROPE, tm), lambda i: (i // tpg, 0, i % tpg)),
               jax.ShapeDtypeStruct((n // rows_per_group, QK_ROPE, rows_per_group), F32))
        outs = [out(KV_LORA, F32), krt, qo, out(MLA_HEADS * HEAD_PAD, BF16), out(MLA_WIDTH, BF16)] + gla_outs
    else:
        krt = (pl.BlockSpec((QK_ROPE, tm), lambda i: (0, i)), jax.ShapeDtypeStruct((QK_ROPE, n), F32))
        outs = [out(KV_LORA, F32), out(QK_ROPE, F32), krt, qo] + gla_outs
    return pl.pallas_call(
        functools.partial(_proj_kernel, gpt=gpt, prompt=prompt),
        grid=(n // tm,),
        in_specs=[rows(d), mod_spec(0), mod_spec(1), _const((1, d)), _const(w["w_in"].shape),
                  _const((1, Q_LORA)), _const(w["w_uq"].shape), _const((1, KV_LORA)), _const(w["w_ukv"].shape),
                  _const((1, LANES)), _const((1, LANES)), _const((1, LANES)), _const((1, LANES)),
                  _const((1, LANES)), _const((1, LANES)),
                  pl.BlockSpec((tm, 3 * LANES), lambda i: (i % ntab, 0)),
                  _const(w["w_gate"].shape), _const((1, GLA_QK))],
        out_specs=[o[0] for o in outs],
        out_shape=[o[1] for o in outs],
        compiler_params=_params("arbitrary"),
        name="in_proj",
    )(x2, mod4, mod4, w["g_norm1"], w["w_in"], w["g_q_lora"], w["w_uq"], w["g_kv_lora"], w["w_ukv"],
      w["gqn"], w["gqr"], w["gkn"], w["gkr"], w["qone"], w["kbias"], tab, w["w_gate"], w["b_gate"])


def _attn_prompt_kernel(fast_ref, q_ref, k_ref, v_ref, o_ref, vx_ref, m_ref, acc_ref, *, tq):
    i = pl.program_id(1)
    heads = range(MLA_HEADS)

    @pl.when(i == 0)
    def _():
        for hh in heads:
            vx_ref[hh, :, 0:V_DIM] = v_ref[:, hh * V_DIM:(hh + 1) * V_DIM]
            vx_ref[hh, :, V_DIM:] = jnp.ones((v_ref.shape[0], V_DIM), BF16)

    def scores(j, hh, masked):
        hs = slice(hh * HEAD_PAD, (hh + 1) * HEAD_PAD)
        s = _dot_nt(q_ref[:, hs], k_ref[pl.ds(pl.multiple_of(j * tq, tq), tq), hs])
        if masked:
            qc = lax.broadcasted_iota(jnp.int32, (tq, tq), 0) // CHUNK
            kc = lax.broadcasted_iota(jnp.int32, (tq, tq), 1) // CHUNK
            s = jnp.where(kc <= qc, s, -jnp.inf)
        return s

    def fast_diagonal():
        hq = tq // 2
        k0 = pl.multiple_of(i * tq, tq)
        qc = lax.broadcasted_iota(jnp.int32, (hq, hq), 0) // CHUNK
        kc = lax.broadcasted_iota(jnp.int32, (hq, hq), 1) // CHUNK
        tri = kc <= qc
        for hh in heads:
            hs = slice(hh * HEAD_PAD, (hh + 1) * HEAD_PAD)
            top = jnp.where(tri, _dot_nt(q_ref[0:hq, hs], k_ref[pl.ds(k0, hq), hs]), -jnp.inf)
            acc_ref[hh, 0:hq] = _dot(jnp.exp2(top).astype(BF16), vx_ref[hh, pl.ds(k0, hq), :])
            bot = _dot_nt(q_ref[hq:tq, hs], k_ref[pl.ds(k0, tq), hs])
            bot = jnp.concatenate([bot[:, 0:hq], jnp.where(tri, bot[:, hq:tq], -jnp.inf)], axis=1)
            acc_ref[hh, hq:tq] = _dot(jnp.exp2(bot).astype(BF16), vx_ref[hh, pl.ds(k0, tq), :])

    def vblock(j, hh):
        return vx_ref[hh, pl.ds(pl.multiple_of(j * tq, tq), tq), :]

    def fast_block(j, masked):
        for hh in heads:
            acc_ref[hh] += _dot(jnp.exp2(scores(j, hh, masked)).astype(BF16), vblock(j, hh))

    def safe_block(j, masked):
        for hh in heads:
            s = scores(j, hh, masked)
            m = m_ref[hh]
            m_new = jnp.maximum(m, jnp.max(s, axis=1, keepdims=True))
            p = jnp.exp2(s - m_new).astype(BF16)
            acc_ref[hh] = jnp.exp2(m - m_new) * acc_ref[hh] + _dot(p, vblock(j, hh))
            m_ref[hh] = m_new

    def sweep(block):
        def body(jj, c):
            block(2 * jj, False)
            block(2 * jj + 1, False)
            return c

        lax.fori_loop(0, i // 2, body, 0)

        @pl.when(i % 2 == 1)
        def _():
            block(i - 1, False)

    @pl.when(fast_ref[0] == 1)
    def _():
        fast_diagonal()
        sweep(fast_block)

    @pl.when(fast_ref[0] != 1)
    def _():
        m_ref[...] = jnp.full(m_ref.shape, -jnp.inf, F32)
        acc_ref[...] = jnp.zeros(acc_ref.shape, F32)
        safe_block(i, True)
        sweep(safe_block)

    for hh in heads:
        acc = acc_ref[hh]
        o_ref[:, hh * V_DIM:(hh + 1) * V_DIM] = (acc[:, 0:V_DIM] / acc[:, V_DIM:]).astype(BF16)


def _attention_prompt(fast, q, k, v, *, batch, seq, tq):
    nq = seq // tq
    return pl.pallas_call(
        functools.partial(_attn_prompt_kernel, tq=tq),
        grid=(batch, nq),
        in_specs=[pl.BlockSpec(memory_space=pltpu.SMEM),
                  pl.BlockSpec((tq, MLA_HEADS * HEAD_PAD), lambda b, i: (b * nq + i, 0)),
                  pl.BlockSpec((seq, MLA_HEADS * HEAD_PAD), lambda b, i: (b, 0)),
                  pl.BlockSpec((seq, MLA_WIDTH), lambda b, i: (b, 0))],
        out_specs=pl.BlockSpec((tq, MLA_WIDTH), lambda b, i: (b * nq + i, 0)),
        out_shape=jax.ShapeDtypeStruct((batch * seq, MLA_WIDTH), BF16),
        scratch_shapes=[pltpu.VMEM((MLA_HEADS, seq, 2 * V_DIM), BF16), pltpu.VMEM((MLA_HEADS, tq, 1), F32),
                        pltpu.VMEM((MLA_HEADS, tq, 2 * V_DIM), F32)],
        compiler_params=_params("arbitrary", "arbitrary"),
        name="mla_attn_prompt",
    )(fast, q, k, v)


def _attn_sample_kernel(fast_ref, bound_ref, q_ref, lat_ref, krt_ref, latn_ref, krtn_ref, tab_ref, wukt_ref, wuv_ref,
                        gkn_ref, gkr_ref, o_ref, wq_ref, latb_ref, s_ref, *, tkb):
    t = q_ref.shape[0] // 2
    past = lat_ref.shape[1]
    nk = MLA_HEADS * QK_NOPE
    half = QK_ROPE // 2
    streams = range(2)

    @pl.when(pl.program_id(0) == 0)
    def _():
        for e in streams:
            wq_ref[e, 0:nk, :] = wukt_ref[...]

    qr = []
    for e in streams:
        rows = slice(e * t, (e + 1) * t)
        parts = []
        for hh in range(MLA_HEADS):
            qn = (q_ref[rows, hh * HEAD_PAD:hh * HEAD_PAD + QK_NOPE].astype(F32) * gkn_ref[...]).astype(BF16)
            wq_ref[e, nk + hh * t:nk + (hh + 1) * t, :] = _dot(
                qn, wukt_ref[hh * QK_NOPE:(hh + 1) * QK_NOPE, :]).astype(BF16)
            parts.append(q_ref[rows, hh * HEAD_PAD + QK_NOPE:(hh + 1) * HEAD_PAD])
        qr.append(jnp.concatenate(parts, axis=0))

    def key_block(e, lat, krt, tab, width, valid):
        latb = lat.astype(BF16)
        g = _dot_nt(wq_ref[e], latb)
        ssr = jnp.sum(krt * krt, axis=0, keepdims=True)
        kg = krt * gkr_ref[...]
        x1, x2 = kg[0:half], kg[half:QK_ROPE]
        c, sn = tab[0:half], tab[half:QK_ROPE]
        rope = jnp.concatenate([x1 * c - x2 * sn, x2 * c + x1 * sn, jnp.zeros((LANES - QK_ROPE, width), F32)], axis=0)
        srope = _dot(qr[e], rope.astype(BF16))
        rows = []
        for hh in range(MLA_HEADS):
            kn = g[hh * QK_NOPE:(hh + 1) * QK_NOPE]
            inv = lax.rsqrt((jnp.sum(kn * kn, axis=0, keepdims=True) + ssr) * (1.0 / QK_DIM) + EPS)
            rows.append((g[nk + hh * t:nk + (hh + 1) * t] + srope[hh * t:(hh + 1) * t]) * inv)
        sc = jnp.concatenate(rows, axis=0) - bound_ref[0]
        if valid is not None:
            sc = jnp.where(valid, sc, -jnp.inf)
        return sc, latb

    def blocks():
        for blk in range(past // tkb):
            c0 = blk * tkb
            for e in streams:
                yield e, c0, tkb, key_block(e, lat_ref[e, c0:c0 + tkb, :], krt_ref[e, :, c0:c0 + tkb],
                                            tab_ref[:, c0:c0 + tkb], tkb, None)
        for e in streams:
            mine = lax.broadcasted_iota(jnp.int32, (1, 2 * t), 1) // t == e
            yield e, past, 2 * t, key_block(e, latn_ref[...], krtn_ref[...], tab_ref[:, past:past + 2 * t], 2 * t, mine)

    def finish(e, wlat, l):
        wlat = wlat.astype(BF16)
        for hh in range(MLA_HEADS):
            o = _dot(wlat[hh * t:(hh + 1) * t], wuv_ref[:, hh * V_DIM:(hh + 1) * V_DIM])
            o_ref[e * t:(e + 1) * t, hh * V_DIM:(hh + 1) * V_DIM] = (o / l[hh * t:(hh + 1) * t]).astype(BF16)

    @pl.when(fast_ref[0] == 1)
    def _():
        wlat = [jnp.zeros((MLA_HEADS * t, KV_LORA), F32) for _ in streams]
        lsum = [jnp.zeros((MLA_HEADS * t, LANES), F32) for _ in streams]
        for e, _, width, (sc, latb) in blocks():
            p = jnp.exp2(sc)
            wlat[e] += _dot(p.astype(BF16), latb)
            for c in range(width // LANES):
                lsum[e] += p[:, c * LANES:(c + 1) * LANES]
        for e in streams:
            finish(e, wlat[e], jnp.sum(lsum[e], axis=1, keepdims=True))

    @pl.when(fast_ref[0] != 1)
    def _():
        for e, c0, width, (sc, latb) in blocks():
            s_ref[e, :, c0:c0 + width] = sc
            latb_ref[e, c0:c0 + width, :] = latb
        for e in streams:
            s = s_ref[e]
            p = jnp.exp2(s - jnp.max(s, axis=1, keepdims=True))
            finish(e, _dot(p.astype(BF16), latb_ref[e]), jnp.sum(p, axis=1, keepdims=True))


def _attention_sample(q, past_lat, past_krt, lat_new, krt_new, w, *, tkb):
    batch, past, _ = past_lat.shape
    t = q.shape[0] // batch
    half = QK_ROPE // 2
    inv = ROPE_THETA ** (-np.arange(half, dtype=np.float64) / half)
    pos = np.concatenate([np.arange(past), past + np.arange(t), past + np.arange(t)]).astype(np.float64)
    ang = inv[:, None] * pos[None, :]
    tab = jnp.asarray(np.concatenate([np.cos(ang), np.sin(ang)], axis=0).astype(np.float32))
    s_pad = past + 2 * t
    return pl.pallas_call(
        functools.partial(_attn_sample_kernel, tkb=tkb),
        grid=(batch // 2,),
        in_specs=[pl.BlockSpec(memory_space=pltpu.SMEM), pl.BlockSpec(memory_space=pltpu.SMEM),
                  pl.BlockSpec((2 * t, MLA_HEADS * HEAD_PAD), lambda g: (g, 0)),
                  pl.BlockSpec((2, past, KV_LORA), lambda g: (g, 0, 0)),
                  pl.BlockSpec((2, QK_ROPE, past), lambda g: (g, 0, 0)),
                  pl.BlockSpec((2 * t, KV_LORA), lambda g: (g, 0)),
                  pl.BlockSpec((QK_ROPE, 2 * t), lambda g: (0, g)),
                  _const(tab.shape), _const(w["w_ukt"].shape), _const(w["w_uv"].shape),
                  _const((1, LANES)), _const((QK_ROPE, 1))],
        out_specs=pl.BlockSpec((2 * t, MLA_WIDTH), lambda g: (g, 0)),
        out_shape=jax.ShapeDtypeStruct((batch * t, MLA_WIDTH), BF16),
        scratch_shapes=[pltpu.VMEM((2, MLA_HEADS * (QK_NOPE + t), KV_LORA), BF16),
                        pltpu.VMEM((2, s_pad, KV_LORA), BF16),
                        pltpu.VMEM((2, MLA_HEADS * t, s_pad), F32)],
        compiler_params=_params("arbitrary"),
        name="mla_attn_sample",
    )(w["fast_softmax"], w["bound"], q, past_lat, past_krt, lat_new, krt_new, tab, w["w_ukt"], w["w_uv"], w["gkn"],
      w["gkr_col"])


def _gla_kernel(q_ref, k_ref, v_ref, la_ref, r_ref, s0_ref, g_ref, spread_ref, o_ref, sn_ref,
                st_ref, kp_ref, ap_ref, p_ref, on_ref, at_ref, *, gpt):
    t_idx = pl.program_id(1)
    L, W, P = CHUNK, GLA_QK, LANES
    R = q_ref.shape[0]
    n_chunks = R // L
    cpg = n_chunks // gpt
    n_pairs = GLA_HEADS // 2

    lane_p = lax.broadcasted_iota(jnp.int32, (1, P), 1)
    even = lane_p < GLA_DK
    bd_mask = (lax.broadcasted_iota(jnp.int32, (2 * GLA_DV, P), 0) // GLA_DV
               == lax.broadcasted_iota(jnp.int32, (2 * GLA_DV, P), 1) // GLA_DK)

    @pl.when(t_idx == 0)
    def _():
        kp_ref[0:SUB, :] = jnp.zeros((SUB, W), F32)
        ap_ref[0:SUB, :] = jnp.zeros((SUB, W), F32)
        for gi in range(gpt):
            for pr in range(n_pairs):
                tt = s0_ref[gi, 2 * pr:2 * pr + 2].reshape(2 * GLA_DK, GLA_DV).T
                st_ref[gi, pr] = jnp.where(bd_mask, jnp.concatenate([tt, tt], axis=0), 0.0)

    q = q_ref[...]
    k = k_ref[...]
    la = la_ref[...]

    tri = (lax.broadcasted_iota(jnp.int32, (L, L), 0) >= lax.broadcasted_iota(jnp.int32, (L, L), 1)).astype(BF16)
    la_hi = la.astype(BF16)
    la2 = jnp.concatenate([la_hi, (la - la_hi.astype(F32)).astype(BF16)], axis=1)
    bs = []
    for c in range(n_chunks):
        t2 = _dot(tri, la2[c * L:(c + 1) * L, :])
        bs.append(t2[:, 0:W] + t2[:, W:2 * W])
    b = (jnp.concatenate(bs, axis=0) if n_chunks > 1 else bs[0]) * LOG2E
    b3 = b.reshape(n_chunks, L, W)

    def chunk_row(r):
        return jnp.broadcast_to(b3[:, r:r + 1, :], (n_chunks, L, W)).reshape(R, W)

    b_sub = jnp.broadcast_to(b.reshape(R // SUB, SUB, W)[:, 0:1, :], (R // SUB, SUB, W)).reshape(R, W)
    sub = (lax.broadcasted_iota(jnp.int32, (R, W), 0) % L) // SUB

    qt = q * jnp.exp2(b - b_sub)
    gap = jnp.max(b_sub - b)
    zb = jnp.zeros((), BF16)
    zq = jnp.zeros((SUB, P), F32)
    n_sub = L // SUB

    def rescaled_keys(i, own):
        kt = (k * jnp.exp2(chunk_row(i * SUB) - b)).astype(BF16)
        keep = (sub[:, 0:P] <= i) if own else (sub[:, 0:P] < i)
        return [[jnp.where(keep & (even if e == 0 else ~even), kt[:, pr * P:(pr + 1) * P], zb)
                 for e in range(2)] for pr in range(n_pairs)]

    def stacked_scores(c, pr, ktm, first):
        rs, ls = slice(c * L, (c + 1) * L), slice(pr * P, (pr + 1) * P)
        lhs_c = jnp.concatenate(
            [jnp.concatenate([qt[c * L + r * SUB:c * L + (r + 1) * SUB, ls] if i == r else zq
                              for i in range(first, n_sub)], axis=1) for r in range(n_sub)], axis=0).astype(BF16)
        rhs_c = jnp.concatenate([jnp.concatenate([m[pr][0][rs], m[pr][1][rs]], axis=0) for m in ktm], axis=1)
        return _dot_nt(lhs_c, rhs_c)

    @pl.when(gap <= MAX_SUB_DECAY)
    def _():
        ktm = [rescaled_keys(i, True) for i in range(n_sub)]
        causal = (lax.broadcasted_iota(jnp.int32, (L, P), 0) >= lax.broadcasted_iota(jnp.int32, (L, P), 1) % L)
        for c in range(n_chunks):
            for pr in range(n_pairs):
                at_ref[c * L:(c + 1) * L, pr * P:(pr + 1) * P] = jnp.where(
                    causal, stacked_scores(c, pr, ktm, 0), 0.0).astype(BF16)

    @pl.when(jnp.logical_not(gap <= MAX_SUB_DECAY))
    def _():
        ktm = [rescaled_keys(i, False) for i in range(1, n_sub)]
        a = jnp.exp(la)
        kp_ref[SUB:SUB + R, :] = k
        ap_ref[SUB:SUB + R, :] = a
        p_ref[:, 0:W] = (q * k).astype(BF16)
        e = a
        for d in range(1, SUB):
            if d > 1:
                e = e * ap_ref[SUB - d + 1:SUB - d + 1 + R, :]
            p_ref[:, d * W:(d + 1) * W] = (q * kp_ref[SUB - d:SUB - d + R, :] * e).astype(BF16)
        cband = _dot(p_ref[...], spread_ref[...])
        same_sub = (lax.broadcasted_iota(jnp.int32, (L, W), 0) // SUB
                    == (lax.broadcasted_iota(jnp.int32, (L, W), 1) % L) // SUB)
        for c in range(n_chunks):
            a_band = jnp.where(same_sub, pltpu.roll(cband[c * L:(c + 1) * L], W - (SUB - 1), 1, stride=1,
                                                    stride_axis=0), 0.0)
            for pr in range(n_pairs):
                at_ref[c * L:(c + 1) * L, pr * P:(pr + 1) * P] = (
                    a_band[:, pr * P:(pr + 1) * P] + stacked_scores(c, pr, ktm, 1)).astype(BF16)

    qe = (q * jnp.exp2(b)).astype(BF16)
    kd = (k * jnp.exp2(chunk_row(L - 1) - b)).astype(BF16)
    zv = jnp.zeros((L, 2 * GLA_DV), BF16)

    o_intra, d_st, dec = {}, {}, []
    for c in range(n_chunks):
        rs = slice(c * L, (c + 1) * L)
        dec.append(jnp.exp2(b[c * L + L - 1:c * L + L, :]))
        for pr in range(n_pairs):
            ls = slice(pr * P, (pr + 1) * P)
            vp = v_ref[rs, 2 * pr * GLA_DV:(2 * pr + 2) * GLA_DV]
            v_bd = jnp.concatenate([jnp.concatenate([vp[:, 0:GLA_DV], zv[:, 0:GLA_DV]], axis=1),
                                    jnp.concatenate([zv[:, 0:GLA_DV], vp[:, GLA_DV:]], axis=1)], axis=0)
            o_intra[c, pr] = _dot(at_ref[rs, ls], v_bd)
            d_st[c, pr] = jnp.where(bd_mask, _dot_tn(vp, kd[rs, ls]), 0.0)
    st_in = {}
    for gi in range(gpt):
        for pr in range(n_pairs):
            st = st_ref[gi, pr]
            for c in range(gi * cpg, (gi + 1) * cpg):
                st_in[c, pr] = st.astype(BF16)
                st = st * dec[c][:, pr * P:(pr + 1) * P] + d_st[c, pr]
            st_ref[gi, pr] = st
    for c in range(n_chunks):
        rs = slice(c * L, (c + 1) * L)
        for pr in range(n_pairs):
            on_ref[rs, 2 * pr * GLA_DV:(2 * pr + 2) * GLA_DV] = (
                o_intra[c, pr] + _dot_nt(qe[rs, pr * P:(pr + 1) * P], st_in[c, pr]))

    for hh in range(GLA_HEADS):
        hs = slice(hh * GLA_DV, (hh + 1) * GLA_DV)
        o = on_ref[:, hs]
        on = o * lax.rsqrt(jnp.mean(o * o, axis=-1, keepdims=True) + EPS) * g_ref[:, hs]
        o_ref[:, hs] = (on * r_ref[:, hs].astype(F32)).astype(BF16)

    @pl.when(t_idx == pl.num_programs(1) - 1)
    def _():
        for gi in range(gpt):
            for pr in range(n_pairs):
                st = st_ref[gi, pr]
                tt = jnp.where(even, st[0:GLA_DV], st[GLA_DV:2 * GLA_DV])
                sn_ref[gi, 2 * pr:2 * pr + 2] = tt.T.reshape(2, GLA_DK, GLA_DV)


def _band_spread():
    m = np.zeros((SUB, GLA_HEADS, GLA_DK, GLA_QK), np.float32)
    for d in range(SUB):
        for h in range(GLA_HEADS):
            m[d, h, :, h * GLA_DK + SUB - 1 - d] = 1.0
    return jnp.asarray(m.reshape(SUB * GLA_QK, GLA_QK), BF16)


def _gla(gq, gk, gv, la, gr, s0, g_out, *, groups, rows_per_group, tc):
    gpt = max(1, tc // rows_per_group)
    nt = max(1, rows_per_group // tc)

    def rows(c):
        return pl.BlockSpec((tc, c), lambda g, t: (g * nt + t, 0))

    state = pl.BlockSpec((gpt, GLA_HEADS, GLA_DK, GLA_DV), lambda g, t: (g, 0, 0, 0))
    spread = _band_spread()
    return pl.pallas_call(
        functools.partial(_gla_kernel, gpt=gpt),
        grid=(groups // gpt, nt),
        in_specs=[rows(GLA_QK), rows(GLA_QK), rows(GLA_WIDTH), rows(GLA_QK), rows(GLA_WIDTH), state,
                  _const((1, GLA_WIDTH)), _const(spread.shape)],
        out_specs=[rows(GLA_WIDTH), state],
        out_shape=[jax.ShapeDtypeStruct((groups * rows_per_group, GLA_WIDTH), BF16),
                   jax.ShapeDtypeStruct((groups, GLA_HEADS, GLA_DK, GLA_DV), F32)],
        scratch_shapes=[pltpu.VMEM((gpt, GLA_HEADS // 2, 2 * GLA_DV, LANES), F32),
                        pltpu.VMEM((SUB + tc, GLA_QK), F32), pltpu.VMEM((SUB + tc, GLA_QK), F32),
                        pltpu.VMEM((tc, SUB * GLA_QK), BF16), pltpu.VMEM((tc, GLA_WIDTH), F32),
                        pltpu.VMEM((tc, GLA_QK), BF16)],
        compiler_params=_params("arbitrary", "arbitrary"),
        name="gla",
    )(gq, gk, gv, la, gr, s0, g_out, spread)


def _mlp_tile(x_ref, a_ref, b_ref, g1_ref, sh2_ref, sc2_ref, g2_ref, gn_ref, wo_ref, wu_ref, wd_ref, y_ref, *, gpt):
    tm, d = x_ref.shape

    def per_group(val, ref, scale_plus_one=False):
        m = ref[...]
        if scale_plus_one:
            m = 1.0 + m
        return (val.reshape(gpt, tm // gpt, d) * m).reshape(tm, d)

    mix = jnp.concatenate([a_ref[...], b_ref[...]], axis=1)
    x1 = x_ref[...] + per_group(_dot(mix, wo_ref[...]), g1_ref)
    xn = x1 * lax.rsqrt(jnp.mean(x1 * x1, axis=-1, keepdims=True) + EPS) * gn_ref[...]
    h2 = (per_group(xn, sc2_ref, True).reshape(gpt, tm // gpt, d) + sh2_ref[...]).reshape(tm, d).astype(BF16)
    acc = jnp.zeros((tm, d), F32)
    for j in range(wu_ref.shape[1] // FF_SLICE):
        u = jnp.maximum(_dot(h2, wu_ref[:, j * FF_SLICE:(j + 1) * FF_SLICE]), 0.0)
        acc += _dot((u * u).astype(BF16), wd_ref[j * FF_SLICE:(j + 1) * FF_SLICE, :])
    y_ref[...] = x1 + per_group(acc, g2_ref)


def _mlp_kernel(*refs, tiles, gpts):
    per_phase = 7
    gn_ref, wo_ref, wu_ref, wd_ref = refs[len(tiles) * per_phase:len(tiles) * per_phase + 4]
    y_refs = refs[len(tiles) * per_phase + 4:]
    i = pl.program_id(0)
    first = 0
    for p, (n_tiles, gpt) in enumerate(zip(tiles, gpts)):
        ins = refs[p * per_phase:(p + 1) * per_phase]

        @pl.when(jnp.logical_and(i >= first, i < first + n_tiles))
        def _(ins=ins, p=p, gpt=gpt):
            _mlp_tile(*ins, gn_ref, wo_ref, wu_ref, wd_ref, y_refs[p], gpt=gpt)

        first += n_tiles


def _mlp(phases, w, *, tm):
    d = phases[0][0].shape[1]
    tiles = [ph[0].shape[0] // tm for ph in phases]
    gpts = [max(1, tm // ph[4]) for ph in phases]
    in_specs, args, first = [], [], 0
    for (x2, a_out, b_out, mod4, rpg), n_tiles, gpt in zip(phases, tiles, gpts):
        tpg = max(1, rpg // tm)

        def tile(i, first=first, n_tiles=n_tiles):
            return jnp.clip(i - first, 0, n_tiles - 1)

        def rows(c, tile=tile):
            return pl.BlockSpec((tm, c), lambda i: (tile(i), 0))

        def mod_spec(j, tile=tile, gpt=gpt, tpg=tpg):
            return pl.BlockSpec((gpt, None, 1, d), lambda i: (tile(i) // tpg, j, 0, 0))

        in_specs += [rows(d), rows(MLA_WIDTH), rows(GLA_WIDTH), mod_spec(2), mod_spec(3), mod_spec(4), mod_spec(5)]
        args += [x2, a_out, b_out, mod4, mod4, mod4, mod4]
        first += n_tiles
    out_specs, first = [], 0
    for n_tiles in tiles:
        out_specs.append(pl.BlockSpec((tm, d), lambda i, first=first, n_tiles=n_tiles:
                                      (jnp.clip(i - first, 0, n_tiles - 1), 0)))
        first += n_tiles
    return pl.pallas_call(
        functools.partial(_mlp_kernel, tiles=tuple(tiles), gpts=tuple(gpts)),
        grid=(sum(tiles),),
        in_specs=in_specs + [_const((1, d)), _const(w["w_out"].shape), _const(w["w_up"].shape),
                             _const(w["w_down"].shape)],
        out_specs=out_specs,
        out_shape=[jax.ShapeDtypeStruct(ph[0].shape, F32) for ph in phases],
        compiler_params=_params("arbitrary"),
        name="out_proj_mlp",
    )(*args, w["g_norm2"], w["w_out"], w["w_up"], w["w_down"])


def _rope_table(start, count, repeat=1):
    half = QK_ROPE // 2
    inv = ROPE_THETA ** (-np.arange(half, dtype=np.float64) / half)
    ang = (start + np.arange(count, dtype=np.float64))[:, None] * inv[None, :]
    c, s, z = np.cos(ang), np.sin(ang), np.zeros_like(ang)
    tab = np.concatenate([c, c, z, z, -s, z, z, z, z, s, z, z], axis=1).astype(np.float32)
    return jnp.asarray(np.tile(tab, (repeat, 1)))


def _pad_gain(g_rope):
    return jnp.concatenate([g_rope, jnp.zeros((LANES - QK_ROPE,), F32)]).reshape(1, LANES)


def _relayout_kernel(wint_ref, wuq_ref, wukv_ref, wg_ref, win_o, wuq_o, wukv_o, wukt_o, wuv_o, wg_o):
    s = np.cumsum([0, Q_LORA, KV_LORA, QK_ROPE, GLA_QK, GLA_QK, GLA_WIDTH, GLA_GATE_RANK, GLA_WIDTH])

    def piece(i):
        return wint_ref[int(s[i]):int(s[i + 1]), :]

    d = wint_ref.shape[1]
    zeros = jnp.zeros((LANES - QK_ROPE - GLA_GATE_RANK, d), F32)
    win_o[:, 0:Q_LORA] = piece(0).T.astype(BF16)
    win_o[:, Q_LORA:C_QKR[1]] = jnp.concatenate([piece(2), piece(6), zeros], axis=0).T.astype(BF16)
    win_o[:, C_KV[0]:C_KV[1]] = piece(1).T.astype(BF16)
    win_o[:, C_GQ[0]:C_GQ[1]] = (piece(3).T * (GLA_DK ** -0.5)).astype(BF16)
    win_o[:, C_GK[0]:C_GK[1]] = piece(4).T.astype(BF16)
    win_o[:, C_GV[0]:C_GV[1]] = piece(5).T.astype(BF16)
    win_o[:, C_GR[0]:C_GR[1]] = piece(7).T.astype(BF16)

    zq = jnp.zeros((Q_LORA, HEAD_PAD - QK_DIM), BF16)
    kvw = QK_NOPE + V_DIM
    for hh in range(MLA_HEADS):
        wuq_o[:, hh * HEAD_PAD:hh * HEAD_PAD + QK_DIM] = wuq_ref[:, hh * QK_DIM:(hh + 1) * QK_DIM].astype(BF16)
        wuq_o[:, hh * HEAD_PAD + QK_DIM:(hh + 1) * HEAD_PAD] = zq
        uk = wukv_ref[:, hh * kvw:hh * kvw + QK_NOPE]
        uv = wukv_ref[:, hh * kvw + QK_NOPE:(hh + 1) * kvw].astype(BF16)
        wukv_o[:, hh * QK_NOPE:(hh + 1) * QK_NOPE] = uk.astype(BF16)
        wukv_o[:, (MLA_HEADS + hh) * V_DIM:(MLA_HEADS + hh + 1) * V_DIM] = uv
        wukt_o[hh * QK_NOPE:(hh + 1) * QK_NOPE, :] = uk.T.astype(BF16)
        wuv_o[:, hh * V_DIM:(hh + 1) * V_DIM] = uv

    wg_o[...] = jnp.zeros(wg_o.shape, BF16)
    wg_o[QK_ROPE:QK_ROPE + GLA_GATE_RANK, :] = wg_ref[...].astype(BF16)


def _relayout(w_in_t, w_uq, w_ukv, w_gate_up):
    d = w_in_t.shape[1]
    shapes = [(d, C_GR[1]), (Q_LORA, MLA_HEADS * HEAD_PAD), (KV_LORA, MLA_HEADS * (QK_NOPE + V_DIM)),
              (MLA_HEADS * QK_NOPE, KV_LORA), (KV_LORA, MLA_WIDTH), (LANES, GLA_QK)]
    return pl.pallas_call(
        _relayout_kernel,
        out_shape=[jax.ShapeDtypeStruct(sh, BF16) for sh in shapes],
        compiler_params=pltpu.CompilerParams(vmem_limit_bytes=VMEM_LIMIT),
        name="weight_relayout",
    )(w_in_t, w_uq, w_ukv, w_gate_up)


def _prep_weights(w_in, g_norm1, g_q_lora, w_uq, g_kv_lora, w_ukv, g_q_head, g_k_head,
                  w_gate_up, b_gate_up, g_gla_out, w_out, g_norm2, w_up, w_down):
    d = w_in.shape[0]
    w_in_p, w_uq_p, w_ukv_p, w_ukt, w_uv, w_gate = _relayout(w_in.T, w_uq, w_ukv, w_gate_up)
    qscale = QK_DIM ** -0.5 * LOG2E
    bound = 1.02 * QK_DIM ** 0.5 * LOG2E * jnp.max(jnp.abs(g_q_head)) * jnp.max(jnp.abs(g_k_head))
    lane = jnp.arange(LANES) == QK_ROPE
    return {
        "w_in": w_in_p, "g_norm1": g_norm1.reshape(1, d), "g_q_lora": g_q_lora.reshape(1, Q_LORA),
        "w_uq": w_uq_p, "g_kv_lora": g_kv_lora.reshape(1, KV_LORA), "w_ukv": w_ukv_p, "w_ukt": w_ukt, "w_uv": w_uv,
        "gkr_col": g_k_head[QK_NOPE:].reshape(QK_ROPE, 1),
        "qone": lane.astype(F32).reshape(1, LANES), "kbias": jnp.where(lane, -bound, 0.0).reshape(1, LANES),
        "fast_softmax": (bound <= MAX_FIXED_SHIFT).astype(jnp.int32).reshape(1), "bound": bound.reshape(1),
        "gqn": (g_q_head[:QK_NOPE] * qscale).reshape(1, LANES), "gqr": _pad_gain(g_q_head[QK_NOPE:] * qscale),
        "gkn": g_k_head[:QK_NOPE].reshape(1, LANES), "gkr": _pad_gain(g_k_head[QK_NOPE:]),
        "w_gate": w_gate, "b_gate": b_gate_up.reshape(1, GLA_QK),
        "g_gla_out": g_gla_out.reshape(1, GLA_WIDTH), "w_out": w_out.astype(BF16),
        "g_norm2": g_norm2.reshape(1, d), "w_up": w_up.astype(BF16), "w_down": w_down.astype(BF16),
    }


def _mixers(x, mod, past_lat, past_kr, s0, w, *, tm):
    batch, seq, d = x.shape
    n = batch * seq
    past = 0 if past_lat is None else past_lat.shape[1]
    x2 = x.reshape(n, d)
    mod4 = mod.reshape(batch, 6, 1, d)
    tm = min(tm, n)
    tab = _rope_table(past, seq, repeat=max(1, tm // seq))
    if past == 0:
        lat, krt, q, k, v, gq, gk, gv, la, gr = _projection(x2, mod4, tab, w, rows_per_group=seq, tm=tm, prompt=True)
        kr = jnp.swapaxes(krt, 1, 2)
        a_out = _attention_prompt(w["fast_softmax"], q, k, v, batch=batch, seq=seq, tq=min(ATTN_TILE, seq))
    else:
        assert seq == CHUNK and past % CHUNK == 0
        lat, kr, krt, q, gq, gk, gv, la, gr = _projection(x2, mod4, tab, w, rows_per_group=seq, tm=tm, prompt=False)
        a_out = _attention_sample(q, past_lat, jnp.swapaxes(past_kr, 1, 2), lat, krt, w, tkb=min(CACHE_BLOCK, past))
    b_out, s_new = _gla(gq, gk, gv, la, gr, s0, w["g_gla_out"], groups=batch, rows_per_group=seq,
                        tc=tm)
    return (x2, a_out, b_out, mod4, seq), (lat.reshape(batch, seq, KV_LORA), kr.reshape(batch, seq, QK_ROPE), s_new)


def kernel(x_prompt, x_sample, cache_mla_latent, cache_mla_krope, state_gla, c_prompt, c_sample,
           w_ada, b_ada, g_norm1, w_in, g_q_lora, w_uq, g_kv_lora, w_ukv, g_q_head, g_k_head,
           w_gate_up, b_gate_up, g_gla_out, w_out, g_norm2, w_up, w_down):
    nb = x_prompt.shape[0]
    depth = w_ada.shape[0]
    y_p, y_s = x_prompt, x_sample
    outs = [[] for _ in range(6)]
    for l in range(depth):
        w = _prep_weights(w_in[l], g_norm1[l], g_q_lora[l], w_uq[l], g_kv_lora[l], w_ukv[l], g_q_head[l],
                          g_k_head[l], w_gate_up[l], b_gate_up[l], g_gla_out[l], w_out[l], g_norm2[l],
                          w_up[l], w_down[l])
        mod_p, mod_s = _modulation(c_prompt, c_sample, w_ada[l], b_ada[l])
        zero_state = jnp.zeros((nb, GLA_HEADS, GLA_DK, GLA_DV), x_prompt.dtype)
        mix_p, new_p = _mixers(y_p, mod_p, None, None, zero_state, w, tm=ROW_TILE)
        mix_s, new_s = _mixers(y_s, mod_s, cache_mla_latent[l], cache_mla_krope[l], state_gla[l], w, tm=ROW_TILE)
        y2_p, y2_s = _mlp([mix_p, mix_s], w, tm=ROW_TILE)
        y_p, y_s = y2_p.reshape(y_p.shape), y2_s.reshape(y_s.shape)
        for o, new in zip(outs, new_p + new_s):
            o.append(new)
    return (y_p, y_s) + tuple(jnp.stack(o) for o in outs)
```

```python
import functools

import jax
import jax.numpy as jnp
import numpy as np
from jax import lax
from jax.experimental import pallas as pl
from jax.experimental.pallas import tpu as pltpu

F32 = jnp.float32
BF16 = jnp.bfloat16

CHUNK = 64
EPS = 1e-6
MLA_HEADS = 4
Q_LORA = 384
KV_LORA = 256
QK_NOPE = 128
QK_ROPE = 64
QK_DIM = QK_NOPE + QK_ROPE
V_DIM = 128
ROPE_THETA = 10000.0
GLA_HEADS = 4
GLA_DK = 64
GLA_DV = 128
GLA_GATE_RANK = 16
GLA_TAU = 16.0
GLA_QK = GLA_HEADS * GLA_DK
GLA_WIDTH = GLA_HEADS * GLA_DV
MLA_WIDTH = MLA_HEADS * V_DIM
HEAD_PAD = 256
SUB = 8
LOG2E = 1.4426950408889634
MAX_FIXED_SHIFT = 48.0

LANES = 128
VMEM_LIMIT = 56 * 1024 * 1024
ROW_TILE = 512
ATTN_TILE = 512
CACHE_BLOCK = 512
FF_SLICE = 1024
MOD_TILE = 1024

C_QKR = (0, 512)
C_KV = (512, 768)
C_GQ = (768, 1024)
C_GK = (1024, 1280)
C_GV = (1280, 1792)
C_GR = (1792, 2304)


def _dot(a, b):
    return jnp.dot(a, b, preferred_element_type=F32)


def _dot_nt(a, b):
    return lax.dot_general(a, b, (((1,), (1,)), ((), ())), preferred_element_type=F32)


def _dot_tn(a, b):
    return lax.dot_general(a, b, (((0,), (0,)), ((), ())), preferred_element_type=F32)


def _rope_tile(t, c, sa, sb):
    return t * c + pltpu.roll(t, 96, 1) * sa + pltpu.roll(t, 32, 1) * sb


def _params(*sem):
    return pltpu.CompilerParams(dimension_semantics=sem, vmem_limit_bytes=VMEM_LIMIT)


def _const(shape):
    return pl.BlockSpec(shape, lambda *_: (0,) * len(shape), pipeline_mode=pl.Buffered(1))


def _mod_kernel(cs_ref, cp_ref, w_ref, b_ref, os_ref, op_ref):
    ns = cs_ref.shape[0]
    c = jnp.concatenate([cs_ref[...], cp_ref[...]], axis=0)
    s = (c * jax.nn.sigmoid(c)).astype(BF16)
    mod = _dot(s, w_ref[...].astype(BF16)) + b_ref[...]
    os_ref[...] = mod[0:ns]
    op_ref[...] = mod[ns:]


def _modulation(c_prompt, c_sample, w_ada, b_ada):
    d, n = w_ada.shape
    tn = MOD_TILE

    def whole(c):
        return pl.BlockSpec(c.shape, lambda j: (0, 0))

    def cols(c):
        return pl.BlockSpec((c.shape[0], tn), lambda j: (0, j))

    mod_s, mod_p = pl.pallas_call(
        _mod_kernel,
        grid=(n // tn,),
        in_specs=[whole(c_sample), whole(c_prompt),
                  pl.BlockSpec((d, tn), lambda j: (0, j)),
                  pl.BlockSpec((1, tn), lambda j: (0, j))],
        out_specs=[cols(c_sample), cols(c_prompt)],
        out_shape=[jax.ShapeDtypeStruct((c_sample.shape[0], n), F32), jax.ShapeDtypeStruct((c_prompt.shape[0], n), F32)],
        compiler_params=_params("arbitrary"),
        name="adaln_mod",
    )(c_sample, c_prompt, w_ada, b_ada.reshape(1, n))
    return mod_p, mod_s


def _proj_kernel(x_ref, sh_ref, sc_ref, g1_ref, win_ref, gql_ref, wuq_ref, gkv_ref, wukv_ref,
                 gqn_ref, gqr_ref, gkn_ref, gkr_ref, qone_ref, kbias_ref, tab_ref, wg_ref, bg_ref,
                 *outs, gpt, prompt):
    if prompt:
        lat_ref, krt_ref, q_ref, k_ref, v_ref, gq_ref, gk_ref, gv_ref, la_ref, gr_ref = outs
    else:
        lat_ref, kr_ref, krt_ref, q_ref, gq_ref, gk_ref, gv_ref, la_ref, gr_ref = outs
    tm, d = x_ref.shape
    x = x_ref[...]
    xn = x * lax.rsqrt(jnp.mean(x * x, axis=-1, keepdims=True) + EPS)
    h = (xn.reshape(gpt, tm // gpt, d) * (g1_ref[...] * (1.0 + sc_ref[...])) + sh_ref[...]).reshape(tm, d)
    hb = h.astype(BF16)

    def col(c):
        return _dot(hb, win_ref[:, c[0]:c[1]])

    tab = tab_ref[...]
    cos, sa, sb = tab[:, 0:LANES], tab[:, LANES:2 * LANES], tab[:, 2 * LANES:3 * LANES]

    qkr = col(C_QKR)
    krg = qkr[:, Q_LORA:]
    z = _dot(krg.astype(BF16), wg_ref[...]) + bg_ref[...]
    la_ref[...] = (jnp.minimum(z, 0.0) - jnp.log(1.0 + jnp.exp(-jnp.abs(z)))) * (1.0 / GLA_TAU)
    r = col(C_GR)
    gr_ref[...] = (r * jax.nn.sigmoid(r)).astype(BF16)

    cq = qkr[:, 0:Q_LORA]
    cqn = cq * lax.rsqrt(jnp.mean(cq * cq, axis=-1, keepdims=True) + EPS) * gql_ref[...]
    qp = _dot(cqn.astype(BF16), wuq_ref[...])
    for hh in range(MLA_HEADS):
        nope = qp[:, hh * HEAD_PAD:hh * HEAD_PAD + QK_NOPE]
        rt = qp[:, hh * HEAD_PAD + QK_NOPE:(hh + 1) * HEAD_PAD]
        ss = jnp.sum(nope * nope, axis=-1, keepdims=True) + jnp.sum(rt * rt, axis=-1, keepdims=True)
        inv = lax.rsqrt(ss * (1.0 / QK_DIM) + EPS)
        q_ref[:, hh * HEAD_PAD:hh * HEAD_PAD + QK_NOPE] = (nope * inv * gqn_ref[...]).astype(BF16)
        rq = _rope_tile(rt * inv * gqr_ref[...], cos, sa, sb)
        q_ref[:, hh * HEAD_PAD + QK_NOPE:(hh + 1) * HEAD_PAD] = (rq + qone_ref[...]).astype(BF16)

    ckv = col(C_KV)
    lat = ckv * lax.rsqrt(jnp.mean(ckv * ckv, axis=-1, keepdims=True) + EPS) * gkv_ref[...]
    lat_ref[...] = lat
    krt_ref[...] = krg.T[0:QK_ROPE, :]
    if prompt:
        lane = lax.broadcasted_iota(jnp.int32, (1, LANES), 1)
        krm = jnp.where(lane < QK_ROPE, krg, 0.0)
        ssr = jnp.sum(krm * krm, axis=-1, keepdims=True)
        rk = _rope_tile(krm * gkr_ref[...], cos, sa, sb)
        kv = _dot(lat.astype(BF16), wukv_ref[...])
        for hh in range(MLA_HEADS):
            kn = kv[:, hh * QK_NOPE:(hh + 1) * QK_NOPE]
            inv = lax.rsqrt((jnp.sum(kn * kn, axis=-1, keepdims=True) + ssr) * (1.0 / QK_DIM) + EPS)
            k_ref[:, hh * HEAD_PAD:hh * HEAD_PAD + QK_NOPE] = (kn * inv * gkn_ref[...]).astype(BF16)
            k_ref[:, hh * HEAD_PAD + QK_NOPE:(hh + 1) * HEAD_PAD] = (rk * inv + kbias_ref[...]).astype(BF16)
        v_ref[...] = kv[:, MLA_HEADS * QK_NOPE:].astype(BF16)
    else:
        kr_ref[...] = krg[:, 0:QK_ROPE]

    gq_ref[...] = col(C_GQ)
    gk_ref[...] = col(C_GK)
    gv_ref[...] = col(C_GV).astype(BF16)


def _projection(x2, mod4, tab, w, *, rows_per_group, tm, prompt):
    n, d = x2.shape
    gpt = max(1, tm // rows_per_group)
    tpg = max(1, rows_per_group // tm)
    ntab = tab.shape[0] // tm

    def mod_spec(j):
        return pl.BlockSpec((gpt, None, 1, d), lambda i: ((i // tpg) if gpt == 1 else i, j, 0, 0))

    def rows(c):
        return pl.BlockSpec((tm, c), lambda i: (i, 0))

    def out(c, t):
        return rows(c), jax.ShapeDtypeStruct((n, c), t)

    gla_outs = [out(GLA_QK, F32), out(GLA_QK, F32), out(GLA_WIDTH, BF16), out(GLA_QK, F32), out(GLA_WIDTH, BF16)]
    qo = out(MLA_HEADS * HEAD_PAD, BF16)
    if prompt:
        krt = (pl.BlockSpec((None, QK_ROPE, tm), lambda i: (i // tpg, 0, i % tpg)),
               jax.ShapeDtypeStruct((n // rows_per_group, QK_ROPE, rows_per_group), F32))
        outs = [out(KV_LORA, F32), krt, qo, out(MLA_HEADS * HEAD_PAD, BF16), out(MLA_WIDTH, BF16)] + gla_outs
    else:
        krt = (pl.BlockSpec((QK_ROPE, tm), lambda i: (0, i)), jax.ShapeDtypeStruct((QK_ROPE, n), F32))
        outs = [out(KV_LORA, F32), out(QK_ROPE, F32), krt, qo] + gla_outs
    return pl.pallas_call(
        functools.partial(_proj_kernel, gpt=gpt, prompt=prompt),
        grid=(n // tm,),
        in_specs=[rows(d), mod_spec(0), mod_spec(1), _const((1, d)), _const(w["w_in"].shape),
                  _const((1, Q_LORA)), _const(w["w_uq"].shape), _const((1, KV_LORA)), _const(w["w_ukv"].shape),
                  _const((1, LANES)), _const((1, LANES)), _const((1, LANES)), _const((1, LANES)),
                  _const((1, LANES)), _const((1, LANES)),
                  pl.BlockSpec((tm, 3 * LANES), lambda i: (i % ntab, 0)),
                  _const(w["w_gate"].shape), _const((1, GLA_QK))],
        out_specs=[o[0] for o in outs],
        out_shape=[o[1] for o in outs],
        compiler_params=_params("arbitrary"),
        name="in_proj",
    )(x2, mod4, mod4, w["g_norm1"], w["w_in"], w["g_q_lora"], w["w_uq"], w["g_kv_lora"], w["w_ukv"],
      w["gqn"], w["gqr"], w["gkn"], w["gkr"], w["qone"], w["kbias"], tab, w["w_gate"], w["b_gate"])


def _attn_prompt_kernel(fast_ref, q_ref, k_ref, v_ref, o_ref, vx_ref, m_ref, acc_ref, *, tq):
    i = pl.program_id(1)
    heads = range(MLA_HEADS)

    @pl.when(i == 0)
    def _():
        for hh in heads:
            vx_ref[hh, :, 0:V_DIM] = v_ref[:, hh * V_DIM:(hh + 1) * V_DIM]
            vx_ref[hh, :, V_DIM:] = jnp.ones((v_ref.shape[0], V_DIM), BF16)

    def scores(j, hh, masked):
        hs = slice(hh * HEAD_PAD, (hh + 1) * HEAD_PAD)
        s = _dot_nt(q_ref[:, hs], k_ref[pl.ds(pl.multiple_of(j * tq, tq), tq), hs])
        if masked:
            qc = lax.broadcasted_iota(jnp.int32, (tq, tq), 0) // CHUNK
            kc = lax.broadcasted_iota(jnp.int32, (tq, tq), 1) // CHUNK
            s = jnp.where(kc <= qc, s, -jnp.inf)
        return s

    def fast_diagonal():
        hq = tq // 2
        k0 = pl.multiple_of(i * tq, tq)
        qc = lax.broadcasted_iota(jnp.int32, (hq, hq), 0) // CHUNK
        kc = lax.broadcasted_iota(jnp.int32, (hq, hq), 1) // CHUNK
        tri = kc <= qc
        for hh in heads:
            hs = slice(hh * HEAD_PAD, (hh + 1) * HEAD_PAD)
            top = jnp.where(tri, _dot_nt(q_ref[0:hq, hs], k_ref[pl.ds(k0, hq), hs]), -jnp.inf)
            acc_ref[hh, 0:hq] = _dot(jnp.exp2(top).astype(BF16), vx_ref[hh, pl.ds(k0, hq), :])
            bot = _dot_nt(q_ref[hq:tq, hs], k_ref[pl.ds(k0, tq), hs])
            bot = jnp.concatenate([bot[:, 0:hq], jnp.where(tri, bot[:, hq:tq], -jnp.inf)], axis=1)
            acc_ref[hh, hq:tq] = _dot(jnp.exp2(bot).astype(BF16), vx_ref[hh, pl.ds(k0, tq), :])

    def vblock(j, hh):
        return vx_ref[hh, pl.ds(pl.multiple_of(j * tq, tq), tq), :]

    def fast_block(j, masked):
        for hh in heads:
            acc_ref[hh] += _dot(jnp.exp2(scores(j, hh, masked)).astype(BF16), vblock(j, hh))

    def safe_block(j, masked):
        for hh in heads:
            s = scores(j, hh, masked)
            m = m_ref[hh]
            m_new = jnp.maximum(m, jnp.max(s, axis=1, keepdims=True))
            p = jnp.exp2(s - m_new).astype(BF16)
            acc_ref[hh] = jnp.exp2(m - m_new) * acc_ref[hh] + _dot(p, vblock(j, hh))
            m_ref[hh] = m_new

    def sweep(block):
        def body(jj, c):
            block(2 * jj, False)
            block(2 * jj + 1, False)
            return c

        lax.fori_loop(0, i // 2, body, 0)

        @pl.when(i % 2 == 1)
        def _():
            block(i - 1, False)

    @pl.when(fast_ref[0] == 1)
    def _():
        fast_diagonal()
        sweep(fast_block)

    @pl.when(fast_ref[0] != 1)
    def _():
        m_ref[...] = jnp.full(m_ref.shape, -jnp.inf, F32)
        acc_ref[...] = jnp.zeros(acc_ref.shape, F32)
        safe_block(i, True)
        sweep(safe_block)

    for hh in heads:
        acc = acc_ref[hh]
        o_ref[:, hh * V_DIM:(hh + 1) * V_DIM] = (acc[:, 0:V_DIM] / acc[:, V_DIM:]).astype(BF16)


def _attention_prompt(fast, q, k, v, *, batch, seq, tq):
    nq = seq // tq
    return pl.pallas_call(
        functools.partial(_attn_prompt_kernel, tq=tq),
        grid=(batch, nq),
        in_specs=[pl.BlockSpec(memory_space=pltpu.SMEM),
                  pl.BlockSpec((tq, MLA_HEADS * HEAD_PAD), lambda b, i: (b * nq + i, 0)),
                  pl.BlockSpec((seq, MLA_HEADS * HEAD_PAD), lambda b, i: (b, 0)),
                  pl.BlockSpec((seq, MLA_WIDTH), lambda b, i: (b, 0))],
        out_specs=pl.BlockSpec((tq, MLA_WIDTH), lambda b, i: (b * nq + i, 0)),
        out_shape=jax.ShapeDtypeStruct((batch * seq, MLA_WIDTH), BF16),
        scratch_shapes=[pltpu.VMEM((MLA_HEADS, seq, 2 * V_DIM), BF16), pltpu.VMEM((MLA_HEADS, tq, 1), F32),
                        pltpu.VMEM((MLA_HEADS, tq, 2 * V_DIM), F32)],
        compiler_params=_params("arbitrary", "arbitrary"),
        name="mla_attn_prompt",
    )(fast, q, k, v)


def _attn_sample_kernel(fast_ref, bound_ref, q_ref, lat_ref, krt_ref, latn_ref, krtn_ref, tab_ref, wukt_ref, wuv_ref,
                        gkn_ref, gkr_ref, o_ref, wq_ref, latb_ref, s_ref, *, tkb):
    t = q_ref.shape[0] // 2
    past = lat_ref.shape[1]
    nk = MLA_HEADS * QK_NOPE
    half = QK_ROPE // 2
    streams = range(2)

    @pl.when(pl.program_id(0) == 0)
    def _():
        for e in streams:
            wq_ref[e, 0:nk, :] = wukt_ref[...]

    qr = []
    for e in streams:
        rows = slice(e * t, (e + 1) * t)
        parts = []
        for hh in range(MLA_HEADS):
            qn = (q_ref[rows, hh * HEAD_PAD:hh * HEAD_PAD + QK_NOPE].astype(F32) * gkn_ref[...]).astype(BF16)
            wq_ref[e, nk + hh * t:nk + (hh + 1) * t, :] = _dot(
                qn, wukt_ref[hh * QK_NOPE:(hh + 1) * QK_NOPE, :]).astype(BF16)
            parts.append(q_ref[rows, hh * HEAD_PAD + QK_NOPE:(hh + 1) * HEAD_PAD])
        qr.append(jnp.concatenate(parts, axis=0))

    def key_block(e, lat, krt, tab, width, valid):
        latb = lat.astype(BF16)
        g = _dot_nt(wq_ref[e], latb)
        ssr = jnp.sum(krt * krt, axis=0, keepdims=True)
        kg = krt * gkr_ref[...]
        x1, x2 = kg[0:half], kg[half:QK_ROPE]
        c, sn = tab[0:half], tab[half:QK_ROPE]
        rope = jnp.concatenate([x1 * c - x2 * sn, x2 * c + x1 * sn, jnp.zeros((LANES - QK_ROPE, width), F32)], axis=0)
        srope = _dot(qr[e], rope.astype(BF16))
        rows = []
        for hh in range(MLA_HEADS):
            kn = g[hh * QK_NOPE:(hh + 1) * QK_NOPE]
            inv = lax.rsqrt((jnp.sum(kn * kn, axis=0, keepdims=True) + ssr) * (1.0 / QK_DIM) + EPS)
            rows.append((g[nk + hh * t:nk + (hh + 1) * t] + srope[hh * t:(hh + 1) * t]) * inv)
        sc = jnp.concatenate(rows, axis=0) - bound_ref[0]
        if valid is not None:
            sc = jnp.where(valid, sc, -jnp.inf)
        return sc, latb

    def blocks():
        for blk in range(past // tkb):
            c0 = blk * tkb
            for e in streams:
                yield e, c0, tkb, key_block(e, lat_ref[e, c0:c0 + tkb, :], krt_ref[e, :, c0:c0 + tkb],
                                            tab_ref[:, c0:c0 + tkb], tkb, None)
        for e in streams:
            mine = lax.broadcasted_iota(jnp.int32, (1, 2 * t), 1) // t == e
            yield e, past, 2 * t, key_block(e, latn_ref[...], krtn_ref[...], tab_ref[:, past:past + 2 * t], 2 * t, mine)

    def finish(e, wlat, l):
        wlat = wlat.astype(BF16)
        for hh in range(MLA_HEADS):
            o = _dot(wlat[hh * t:(hh + 1) * t], wuv_ref[:, hh * V_DIM:(hh + 1) * V_DIM])
            o_ref[e * t:(e + 1) * t, hh * V_DIM:(hh + 1) * V_DIM] = (o / l[hh * t:(hh + 1) * t]).astype(BF16)

    @pl.when(fast_ref[0] == 1)
    def _():
        wlat = [jnp.zeros((MLA_HEADS * t, KV_LORA), F32) for _ in streams]
        lsum = [jnp.zeros((MLA_HEADS * t, LANES), F32) for _ in streams]
        for e, _, width, (sc, latb) in blocks():
            p = jnp.exp2(sc)
            wlat[e] += _dot(p.astype(BF16), latb)
            for c in range(width // LANES):
                lsum[e] += p[:, c * LANES:(c + 1) * LANES]
        for e in streams:
            finish(e, wlat[e], jnp.sum(lsum[e], axis=1, keepdims=True))

    @pl.when(fast_ref[0] != 1)
    def _():
        for e, c0, width, (sc, latb) in blocks():
            s_ref[e, :, c0:c0 + width] = sc
            latb_ref[e, c0:c0 + width, :] = latb
        for e in streams:
            s = s_ref[e]
            p = jnp.exp2(s - jnp.max(s, axis=1, keepdims=True))
            finish(e, _dot(p.astype(BF16), latb_ref[e]), jnp.sum(p, axis=1, keepdims=True))


def _attention_sample(q, past_lat, past_krt, lat_new, krt_new, w, *, tkb):
    batch, past, _ = past_lat.shape
    t = q.shape[0] // batch
    half = QK_ROPE // 2
    inv = ROPE_THETA ** (-np.arange(half, dtype=np.float64) / half)
    pos = np.concatenate([np.arange(past), past + np.arange(t), past + np.arange(t)]).astype(np.float64)
    ang = inv[:, None] * pos[None, :]
    tab = jnp.asarray(np.concatenate([np.cos(ang), np.sin(ang)], axis=0).astype(np.float32))
    s_pad = past + 2 * t
    return pl.pallas_call(
        functools.partial(_attn_sample_kernel, tkb=tkb),
        grid=(batch // 2,),
        in_specs=[pl.BlockSpec(memory_space=pltpu.SMEM), pl.BlockSpec(memory_space=pltpu.SMEM),
                  pl.BlockSpec((2 * t, MLA_HEADS * HEAD_PAD), lambda g: (g, 0)),
                  pl.BlockSpec((2, past, KV_LORA), lambda g: (g, 0, 0)),
                  pl.BlockSpec((2, QK_ROPE, past), lambda g: (g, 0, 0)),
                  pl.BlockSpec((2 * t, KV_LORA), lambda g: (g, 0)),
                  pl.BlockSpec((QK_ROPE, 2 * t), lambda g: (0, g)),
                  _const(tab.shape), _const(w["w_ukt"].shape), _const(w["w_uv"].shape),
                  _const((1, LANES)), _const((QK_ROPE, 1))],
        out_specs=pl.BlockSpec((2 * t, MLA_WIDTH), lambda g: (g, 0)),
        out_shape=jax.ShapeDtypeStruct((batch * t, MLA_WIDTH), BF16),
        scratch_shapes=[pltpu.VMEM((2, MLA_HEADS * (QK_NOPE + t), KV_LORA), BF16),
                        pltpu.VMEM((2, s_pad, KV_LORA), BF16),
                        pltpu.VMEM((2, MLA_HEADS * t, s_pad), F32)],
        compiler_params=_params("arbitrary"),
        name="mla_attn_sample",
    )(w["fast_softmax"], w["bound"], q, past_lat, past_krt, lat_new, krt_new, tab, w["w_ukt"], w["w_uv"], w["gkn"],
      w["gkr_col"])


def _gla_kernel(q_ref, k_ref, v_ref, la_ref, r_ref, s0_ref, g_ref, spread_ref, o_ref, sn_ref,
                st_ref, kp_ref, ap_ref, p_ref, on_ref, *, gpt):
    t_idx = pl.program_id(1)
    L, W, P = CHUNK, GLA_QK, LANES
    R = q_ref.shape[0]
    n_chunks = R // L
    cpg = n_chunks // gpt
    n_pairs = GLA_HEADS // 2

    lane_p = lax.broadcasted_iota(jnp.int32, (1, P), 1)
    even = lane_p < GLA_DK
    bd_mask = (lax.broadcasted_iota(jnp.int32, (2 * GLA_DV, P), 0) // GLA_DV
               == lax.broadcasted_iota(jnp.int32, (2 * GLA_DV, P), 1) // GLA_DK)

    @pl.when(t_idx == 0)
    def _():
        kp_ref[0:SUB, :] = jnp.zeros((SUB, W), F32)
        ap_ref[0:SUB, :] = jnp.zeros((SUB, W), F32)
        for gi in range(gpt):
            for pr in range(n_pairs):
                tt = s0_ref[gi, 2 * pr:2 * pr + 2].reshape(2 * GLA_DK, GLA_DV).T
                st_ref[gi, pr] = jnp.where(bd_mask, jnp.concatenate([tt, tt], axis=0), 0.0)

    q = q_ref[...]
    k = k_ref[...]
    la = la_ref[...]

    tri = (lax.broadcasted_iota(jnp.int32, (L, L), 0) >= lax.broadcasted_iota(jnp.int32, (L, L), 1)).astype(BF16)
    la_hi = la.astype(BF16)
    la2 = jnp.concatenate([la_hi, (la - la_hi.astype(F32)).astype(BF16)], axis=1)
    bs = []
    for c in range(n_chunks):
        t2 = _dot(tri, la2[c * L:(c + 1) * L, :])
        bs.append(t2[:, 0:W] + t2[:, W:2 * W])
    b = (jnp.concatenate(bs, axis=0) if n_chunks > 1 else bs[0]) * LOG2E
    b3 = b.reshape(n_chunks, L, W)

    def chunk_row(r):
        return jnp.broadcast_to(b3[:, r:r + 1, :], (n_chunks, L, W)).reshape(R, W)

    b_sub = jnp.broadcast_to(b.reshape(R // SUB, SUB, W)[:, 0:1, :], (R // SUB, SUB, W)).reshape(R, W)
    sub = (lax.broadcasted_iota(jnp.int32, (R, W), 0) % L) // SUB

    qt = q * jnp.exp2(b - b_sub)
    zb = jnp.zeros((), BF16)
    zq = jnp.zeros((SUB, P), F32)
    ktm = []
    for i in range(1, L // SUB):
        kt = (k * jnp.exp2(chunk_row(i * SUB) - b)).astype(BF16)
        ktm.append([[jnp.where((sub[:, 0:P] < i) & (even if e == 0 else ~even), kt[:, pr * P:(pr + 1) * P], zb)
                     for e in range(2)] for pr in range(n_pairs)])

    a = jnp.exp(la)
    kp_ref[SUB:SUB + R, :] = k
    ap_ref[SUB:SUB + R, :] = a
    p_ref[:, 0:W] = (q * k).astype(BF16)
    e = a
    for d in range(1, SUB):
        if d > 1:
            e = e * ap_ref[SUB - d + 1:SUB - d + 1 + R, :]
        p_ref[:, d * W:(d + 1) * W] = (q * kp_ref[SUB - d:SUB - d + R, :] * e).astype(BF16)
    cband = _dot(p_ref[...], spread_ref[...])
    same_sub = (lax.broadcasted_iota(jnp.int32, (L, W), 0) // SUB
                == (lax.broadcasted_iota(jnp.int32, (L, W), 1) % L) // SUB)

    qe = (q * jnp.exp2(b)).astype(BF16)
    kd = (k * jnp.exp2(chunk_row(L - 1) - b)).astype(BF16)
    zv = jnp.zeros((L, 2 * GLA_DV), BF16)

    o_intra, d_st, dec = {}, {}, []
    for c in range(n_chunks):
        rs = slice(c * L, (c + 1) * L)
        a_band = jnp.where(same_sub, pltpu.roll(cband[rs], W - (SUB - 1), 1, stride=1, stride_axis=0), 0.0)
        dec.append(jnp.exp2(b[c * L + L - 1:c * L + L, :]))
        for pr in range(n_pairs):
            ls = slice(pr * P, (pr + 1) * P)
            lhs_c = jnp.concatenate(
                [jnp.concatenate([qt[c * L + r * SUB:c * L + (r + 1) * SUB, ls] if i == r else zq
                                  for i in range(1, L // SUB)], axis=1) for r in range(L // SUB)],
                axis=0).astype(BF16)
            rhs_c = jnp.concatenate([jnp.concatenate([m[pr][0][rs], m[pr][1][rs]], axis=0) for m in ktm], axis=1)
            a_tot = (a_band[:, ls] + _dot_nt(lhs_c, rhs_c)).astype(BF16)
            vp = v_ref[rs, 2 * pr * GLA_DV:(2 * pr + 2) * GLA_DV]
            v_bd = jnp.concatenate([jnp.concatenate([vp[:, 0:GLA_DV], zv[:, 0:GLA_DV]], axis=1),
                                    jnp.concatenate([zv[:, 0:GLA_DV], vp[:, GLA_DV:]], axis=1)], axis=0)
            o_intra[c, pr] = _dot(a_tot, v_bd)
            d_st[c, pr] = jnp.where(bd_mask, _dot_tn(vp, kd[rs, ls]), 0.0)
    st_in = {}
    for gi in range(gpt):
        for pr in range(n_pairs):
            st = st_ref[gi, pr]
            for c in range(gi * cpg, (gi + 1) * cpg):
                st_in[c, pr] = st.astype(BF16)
                st = st * dec[c][:, pr * P:(pr + 1) * P] + d_st[c, pr]
            st_ref[gi, pr] = st
    for c in range(n_chunks):
        rs = slice(c * L, (c + 1) * L)
        for pr in range(n_pairs):
            on_ref[rs, 2 * pr * GLA_DV:(2 * pr + 2) * GLA_DV] = (
                o_intra[c, pr] + _dot_nt(qe[rs, pr * P:(pr + 1) * P], st_in[c, pr]))

    for hh in range(GLA_HEADS):
        hs = slice(hh * GLA_DV, (hh + 1) * GLA_DV)
        o = on_ref[:, hs]
        on = o * lax.rsqrt(jnp.mean(o * o, axis=-1, keepdims=True) + EPS) * g_ref[:, hs]
        o_ref[:, hs] = (on * r_ref[:, hs].astype(F32)).astype(BF16)

    @pl.when(t_idx == pl.num_programs(1) - 1)
    def _():
        for gi in range(gpt):
            for pr in range(n_pairs):
                st = st_ref[gi, pr]
                tt = jnp.where(even, st[0:GLA_DV], st[GLA_DV:2 * GLA_DV])
                sn_ref[gi, 2 * pr:2 * pr + 2] = tt.T.reshape(2, GLA_DK, GLA_DV)


def _band_spread():
    m = np.zeros((SUB, GLA_HEADS, GLA_DK, GLA_QK), np.float32)
    for d in range(SUB):
        for h in range(GLA_HEADS):
            m[d, h, :, h * GLA_DK + SUB - 1 - d] = 1.0
    return jnp.asarray(m.reshape(SUB * GLA_QK, GLA_QK), BF16)


def _gla(gq, gk, gv, la, gr, s0, g_out, *, groups, rows_per_group, tc):
    gpt = max(1, tc // rows_per_group)
    nt = max(1, rows_per_group // tc)

    def rows(c):
        return pl.BlockSpec((tc, c), lambda g, t: (g * nt + t, 0))

    state = pl.BlockSpec((gpt, GLA_HEADS, GLA_DK, GLA_DV), lambda g, t: (g, 0, 0, 0))
    spread = _band_spread()
    return pl.pallas_call(
        functools.partial(_gla_kernel, gpt=gpt),
        grid=(groups // gpt, nt),
        in_specs=[rows(GLA_QK), rows(GLA_QK), rows(GLA_WIDTH), rows(GLA_QK), rows(GLA_WIDTH), state,
                  _const((1, GLA_WIDTH)), _const(spread.shape)],
        out_specs=[rows(GLA_WIDTH), state],
        out_shape=[jax.ShapeDtypeStruct((groups * rows_per_group, GLA_WIDTH), BF16),
                   jax.ShapeDtypeStruct((groups, GLA_HEADS, GLA_DK, GLA_DV), F32)],
        scratch_shapes=[pltpu.VMEM((gpt, GLA_HEADS // 2, 2 * GLA_DV, LANES), F32),
                        pltpu.VMEM((SUB + tc, GLA_QK), F32), pltpu.VMEM((SUB + tc, GLA_QK), F32),
                        pltpu.VMEM((tc, SUB * GLA_QK), BF16), pltpu.VMEM((tc, GLA_WIDTH), F32)],
        compiler_params=_params("arbitrary", "arbitrary"),
        name="gla",
    )(gq, gk, gv, la, gr, s0, g_out, spread)


def _mlp_tile(x_ref, a_ref, b_ref, g1_ref, sh2_ref, sc2_ref, g2_ref, gn_ref, wo_ref, wu_ref, wd_ref, y_ref, *, gpt):
    tm, d = x_ref.shape

    def per_group(val, ref, scale_plus_one=False):
        m = ref[...]
        if scale_plus_one:
            m = 1.0 + m
        return (val.reshape(gpt, tm // gpt, d) * m).reshape(tm, d)

    mix = jnp.concatenate([a_ref[...], b_ref[...]], axis=1)
    x1 = x_ref[...] + per_group(_dot(mix, wo_ref[...]), g1_ref)
    xn = x1 * lax.rsqrt(jnp.mean(x1 * x1, axis=-1, keepdims=True) + EPS) * gn_ref[...]
    h2 = (per_group(xn, sc2_ref, True).reshape(gpt, tm // gpt, d) + sh2_ref[...]).reshape(tm, d).astype(BF16)
    acc = jnp.zeros((tm, d), F32)
    for j in range(wu_ref.shape[1] // FF_SLICE):
        u = jnp.maximum(_dot(h2, wu_ref[:, j * FF_SLICE:(j + 1) * FF_SLICE]), 0.0)
        acc += _dot((u * u).astype(BF16), wd_ref[j * FF_SLICE:(j + 1) * FF_SLICE, :])
    y_ref[...] = x1 + per_group(acc, g2_ref)


def _mlp_kernel(*refs, tiles, gpts):
    per_phase = 7
    gn_ref, wo_ref, wu_ref, wd_ref = refs[len(tiles) * per_phase:len(tiles) * per_phase + 4]
    y_refs = refs[len(tiles) * per_phase + 4:]
    i = pl.program_id(0)
    first = 0
    for p, (n_tiles, gpt) in enumerate(zip(tiles, gpts)):
        ins = refs[p * per_phase:(p + 1) * per_phase]

        @pl.when(jnp.logical_and(i >= first, i < first + n_tiles))
        def _(ins=ins, p=p, gpt=gpt):
            _mlp_tile(*ins, gn_ref, wo_ref, wu_ref, wd_ref, y_refs[p], gpt=gpt)

        first += n_tiles


def _mlp(phases, w, *, tm):
    d = phases[0][0].shape[1]
    tiles = [ph[0].shape[0] // tm for ph in phases]
    gpts = [max(1, tm // ph[4]) for ph in phases]
    in_specs, args, first = [], [], 0
    for (x2, a_out, b_out, mod4, rpg), n_tiles, gpt in zip(phases, tiles, gpts):
        tpg = max(1, rpg // tm)

        def tile(i, first=first, n_tiles=n_tiles):
            return jnp.clip(i - first, 0, n_tiles - 1)

        def rows(c, tile=tile):
            return pl.BlockSpec((tm, c), lambda i: (tile(i), 0))

        def mod_spec(j, tile=tile, gpt=gpt, tpg=tpg):
            return pl.BlockSpec((gpt, None, 1, d), lambda i: (tile(i) // tpg, j, 0, 0))

        in_specs += [rows(d), rows(MLA_WIDTH), rows(GLA_WIDTH), mod_spec(2), mod_spec(3), mod_spec(4), mod_spec(5)]
        args += [x2, a_out, b_out, mod4, mod4, mod4, mod4]
        first += n_tiles
    out_specs, first = [], 0
    for n_tiles in tiles:
        out_specs.append(pl.BlockSpec((tm, d), lambda i, first=first, n_tiles=n_tiles:
                                      (jnp.clip(i - first, 0, n_tiles - 1), 0)))
        first += n_tiles
    return pl.pallas_call(
        functools.partial(_mlp_kernel, tiles=tuple(tiles), gpts=tuple(gpts)),
        grid=(sum(tiles),),
        in_specs=in_specs + [_const((1, d)), _const(w["w_out"].shape), _const(w["w_up"].shape),
                             _const(w["w_down"].shape)],
        out_specs=out_specs,
        out_shape=[jax.ShapeDtypeStruct(ph[0].shape, F32) for ph in phases],
        compiler_params=_params("arbitrary"),
        name="out_proj_mlp",
    )(*args, w["g_norm2"], w["w_out"], w["w_up"], w["w_down"])


def _rope_table(start, count, repeat=1):
    half = QK_ROPE // 2
    inv = ROPE_THETA ** (-np.arange(half, dtype=np.float64) / half)
    ang = (start + np.arange(count, dtype=np.float64))[:, None] * inv[None, :]
    c, s, z = np.cos(ang), np.sin(ang), np.zeros_like(ang)
    tab = np.concatenate([c, c, z, z, -s, z, z, z, z, s, z, z], axis=1).astype(np.float32)
    return jnp.asarray(np.tile(tab, (repeat, 1)))


def _pad_gain(g_rope):
    return jnp.concatenate([g_rope, jnp.zeros((LANES - QK_ROPE,), F32)]).reshape(1, LANES)


def _relayout_kernel(wint_ref, wuq_ref, wukv_ref, wg_ref, win_o, wuq_o, wukv_o, wukt_o, wuv_o, wg_o):
    s = np.cumsum([0, Q_LORA, KV_LORA, QK_ROPE, GLA_QK, GLA_QK, GLA_WIDTH, GLA_GATE_RANK, GLA_WIDTH])

    def piece(i):
        return wint_ref[int(s[i]):int(s[i + 1]), :]

    d = wint_ref.shape[1]
    zeros = jnp.zeros((LANES - QK_ROPE - GLA_GATE_RANK, d), F32)
    win_o[:, 0:Q_LORA] = piece(0).T.astype(BF16)
    win_o[:, Q_LORA:C_QKR[1]] = jnp.concatenate([piece(2), piece(6), zeros], axis=0).T.astype(BF16)
    win_o[:, C_KV[0]:C_KV[1]] = piece(1).T.astype(BF16)
    win_o[:, C_GQ[0]:C_GQ[1]] = (piece(3).T * (GLA_DK ** -0.5)).astype(BF16)
    win_o[:, C_GK[0]:C_GK[1]] = piece(4).T.astype(BF16)
    win_o[:, C_GV[0]:C_GV[1]] = piece(5).T.astype(BF16)
    win_o[:, C_GR[0]:C_GR[1]] = piece(7).T.astype(BF16)

    zq = jnp.zeros((Q_LORA, HEAD_PAD - QK_DIM), BF16)
    kvw = QK_NOPE + V_DIM
    for hh in range(MLA_HEADS):
        wuq_o[:, hh * HEAD_PAD:hh * HEAD_PAD + QK_DIM] = wuq_ref[:, hh * QK_DIM:(hh + 1) * QK_DIM].astype(BF16)
        wuq_o[:, hh * HEAD_PAD + QK_DIM:(hh + 1) * HEAD_PAD] = zq
        uk = wukv_ref[:, hh * kvw:hh * kvw + QK_NOPE]
        uv = wukv_ref[:, hh * kvw + QK_NOPE:(hh + 1) * kvw].astype(BF16)
        wukv_o[:, hh * QK_NOPE:(hh + 1) * QK_NOPE] = uk.astype(BF16)
        wukv_o[:, (MLA_HEADS + hh) * V_DIM:(MLA_HEADS + hh + 1) * V_DIM] = uv
        wukt_o[hh * QK_NOPE:(hh + 1) * QK_NOPE, :] = uk.T.astype(BF16)
        wuv_o[:, hh * V_DIM:(hh + 1) * V_DIM] = uv

    wg_o[...] = jnp.zeros(wg_o.shape, BF16)
    wg_o[QK_ROPE:QK_ROPE + GLA_GATE_RANK, :] = wg_ref[...].astype(BF16)


def _relayout(w_in_t, w_uq, w_ukv, w_gate_up):
    d = w_in_t.shape[1]
    shapes = [(d, C_GR[1]), (Q_LORA, MLA_HEADS * HEAD_PAD), (KV_LORA, MLA_HEADS * (QK_NOPE + V_DIM)),
              (MLA_HEADS * QK_NOPE, KV_LORA), (KV_LORA, MLA_WIDTH), (LANES, GLA_QK)]
    return pl.pallas_call(
        _relayout_kernel,
        out_shape=[jax.ShapeDtypeStruct(sh, BF16) for sh in shapes],
        compiler_params=pltpu.CompilerParams(vmem_limit_bytes=VMEM_LIMIT),
        name="weight_relayout",
    )(w_in_t, w_uq, w_ukv, w_gate_up)


def _prep_weights(w_in, g_norm1, g_q_lora, w_uq, g_kv_lora, w_ukv, g_q_head, g_k_head,
                  w_gate_up, b_gate_up, g_gla_out, w_out, g_norm2, w_up, w_down):
    d = w_in.shape[0]
    w_in_p, w_uq_p, w_ukv_p, w_ukt, w_uv, w_gate = _relayout(w_in.T, w_uq, w_ukv, w_gate_up)
    qscale = QK_DIM ** -0.5 * LOG2E
    bound = 1.02 * QK_DIM ** 0.5 * LOG2E * jnp.max(jnp.abs(g_q_head)) * jnp.max(jnp.abs(g_k_head))
    lane = jnp.arange(LANES) == QK_ROPE
    return {
        "w_in": w_in_p, "g_norm1": g_norm1.reshape(1, d), "g_q_lora": g_q_lora.reshape(1, Q_LORA),
        "w_uq": w_uq_p, "g_kv_lora": g_kv_lora.reshape(1, KV_LORA), "w_ukv": w_ukv_p, "w_ukt": w_ukt, "w_uv": w_uv,
        "gkr_col": g_k_head[QK_NOPE:].reshape(QK_ROPE, 1),
        "qone": lane.astype(F32).reshape(1, LANES), "kbias": jnp.where(lane, -bound, 0.0).reshape(1, LANES),
        "fast_softmax": (bound <= MAX_FIXED_SHIFT).astype(jnp.int32).reshape(1), "bound": bound.reshape(1),
        "gqn": (g_q_head[:QK_NOPE] * qscale).reshape(1, LANES), "gqr": _pad_gain(g_q_head[QK_NOPE:] * qscale),
        "gkn": g_k_head[:QK_NOPE].reshape(1, LANES), "gkr": _pad_gain(g_k_head[QK_NOPE:]),
        "w_gate": w_gate, "b_gate": b_gate_up.reshape(1, GLA_QK),
        "g_gla_out": g_gla_out.reshape(1, GLA_WIDTH), "w_out": w_out.astype(BF16),
        "g_norm2": g_norm2.reshape(1, d), "w_up": w_up.astype(BF16), "w_down": w_down.astype(BF16),
    }


def _mixers(x, mod, past_lat, past_kr, s0, w, *, tm):
    batch, seq, d = x.shape
    n = batch * seq
    past = 0 if past_lat is None else past_lat.shape[1]
    x2 = x.reshape(n, d)
    mod4 = mod.reshape(batch, 6, 1, d)
    tm = min(tm, n)
    tab = _rope_table(past, seq, repeat=max(1, tm // seq))
    if past == 0:
        lat, krt, q, k, v, gq, gk, gv, la, gr = _projection(x2, mod4, tab, w, rows_per_group=seq, tm=tm, prompt=True)
        kr = jnp.swapaxes(krt, 1, 2)
        a_out = _attention_prompt(w["fast_softmax"], q, k, v, batch=batch, seq=seq, tq=min(ATTN_TILE, seq))
    else:
        assert seq == CHUNK and past % CHUNK == 0
        lat, kr, krt, q, gq, gk, gv, la, gr = _projection(x2, mod4, tab, w, rows_per_group=seq, tm=tm, prompt=False)
        a_out = _attention_sample(q, past_lat, jnp.swapaxes(past_kr, 1, 2), lat, krt, w, tkb=min(CACHE_BLOCK, past))
    b_out, s_new = _gla(gq, gk, gv, la, gr, s0, w["g_gla_out"], groups=batch, rows_per_group=seq,
                        tc=tm)
    return (x2, a_out, b_out, mod4, seq), (lat.reshape(batch, seq, KV_LORA), kr.reshape(batch, seq, QK_ROPE), s_new)


def kernel(x_prompt, x_sample, cache_mla_latent, cache_mla_krope, state_gla, c_prompt, c_sample,
           w_ada, b_ada, g_norm1, w_in, g_q_lora, w_uq, g_kv_lora, w_ukv, g_q_head, g_k_head,
           w_gate_up, b_gate_up, g_gla_out, w_out, g_norm2, w_up, w_down):
    nb = x_prompt.shape[0]
    depth = w_ada.shape[0]
    y_p, y_s = x_prompt, x_sample
    outs = [[] for _ in range(6)]
    for l in range(depth):
        w = _prep_weights(w_in[l], g_norm1[l], g_q_lora[l], w_uq[l], g_kv_lora[l], w_ukv[l], g_q_head[l],
                          g_k_head[l], w_gate_up[l], b_gate_up[l], g_gla_out[l], w_out[l], g_norm2[l],
                          w_up[l], w_down[l])
        mod_p, mod_s = _modulation(c_prompt, c_sample, w_ada[l], b_ada[l])
        zero_state = jnp.zeros((nb, GLA_HEADS, GLA_DK, GLA_DV), x_prompt.dtype)
        mix_p, new_p = _mixers(y_p, mod_p, None, None, zero_state, w, tm=ROW_TILE)
        mix_s, new_s = _mixers(y_s, mod_s, cache_mla_latent[l], cache_mla_krope[l], state_gla[l], w, tm=ROW_TILE)
        y2_p, y2_s = _mlp([mix_p, mix_s], w, tm=ROW_TILE)
        y_p, y_s = y2_p.reshape(y_p.shape), y2_s.reshape(y_s.shape)
        for o, new in zip(outs, new_p + new_s):
            o.append(new)
    return (y_p, y_s) + tuple(jnp.stack(o) for o in outs)
```

```python
import functools

import jax
import jax.numpy as jnp
import numpy as np
from jax import lax
from jax.experimental import pallas as pl
from jax.experimental.pallas import tpu as pltpu

F32 = jnp.float32
BF16 = jnp.bfloat16

CHUNK = 64
EPS = 1e-6
MLA_HEADS = 4
Q_LORA = 384
KV_LORA = 256
QK_NOPE = 128
QK_ROPE = 64
QK_DIM = QK_NOPE + QK_ROPE
V_DIM = 128
ROPE_THETA = 10000.0
GLA_HEADS = 4
GLA_DK = 64
GLA_DV = 128
GLA_GATE_RANK = 16
GLA_TAU = 16.0
GLA_QK = GLA_HEADS * GLA_DK
GLA_WIDTH = GLA_HEADS * GLA_DV
MLA_WIDTH = MLA_HEADS * V_DIM
HEAD_PAD = 256
SUB = 8
LOG2E = 1.4426950408889634
MAX_FIXED_SHIFT = 48.0

LANES = 128
VMEM_LIMIT = 56 * 1024 * 1024
ROW_TILE = 512
ATTN_TILE = 512
CACHE_BLOCK = 512
FF_SLICE = 1024
MOD_TILE = 1024

C_QKR = (0, 512)
C_KV = (512, 768)
C_GQ = (768, 1024)
C_GK = (1024, 1280)
C_GV = (1280, 1792)
C_GR = (1792, 2304)


def _dot(a, b):
    return jnp.dot(a, b, preferred_element_type=F32)


def _dot_nt(a, b):
    return lax.dot_general(a, b, (((1,), (1,)), ((), ())), preferred_element_type=F32)


def _dot_tn(a, b):
    return lax.dot_general(a, b, (((0,), (0,)), ((), ())), preferred_element_type=F32)


def _rope_tile(t, c, sa, sb):
    return t * c + pltpu.roll(t, 96, 1) * sa + pltpu.roll(t, 32, 1) * sb


def _params(*sem):
    return pltpu.CompilerParams(dimension_semantics=sem, vmem_limit_bytes=VMEM_LIMIT)


def _const(shape):
    return pl.BlockSpec(shape, lambda *_: (0,) * len(shape), pipeline_mode=pl.Buffered(1))


def _mod_kernel(cs_ref, cp_ref, w_ref, b_ref, os_ref, op_ref):
    ns = cs_ref.shape[0]
    c = jnp.concatenate([cs_ref[...], cp_ref[...]], axis=0)
    s = (c * jax.nn.sigmoid(c)).astype(BF16)
    mod = _dot(s, w_ref[...].astype(BF16)) + b_ref[...]
    os_ref[...] = mod[0:ns]
    op_ref[...] = mod[ns:]


def _modulation(c_prompt, c_sample, w_ada, b_ada):
    d, n = w_ada.shape
    tn = MOD_TILE

    def whole(c):
        return pl.BlockSpec(c.shape, lambda j: (0, 0))

    def cols(c):
        return pl.BlockSpec((c.shape[0], tn), lambda j: (0, j))

    mod_s, mod_p = pl.pallas_call(
        _mod_kernel,
        grid=(n // tn,),
        in_specs=[whole(c_sample), whole(c_prompt),
                  pl.BlockSpec((d, tn), lambda j: (0, j)),
                  pl.BlockSpec((1, tn), lambda j: (0, j))],
        out_specs=[cols(c_sample), cols(c_prompt)],
        out_shape=[jax.ShapeDtypeStruct((c_sample.shape[0], n), F32), jax.ShapeDtypeStruct((c_prompt.shape[0], n), F32)],
        compiler_params=_params("arbitrary"),
        name="adaln_mod",
    )(c_sample, c_prompt, w_ada, b_ada.reshape(1, n))
    return mod_p, mod_s


def _proj_kernel(x_ref, sh_ref, sc_ref, g1_ref, win_ref, gql_ref, wuq_ref, gkv_ref, wukv_ref,
                 gqn_ref, gqr_ref, gkn_ref, gkr_ref, qone_ref, kbias_ref, tab_ref, wg_ref, bg_ref,
                 *outs, gpt, prompt):
    if prompt:
        lat_ref, krt_ref, q_ref, k_ref, v_ref, gq_ref, gk_ref, gv_ref, la_ref, gr_ref = outs
    else:
        lat_ref, kr_ref, krt_ref, q_ref, gq_ref, gk_ref, gv_ref, la_ref, gr_ref = outs
    tm, d = x_ref.shape
    x = x_ref[...]
    xn = x * lax.rsqrt(jnp.mean(x * x, axis=-1, keepdims=True) + EPS)
    h = (xn.reshape(gpt, tm // gpt, d) * (g1_ref[...] * (1.0 + sc_ref[...])) + sh_ref[...]).reshape(tm, d)
    hb = h.astype(BF16)

    def col(c):
        return _dot(hb, win_ref[:, c[0]:c[1]])

    tab = tab_ref[...]
    cos, sa, sb = tab[:, 0:LANES], tab[:, LANES:2 * LANES], tab[:, 2 * LANES:3 * LANES]

    qkr = col(C_QKR)
    krg = qkr[:, Q_LORA:]
    z = _dot(krg.astype(BF16), wg_ref[...]) + bg_ref[...]
    la_ref[...] = (jnp.minimum(z, 0.0) - jnp.log(1.0 + jnp.exp(-jnp.abs(z)))) * (1.0 / GLA_TAU)
    r = col(C_GR)
    gr_ref[...] = (r * jax.nn.sigmoid(r)).astype(BF16)

    cq = qkr[:, 0:Q_LORA]
    cqn = cq * lax.rsqrt(jnp.mean(cq * cq, axis=-1, keepdims=True) + EPS) * gql_ref[...]
    qp = _dot(cqn.astype(BF16), wuq_ref[...])
    for hh in range(MLA_HEADS):
        nope = qp[:, hh * HEAD_PAD:hh * HEAD_PAD + QK_NOPE]
        rt = qp[:, hh * HEAD_PAD + QK_NOPE:(hh + 1) * HEAD_PAD]
        ss = jnp.sum(nope * nope, axis=-1, keepdims=True) + jnp.sum(rt * rt, axis=-1, keepdims=True)
        inv = lax.rsqrt(ss * (1.0 / QK_DIM) + EPS)
        q_ref[:, hh * HEAD_PAD:hh * HEAD_PAD + QK_NOPE] = (nope * inv * gqn_ref[...]).astype(BF16)
        rq = _rope_tile(rt * inv * gqr_ref[...], cos, sa, sb)
        q_ref[:, hh * HEAD_PAD + QK_NOPE:(hh + 1) * HEAD_PAD] = (rq + qone_ref[...]).astype(BF16)

    ckv = col(C_KV)
    lat = ckv * lax.rsqrt(jnp.mean(ckv * ckv, axis=-1, keepdims=True) + EPS) * gkv_ref[...]
    lat_ref[...] = lat
    krt_ref[...] = krg.T[0:QK_ROPE, :]
    if prompt:
        lane = lax.broadcasted_iota(jnp.int32, (1, LANES), 1)
        krm = jnp.where(lane < QK_ROPE, krg, 0.0)
        ssr = jnp.sum(krm * krm, axis=-1, keepdims=True)
        rk = _rope_tile(krm * gkr_ref[...], cos, sa, sb)
        kv = _dot(lat.astype(BF16), wukv_ref[...])
        for hh in range(MLA_HEADS):
            kn = kv[:, hh * QK_NOPE:(hh + 1) * QK_NOPE]
            inv = lax.rsqrt((jnp.sum(kn * kn, axis=-1, keepdims=True) + ssr) * (1.0 / QK_DIM) + EPS)
            k_ref[:, hh * HEAD_PAD:hh * HEAD_PAD + QK_NOPE] = (kn * inv * gkn_ref[...]).astype(BF16)
            k_ref[:, hh * HEAD_PAD + QK_NOPE:(hh + 1) * HEAD_PAD] = (rk * inv + kbias_ref[...]).astype(BF16)
        v_ref[...] = kv[:, MLA_HEADS * QK_NOPE:].astype(BF16)
    else:
        kr_ref[...] = krg[:, 0:QK_ROPE]

    gq_ref[...] = col(C_GQ)
    gk_ref[...] = col(C_GK)
    gv_ref[...] = col(C_GV).astype(BF16)


def _projection(x2, mod4, tab, w, *, rows_per_group, tm, prompt):
    n, d = x2.shape
    gpt = max(1, tm // rows_per_group)
    tpg = max(1, rows_per_group // tm)
    ntab = tab.shape[0] // tm

    def mod_spec(j):
        return pl.BlockSpec((gpt, None, 1, d), lambda i: ((i // tpg) if gpt == 1 else i, j, 0, 0))

    def rows(c):
        return pl.BlockSpec((tm, c), lambda i: (i, 0))

    def out(c, t):
        return rows(c), jax.ShapeDtypeStruct((n, c), t)

    gla_outs = [out(GLA_QK, F32), out(GLA_QK, F32), out(GLA_WIDTH, BF16), out(GLA_QK, F32), out(GLA_WIDTH, BF16)]
    qo = out(MLA_HEADS * HEAD_PAD, BF16)
    if prompt:
        krt = (pl.BlockSpec((None, QK_ROPE, tm), lambda i: (i // tpg, 0, i % tpg)),
               jax.ShapeDtypeStruct((n // rows_per_group, QK_ROPE, rows_per_group), F32))
        outs = [out(KV_LORA, F32), krt, qo, out(MLA_HEADS * HEAD_PAD, BF16), out(MLA_WIDTH, BF16)] + gla_outs
    else:
        krt = (pl.BlockSpec((QK_ROPE, tm), lambda i: (0, i)), jax.ShapeDtypeStruct((QK_ROPE, n), F32))
        outs = [out(KV_LORA, F32), out(QK_ROPE, F32), krt, qo] + gla_outs
    return pl.pallas_call(
        functools.partial(_proj_kernel, gpt=gpt, prompt=prompt),
        grid=(n // tm,),
        in_specs=[rows(d), mod_spec(0), mod_spec(1), _const((1, d)), _const(w["w_in"].shape),
                  _const((1, Q_LORA)), _const(w["w_uq"].shape), _const((1, KV_LORA)), _const(w["w_ukv"].shape),
                  _const((1, LANES)), _const((1, LANES)), _const((1, LANES)), _const((1, LANES)),
                  _const((1, LANES)), _const((1, LANES)),
                  pl.BlockSpec((tm, 3 * LANES), lambda i: (i % ntab, 0)),
                  _const(w["w_gate"].shape), _const((1, GLA_QK))],
        out_specs=[o[0] for o in outs],
        out_shape=[o[1] for o in outs],
        compiler_params=_params("arbitrary"),
        name="in_proj",
    )(x2, mod4, mod4, w["g_norm1"], w["w_in"], w["g_q_lora"], w["w_uq"], w["g_kv_lora"], w["w_ukv"],
      w["gqn"], w["gqr"], w["gkn"], w["gkr"], w["qone"], w["kbias"], tab, w["w_gate"], w["b_gate"])


def _attn_prompt_kernel(fast_ref, q_ref, k_ref, v_ref, o_ref, vx_ref, m_ref, acc_ref, *, tq):
    i = pl.program_id(1)
    heads = range(MLA_HEADS)

    @pl.when(i == 0)
    def _():
        for hh in heads:
            vx_ref[hh, :, 0:V_DIM] = v_ref[:, hh * V_DIM:(hh + 1) * V_DIM]
            vx_ref[hh, :, V_DIM:] = jnp.ones((v_ref.shape[0], V_DIM), BF16)

    def scores(j, hh, masked):
        hs = slice(hh * HEAD_PAD, (hh + 1) * HEAD_PAD)
        s = _dot_nt(q_ref[:, hs], k_ref[pl.ds(pl.multiple_of(j * tq, tq), tq), hs])
        if masked:
            qc = lax.broadcasted_iota(jnp.int32, (tq, tq), 0) // CHUNK
            kc = lax.broadcasted_iota(jnp.int32, (tq, tq), 1) // CHUNK
            s = jnp.where(kc <= qc, s, -jnp.inf)
        return s

    def fast_diagonal():
        hq = tq // 2
        k0 = pl.multiple_of(i * tq, tq)
        qc = lax.broadcasted_iota(jnp.int32, (hq, hq), 0) // CHUNK
        kc = lax.broadcasted_iota(jnp.int32, (hq, hq), 1) // CHUNK
        tri = kc <= qc
        for hh in heads:
            hs = slice(hh * HEAD_PAD, (hh + 1) * HEAD_PAD)
            top = jnp.where(tri, _dot_nt(q_ref[0:hq, hs], k_ref[pl.ds(k0, hq), hs]), -jnp.inf)
            acc_ref[hh, 0:hq] = _dot(jnp.exp2(top).astype(BF16), vx_ref[hh, pl.ds(k0, hq), :])
            bot = _dot_nt(q_ref[hq:tq, hs], k_ref[pl.ds(k0, tq), hs])
            bot = jnp.concatenate([bot[:, 0:hq], jnp.where(tri, bot[:, hq:tq], -jnp.inf)], axis=1)
            acc_ref[hh, hq:tq] = _dot(jnp.exp2(bot).astype(BF16), vx_ref[hh, pl.ds(k0, tq), :])

    def vblock(j, hh):
        return vx_ref[hh, pl.ds(pl.multiple_of(j * tq, tq), tq), :]

    def fast_block(j, masked):
        for hh in heads:
            acc_ref[hh] += _dot(jnp.exp2(scores(j, hh, masked)).astype(BF16), vblock(j, hh))

    def safe_block(j, masked):
        for hh in heads:
            s = scores(j, hh, masked)
            m = m_ref[hh]
            m_new = jnp.maximum(m, jnp.max(s, axis=1, keepdims=True))
            p = jnp.exp2(s - m_new).astype(BF16)
            acc_ref[hh] = jnp.exp2(m - m_new) * acc_ref[hh] + _dot(p, vblock(j, hh))
            m_ref[hh] = m_new

    def sweep(block):
        def body(jj, c):
            for u in range(4):
                block(4 * jj + u, False)
            return c

        lax.fori_loop(0, i // 4, body, 0)

        @pl.when(i % 4 >= 2)
        def _():
            block((i // 4) * 4, False)
            block((i // 4) * 4 + 1, False)

        @pl.when(i % 2 == 1)
        def _():
            block(i - 1, False)

    @pl.when(fast_ref[0] == 1)
    def _():
        fast_diagonal()
        sweep(fast_block)

    @pl.when(fast_ref[0] != 1)
    def _():
        m_ref[...] = jnp.full(m_ref.shape, -jnp.inf, F32)
        acc_ref[...] = jnp.zeros(acc_ref.shape, F32)
        safe_block(i, True)
        sweep(safe_block)

    for hh in heads:
        acc = acc_ref[hh]
        o_ref[:, hh * V_DIM:(hh + 1) * V_DIM] = (acc[:, 0:V_DIM] / acc[:, V_DIM:]).astype(BF16)


def _attention_prompt(fast, q, k, v, *, batch, seq, tq):
    nq = seq // tq
    return pl.pallas_call(
        functools.partial(_attn_prompt_kernel, tq=tq),
        grid=(batch, nq),
        in_specs=[pl.BlockSpec(memory_space=pltpu.SMEM),
                  pl.BlockSpec((tq, MLA_HEADS * HEAD_PAD), lambda b, i: (b * nq + i, 0)),
                  pl.BlockSpec((seq, MLA_HEADS * HEAD_PAD), lambda b, i: (b, 0)),
                  pl.BlockSpec((seq, MLA_WIDTH), lambda b, i: (b, 0))],
        out_specs=pl.BlockSpec((tq, MLA_WIDTH), lambda b, i: (b * nq + i, 0)),
        out_shape=jax.ShapeDtypeStruct((batch * seq, MLA_WIDTH), BF16),
        scratch_shapes=[pltpu.VMEM((MLA_HEADS, seq, 2 * V_DIM), BF16), pltpu.VMEM((MLA_HEADS, tq, 1), F32),
                        pltpu.VMEM((MLA_HEADS, tq, 2 * V_DIM), F32)],
        compiler_params=_params("arbitrary", "arbitrary"),
        name="mla_attn_prompt",
    )(fast, q, k, v)


def _attn_sample_kernel(fast_ref, bound_ref, q_ref, lat_ref, krt_ref, latn_ref, krtn_ref, tab_ref, wukt_ref, wuv_ref,
                        gkn_ref, gkr_ref, o_ref, wq_ref, latb_ref, s_ref, *, tkb):
    t = q_ref.shape[0] // 2
    past = lat_ref.shape[1]
    nk = MLA_HEADS * QK_NOPE
    half = QK_ROPE // 2
    streams = range(2)

    @pl.when(pl.program_id(0) == 0)
    def _():
        for e in streams:
            wq_ref[e, 0:nk, :] = wukt_ref[...]

    qr = []
    for e in streams:
        rows = slice(e * t, (e + 1) * t)
        parts = []
        for hh in range(MLA_HEADS):
            qn = (q_ref[rows, hh * HEAD_PAD:hh * HEAD_PAD + QK_NOPE].astype(F32) * gkn_ref[...]).astype(BF16)
            wq_ref[e, nk + hh * t:nk + (hh + 1) * t, :] = _dot(
                qn, wukt_ref[hh * QK_NOPE:(hh + 1) * QK_NOPE, :]).astype(BF16)
            parts.append(q_ref[rows, hh * HEAD_PAD + QK_NOPE:(hh + 1) * HEAD_PAD])
        qr.append(jnp.concatenate(parts, axis=0))

    def key_block(e, lat, krt, tab, width, valid):
        latb = lat.astype(BF16)
        g = _dot_nt(wq_ref[e], latb)
        ssr = jnp.sum(krt * krt, axis=0, keepdims=True)
        kg = krt * gkr_ref[...]
        x1, x2 = kg[0:half], kg[half:QK_ROPE]
        c, sn = tab[0:half], tab[half:QK_ROPE]
        rope = jnp.concatenate([x1 * c - x2 * sn, x2 * c + x1 * sn, jnp.zeros((LANES - QK_ROPE, width), F32)], axis=0)
        srope = _dot(qr[e], rope.astype(BF16))
        rows = []
        for hh in range(MLA_HEADS):
            kn = g[hh * QK_NOPE:(hh + 1) * QK_NOPE]
            inv = lax.rsqrt((jnp.sum(kn * kn, axis=0, keepdims=True) + ssr) * (1.0 / QK_DIM) + EPS)
            rows.append((g[nk + hh * t:nk + (hh + 1) * t] + srope[hh * t:(hh + 1) * t]) * inv)
        sc = jnp.concatenate(rows, axis=0) - bound_ref[0]
        if valid is not None:
            sc = jnp.where(valid, sc, -jnp.inf)
        return sc, latb

    def blocks():
        for blk in range(past // tkb):
            c0 = blk * tkb
            for e in streams:
                yield e, c0, tkb, key_block(e, lat_ref[e, c0:c0 + tkb, :], krt_ref[e, :, c0:c0 + tkb],
                                            tab_ref[:, c0:c0 + tkb], tkb, None)
        for e in streams:
            mine = lax.broadcasted_iota(jnp.int32, (1, 2 * t), 1) // t == e
            yield e, past, 2 * t, key_block(e, latn_ref[...], krtn_ref[...], tab_ref[:, past:past + 2 * t], 2 * t, mine)

    def finish(e, wlat, l):
        wlat = wlat.astype(BF16)
        for hh in range(MLA_HEADS):
            o = _dot(wlat[hh * t:(hh + 1) * t], wuv_ref[:, hh * V_DIM:(hh + 1) * V_DIM])
            o_ref[e * t:(e + 1) * t, hh * V_DIM:(hh + 1) * V_DIM] = (o / l[hh * t:(hh + 1) * t]).astype(BF16)

    @pl.when(fast_ref[0] == 1)
    def _():
        wlat = [jnp.zeros((MLA_HEADS * t, KV_LORA), F32) for _ in streams]
        lsum = [jnp.zeros((MLA_HEADS * t, LANES), F32) for _ in streams]
        for e, _, width, (sc, latb) in blocks():
            p = jnp.exp2(sc)
            wlat[e] += _dot(p.astype(BF16), latb)
            for c in range(width // LANES):
                lsum[e] += p[:, c * LANES:(c + 1) * LANES]
        for e in streams:
            finish(e, wlat[e], jnp.sum(lsum[e], axis=1, keepdims=True))

    @pl.when(fast_ref[0] != 1)
    def _():
        for e, c0, width, (sc, latb) in blocks():
            s_ref[e, :, c0:c0 + width] = sc
            latb_ref[e, c0:c0 + width, :] = latb
        for e in streams:
            s = s_ref[e]
            p = jnp.exp2(s - jnp.max(s, axis=1, keepdims=True))
            finish(e, _dot(p.astype(BF16), latb_ref[e]), jnp.sum(p, axis=1, keepdims=True))


def _attention_sample(q, past_lat, past_krt, lat_new, krt_new, w, *, tkb):
    batch, past, _ = past_lat.shape
    t = q.shape[0] // batch
    half = QK_ROPE // 2
    inv = ROPE_THETA ** (-np.arange(half, dtype=np.float64) / half)
    pos = np.concatenate([np.arange(past), past + np.arange(t), past + np.arange(t)]).astype(np.float64)
    ang = inv[:, None] * pos[None, :]
    tab = jnp.asarray(np.concatenate([np.cos(ang), np.sin(ang)], axis=0).astype(np.float32))
    s_pad = past + 2 * t
    return pl.pallas_call(
        functools.partial(_attn_sample_kernel, tkb=tkb),
        grid=(batch // 2,),
        in_specs=[pl.BlockSpec(memory_space=pltpu.SMEM), pl.BlockSpec(memory_space=pltpu.SMEM),
                  pl.BlockSpec((2 * t, MLA_HEADS * HEAD_PAD), lambda g: (g, 0)),
                  pl.BlockSpec((2, past, KV_LORA), lambda g: (g, 0, 0)),
                  pl.BlockSpec((2, QK_ROPE, past), lambda g: (g, 0, 0)),
                  pl.BlockSpec((2 * t, KV_LORA), lambda g: (g, 0)),
                  pl.BlockSpec((QK_ROPE, 2 * t), lambda g: (0, g)),
                  _const(tab.shape), _const(w["w_ukt"].shape), _const(w["w_uv"].shape),
                  _const((1, LANES)), _const((QK_ROPE, 1))],
        out_specs=pl.BlockSpec((2 * t, MLA_WIDTH), lambda g: (g, 0)),
        out_shape=jax.ShapeDtypeStruct((batch * t, MLA_WIDTH), BF16),
        scratch_shapes=[pltpu.VMEM((2, MLA_HEADS * (QK_NOPE + t), KV_LORA), BF16),
                        pltpu.VMEM((2, s_pad, KV_LORA), BF16),
                        pltpu.VMEM((2, MLA_HEADS * t, s_pad), F32)],
        compiler_params=_params("arbitrary"),
        name="mla_attn_sample",
    )(w["fast_softmax"], w["bound"], q, past_lat, past_krt, lat_new, krt_new, tab, w["w_ukt"], w["w_uv"], w["gkn"],
      w["gkr_col"])


def _gla_kernel(q_ref, k_ref, v_ref, la_ref, r_ref, s0_ref, g_ref, spread_ref, o_ref, sn_ref,
                st_ref, kp_ref, ap_ref, p_ref, on_ref, *, gpt):
    t_idx = pl.program_id(1)
    L, W, P = CHUNK, GLA_QK, LANES
    R = q_ref.shape[0]
    n_chunks = R // L
    cpg = n_chunks // gpt
    n_pairs = GLA_HEADS // 2

    lane_p = lax.broadcasted_iota(jnp.int32, (1, P), 1)
    even = lane_p < GLA_DK
    bd_mask = (lax.broadcasted_iota(jnp.int32, (2 * GLA_DV, P), 0) // GLA_DV
               == lax.broadcasted_iota(jnp.int32, (2 * GLA_DV, P), 1) // GLA_DK)

    @pl.when(t_idx == 0)
    def _():
        kp_ref[0:SUB, :] = jnp.zeros((SUB, W), F32)
        ap_ref[0:SUB, :] = jnp.zeros((SUB, W), F32)
        for gi in range(gpt):
            for pr in range(n_pairs):
                tt = s0_ref[gi, 2 * pr:2 * pr + 2].reshape(2 * GLA_DK, GLA_DV).T
                st_ref[gi, pr] = jnp.where(bd_mask, jnp.concatenate([tt, tt], axis=0), 0.0)

    q = q_ref[...]
    k = k_ref[...]
    la = la_ref[...]

    tri = (lax.broadcasted_iota(jnp.int32, (L, L), 0) >= lax.broadcasted_iota(jnp.int32, (L, L), 1)).astype(BF16)
    la_hi = la.astype(BF16)
    la2 = jnp.concatenate([la_hi, (la - la_hi.astype(F32)).astype(BF16)], axis=1)
    bs = []
    for c in range(n_chunks):
        t2 = _dot(tri, la2[c * L:(c + 1) * L, :])
        bs.append(t2[:, 0:W] + t2[:, W:2 * W])
    b = (jnp.concatenate(bs, axis=0) if n_chunks > 1 else bs[0]) * LOG2E
    b3 = b.reshape(n_chunks, L, W)

    def chunk_row(r):
        return jnp.broadcast_to(b3[:, r:r + 1, :], (n_chunks, L, W)).reshape(R, W)

    b_sub = jnp.broadcast_to(b.reshape(R // SUB, SUB, W)[:, 0:1, :], (R // SUB, SUB, W)).reshape(R, W)
    sub = (lax.broadcasted_iota(jnp.int32, (R, W), 0) % L) // SUB

    qt = q * jnp.exp2(b - b_sub)
    zb = jnp.zeros((), BF16)
    zq = jnp.zeros((SUB, P), F32)
    ktm = []
    for i in range(1, L // SUB):
        kt = (k * jnp.exp2(chunk_row(i * SUB) - b)).astype(BF16)
        ktm.append([[jnp.where((sub[:, 0:P] < i) & (even if e == 0 else ~even), kt[:, pr * P:(pr + 1) * P], zb)
                     for e in range(2)] for pr in range(n_pairs)])

    a = jnp.exp(la)
    kp_ref[SUB:SUB + R, :] = k
    ap_ref[SUB:SUB + R, :] = a
    p_ref[:, 0:W] = (q * k).astype(BF16)
    e = a
    for d in range(1, SUB):
        if d > 1:
            e = e * ap_ref[SUB - d + 1:SUB - d + 1 + R, :]
        p_ref[:, d * W:(d + 1) * W] = (q * kp_ref[SUB - d:SUB - d + R, :] * e).astype(BF16)
    cband = _dot(p_ref[...], spread_ref[...])
    same_sub = (lax.broadcasted_iota(jnp.int32, (L, W), 0) // SUB
                == (lax.broadcasted_iota(jnp.int32, (L, W), 1) % L) // SUB)

    qe = (q * jnp.exp2(b)).astype(BF16)
    kd = (k * jnp.exp2(chunk_row(L - 1) - b)).astype(BF16)
    zv = jnp.zeros((L, 2 * GLA_DV), BF16)

    o_intra, d_st, dec = {}, {}, []
    for c in range(n_chunks):
        rs = slice(c * L, (c + 1) * L)
        a_band = jnp.where(same_sub, pltpu.roll(cband[rs], W - (SUB - 1), 1, stride=1, stride_axis=0), 0.0)
        dec.append(jnp.exp2(b[c * L + L - 1:c * L + L, :]))
        for pr in range(n_pairs):
            ls = slice(pr * P, (pr + 1) * P)
            lhs_c = jnp.concatenate(
                [jnp.concatenate([qt[c * L + r * SUB:c * L + (r + 1) * SUB, ls] if i == r else zq
                                  for i in range(1, L // SUB)], axis=1) for r in range(L // SUB)],
                axis=0).astype(BF16)
            rhs_c = jnp.concatenate([jnp.concatenate([m[pr][0][rs], m[pr][1][rs]], axis=0) for m in ktm], axis=1)
            a_tot = (a_band[:, ls] + _dot_nt(lhs_c, rhs_c)).astype(BF16)
            vp = v_ref[rs, 2 * pr * GLA_DV:(2 * pr + 2) * GLA_DV]
            v_bd = jnp.concatenate([jnp.concatenate([vp[:, 0:GLA_DV], zv[:, 0:GLA_DV]], axis=1),
                                    jnp.concatenate([zv[:, 0:GLA_DV], vp[:, GLA_DV:]], axis=1)], axis=0)
            o_intra[c, pr] = _dot(a_tot, v_bd)
            d_st[c, pr] = jnp.where(bd_mask, _dot_tn(vp, kd[rs, ls]), 0.0)
    st_in = {}
    for gi in range(gpt):
        for pr in range(n_pairs):
            st = st_ref[gi, pr]
            for c in range(gi * cpg, (gi + 1) * cpg):
                st_in[c, pr] = st.astype(BF16)
                st = st * dec[c][:, pr * P:(pr + 1) * P] + d_st[c, pr]
            st_ref[gi, pr] = st
    for c in range(n_chunks):
        rs = slice(c * L, (c + 1) * L)
        for pr in range(n_pairs):
            on_ref[rs, 2 * pr * GLA_DV:(2 * pr + 2) * GLA_DV] = (
                o_intra[c, pr] + _dot_nt(qe[rs, pr * P:(pr + 1) * P], st_in[c, pr]))

    for hh in range(GLA_HEADS):
        hs = slice(hh * GLA_DV, (hh + 1) * GLA_DV)
        o = on_ref[:, hs]
        on = o * lax.rsqrt(jnp.mean(o * o, axis=-1, keepdims=True) + EPS) * g_ref[:, hs]
        o_ref[:, hs] = (on * r_ref[:, hs].astype(F32)).astype(BF16)

    @pl.when(t_idx == pl.num_programs(1) - 1)
    def _():
        for gi in range(gpt):
            for pr in range(n_pairs):
                st = st_ref[gi, pr]
                tt = jnp.where(even, st[0:GLA_DV], st[GLA_DV:2 * GLA_DV])
                sn_ref[gi, 2 * pr:2 * pr + 2] = tt.T.reshape(2, GLA_DK, GLA_DV)


def _band_spread():
    m = np.zeros((SUB, GLA_HEADS, GLA_DK, GLA_QK), np.float32)
    for d in range(SUB):
        for h in range(GLA_HEADS):
            m[d, h, :, h * GLA_DK + SUB - 1 - d] = 1.0
    return jnp.asarray(m.reshape(SUB * GLA_QK, GLA_QK), BF16)


def _gla(gq, gk, gv, la, gr, s0, g_out, *, groups, rows_per_group, tc):
    gpt = max(1, tc // rows_per_group)
    nt = max(1, rows_per_group // tc)

    def rows(c):
        return pl.BlockSpec((tc, c), lambda g, t: (g * nt + t, 0))

    state = pl.BlockSpec((gpt, GLA_HEADS, GLA_DK, GLA_DV), lambda g, t: (g, 0, 0, 0))
    spread = _band_spread()
    return pl.pallas_call(
        functools.partial(_gla_kernel, gpt=gpt),
        grid=(groups // gpt, nt),
        in_specs=[rows(GLA_QK), rows(GLA_QK), rows(GLA_WIDTH), rows(GLA_QK), rows(GLA_WIDTH), state,
                  _const((1, GLA_WIDTH)), _const(spread.shape)],
        out_specs=[rows(GLA_WIDTH), state],
        out_shape=[jax.ShapeDtypeStruct((groups * rows_per_group, GLA_WIDTH), BF16),
                   jax.ShapeDtypeStruct((groups, GLA_HEADS, GLA_DK, GLA_DV), F32)],
        scratch_shapes=[pltpu.VMEM((gpt, GLA_HEADS // 2, 2 * GLA_DV, LANES), F32),
                        pltpu.VMEM((SUB + tc, GLA_QK), F32), pltpu.VMEM((SUB + tc, GLA_QK), F32),
                        pltpu.VMEM((tc, SUB * GLA_QK), BF16), pltpu.VMEM((tc, GLA_WIDTH), F32)],
        compiler_params=_params("arbitrary", "arbitrary"),
        name="gla",
    )(gq, gk, gv, la, gr, s0, g_out, spread)


def _mlp_tile(x_ref, a_ref, b_ref, g1_ref, sh2_ref, sc2_ref, g2_ref, gn_ref, wo_ref, wu_ref, wd_ref, y_ref, *, gpt):
    tm, d = x_ref.shape

    def per_group(val, ref, scale_plus_one=False):
        m = ref[...]
        if scale_plus_one:
            m = 1.0 + m
        return (val.reshape(gpt, tm // gpt, d) * m).reshape(tm, d)

    mix = jnp.concatenate([a_ref[...], b_ref[...]], axis=1)
    x1 = x_ref[...] + per_group(_dot(mix, wo_ref[...]), g1_ref)
    xn = x1 * lax.rsqrt(jnp.mean(x1 * x1, axis=-1, keepdims=True) + EPS) * gn_ref[...]
    h2 = (per_group(xn, sc2_ref, True).reshape(gpt, tm // gpt, d) + sh2_ref[...]).reshape(tm, d).astype(BF16)
    acc = jnp.zeros((tm, d), F32)
    for j in range(wu_ref.shape[1] // FF_SLICE):
        u = jnp.maximum(_dot(h2, wu_ref[:, j * FF_SLICE:(j + 1) * FF_SLICE]), 0.0)
        acc += _dot((u * u).astype(BF16), wd_ref[j * FF_SLICE:(j + 1) * FF_SLICE, :])
    y_ref[...] = x1 + per_group(acc, g2_ref)


def _mlp_kernel(*refs, tiles, gpts):
    per_phase = 7
    gn_ref, wo_ref, wu_ref, wd_ref = refs[len(tiles) * per_phase:len(tiles) * per_phase + 4]
    y_refs = refs[len(tiles) * per_phase + 4:]
    i = pl.program_id(0)
    first = 0
    for p, (n_tiles, gpt) in enumerate(zip(tiles, gpts)):
        ins = refs[p * per_phase:(p + 1) * per_phase]

        @pl.when(jnp.logical_and(i >= first, i < first + n_tiles))
        def _(ins=ins, p=p, gpt=gpt):
            _mlp_tile(*ins, gn_ref, wo_ref, wu_ref, wd_ref, y_refs[p], gpt=gpt)

        first += n_tiles


def _mlp(phases, w, *, tm):
    d = phases[0][0].shape[1]
    tiles = [ph[0].shape[0] // tm for ph in phases]
    gpts = [max(1, tm // ph[4]) for ph in phases]
    in_specs, args, first = [], [], 0
    for (x2, a_out, b_out, mod4, rpg), n_tiles, gpt in zip(phases, tiles, gpts):
        tpg = max(1, rpg // tm)

        def tile(i, first=first, n_tiles=n_tiles):
            return jnp.clip(i - first, 0, n_tiles - 1)

        def rows(c, tile=tile):
            return pl.BlockSpec((tm, c), lambda i: (tile(i), 0))

        def mod_spec(j, tile=tile, gpt=gpt, tpg=tpg):
            return pl.BlockSpec((gpt, None, 1, d), lambda i: (tile(i) // tpg, j, 0, 0))

        in_specs += [rows(d), rows(MLA_WIDTH), rows(GLA_WIDTH), mod_spec(2), mod_spec(3), mod_spec(4), mod_spec(5)]
        args += [x2, a_out, b_out, mod4, mod4, mod4, mod4]
        first += n_tiles
    out_specs, first = [], 0
    for n_tiles in tiles:
        out_specs.append(pl.BlockSpec((tm, d), lambda i, first=first, n_tiles=n_tiles:
                                      (jnp.clip(i - first, 0, n_tiles - 1), 0)))
        first += n_tiles
    return pl.pallas_call(
        functools.partial(_mlp_kernel, tiles=tuple(tiles), gpts=tuple(gpts)),
        grid=(sum(tiles),),
        in_specs=in_specs + [_const((1, d)), _const(w["w_out"].shape), _const(w["w_up"].shape),
                             _const(w["w_down"].shape)],
        out_specs=out_specs,
        out_shape=[jax.ShapeDtypeStruct(ph[0].shape, F32) for ph in phases],
        compiler_params=_params("arbitrary"),
        name="out_proj_mlp",
    )(*args, w["g_norm2"], w["w_out"], w["w_up"], w["w_down"])


def _rope_table(start, count, repeat=1):
    half = QK_ROPE // 2
    inv = ROPE_THETA ** (-np.arange(half, dtype=np.float64) / half)
    ang = (start + np.arange(count, dtype=np.float64))[:, None] * inv[None, :]
    c, s, z = np.cos(ang), np.sin(ang), np.zeros_like(ang)
    tab = np.concatenate([c, c, z, z, -s, z, z, z, z, s, z, z], axis=1).astype(np.float32)
    return jnp.asarray(np.tile(tab, (repeat, 1)))


def _pad_gain(g_rope):
    return jnp.concatenate([g_rope, jnp.zeros((LANES - QK_ROPE,), F32)]).reshape(1, LANES)


def _relayout_kernel(wint_ref, wuq_ref, wukv_ref, wg_ref, win_o, wuq_o, wukv_o, wukt_o, wuv_o, wg_o):
    s = np.cumsum([0, Q_LORA, KV_LORA, QK_ROPE, GLA_QK, GLA_QK, GLA_WIDTH, GLA_GATE_RANK, GLA_WIDTH])

    def piece(i):
        return wint_ref[int(s[i]):int(s[i + 1]), :]

    d = wint_ref.shape[1]
    zeros = jnp.zeros((LANES - QK_ROPE - GLA_GATE_RANK, d), F32)
    win_o[:, 0:Q_LORA] = piece(0).T.astype(BF16)
    win_o[:, Q_LORA:C_QKR[1]] = jnp.concatenate([piece(2), piece(6), zeros], axis=0).T.astype(BF16)
    win_o[:, C_KV[0]:C_KV[1]] = piece(1).T.astype(BF16)
    win_o[:, C_GQ[0]:C_GQ[1]] = (piece(3).T * (GLA_DK ** -0.5)).astype(BF16)
    win_o[:, C_GK[0]:C_GK[1]] = piece(4).T.astype(BF16)
    win_o[:, C_GV[0]:C_GV[1]] = piece(5).T.astype(BF16)
    win_o[:, C_GR[0]:C_GR[1]] = piece(7).T.astype(BF16)

    zq = jnp.zeros((Q_LORA, HEAD_PAD - QK_DIM), BF16)
    kvw = QK_NOPE + V_DIM
    for hh in range(MLA_HEADS):
        wuq_o[:, hh * HEAD_PAD:hh * HEAD_PAD + QK_DIM] = wuq_ref[:, hh * QK_DIM:(hh + 1) * QK_DIM].astype(BF16)
        wuq_o[:, hh * HEAD_PAD + QK_DIM:(hh + 1) * HEAD_PAD] = zq
        uk = wukv_ref[:, hh * kvw:hh * kvw + QK_NOPE]
        uv = wukv_ref[:, hh * kvw + QK_NOPE:(hh + 1) * kvw].astype(BF16)
        wukv_o[:, hh * QK_NOPE:(hh + 1) * QK_NOPE] = uk.astype(BF16)
        wukv_o[:, (MLA_HEADS + hh) * V_DIM:(MLA_HEADS + hh + 1) * V_DIM] = uv
        wukt_o[hh * QK_NOPE:(hh + 1) * QK_NOPE, :] = uk.T.astype(BF16)
        wuv_o[:, hh * V_DIM:(hh + 1) * V_DIM] = uv

    wg_o[...] = jnp.zeros(wg_o.shape, BF16)
    wg_o[QK_ROPE:QK_ROPE + GLA_GATE_RANK, :] = wg_ref[...].astype(BF16)


def _relayout(w_in_t, w_uq, w_ukv, w_gate_up):
    d = w_in_t.shape[1]
    shapes = [(d, C_GR[1]), (Q_LORA, MLA_HEADS * HEAD_PAD), (KV_LORA, MLA_HEADS * (QK_NOPE + V_DIM)),
              (MLA_HEADS * QK_NOPE, KV_LORA), (KV_LORA, MLA_WIDTH), (LANES, GLA_QK)]
    return pl.pallas_call(
        _relayout_kernel,
        out_shape=[jax.ShapeDtypeStruct(sh, BF16) for sh in shapes],
        compiler_params=pltpu.CompilerParams(vmem_limit_bytes=VMEM_LIMIT),
        name="weight_relayout",
    )(w_in_t, w_uq, w_ukv, w_gate_up)


def _prep_weights(w_in, g_norm1, g_q_lora, w_uq, g_kv_lora, w_ukv, g_q_head, g_k_head,
                  w_gate_up, b_gate_up, g_gla_out, w_out, g_norm2, w_up, w_down):
    d = w_in.shape[0]
    w_in_p, w_uq_p, w_ukv_p, w_ukt, w_uv, w_gate = _relayout(w_in.T, w_uq, w_ukv, w_gate_up)
    qscale = QK_DIM ** -0.5 * LOG2E
    bound = 1.02 * QK_DIM ** 0.5 * LOG2E * jnp.max(jnp.abs(g_q_head)) * jnp.max(jnp.abs(g_k_head))
    lane = jnp.arange(LANES) == QK_ROPE
    return {
        "w_in": w_in_p, "g_norm1": g_norm1.reshape(1, d), "g_q_lora": g_q_lora.reshape(1, Q_LORA),
        "w_uq": w_uq_p, "g_kv_lora": g_kv_lora.reshape(1, KV_LORA), "w_ukv": w_ukv_p, "w_ukt": w_ukt, "w_uv": w_uv,
        "gkr_col": g_k_head[QK_NOPE:].reshape(QK_ROPE, 1),
        "qone": lane.astype(F32).reshape(1, LANES), "kbias": jnp.where(lane, -bound, 0.0).reshape(1, LANES),
        "fast_softmax": (bound <= MAX_FIXED_SHIFT).astype(jnp.int32).reshape(1), "bound": bound.reshape(1),
        "gqn": (g_q_head[:QK_NOPE] * qscale).reshape(1, LANES), "gqr": _pad_gain(g_q_head[QK_NOPE:] * qscale),
        "gkn": g_k_head[:QK_NOPE].reshape(1, LANES), "gkr": _pad_gain(g_k_head[QK_NOPE:]),
        "w_gate": w_gate, "b_gate": b_gate_up.reshape(1, GLA_QK),
        "g_gla_out": g_gla_out.reshape(1, GLA_WIDTH), "w_out": w_out.astype(BF16),
        "g_norm2": g_norm2.reshape(1, d), "w_up": w_up.astype(BF16), "w_down": w_down.astype(BF16),
    }


def _mixers(x, mod, past_lat, past_kr, s0, w, *, tm):
    batch, seq, d = x.shape
    n = batch * seq
    past = 0 if past_lat is None else past_lat.shape[1]
    x2 = x.reshape(n, d)
    mod4 = mod.reshape(batch, 6, 1, d)
    tm = min(tm, n)
    tab = _rope_table(past, seq, repeat=max(1, tm // seq))
    if past == 0:
        lat, krt, q, k, v, gq, gk, gv, la, gr = _projection(x2, mod4, tab, w, rows_per_group=seq, tm=tm, prompt=True)
        kr = jnp.swapaxes(krt, 1, 2)
        a_out = _attention_prompt(w["fast_softmax"], q, k, v, batch=batch, seq=seq, tq=min(ATTN_TILE, seq))
    else:
        assert seq == CHUNK and past % CHUNK == 0
        lat, kr, krt, q, gq, gk, gv, la, gr = _projection(x2, mod4, tab, w, rows_per_group=seq, tm=tm, prompt=False)
        a_out = _attention_sample(q, past_lat, jnp.swapaxes(past_kr, 1, 2), lat, krt, w, tkb=min(CACHE_BLOCK, past))
    b_out, s_new = _gla(gq, gk, gv, la, gr, s0, w["g_gla_out"], groups=batch, rows_per_group=seq,
                        tc=tm)
    return (x2, a_out, b_out, mod4, seq), (lat.reshape(batch, seq, KV_LORA), kr.reshape(batch, seq, QK_ROPE), s_new)


def kernel(x_prompt, x_sample, cache_mla_latent, cache_mla_krope, state_gla, c_prompt, c_sample,
           w_ada, b_ada, g_norm1, w_in, g_q_lora, w_uq, g_kv_lora, w_ukv, g_q_head, g_k_head,
           w_gate_up, b_gate_up, g_gla_out, w_out, g_norm2, w_up, w_down):
    nb = x_prompt.shape[0]
    depth = w_ada.shape[0]
    y_p, y_s = x_prompt, x_sample
    outs = [[] for _ in range(6)]
    for l in range(depth):
        w = _prep_weights(w_in[l], g_norm1[l], g_q_lora[l], w_uq[l], g_kv_lora[l], w_ukv[l], g_q_head[l],
                          g_k_head[l], w_gate_up[l], b_gate_up[l], g_gla_out[l], w_out[l], g_norm2[l],
                          w_up[l], w_down[l])
        mod_p, mod_s = _modulation(c_prompt, c_sample, w_ada[l], b_ada[l])
        zero_state = jnp.zeros((nb, GLA_HEADS, GLA_DK, GLA_DV), x_prompt.dtype)
        mix_p, new_p = _mixers(y_p, mod_p, None, None, zero_state, w, tm=ROW_TILE)
        mix_s, new_s = _mixers(y_s, mod_s, cache_mla_latent[l], cache_mla_krope[l], state_gla[l], w, tm=ROW_TILE)
        y2_p, y2_s = _mlp([mix_p, mix_s], w, tm=ROW_TILE)
        y_p, y_s = y2_p.reshape(y_p.shape), y2_s.reshape(y_s.shape)
        for o, new in zip(outs, new_p + new_s):
            o.append(new)
    return (y_p, y_s) + tuple(jnp.stack(o) for o in outs)
```

```python
import functools

import jax
import jax.numpy as jnp
import numpy as np
from jax import lax
from jax.experimental import pallas as pl
from jax.experimental.pallas import tpu as pltpu

F32 = jnp.float32
BF16 = jnp.bfloat16

CHUNK = 64
EPS = 1e-6
MLA_HEADS = 4
Q_LORA = 384
KV_LORA = 256
QK_NOPE = 128
QK_ROPE = 64
QK_DIM = QK_NOPE + QK_ROPE
V_DIM = 128
ROPE_THETA = 10000.0
GLA_HEADS = 4
GLA_DK = 64
GLA_DV = 128
GLA_GATE_RANK = 16
GLA_TAU = 16.0
GLA_QK = GLA_HEADS * GLA_DK
GLA_WIDTH = GLA_HEADS * GLA_DV
MLA_WIDTH = MLA_HEADS * V_DIM
HEAD_PAD = 256
SUB = 8
LOG2E = 1.4426950408889634
MAX_FIXED_SHIFT = 48.0

LANES = 128
VMEM_LIMIT = 56 * 1024 * 1024
ROW_TILE = 512
ATTN_TILE = 512
CACHE_BLOCK = 512
FF_SLICE = 1024
MOD_TILE = 1024

C_QKR = (0, 512)
C_KV = (512, 768)
C_GQ = (768, 1024)
C_GK = (1024, 1280)
C_GV = (1280, 1792)
C_GR = (1792, 2304)


def _dot(a, b):
    return jnp.dot(a, b, preferred_element_type=F32)


def _dot_nt(a, b):
    return lax.dot_general(a, b, (((1,), (1,)), ((), ())), preferred_element_type=F32)


def _dot_tn(a, b):
    return lax.dot_general(a, b, (((0,), (0,)), ((), ())), preferred_element_type=F32)


def _rope_tile(t, c, sa, sb):
    return t * c + pltpu.roll(t, 96, 1) * sa + pltpu.roll(t, 32, 1) * sb


def _params(*sem):
    return pltpu.CompilerParams(dimension_semantics=sem, vmem_limit_bytes=VMEM_LIMIT)


def _const(shape):
    return pl.BlockSpec(shape, lambda *_: (0,) * len(shape), pipeline_mode=pl.Buffered(1))


def _mod_kernel(cs_ref, cp_ref, w_ref, b_ref, os_ref, op_ref):
    ns = cs_ref.shape[0]
    c = jnp.concatenate([cs_ref[...], cp_ref[...]], axis=0)
    s = (c * jax.nn.sigmoid(c)).astype(BF16)
    mod = _dot(s, w_ref[...].astype(BF16)) + b_ref[...]
    os_ref[...] = mod[0:ns]
    op_ref[...] = mod[ns:]


def _modulation(c_prompt, c_sample, w_ada, b_ada):
    d, n = w_ada.shape
    tn = MOD_TILE

    def whole(c):
        return pl.BlockSpec(c.shape, lambda j: (0, 0))

    def cols(c):
        return pl.BlockSpec((c.shape[0], tn), lambda j: (0, j))

    mod_s, mod_p = pl.pallas_call(
        _mod_kernel,
        grid=(n // tn,),
        in_specs=[whole(c_sample), whole(c_prompt),
                  pl.BlockSpec((d, tn), lambda j: (0, j)),
                  pl.BlockSpec((1, tn), lambda j: (0, j))],
        out_specs=[cols(c_sample), cols(c_prompt)],
        out_shape=[jax.ShapeDtypeStruct((c_sample.shape[0], n), F32), jax.ShapeDtypeStruct((c_prompt.shape[0], n), F32)],
        compiler_params=_params("arbitrary"),
        name="adaln_mod",
    )(c_sample, c_prompt, w_ada, b_ada.reshape(1, n))
    return mod_p, mod_s


def _proj_kernel(x_ref, sh_ref, sc_ref, g1_ref, win_ref, gql_ref, wuq_ref, gkv_ref, wukv_ref,
                 gqn_ref, gqr_ref, gkn_ref, gkr_ref, qone_ref, kbias_ref, tab_ref, wg_ref, bg_ref,
                 *outs, gpt, prompt):
    if prompt:
        lat_ref, krt_ref, q_ref, k_ref, v_ref, gq_ref, gk_ref, gv_ref, la_ref, gr_ref = outs
    else:
        lat_ref, kr_ref, krt_ref, q_ref, gq_ref, gk_ref, gv_ref, la_ref, gr_ref = outs
    tm, d = x_ref.shape
    x = x_ref[...]
    xn = x * lax.rsqrt(jnp.mean(x * x, axis=-1, keepdims=True) + EPS)
    h = (xn.reshape(gpt, tm // gpt, d) * (g1_ref[...] * (1.0 + sc_ref[...])) + sh_ref[...]).reshape(tm, d)
    hb = h.astype(BF16)

    def col(c):
        return _dot(hb, win_ref[:, c[0]:c[1]])

    tab = tab_ref[...]
    cos, sa, sb = tab[:, 0:LANES], tab[:, LANES:2 * LANES], tab[:, 2 * LANES:3 * LANES]

    qkr = col(C_QKR)
    krg = qkr[:, Q_LORA:]
    z = _dot(krg.astype(BF16), wg_ref[...]) + bg_ref[...]
    la_ref[...] = (jnp.minimum(z, 0.0) - jnp.log(1.0 + jnp.exp(-jnp.abs(z)))) * (1.0 / GLA_TAU)
    r = col(C_GR)
    gr_ref[...] = (r * jax.nn.sigmoid(r)).astype(BF16)

    cq = qkr[:, 0:Q_LORA]
    cqn = cq * lax.rsqrt(jnp.mean(cq * cq, axis=-1, keepdims=True) + EPS) * gql_ref[...]
    qp = _dot(cqn.astype(BF16), wuq_ref[...])
    for hh in range(MLA_HEADS):
        nope = qp[:, hh * HEAD_PAD:hh * HEAD_PAD + QK_NOPE]
        rt = qp[:, hh * HEAD_PAD + QK_NOPE:(hh + 1) * HEAD_PAD]
        ss = jnp.sum(nope * nope, axis=-1, keepdims=True) + jnp.sum(rt * rt, axis=-1, keepdims=True)
        inv = lax.rsqrt(ss * (1.0 / QK_DIM) + EPS)
        q_ref[:, hh * HEAD_PAD:hh * HEAD_PAD + QK_NOPE] = (nope * inv * gqn_ref[...]).astype(BF16)
        rq = _rope_tile(rt * inv * gqr_ref[...], cos, sa, sb)
        q_ref[:, hh * HEAD_PAD + QK_NOPE:(hh + 1) * HEAD_PAD] = (rq + qone_ref[...]).astype(BF16)

    ckv = col(C_KV)
    lat = ckv * lax.rsqrt(jnp.mean(ckv * ckv, axis=-1, keepdims=True) + EPS) * gkv_ref[...]
    lat_ref[...] = lat
    krt_ref[...] = krg.T[0:QK_ROPE, :]
    if prompt:
        lane = lax.broadcasted_iota(jnp.int32, (1, LANES), 1)
        krm = jnp.where(lane < QK_ROPE, krg, 0.0)
        ssr = jnp.sum(krm * krm, axis=-1, keepdims=True)
        rk = _rope_tile(krm * gkr_ref[...], cos, sa, sb)
        kv = _dot(lat.astype(BF16), wukv_ref[...])
        for hh in range(MLA_HEADS):
            kn = kv[:, hh * QK_NOPE:(hh + 1) * QK_NOPE]
            inv = lax.rsqrt((jnp.sum(kn * kn, axis=-1, keepdims=True) + ssr) * (1.0 / QK_DIM) + EPS)
            k_ref[:, hh * HEAD_PAD:hh * HEAD_PAD + QK_NOPE] = (kn * inv * gkn_ref[...]).astype(BF16)
            k_ref[:, hh * HEAD_PAD + QK_NOPE:(hh + 1) * HEAD_PAD] = (rk * inv + kbias_ref[...]).astype(BF16)
        v_ref[...] = kv[:, MLA_HEADS * QK_NOPE:].astype(BF16)
    else:
        kr_ref[...] = krg[:, 0:QK_ROPE]

    gq_ref[...] = col(C_GQ)
    gk_ref[...] = col(C_GK)
    gv_ref[...] = col(C_GV).astype(BF16)


def _projection(x2, mod4, tab, w, *, rows_per_group, tm, prompt):
    n, d = x2.shape
    gpt = max(1, tm // rows_per_group)
    tpg = max(1, rows_per_group // tm)
    ntab = tab.shape[0] // tm

    def mod_spec(j):
        return pl.BlockSpec((gpt, None, 1, d), lambda i: ((i // tpg) if gpt == 1 else i, j, 0, 0))

    def rows(c):
        return pl.BlockSpec((tm, c), lambda i: (i, 0))

    def out(c, t):
        return rows(c), jax.ShapeDtypeStruct((n, c), t)

    gla_outs = [out(GLA_QK, F32), out(GLA_QK, F32), out(GLA_WIDTH, BF16), out(GLA_QK, F32), out(GLA_WIDTH, BF16)]
    qo = out(MLA_HEADS * HEAD_PAD, BF16)
    if prompt:
        krt = (pl.BlockSpec((None, QK_ROPE, tm), lambda i: (i // tpg, 0, i % tpg)),
               jax.ShapeDtypeStruct((n // rows_per_group, QK_ROPE, rows_per_group), F32))
        outs = [out(KV_LORA, F32), krt, qo, out(MLA_HEADS * HEAD_PAD, BF16), out(MLA_WIDTH, BF16)] + gla_outs
    else:
        krt = (pl.BlockSpec((QK_ROPE, tm), lambda i: (0, i)), jax.ShapeDtypeStruct((QK_ROPE, n), F32))
        outs = [out(KV_LORA, F32), out(QK_ROPE, F32), krt, qo] + gla_outs
    return pl.pallas_call(
        functools.partial(_proj_kernel, gpt=gpt, prompt=prompt),
        grid=(n // tm,),
        in_specs=[rows(d), mod_spec(0), mod_spec(1), _const((1, d)), _const(w["w_in"].shape),
                  _const((1, Q_LORA)), _const(w["w_uq"].shape), _const((1, KV_LORA)), _const(w["w_ukv"].shape),
                  _const((1, LANES)), _const((1, LANES)), _const((1, LANES)), _const((1, LANES)),
                  _const((1, LANES)), _const((1, LANES)),
                  pl.BlockSpec((tm, 3 * LANES), lambda i: (i % ntab, 0)),
                  _const(w["w_gate"].shape), _const((1, GLA_QK))],
        out_specs=[o[0] for o in outs],
        out_shape=[o[1] for o in outs],
        compiler_params=_params("arbitrary"),
        name="in_proj",
    )(x2, mod4, mod4, w["g_norm1"], w["w_in"], w["g_q_lora"], w["w_uq"], w["g_kv_lora"], w["w_ukv"],
      w["gqn"], w["gqr"], w["gkn"], w["gkr"], w["qone"], w["kbias"], tab, w["w_gate"], w["b_gate"])


def _attn_prompt_kernel(q_ref, k_ref, v_ref, o_ref, vx_ref, m_ref, acc_ref, *, tq, fast):
    i = pl.program_id(1)
    heads = range(MLA_HEADS)

    @pl.when(i == 0)
    def _():
        for hh in heads:
            vx_ref[hh, :, 0:V_DIM] = v_ref[:, hh * V_DIM:(hh + 1) * V_DIM]
            vx_ref[hh, :, V_DIM:] = jnp.ones((v_ref.shape[0], V_DIM), BF16)

    def scores(j, hh, masked):
        hs = slice(hh * HEAD_PAD, (hh + 1) * HEAD_PAD)
        s = _dot_nt(q_ref[:, hs], k_ref[pl.ds(pl.multiple_of(j * tq, tq), tq), hs])
        if masked:
            qc = lax.broadcasted_iota(jnp.int32, (tq, tq), 0) // CHUNK
            kc = lax.broadcasted_iota(jnp.int32, (tq, tq), 1) // CHUNK
            s = jnp.where(kc <= qc, s, -jnp.inf)
        return s

    def fast_diagonal():
        hq = tq // 2
        k0 = pl.multiple_of(i * tq, tq)
        qc = lax.broadcasted_iota(jnp.int32, (hq, hq), 0) // CHUNK
        kc = lax.broadcasted_iota(jnp.int32, (hq, hq), 1) // CHUNK
        tri = kc <= qc
        for hh in heads:
            hs = slice(hh * HEAD_PAD, (hh + 1) * HEAD_PAD)
            top = jnp.where(tri, _dot_nt(q_ref[0:hq, hs], k_ref[pl.ds(k0, hq), hs]), -jnp.inf)
            acc_ref[hh, 0:hq] = _dot(jnp.exp2(top).astype(BF16), vx_ref[hh, pl.ds(k0, hq), :])
            bot = _dot_nt(q_ref[hq:tq, hs], k_ref[pl.ds(k0, tq), hs])
            bot = jnp.concatenate([bot[:, 0:hq], jnp.where(tri, bot[:, hq:tq], -jnp.inf)], axis=1)
            acc_ref[hh, hq:tq] = _dot(jnp.exp2(bot).astype(BF16), vx_ref[hh, pl.ds(k0, tq), :])

    def vblock(j, hh):
        return vx_ref[hh, pl.ds(pl.multiple_of(j * tq, tq), tq), :]

    def fast_block(j, masked):
        for hh in heads:
            acc_ref[hh] += _dot(jnp.exp2(scores(j, hh, masked)).astype(BF16), vblock(j, hh))

    def safe_block(j, masked):
        for hh in heads:
            s = scores(j, hh, masked)
            m = m_ref[hh]
            m_new = jnp.maximum(m, jnp.max(s, axis=1, keepdims=True))
            p = jnp.exp2(s - m_new).astype(BF16)
            acc_ref[hh] = jnp.exp2(m - m_new) * acc_ref[hh] + _dot(p, vblock(j, hh))
            m_ref[hh] = m_new

    def sweep(block):
        def body(jj, c):
            block(2 * jj, False)
            block(2 * jj + 1, False)
            return c

        lax.fori_loop(0, i // 2, body, 0)

        @pl.when(i % 2 == 1)
        def _():
            block(i - 1, False)

    if fast:
        fast_diagonal()
        sweep(fast_block)
    else:
        m_ref[...] = jnp.full(m_ref.shape, -jnp.inf, F32)
        acc_ref[...] = jnp.zeros(acc_ref.shape, F32)
        safe_block(i, True)
        sweep(safe_block)

    for hh in heads:
        acc = acc_ref[hh]
        o_ref[:, hh * V_DIM:(hh + 1) * V_DIM] = (acc[:, 0:V_DIM] / acc[:, V_DIM:]).astype(BF16)


def _attention_prompt(q, k, v, *, batch, seq, tq, fast):
    nq = seq // tq
    return pl.pallas_call(
        functools.partial(_attn_prompt_kernel, tq=tq, fast=fast),
        grid=(batch, nq),
        in_specs=[pl.BlockSpec((tq, MLA_HEADS * HEAD_PAD), lambda b, i: (b * nq + i, 0)),
                  pl.BlockSpec((seq, MLA_HEADS * HEAD_PAD), lambda b, i: (b, 0)),
                  pl.BlockSpec((seq, MLA_WIDTH), lambda b, i: (b, 0))],
        out_specs=pl.BlockSpec((tq, MLA_WIDTH), lambda b, i: (b * nq + i, 0)),
        out_shape=jax.ShapeDtypeStruct((batch * seq, MLA_WIDTH), BF16),
        scratch_shapes=[pltpu.VMEM((MLA_HEADS, seq, 2 * V_DIM), BF16), pltpu.VMEM((MLA_HEADS, tq, 1), F32),
                        pltpu.VMEM((MLA_HEADS, tq, 2 * V_DIM), F32)],
        compiler_params=_params("arbitrary", "arbitrary"),
        name="mla_attn_prompt",
    )(q, k, v)


def _attn_sample_kernel(bound_ref, q_ref, lat_ref, krt_ref, latn_ref, krtn_ref, tab_ref, wukt_ref, wuv_ref,
                        gkn_ref, gkr_ref, o_ref, wq_ref, latb_ref, s_ref, *, tkb, fast):
    t = q_ref.shape[0] // 2
    past = lat_ref.shape[1]
    nk = MLA_HEADS * QK_NOPE
    half = QK_ROPE // 2
    streams = range(2)

    @pl.when(pl.program_id(0) == 0)
    def _():
        for e in streams:
            wq_ref[e, 0:nk, :] = wukt_ref[...]

    qr = []
    for e in streams:
        rows = slice(e * t, (e + 1) * t)
        parts = []
        for hh in range(MLA_HEADS):
            qn = (q_ref[rows, hh * HEAD_PAD:hh * HEAD_PAD + QK_NOPE].astype(F32) * gkn_ref[...]).astype(BF16)
            wq_ref[e, nk + hh * t:nk + (hh + 1) * t, :] = _dot(
                qn, wukt_ref[hh * QK_NOPE:(hh + 1) * QK_NOPE, :]).astype(BF16)
            parts.append(q_ref[rows, hh * HEAD_PAD + QK_NOPE:(hh + 1) * HEAD_PAD])
        qr.append(jnp.concatenate(parts, axis=0))

    def key_block(e, lat, krt, tab, width, valid):
        latb = lat.astype(BF16)
        g = _dot_nt(wq_ref[e], latb)
        ssr = jnp.sum(krt * krt, axis=0, keepdims=True)
        kg = krt * gkr_ref[...]
        x1, x2 = kg[0:half], kg[half:QK_ROPE]
        c, sn = tab[0:half], tab[half:QK_ROPE]
        rope = jnp.concatenate([x1 * c - x2 * sn, x2 * c + x1 * sn, jnp.zeros((LANES - QK_ROPE, width), F32)], axis=0)
        srope = _dot(qr[e], rope.astype(BF16))
        rows = []
        for hh in range(MLA_HEADS):
            kn = g[hh * QK_NOPE:(hh + 1) * QK_NOPE]
            inv = lax.rsqrt((jnp.sum(kn * kn, axis=0, keepdims=True) + ssr) * (1.0 / QK_DIM) + EPS)
            rows.append((g[nk + hh * t:nk + (hh + 1) * t] + srope[hh * t:(hh + 1) * t]) * inv)
        sc = jnp.concatenate(rows, axis=0) - bound_ref[0]
        if valid is not None:
            sc = jnp.where(valid, sc, -jnp.inf)
        return sc, latb

    def blocks():
        for blk in range(past // tkb):
            c0 = blk * tkb
            for e in streams:
                yield e, c0, tkb, key_block(e, lat_ref[e, c0:c0 + tkb, :], krt_ref[e, :, c0:c0 + tkb],
                                            tab_ref[:, c0:c0 + tkb], tkb, None)
        for e in streams:
            mine = lax.broadcasted_iota(jnp.int32, (1, 2 * t), 1) // t == e
            yield e, past, 2 * t, key_block(e, latn_ref[...], krtn_ref[...], tab_ref[:, past:past + 2 * t], 2 * t, mine)

    def finish(e, wlat, l):
        wlat = wlat.astype(BF16)
        for hh in range(MLA_HEADS):
            o = _dot(wlat[hh * t:(hh + 1) * t], wuv_ref[:, hh * V_DIM:(hh + 1) * V_DIM])
            o_ref[e * t:(e + 1) * t, hh * V_DIM:(hh + 1) * V_DIM] = (o / l[hh * t:(hh + 1) * t]).astype(BF16)

    if fast:
        wlat = [jnp.zeros((MLA_HEADS * t, KV_LORA), F32) for _ in streams]
        lsum = [jnp.zeros((MLA_HEADS * t, LANES), F32) for _ in streams]
        for e, _, width, (sc, latb) in blocks():
            p = jnp.exp2(sc)
            wlat[e] += _dot(p.astype(BF16), latb)
            for c in range(width // LANES):
                lsum[e] += p[:, c * LANES:(c + 1) * LANES]
        for e in streams:
            finish(e, wlat[e], jnp.sum(lsum[e], axis=1, keepdims=True))

    else:
        for e, c0, width, (sc, latb) in blocks():
            s_ref[e, :, c0:c0 + width] = sc
            latb_ref[e, c0:c0 + width, :] = latb
        for e in streams:
            s = s_ref[e]
            p = jnp.exp2(s - jnp.max(s, axis=1, keepdims=True))
            finish(e, _dot(p.astype(BF16), latb_ref[e]), jnp.sum(p, axis=1, keepdims=True))


def _attention_sample(q, past_lat, past_krt, lat_new, krt_new, w, *, tkb, fast):
    batch, past, _ = past_lat.shape
    t = q.shape[0] // batch
    half = QK_ROPE // 2
    inv = ROPE_THETA ** (-np.arange(half, dtype=np.float64) / half)
    pos = np.concatenate([np.arange(past), past + np.arange(t), past + np.arange(t)]).astype(np.float64)
    ang = inv[:, None] * pos[None, :]
    tab = jnp.asarray(np.concatenate([np.cos(ang), np.sin(ang)], axis=0).astype(np.float32))
    s_pad = past + 2 * t
    return pl.pallas_call(
        functools.partial(_attn_sample_kernel, tkb=tkb, fast=fast),
        grid=(batch // 2,),
        in_specs=[pl.BlockSpec(memory_space=pltpu.SMEM),
                  pl.BlockSpec((2 * t, MLA_HEADS * HEAD_PAD), lambda g: (g, 0)),
                  pl.BlockSpec((2, past, KV_LORA), lambda g: (g, 0, 0)),
                  pl.BlockSpec((2, QK_ROPE, past), lambda g: (g, 0, 0)),
                  pl.BlockSpec((2 * t, KV_LORA), lambda g: (g, 0)),
                  pl.BlockSpec((QK_ROPE, 2 * t), lambda g: (0, g)),
                  _const(tab.shape), _const(w["w_ukt"].shape), _const(w["w_uv"].shape),
                  _const((1, LANES)), _const((QK_ROPE, 1))],
        out_specs=pl.BlockSpec((2 * t, MLA_WIDTH), lambda g: (g, 0)),
        out_shape=jax.ShapeDtypeStruct((batch * t, MLA_WIDTH), BF16),
        scratch_shapes=[pltpu.VMEM((2, MLA_HEADS * (QK_NOPE + t), KV_LORA), BF16),
                        pltpu.VMEM((2, s_pad, KV_LORA), BF16),
                        pltpu.VMEM((2, MLA_HEADS * t, s_pad), F32)],
        compiler_params=_params("arbitrary"),
        name="mla_attn_sample",
    )(w["bound"], q, past_lat, past_krt, lat_new, krt_new, tab, w["w_ukt"], w["w_uv"], w["gkn"], w["gkr_col"])


def _gla_kernel(q_ref, k_ref, v_ref, la_ref, r_ref, s0_ref, g_ref, spread_ref, o_ref, sn_ref,
                st_ref, kp_ref, ap_ref, p_ref, on_ref, *, gpt):
    t_idx = pl.program_id(1)
    L, W, P = CHUNK, GLA_QK, LANES
    R = q_ref.shape[0]
    n_chunks = R // L
    cpg = n_chunks // gpt
    n_pairs = GLA_HEADS // 2

    lane_p = lax.broadcasted_iota(jnp.int32, (1, P), 1)
    even = lane_p < GLA_DK
    bd_mask = (lax.broadcasted_iota(jnp.int32, (2 * GLA_DV, P), 0) // GLA_DV
               == lax.broadcasted_iota(jnp.int32, (2 * GLA_DV, P), 1) // GLA_DK)

    @pl.when(t_idx == 0)
    def _():
        kp_ref[0:SUB, :] = jnp.zeros((SUB, W), F32)
        ap_ref[0:SUB, :] = jnp.zeros((SUB, W), F32)
        for gi in range(gpt):
            for pr in range(n_pairs):
                tt = s0_ref[gi, 2 * pr:2 * pr + 2].reshape(2 * GLA_DK, GLA_DV).T
                st_ref[gi, pr] = jnp.where(bd_mask, jnp.concatenate([tt, tt], axis=0), 0.0)

    q = q_ref[...]
    k = k_ref[...]
    la = la_ref[...]

    tri = (lax.broadcasted_iota(jnp.int32, (L, L), 0) >= lax.broadcasted_iota(jnp.int32, (L, L), 1)).astype(BF16)
    la_hi = la.astype(BF16)
    la2 = jnp.concatenate([la_hi, (la - la_hi.astype(F32)).astype(BF16)], axis=1)
    bs = []
    for c in range(n_chunks):
        t2 = _dot(tri, la2[c * L:(c + 1) * L, :])
        bs.append(t2[:, 0:W] + t2[:, W:2 * W])
    b = (jnp.concatenate(bs, axis=0) if n_chunks > 1 else bs[0]) * LOG2E
    b3 = b.reshape(n_chunks, L, W)

    def chunk_row(r):
        return jnp.broadcast_to(b3[:, r:r + 1, :], (n_chunks, L, W)).reshape(R, W)

    b_sub = jnp.broadcast_to(b.reshape(R // SUB, SUB, W)[:, 0:1, :], (R // SUB, SUB, W)).reshape(R, W)
    sub = (lax.broadcasted_iota(jnp.int32, (R, W), 0) % L) // SUB

    qt = q * jnp.exp2(b - b_sub)
    zb = jnp.zeros((), BF16)
    zq = jnp.zeros((SUB, P), F32)
    ktm = []
    for i in range(1, L // SUB):
        kt = (k * jnp.exp2(chunk_row(i * SUB) - b)).astype(BF16)
        ktm.append([[jnp.where((sub[:, 0:P] < i) & (even if e == 0 else ~even), kt[:, pr * P:(pr + 1) * P], zb)
                     for e in range(2)] for pr in range(n_pairs)])

    a = jnp.exp(la)
    kp_ref[SUB:SUB + R, :] = k
    ap_ref[SUB:SUB + R, :] = a
    p_ref[:, 0:W] = (q * k).astype(BF16)
    e = a
    for d in range(1, SUB):
        if d > 1:
            e = e * ap_ref[SUB - d + 1:SUB - d + 1 + R, :]
        p_ref[:, d * W:(d + 1) * W] = (q * kp_ref[SUB - d:SUB - d + R, :] * e).astype(BF16)
    cband = _dot(p_ref[...], spread_ref[...])
    same_sub = (lax.broadcasted_iota(jnp.int32, (L, W), 0) // SUB
                == (lax.broadcasted_iota(jnp.int32, (L, W), 1) % L) // SUB)

    qe = (q * jnp.exp2(b)).astype(BF16)
    kd = (k * jnp.exp2(chunk_row(L - 1) - b)).astype(BF16)
    zv = jnp.zeros((L, 2 * GLA_DV), BF16)

    o_intra, d_st, dec = {}, {}, []
    for c in range(n_chunks):
        rs = slice(c * L, (c + 1) * L)
        a_band = jnp.where(same_sub, pltpu.roll(cband[rs], W - (SUB - 1), 1, stride=1, stride_axis=0), 0.0)
        dec.append(jnp.exp2(b[c * L + L - 1:c * L + L, :]))
        for pr in range(n_pairs):
            ls = slice(pr * P, (pr + 1) * P)
            lhs_c = jnp.concatenate(
                [jnp.concatenate([qt[c * L + r * SUB:c * L + (r + 1) * SUB, ls] if i == r else zq
                                  for i in range(1, L // SUB)], axis=1) for r in range(L // SUB)],
                axis=0).astype(BF16)
            rhs_c = jnp.concatenate([jnp.concatenate([m[pr][0][rs], m[pr][1][rs]], axis=0) for m in ktm], axis=1)
            a_tot = (a_band[:, ls] + _dot_nt(lhs_c, rhs_c)).astype(BF16)
            vp = v_ref[rs, 2 * pr * GLA_DV:(2 * pr + 2) * GLA_DV]
            v_bd = jnp.concatenate([jnp.concatenate([vp[:, 0:GLA_DV], zv[:, 0:GLA_DV]], axis=1),
                                    jnp.concatenate([zv[:, 0:GLA_DV], vp[:, GLA_DV:]], axis=1)], axis=0)
            o_intra[c, pr] = _dot(a_tot, v_bd)
            d_st[c, pr] = jnp.where(bd_mask, _dot_tn(vp, kd[rs, ls]), 0.0)
    st_in = {}
    for gi in range(gpt):
        for pr in range(n_pairs):
            st = st_ref[gi, pr]
            for c in range(gi * cpg, (gi + 1) * cpg):
                st_in[c, pr] = st.astype(BF16)
                st = st * dec[c][:, pr * P:(pr + 1) * P] + d_st[c, pr]
            st_ref[gi, pr] = st
    for c in range(n_chunks):
        rs = slice(c * L, (c + 1) * L)
        for pr in range(n_pairs):
            on_ref[rs, 2 * pr * GLA_DV:(2 * pr + 2) * GLA_DV] = (
                o_intra[c, pr] + _dot_nt(qe[rs, pr * P:(pr + 1) * P], st_in[c, pr]))

    for hh in range(GLA_HEADS):
        hs = slice(hh * GLA_DV, (hh + 1) * GLA_DV)
        o = on_ref[:, hs]
        on = o * lax.rsqrt(jnp.mean(o * o, axis=-1, keepdims=True) + EPS) * g_ref[:, hs]
        o_ref[:, hs] = (on * r_ref[:, hs].astype(F32)).astype(BF16)

    @pl.when(t_idx == pl.num_programs(1) - 1)
    def _():
        for gi in range(gpt):
            for pr in range(n_pairs):
                st = st_ref[gi, pr]
                tt = jnp.where(even, st[0:GLA_DV], st[GLA_DV:2 * GLA_DV])
                sn_ref[gi, 2 * pr:2 * pr + 2] = tt.T.reshape(2, GLA_DK, GLA_DV)


def _band_spread():
    m = np.zeros((SUB, GLA_HEADS, GLA_DK, GLA_QK), np.float32)
    for d in range(SUB):
        for h in range(GLA_HEADS):
            m[d, h, :, h * GLA_DK + SUB - 1 - d] = 1.0
    return jnp.asarray(m.reshape(SUB * GLA_QK, GLA_QK), BF16)


def _gla(gq, gk, gv, la, gr, s0, g_out, *, groups, rows_per_group, tc):
    gpt = max(1, tc // rows_per_group)
    nt = max(1, rows_per_group // tc)

    def rows(c):
        return pl.BlockSpec((tc, c), lambda g, t: (g * nt + t, 0))

    state = pl.BlockSpec((gpt, GLA_HEADS, GLA_DK, GLA_DV), lambda g, t: (g, 0, 0, 0))
    spread = _band_spread()
    return pl.pallas_call(
        functools.partial(_gla_kernel, gpt=gpt),
        grid=(groups // gpt, nt),
        in_specs=[rows(GLA_QK), rows(GLA_QK), rows(GLA_WIDTH), rows(GLA_QK), rows(GLA_WIDTH), state,
                  _const((1, GLA_WIDTH)), _const(spread.shape)],
        out_specs=[rows(GLA_WIDTH), state],
        out_shape=[jax.ShapeDtypeStruct((groups * rows_per_group, GLA_WIDTH), BF16),
                   jax.ShapeDtypeStruct((groups, GLA_HEADS, GLA_DK, GLA_DV), F32)],
        scratch_shapes=[pltpu.VMEM((gpt, GLA_HEADS // 2, 2 * GLA_DV, LANES), F32),
                        pltpu.VMEM((SUB + tc, GLA_QK), F32), pltpu.VMEM((SUB + tc, GLA_QK), F32),
                        pltpu.VMEM((tc, SUB * GLA_QK), BF16), pltpu.VMEM((tc, GLA_WIDTH), F32)],
        compiler_params=_params("arbitrary", "arbitrary"),
        name="gla",
    )(gq, gk, gv, la, gr, s0, g_out, spread)


def _mlp_tile(x_ref, a_ref, b_ref, g1_ref, sh2_ref, sc2_ref, g2_ref, gn_ref, wo_ref, wu_ref, wd_ref, y_ref, *, gpt):
    tm, d = x_ref.shape

    def per_group(val, ref, scale_plus_one=False):
        m = ref[...]
        if scale_plus_one:
            m = 1.0 + m
        return (val.reshape(gpt, tm // gpt, d) * m).reshape(tm, d)

    mix = jnp.concatenate([a_ref[...], b_ref[...]], axis=1)
    x1 = x_ref[...] + per_group(_dot(mix, wo_ref[...]), g1_ref)
    xn = x1 * lax.rsqrt(jnp.mean(x1 * x1, axis=-1, keepdims=True) + EPS) * gn_ref[...]
    h2 = (per_group(xn, sc2_ref, True).reshape(gpt, tm // gpt, d) + sh2_ref[...]).reshape(tm, d).astype(BF16)
    acc = jnp.zeros((tm, d), F32)
    for j in range(wu_ref.shape[1] // FF_SLICE):
        u = jnp.maximum(_dot(h2, wu_ref[:, j * FF_SLICE:(j + 1) * FF_SLICE]), 0.0)
        acc += _dot((u * u).astype(BF16), wd_ref[j * FF_SLICE:(j + 1) * FF_SLICE, :])
    y_ref[...] = x1 + per_group(acc, g2_ref)


def _mlp_kernel(*refs, tiles, gpts):
    per_phase = 7
    gn_ref, wo_ref, wu_ref, wd_ref = refs[len(tiles) * per_phase:len(tiles) * per_phase + 4]
    y_refs = refs[len(tiles) * per_phase + 4:]
    i = pl.program_id(0)
    first = 0
    for p, (n_tiles, gpt) in enumerate(zip(tiles, gpts)):
        ins = refs[p * per_phase:(p + 1) * per_phase]

        @pl.when(jnp.logical_and(i >= first, i < first + n_tiles))
        def _(ins=ins, p=p, gpt=gpt):
            _mlp_tile(*ins, gn_ref, wo_ref, wu_ref, wd_ref, y_refs[p], gpt=gpt)

        first += n_tiles


def _mlp(phases, w, *, tm):
    d = phases[0][0].shape[1]
    tiles = [ph[0].shape[0] // tm for ph in phases]
    gpts = [max(1, tm // ph[4]) for ph in phases]
    in_specs, args, first = [], [], 0
    for (x2, a_out, b_out, mod4, rpg), n_tiles, gpt in zip(phases, tiles, gpts):
        tpg = max(1, rpg // tm)

        def tile(i, first=first, n_tiles=n_tiles):
            return jnp.clip(i - first, 0, n_tiles - 1)

        def rows(c, tile=tile):
            return pl.BlockSpec((tm, c), lambda i: (tile(i), 0))

        def mod_spec(j, tile=tile, gpt=gpt, tpg=tpg):
            return pl.BlockSpec((gpt, None, 1, d), lambda i: (tile(i) // tpg, j, 0, 0))

        in_specs += [rows(d), rows(MLA_WIDTH), rows(GLA_WIDTH), mod_spec(2), mod_spec(3), mod_spec(4), mod_spec(5)]
        args += [x2, a_out, b_out, mod4, mod4, mod4, mod4]
        first += n_tiles
    out_specs, first = [], 0
    for n_tiles in tiles:
        out_specs.append(pl.BlockSpec((tm, d), lambda i, first=first, n_tiles=n_tiles:
                                      (jnp.clip(i - first, 0, n_tiles - 1), 0)))
        first += n_tiles
    return pl.pallas_call(
        functools.partial(_mlp_kernel, tiles=tuple(tiles), gpts=tuple(gpts)),
        grid=(sum(tiles),),
        in_specs=in_specs + [_const((1, d)), _const(w["w_out"].shape), _const(w["w_up"].shape),
                             _const(w["w_down"].shape)],
        out_specs=out_specs,
        out_shape=[jax.ShapeDtypeStruct(ph[0].shape, F32) for ph in phases],
        compiler_params=_params("arbitrary"),
        name="out_proj_mlp",
    )(*args, w["g_norm2"], w["w_out"], w["w_up"], w["w_down"])


def _rope_table(start, count, repeat=1):
    half = QK_ROPE // 2
    inv = ROPE_THETA ** (-np.arange(half, dtype=np.float64) / half)
    ang = (start + np.arange(count, dtype=np.float64))[:, None] * inv[None, :]
    c, s, z = np.cos(ang), np.sin(ang), np.zeros_like(ang)
    tab = np.concatenate([c, c, z, z, -s, z, z, z, z, s, z, z], axis=1).astype(np.float32)
    return jnp.asarray(np.tile(tab, (repeat, 1)))


def _pad_gain(g_rope):
    return jnp.concatenate([g_rope, jnp.zeros((LANES - QK_ROPE,), F32)]).reshape(1, LANES)


def _relayout_kernel(wint_ref, wuq_ref, wukv_ref, wg_ref, win_o, wuq_o, wukv_o, wukt_o, wuv_o, wg_o):
    s = np.cumsum([0, Q_LORA, KV_LORA, QK_ROPE, GLA_QK, GLA_QK, GLA_WIDTH, GLA_GATE_RANK, GLA_WIDTH])

    def piece(i):
        return wint_ref[int(s[i]):int(s[i + 1]), :]

    d = wint_ref.shape[1]
    zeros = jnp.zeros((LANES - QK_ROPE - GLA_GATE_RANK, d), F32)
    win_o[:, 0:Q_LORA] = piece(0).T.astype(BF16)
    win_o[:, Q_LORA:C_QKR[1]] = jnp.concatenate([piece(2), piece(6), zeros], axis=0).T.astype(BF16)
    win_o[:, C_KV[0]:C_KV[1]] = piece(1).T.astype(BF16)
    win_o[:, C_GQ[0]:C_GQ[1]] = (piece(3).T * (GLA_DK ** -0.5)).astype(BF16)
    win_o[:, C_GK[0]:C_GK[1]] = piece(4).T.astype(BF16)
    win_o[:, C_GV[0]:C_GV[1]] = piece(5).T.astype(BF16)
    win_o[:, C_GR[0]:C_GR[1]] = piece(7).T.astype(BF16)

    zq = jnp.zeros((Q_LORA, HEAD_PAD - QK_DIM), BF16)
    kvw = QK_NOPE + V_DIM
    for hh in range(MLA_HEADS):
        wuq_o[:, hh * HEAD_PAD:hh * HEAD_PAD + QK_DIM] = wuq_ref[:, hh * QK_DIM:(hh + 1) * QK_DIM].astype(BF16)
        wuq_o[:, hh * HEAD_PAD + QK_DIM:(hh + 1) * HEAD_PAD] = zq
        uk = wukv_ref[:, hh * kvw:hh * kvw + QK_NOPE]
        uv = wukv_ref[:, hh * kvw + QK_NOPE:(hh + 1) * kvw].astype(BF16)
        wukv_o[:, hh * QK_NOPE:(hh + 1) * QK_NOPE] = uk.astype(BF16)
        wukv_o[:, (MLA_HEADS + hh) * V_DIM:(MLA_HEADS + hh + 1) * V_DIM] = uv
        wukt_o[hh * QK_NOPE:(hh + 1) * QK_NOPE, :] = uk.T.astype(BF16)
        wuv_o[:, hh * V_DIM:(hh + 1) * V_DIM] = uv

    wg_o[...] = jnp.zeros(wg_o.shape, BF16)
    wg_o[QK_ROPE:QK_ROPE + GLA_GATE_RANK, :] = wg_ref[...].astype(BF16)


def _relayout(w_in_t, w_uq, w_ukv, w_gate_up):
    d = w_in_t.shape[1]
    shapes = [(d, C_GR[1]), (Q_LORA, MLA_HEADS * HEAD_PAD), (KV_LORA, MLA_HEADS * (QK_NOPE + V_DIM)),
              (MLA_HEADS * QK_NOPE, KV_LORA), (KV_LORA, MLA_WIDTH), (LANES, GLA_QK)]
    return pl.pallas_call(
        _relayout_kernel,
        out_shape=[jax.ShapeDtypeStruct(sh, BF16) for sh in shapes],
        compiler_params=pltpu.CompilerParams(vmem_limit_bytes=VMEM_LIMIT),
        name="weight_relayout",
    )(w_in_t, w_uq, w_ukv, w_gate_up)


def _prep_weights(w_in, g_norm1, g_q_lora, w_uq, g_kv_lora, w_ukv, g_q_head, g_k_head,
                  w_gate_up, b_gate_up, g_gla_out, w_out, g_norm2, w_up, w_down):
    d = w_in.shape[0]
    w_in_p, w_uq_p, w_ukv_p, w_ukt, w_uv, w_gate = _relayout(w_in.T, w_uq, w_ukv, w_gate_up)
    qscale = QK_DIM ** -0.5 * LOG2E
    bound = 1.02 * QK_DIM ** 0.5 * LOG2E * jnp.max(jnp.abs(g_q_head)) * jnp.max(jnp.abs(g_k_head))
    lane = jnp.arange(LANES) == QK_ROPE
    return {
        "w_in": w_in_p, "g_norm1": g_norm1.reshape(1, d), "g_q_lora": g_q_lora.reshape(1, Q_LORA),
        "w_uq": w_uq_p, "g_kv_lora": g_kv_lora.reshape(1, KV_LORA), "w_ukv": w_ukv_p, "w_ukt": w_ukt, "w_uv": w_uv,
        "gkr_col": g_k_head[QK_NOPE:].reshape(QK_ROPE, 1),
        "qone": lane.astype(F32).reshape(1, LANES), "kbias": jnp.where(lane, -bound, 0.0).reshape(1, LANES),
        "fast_softmax": (bound <= MAX_FIXED_SHIFT).astype(jnp.int32).reshape(1), "bound": bound.reshape(1),
        "gqn": (g_q_head[:QK_NOPE] * qscale).reshape(1, LANES), "gqr": _pad_gain(g_q_head[QK_NOPE:] * qscale),
        "gkn": g_k_head[:QK_NOPE].reshape(1, LANES), "gkr": _pad_gain(g_k_head[QK_NOPE:]),
        "w_gate": w_gate, "b_gate": b_gate_up.reshape(1, GLA_QK),
        "g_gla_out": g_gla_out.reshape(1, GLA_WIDTH), "w_out": w_out.astype(BF16),
        "g_norm2": g_norm2.reshape(1, d), "w_up": w_up.astype(BF16), "w_down": w_down.astype(BF16),
    }


def _mixers(x, mod, past_lat, past_kr, s0, w, *, tm):
    batch, seq, d = x.shape
    n = batch * seq
    past = 0 if past_lat is None else past_lat.shape[1]
    x2 = x.reshape(n, d)
    mod4 = mod.reshape(batch, 6, 1, d)
    tm = min(tm, n)
    tab = _rope_table(past, seq, repeat=max(1, tm // seq))
    if past == 0:
        lat, krt, q, k, v, gq, gk, gv, la, gr = _projection(x2, mod4, tab, w, rows_per_group=seq, tm=tm, prompt=True)
        kr = jnp.swapaxes(krt, 1, 2)
        attend = functools.partial(_attention_prompt, q, k, v, batch=batch, seq=seq, tq=min(ATTN_TILE, seq))
        a_out = lax.cond(w["fast_softmax"][0] == 1, functools.partial(attend, fast=True),
                         functools.partial(attend, fast=False))
    else:
        assert seq == CHUNK and past % CHUNK == 0
        lat, kr, krt, q, gq, gk, gv, la, gr = _projection(x2, mod4, tab, w, rows_per_group=seq, tm=tm, prompt=False)
        attend = functools.partial(_attention_sample, q, past_lat, jnp.swapaxes(past_kr, 1, 2), lat, krt, w,
                                   tkb=min(CACHE_BLOCK, past))
        a_out = lax.cond(w["fast_softmax"][0] == 1, functools.partial(attend, fast=True),
                         functools.partial(attend, fast=False))
    b_out, s_new = _gla(gq, gk, gv, la, gr, s0, w["g_gla_out"], groups=batch, rows_per_group=seq,
                        tc=tm)
    return (x2, a_out, b_out, mod4, seq), (lat.reshape(batch, seq, KV_LORA), kr.reshape(batch, seq, QK_ROPE), s_new)


def kernel(x_prompt, x_sample, cache_mla_latent, cache_mla_krope, state_gla, c_prompt, c_sample,
           w_ada, b_ada, g_norm1, w_in, g_q_lora, w_uq, g_kv_lora, w_ukv, g_q_head, g_k_head,
           w_gate_up, b_gate_up, g_gla_out, w_out, g_norm2, w_up, w_down):
    nb = x_prompt.shape[0]
    depth = w_ada.shape[0]
    y_p, y_s = x_prompt, x_sample
    outs = [[] for _ in range(6)]
    for l in range(depth):
        w = _prep_weights(w_in[l], g_norm1[l], g_q_lora[l], w_uq[l], g_kv_lora[l], w_ukv[l], g_q_head[l],
                          g_k_head[l], w_gate_up[l], b_gate_up[l], g_gla_out[l], w_out[l], g_norm2[l],
                          w_up[l], w_down[l])
        mod_p, mod_s = _modulation(c_prompt, c_sample, w_ada[l], b_ada[l])
        zero_state = jnp.zeros((nb, GLA_HEADS, GLA_DK, GLA_DV), x_prompt.dtype)
        mix_p, new_p = _mixers(y_p, mod_p, None, None, zero_state, w, tm=ROW_TILE)
        mix_s, new_s = _mixers(y_s, mod_s, cache_mla_latent[l], cache_mla_krope[l], state_gla[l], w, tm=ROW_TILE)
        y2_p, y2_s = _mlp([mix_p, mix_s], w, tm=ROW_TILE)
        y_p, y_s = y2_p.reshape(y_p.shape), y2_s.reshape(y_s.shape)
        for o, new in zip(outs, new_p + new_s):
            o.append(new)
    return (y_p, y_s) + tuple(jnp.stack(o) for o in outs)
```

```python
import functools

import jax
import jax.numpy as jnp
import numpy as np
from jax import lax
from jax.experimental import pallas as pl
from jax.experimental.pallas import tpu as pltpu

F32 = jnp.float32
BF16 = jnp.bfloat16

CHUNK = 64
EPS = 1e-6
MLA_HEADS = 4
Q_LORA = 384
KV_LORA = 256
QK_NOPE = 128
QK_ROPE = 64
QK_DIM = QK_NOPE + QK_ROPE
V_DIM = 128
ROPE_THETA = 10000.0
GLA_HEADS = 4
GLA_DK = 64
GLA_DV = 128
GLA_GATE_RANK = 16
GLA_TAU = 16.0
GLA_QK = GLA_HEADS * GLA_DK
GLA_WIDTH = GLA_HEADS * GLA_DV
MLA_WIDTH = MLA_HEADS * V_DIM
HEAD_PAD = 256
SUB = 8
LOG2E = 1.4426950408889634
MAX_FIXED_SHIFT = -1.0

LANES = 128
VMEM_LIMIT = 56 * 1024 * 1024
ROW_TILE = 512
ATTN_TILE = 512
CACHE_BLOCK = 512
FF_SLICE = 1024
MOD_TILE = 1024

C_QKR = (0, 512)
C_KV = (512, 768)
C_GQ = (768, 1024)
C_GK = (1024, 1280)
C_GV = (1280, 1792)
C_GR = (1792, 2304)


def _dot(a, b):
    return jnp.dot(a, b, preferred_element_type=F32)


def _dot_nt(a, b):
    return lax.dot_general(a, b, (((1,), (1,)), ((), ())), preferred_element_type=F32)


def _dot_tn(a, b):
    return lax.dot_general(a, b, (((0,), (0,)), ((), ())), preferred_element_type=F32)


def _rope_tile(t, c, sa, sb):
    return t * c + pltpu.roll(t, 96, 1) * sa + pltpu.roll(t, 32, 1) * sb


def _params(*sem):
    return pltpu.CompilerParams(dimension_semantics=sem, vmem_limit_bytes=VMEM_LIMIT)


def _const(shape):
    return pl.BlockSpec(shape, lambda *_: (0,) * len(shape), pipeline_mode=pl.Buffered(1))


def _mod_kernel(cs_ref, cp_ref, w_ref, b_ref, os_ref, op_ref):
    ns = cs_ref.shape[0]
    c = jnp.concatenate([cs_ref[...], cp_ref[...]], axis=0)
    s = (c * jax.nn.sigmoid(c)).astype(BF16)
    mod = _dot(s, w_ref[...].astype(BF16)) + b_ref[...]
    os_ref[...] = mod[0:ns]
    op_ref[...] = mod[ns:]


def _modulation(c_prompt, c_sample, w_ada, b_ada):
    d, n = w_ada.shape
    tn = MOD_TILE

    def whole(c):
        return pl.BlockSpec(c.shape, lambda j: (0, 0))

    def cols(c):
        return pl.BlockSpec((c.shape[0], tn), lambda j: (0, j))

    mod_s, mod_p = pl.pallas_call(
        _mod_kernel,
        grid=(n // tn,),
        in_specs=[whole(c_sample), whole(c_prompt),
                  pl.BlockSpec((d, tn), lambda j: (0, j)),
                  pl.BlockSpec((1, tn), lambda j: (0, j))],
        out_specs=[cols(c_sample), cols(c_prompt)],
        out_shape=[jax.ShapeDtypeStruct((c_sample.shape[0], n), F32), jax.ShapeDtypeStruct((c_prompt.shape[0], n), F32)],
        compiler_params=_params("arbitrary"),
        name="adaln_mod",
    )(c_sample, c_prompt, w_ada, b_ada.reshape(1, n))
    return mod_p, mod_s


def _proj_kernel(x_ref, sh_ref, sc_ref, g1_ref, win_ref, gql_ref, wuq_ref, gkv_ref, wukv_ref,
                 gqn_ref, gqr_ref, gkn_ref, gkr_ref, qone_ref, kbias_ref, tab_ref, wg_ref, bg_ref,
                 *outs, gpt, prompt):
    if prompt:
        lat_ref, krt_ref, q_ref, k_ref, v_ref, gq_ref, gk_ref, gv_ref, la_ref, gr_ref = outs
    else:
        lat_ref, kr_ref, krt_ref, q_ref, gq_ref, gk_ref, gv_ref, la_ref, gr_ref = outs
    tm, d = x_ref.shape
    x = x_ref[...]
    xn = x * lax.rsqrt(jnp.mean(x * x, axis=-1, keepdims=True) + EPS)
    h = (xn.reshape(gpt, tm // gpt, d) * (g1_ref[...] * (1.0 + sc_ref[...])) + sh_ref[...]).reshape(tm, d)
    hb = h.astype(BF16)

    def col(c):
        return _dot(hb, win_ref[:, c[0]:c[1]])

    tab = tab_ref[...]
    cos, sa, sb = tab[:, 0:LANES], tab[:, LANES:2 * LANES], tab[:, 2 * LANES:3 * LANES]

    qkr = col(C_QKR)
    krg = qkr[:, Q_LORA:]
    z = _dot(krg.astype(BF16), wg_ref[...]) + bg_ref[...]
    la_ref[...] = (jnp.minimum(z, 0.0) - jnp.log(1.0 + jnp.exp(-jnp.abs(z)))) * (1.0 / GLA_TAU)
    r = col(C_GR)
    gr_ref[...] = (r * jax.nn.sigmoid(r)).astype(BF16)

    cq = qkr[:, 0:Q_LORA]
    cqn = cq * lax.rsqrt(jnp.mean(cq * cq, axis=-1, keepdims=True) + EPS) * gql_ref[...]
    qp = _dot(cqn.astype(BF16), wuq_ref[...])
    for hh in range(MLA_HEADS):
        nope = qp[:, hh * HEAD_PAD:hh * HEAD_PAD + QK_NOPE]
        rt = qp[:, hh * HEAD_PAD + QK_NOPE:(hh + 1) * HEAD_PAD]
        ss = jnp.sum(nope * nope, axis=-1, keepdims=True) + jnp.sum(rt * rt, axis=-1, keepdims=True)
        inv = lax.rsqrt(ss * (1.0 / QK_DIM) + EPS)
        q_ref[:, hh * HEAD_PAD:hh * HEAD_PAD + QK_NOPE] = (nope * inv * gqn_ref[...]).astype(BF16)
        rq = _rope_tile(rt * inv * gqr_ref[...], cos, sa, sb)
        q_ref[:, hh * HEAD_PAD + QK_NOPE:(hh + 1) * HEAD_PAD] = (rq + qone_ref[...]).astype(BF16)

    ckv = col(C_KV)
    lat = ckv * lax.rsqrt(jnp.mean(ckv * ckv, axis=-1, keepdims=True) + EPS) * gkv_ref[...]
    lat_ref[...] = lat
    krt_ref[...] = krg.T[0:QK_ROPE, :]
    if prompt:
        lane = lax.broadcasted_iota(jnp.int32, (1, LANES), 1)
        krm = jnp.where(lane < QK_ROPE, krg, 0.0)
        ssr = jnp.sum(krm * krm, axis=-1, keepdims=True)
        rk = _rope_tile(krm * gkr_ref[...], cos, sa, sb)
        kv = _dot(lat.astype(BF16), wukv_ref[...])
        for hh in range(MLA_HEADS):
            kn = kv[:, hh * QK_NOPE:(hh + 1) * QK_NOPE]
            inv = lax.rsqrt((jnp.sum(kn * kn, axis=-1, keepdims=True) + ssr) * (1.0 / QK_DIM) + EPS)
            k_ref[:, hh * HEAD_PAD:hh * HEAD_PAD + QK_NOPE] = (kn * inv * gkn_ref[...]).astype(BF16)
            k_ref[:, hh * HEAD_PAD + QK_NOPE:(hh + 1) * HEAD_PAD] = (rk * inv + kbias_ref[...]).astype(BF16)
        v_ref[...] = kv[:, MLA_HEADS * QK_NOPE:].astype(BF16)
    else:
        kr_ref[...] = krg[:, 0:QK_ROPE]

    gq_ref[...] = col(C_GQ)
    gk_ref[...] = col(C_GK)
    gv_ref[...] = col(C_GV).astype(BF16)


def _projection(x2, mod4, tab, w, *, rows_per_group, tm, prompt):
    n, d = x2.shape
    gpt = max(1, tm // rows_per_group)
    tpg = max(1, rows_per_group // tm)
    ntab = tab.shape[0] // tm

    def mod_spec(j):
        return pl.BlockSpec((gpt, None, 1, d), lambda i: ((i // tpg) if gpt == 1 else i, j, 0, 0))

    def rows(c):
        return pl.BlockSpec((tm, c), lambda i: (i, 0))

    def out(c, t):
        return rows(c), jax.ShapeDtypeStruct((n, c), t)

    gla_outs = [out(GLA_QK, F32), out(GLA_QK, F32), out(GLA_WIDTH, BF16), out(GLA_QK, F32), out(GLA_WIDTH, BF16)]
    qo = out(MLA_HEADS * HEAD_PAD, BF16)
    if prompt:
        krt = (pl.BlockSpec((None, QK_ROPE, tm), lambda i: (i // tpg, 0, i % tpg)),
               jax.ShapeDtypeStruct((n // rows_per_group, QK_ROPE, rows_per_group), F32))
        outs = [out(KV_LORA, F32), krt, qo, out(MLA_HEADS * HEAD_PAD, BF16), out(MLA_WIDTH, BF16)] + gla_outs
    else:
        krt = (pl.BlockSpec((QK_ROPE, tm), lambda i: (0, i)), jax.ShapeDtypeStruct((QK_ROPE, n), F32))
        outs = [out(KV_LORA, F32), out(QK_ROPE, F32), krt, qo] + gla_outs
    return pl.pallas_call(
        functools.partial(_proj_kernel, gpt=gpt, prompt=prompt),
        grid=(n // tm,),
        in_specs=[rows(d), mod_spec(0), mod_spec(1), _const((1, d)), _const(w["w_in"].shape),
                  _const((1, Q_LORA)), _const(w["w_uq"].shape), _const((1, KV_LORA)), _const(w["w_ukv"].shape),
                  _const((1, LANES)), _const((1, LANES)), _const((1, LANES)), _const((1, LANES)),
                  _const((1, LANES)), _const((1, LANES)),
                  pl.BlockSpec((tm, 3 * LANES), lambda i: (i % ntab, 0)),
                  _const(w["w_gate"].shape), _const((1, GLA_QK))],
        out_specs=[o[0] for o in outs],
        out_shape=[o[1] for o in outs],
        compiler_params=_params("arbitrary"),
        name="in_proj",
    )(x2, mod4, mod4, w["g_norm1"], w["w_in"], w["g_q_lora"], w["w_uq"], w["g_kv_lora"], w["w_ukv"],
      w["gqn"], w["gqr"], w["gkn"], w["gkr"], w["qone"], w["kbias"], tab, w["w_gate"], w["b_gate"])


def _attn_prompt_kernel(q_ref, k_ref, v_ref, o_ref, vx_ref, m_ref, acc_ref, *, tq, fast):
    i = pl.program_id(1)
    heads = range(MLA_HEADS)

    @pl.when(i == 0)
    def _():
        for hh in heads:
            vx_ref[hh, :, 0:V_DIM] = v_ref[:, hh * V_DIM:(hh + 1) * V_DIM]
            vx_ref[hh, :, V_DIM:] = jnp.ones((v_ref.shape[0], V_DIM), BF16)

    def scores(j, hh, masked):
        hs = slice(hh * HEAD_PAD, (hh + 1) * HEAD_PAD)
        s = _dot_nt(q_ref[:, hs], k_ref[pl.ds(pl.multiple_of(j * tq, tq), tq), hs])
        if masked:
            qc = lax.broadcasted_iota(jnp.int32, (tq, tq), 0) // CHUNK
            kc = lax.broadcasted_iota(jnp.int32, (tq, tq), 1) // CHUNK
            s = jnp.where(kc <= qc, s, -jnp.inf)
        return s

    def fast_diagonal():
        hq = tq // 2
        k0 = pl.multiple_of(i * tq, tq)
        qc = lax.broadcasted_iota(jnp.int32, (hq, hq), 0) // CHUNK
        kc = lax.broadcasted_iota(jnp.int32, (hq, hq), 1) // CHUNK
        tri = kc <= qc
        for hh in heads:
            hs = slice(hh * HEAD_PAD, (hh + 1) * HEAD_PAD)
            top = jnp.where(tri, _dot_nt(q_ref[0:hq, hs], k_ref[pl.ds(k0, hq), hs]), -jnp.inf)
            acc_ref[hh, 0:hq] = _dot(jnp.exp2(top).astype(BF16), vx_ref[hh, pl.ds(k0, hq), :])
            bot = _dot_nt(q_ref[hq:tq, hs], k_ref[pl.ds(k0, tq), hs])
            bot = jnp.concatenate([bot[:, 0:hq], jnp.where(tri, bot[:, hq:tq], -jnp.inf)], axis=1)
            acc_ref[hh, hq:tq] = _dot(jnp.exp2(bot).astype(BF16), vx_ref[hh, pl.ds(k0, tq), :])

    def vblock(j, hh):
        return vx_ref[hh, pl.ds(pl.multiple_of(j * tq, tq), tq), :]

    def fast_block(j, masked):
        for hh in heads:
            acc_ref[hh] += _dot(jnp.exp2(scores(j, hh, masked)).astype(BF16), vblock(j, hh))

    def safe_block(j, masked):
        for hh in heads:
            s = scores(j, hh, masked)
            m = m_ref[hh]
            m_new = jnp.maximum(m, jnp.max(s, axis=1, keepdims=True))
            p = jnp.exp2(s - m_new).astype(BF16)
            acc_ref[hh] = jnp.exp2(m - m_new) * acc_ref[hh] + _dot(p, vblock(j, hh))
            m_ref[hh] = m_new

    def sweep(block):
        def body(jj, c):
            block(2 * jj, False)
            block(2 * jj + 1, False)
            return c

        lax.fori_loop(0, i // 2, body, 0)

        @pl.when(i % 2 == 1)
        def _():
            block(i - 1, False)

    if fast:
        fast_diagonal()
        sweep(fast_block)
    else:
        m_ref[...] = jnp.full(m_ref.shape, -jnp.inf, F32)
        acc_ref[...] = jnp.zeros(acc_ref.shape, F32)
        safe_block(i, True)
        sweep(safe_block)

    for hh in heads:
        acc = acc_ref[hh]
        o_ref[:, hh * V_DIM:(hh + 1) * V_DIM] = (acc[:, 0:V_DIM] / acc[:, V_DIM:]).astype(BF16)


def _attention_prompt(q, k, v, *, batch, seq, tq, fast):
    nq = seq // tq
    return pl.pallas_call(
        functools.partial(_attn_prompt_kernel, tq=tq, fast=fast),
        grid=(batch, nq),
        in_specs=[pl.BlockSpec((tq, MLA_HEADS * HEAD_PAD), lambda b, i: (b * nq + i, 0)),
                  pl.BlockSpec((seq, MLA_HEADS * HEAD_PAD), lambda b, i: (b, 0)),
                  pl.BlockSpec((seq, MLA_WIDTH), lambda b, i: (b, 0))],
        out_specs=pl.BlockSpec((tq, MLA_WIDTH), lambda b, i: (b * nq + i, 0)),
        out_shape=jax.ShapeDtypeStruct((batch * seq, MLA_WIDTH), BF16),
        scratch_shapes=[pltpu.VMEM((MLA_HEADS, seq, 2 * V_DIM), BF16), pltpu.VMEM((MLA_HEADS, tq, 1), F32),
                        pltpu.VMEM((MLA_HEADS, tq, 2 * V_DIM), F32)],
        compiler_params=_params("arbitrary", "arbitrary"),
        name="mla_attn_prompt",
    )(q, k, v)


def _attn_sample_kernel(bound_ref, q_ref, lat_ref, krt_ref, latn_ref, krtn_ref, tab_ref, wukt_ref, wuv_ref,
                        gkn_ref, gkr_ref, o_ref, wq_ref, latb_ref, s_ref, *, tkb, fast):
    t = q_ref.shape[0] // 2
    past = lat_ref.shape[1]
    nk = MLA_HEADS * QK_NOPE
    half = QK_ROPE // 2
    streams = range(2)

    @pl.when(pl.program_id(0) == 0)
    def _():
        for e in streams:
            wq_ref[e, 0:nk, :] = wukt_ref[...]

    qr = []
    for e in streams:
        rows = slice(e * t, (e + 1) * t)
        parts = []
        for hh in range(MLA_HEADS):
            qn = (q_ref[rows, hh * HEAD_PAD:hh * HEAD_PAD + QK_NOPE].astype(F32) * gkn_ref[...]).astype(BF16)
            wq_ref[e, nk + hh * t:nk + (hh + 1) * t, :] = _dot(
                qn, wukt_ref[hh * QK_NOPE:(hh + 1) * QK_NOPE, :]).astype(BF16)
            parts.append(q_ref[rows, hh * HEAD_PAD + QK_NOPE:(hh + 1) * HEAD_PAD])
        qr.append(jnp.concatenate(parts, axis=0))

    def key_block(e, lat, krt, tab, width, valid):
        latb = lat.astype(BF16)
        g = _dot_nt(wq_ref[e], latb)
        ssr = jnp.sum(krt * krt, axis=0, keepdims=True)
        kg = krt * gkr_ref[...]
        x1, x2 = kg[0:half], kg[half:QK_ROPE]
        c, sn = tab[0:half], tab[half:QK_ROPE]
        rope = jnp.concatenate([x1 * c - x2 * sn, x2 * c + x1 * sn, jnp.zeros((LANES - QK_ROPE, width), F32)], axis=0)
        srope = _dot(qr[e], rope.astype(BF16))
        rows = []
        for hh in range(MLA_HEADS):
            kn = g[hh * QK_NOPE:(hh + 1) * QK_NOPE]
            inv = lax.rsqrt((jnp.sum(kn * kn, axis=0, keepdims=True) + ssr) * (1.0 / QK_DIM) + EPS)
            rows.append((g[nk + hh * t:nk + (hh + 1) * t] + srope[hh * t:(hh + 1) * t]) * inv)
        sc = jnp.concatenate(rows, axis=0) - bound_ref[0]
        if valid is not None:
            sc = jnp.where(valid, sc, -jnp.inf)
        return sc, latb

    def blocks():
        for blk in range(past // tkb):
            c0 = blk * tkb
            for e in streams:
                yield e, c0, tkb, key_block(e, lat_ref[e, c0:c0 + tkb, :], krt_ref[e, :, c0:c0 + tkb],
                                            tab_ref[:, c0:c0 + tkb], tkb, None)
        for e in streams:
            mine = lax.broadcasted_iota(jnp.int32, (1, 2 * t), 1) // t == e
            yield e, past, 2 * t, key_block(e, latn_ref[...], krtn_ref[...], tab_ref[:, past:past + 2 * t], 2 * t, mine)

    def finish(e, wlat, l):
        wlat = wlat.astype(BF16)
        for hh in range(MLA_HEADS):
            o = _dot(wlat[hh * t:(hh + 1) * t], wuv_ref[:, hh * V_DIM:(hh + 1) * V_DIM])
            o_ref[e * t:(e + 1) * t, hh * V_DIM:(hh + 1) * V_DIM] = (o / l[hh * t:(hh + 1) * t]).astype(BF16)

    if fast:
        wlat = [jnp.zeros((MLA_HEADS * t, KV_LORA), F32) for _ in streams]
        lsum = [jnp.zeros((MLA_HEADS * t, LANES), F32) for _ in streams]
        for e, _, width, (sc, latb) in blocks():
            p = jnp.exp2(sc)
            wlat[e] += _dot(p.astype(BF16), latb)
            for c in range(width // LANES):
                lsum[e] += p[:, c * LANES:(c + 1) * LANES]
        for e in streams:
            finish(e, wlat[e], jnp.sum(lsum[e], axis=1, keepdims=True))

    else:
        for e, c0, width, (sc, latb) in blocks():
            s_ref[e, :, c0:c0 + width] = sc
            latb_ref[e, c0:c0 + width, :] = latb
        for e in streams:
            s = s_ref[e]
            p = jnp.exp2(s - jnp.max(s, axis=1, keepdims=True))
            finish(e, _dot(p.astype(BF16), latb_ref[e]), jnp.sum(p, axis=1, keepdims=True))


def _attention_sample(q, past_lat, past_krt, lat_new, krt_new, w, *, tkb, fast):
    batch, past, _ = past_lat.shape
    t = q.shape[0] // batch
    half = QK_ROPE // 2
    inv = ROPE_THETA ** (-np.arange(half, dtype=np.float64) / half)
    pos = np.concatenate([np.arange(past), past + np.arange(t), past + np.arange(t)]).astype(np.float64)
    ang = inv[:, None] * pos[None, :]
    tab = jnp.asarray(np.concatenate([np.cos(ang), np.sin(ang)], axis=0).astype(np.float32))
    s_pad = past + 2 * t
    return pl.pallas_call(
        functools.partial(_attn_sample_kernel, tkb=tkb, fast=fast),
        grid=(batch // 2,),
        in_specs=[pl.BlockSpec(memory_space=pltpu.SMEM),
                  pl.BlockSpec((2 * t, MLA_HEADS * HEAD_PAD), lambda g: (g, 0)),
                  pl.BlockSpec((2, past, KV_LORA), lambda g: (g, 0, 0)),
                  pl.BlockSpec((2, QK_ROPE, past), lambda g: (g, 0, 0)),
                  pl.BlockSpec((2 * t, KV_LORA), lambda g: (g, 0)),
                  pl.BlockSpec((QK_ROPE, 2 * t), lambda g: (0, g)),
                  _const(tab.shape), _const(w["w_ukt"].shape), _const(w["w_uv"].shape),
                  _const((1, LANES)), _const((QK_ROPE, 1))],
        out_specs=pl.BlockSpec((2 * t, MLA_WIDTH), lambda g: (g, 0)),
        out_shape=jax.ShapeDtypeStruct((batch * t, MLA_WIDTH), BF16),
        scratch_shapes=[pltpu.VMEM((2, MLA_HEADS * (QK_NOPE + t), KV_LORA), BF16),
                        pltpu.VMEM((2, s_pad, KV_LORA), BF16),
                        pltpu.VMEM((2, MLA_HEADS * t, s_pad), F32)],
        compiler_params=_params("arbitrary"),
        name="mla_attn_sample",
    )(w["bound"], q, past_lat, past_krt, lat_new, krt_new, tab, w["w_ukt"], w["w_uv"], w["gkn"], w["gkr_col"])


def _gla_kernel(q_ref, k_ref, v_ref, la_ref, r_ref, s0_ref, g_ref, spread_ref, o_ref, sn_ref,
                st_ref, kp_ref, ap_ref, p_ref, on_ref, *, gpt):
    t_idx = pl.program_id(1)
    L, W, P = CHUNK, GLA_QK, LANES
    R = q_ref.shape[0]
    n_chunks = R // L
    cpg = n_chunks // gpt
    n_pairs = GLA_HEADS // 2

    lane_p = lax.broadcasted_iota(jnp.int32, (1, P), 1)
    even = lane_p < GLA_DK
    bd_mask = (lax.broadcasted_iota(jnp.int32, (2 * GLA_DV, P), 0) // GLA_DV
               == lax.broadcasted_iota(jnp.int32, (2 * GLA_DV, P), 1) // GLA_DK)

    @pl.when(t_idx == 0)
    def _():
        kp_ref[0:SUB, :] = jnp.zeros((SUB, W), F32)
        ap_ref[0:SUB, :] = jnp.zeros((SUB, W), F32)
        for gi in range(gpt):
            for pr in range(n_pairs):
                tt = s0_ref[gi, 2 * pr:2 * pr + 2].reshape(2 * GLA_DK, GLA_DV).T
                st_ref[gi, pr] = jnp.where(bd_mask, jnp.concatenate([tt, tt], axis=0), 0.0)

    q = q_ref[...]
    k = k_ref[...]
    la = la_ref[...]

    tri = (lax.broadcasted_iota(jnp.int32, (L, L), 0) >= lax.broadcasted_iota(jnp.int32, (L, L), 1)).astype(BF16)
    la_hi = la.astype(BF16)
    la2 = jnp.concatenate([la_hi, (la - la_hi.astype(F32)).astype(BF16)], axis=1)
    bs = []
    for c in range(n_chunks):
        t2 = _dot(tri, la2[c * L:(c + 1) * L, :])
        bs.append(t2[:, 0:W] + t2[:, W:2 * W])
    b = (jnp.concatenate(bs, axis=0) if n_chunks > 1 else bs[0]) * LOG2E
    b3 = b.reshape(n_chunks, L, W)

    def chunk_row(r):
        return jnp.broadcast_to(b3[:, r:r + 1, :], (n_chunks, L, W)).reshape(R, W)

    b_sub = jnp.broadcast_to(b.reshape(R // SUB, SUB, W)[:, 0:1, :], (R // SUB, SUB, W)).reshape(R, W)
    sub = (lax.broadcasted_iota(jnp.int32, (R, W), 0) % L) // SUB

    qt = q * jnp.exp2(b - b_sub)
    zb = jnp.zeros((), BF16)
    zq = jnp.zeros((SUB, P), F32)
    ktm = []
    for i in range(1, L // SUB):
        kt = (k * jnp.exp2(chunk_row(i * SUB) - b)).astype(BF16)
        ktm.append([[jnp.where((sub[:, 0:P] < i) & (even if e == 0 else ~even), kt[:, pr * P:(pr + 1) * P], zb)
                     for e in range(2)] for pr in range(n_pairs)])

    a = jnp.exp(la)
    kp_ref[SUB:SUB + R, :] = k
    ap_ref[SUB:SUB + R, :] = a
    p_ref[:, 0:W] = (q * k).astype(BF16)
    e = a
    for d in range(1, SUB):
        if d > 1:
            e = e * ap_ref[SUB - d + 1:SUB - d + 1 + R, :]
        p_ref[:, d * W:(d + 1) * W] = (q * kp_ref[SUB - d:SUB - d + R, :] * e).astype(BF16)
    cband = _dot(p_ref[...], spread_ref[...])
    same_sub = (lax.broadcasted_iota(jnp.int32, (L, W), 0) // SUB
                == (lax.broadcasted_iota(jnp.int32, (L, W), 1) % L) // SUB)

    qe = (q * jnp.exp2(b)).astype(BF16)
    kd = (k * jnp.exp2(chunk_row(L - 1) - b)).astype(BF16)
    zv = jnp.zeros((L, 2 * GLA_DV), BF16)

    o_intra, d_st, dec = {}, {}, []
    for c in range(n_chunks):
        rs = slice(c * L, (c + 1) * L)
        a_band = jnp.where(same_sub, pltpu.roll(cband[rs], W - (SUB - 1), 1, stride=1, stride_axis=0), 0.0)
        dec.append(jnp.exp2(b[c * L + L - 1:c * L + L, :]))
        for pr in range(n_pairs):
            ls = slice(pr * P, (pr + 1) * P)
            lhs_c = jnp.concatenate(
                [jnp.concatenate([qt[c * L + r * SUB:c * L + (r + 1) * SUB, ls] if i == r else zq
                                  for i in range(1, L // SUB)], axis=1) for r in range(L // SUB)],
                axis=0).astype(BF16)
            rhs_c = jnp.concatenate([jnp.concatenate([m[pr][0][rs], m[pr][1][rs]], axis=0) for m in ktm], axis=1)
            a_tot = (a_band[:, ls] + _dot_nt(lhs_c, rhs_c)).astype(BF16)
            vp = v_ref[rs, 2 * pr * GLA_DV:(2 * pr + 2) * GLA_DV]
            v_bd = jnp.concatenate([jnp.concatenate([vp[:, 0:GLA_DV], zv[:, 0:GLA_DV]], axis=1),
                                    jnp.concatenate([zv[:, 0:GLA_DV], vp[:, GLA_DV:]], axis=1)], axis=0)
            o_intra[c, pr] = _dot(a_tot, v_bd)
            d_st[c, pr] = jnp.where(bd_mask, _dot_tn(vp, kd[rs, ls]), 0.0)
    st_in = {}
    for gi in range(gpt):
        for pr in range(n_pairs):
            st = st_ref[gi, pr]
            for c in range(gi * cpg, (gi + 1) * cpg):
                st_in[c, pr] = st.astype(BF16)
                st = st * dec[c][:, pr * P:(pr + 1) * P] + d_st[c, pr]
            st_ref[gi, pr] = st
    for c in range(n_chunks):
        rs = slice(c * L, (c + 1) * L)
        for pr in range(n_pairs):
            on_ref[rs, 2 * pr * GLA_DV:(2 * pr + 2) * GLA_DV] = (
                o_intra[c, pr] + _dot_nt(qe[rs, pr * P:(pr + 1) * P], st_in[c, pr]))

    for hh in range(GLA_HEADS):
        hs = slice(hh * GLA_DV, (hh + 1) * GLA_DV)
        o = on_ref[:, hs]
        on = o * lax.rsqrt(jnp.mean(o * o, axis=-1, keepdims=True) + EPS) * g_ref[:, hs]
        o_ref[:, hs] = (on * r_ref[:, hs].astype(F32)).astype(BF16)

    @pl.when(t_idx == pl.num_programs(1) - 1)
    def _():
        for gi in range(gpt):
            for pr in range(n_pairs):
                st = st_ref[gi, pr]
                tt = jnp.where(even, st[0:GLA_DV], st[GLA_DV:2 * GLA_DV])
                sn_ref[gi, 2 * pr:2 * pr + 2] = tt.T.reshape(2, GLA_DK, GLA_DV)


def _band_spread():
    m = np.zeros((SUB, GLA_HEADS, GLA_DK, GLA_QK), np.float32)
    for d in range(SUB):
        for h in range(GLA_HEADS):
            m[d, h, :, h * GLA_DK + SUB - 1 - d] = 1.0
    return jnp.asarray(m.reshape(SUB * GLA_QK, GLA_QK), BF16)


def _gla(gq, gk, gv, la, gr, s0, g_out, *, groups, rows_per_group, tc):
    gpt = max(1, tc // rows_per_group)
    nt = max(1, rows_per_group // tc)

    def rows(c):
        return pl.BlockSpec((tc, c), lambda g, t: (g * nt + t, 0))

    state = pl.BlockSpec((gpt, GLA_HEADS, GLA_DK, GLA_DV), lambda g, t: (g, 0, 0, 0))
    spread = _band_spread()
    return pl.pallas_call(
        functools.partial(_gla_kernel, gpt=gpt),
        grid=(groups // gpt, nt),
        in_specs=[rows(GLA_QK), rows(GLA_QK), rows(GLA_WIDTH), rows(GLA_QK), rows(GLA_WIDTH), state,
                  _const((1, GLA_WIDTH)), _const(spread.shape)],
        out_specs=[rows(GLA_WIDTH), state],
        out_shape=[jax.ShapeDtypeStruct((groups * rows_per_group, GLA_WIDTH), BF16),
                   jax.ShapeDtypeStruct((groups, GLA_HEADS, GLA_DK, GLA_DV), F32)],
        scratch_shapes=[pltpu.VMEM((gpt, GLA_HEADS // 2, 2 * GLA_DV, LANES), F32),
                        pltpu.VMEM((SUB + tc, GLA_QK), F32), pltpu.VMEM((SUB + tc, GLA_QK), F32),
                        pltpu.VMEM((tc, SUB * GLA_QK), BF16), pltpu.VMEM((tc, GLA_WIDTH), F32)],
        compiler_params=_params("arbitrary", "arbitrary"),
        name="gla",
    )(gq, gk, gv, la, gr, s0, g_out, spread)


def _mlp_tile(x_ref, a_ref, b_ref, g1_ref, sh2_ref, sc2_ref, g2_ref, gn_ref, wo_ref, wu_ref, wd_ref, y_ref, *, gpt):
    tm, d = x_ref.shape

    def per_group(val, ref, scale_plus_one=False):
        m = ref[...]
        if scale_plus_one:
            m = 1.0 + m
        return (val.reshape(gpt, tm // gpt, d) * m).reshape(tm, d)

    mix = jnp.concatenate([a_ref[...], b_ref[...]], axis=1)
    x1 = x_ref[...] + per_group(_dot(mix, wo_ref[...]), g1_ref)
    xn = x1 * lax.rsqrt(jnp.mean(x1 * x1, axis=-1, keepdims=True) + EPS) * gn_ref[...]
    h2 = (per_group(xn, sc2_ref, True).reshape(gpt, tm // gpt, d) + sh2_ref[...]).reshape(tm, d).astype(BF16)
    acc = jnp.zeros((tm, d), F32)
    for j in range(wu_ref.shape[1] // FF_SLICE):
        u = jnp.maximum(_dot(h2, wu_ref[:, j * FF_SLICE:(j + 1) * FF_SLICE]), 0.0)
        acc += _dot((u * u).astype(BF16), wd_ref[j * FF_SLICE:(j + 1) * FF_SLICE, :])
    y_ref[...] = x1 + per_group(acc, g2_ref)


def _mlp_kernel(*refs, tiles, gpts):
    per_phase = 7
    gn_ref, wo_ref, wu_ref, wd_ref = refs[len(tiles) * per_phase:len(tiles) * per_phase + 4]
    y_refs = refs[len(tiles) * per_phase + 4:]
    i = pl.program_id(0)
    first = 0
    for p, (n_tiles, gpt) in enumerate(zip(tiles, gpts)):
        ins = refs[p * per_phase:(p + 1) * per_phase]

        @pl.when(jnp.logical_and(i >= first, i < first + n_tiles))
        def _(ins=ins, p=p, gpt=gpt):
            _mlp_tile(*ins, gn_ref, wo_ref, wu_ref, wd_ref, y_refs[p], gpt=gpt)

        first += n_tiles


def _mlp(phases, w, *, tm):
    d = phases[0][0].shape[1]
    tiles = [ph[0].shape[0] // tm for ph in phases]
    gpts = [max(1, tm // ph[4]) for ph in phases]
    in_specs, args, first = [], [], 0
    for (x2, a_out, b_out, mod4, rpg), n_tiles, gpt in zip(phases, tiles, gpts):
        tpg = max(1, rpg // tm)

        def tile(i, first=first, n_tiles=n_tiles):
            return jnp.clip(i - first, 0, n_tiles - 1)

        def rows(c, tile=tile):
            return pl.BlockSpec((tm, c), lambda i: (tile(i), 0))

        def mod_spec(j, tile=tile, gpt=gpt, tpg=tpg):
            return pl.BlockSpec((gpt, None, 1, d), lambda i: (tile(i) // tpg, j, 0, 0))

        in_specs += [rows(d), rows(MLA_WIDTH), rows(GLA_WIDTH), mod_spec(2), mod_spec(3), mod_spec(4), mod_spec(5)]
        args += [x2, a_out, b_out, mod4, mod4, mod4, mod4]
        first += n_tiles
    out_specs, first = [], 0
    for n_tiles in tiles:
        out_specs.append(pl.BlockSpec((tm, d), lambda i, first=first, n_tiles=n_tiles:
                                      (jnp.clip(i - first, 0, n_tiles - 1), 0)))
        first += n_tiles
    return pl.pallas_call(
        functools.partial(_mlp_kernel, tiles=tuple(tiles), gpts=tuple(gpts)),
        grid=(sum(tiles),),
        in_specs=in_specs + [_const((1, d)), _const(w["w_out"].shape), _const(w["w_up"].shape),
                             _const(w["w_down"].shape)],
        out_specs=out_specs,
        out_shape=[jax.ShapeDtypeStruct(ph[0].shape, F32) for ph in phases],
        compiler_params=_params("arbitrary"),
        name="out_proj_mlp",
    )(*args, w["g_norm2"], w["w_out"], w["w_up"], w["w_down"])


def _rope_table(start, count, repeat=1):
    half = QK_ROPE // 2
    inv = ROPE_THETA ** (-np.arange(half, dtype=np.float64) / half)
    ang = (start + np.arange(count, dtype=np.float64))[:, None] * inv[None, :]
    c, s, z = np.cos(ang), np.sin(ang), np.zeros_like(ang)
    tab = np.concatenate([c, c, z, z, -s, z, z, z, z, s, z, z], axis=1).astype(np.float32)
    return jnp.asarray(np.tile(tab, (repeat, 1)))


def _pad_gain(g_rope):
    return jnp.concatenate([g_rope, jnp.zeros((LANES - QK_ROPE,), F32)]).reshape(1, LANES)


def _relayout_kernel(wint_ref, wuq_ref, wukv_ref, wg_ref, win_o, wuq_o, wukv_o, wukt_o, wuv_o, wg_o):
    s = np.cumsum([0, Q_LORA, KV_LORA, QK_ROPE, GLA_QK, GLA_QK, GLA_WIDTH, GLA_GATE_RANK, GLA_WIDTH])

    def piece(i):
        return wint_ref[int(s[i]):int(s[i + 1]), :]

    d = wint_ref.shape[1]
    zeros = jnp.zeros((LANES - QK_ROPE - GLA_GATE_RANK, d), F32)
    win_o[:, 0:Q_LORA] = piece(0).T.astype(BF16)
    win_o[:, Q_LORA:C_QKR[1]] = jnp.concatenate([piece(2), piece(6), zeros], axis=0).T.astype(BF16)
    win_o[:, C_KV[0]:C_KV[1]] = piece(1).T.astype(BF16)
    win_o[:, C_GQ[0]:C_GQ[1]] = (piece(3).T * (GLA_DK ** -0.5)).astype(BF16)
    win_o[:, C_GK[0]:C_GK[1]] = piece(4).T.astype(BF16)
    win_o[:, C_GV[0]:C_GV[1]] = piece(5).T.astype(BF16)
    win_o[:, C_GR[0]:C_GR[1]] = piece(7).T.astype(BF16)

    zq = jnp.zeros((Q_LORA, HEAD_PAD - QK_DIM), BF16)
    kvw = QK_NOPE + V_DIM
    for hh in range(MLA_HEADS):
        wuq_o[:, hh * HEAD_PAD:hh * HEAD_PAD + QK_DIM] = wuq_ref[:, hh * QK_DIM:(hh + 1) * QK_DIM].astype(BF16)
        wuq_o[:, hh * HEAD_PAD + QK_DIM:(hh + 1) * HEAD_PAD] = zq
        uk = wukv_ref[:, hh * kvw:hh * kvw + QK_NOPE]
        uv = wukv_ref[:, hh * kvw + QK_NOPE:(hh + 1) * kvw].astype(BF16)
        wukv_o[:, hh * QK_NOPE:(hh + 1) * QK_NOPE] = uk.astype(BF16)
        wukv_o[:, (MLA_HEADS + hh) * V_DIM:(MLA_HEADS + hh + 1) * V_DIM] = uv
        wukt_o[hh * QK_NOPE:(hh + 1) * QK_NOPE, :] = uk.T.astype(BF16)
        wuv_o[:, hh * V_DIM:(hh + 1) * V_DIM] = uv

    wg_o[...] = jnp.zeros(wg_o.shape, BF16)
    wg_o[QK_ROPE:QK_ROPE + GLA_GATE_RANK, :] = wg_ref[...].astype(BF16)


def _relayout(w_in_t, w_uq, w_ukv, w_gate_up):
    d = w_in_t.shape[1]
    shapes = [(d, C_GR[1]), (Q_LORA, MLA_HEADS * HEAD_PAD), (KV_LORA, MLA_HEADS * (QK_NOPE + V_DIM)),
              (MLA_HEADS * QK_NOPE, KV_LORA), (KV_LORA, MLA_WIDTH), (LANES, GLA_QK)]
    return pl.pallas_call(
        _relayout_kernel,
        out_shape=[jax.ShapeDtypeStruct(sh, BF16) for sh in shapes],
        compiler_params=pltpu.CompilerParams(vmem_limit_bytes=VMEM_LIMIT),
        name="weight_relayout",
    )(w_in_t, w_uq, w_ukv, w_gate_up)


def _prep_weights(w_in, g_norm1, g_q_lora, w_uq, g_kv_lora, w_ukv, g_q_head, g_k_head,
                  w_gate_up, b_gate_up, g_gla_out, w_out, g_norm2, w_up, w_down):
    d = w_in.shape[0]
    w_in_p, w_uq_p, w_ukv_p, w_ukt, w_uv, w_gate = _relayout(w_in.T, w_uq, w_ukv, w_gate_up)
    qscale = QK_DIM ** -0.5 * LOG2E
    bound = 1.02 * QK_DIM ** 0.5 * LOG2E * jnp.max(jnp.abs(g_q_head)) * jnp.max(jnp.abs(g_k_head))
    lane = jnp.arange(LANES) == QK_ROPE
    return {
        "w_in": w_in_p, "g_norm1": g_norm1.reshape(1, d), "g_q_lora": g_q_lora.reshape(1, Q_LORA),
        "w_uq": w_uq_p, "g_kv_lora": g_kv_lora.reshape(1, KV_LORA), "w_ukv": w_ukv_p, "w_ukt": w_ukt, "w_uv": w_uv,
        "gkr_col": g_k_head[QK_NOPE:].reshape(QK_ROPE, 1),
        "qone": lane.astype(F32).reshape(1, LANES), "kbias": jnp.where(lane, -bound, 0.0).reshape(1, LANES),
        "fast_softmax": (bound <= MAX_FIXED_SHIFT).astype(jnp.int32).reshape(1), "bound": bound.reshape(1),
        "gqn": (g_q_head[:QK_NOPE] * qscale).reshape(1, LANES), "gqr": _pad_gain(g_q_head[QK_NOPE:] * qscale),
        "gkn": g_k_head[:QK_NOPE].reshape(1, LANES), "gkr": _pad_gain(g_k_head[QK_NOPE:]),
        "w_gate": w_gate, "b_gate": b_gate_up.reshape(1, GLA_QK),
        "g_gla_out": g_gla_out.reshape(1, GLA_WIDTH), "w_out": w_out.astype(BF16),
        "g_norm2": g_norm2.reshape(1, d), "w_up": w_up.astype(BF16), "w_down": w_down.astype(BF16),
    }


def _mixers(x, mod, past_lat, past_kr, s0, w, *, tm):
    batch, seq, d = x.shape
    n = batch * seq
    past = 0 if past_lat is None else past_lat.shape[1]
    x2 = x.reshape(n, d)
    mod4 = mod.reshape(batch, 6, 1, d)
    tm = min(tm, n)
    tab = _rope_table(past, seq, repeat=max(1, tm // seq))
    if past == 0:
        lat, krt, q, k, v, gq, gk, gv, la, gr = _projection(x2, mod4, tab, w, rows_per_group=seq, tm=tm, prompt=True)
        kr = jnp.swapaxes(krt, 1, 2)
        attend = functools.partial(_attention_prompt, q, k, v, batch=batch, seq=seq, tq=min(ATTN_TILE, seq))
        a_out = lax.cond(w["fast_softmax"][0] == 1, functools.partial(attend, fast=True),
                         functools.partial(attend, fast=False))
    else:
        assert seq == CHUNK and past % CHUNK == 0
        lat, kr, krt, q, gq, gk, gv, la, gr = _projection(x2, mod4, tab, w, rows_per_group=seq, tm=tm, prompt=False)
        attend = functools.partial(_attention_sample, q, past_lat, jnp.swapaxes(past_kr, 1, 2), lat, krt, w,
                                   tkb=min(CACHE_BLOCK, past))
        a_out = lax.cond(w["fast_softmax"][0] == 1, functools.partial(attend, fast=True),
                         functools.partial(attend, fast=False))
    b_out, s_new = _gla(gq, gk, gv, la, gr, s0, w["g_gla_out"], groups=batch, rows_per_group=seq,
                        tc=tm)
    return (x2, a_out, b_out, mod4, seq), (lat.reshape(batch, seq, KV_LORA), kr.reshape(batch, seq, QK_ROPE), s_new)


def kernel(x_prompt, x_sample, cache_mla_latent, cache_mla_krope, state_gla, c_prompt, c_sample,
           w_ada, b_ada, g_norm1, w_in, g_q_lora, w_uq, g_kv_lora, w_ukv, g_q_head, g_k_head,
           w_gate_up, b_gate_up, g_gla_out, w_out, g_norm2, w_up, w_down):
    nb = x_prompt.shape[0]
    depth = w_ada.shape[0]
    y_p, y_s = x_prompt, x_sample
    outs = [[] for _ in range(6)]
    for l in range(depth):
        w = _prep_weights(w_in[l], g_norm1[l], g_q_lora[l], w_uq[l], g_kv_lora[l], w_ukv[l], g_q_head[l],
                          g_k_head[l], w_gate_up[l], b_gate_up[l], g_gla_out[l], w_out[l], g_norm2[l],
                          w_up[l], w_down[l])
        mod_p, mod_s = _modulation(c_prompt, c_sample, w_ada[l], b_ada[l])
        zero_state = jnp.zeros((nb, GLA_HEADS, GLA_DK, GLA_DV), x_prompt.dtype)
        mix_p, new_p = _mixers(y_p, mod_p, None, None, zero_state, w, tm=ROW_TILE)
        mix_s, new_s = _mixers(y_s, mod_s, cache_mla_latent[l], cache_mla_krope[l], state_gla[l], w, tm=ROW_TILE)
        y2_p, y2_s = _mlp([mix_p, mix_s], w, tm=ROW_TILE)
        y_p, y_s = y2_p.reshape(y_p.shape), y2_s.reshape(y_s.shape)
        for o, new in zip(outs, new_p + new_s):
            o.append(new)
    return (y_p, y_s) + tuple(jnp.stack(o) for o in outs)
```

```python
import functools

import jax
import jax.numpy as jnp
import numpy as np
from jax import lax
from jax.experimental import pallas as pl
from jax.experimental.pallas import tpu as pltpu

F32 = jnp.float32
BF16 = jnp.bfloat16

CHUNK = 64
EPS = 1e-6
MLA_HEADS = 4
Q_LORA = 384
KV_LORA = 256
QK_NOPE = 128
QK_ROPE = 64
QK_DIM = QK_NOPE + QK_ROPE
V_DIM = 128
ROPE_THETA = 10000.0
GLA_HEADS = 4
GLA_DK = 64
GLA_DV = 128
GLA_GATE_RANK = 16
GLA_TAU = 16.0
GLA_QK = GLA_HEADS * GLA_DK
GLA_WIDTH = GLA_HEADS * GLA_DV
MLA_WIDTH = MLA_HEADS * V_DIM
HEAD_PAD = 256
SUB = 8
LOG2E = 1.4426950408889634
MAX_FIXED_SHIFT = 48.0
MAX_SUB_DECAY = 60.0

LANES = 128
VMEM_LIMIT = 56 * 1024 * 1024
ROW_TILE = 512
ATTN_TILE = 512
CACHE_BLOCK = 512
FF_SLICE = 1024
MOD_TILE = 1024

C_QKR = (0, 512)
C_KV = (512, 768)
C_GQ = (768, 1024)
C_GK = (1024, 1280)
C_GV = (1280, 1792)
C_GR = (1792, 2304)


def _dot(a, b):
    return jnp.dot(a, b, preferred_element_type=F32)


def _dot_nt(a, b):
    return lax.dot_general(a, b, (((1,), (1,)), ((), ())), preferred_element_type=F32)


def _dot_tn(a, b):
    return lax.dot_general(a, b, (((0,), (0,)), ((), ())), preferred_element_type=F32)


def _rope_tile(t, c, sa, sb):
    return t * c + pltpu.roll(t, 96, 1) * sa + pltpu.roll(t, 32, 1) * sb


def _params(*sem):
    return pltpu.CompilerParams(dimension_semantics=sem, vmem_limit_bytes=VMEM_LIMIT)


def _const(shape):
    return pl.BlockSpec(shape, lambda *_: (0,) * len(shape), pipeline_mode=pl.Buffered(1))


def _mod_kernel(cs_ref, cp_ref, w_ref, b_ref, os_ref, op_ref):
    ns = cs_ref.shape[0]
    c = jnp.concatenate([cs_ref[...], cp_ref[...]], axis=0)
    s = (c * jax.nn.sigmoid(c)).astype(BF16)
    mod = _dot(s, w_ref[...].astype(BF16)) + b_ref[...]
    os_ref[...] = mod[0:ns]
    op_ref[...] = mod[ns:]


def _modulation(c_prompt, c_sample, w_ada, b_ada):
    d, n = w_ada.shape
    tn = MOD_TILE

    def whole(c):
        return pl.BlockSpec(c.shape, lambda j: (0, 0))

    def cols(c):
        return pl.BlockSpec((c.shape[0], tn), lambda j: (0, j))

    mod_s, mod_p = pl.pallas_call(
        _mod_kernel,
        grid=(n // tn,),
        in_specs=[whole(c_sample), whole(c_prompt),
                  pl.BlockSpec((d, tn), lambda j: (0, j)),
                  pl.BlockSpec((1, tn), lambda j: (0, j))],
        out_specs=[cols(c_sample), cols(c_prompt)],
        out_shape=[jax.ShapeDtypeStruct((c_sample.shape[0], n), F32), jax.ShapeDtypeStruct((c_prompt.shape[0], n), F32)],
        compiler_params=_params("arbitrary"),
        name="adaln_mod",
    )(c_sample, c_prompt, w_ada, b_ada.reshape(1, n))
    return mod_p, mod_s


def _proj_kernel(x_ref, sh_ref, sc_ref, g1_ref, win_ref, gql_ref, wuq_ref, gkv_ref, wukv_ref,
                 gqn_ref, gqr_ref, gkn_ref, gkr_ref, qone_ref, kbias_ref, tab_ref, wg_ref, bg_ref,
                 *outs, gpt, prompt):
    if prompt:
        lat_ref, krt_ref, q_ref, k_ref, v_ref, gq_ref, gk_ref, gv_ref, la_ref, gr_ref = outs
    else:
        lat_ref, kr_ref, krt_ref, q_ref, gq_ref, gk_ref, gv_ref, la_ref, gr_ref = outs
    tm, d = x_ref.shape
    x = x_ref[...]
    xn = x * lax.rsqrt(jnp.mean(x * x, axis=-1, keepdims=True) + EPS)
    h = (xn.reshape(gpt, tm // gpt, d) * (g1_ref[...] * (1.0 + sc_ref[...])) + sh_ref[...]).reshape(tm, d)
    hb = h.astype(BF16)

    def col(c):
        return _dot(hb, win_ref[:, c[0]:c[1]])

    tab = tab_ref[...]
    cos, sa, sb = tab[:, 0:LANES], tab[:, LANES:2 * LANES], tab[:, 2 * LANES:3 * LANES]

    qkr = col(C_QKR)
    krg = qkr[:, Q_LORA:]
    z = _dot(krg.astype(BF16), wg_ref[...]) + bg_ref[...]
    la_ref[...] = (jnp.minimum(z, 0.0) - jnp.log(1.0 + jnp.exp(-jnp.abs(z)))) * (1.0 / GLA_TAU)
    r = col(C_GR)
    gr_ref[...] = (r * jax.nn.sigmoid(r)).astype(BF16)

    cq = qkr[:, 0:Q_LORA]
    cqn = cq * lax.rsqrt(jnp.mean(cq * cq, axis=-1, keepdims=True) + EPS) * gql_ref[...]
    qp = _dot(cqn.astype(BF16), wuq_ref[...])
    for hh in range(MLA_HEADS):
        nope = qp[:, hh * HEAD_PAD:hh * HEAD_PAD + QK_NOPE]
        rt = qp[:, hh * HEAD_PAD + QK_NOPE:(hh + 1) * HEAD_PAD]
        ss = jnp.sum(nope * nope, axis=-1, keepdims=True) + jnp.sum(rt * rt, axis=-1, keepdims=True)
        inv = lax.rsqrt(ss * (1.0 / QK_DIM) + EPS)
        q_ref[:, hh * HEAD_PAD:hh * HEAD_PAD + QK_NOPE] = (nope * inv * gqn_ref[...]).astype(BF16)
        rq = _rope_tile(rt * inv * gqr_ref[...], cos, sa, sb)
        q_ref[:, hh * HEAD_PAD + QK_NOPE:(hh + 1) * HEAD_PAD] = (rq + qone_ref[...]).astype(BF16)

    ckv = col(C_KV)
    lat = ckv * lax.rsqrt(jnp.mean(ckv * ckv, axis=-1, keepdims=True) + EPS) * gkv_ref[...]
    lat_ref[...] = lat
    krt_ref[...] = krg.T[0:QK_ROPE, :]
    if prompt:
        lane = lax.broadcasted_iota(jnp.int32, (1, LANES), 1)
        krm = jnp.where(lane < QK_ROPE, krg, 0.0)
        ssr = jnp.sum(krm * krm, axis=-1, keepdims=True)
        rk = _rope_tile(krm * gkr_ref[...], cos, sa, sb)
        kv = _dot(lat.astype(BF16), wukv_ref[...])
        for hh in range(MLA_HEADS):
            kn = kv[:, hh * QK_NOPE:(hh + 1) * QK_NOPE]
            inv = lax.rsqrt((jnp.sum(kn * kn, axis=-1, keepdims=True) + ssr) * (1.0 / QK_DIM) + EPS)
            k_ref[:, hh * HEAD_PAD:hh * HEAD_PAD + QK_NOPE] = (kn * inv * gkn_ref[...]).astype(BF16)
            k_ref[:, hh * HEAD_PAD + QK_NOPE:(hh + 1) * HEAD_PAD] = (rk * inv + kbias_ref[...]).astype(BF16)
        v_ref[...] = kv[:, MLA_HEADS * QK_NOPE:].astype(BF16)
    else:
        kr_ref[...] = krg[:, 0:QK_ROPE]

    gq_ref[...] = col(C_GQ)
    gk_ref[...] = col(C_GK)
    gv_ref[...] = col(C_GV).astype(BF16)


def _projection(x2, mod4, tab, w, *, rows_per_group, tm, prompt):
    n, d = x2.shape
    gpt = max(1, tm // rows_per_group)
    tpg = max(1, rows_per_group // tm)
    ntab = tab.shape[0] // tm

    def mod_spec(j):
        return pl.BlockSpec((gpt, None, 1, d), lambda i: ((i // tpg) if gpt == 1 else i, j, 0, 0))

    def rows(c):
        return pl.BlockSpec((tm, c), lambda i: (i, 0))

    def out(c, t):
        return rows(c), jax.ShapeDtypeStruct((n, c), t)

    gla_outs = [out(GLA_QK, F32), out(GLA_QK, F32), out(GLA_WIDTH, BF16), out(GLA_QK, F32), out(GLA_WIDTH, BF16)]
    qo = out(MLA_HEADS * HEAD_PAD, BF16)
    if prompt:
        krt = (pl.BlockSpec((None, QK_ROPE, tm), lambda i: (i // tpg, 0, i % tpg)),
               jax.ShapeDtypeStruct((n // rows_per_group, QK_ROPE, rows_per_group), F32))
        outs = [out(KV_LORA, F32), krt, qo, out(MLA_HEADS * HEAD_PAD, BF16), out(MLA_WIDTH, BF16)] + gla_outs
    else:
        krt = (pl.BlockSpec((QK_ROPE, tm), lambda i: (0, i)), jax.ShapeDtypeStruct((QK_ROPE, n), F32))
        outs = [out(KV_LORA, F32), out(QK_ROPE, F32), krt, qo] + gla_outs
    return pl.pallas_call(
        functools.partial(_proj_kernel, gpt=gpt, prompt=prompt),
        grid=(n // tm,),
        in_specs=[rows(d), mod_spec(0), mod_spec(1), _const((1, d)), _const(w["w_in"].shape),
                  _const((1, Q_LORA)), _const(w["w_uq"].shape), _const((1, KV_LORA)), _const(w["w_ukv"].shape),
                  _const((1, LANES)), _const((1, LANES)), _const((1, LANES)), _const((1, LANES)),
                  _const((1, LANES)), _const((1, LANES)),
                  pl.BlockSpec((tm, 3 * LANES), lambda i: (i % ntab, 0)),
                  _const(w["w_gate"].shape), _const((1, GLA_QK))],
        out_specs=[o[0] for o in outs],
        out_shape=[o[1] for o in outs],
        compiler_params=_params("arbitrary"),
        name="in_proj",
    )(x2, mod4, mod4, w["g_norm1"], w["w_in"], w["g_q_lora"], w["w_uq"], w["g_kv_lora"], w["w_ukv"],
      w["gqn"], w["gqr"], w["gkn"], w["gkr"], w["qone"], w["kbias"], tab, w["w_gate"], w["b_gate"])


def _attn_prompt_kernel(q_ref, k_ref, v_ref, o_ref, vx_ref, m_ref, acc_ref, *, tq, fast):
    i = pl.program_id(1)
    heads = range(MLA_HEADS)

    @pl.when(i == 0)
    def _():
        for hh in heads:
            vx_ref[hh, :, 0:V_DIM] = v_ref[:, hh * V_DIM:(hh + 1) * V_DIM]
            vx_ref[hh, :, V_DIM:] = jnp.ones((v_ref.shape[0], V_DIM), BF16)

    def scores(j, hh, masked):
        hs = slice(hh * HEAD_PAD, (hh + 1) * HEAD_PAD)
        s = _dot_nt(q_ref[:, hs], k_ref[pl.ds(pl.multiple_of(j * tq, tq), tq), hs])
        if masked:
            qc = lax.broadcasted_iota(jnp.int32, (tq, tq), 0) // CHUNK
            kc = lax.broadcasted_iota(jnp.int32, (tq, tq), 1) // CHUNK
            s = jnp.where(kc <= qc, s, -jnp.inf)
        return s

    def fast_diagonal():
        hq = tq // 2
        k0 = pl.multiple_of(i * tq, tq)
        qc = lax.broadcasted_iota(jnp.int32, (hq, hq), 0) // CHUNK
        kc = lax.broadcasted_iota(jnp.int32, (hq, hq), 1) // CHUNK
        tri = kc <= qc
        for hh in heads:
            hs = slice(hh * HEAD_PAD, (hh + 1) * HEAD_PAD)
            top = jnp.where(tri, _dot_nt(q_ref[0:hq, hs], k_ref[pl.ds(k0, hq), hs]), -jnp.inf)
            acc_ref[hh, 0:hq] = _dot(jnp.exp2(top).astype(BF16), vx_ref[hh, pl.ds(k0, hq), :])
            bot = _dot_nt(q_ref[hq:tq, hs], k_ref[pl.ds(k0, tq), hs])
            bot = jnp.concatenate([bot[:, 0:hq], jnp.where(tri, bot[:, hq:tq], -jnp.inf)], axis=1)
            acc_ref[hh, hq:tq] = _dot(jnp.exp2(bot).astype(BF16), vx_ref[hh, pl.ds(k0, tq), :])

    def vblock(j, hh):
        return vx_ref[hh, pl.ds(pl.multiple_of(j * tq, tq), tq), :]

    def fast_block(j, masked):
        for hh in heads:
            acc_ref[hh] += _dot(jnp.exp2(scores(j, hh, masked)).astype(BF16), vblock(j, hh))

    def safe_block(j, masked):
        for hh in heads:
            s = scores(j, hh, masked)
            m = m_ref[hh]
            m_new = jnp.maximum(m, jnp.max(s, axis=1, keepdims=True))
            p = jnp.exp2(s - m_new).astype(BF16)
            acc_ref[hh] = jnp.exp2(m - m_new) * acc_ref[hh] + _dot(p, vblock(j, hh))
            m_ref[hh] = m_new

    def sweep(block):
        def body(jj, c):
            block(2 * jj, False)
            block(2 * jj + 1, False)
            return c

        lax.fori_loop(0, i // 2, body, 0)

        @pl.when(i % 2 == 1)
        def _():
            block(i - 1, False)

    if fast:
        fast_diagonal()
        sweep(fast_block)
    else:
        m_ref[...] = jnp.full(m_ref.shape, -jnp.inf, F32)
        acc_ref[...] = jnp.zeros(acc_ref.shape, F32)
        safe_block(i, True)
        sweep(safe_block)

    for hh in heads:
        acc = acc_ref[hh]
        o_ref[:, hh * V_DIM:(hh + 1) * V_DIM] = (acc[:, 0:V_DIM] / acc[:, V_DIM:]).astype(BF16)


def _attention_prompt(q, k, v, *, batch, seq, tq, fast):
    nq = seq // tq
    return pl.pallas_call(
        functools.partial(_attn_prompt_kernel, tq=tq, fast=fast),
        grid=(batch, nq),
        in_specs=[pl.BlockSpec((tq, MLA_HEADS * HEAD_PAD), lambda b, i: (b * nq + i, 0)),
                  pl.BlockSpec((seq, MLA_HEADS * HEAD_PAD), lambda b, i: (b, 0)),
                  pl.BlockSpec((seq, MLA_WIDTH), lambda b, i: (b, 0))],
        out_specs=pl.BlockSpec((tq, MLA_WIDTH), lambda b, i: (b * nq + i, 0)),
        out_shape=jax.ShapeDtypeStruct((batch * seq, MLA_WIDTH), BF16),
        scratch_shapes=[pltpu.VMEM((MLA_HEADS, seq, 2 * V_DIM), BF16), pltpu.VMEM((MLA_HEADS, tq, 1), F32),
                        pltpu.VMEM((MLA_HEADS, tq, 2 * V_DIM), F32)],
        compiler_params=_params("arbitrary", "arbitrary"),
        name="mla_attn_prompt",
    )(q, k, v)


def _attn_sample_kernel(bound_ref, q_ref, lat_ref, krt_ref, latn_ref, krtn_ref, tab_ref, wukt_ref, wuv_ref,
                        gkn_ref, gkr_ref, o_ref, wq_ref, latb_ref, s_ref, *, tkb, fast):
    t = q_ref.shape[0] // 2
    past = lat_ref.shape[1]
    nk = MLA_HEADS * QK_NOPE
    half = QK_ROPE // 2
    streams = range(2)

    @pl.when(pl.program_id(0) == 0)
    def _():
        for e in streams:
            wq_ref[e, 0:nk, :] = wukt_ref[...]

    qr = []
    for e in streams:
        rows = slice(e * t, (e + 1) * t)
        parts = []
        for hh in range(MLA_HEADS):
            qn = (q_ref[rows, hh * HEAD_PAD:hh * HEAD_PAD + QK_NOPE].astype(F32) * gkn_ref[...]).astype(BF16)
            wq_ref[e, nk + hh * t:nk + (hh + 1) * t, :] = _dot(
                qn, wukt_ref[hh * QK_NOPE:(hh + 1) * QK_NOPE, :]).astype(BF16)
            parts.append(q_ref[rows, hh * HEAD_PAD + QK_NOPE:(hh + 1) * HEAD_PAD])
        qr.append(jnp.concatenate(parts, axis=0))

    def key_block(e, lat, krt, tab, width, valid):
        latb = lat.astype(BF16)
        g = _dot_nt(wq_ref[e], latb)
        ssr = jnp.sum(krt * krt, axis=0, keepdims=True)
        kg = krt * gkr_ref[...]
        x1, x2 = kg[0:half], kg[half:QK_ROPE]
        c, sn = tab[0:half], tab[half:QK_ROPE]
        rope = jnp.concatenate([x1 * c - x2 * sn, x2 * c + x1 * sn, jnp.zeros((LANES - QK_ROPE, width), F32)], axis=0)
        srope = _dot(qr[e], rope.astype(BF16))
        rows = []
        for hh in range(MLA_HEADS):
            kn = g[hh * QK_NOPE:(hh + 1) * QK_NOPE]
            inv = lax.rsqrt((jnp.sum(kn * kn, axis=0, keepdims=True) + ssr) * (1.0 / QK_DIM) + EPS)
            rows.append((g[nk + hh * t:nk + (hh + 1) * t] + srope[hh * t:(hh + 1) * t]) * inv)
        sc = jnp.concatenate(rows, axis=0) - bound_ref[0]
        if valid is not None:
            sc = jnp.where(valid, sc, -jnp.inf)
        return sc, latb

    def blocks():
        for blk in range(past // tkb):
            c0 = blk * tkb
            for e in streams:
                yield e, c0, tkb, key_block(e, lat_ref[e, c0:c0 + tkb, :], krt_ref[e, :, c0:c0 + tkb],
                                            tab_ref[:, c0:c0 + tkb], tkb, None)
        for e in streams:
            mine = lax.broadcasted_iota(jnp.int32, (1, 2 * t), 1) // t == e
            yield e, past, 2 * t, key_block(e, latn_ref[...], krtn_ref[...], tab_ref[:, past:past + 2 * t], 2 * t, mine)

    def finish(e, wlat, l):
        wlat = wlat.astype(BF16)
        for hh in range(MLA_HEADS):
            o = _dot(wlat[hh * t:(hh + 1) * t], wuv_ref[:, hh * V_DIM:(hh + 1) * V_DIM])
            o_ref[e * t:(e + 1) * t, hh * V_DIM:(hh + 1) * V_DIM] = (o / l[hh * t:(hh + 1) * t]).astype(BF16)

    if fast:
        wlat = [jnp.zeros((MLA_HEADS * t, KV_LORA), F32) for _ in streams]
        lsum = [jnp.zeros((MLA_HEADS * t, LANES), F32) for _ in streams]
        for e, _, width, (sc, latb) in blocks():
            p = jnp.exp2(sc)
            wlat[e] += _dot(p.astype(BF16), latb)
            for c in range(width // LANES):
                lsum[e] += p[:, c * LANES:(c + 1) * LANES]
        for e in streams:
            finish(e, wlat[e], jnp.sum(lsum[e], axis=1, keepdims=True))

    else:
        for e, c0, width, (sc, latb) in blocks():
            s_ref[e, :, c0:c0 + width] = sc
            latb_ref[e, c0:c0 + width, :] = latb
        for e in streams:
            s = s_ref[e]
            p = jnp.exp2(s - jnp.max(s, axis=1, keepdims=True))
            finish(e, _dot(p.astype(BF16), latb_ref[e]), jnp.sum(p, axis=1, keepdims=True))


def _attention_sample(q, past_lat, past_krt, lat_new, krt_new, w, *, tkb, fast):
    batch, past, _ = past_lat.shape
    t = q.shape[0] // batch
    half = QK_ROPE // 2
    inv = ROPE_THETA ** (-np.arange(half, dtype=np.float64) / half)
    pos = np.concatenate([np.arange(past), past + np.arange(t), past + np.arange(t)]).astype(np.float64)
    ang = inv[:, None] * pos[None, :]
    tab = jnp.asarray(np.concatenate([np.cos(ang), np.sin(ang)], axis=0).astype(np.float32))
    s_pad = past + 2 * t
    return pl.pallas_call(
        functools.partial(_attn_sample_kernel, tkb=tkb, fast=fast),
        grid=(batch // 2,),
        in_specs=[pl.BlockSpec(memory_space=pltpu.SMEM),
                  pl.BlockSpec((2 * t, MLA_HEADS * HEAD_PAD), lambda g: (g, 0)),
                  pl.BlockSpec((2, past, KV_LORA), lambda g: (g, 0, 0)),
                  pl.BlockSpec((2, QK_ROPE, past), lambda g: (g, 0, 0)),
                  pl.BlockSpec((2 * t, KV_LORA), lambda g: (g, 0)),
                  pl.BlockSpec((QK_ROPE, 2 * t), lambda g: (0, g)),
                  _const(tab.shape), _const(w["w_ukt"].shape), _const(w["w_uv"].shape),
                  _const((1, LANES)), _const((QK_ROPE, 1))],
        out_specs=pl.BlockSpec((2 * t, MLA_WIDTH), lambda g: (g, 0)),
        out_shape=jax.ShapeDtypeStruct((batch * t, MLA_WIDTH), BF16),
        scratch_shapes=[pltpu.VMEM((2, MLA_HEADS * (QK_NOPE + t), KV_LORA), BF16),
                        pltpu.VMEM((2, s_pad, KV_LORA), BF16),
                        pltpu.VMEM((2, MLA_HEADS * t, s_pad), F32)],
        compiler_params=_params("arbitrary"),
        name="mla_attn_sample",
    )(w["bound"], q, past_lat, past_krt, lat_new, krt_new, tab, w["w_ukt"], w["w_uv"], w["gkn"], w["gkr_col"])


def _gla_kernel(q_ref, k_ref, v_ref, la_ref, r_ref, s0_ref, g_ref, spread_ref, o_ref, sn_ref,
                st_ref, kp_ref, ap_ref, p_ref, on_ref, *, gpt, fast):
    t_idx = pl.program_id(1)
    L, W, P = CHUNK, GLA_QK, LANES
    R = q_ref.shape[0]
    n_chunks = R // L
    cpg = n_chunks // gpt
    n_pairs = GLA_HEADS // 2

    lane_p = lax.broadcasted_iota(jnp.int32, (1, P), 1)
    even = lane_p < GLA_DK
    bd_mask = (lax.broadcasted_iota(jnp.int32, (2 * GLA_DV, P), 0) // GLA_DV
               == lax.broadcasted_iota(jnp.int32, (2 * GLA_DV, P), 1) // GLA_DK)

    @pl.when(t_idx == 0)
    def _():
        kp_ref[0:SUB, :] = jnp.zeros((SUB, W), F32)
        ap_ref[0:SUB, :] = jnp.zeros((SUB, W), F32)
        for gi in range(gpt):
            for pr in range(n_pairs):
                tt = s0_ref[gi, 2 * pr:2 * pr + 2].reshape(2 * GLA_DK, GLA_DV).T
                st_ref[gi, pr] = jnp.where(bd_mask, jnp.concatenate([tt, tt], axis=0), 0.0)

    q = q_ref[...]
    k = k_ref[...]
    la = la_ref[...]

    tri = (lax.broadcasted_iota(jnp.int32, (L, L), 0) >= lax.broadcasted_iota(jnp.int32, (L, L), 1)).astype(BF16)
    la_hi = la.astype(BF16)
    la2 = jnp.concatenate([la_hi, (la - la_hi.astype(F32)).astype(BF16)], axis=1)
    bs = []
    for c in range(n_chunks):
        t2 = _dot(tri, la2[c * L:(c + 1) * L, :])
        bs.append(t2[:, 0:W] + t2[:, W:2 * W])
    b = (jnp.concatenate(bs, axis=0) if n_chunks > 1 else bs[0]) * LOG2E
    b3 = b.reshape(n_chunks, L, W)

    def chunk_row(r):
        return jnp.broadcast_to(b3[:, r:r + 1, :], (n_chunks, L, W)).reshape(R, W)

    b_sub = jnp.broadcast_to(b.reshape(R // SUB, SUB, W)[:, 0:1, :], (R // SUB, SUB, W)).reshape(R, W)
    sub = (lax.broadcasted_iota(jnp.int32, (R, W), 0) % L) // SUB

    qt = q * jnp.exp2(b - b_sub)
    zb = jnp.zeros((), BF16)
    zq = jnp.zeros((SUB, P), F32)
    n_sub = L // SUB
    first = 0 if fast else 1
    ktm = []
    for i in range(first, n_sub):
        kt = (k * jnp.exp2(chunk_row(i * SUB) - b)).astype(BF16)
        keep = (sub[:, 0:P] <= i) if fast else (sub[:, 0:P] < i)
        ktm.append([[jnp.where(keep & (even if e == 0 else ~even), kt[:, pr * P:(pr + 1) * P], zb)
                     for e in range(2)] for pr in range(n_pairs)])

    if fast:
        causal = (lax.broadcasted_iota(jnp.int32, (L, P), 0) >= lax.broadcasted_iota(jnp.int32, (L, P), 1) % L)
    else:
        a = jnp.exp(la)
        kp_ref[SUB:SUB + R, :] = k
        ap_ref[SUB:SUB + R, :] = a
        p_ref[:, 0:W] = (q * k).astype(BF16)
        e = a
        for d in range(1, SUB):
            if d > 1:
                e = e * ap_ref[SUB - d + 1:SUB - d + 1 + R, :]
            p_ref[:, d * W:(d + 1) * W] = (q * kp_ref[SUB - d:SUB - d + R, :] * e).astype(BF16)
        cband = _dot(p_ref[...], spread_ref[...])
        same_sub = (lax.broadcasted_iota(jnp.int32, (L, W), 0) // SUB
                    == (lax.broadcasted_iota(jnp.int32, (L, W), 1) % L) // SUB)

    qe = (q * jnp.exp2(b)).astype(BF16)
    kd = (k * jnp.exp2(chunk_row(L - 1) - b)).astype(BF16)
    zv = jnp.zeros((L, 2 * GLA_DV), BF16)

    o_intra, d_st, dec = {}, {}, []
    for c in range(n_chunks):
        rs = slice(c * L, (c + 1) * L)
        if not fast:
            a_band = jnp.where(same_sub, pltpu.roll(cband[rs], W - (SUB - 1), 1, stride=1, stride_axis=0), 0.0)
        dec.append(jnp.exp2(b[c * L + L - 1:c * L + L, :]))
        for pr in range(n_pairs):
            ls = slice(pr * P, (pr + 1) * P)
            lhs_c = jnp.concatenate(
                [jnp.concatenate([qt[c * L + r * SUB:c * L + (r + 1) * SUB, ls] if i == r else zq
                                  for i in range(first, n_sub)], axis=1) for r in range(n_sub)],
                axis=0).astype(BF16)
            rhs_c = jnp.concatenate([jnp.concatenate([m[pr][0][rs], m[pr][1][rs]], axis=0) for m in ktm], axis=1)
            scores = _dot_nt(lhs_c, rhs_c)
            a_tot = (jnp.where(causal, scores, 0.0) if fast else a_band[:, ls] + scores).astype(BF16)
            vp = v_ref[rs, 2 * pr * GLA_DV:(2 * pr + 2) * GLA_DV]
            v_bd = jnp.concatenate([jnp.concatenate([vp[:, 0:GLA_DV], zv[:, 0:GLA_DV]], axis=1),
                                    jnp.concatenate([zv[:, 0:GLA_DV], vp[:, GLA_DV:]], axis=1)], axis=0)
            o_intra[c, pr] = _dot(a_tot, v_bd)
            d_st[c, pr] = jnp.where(bd_mask, _dot_tn(vp, kd[rs, ls]), 0.0)
    st_in = {}
    for gi in range(gpt):
        for pr in range(n_pairs):
            st = st_ref[gi, pr]
            for c in range(gi * cpg, (gi + 1) * cpg):
                st_in[c, pr] = st.astype(BF16)
                st = st * dec[c][:, pr * P:(pr + 1) * P] + d_st[c, pr]
            st_ref[gi, pr] = st
    for c in range(n_chunks):
        rs = slice(c * L, (c + 1) * L)
        for pr in range(n_pairs):
            on_ref[rs, 2 * pr * GLA_DV:(2 * pr + 2) * GLA_DV] = (
                o_intra[c, pr] + _dot_nt(qe[rs, pr * P:(pr + 1) * P], st_in[c, pr]))

    for hh in range(GLA_HEADS):
        hs = slice(hh * GLA_DV, (hh + 1) * GLA_DV)
        o = on_ref[:, hs]
        on = o * lax.rsqrt(jnp.mean(o * o, axis=-1, keepdims=True) + EPS) * g_ref[:, hs]
        o_ref[:, hs] = (on * r_ref[:, hs].astype(F32)).astype(BF16)

    @pl.when(t_idx == pl.num_programs(1) - 1)
    def _():
        for gi in range(gpt):
            for pr in range(n_pairs):
                st = st_ref[gi, pr]
                tt = jnp.where(even, st[0:GLA_DV], st[GLA_DV:2 * GLA_DV])
                sn_ref[gi, 2 * pr:2 * pr + 2] = tt.T.reshape(2, GLA_DK, GLA_DV)


def _band_spread():
    m = np.zeros((SUB, GLA_HEADS, GLA_DK, GLA_QK), np.float32)
    for d in range(SUB):
        for h in range(GLA_HEADS):
            m[d, h, :, h * GLA_DK + SUB - 1 - d] = 1.0
    return jnp.asarray(m.reshape(SUB * GLA_QK, GLA_QK), BF16)


def _gla(gq, gk, gv, la, gr, s0, g_out, *, groups, rows_per_group, tc, fast):
    gpt = max(1, tc // rows_per_group)
    nt = max(1, rows_per_group // tc)

    def rows(c):
        return pl.BlockSpec((tc, c), lambda g, t: (g * nt + t, 0))

    state = pl.BlockSpec((gpt, GLA_HEADS, GLA_DK, GLA_DV), lambda g, t: (g, 0, 0, 0))
    spread = _band_spread()
    return pl.pallas_call(
        functools.partial(_gla_kernel, gpt=gpt, fast=fast),
        grid=(groups // gpt, nt),
        in_specs=[rows(GLA_QK), rows(GLA_QK), rows(GLA_WIDTH), rows(GLA_QK), rows(GLA_WIDTH), state,
                  _const((1, GLA_WIDTH)), _const(spread.shape)],
        out_specs=[rows(GLA_WIDTH), state],
        out_shape=[jax.ShapeDtypeStruct((groups * rows_per_group, GLA_WIDTH), BF16),
                   jax.ShapeDtypeStruct((groups, GLA_HEADS, GLA_DK, GLA_DV), F32)],
        scratch_shapes=[pltpu.VMEM((gpt, GLA_HEADS // 2, 2 * GLA_DV, LANES), F32),
                        pltpu.VMEM((SUB + tc, GLA_QK), F32), pltpu.VMEM((SUB + tc, GLA_QK), F32),
                        pltpu.VMEM((tc, SUB * GLA_QK), BF16), pltpu.VMEM((tc, GLA_WIDTH), F32)],
        compiler_params=_params("arbitrary", "arbitrary"),
        name="gla",
    )(gq, gk, gv, la, gr, s0, g_out, spread)


def _mlp_tile(x_ref, a_ref, b_ref, g1_ref, sh2_ref, sc2_ref, g2_ref, gn_ref, wo_ref, wu_ref, wd_ref, y_ref, *, gpt):
    tm, d = x_ref.shape

    def per_group(val, ref, scale_plus_one=False):
        m = ref[...]
        if scale_plus_one:
            m = 1.0 + m
        return (val.reshape(gpt, tm // gpt, d) * m).reshape(tm, d)

    mix = jnp.concatenate([a_ref[...], b_ref[...]], axis=1)
    x1 = x_ref[...] + per_group(_dot(mix, wo_ref[...]), g1_ref)
    xn = x1 * lax.rsqrt(jnp.mean(x1 * x1, axis=-1, keepdims=True) + EPS) * gn_ref[...]
    h2 = (per_group(xn, sc2_ref, True).reshape(gpt, tm // gpt, d) + sh2_ref[...]).reshape(tm, d).astype(BF16)
    acc = jnp.zeros((tm, d), F32)
    for j in range(wu_ref.shape[1] // FF_SLICE):
        u = jnp.maximum(_dot(h2, wu_ref[:, j * FF_SLICE:(j + 1) * FF_SLICE]), 0.0)
        acc += _dot((u * u).astype(BF16), wd_ref[j * FF_SLICE:(j + 1) * FF_SLICE, :])
    y_ref[...] = x1 + per_group(acc, g2_ref)


def _mlp_kernel(*refs, tiles, gpts):
    per_phase = 7
    gn_ref, wo_ref, wu_ref, wd_ref = refs[len(tiles) * per_phase:len(tiles) * per_phase + 4]
    y_refs = refs[len(tiles) * per_phase + 4:]
    i = pl.program_id(0)
    first = 0
    for p, (n_tiles, gpt) in enumerate(zip(tiles, gpts)):
        ins = refs[p * per_phase:(p + 1) * per_phase]

        @pl.when(jnp.logical_and(i >= first, i < first + n_tiles))
        def _(ins=ins, p=p, gpt=gpt):
            _mlp_tile(*ins, gn_ref, wo_ref, wu_ref, wd_ref, y_refs[p], gpt=gpt)

        first += n_tiles


def _mlp(phases, w, *, tm):
    d = phases[0][0].shape[1]
    tiles = [ph[0].shape[0] // tm for ph in phases]
    gpts = [max(1, tm // ph[4]) for ph in phases]
    in_specs, args, first = [], [], 0
    for (x2, a_out, b_out, mod4, rpg), n_tiles, gpt in zip(phases, tiles, gpts):
        tpg = max(1, rpg // tm)

        def tile(i, first=first, n_tiles=n_tiles):
            return jnp.clip(i - first, 0, n_tiles - 1)

        def rows(c, tile=tile):
            return pl.BlockSpec((tm, c), lambda i: (tile(i), 0))

        def mod_spec(j, tile=tile, gpt=gpt, tpg=tpg):
            return pl.BlockSpec((gpt, None, 1, d), lambda i: (tile(i) // tpg, j, 0, 0))

        in_specs += [rows(d), rows(MLA_WIDTH), rows(GLA_WIDTH), mod_spec(2), mod_spec(3), mod_spec(4), mod_spec(5)]
        args += [x2, a_out, b_out, mod4, mod4, mod4, mod4]
        first += n_tiles
    out_specs, first = [], 0
    for n_tiles in tiles:
        out_specs.append(pl.BlockSpec((tm, d), lambda i, first=first, n_tiles=n_tiles:
                                      (jnp.clip(i - first, 0, n_tiles - 1), 0)))
        first += n_tiles
    return pl.pallas_call(
        functools.partial(_mlp_kernel, tiles=tuple(tiles), gpts=tuple(gpts)),
        grid=(sum(tiles),),
        in_specs=in_specs + [_const((1, d)), _const(w["w_out"].shape), _const(w["w_up"].shape),
                             _const(w["w_down"].shape)],
        out_specs=out_specs,
        out_shape=[jax.ShapeDtypeStruct(ph[0].shape, F32) for ph in phases],
        compiler_params=_params("arbitrary"),
        name="out_proj_mlp",
    )(*args, w["g_norm2"], w["w_out"], w["w_up"], w["w_down"])


def _rope_table(start, count, repeat=1):
    half = QK_ROPE // 2
    inv = ROPE_THETA ** (-np.arange(half, dtype=np.float64) / half)
    ang = (start + np.arange(count, dtype=np.float64))[:, None] * inv[None, :]
    c, s, z = np.cos(ang), np.sin(ang), np.zeros_like(ang)
    tab = np.concatenate([c, c, z, z, -s, z, z, z, z, s, z, z], axis=1).astype(np.float32)
    return jnp.asarray(np.tile(tab, (repeat, 1)))


def _pad_gain(g_rope):
    return jnp.concatenate([g_rope, jnp.zeros((LANES - QK_ROPE,), F32)]).reshape(1, LANES)


def _relayout_kernel(wint_ref, wuq_ref, wukv_ref, wg_ref, win_o, wuq_o, wukv_o, wukt_o, wuv_o, wg_o):
    s = np.cumsum([0, Q_LORA, KV_LORA, QK_ROPE, GLA_QK, GLA_QK, GLA_WIDTH, GLA_GATE_RANK, GLA_WIDTH])

    def piece(i):
        return wint_ref[int(s[i]):int(s[i + 1]), :]

    d = wint_ref.shape[1]
    zeros = jnp.zeros((LANES - QK_ROPE - GLA_GATE_RANK, d), F32)
    win_o[:, 0:Q_LORA] = piece(0).T.astype(BF16)
    win_o[:, Q_LORA:C_QKR[1]] = jnp.concatenate([piece(2), piece(6), zeros], axis=0).T.astype(BF16)
    win_o[:, C_KV[0]:C_KV[1]] = piece(1).T.astype(BF16)
    win_o[:, C_GQ[0]:C_GQ[1]] = (piece(3).T * (GLA_DK ** -0.5)).astype(BF16)
    win_o[:, C_GK[0]:C_GK[1]] = piece(4).T.astype(BF16)
    win_o[:, C_GV[0]:C_GV[1]] = piece(5).T.astype(BF16)
    win_o[:, C_GR[0]:C_GR[1]] = piece(7).T.astype(BF16)

    zq = jnp.zeros((Q_LORA, HEAD_PAD - QK_DIM), BF16)
    kvw = QK_NOPE + V_DIM
    for hh in range(MLA_HEADS):
        wuq_o[:, hh * HEAD_PAD:hh * HEAD_PAD + QK_DIM] = wuq_ref[:, hh * QK_DIM:(hh + 1) * QK_DIM].astype(BF16)
        wuq_o[:, hh * HEAD_PAD + QK_DIM:(hh + 1) * HEAD_PAD] = zq
        uk = wukv_ref[:, hh * kvw:hh * kvw + QK_NOPE]
        uv = wukv_ref[:, hh * kvw + QK_NOPE:(hh + 1) * kvw].astype(BF16)
        wukv_o[:, hh * QK_NOPE:(hh + 1) * QK_NOPE] = uk.astype(BF16)
        wukv_o[:, (MLA_HEADS + hh) * V_DIM:(MLA_HEADS + hh + 1) * V_DIM] = uv
        wukt_o[hh * QK_NOPE:(hh + 1) * QK_NOPE, :] = uk.T.astype(BF16)
        wuv_o[:, hh * V_DIM:(hh + 1) * V_DIM] = uv

    wg_o[...] = jnp.zeros(wg_o.shape, BF16)
    wg_o[QK_ROPE:QK_ROPE + GLA_GATE_RANK, :] = wg_ref[...].astype(BF16)


def _relayout(w_in_t, w_uq, w_ukv, w_gate_up):
    d = w_in_t.shape[1]
    shapes = [(d, C_GR[1]), (Q_LORA, MLA_HEADS * HEAD_PAD), (KV_LORA, MLA_HEADS * (QK_NOPE + V_DIM)),
              (MLA_HEADS * QK_NOPE, KV_LORA), (KV_LORA, MLA_WIDTH), (LANES, GLA_QK)]
    return pl.pallas_call(
        _relayout_kernel,
        out_shape=[jax.ShapeDtypeStruct(sh, BF16) for sh in shapes],
        compiler_params=pltpu.CompilerParams(vmem_limit_bytes=VMEM_LIMIT),
        name="weight_relayout",
    )(w_in_t, w_uq, w_ukv, w_gate_up)


def _prep_weights(w_in, g_norm1, g_q_lora, w_uq, g_kv_lora, w_ukv, g_q_head, g_k_head,
                  w_gate_up, b_gate_up, g_gla_out, w_out, g_norm2, w_up, w_down):
    d = w_in.shape[0]
    w_in_p, w_uq_p, w_ukv_p, w_ukt, w_uv, w_gate = _relayout(w_in.T, w_uq, w_ukv, w_gate_up)
    qscale = QK_DIM ** -0.5 * LOG2E
    bound = 1.02 * QK_DIM ** 0.5 * LOG2E * jnp.max(jnp.abs(g_q_head)) * jnp.max(jnp.abs(g_k_head))
    lane = jnp.arange(LANES) == QK_ROPE
    return {
        "w_in": w_in_p, "g_norm1": g_norm1.reshape(1, d), "g_q_lora": g_q_lora.reshape(1, Q_LORA),
        "w_uq": w_uq_p, "g_kv_lora": g_kv_lora.reshape(1, KV_LORA), "w_ukv": w_ukv_p, "w_ukt": w_ukt, "w_uv": w_uv,
        "gkr_col": g_k_head[QK_NOPE:].reshape(QK_ROPE, 1),
        "qone": lane.astype(F32).reshape(1, LANES), "kbias": jnp.where(lane, -bound, 0.0).reshape(1, LANES),
        "fast_softmax": (bound <= MAX_FIXED_SHIFT).astype(jnp.int32).reshape(1), "bound": bound.reshape(1),
        "gqn": (g_q_head[:QK_NOPE] * qscale).reshape(1, LANES), "gqr": _pad_gain(g_q_head[QK_NOPE:] * qscale),
        "gkn": g_k_head[:QK_NOPE].reshape(1, LANES), "gkr": _pad_gain(g_k_head[QK_NOPE:]),
        "w_gate": w_gate, "b_gate": b_gate_up.reshape(1, GLA_QK),
        "g_gla_out": g_gla_out.reshape(1, GLA_WIDTH), "w_out": w_out.astype(BF16),
        "g_norm2": g_norm2.reshape(1, d), "w_up": w_up.astype(BF16), "w_down": w_down.astype(BF16),
    }


def _mixers(x, mod, past_lat, past_kr, s0, w, *, tm):
    batch, seq, d = x.shape
    n = batch * seq
    past = 0 if past_lat is None else past_lat.shape[1]
    x2 = x.reshape(n, d)
    mod4 = mod.reshape(batch, 6, 1, d)
    tm = min(tm, n)
    tab = _rope_table(past, seq, repeat=max(1, tm // seq))
    if past == 0:
        lat, krt, q, k, v, gq, gk, gv, la, gr = _projection(x2, mod4, tab, w, rows_per_group=seq, tm=tm, prompt=True)
        kr = jnp.swapaxes(krt, 1, 2)
        attend = functools.partial(_attention_prompt, q, k, v, batch=batch, seq=seq, tq=min(ATTN_TILE, seq))
        a_out = lax.cond(w["fast_softmax"][0] == 1, functools.partial(attend, fast=True),
                         functools.partial(attend, fast=False))
    else:
        assert seq == CHUNK and past % CHUNK == 0
        lat, kr, krt, q, gq, gk, gv, la, gr = _projection(x2, mod4, tab, w, rows_per_group=seq, tm=tm, prompt=False)
        attend = functools.partial(_attention_sample, q, past_lat, jnp.swapaxes(past_kr, 1, 2), lat, krt, w,
                                   tkb=min(CACHE_BLOCK, past))
        a_out = lax.cond(w["fast_softmax"][0] == 1, functools.partial(attend, fast=True),
                         functools.partial(attend, fast=False))
    gap = -LOG2E * jnp.min(jnp.sum(la.reshape(n // SUB, SUB, GLA_QK)[:, 1:, :], axis=1))
    gla = functools.partial(_gla, gq, gk, gv, la, gr, s0, w["g_gla_out"], groups=batch, rows_per_group=seq, tc=tm)
    b_out, s_new = lax.cond(gap <= MAX_SUB_DECAY, functools.partial(gla, fast=True), functools.partial(gla, fast=False))
    return (x2, a_out, b_out, mod4, seq), (lat.reshape(batch, seq, KV_LORA), kr.reshape(batch, seq, QK_ROPE), s_new)


def kernel(x_prompt, x_sample, cache_mla_latent, cache_mla_krope, state_gla, c_prompt, c_sample,
           w_ada, b_ada, g_norm1, w_in, g_q_lora, w_uq, g_kv_lora, w_ukv, g_q_head, g_k_head,
           w_gate_up, b_gate_up, g_gla_out, w_out, g_norm2, w_up, w_down):
    nb = x_prompt.shape[0]
    depth = w_ada.shape[0]
    y_p, y_s = x_prompt, x_sample
    outs = [[] for _ in range(6)]
    for l in range(depth):
        w = _prep_weights(w_in[l], g_norm1[l], g_q_lora[l], w_uq[l], g_kv_lora[l], w_ukv[l], g_q_head[l],
                          g_k_head[l], w_gate_up[l], b_gate_up[l], g_gla_out[l], w_out[l], g_norm2[l],
                          w_up[l], w_down[l])
        mod_p, mod_s = _modulation(c_prompt, c_sample, w_ada[l], b_ada[l])
        zero_state = jnp.zeros((nb, GLA_HEADS, GLA_DK, GLA_DV), x_prompt.dtype)
        mix_p, new_p = _mixers(y_p, mod_p, None, None, zero_state, w, tm=ROW_TILE)
        mix_s, new_s = _mixers(y_s, mod_s, cache_mla_latent[l], cache_mla_krope[l], state_gla[l], w, tm=ROW_TILE)
        y2_p, y2_s = _mlp([mix_p, mix_s], w, tm=ROW_TILE)
        y_p, y_s = y2_p.reshape(y_p.shape), y2_s.reshape(y_s.shape)
        for o, new in zip(outs, new_p + new_s):
            o.append(new)
    return (y_p, y_s) + tuple(jnp.stack(o) for o in outs)
```

```python
import functools

import jax
import jax.numpy as jnp
import numpy as np
from jax import lax
from jax.experimental import pallas as pl
from jax.experimental.pallas import tpu as pltpu

F32 = jnp.float32
BF16 = jnp.bfloat16

CHUNK = 64
EPS = 1e-6
MLA_HEADS = 4
Q_LORA = 384
KV_LORA = 256
QK_NOPE = 128
QK_ROPE = 64
QK_DIM = QK_NOPE + QK_ROPE
V_DIM = 128
ROPE_THETA = 10000.0
GLA_HEADS = 4
GLA_DK = 64
GLA_DV = 128
GLA_GATE_RANK = 16
GLA_TAU = 16.0
GLA_QK = GLA_HEADS * GLA_DK
GLA_WIDTH = GLA_HEADS * GLA_DV
MLA_WIDTH = MLA_HEADS * V_DIM
HEAD_PAD = 256
SUB = 8
LOG2E = 1.4426950408889634
MAX_FIXED_SHIFT = 48.0
MAX_SUB_DECAY = 60.0

LANES = 128
VMEM_LIMIT = 56 * 1024 * 1024
ROW_TILE = 512
ATTN_TILE = 512
CACHE_BLOCK = 512
FF_SLICE = 1024
MOD_TILE = 1024

C_QKR = (0, 512)
C_KV = (512, 768)
C_GQ = (768, 1024)
C_GK = (1024, 1280)
C_GV = (1280, 1792)
C_GR = (1792, 2304)


def _dot(a, b):
    return jnp.dot(a, b, preferred_element_type=F32)


def _dot_nt(a, b):
    return lax.dot_general(a, b, (((1,), (1,)), ((), ())), preferred_element_type=F32)


def _dot_tn(a, b):
    return lax.dot_general(a, b, (((0,), (0,)), ((), ())), preferred_element_type=F32)


def _rope_tile(t, c, sa, sb):
    return t * c + pltpu.roll(t, 96, 1) * sa + pltpu.roll(t, 32, 1) * sb


def _params(*sem):
    return pltpu.CompilerParams(dimension_semantics=sem, vmem_limit_bytes=VMEM_LIMIT)


def _const(shape):
    return pl.BlockSpec(shape, lambda *_: (0,) * len(shape), pipeline_mode=pl.Buffered(1))


def _mod_kernel(cs_ref, cp_ref, w_ref, b_ref, os_ref, op_ref):
    ns = cs_ref.shape[0]
    c = jnp.concatenate([cs_ref[...], cp_ref[...]], axis=0)
    s = (c * jax.nn.sigmoid(c)).astype(BF16)
    mod = _dot(s, w_ref[...].astype(BF16)) + b_ref[...]
    os_ref[...] = mod[0:ns]
    op_ref[...] = mod[ns:]


def _modulation(c_prompt, c_sample, w_ada, b_ada):
    d, n = w_ada.shape
    tn = MOD_TILE

    def whole(c):
        return pl.BlockSpec(c.shape, lambda j: (0, 0))

    def cols(c):
        return pl.BlockSpec((c.shape[0], tn), lambda j: (0, j))

    mod_s, mod_p = pl.pallas_call(
        _mod_kernel,
        grid=(n // tn,),
        in_specs=[whole(c_sample), whole(c_prompt),
                  pl.BlockSpec((d, tn), lambda j: (0, j)),
                  pl.BlockSpec((1, tn), lambda j: (0, j))],
        out_specs=[cols(c_sample), cols(c_prompt)],
        out_shape=[jax.ShapeDtypeStruct((c_sample.shape[0], n), F32), jax.ShapeDtypeStruct((c_prompt.shape[0], n), F32)],
        compiler_params=_params("arbitrary"),
        name="adaln_mod",
    )(c_sample, c_prompt, w_ada, b_ada.reshape(1, n))
    return mod_p, mod_s


def _proj_kernel(x_ref, sh_ref, sc_ref, g1_ref, win_ref, gql_ref, wuq_ref, gkv_ref, wukv_ref,
                 gqn_ref, gqr_ref, gkn_ref, gkr_ref, qone_ref, kbias_ref, tab_ref, wg_ref, bg_ref,
                 *outs, gpt, prompt):
    if prompt:
        lat_ref, krt_ref, q_ref, k_ref, v_ref, gq_ref, gk_ref, gv_ref, la_ref, gr_ref, dmin_ref = outs
    else:
        lat_ref, kr_ref, krt_ref, q_ref, gq_ref, gk_ref, gv_ref, la_ref, gr_ref, dmin_ref = outs
    tm, d = x_ref.shape
    x = x_ref[...]
    xn = x * lax.rsqrt(jnp.mean(x * x, axis=-1, keepdims=True) + EPS)
    h = (xn.reshape(gpt, tm // gpt, d) * (g1_ref[...] * (1.0 + sc_ref[...])) + sh_ref[...]).reshape(tm, d)
    hb = h.astype(BF16)

    def col(c):
        return _dot(hb, win_ref[:, c[0]:c[1]])

    tab = tab_ref[...]
    cos, sa, sb = tab[:, 0:LANES], tab[:, LANES:2 * LANES], tab[:, 2 * LANES:3 * LANES]

    qkr = col(C_QKR)
    krg = qkr[:, Q_LORA:]
    z = _dot(krg.astype(BF16), wg_ref[...]) + bg_ref[...]
    la = (jnp.minimum(z, 0.0) - jnp.log(1.0 + jnp.exp(-jnp.abs(z)))) * (1.0 / GLA_TAU)
    la_ref[...] = la
    dmin_ref[...] = jnp.min(la, axis=0, keepdims=True)
    r = col(C_GR)
    gr_ref[...] = (r * jax.nn.sigmoid(r)).astype(BF16)

    cq = qkr[:, 0:Q_LORA]
    cqn = cq * lax.rsqrt(jnp.mean(cq * cq, axis=-1, keepdims=True) + EPS) * gql_ref[...]
    qp = _dot(cqn.astype(BF16), wuq_ref[...])
    for hh in range(MLA_HEADS):
        nope = qp[:, hh * HEAD_PAD:hh * HEAD_PAD + QK_NOPE]
        rt = qp[:, hh * HEAD_PAD + QK_NOPE:(hh + 1) * HEAD_PAD]
        ss = jnp.sum(nope * nope, axis=-1, keepdims=True) + jnp.sum(rt * rt, axis=-1, keepdims=True)
        inv = lax.rsqrt(ss * (1.0 / QK_DIM) + EPS)
        q_ref[:, hh * HEAD_PAD:hh * HEAD_PAD + QK_NOPE] = (nope * inv * gqn_ref[...]).astype(BF16)
        rq = _rope_tile(rt * inv * gqr_ref[...], cos, sa, sb)
        q_ref[:, hh * HEAD_PAD + QK_NOPE:(hh + 1) * HEAD_PAD] = (rq + qone_ref[...]).astype(BF16)

    ckv = col(C_KV)
    lat = ckv * lax.rsqrt(jnp.mean(ckv * ckv, axis=-1, keepdims=True) + EPS) * gkv_ref[...]
    lat_ref[...] = lat
    krt_ref[...] = krg.T[0:QK_ROPE, :]
    if prompt:
        lane = lax.broadcasted_iota(jnp.int32, (1, LANES), 1)
        krm = jnp.where(lane < QK_ROPE, krg, 0.0)
        ssr = jnp.sum(krm * krm, axis=-1, keepdims=True)
        rk = _rope_tile(krm * gkr_ref[...], cos, sa, sb)
        kv = _dot(lat.astype(BF16), wukv_ref[...])
        for hh in range(MLA_HEADS):
            kn = kv[:, hh * QK_NOPE:(hh + 1) * QK_NOPE]
            inv = lax.rsqrt((jnp.sum(kn * kn, axis=-1, keepdims=True) + ssr) * (1.0 / QK_DIM) + EPS)
            k_ref[:, hh * HEAD_PAD:hh * HEAD_PAD + QK_NOPE] = (kn * inv * gkn_ref[...]).astype(BF16)
            k_ref[:, hh * HEAD_PAD + QK_NOPE:(hh + 1) * HEAD_PAD] = (rk * inv + kbias_ref[...]).astype(BF16)
        v_ref[...] = kv[:, MLA_HEADS * QK_NOPE:].astype(BF16)
    else:
        kr_ref[...] = krg[:, 0:QK_ROPE]

    gq_ref[...] = col(C_GQ)
    gk_ref[...] = col(C_GK)
    gv_ref[...] = col(C_GV).astype(BF16)


def _projection(x2, mod4, tab, w, *, rows_per_group, tm, prompt):
    n, d = x2.shape
    gpt = max(1, tm // rows_per_group)
    tpg = max(1, rows_per_group // tm)
    ntab = tab.shape[0] // tm

    def mod_spec(j):
        return pl.BlockSpec((gpt, None, 1, d), lambda i: ((i // tpg) if gpt == 1 else i, j, 0, 0))

    def rows(c):
        return pl.BlockSpec((tm, c), lambda i: (i, 0))

    def out(c, t):
        return rows(c), jax.ShapeDtypeStruct((n, c), t)

    dmin = (pl.BlockSpec((None, 1, GLA_QK), lambda i: (i, 0, 0)), jax.ShapeDtypeStruct((n // tm, 1, GLA_QK), F32))
    gla_outs = [out(GLA_QK, F32), out(GLA_QK, F32), out(GLA_WIDTH, BF16), out(GLA_QK, F32), out(GLA_WIDTH, BF16),
                dmin]
    qo = out(MLA_HEADS * HEAD_PAD, BF16)
    if prompt:
        krt = (pl.BlockSpec((None, QK_ROPE, tm), lambda i: (i // tpg, 0, i % tpg)),
               jax.ShapeDtypeStruct((n // rows_per_group, QK_ROPE, rows_per_group), F32))
        outs = [out(KV_LORA, F32), krt, qo, out(MLA_HEADS * HEAD_PAD, BF16), out(MLA_WIDTH, BF16)] + gla_outs
    else:
        krt = (pl.BlockSpec((QK_ROPE, tm), lambda i: (0, i)), jax.ShapeDtypeStruct((QK_ROPE, n), F32))
        outs = [out(KV_LORA, F32), out(QK_ROPE, F32), krt, qo] + gla_outs
    return pl.pallas_call(
        functools.partial(_proj_kernel, gpt=gpt, prompt=prompt),
        grid=(n // tm,),
        in_specs=[rows(d), mod_spec(0), mod_spec(1), _const((1, d)), _const(w["w_in"].shape),
                  _const((1, Q_LORA)), _const(w["w_uq"].shape), _const((1, KV_LORA)), _const(w["w_ukv"].shape),
                  _const((1, LANES)), _const((1, LANES)), _const((1, LANES)), _const((1, LANES)),
                  _const((1, LANES)), _const((1, LANES)),
                  pl.BlockSpec((tm, 3 * LANES), lambda i: (i % ntab, 0)),
                  _const(w["w_gate"].shape), _const((1, GLA_QK))],
        out_specs=[o[0] for o in outs],
        out_shape=[o[1] for o in outs],
        compiler_params=_params("arbitrary"),
        name="in_proj",
    )(x2, mod4, mod4, w["g_norm1"], w["w_in"], w["g_q_lora"], w["w_uq"], w["g_kv_lora"], w["w_ukv"],
      w["gqn"], w["gqr"], w["gkn"], w["gkr"], w["qone"], w["kbias"], tab, w["w_gate"], w["b_gate"])


def _attn_prompt_kernel(q_ref, k_ref, v_ref, o_ref, vx_ref, m_ref, acc_ref, *, tq, fast):
    i = pl.program_id(1)
    heads = range(MLA_HEADS)

    @pl.when(i == 0)
    def _():
        for hh in heads:
            vx_ref[hh, :, 0:V_DIM] = v_ref[:, hh * V_DIM:(hh + 1) * V_DIM]
            vx_ref[hh, :, V_DIM:] = jnp.ones((v_ref.shape[0], V_DIM), BF16)

    def scores(j, hh, masked):
        hs = slice(hh * HEAD_PAD, (hh + 1) * HEAD_PAD)
        s = _dot_nt(q_ref[:, hs], k_ref[pl.ds(pl.multiple_of(j * tq, tq), tq), hs])
        if masked:
            qc = lax.broadcasted_iota(jnp.int32, (tq, tq), 0) // CHUNK
            kc = lax.broadcasted_iota(jnp.int32, (tq, tq), 1) // CHUNK
            s = jnp.where(kc <= qc, s, -jnp.inf)
        return s

    def fast_diagonal():
        hq = tq // 2
        k0 = pl.multiple_of(i * tq, tq)
        qc = lax.broadcasted_iota(jnp.int32, (hq, hq), 0) // CHUNK
        kc = lax.broadcasted_iota(jnp.int32, (hq, hq), 1) // CHUNK
        tri = kc <= qc
        for hh in heads:
            hs = slice(hh * HEAD_PAD, (hh + 1) * HEAD_PAD)
            top = jnp.where(tri, _dot_nt(q_ref[0:hq, hs], k_ref[pl.ds(k0, hq), hs]), -jnp.inf)
            acc_ref[hh, 0:hq] = _dot(jnp.exp2(top).astype(BF16), vx_ref[hh, pl.ds(k0, hq), :])
            bot = _dot_nt(q_ref[hq:tq, hs], k_ref[pl.ds(k0, tq), hs])
            bot = jnp.concatenate([bot[:, 0:hq], jnp.where(tri, bot[:, hq:tq], -jnp.inf)], axis=1)
            acc_ref[hh, hq:tq] = _dot(jnp.exp2(bot).astype(BF16), vx_ref[hh, pl.ds(k0, tq), :])

    def vblock(j, hh):
        return vx_ref[hh, pl.ds(pl.multiple_of(j * tq, tq), tq), :]

    def fast_block(j, masked):
        for hh in heads:
            acc_ref[hh] += _dot(jnp.exp2(scores(j, hh, masked)).astype(BF16), vblock(j, hh))

    def safe_block(j, masked):
        for hh in heads:
            s = scores(j, hh, masked)
            m = m_ref[hh]
            m_new = jnp.maximum(m, jnp.max(s, axis=1, keepdims=True))
            p = jnp.exp2(s - m_new).astype(BF16)
            acc_ref[hh] = jnp.exp2(m - m_new) * acc_ref[hh] + _dot(p, vblock(j, hh))
            m_ref[hh] = m_new

    def sweep(block):
        def body(jj, c):
            block(2 * jj, False)
            block(2 * jj + 1, False)
            return c

        lax.fori_loop(0, i // 2, body, 0)

        @pl.when(i % 2 == 1)
        def _():
            block(i - 1, False)

    if fast:
        fast_diagonal()
        sweep(fast_block)
    else:
        m_ref[...] = jnp.full(m_ref.shape, -jnp.inf, F32)
        acc_ref[...] = jnp.zeros(acc_ref.shape, F32)
        safe_block(i, True)
        sweep(safe_block)

    for hh in heads:
        acc = acc_ref[hh]
        o_ref[:, hh * V_DIM:(hh + 1) * V_DIM] = (acc[:, 0:V_DIM] / acc[:, V_DIM:]).astype(BF16)


def _attention_prompt(q, k, v, *, batch, seq, tq, fast):
    nq = seq // tq
    return pl.pallas_call(
        functools.partial(_attn_prompt_kernel, tq=tq, fast=fast),
        grid=(batch, nq),
        in_specs=[pl.BlockSpec((tq, MLA_HEADS * HEAD_PAD), lambda b, i: (b * nq + i, 0)),
                  pl.BlockSpec((seq, MLA_HEADS * HEAD_PAD), lambda b, i: (b, 0)),
                  pl.BlockSpec((seq, MLA_WIDTH), lambda b, i: (b, 0))],
        out_specs=pl.BlockSpec((tq, MLA_WIDTH), lambda b, i: (b * nq + i, 0)),
        out_shape=jax.ShapeDtypeStruct((batch * seq, MLA_WIDTH), BF16),
        scratch_shapes=[pltpu.VMEM((MLA_HEADS, seq, 2 * V_DIM), BF16), pltpu.VMEM((MLA_HEADS, tq, 1), F32),
                        pltpu.VMEM((MLA_HEADS, tq, 2 * V_DIM), F32)],
        compiler_params=_params("arbitrary", "arbitrary"),
        name="mla_attn_prompt",
    )(q, k, v)


def _attn_sample_kernel(bound_ref, q_ref, lat_ref, krt_ref, latn_ref, krtn_ref, tab_ref, wukt_ref, wuv_ref,
                        gkn_ref, gkr_ref, o_ref, wq_ref, latb_ref, s_ref, *, tkb, fast):
    t = q_ref.shape[0] // 2
    past = lat_ref.shape[1]
    nk = MLA_HEADS * QK_NOPE
    half = QK_ROPE // 2
    streams = range(2)

    @pl.when(pl.program_id(0) == 0)
    def _():
        for e in streams:
            wq_ref[e, 0:nk, :] = wukt_ref[...]

    qr = []
    for e in streams:
        rows = slice(e * t, (e + 1) * t)
        parts = []
        for hh in range(MLA_HEADS):
            qn = (q_ref[rows, hh * HEAD_PAD:hh * HEAD_PAD + QK_NOPE].astype(F32) * gkn_ref[...]).astype(BF16)
            wq_ref[e, nk + hh * t:nk + (hh + 1) * t, :] = _dot(
                qn, wukt_ref[hh * QK_NOPE:(hh + 1) * QK_NOPE, :]).astype(BF16)
            parts.append(q_ref[rows, hh * HEAD_PAD + QK_NOPE:(hh + 1) * HEAD_PAD])
        qr.append(jnp.concatenate(parts, axis=0))

    def key_block(e, lat, krt, tab, width, valid):
        latb = lat.astype(BF16)
        g = _dot_nt(wq_ref[e], latb)
        ssr = jnp.sum(krt * krt, axis=0, keepdims=True)
        kg = krt * gkr_ref[...]
        x1, x2 = kg[0:half], kg[half:QK_ROPE]
        c, sn = tab[0:half], tab[half:QK_ROPE]
        rope = jnp.concatenate([x1 * c - x2 * sn, x2 * c + x1 * sn, jnp.zeros((LANES - QK_ROPE, width), F32)], axis=0)
        srope = _dot(qr[e], rope.astype(BF16))
        rows = []
        for hh in range(MLA_HEADS):
            kn = g[hh * QK_NOPE:(hh + 1) * QK_NOPE]
            inv = lax.rsqrt((jnp.sum(kn * kn, axis=0, keepdims=True) + ssr) * (1.0 / QK_DIM) + EPS)
            rows.append((g[nk + hh * t:nk + (hh + 1) * t] + srope[hh * t:(hh + 1) * t]) * inv)
        sc = jnp.concatenate(rows, axis=0) - bound_ref[0]
        if valid is not None:
            sc = jnp.where(valid, sc, -jnp.inf)
        return sc, latb

    def blocks():
        for blk in range(past // tkb):
            c0 = blk * tkb
            for e in streams:
                yield e, c0, tkb, key_block(e, lat_ref[e, c0:c0 + tkb, :], krt_ref[e, :, c0:c0 + tkb],
                                            tab_ref[:, c0:c0 + tkb], tkb, None)
        for e in streams:
            mine = lax.broadcasted_iota(jnp.int32, (1, 2 * t), 1) // t == e
            yield e, past, 2 * t, key_block(e, latn_ref[...], krtn_ref[...], tab_ref[:, past:past + 2 * t], 2 * t, mine)

    def finish(e, wlat, l):
        wlat = wlat.astype(BF16)
        for hh in range(MLA_HEADS):
            o = _dot(wlat[hh * t:(hh + 1) * t], wuv_ref[:, hh * V_DIM:(hh + 1) * V_DIM])
            o_ref[e * t:(e + 1) * t, hh * V_DIM:(hh + 1) * V_DIM] = (o / l[hh * t:(hh + 1) * t]).astype(BF16)

    if fast:
        wlat = [jnp.zeros((MLA_HEADS * t, KV_LORA), F32) for _ in streams]
        lsum = [jnp.zeros((MLA_HEADS * t, LANES), F32) for _ in streams]
        for e, _, width, (sc, latb) in blocks():
            p = jnp.exp2(sc)
            wlat[e] += _dot(p.astype(BF16), latb)
            for c in range(width // LANES):
                lsum[e] += p[:, c * LANES:(c + 1) * LANES]
        for e in streams:
            finish(e, wlat[e], jnp.sum(lsum[e], axis=1, keepdims=True))

    else:
        for e, c0, width, (sc, latb) in blocks():
            s_ref[e, :, c0:c0 + width] = sc
            latb_ref[e, c0:c0 + width, :] = latb
        for e in streams:
            s = s_ref[e]
            p = jnp.exp2(s - jnp.max(s, axis=1, keepdims=True))
            finish(e, _dot(p.astype(BF16), latb_ref[e]), jnp.sum(p, axis=1, keepdims=True))


def _attention_sample(q, past_lat, past_krt, lat_new, krt_new, w, *, tkb, fast):
    batch, past, _ = past_lat.shape
    t = q.shape[0] // batch
    half = QK_ROPE // 2
    inv = ROPE_THETA ** (-np.arange(half, dtype=np.float64) / half)
    pos = np.concatenate([np.arange(past), past + np.arange(t), past + np.arange(t)]).astype(np.float64)
    ang = inv[:, None] * pos[None, :]
    tab = jnp.asarray(np.concatenate([np.cos(ang), np.sin(ang)], axis=0).astype(np.float32))
    s_pad = past + 2 * t
    return pl.pallas_call(
        functools.partial(_attn_sample_kernel, tkb=tkb, fast=fast),
        grid=(batch // 2,),
        in_specs=[pl.BlockSpec(memory_space=pltpu.SMEM),
                  pl.BlockSpec((2 * t, MLA_HEADS * HEAD_PAD), lambda g: (g, 0)),
                  pl.BlockSpec((2, past, KV_LORA), lambda g: (g, 0, 0)),
                  pl.BlockSpec((2, QK_ROPE, past), lambda g: (g, 0, 0)),
                  pl.BlockSpec((2 * t, KV_LORA), lambda g: (g, 0)),
                  pl.BlockSpec((QK_ROPE, 2 * t), lambda g: (0, g)),
                  _const(tab.shape), _const(w["w_ukt"].shape), _const(w["w_uv"].shape),
                  _const((1, LANES)), _const((QK_ROPE, 1))],
        out_specs=pl.BlockSpec((2 * t, MLA_WIDTH), lambda g: (g, 0)),
        out_shape=jax.ShapeDtypeStruct((batch * t, MLA_WIDTH), BF16),
        scratch_shapes=[pltpu.VMEM((2, MLA_HEADS * (QK_NOPE + t), KV_LORA), BF16),
                        pltpu.VMEM((2, s_pad, KV_LORA), BF16),
                        pltpu.VMEM((2, MLA_HEADS * t, s_pad), F32)],
        compiler_params=_params("arbitrary"),
        name="mla_attn_sample",
    )(w["bound"], q, past_lat, past_krt, lat_new, krt_new, tab, w["w_ukt"], w["w_uv"], w["gkn"], w["gkr_col"])


def _gla_kernel(q_ref, k_ref, v_ref, la_ref, r_ref, s0_ref, g_ref, spread_ref, o_ref, sn_ref,
                st_ref, kp_ref, ap_ref, p_ref, on_ref, *, gpt, fast):
    t_idx = pl.program_id(1)
    L, W, P = CHUNK, GLA_QK, LANES
    R = q_ref.shape[0]
    n_chunks = R // L
    cpg = n_chunks // gpt
    n_pairs = GLA_HEADS // 2

    lane_p = lax.broadcasted_iota(jnp.int32, (1, P), 1)
    even = lane_p < GLA_DK
    bd_mask = (lax.broadcasted_iota(jnp.int32, (2 * GLA_DV, P), 0) // GLA_DV
               == lax.broadcasted_iota(jnp.int32, (2 * GLA_DV, P), 1) // GLA_DK)

    @pl.when(t_idx == 0)
    def _():
        kp_ref[0:SUB, :] = jnp.zeros((SUB, W), F32)
        ap_ref[0:SUB, :] = jnp.zeros((SUB, W), F32)
        for gi in range(gpt):
            for pr in range(n_pairs):
                tt = s0_ref[gi, 2 * pr:2 * pr + 2].reshape(2 * GLA_DK, GLA_DV).T
                st_ref[gi, pr] = jnp.where(bd_mask, jnp.concatenate([tt, tt], axis=0), 0.0)

    q = q_ref[...]
    k = k_ref[...]
    la = la_ref[...]

    tri = (lax.broadcasted_iota(jnp.int32, (L, L), 0) >= lax.broadcasted_iota(jnp.int32, (L, L), 1)).astype(BF16)
    la_hi = la.astype(BF16)
    la2 = jnp.concatenate([la_hi, (la - la_hi.astype(F32)).astype(BF16)], axis=1)
    bs = []
    for c in range(n_chunks):
        t2 = _dot(tri, la2[c * L:(c + 1) * L, :])
        bs.append(t2[:, 0:W] + t2[:, W:2 * W])
    b = (jnp.concatenate(bs, axis=0) if n_chunks > 1 else bs[0]) * LOG2E
    b3 = b.reshape(n_chunks, L, W)

    def chunk_row(r):
        return jnp.broadcast_to(b3[:, r:r + 1, :], (n_chunks, L, W)).reshape(R, W)

    b_sub = jnp.broadcast_to(b.reshape(R // SUB, SUB, W)[:, 0:1, :], (R // SUB, SUB, W)).reshape(R, W)
    sub = (lax.broadcasted_iota(jnp.int32, (R, W), 0) % L) // SUB

    qt = q * jnp.exp2(b - b_sub)
    zb = jnp.zeros((), BF16)
    zq = jnp.zeros((SUB, P), F32)
    n_sub = L // SUB
    first = 0 if fast else 1
    ktm = []
    for i in range(first, n_sub):
        kt = (k * jnp.exp2(chunk_row(i * SUB) - b)).astype(BF16)
        keep = (sub[:, 0:P] <= i) if fast else (sub[:, 0:P] < i)
        ktm.append([[jnp.where(keep & (even if e == 0 else ~even), kt[:, pr * P:(pr + 1) * P], zb)
                     for e in range(2)] for pr in range(n_pairs)])

    if fast:
        causal = (lax.broadcasted_iota(jnp.int32, (L, P), 0) >= lax.broadcasted_iota(jnp.int32, (L, P), 1) % L)
    else:
        a = jnp.exp(la)
        kp_ref[SUB:SUB + R, :] = k
        ap_ref[SUB:SUB + R, :] = a
        p_ref[:, 0:W] = (q * k).astype(BF16)
        e = a
        for d in range(1, SUB):
            if d > 1:
                e = e * ap_ref[SUB - d + 1:SUB - d + 1 + R, :]
            p_ref[:, d * W:(d + 1) * W] = (q * kp_ref[SUB - d:SUB - d + R, :] * e).astype(BF16)
        cband = _dot(p_ref[...], spread_ref[...])
        same_sub = (lax.broadcasted_iota(jnp.int32, (L, W), 0) // SUB
                    == (lax.broadcasted_iota(jnp.int32, (L, W), 1) % L) // SUB)

    qe = (q * jnp.exp2(b)).astype(BF16)
    kd = (k * jnp.exp2(chunk_row(L - 1) - b)).astype(BF16)
    zv = jnp.zeros((L, 2 * GLA_DV), BF16)

    o_intra, d_st, dec = {}, {}, []
    for c in range(n_chunks):
        rs = slice(c * L, (c + 1) * L)
        if not fast:
            a_band = jnp.where(same_sub, pltpu.roll(cband[rs], W - (SUB - 1), 1, stride=1, stride_axis=0), 0.0)
        dec.append(jnp.exp2(b[c * L + L - 1:c * L + L, :]))
        for pr in range(n_pairs):
            ls = slice(pr * P, (pr + 1) * P)
            lhs_c = jnp.concatenate(
                [jnp.concatenate([qt[c * L + r * SUB:c * L + (r + 1) * SUB, ls] if i == r else zq
                                  for i in range(first, n_sub)], axis=1) for r in range(n_sub)],
                axis=0).astype(BF16)
            rhs_c = jnp.concatenate([jnp.concatenate([m[pr][0][rs], m[pr][1][rs]], axis=0) for m in ktm], axis=1)
            scores = _dot_nt(lhs_c, rhs_c)
            a_tot = (jnp.where(causal, scores, 0.0) if fast else a_band[:, ls] + scores).astype(BF16)
            vp = v_ref[rs, 2 * pr * GLA_DV:(2 * pr + 2) * GLA_DV]
            v_bd = jnp.concatenate([jnp.concatenate([vp[:, 0:GLA_DV], zv[:, 0:GLA_DV]], axis=1),
                                    jnp.concatenate([zv[:, 0:GLA_DV], vp[:, GLA_DV:]], axis=1)], axis=0)
            o_intra[c, pr] = _dot(a_tot, v_bd)
            d_st[c, pr] = jnp.where(bd_mask, _dot_tn(vp, kd[rs, ls]), 0.0)
    st_in = {}
    for gi in range(gpt):
        for pr in range(n_pairs):
            st = st_ref[gi, pr]
            for c in range(gi * cpg, (gi + 1) * cpg):
                st_in[c, pr] = st.astype(BF16)
                st = st * dec[c][:, pr * P:(pr + 1) * P] + d_st[c, pr]
            st_ref[gi, pr] = st
    for c in range(n_chunks):
        rs = slice(c * L, (c + 1) * L)
        for pr in range(n_pairs):
            on_ref[rs, 2 * pr * GLA_DV:(2 * pr + 2) * GLA_DV] = (
                o_intra[c, pr] + _dot_nt(qe[rs, pr * P:(pr + 1) * P], st_in[c, pr]))

    for hh in range(GLA_HEADS):
        hs = slice(hh * GLA_DV, (hh + 1) * GLA_DV)
        o = on_ref[:, hs]
        on = o * lax.rsqrt(jnp.mean(o * o, axis=-1, keepdims=True) + EPS) * g_ref[:, hs]
        o_ref[:, hs] = (on * r_ref[:, hs].astype(F32)).astype(BF16)

    @pl.when(t_idx == pl.num_programs(1) - 1)
    def _():
        for gi in range(gpt):
            for pr in range(n_pairs):
                st = st_ref[gi, pr]
                tt = jnp.where(even, st[0:GLA_DV], st[GLA_DV:2 * GLA_DV])
                sn_ref[gi, 2 * pr:2 * pr + 2] = tt.T.reshape(2, GLA_DK, GLA_DV)


def _band_spread():
    m = np.zeros((SUB, GLA_HEADS, GLA_DK, GLA_QK), np.float32)
    for d in range(SUB):
        for h in range(GLA_HEADS):
            m[d, h, :, h * GLA_DK + SUB - 1 - d] = 1.0
    return jnp.asarray(m.reshape(SUB * GLA_QK, GLA_QK), BF16)


def _gla(gq, gk, gv, la, gr, s0, g_out, *, groups, rows_per_group, tc, fast):
    gpt = max(1, tc // rows_per_group)
    nt = max(1, rows_per_group // tc)

    def rows(c):
        return pl.BlockSpec((tc, c), lambda g, t: (g * nt + t, 0))

    state = pl.BlockSpec((gpt, GLA_HEADS, GLA_DK, GLA_DV), lambda g, t: (g, 0, 0, 0))
    spread = _band_spread()
    return pl.pallas_call(
        functools.partial(_gla_kernel, gpt=gpt, fast=fast),
        grid=(groups // gpt, nt),
        in_specs=[rows(GLA_QK), rows(GLA_QK), rows(GLA_WIDTH), rows(GLA_QK), rows(GLA_WIDTH), state,
                  _const((1, GLA_WIDTH)), _const(spread.shape)],
        out_specs=[rows(GLA_WIDTH), state],
        out_shape=[jax.ShapeDtypeStruct((groups * rows_per_group, GLA_WIDTH), BF16),
                   jax.ShapeDtypeStruct((groups, GLA_HEADS, GLA_DK, GLA_DV), F32)],
        scratch_shapes=[pltpu.VMEM((gpt, GLA_HEADS // 2, 2 * GLA_DV, LANES), F32),
                        pltpu.VMEM((SUB + tc, GLA_QK), F32), pltpu.VMEM((SUB + tc, GLA_QK), F32),
                        pltpu.VMEM((tc, SUB * GLA_QK), BF16), pltpu.VMEM((tc, GLA_WIDTH), F32)],
        compiler_params=_params("arbitrary", "arbitrary"),
        name="gla",
    )(gq, gk, gv, la, gr, s0, g_out, spread)


def _mlp_tile(x_ref, a_ref, b_ref, g1_ref, sh2_ref, sc2_ref, g2_ref, gn_ref, wo_ref, wu_ref, wd_ref, y_ref, *, gpt):
    tm, d = x_ref.shape

    def per_group(val, ref, scale_plus_one=False):
        m = ref[...]
        if scale_plus_one:
            m = 1.0 + m
        return (val.reshape(gpt, tm // gpt, d) * m).reshape(tm, d)

    mix = jnp.concatenate([a_ref[...], b_ref[...]], axis=1)
    x1 = x_ref[...] + per_group(_dot(mix, wo_ref[...]), g1_ref)
    xn = x1 * lax.rsqrt(jnp.mean(x1 * x1, axis=-1, keepdims=True) + EPS) * gn_ref[...]
    h2 = (per_group(xn, sc2_ref, True).reshape(gpt, tm // gpt, d) + sh2_ref[...]).reshape(tm, d).astype(BF16)
    acc = jnp.zeros((tm, d), F32)
    for j in range(wu_ref.shape[1] // FF_SLICE):
        u = jnp.maximum(_dot(h2, wu_ref[:, j * FF_SLICE:(j + 1) * FF_SLICE]), 0.0)
        acc += _dot((u * u).astype(BF16), wd_ref[j * FF_SLICE:(j + 1) * FF_SLICE, :])
    y_ref[...] = x1 + per_group(acc, g2_ref)


def _mlp_kernel(*refs, tiles, gpts):
    per_phase = 7
    gn_ref, wo_ref, wu_ref, wd_ref = refs[len(tiles) * per_phase:len(tiles) * per_phase + 4]
    y_refs = refs[len(tiles) * per_phase + 4:]
    i = pl.program_id(0)
    first = 0
    for p, (n_tiles, gpt) in enumerate(zip(tiles, gpts)):
        ins = refs[p * per_phase:(p + 1) * per_phase]

        @pl.when(jnp.logical_and(i >= first, i < first + n_tiles))
        def _(ins=ins, p=p, gpt=gpt):
            _mlp_tile(*ins, gn_ref, wo_ref, wu_ref, wd_ref, y_refs[p], gpt=gpt)

        first += n_tiles


def _mlp(phases, w, *, tm):
    d = phases[0][0].shape[1]
    tiles = [ph[0].shape[0] // tm for ph in phases]
    gpts = [max(1, tm // ph[4]) for ph in phases]
    in_specs, args, first = [], [], 0
    for (x2, a_out, b_out, mod4, rpg), n_tiles, gpt in zip(phases, tiles, gpts):
        tpg = max(1, rpg // tm)

        def tile(i, first=first, n_tiles=n_tiles):
            return jnp.clip(i - first, 0, n_tiles - 1)

        def rows(c, tile=tile):
            return pl.BlockSpec((tm, c), lambda i: (tile(i), 0))

        def mod_spec(j, tile=tile, gpt=gpt, tpg=tpg):
            return pl.BlockSpec((gpt, None, 1, d), lambda i: (tile(i) // tpg, j, 0, 0))

        in_specs += [rows(d), rows(MLA_WIDTH), rows(GLA_WIDTH), mod_spec(2), mod_spec(3), mod_spec(4), mod_spec(5)]
        args += [x2, a_out, b_out, mod4, mod4, mod4, mod4]
        first += n_tiles
    out_specs, first = [], 0
    for n_tiles in tiles:
        out_specs.append(pl.BlockSpec((tm, d), lambda i, first=first, n_tiles=n_tiles:
                                      (jnp.clip(i - first, 0, n_tiles - 1), 0)))
        first += n_tiles
    return pl.pallas_call(
        functools.partial(_mlp_kernel, tiles=tuple(tiles), gpts=tuple(gpts)),
        grid=(sum(tiles),),
        in_specs=in_specs + [_const((1, d)), _const(w["w_out"].shape), _const(w["w_up"].shape),
                             _const(w["w_down"].shape)],
        out_specs=out_specs,
        out_shape=[jax.ShapeDtypeStruct(ph[0].shape, F32) for ph in phases],
        compiler_params=_params("arbitrary"),
        name="out_proj_mlp",
    )(*args, w["g_norm2"], w["w_out"], w["w_up"], w["w_down"])


def _rope_table(start, count, repeat=1):
    half = QK_ROPE // 2
    inv = ROPE_THETA ** (-np.arange(half, dtype=np.float64) / half)
    ang = (start + np.arange(count, dtype=np.float64))[:, None] * inv[None, :]
    c, s, z = np.cos(ang), np.sin(ang), np.zeros_like(ang)
    tab = np.concatenate([c, c, z, z, -s, z, z, z, z, s, z, z], axis=1).astype(np.float32)
    return jnp.asarray(np.tile(tab, (repeat, 1)))


def _pad_gain(g_rope):
    return jnp.concatenate([g_rope, jnp.zeros((LANES - QK_ROPE,), F32)]).reshape(1, LANES)


def _relayout_kernel(wint_ref, wuq_ref, wukv_ref, wg_ref, win_o, wuq_o, wukv_o, wukt_o, wuv_o, wg_o):
    s = np.cumsum([0, Q_LORA, KV_LORA, QK_ROPE, GLA_QK, GLA_QK, GLA_WIDTH, GLA_GATE_RANK, GLA_WIDTH])

    def piece(i):
        return wint_ref[int(s[i]):int(s[i + 1]), :]

    d = wint_ref.shape[1]
    zeros = jnp.zeros((LANES - QK_ROPE - GLA_GATE_RANK, d), F32)
    win_o[:, 0:Q_LORA] = piece(0).T.astype(BF16)
    win_o[:, Q_LORA:C_QKR[1]] = jnp.concatenate([piece(2), piece(6), zeros], axis=0).T.astype(BF16)
    win_o[:, C_KV[0]:C_KV[1]] = piece(1).T.astype(BF16)
    win_o[:, C_GQ[0]:C_GQ[1]] = (piece(3).T * (GLA_DK ** -0.5)).astype(BF16)
    win_o[:, C_GK[0]:C_GK[1]] = piece(4).T.astype(BF16)
    win_o[:, C_GV[0]:C_GV[1]] = piece(5).T.astype(BF16)
    win_o[:, C_GR[0]:C_GR[1]] = piece(7).T.astype(BF16)

    zq = jnp.zeros((Q_LORA, HEAD_PAD - QK_DIM), BF16)
    kvw = QK_NOPE + V_DIM
    for hh in range(MLA_HEADS):
        wuq_o[:, hh * HEAD_PAD:hh * HEAD_PAD + QK_DIM] = wuq_ref[:, hh * QK_DIM:(hh + 1) * QK_DIM].astype(BF16)
        wuq_o[:, hh * HEAD_PAD + QK_DIM:(hh + 1) * HEAD_PAD] = zq
        uk = wukv_ref[:, hh * kvw:hh * kvw + QK_NOPE]
        uv = wukv_ref[:, hh * kvw + QK_NOPE:(hh + 1) * kvw].astype(BF16)
        wukv_o[:, hh * QK_NOPE:(hh + 1) * QK_NOPE] = uk.astype(BF16)
        wukv_o[:, (MLA_HEADS + hh) * V_DIM:(MLA_HEADS + hh + 1) * V_DIM] = uv
        wukt_o[hh * QK_NOPE:(hh + 1) * QK_NOPE, :] = uk.T.astype(BF16)
        wuv_o[:, hh * V_DIM:(hh + 1) * V_DIM] = uv

    wg_o[...] = jnp.zeros(wg_o.shape, BF16)
    wg_o[QK_ROPE:QK_ROPE + GLA_GATE_RANK, :] = wg_ref[...].astype(BF16)


def _relayout(w_in_t, w_uq, w_ukv, w_gate_up):
    d = w_in_t.shape[1]
    shapes = [(d, C_GR[1]), (Q_LORA, MLA_HEADS * HEAD_PAD), (KV_LORA, MLA_HEADS * (QK_NOPE + V_DIM)),
              (MLA_HEADS * QK_NOPE, KV_LORA), (KV_LORA, MLA_WIDTH), (LANES, GLA_QK)]
    return pl.pallas_call(
        _relayout_kernel,
        out_shape=[jax.ShapeDtypeStruct(sh, BF16) for sh in shapes],
        compiler_params=pltpu.CompilerParams(vmem_limit_bytes=VMEM_LIMIT),
        name="weight_relayout",
    )(w_in_t, w_uq, w_ukv, w_gate_up)


def _prep_weights(w_in, g_norm1, g_q_lora, w_uq, g_kv_lora, w_ukv, g_q_head, g_k_head,
                  w_gate_up, b_gate_up, g_gla_out, w_out, g_norm2, w_up, w_down):
    d = w_in.shape[0]
    w_in_p, w_uq_p, w_ukv_p, w_ukt, w_uv, w_gate = _relayout(w_in.T, w_uq, w_ukv, w_gate_up)
    qscale = QK_DIM ** -0.5 * LOG2E
    bound = 1.02 * QK_DIM ** 0.5 * LOG2E * jnp.max(jnp.abs(g_q_head)) * jnp.max(jnp.abs(g_k_head))
    lane = jnp.arange(LANES) == QK_ROPE
    return {
        "w_in": w_in_p, "g_norm1": g_norm1.reshape(1, d), "g_q_lora": g_q_lora.reshape(1, Q_LORA),
        "w_uq": w_uq_p, "g_kv_lora": g_kv_lora.reshape(1, KV_LORA), "w_ukv": w_ukv_p, "w_ukt": w_ukt, "w_uv": w_uv,
        "gkr_col": g_k_head[QK_NOPE:].reshape(QK_ROPE, 1),
        "qone": lane.astype(F32).reshape(1, LANES), "kbias": jnp.where(lane, -bound, 0.0).reshape(1, LANES),
        "fast_softmax": (bound <= MAX_FIXED_SHIFT).astype(jnp.int32).reshape(1), "bound": bound.reshape(1),
        "gqn": (g_q_head[:QK_NOPE] * qscale).reshape(1, LANES), "gqr": _pad_gain(g_q_head[QK_NOPE:] * qscale),
        "gkn": g_k_head[:QK_NOPE].reshape(1, LANES), "gkr": _pad_gain(g_k_head[QK_NOPE:]),
        "w_gate": w_gate, "b_gate": b_gate_up.reshape(1, GLA_QK),
        "g_gla_out": g_gla_out.reshape(1, GLA_WIDTH), "w_out": w_out.astype(BF16),
        "g_norm2": g_norm2.reshape(1, d), "w_up": w_up.astype(BF16), "w_down": w_down.astype(BF16),
    }


def _mixers(x, mod, past_lat, past_kr, s0, w, *, tm):
    batch, seq, d = x.shape
    n = batch * seq
    past = 0 if past_lat is None else past_lat.shape[1]
    x2 = x.reshape(n, d)
    mod4 = mod.reshape(batch, 6, 1, d)
    tm = min(tm, n)
    tab = _rope_table(past, seq, repeat=max(1, tm // seq))
    if past == 0:
        lat, krt, q, k, v, gq, gk, gv, la, gr, dmin = _projection(x2, mod4, tab, w, rows_per_group=seq, tm=tm,
                                                                  prompt=True)
        kr = jnp.swapaxes(krt, 1, 2)
        attend = functools.partial(_attention_prompt, q, k, v, batch=batch, seq=seq, tq=min(ATTN_TILE, seq))
        a_out = lax.cond(w["fast_softmax"][0] == 1, functools.partial(attend, fast=True),
                         functools.partial(attend, fast=False))
    else:
        assert seq == CHUNK and past % CHUNK == 0
        lat, kr, krt, q, gq, gk, gv, la, gr, dmin = _projection(x2, mod4, tab, w, rows_per_group=seq, tm=tm,
                                                                prompt=False)
        attend = functools.partial(_attention_sample, q, past_lat, jnp.swapaxes(past_kr, 1, 2), lat, krt, w,
                                   tkb=min(CACHE_BLOCK, past))
        a_out = lax.cond(w["fast_softmax"][0] == 1, functools.partial(attend, fast=True),
                         functools.partial(attend, fast=False))
    gap = -(SUB - 1) * LOG2E * jnp.min(dmin)
    gla = functools.partial(_gla, gq, gk, gv, la, gr, s0, w["g_gla_out"], groups=batch, rows_per_group=seq, tc=tm)
    b_out, s_new = lax.cond(gap <= MAX_SUB_DECAY, functools.partial(gla, fast=True), functools.partial(gla, fast=False))
    return (x2, a_out, b_out, mod4, seq), (lat.reshape(batch, seq, KV_LORA), kr.reshape(batch, seq, QK_ROPE), s_new)


def kernel(x_prompt, x_sample, cache_mla_latent, cache_mla_krope, state_gla, c_prompt, c_sample,
           w_ada, b_ada, g_norm1, w_in, g_q_lora, w_uq, g_kv_lora, w_ukv, g_q_head, g_k_head,
           w_gate_up, b_gate_up, g_gla_out, w_out, g_norm2, w_up, w_down):
    nb = x_prompt.shape[0]
    depth = w_ada.shape[0]
    y_p, y_s = x_prompt, x_sample
    outs = [[] for _ in range(6)]
    for l in range(depth):
        w = _prep_weights(w_in[l], g_norm1[l], g_q_lora[l], w_uq[l], g_kv_lora[l], w_ukv[l], g_q_head[l],
                          g_k_head[l], w_gate_up[l], b_gate_up[l], g_gla_out[l], w_out[l], g_norm2[l],
                          w_up[l], w_down[l])
        mod_p, mod_s = _modulation(c_prompt, c_sample, w_ada[l], b_ada[l])
        zero_state = jnp.zeros((nb, GLA_HEADS, GLA_DK, GLA_DV), x_prompt.dtype)
        mix_p, new_p = _mixers(y_p, mod_p, None, None, zero_state, w, tm=ROW_TILE)
        mix_s, new_s = _mixers(y_s, mod_s, cache_mla_latent[l], cache_mla_krope[l], state_gla[l], w, tm=ROW_TILE)
        y2_p, y2_s = _mlp([mix_p, mix_s], w, tm=ROW_TILE)
        y_p, y_s = y2_p.reshape(y_p.shape), y2_s.reshape(y_s.shape)
        for o, new in zip(outs, new_p + new_s):
            o.append(new)
    return (y_p, y_s) + tuple(jnp.stack(o) for o in outs)
```

```python
import functools

import jax
import jax.numpy as jnp
import numpy as np
from jax import lax
from jax.experimental import pallas as pl
from jax.experimental.pallas import tpu as pltpu

F32 = jnp.float32
BF16 = jnp.bfloat16

CHUNK = 64
EPS = 1e-6
MLA_HEADS = 4
Q_LORA = 384
KV_LORA = 256
QK_NOPE = 128
QK_ROPE = 64
QK_DIM = QK_NOPE + QK_ROPE
V_DIM = 128
ROPE_THETA = 10000.0
GLA_HEADS = 4
GLA_DK = 64
GLA_DV = 128
GLA_GATE_RANK = 16
GLA_TAU = 16.0
GLA_QK = GLA_HEADS * GLA_DK
GLA_WIDTH = GLA_HEADS * GLA_DV
MLA_WIDTH = MLA_HEADS * V_DIM
HEAD_PAD = 256
SUB = 8
LOG2E = 1.4426950408889634
MAX_FIXED_SHIFT = 48.0
MAX_SUB_DECAY = 60.0

LANES = 128
VMEM_LIMIT = 56 * 1024 * 1024
ROW_TILE = 512
ATTN_TILE = 512
CACHE_BLOCK = 512
FF_SLICE = 1024
MOD_TILE = 1024

C_QKR = (0, 512)
C_KV = (512, 768)
C_GQ = (768, 1024)
C_GK = (1024, 1280)
C_GV = (1280, 1792)
C_GR = (1792, 2304)


def _dot(a, b):
    return jnp.dot(a, b, preferred_element_type=F32)


def _dot_nt(a, b):
    return lax.dot_general(a, b, (((1,), (1,)), ((), ())), preferred_element_type=F32)


def _dot_tn(a, b):
    return lax.dot_general(a, b, (((0,), (0,)), ((), ())), preferred_element_type=F32)


def _rope_tile(t, c, sa, sb):
    return t * c + pltpu.roll(t, 96, 1) * sa + pltpu.roll(t, 32, 1) * sb


def _params(*sem):
    return pltpu.CompilerParams(dimension_semantics=sem, vmem_limit_bytes=VMEM_LIMIT)


def _const(shape):
    return pl.BlockSpec(shape, lambda *_: (0,) * len(shape), pipeline_mode=pl.Buffered(1))


def _mod_kernel(cs_ref, cp_ref, w_ref, b_ref, os_ref, op_ref):
    ns = cs_ref.shape[0]
    c = jnp.concatenate([cs_ref[...], cp_ref[...]], axis=0)
    s = (c * jax.nn.sigmoid(c)).astype(BF16)
    mod = _dot(s, w_ref[...].astype(BF16)) + b_ref[...]
    os_ref[...] = mod[0:ns]
    op_ref[...] = mod[ns:]


def _modulation(c_prompt, c_sample, w_ada, b_ada):
    d, n = w_ada.shape
    tn = MOD_TILE

    def whole(c):
        return pl.BlockSpec(c.shape, lambda j: (0, 0))

    def cols(c):
        return pl.BlockSpec((c.shape[0], tn), lambda j: (0, j))

    mod_s, mod_p = pl.pallas_call(
        _mod_kernel,
        grid=(n // tn,),
        in_specs=[whole(c_sample), whole(c_prompt),
                  pl.BlockSpec((d, tn), lambda j: (0, j)),
                  pl.BlockSpec((1, tn), lambda j: (0, j))],
        out_specs=[cols(c_sample), cols(c_prompt)],
        out_shape=[jax.ShapeDtypeStruct((c_sample.shape[0], n), F32), jax.ShapeDtypeStruct((c_prompt.shape[0], n), F32)],
        compiler_params=_params("arbitrary"),
        name="adaln_mod",
    )(c_sample, c_prompt, w_ada, b_ada.reshape(1, n))
    return mod_p, mod_s


def _proj_kernel(x_ref, sh_ref, sc_ref, g1_ref, win_ref, gql_ref, wuq_ref, gkv_ref, wukv_ref,
                 gqn_ref, gqr_ref, gkn_ref, gkr_ref, qone_ref, kbias_ref, tab_ref, wg_ref, bg_ref,
                 *outs, gpt, prompt):
    if prompt:
        lat_ref, krt_ref, q_ref, k_ref, v_ref, gq_ref, gk_ref, gv_ref, la_ref, gr_ref, dmin_ref = outs
    else:
        lat_ref, kr_ref, krt_ref, q_ref, gq_ref, gk_ref, gv_ref, la_ref, gr_ref, dmin_ref = outs
    tm, d = x_ref.shape
    x = x_ref[...]
    xn = x * lax.rsqrt(jnp.mean(x * x, axis=-1, keepdims=True) + EPS)
    h = (xn.reshape(gpt, tm // gpt, d) * (g1_ref[...] * (1.0 + sc_ref[...])) + sh_ref[...]).reshape(tm, d)
    hb = h.astype(BF16)

    def col(c):
        return _dot(hb, win_ref[:, c[0]:c[1]])

    tab = tab_ref[...]
    cos, sa, sb = tab[:, 0:LANES], tab[:, LANES:2 * LANES], tab[:, 2 * LANES:3 * LANES]

    qkr = col(C_QKR)
    krg = qkr[:, Q_LORA:]
    z = _dot(krg.astype(BF16), wg_ref[...]) + bg_ref[...]
    la = (jnp.minimum(z, 0.0) - jnp.log(1.0 + jnp.exp(-jnp.abs(z)))) * (1.0 / GLA_TAU)
    la_ref[...] = la
    dmin_ref[...] = jnp.min(la, axis=0, keepdims=True)
    r = col(C_GR)
    gr_ref[...] = (r * jax.nn.sigmoid(r)).astype(BF16)

    cq = qkr[:, 0:Q_LORA]
    cqn = cq * lax.rsqrt(jnp.mean(cq * cq, axis=-1, keepdims=True) + EPS) * gql_ref[...]
    qp = _dot(cqn.astype(BF16), wuq_ref[...])
    for hh in range(MLA_HEADS):
        nope = qp[:, hh * HEAD_PAD:hh * HEAD_PAD + QK_NOPE]
        rt = qp[:, hh * HEAD_PAD + QK_NOPE:(hh + 1) * HEAD_PAD]
        ss = jnp.sum(nope * nope, axis=-1, keepdims=True) + jnp.sum(rt * rt, axis=-1, keepdims=True)
        inv = lax.rsqrt(ss * (1.0 / QK_DIM) + EPS)
        q_ref[:, hh * HEAD_PAD:hh * HEAD_PAD + QK_NOPE] = (nope * inv * gqn_ref[...]).astype(BF16)
        rq = _rope_tile(rt * inv * gqr_ref[...], cos, sa, sb)
        q_ref[:, hh * HEAD_PAD + QK_NOPE:(hh + 1) * HEAD_PAD] = (rq + qone_ref[...]).astype(BF16)

    ckv = col(C_KV)
    lat = ckv * lax.rsqrt(jnp.mean(ckv * ckv, axis=-1, keepdims=True) + EPS) * gkv_ref[...]
    lat_ref[...] = lat
    krt_ref[...] = krg.T[0:QK_ROPE, :]
    if prompt:
        lane = lax.broadcasted_iota(jnp.int32, (1, LANES), 1)
        krm = jnp.where(lane < QK_ROPE, krg, 0.0)
        ssr = jnp.sum(krm * krm, axis=-1, keepdims=True)
        rk = _rope_tile(krm * gkr_ref[...], cos, sa, sb)
        kv = _dot(lat.astype(BF16), wukv_ref[...])
        for hh in range(MLA_HEADS):
            kn = kv[:, hh * QK_NOPE:(hh + 1) * QK_NOPE]
            inv = lax.rsqrt((jnp.sum(kn * kn, axis=-1, keepdims=True) + ssr) * (1.0 / QK_DIM) + EPS)
            k_ref[:, hh * HEAD_PAD:hh * HEAD_PAD + QK_NOPE] = (kn * inv * gkn_ref[...]).astype(BF16)
            k_ref[:, hh * HEAD_PAD + QK_NOPE:(hh + 1) * HEAD_PAD] = (rk * inv + kbias_ref[...]).astype(BF16)
        v_ref[...] = kv[:, MLA_HEADS * QK_NOPE:].astype(BF16)
    else:
        kr_ref[...] = krg[:, 0:QK_ROPE]

    gq_ref[...] = col(C_GQ)
    gk_ref[...] = col(C_GK)
    gv_ref[...] = col(C_GV).astype(BF16)


def _projection(x2, mod4, tab, w, *, rows_per_group, tm, prompt):
    n, d = x2.shape
    gpt = max(1, tm // rows_per_group)
    tpg = max(1, rows_per_group // tm)
    ntab = tab.shape[0] // tm

    def mod_spec(j):
        return pl.BlockSpec((gpt, None, 1, d), lambda i: ((i // tpg) if gpt == 1 else i, j, 0, 0))

    def rows(c):
        return pl.BlockSpec((tm, c), lambda i: (i, 0))

    def out(c, t):
        return rows(c), jax.ShapeDtypeStruct((n, c), t)

    dmin = (pl.BlockSpec((None, 1, GLA_QK), lambda i: (i, 0, 0)), jax.ShapeDtypeStruct((n // tm, 1, GLA_QK), F32))
    gla_outs = [out(GLA_QK, F32), out(GLA_QK, F32), out(GLA_WIDTH, BF16), out(GLA_QK, F32), out(GLA_WIDTH, BF16),
                dmin]
    qo = out(MLA_HEADS * HEAD_PAD, BF16)
    if prompt:
        krt = (pl.BlockSpec((None, QK_ROPE, tm), lambda i: (i // tpg, 0, i % tpg)),
               jax.ShapeDtypeStruct((n // rows_per_group, QK_ROPE, rows_per_group), F32))
        outs = [out(KV_LORA, F32), krt, qo, out(MLA_HEADS * HEAD_PAD, BF16), out(MLA_WIDTH, BF16)] + gla_outs
    else:
        krt = (pl.BlockSpec((QK_ROPE, tm), lambda i: (0, i)), jax.ShapeDtypeStruct((QK_ROPE, n), F32))
        outs = [out(KV_LORA, F32), out(QK_ROPE, F32), krt, qo] + gla_outs
    return pl.pallas_call(
        functools.partial(_proj_kernel, gpt=gpt, prompt=prompt),
        grid=(n // tm,),
        in_specs=[rows(d), mod_spec(0), mod_spec(1), _const((1, d)), _const(w["w_in"].shape),
                  _const((1, Q_LORA)), _const(w["w_uq"].shape), _const((1, KV_LORA)), _const(w["w_ukv"].shape),
                  _const((1, LANES)), _const((1, LANES)), _const((1, LANES)), _const((1, LANES)),
                  _const((1, LANES)), _const((1, LANES)),
                  pl.BlockSpec((tm, 3 * LANES), lambda i: (i % ntab, 0)),
                  _const(w["w_gate"].shape), _const((1, GLA_QK))],
        out_specs=[o[0] for o in outs],
        out_shape=[o[1] for o in outs],
        compiler_params=_params("arbitrary"),
        name="in_proj",
    )(x2, mod4, mod4, w["g_norm1"], w["w_in"], w["g_q_lora"], w["w_uq"], w["g_kv_lora"], w["w_ukv"],
      w["gqn"], w["gqr"], w["gkn"], w["gkr"], w["qone"], w["kbias"], tab, w["w_gate"], w["b_gate"])


def _attn_prompt_kernel(fast_ref, q_ref, k_ref, v_ref, o_ref, vx_ref, m_ref, acc_ref, *, tq):
    i = pl.program_id(1)
    heads = range(MLA_HEADS)

    @pl.when(i == 0)
    def _():
        for hh in heads:
            vx_ref[hh, :, 0:V_DIM] = v_ref[:, hh * V_DIM:(hh + 1) * V_DIM]
            vx_ref[hh, :, V_DIM:] = jnp.ones((v_ref.shape[0], V_DIM), BF16)

    def scores(j, hh, masked):
        hs = slice(hh * HEAD_PAD, (hh + 1) * HEAD_PAD)
        s = _dot_nt(q_ref[:, hs], k_ref[pl.ds(pl.multiple_of(j * tq, tq), tq), hs])
        if masked:
            qc = lax.broadcasted_iota(jnp.int32, (tq, tq), 0) // CHUNK
            kc = lax.broadcasted_iota(jnp.int32, (tq, tq), 1) // CHUNK
            s = jnp.where(kc <= qc, s, -jnp.inf)
        return s

    def fast_diagonal():
        hq = tq // 2
        k0 = pl.multiple_of(i * tq, tq)
        qc = lax.broadcasted_iota(jnp.int32, (hq, hq), 0) // CHUNK
        kc = lax.broadcasted_iota(jnp.int32, (hq, hq), 1) // CHUNK
        tri = kc <= qc
        for hh in heads:
            hs = slice(hh * HEAD_PAD, (hh + 1) * HEAD_PAD)
            top = jnp.where(tri, _dot_nt(q_ref[0:hq, hs], k_ref[pl.ds(k0, hq), hs]), -jnp.inf)
            acc_ref[hh, 0:hq] = _dot(jnp.exp2(top).astype(BF16), vx_ref[hh, pl.ds(k0, hq), :])
            bot = _dot_nt(q_ref[hq:tq, hs], k_ref[pl.ds(k0, tq), hs])
            bot = jnp.concatenate([bot[:, 0:hq], jnp.where(tri, bot[:, hq:tq], -jnp.inf)], axis=1)
            acc_ref[hh, hq:tq] = _dot(jnp.exp2(bot).astype(BF16), vx_ref[hh, pl.ds(k0, tq), :])

    def vblock(j, hh):
        return vx_ref[hh, pl.ds(pl.multiple_of(j * tq, tq), tq), :]

    def fast_block(j, masked):
        for hh in heads:
            acc_ref[hh] += _dot(jnp.exp2(scores(j, hh, masked)).astype(BF16), vblock(j, hh))

    def safe_block(j, masked):
        for hh in heads:
            s = scores(j, hh, masked)
            m = m_ref[hh]
            m_new = jnp.maximum(m, jnp.max(s, axis=1, keepdims=True))
            p = jnp.exp2(s - m_new).astype(BF16)
            acc_ref[hh] = jnp.exp2(m - m_new) * acc_ref[hh] + _dot(p, vblock(j, hh))
            m_ref[hh] = m_new

    def sweep(block):
        def body(jj, c):
            block(2 * jj, False)
            block(2 * jj + 1, False)
            return c

        lax.fori_loop(0, i // 2, body, 0)

        @pl.when(i % 2 == 1)
        def _():
            block(i - 1, False)

    @pl.when(fast_ref[0] == 1)
    def _():
        fast_diagonal()
        sweep(fast_block)

    @pl.when(fast_ref[0] != 1)
    def _():
        m_ref[...] = jnp.full(m_ref.shape, -jnp.inf, F32)
        acc_ref[...] = jnp.zeros(acc_ref.shape, F32)
        safe_block(i, True)
        sweep(safe_block)

    for hh in heads:
        acc = acc_ref[hh]
        o_ref[:, hh * V_DIM:(hh + 1) * V_DIM] = (acc[:, 0:V_DIM] / acc[:, V_DIM:]).astype(BF16)


def _attention_prompt(fast, q, k, v, *, batch, seq, tq):
    nq = seq // tq
    return pl.pallas_call(
        functools.partial(_attn_prompt_kernel, tq=tq),
        grid=(batch, nq),
        in_specs=[pl.BlockSpec(memory_space=pltpu.SMEM),
                  pl.BlockSpec((tq, MLA_HEADS * HEAD_PAD), lambda b, i: (b * nq + i, 0)),
                  pl.BlockSpec((seq, MLA_HEADS * HEAD_PAD), lambda b, i: (b, 0)),
                  pl.BlockSpec((seq, MLA_WIDTH), lambda b, i: (b, 0))],
        out_specs=pl.BlockSpec((tq, MLA_WIDTH), lambda b, i: (b * nq + i, 0)),
        out_shape=jax.ShapeDtypeStruct((batch * seq, MLA_WIDTH), BF16),
        scratch_shapes=[pltpu.VMEM((MLA_HEADS, seq, 2 * V_DIM), BF16), pltpu.VMEM((MLA_HEADS, tq, 1), F32),
                        pltpu.VMEM((MLA_HEADS, tq, 2 * V_DIM), F32)],
        compiler_params=_params("arbitrary", "arbitrary"),
        name="mla_attn_prompt",
    )(fast, q, k, v)


def _attn_sample_kernel(bound_ref, q_ref, lat_ref, krt_ref, latn_ref, krtn_ref, tab_ref, wukt_ref, wuv_ref,
                        gkn_ref, gkr_ref, o_ref, wq_ref, latb_ref, s_ref, *, tkb, fast):
    t = q_ref.shape[0] // 2
    past = lat_ref.shape[1]
    nk = MLA_HEADS * QK_NOPE
    half = QK_ROPE // 2
    streams = range(2)

    @pl.when(pl.program_id(0) == 0)
    def _():
        for e in streams:
            wq_ref[e, 0:nk, :] = wukt_ref[...]

    qr = []
    for e in streams:
        rows = slice(e * t, (e + 1) * t)
        parts = []
        for hh in range(MLA_HEADS):
            qn = (q_ref[rows, hh * HEAD_PAD:hh * HEAD_PAD + QK_NOPE].astype(F32) * gkn_ref[...]).astype(BF16)
            wq_ref[e, nk + hh * t:nk + (hh + 1) * t, :] = _dot(
                qn, wukt_ref[hh * QK_NOPE:(hh + 1) * QK_NOPE, :]).astype(BF16)
            parts.append(q_ref[rows, hh * HEAD_PAD + QK_NOPE:(hh + 1) * HEAD_PAD])
        qr.append(jnp.concatenate(parts, axis=0))

    def key_block(e, lat, krt, tab, width, valid):
        latb = lat.astype(BF16)
        g = _dot_nt(wq_ref[e], latb)
        ssr = jnp.sum(krt * krt, axis=0, keepdims=True)
        kg = krt * gkr_ref[...]
        x1, x2 = kg[0:half], kg[half:QK_ROPE]
        c, sn = tab[0:half], tab[half:QK_ROPE]
        rope = jnp.concatenate([x1 * c - x2 * sn, x2 * c + x1 * sn, jnp.zeros((LANES - QK_ROPE, width), F32)], axis=0)
        srope = _dot(qr[e], rope.astype(BF16))
        rows = []
        for hh in range(MLA_HEADS):
            kn = g[hh * QK_NOPE:(hh + 1) * QK_NOPE]
            inv = lax.rsqrt((jnp.sum(kn * kn, axis=0, keepdims=True) + ssr) * (1.0 / QK_DIM) + EPS)
            rows.append((g[nk + hh * t:nk + (hh + 1) * t] + srope[hh * t:(hh + 1) * t]) * inv)
        sc = jnp.concatenate(rows, axis=0) - bound_ref[0]
        if valid is not None:
            sc = jnp.where(valid, sc, -jnp.inf)
        return sc, latb

    def blocks():
        for blk in range(past // tkb):
            c0 = blk * tkb
            for e in streams:
                yield e, c0, tkb, key_block(e, lat_ref[e, c0:c0 + tkb, :], krt_ref[e, :, c0:c0 + tkb],
                                            tab_ref[:, c0:c0 + tkb], tkb, None)
        for e in streams:
            mine = lax.broadcasted_iota(jnp.int32, (1, 2 * t), 1) // t == e
            yield e, past, 2 * t, key_block(e, latn_ref[...], krtn_ref[...], tab_ref[:, past:past + 2 * t], 2 * t, mine)

    def finish(e, wlat, l):
        wlat = wlat.astype(BF16)
        for hh in range(MLA_HEADS):
            o = _dot(wlat[hh * t:(hh + 1) * t], wuv_ref[:, hh * V_DIM:(hh + 1) * V_DIM])
            o_ref[e * t:(e + 1) * t, hh * V_DIM:(hh + 1) * V_DIM] = (o / l[hh * t:(hh + 1) * t]).astype(BF16)

    if fast:
        wlat = [jnp.zeros((MLA_HEADS * t, KV_LORA), F32) for _ in streams]
        lsum = [jnp.zeros((MLA_HEADS * t, LANES), F32) for _ in streams]
        for e, _, width, (sc, latb) in blocks():
            p = jnp.exp2(sc)
            wlat[e] += _dot(p.astype(BF16), latb)
            for c in range(width // LANES):
                lsum[e] += p[:, c * LANES:(c + 1) * LANES]
        for e in streams:
            finish(e, wlat[e], jnp.sum(lsum[e], axis=1, keepdims=True))

    else:
        for e, c0, width, (sc, latb) in blocks():
            s_ref[e, :, c0:c0 + width] = sc
            latb_ref[e, c0:c0 + width, :] = latb
        for e in streams:
            s = s_ref[e]
            p = jnp.exp2(s - jnp.max(s, axis=1, keepdims=True))
            finish(e, _dot(p.astype(BF16), latb_ref[e]), jnp.sum(p, axis=1, keepdims=True))


def _attention_sample(q, past_lat, past_krt, lat_new, krt_new, w, *, tkb, fast):
    batch, past, _ = past_lat.shape
    t = q.shape[0] // batch
    half = QK_ROPE // 2
    inv = ROPE_THETA ** (-np.arange(half, dtype=np.float64) / half)
    pos = np.concatenate([np.arange(past), past + np.arange(t), past + np.arange(t)]).astype(np.float64)
    ang = inv[:, None] * pos[None, :]
    tab = jnp.asarray(np.concatenate([np.cos(ang), np.sin(ang)], axis=0).astype(np.float32))
    s_pad = past + 2 * t
    return pl.pallas_call(
        functools.partial(_attn_sample_kernel, tkb=tkb, fast=fast),
        grid=(batch // 2,),
        in_specs=[pl.BlockSpec(memory_space=pltpu.SMEM),
                  pl.BlockSpec((2 * t, MLA_HEADS * HEAD_PAD), lambda g: (g, 0)),
                  pl.BlockSpec((2, past, KV_LORA), lambda g: (g, 0, 0)),
                  pl.BlockSpec((2, QK_ROPE, past), lambda g: (g, 0, 0)),
                  pl.BlockSpec((2 * t, KV_LORA), lambda g: (g, 0)),
                  pl.BlockSpec((QK_ROPE, 2 * t), lambda g: (0, g)),
                  _const(tab.shape), _const(w["w_ukt"].shape), _const(w["w_uv"].shape),
                  _const((1, LANES)), _const((QK_ROPE, 1))],
        out_specs=pl.BlockSpec((2 * t, MLA_WIDTH), lambda g: (g, 0)),
        out_shape=jax.ShapeDtypeStruct((batch * t, MLA_WIDTH), BF16),
        scratch_shapes=[pltpu.VMEM((2, MLA_HEADS * (QK_NOPE + t), KV_LORA), BF16),
                        pltpu.VMEM((2, s_pad, KV_LORA), BF16),
                        pltpu.VMEM((2, MLA_HEADS * t, s_pad), F32)],
        compiler_params=_params("arbitrary"),
        name="mla_attn_sample",
    )(w["bound"], q, past_lat, past_krt, lat_new, krt_new, tab, w["w_ukt"], w["w_uv"], w["gkn"], w["gkr_col"])


def _gla_kernel(q_ref, k_ref, v_ref, la_ref, r_ref, s0_ref, g_ref, spread_ref, o_ref, sn_ref,
                st_ref, kp_ref, ap_ref, p_ref, on_ref, *, gpt, fast):
    t_idx = pl.program_id(1)
    L, W, P = CHUNK, GLA_QK, LANES
    R = q_ref.shape[0]
    n_chunks = R // L
    cpg = n_chunks // gpt
    n_pairs = GLA_HEADS // 2

    lane_p = lax.broadcasted_iota(jnp.int32, (1, P), 1)
    even = lane_p < GLA_DK
    bd_mask = (lax.broadcasted_iota(jnp.int32, (2 * GLA_DV, P), 0) // GLA_DV
               == lax.broadcasted_iota(jnp.int32, (2 * GLA_DV, P), 1) // GLA_DK)

    @pl.when(t_idx == 0)
    def _():
        kp_ref[0:SUB, :] = jnp.zeros((SUB, W), F32)
        ap_ref[0:SUB, :] = jnp.zeros((SUB, W), F32)
        for gi in range(gpt):
            for pr in range(n_pairs):
                tt = s0_ref[gi, 2 * pr:2 * pr + 2].reshape(2 * GLA_DK, GLA_DV).T
                st_ref[gi, pr] = jnp.where(bd_mask, jnp.concatenate([tt, tt], axis=0), 0.0)

    q = q_ref[...]
    k = k_ref[...]
    la = la_ref[...]

    tri = (lax.broadcasted_iota(jnp.int32, (L, L), 0) >= lax.broadcasted_iota(jnp.int32, (L, L), 1)).astype(BF16)
    la_hi = la.astype(BF16)
    la2 = jnp.concatenate([la_hi, (la - la_hi.astype(F32)).astype(BF16)], axis=1)
    bs = []
    for c in range(n_chunks):
        t2 = _dot(tri, la2[c * L:(c + 1) * L, :])
        bs.append(t2[:, 0:W] + t2[:, W:2 * W])
    b = (jnp.concatenate(bs, axis=0) if n_chunks > 1 else bs[0]) * LOG2E
    b3 = b.reshape(n_chunks, L, W)

    def chunk_row(r):
        return jnp.broadcast_to(b3[:, r:r + 1, :], (n_chunks, L, W)).reshape(R, W)

    b_sub = jnp.broadcast_to(b.reshape(R // SUB, SUB, W)[:, 0:1, :], (R // SUB, SUB, W)).reshape(R, W)
    sub = (lax.broadcasted_iota(jnp.int32, (R, W), 0) % L) // SUB

    qt = q * jnp.exp2(b - b_sub)
    zb = jnp.zeros((), BF16)
    zq = jnp.zeros((SUB, P), F32)
    n_sub = L // SUB
    first = 0 if fast else 1
    ktm = []
    for i in range(first, n_sub):
        kt = (k * jnp.exp2(chunk_row(i * SUB) - b)).astype(BF16)
        keep = (sub[:, 0:P] <= i) if fast else (sub[:, 0:P] < i)
        ktm.append([[jnp.where(keep & (even if e == 0 else ~even), kt[:, pr * P:(pr + 1) * P], zb)
                     for e in range(2)] for pr in range(n_pairs)])

    if fast:
        causal = (lax.broadcasted_iota(jnp.int32, (L, P), 0) >= lax.broadcasted_iota(jnp.int32, (L, P), 1) % L)
    else:
        a = jnp.exp(la)
        kp_ref[SUB:SUB + R, :] = k
        ap_ref[SUB:SUB + R, :] = a
        p_ref[:, 0:W] = (q * k).astype(BF16)
        e = a
        for d in range(1, SUB):
            if d > 1:
                e = e * ap_ref[SUB - d + 1:SUB - d + 1 + R, :]
            p_ref[:, d * W:(d + 1) * W] = (q * kp_ref[SUB - d:SUB - d + R, :] * e).astype(BF16)
        cband = _dot(p_ref[...], spread_ref[...])
        same_sub = (lax.broadcasted_iota(jnp.int32, (L, W), 0) // SUB
                    == (lax.broadcasted_iota(jnp.int32, (L, W), 1) % L) // SUB)

    qe = (q * jnp.exp2(b)).astype(BF16)
    kd = (k * jnp.exp2(chunk_row(L - 1) - b)).astype(BF16)
    zv = jnp.zeros((L, 2 * GLA_DV), BF16)

    o_intra, d_st, dec = {}, {}, []
    for c in range(n_chunks):
        rs = slice(c * L, (c + 1) * L)
        if not fast:
            a_band = jnp.where(same_sub, pltpu.roll(cband[rs], W - (SUB - 1), 1, stride=1, stride_axis=0), 0.0)
        dec.append(jnp.exp2(b[c * L + L - 1:c * L + L, :]))
        for pr in range(n_pairs):
            ls = slice(pr * P, (pr + 1) * P)
            lhs_c = jnp.concatenate(
                [jnp.concatenate([qt[c * L + r * SUB:c * L + (r + 1) * SUB, ls] if i == r else zq
                                  for i in range(first, n_sub)], axis=1) for r in range(n_sub)],
                axis=0).astype(BF16)
            rhs_c = jnp.concatenate([jnp.concatenate([m[pr][0][rs], m[pr][1][rs]], axis=0) for m in ktm], axis=1)
            scores = _dot_nt(lhs_c, rhs_c)
            a_tot = (jnp.where(causal, scores, 0.0) if fast else a_band[:, ls] + scores).astype(BF16)
            vp = v_ref[rs, 2 * pr * GLA_DV:(2 * pr + 2) * GLA_DV]
            v_bd = jnp.concatenate([jnp.concatenate([vp[:, 0:GLA_DV], zv[:, 0:GLA_DV]], axis=1),
                                    jnp.concatenate([zv[:, 0:GLA_DV], vp[:, GLA_DV:]], axis=1)], axis=0)
            o_intra[c, pr] = _dot(a_tot, v_bd)
            d_st[c, pr] = jnp.where(bd_mask, _dot_tn(vp, kd[rs, ls]), 0.0)
    st_in = {}
    for gi in range(gpt):
        for pr in range(n_pairs):
            st = st_ref[gi, pr]
            for c in range(gi * cpg, (gi + 1) * cpg):
                st_in[c, pr] = st.astype(BF16)
                st = st * dec[c][:, pr * P:(pr + 1) * P] + d_st[c, pr]
            st_ref[gi, pr] = st
    for c in range(n_chunks):
        rs = slice(c * L, (c + 1) * L)
        for pr in range(n_pairs):
            on_ref[rs, 2 * pr * GLA_DV:(2 * pr + 2) * GLA_DV] = (
                o_intra[c, pr] + _dot_nt(qe[rs, pr * P:(pr + 1) * P], st_in[c, pr]))

    for hh in range(GLA_HEADS):
        hs = slice(hh * GLA_DV, (hh + 1) * GLA_DV)
        o = on_ref[:, hs]
        on = o * lax.rsqrt(jnp.mean(o * o, axis=-1, keepdims=True) + EPS) * g_ref[:, hs]
        o_ref[:, hs] = (on * r_ref[:, hs].astype(F32)).astype(BF16)

    @pl.when(t_idx == pl.num_programs(1) - 1)
    def _():
        for gi in range(gpt):
            for pr in range(n_pairs):
                st = st_ref[gi, pr]
                tt = jnp.where(even, st[0:GLA_DV], st[GLA_DV:2 * GLA_DV])
                sn_ref[gi, 2 * pr:2 * pr + 2] = tt.T.reshape(2, GLA_DK, GLA_DV)


def _band_spread():
    m = np.zeros((SUB, GLA_HEADS, GLA_DK, GLA_QK), np.float32)
    for d in range(SUB):
        for h in range(GLA_HEADS):
            m[d, h, :, h * GLA_DK + SUB - 1 - d] = 1.0
    return jnp.asarray(m.reshape(SUB * GLA_QK, GLA_QK), BF16)


def _gla(gq, gk, gv, la, gr, s0, g_out, *, groups, rows_per_group, tc, fast):
    gpt = max(1, tc // rows_per_group)
    nt = max(1, rows_per_group // tc)

    def rows(c):
        return pl.BlockSpec((tc, c), lambda g, t: (g * nt + t, 0))

    state = pl.BlockSpec((gpt, GLA_HEADS, GLA_DK, GLA_DV), lambda g, t: (g, 0, 0, 0))
    spread = _band_spread()
    return pl.pallas_call(
        functools.partial(_gla_kernel, gpt=gpt, fast=fast),
        grid=(groups // gpt, nt),
        in_specs=[rows(GLA_QK), rows(GLA_QK), rows(GLA_WIDTH), rows(GLA_QK), rows(GLA_WIDTH), state,
                  _const((1, GLA_WIDTH)), _const(spread.shape)],
        out_specs=[rows(GLA_WIDTH), state],
        out_shape=[jax.ShapeDtypeStruct((groups * rows_per_group, GLA_WIDTH), BF16),
                   jax.ShapeDtypeStruct((groups, GLA_HEADS, GLA_DK, GLA_DV), F32)],
        scratch_shapes=[pltpu.VMEM((gpt, GLA_HEADS // 2, 2 * GLA_DV, LANES), F32),
                        pltpu.VMEM((SUB + tc, GLA_QK), F32), pltpu.VMEM((SUB + tc, GLA_QK), F32),
                        pltpu.VMEM((tc, SUB * GLA_QK), BF16), pltpu.VMEM((tc, GLA_WIDTH), F32)],
        compiler_params=_params("arbitrary", "arbitrary"),
        name="gla",
    )(gq, gk, gv, la, gr, s0, g_out, spread)


def _mlp_tile(x_ref, a_ref, b_ref, g1_ref, sh2_ref, sc2_ref, g2_ref, gn_ref, wo_ref, wu_ref, wd_ref, y_ref, *, gpt):
    tm, d = x_ref.shape

    def per_group(val, ref, scale_plus_one=False):
        m = ref[...]
        if scale_plus_one:
            m = 1.0 + m
        return (val.reshape(gpt, tm // gpt, d) * m).reshape(tm, d)

    mix = jnp.concatenate([a_ref[...], b_ref[...]], axis=1)
    x1 = x_ref[...] + per_group(_dot(mix, wo_ref[...]), g1_ref)
    xn = x1 * lax.rsqrt(jnp.mean(x1 * x1, axis=-1, keepdims=True) + EPS) * gn_ref[...]
    h2 = (per_group(xn, sc2_ref, True).reshape(gpt, tm // gpt, d) + sh2_ref[...]).reshape(tm, d).astype(BF16)
    acc = jnp.zeros((tm, d), F32)
    for j in range(wu_ref.shape[1] // FF_SLICE):
        u = jnp.maximum(_dot(h2, wu_ref[:, j * FF_SLICE:(j + 1) * FF_SLICE]), 0.0)
        acc += _dot((u * u).astype(BF16), wd_ref[j * FF_SLICE:(j + 1) * FF_SLICE, :])
    y_ref[...] = x1 + per_group(acc, g2_ref)


def _mlp_kernel(*refs, tiles, gpts):
    per_phase = 7
    gn_ref, wo_ref, wu_ref, wd_ref = refs[len(tiles) * per_phase:len(tiles) * per_phase + 4]
    y_refs = refs[len(tiles) * per_phase + 4:]
    i = pl.program_id(0)
    first = 0
    for p, (n_tiles, gpt) in enumerate(zip(tiles, gpts)):
        ins = refs[p * per_phase:(p + 1) * per_phase]

        @pl.when(jnp.logical_and(i >= first, i < first + n_tiles))
        def _(ins=ins, p=p, gpt=gpt):
            _mlp_tile(*ins, gn_ref, wo_ref, wu_ref, wd_ref, y_refs[p], gpt=gpt)

        first += n_tiles


def _mlp(phases, w, *, tm):
    d = phases[0][0].shape[1]
    tiles = [ph[0].shape[0] // tm for ph in phases]
    gpts = [max(1, tm // ph[4]) for ph in phases]
    in_specs, args, first = [], [], 0
    for (x2, a_out, b_out, mod4, rpg), n_tiles, gpt in zip(phases, tiles, gpts):
        tpg = max(1, rpg // tm)

        def tile(i, first=first, n_tiles=n_tiles):
            return jnp.clip(i - first, 0, n_tiles - 1)

        def rows(c, tile=tile):
            return pl.BlockSpec((tm, c), lambda i: (tile(i), 0))

        def mod_spec(j, tile=tile, gpt=gpt, tpg=tpg):
            return pl.BlockSpec((gpt, None, 1, d), lambda i: (tile(i) // tpg, j, 0, 0))

        in_specs += [rows(d), rows(MLA_WIDTH), rows(GLA_WIDTH), mod_spec(2), mod_spec(3), mod_spec(4), mod_spec(5)]
        args += [x2, a_out, b_out, mod4, mod4, mod4, mod4]
        first += n_tiles
    out_specs, first = [], 0
    for n_tiles in tiles:
        out_specs.append(pl.BlockSpec((tm, d), lambda i, first=first, n_tiles=n_tiles:
                                      (jnp.clip(i - first, 0, n_tiles - 1), 0)))
        first += n_tiles
    return pl.pallas_call(
        functools.partial(_mlp_kernel, tiles=tuple(tiles), gpts=tuple(gpts)),
        grid=(sum(tiles),),
        in_specs=in_specs + [_const((1, d)), _const(w["w_out"].shape), _const(w["w_up"].shape),
                             _const(w["w_down"].shape)],
        out_specs=out_specs,
        out_shape=[jax.ShapeDtypeStruct(ph[0].shape, F32) for ph in phases],
        compiler_params=_params("arbitrary"),
        name="out_proj_mlp",
    )(*args, w["g_norm2"], w["w_out"], w["w_up"], w["w_down"])


def _rope_table(start, count, repeat=1):
    half = QK_ROPE // 2
    inv = ROPE_THETA ** (-np.arange(half, dtype=np.float64) / half)
    ang = (start + np.arange(count, dtype=np.float64))[:, None] * inv[None, :]
    c, s, z = np.cos(ang), np.sin(ang), np.zeros_like(ang)
    tab = np.concatenate([c, c, z, z, -s, z, z, z, z, s, z, z], axis=1).astype(np.float32)
    return jnp.asarray(np.tile(tab, (repeat, 1)))


def _pad_gain(g_rope):
    return jnp.concatenate([g_rope, jnp.zeros((LANES - QK_ROPE,), F32)]).reshape(1, LANES)


def _relayout_kernel(wint_ref, wuq_ref, wukv_ref, wg_ref, win_o, wuq_o, wukv_o, wukt_o, wuv_o, wg_o):
    s = np.cumsum([0, Q_LORA, KV_LORA, QK_ROPE, GLA_QK, GLA_QK, GLA_WIDTH, GLA_GATE_RANK, GLA_WIDTH])

    def piece(i):
        return wint_ref[int(s[i]):int(s[i + 1]), :]

    d = wint_ref.shape[1]
    zeros = jnp.zeros((LANES - QK_ROPE - GLA_GATE_RANK, d), F32)
    win_o[:, 0:Q_LORA] = piece(0).T.astype(BF16)
    win_o[:, Q_LORA:C_QKR[1]] = jnp.concatenate([piece(2), piece(6), zeros], axis=0).T.astype(BF16)
    win_o[:, C_KV[0]:C_KV[1]] = piece(1).T.astype(BF16)
    win_o[:, C_GQ[0]:C_GQ[1]] = (piece(3).T * (GLA_DK ** -0.5)).astype(BF16)
    win_o[:, C_GK[0]:C_GK[1]] = piece(4).T.astype(BF16)
    win_o[:, C_GV[0]:C_GV[1]] = piece(5).T.astype(BF16)
    win_o[:, C_GR[0]:C_GR[1]] = piece(7).T.astype(BF16)

    zq = jnp.zeros((Q_LORA, HEAD_PAD - QK_DIM), BF16)
    kvw = QK_NOPE + V_DIM
    for hh in range(MLA_HEADS):
        wuq_o[:, hh * HEAD_PAD:hh * HEAD_PAD + QK_DIM] = wuq_ref[:, hh * QK_DIM:(hh + 1) * QK_DIM].astype(BF16)
        wuq_o[:, hh * HEAD_PAD + QK_DIM:(hh + 1) * HEAD_PAD] = zq
        uk = wukv_ref[:, hh * kvw:hh * kvw + QK_NOPE]
        uv = wukv_ref[:, hh * kvw + QK_NOPE:(hh + 1) * kvw].astype(BF16)
        wukv_o[:, hh * QK_NOPE:(hh + 1) * QK_NOPE] = uk.astype(BF16)
        wukv_o[:, (MLA_HEADS + hh) * V_DIM:(MLA_HEADS + hh + 1) * V_DIM] = uv
        wukt_o[hh * QK_NOPE:(hh + 1) * QK_NOPE, :] = uk.T.astype(BF16)
        wuv_o[:, hh * V_DIM:(hh + 1) * V_DIM] = uv

    wg_o[...] = jnp.zeros(wg_o.shape, BF16)
    wg_o[QK_ROPE:QK_ROPE + GLA_GATE_RANK, :] = wg_ref[...].astype(BF16)


def _relayout(w_in_t, w_uq, w_ukv, w_gate_up):
    d = w_in_t.shape[1]
    shapes = [(d, C_GR[1]), (Q_LORA, MLA_HEADS * HEAD_PAD), (KV_LORA, MLA_HEADS * (QK_NOPE + V_DIM)),
              (MLA_HEADS * QK_NOPE, KV_LORA), (KV_LORA, MLA_WIDTH), (LANES, GLA_QK)]
    return pl.pallas_call(
        _relayout_kernel,
        out_shape=[jax.ShapeDtypeStruct(sh, BF16) for sh in shapes],
        compiler_params=pltpu.CompilerParams(vmem_limit_bytes=VMEM_LIMIT),
        name="weight_relayout",
    )(w_in_t, w_uq, w_ukv, w_gate_up)


def _prep_weights(w_in, g_norm1, g_q_lora, w_uq, g_kv_lora, w_ukv, g_q_head, g_k_head,
                  w_gate_up, b_gate_up, g_gla_out, w_out, g_norm2, w_up, w_down):
    d = w_in.shape[0]
    w_in_p, w_uq_p, w_ukv_p, w_ukt, w_uv, w_gate = _relayout(w_in.T, w_uq, w_ukv, w_gate_up)
    qscale = QK_DIM ** -0.5 * LOG2E
    bound = 1.02 * QK_DIM ** 0.5 * LOG2E * jnp.max(jnp.abs(g_q_head)) * jnp.max(jnp.abs(g_k_head))
    lane = jnp.arange(LANES) == QK_ROPE
    return {
        "w_in": w_in_p, "g_norm1": g_norm1.reshape(1, d), "g_q_lora": g_q_lora.reshape(1, Q_LORA),
        "w_uq": w_uq_p, "g_kv_lora": g_kv_lora.reshape(1, KV_LORA), "w_ukv": w_ukv_p, "w_ukt": w_ukt, "w_uv": w_uv,
        "gkr_col": g_k_head[QK_NOPE:].reshape(QK_ROPE, 1),
        "qone": lane.astype(F32).reshape(1, LANES), "kbias": jnp.where(lane, -bound, 0.0).reshape(1, LANES),
        "fast_softmax": (bound <= MAX_FIXED_SHIFT).astype(jnp.int32).reshape(1), "bound": bound.reshape(1),
        "gqn": (g_q_head[:QK_NOPE] * qscale).reshape(1, LANES), "gqr": _pad_gain(g_q_head[QK_NOPE:] * qscale),
        "gkn": g_k_head[:QK_NOPE].reshape(1, LANES), "gkr": _pad_gain(g_k_head[QK_NOPE:]),
        "w_gate": w_gate, "b_gate": b_gate_up.reshape(1, GLA_QK),
        "g_gla_out": g_gla_out.reshape(1, GLA_WIDTH), "w_out": w_out.astype(BF16),
        "g_norm2": g_norm2.reshape(1, d), "w_up": w_up.astype(BF16), "w_down": w_down.astype(BF16),
    }


def _mixers(x, mod, past_lat, past_kr, s0, w, *, tm):
    batch, seq, d = x.shape
    n = batch * seq
    past = 0 if past_lat is None else past_lat.shape[1]
    x2 = x.reshape(n, d)
    mod4 = mod.reshape(batch, 6, 1, d)
    tm = min(tm, n)
    tab = _rope_table(past, seq, repeat=max(1, tm // seq))
    if past == 0:
        lat, krt, q, k, v, gq, gk, gv, la, gr, dmin = _projection(x2, mod4, tab, w, rows_per_group=seq, tm=tm,
                                                                  prompt=True)
        kr = jnp.swapaxes(krt, 1, 2)
        a_out = _attention_prompt(w["fast_softmax"], q, k, v, batch=batch, seq=seq, tq=min(ATTN_TILE, seq))
    else:
        assert seq == CHUNK and past % CHUNK == 0
        lat, kr, krt, q, gq, gk, gv, la, gr, dmin = _projection(x2, mod4, tab, w, rows_per_group=seq, tm=tm,
                                                                prompt=False)
        attend = functools.partial(_attention_sample, q, past_lat, jnp.swapaxes(past_kr, 1, 2), lat, krt, w,
                                   tkb=min(CACHE_BLOCK, past))
        a_out = lax.cond(w["fast_softmax"][0] == 1, functools.partial(attend, fast=True),
                         functools.partial(attend, fast=False))
    gap = -(SUB - 1) * LOG2E * jnp.min(dmin)
    gla = functools.partial(_gla, gq, gk, gv, la, gr, s0, w["g_gla_out"], groups=batch, rows_per_group=seq, tc=tm)
    b_out, s_new = lax.cond(gap <= MAX_SUB_DECAY, functools.partial(gla, fast=True), functools.partial(gla, fast=False))
    return (x2, a_out, b_out, mod4, seq), (lat.reshape(batch, seq, KV_LORA), kr.reshape(batch, seq, QK_ROPE), s_new)


def kernel(x_prompt, x_sample, cache_mla_latent, cache_mla_krope, state_gla, c_prompt, c_sample,
           w_ada, b_ada, g_norm1, w_in, g_q_lora, w_uq, g_kv_lora, w_ukv, g_q_head, g_k_head,
           w_gate_up, b_gate_up, g_gla_out, w_out, g_norm2, w_up, w_down):
    nb = x_prompt.shape[0]
    depth = w_ada.shape[0]
    y_p, y_s = x_prompt, x_sample
    outs = [[] for _ in range(6)]
    for l in range(depth):
        w = _prep_weights(w_in[l], g_norm1[l], g_q_lora[l], w_uq[l], g_kv_lora[l], w_ukv[l], g_q_head[l],
                          g_k_head[l], w_gate_up[l], b_gate_up[l], g_gla_out[l], w_out[l], g_norm2[l],
                          w_up[l], w_down[l])
        mod_p, mod_s = _modulation(c_prompt, c_sample, w_ada[l], b_ada[l])
        zero_state = jnp.zeros((nb, GLA_HEADS, GLA_DK, GLA_DV), x_prompt.dtype)
        mix_p, new_p = _mixers(y_p, mod_p, None, None, zero_state, w, tm=ROW_TILE)
        mix_s, new_s = _mixers(y_s, mod_s, cache_mla_latent[l], cache_mla_krope[l], state_gla[l], w, tm=ROW_TILE)
        y2_p, y2_s = _mlp([mix_p, mix_s], w, tm=ROW_TILE)
        y_p, y_s = y2_p.reshape(y_p.shape), y2_s.reshape(y_s.shape)
        for o, new in zip(outs, new_p + new_s):
            o.append(new)
    return (y_p, y_s) + tuple(jnp.stack(o) for o in outs)
```

```python
import functools

import jax
import jax.numpy as jnp
import numpy as np
from jax import lax
from jax.experimental import pallas as pl
from jax.experimental.pallas import tpu as pltpu

F32 = jnp.float32
BF16 = jnp.bfloat16

CHUNK = 64
EPS = 1e-6
MLA_HEADS = 4
Q_LORA = 384
KV_LORA = 256
QK_NOPE = 128
QK_ROPE = 64
QK_DIM = QK_NOPE + QK_ROPE
V_DIM = 128
ROPE_THETA = 10000.0
GLA_HEADS = 4
GLA_DK = 64
GLA_DV = 128
GLA_GATE_RANK = 16
GLA_TAU = 16.0
GLA_QK = GLA_HEADS * GLA_DK
GLA_WIDTH = GLA_HEADS * GLA_DV
MLA_WIDTH = MLA_HEADS * V_DIM
HEAD_PAD = 256
SUB = 8
LOG2E = 1.4426950408889634
MAX_FIXED_SHIFT = 48.0
MAX_SUB_DECAY = 60.0

LANES = 128
VMEM_LIMIT = 56 * 1024 * 1024
ROW_TILE = 512
ATTN_TILE = 512
CACHE_BLOCK = 512
FF_SLICE = 1024

C_QKR = (0, 512)
C_KV = (512, 768)
C_GQ = (768, 1024)
C_GK = (1024, 1280)
C_GV = (1280, 1792)
C_GR = (1792, 2304)


def _dot(a, b):
    return jnp.dot(a, b, preferred_element_type=F32)


def _dot_nt(a, b):
    return lax.dot_general(a, b, (((1,), (1,)), ((), ())), preferred_element_type=F32)


def _dot_tn(a, b):
    return lax.dot_general(a, b, (((0,), (0,)), ((), ())), preferred_element_type=F32)


def _rope_tile(t, c, sa, sb):
    return t * c + pltpu.roll(t, 96, 1) * sa + pltpu.roll(t, 32, 1) * sb


def _params(*sem):
    return pltpu.CompilerParams(dimension_semantics=sem, vmem_limit_bytes=VMEM_LIMIT)


def _const(shape):
    return pl.BlockSpec(shape, lambda *_: (0,) * len(shape), pipeline_mode=pl.Buffered(1))


def _mod_kernel(cs_ref, cp_ref, w_ref, b_ref, os_ref, op_ref):
    ns = cs_ref.shape[0]
    c = jnp.concatenate([cs_ref[...], cp_ref[...]], axis=0)
    s = (c * jax.nn.sigmoid(c)).astype(BF16)
    mod = _dot(s, w_ref[...].astype(BF16)) + b_ref[...]
    for r in range(ns):
        os_ref[r] = mod[r:r + 1]
    for r in range(op_ref.shape[0]):
        op_ref[r] = mod[ns + r:ns + r + 1]


def _modulation(c_prompt, c_sample, w_ada, b_ada):
    d, n = w_ada.shape
    vectors = n // d

    def whole(c):
        return pl.BlockSpec(c.shape, lambda j: (0, 0))

    def vector(c):
        return pl.BlockSpec((c.shape[0], None, 1, d), lambda j: (0, j, 0, 0))

    def shape(c):
        return jax.ShapeDtypeStruct((c.shape[0], vectors, 1, d), F32)

    mod_s, mod_p = pl.pallas_call(
        _mod_kernel,
        grid=(vectors,),
        in_specs=[whole(c_sample), whole(c_prompt),
                  pl.BlockSpec((d, d), lambda j: (0, j)),
                  pl.BlockSpec((1, d), lambda j: (0, j))],
        out_specs=[vector(c_sample), vector(c_prompt)],
        out_shape=[shape(c_sample), shape(c_prompt)],
        compiler_params=_params("arbitrary"),
        name="adaln_mod",
    )(c_sample, c_prompt, w_ada, b_ada.reshape(1, n))
    return mod_p, mod_s


def _proj_kernel(x_ref, sh_ref, sc_ref, g1_ref, win_ref, gql_ref, wuq_ref, gkv_ref, wukv_ref,
                 gqn_ref, gqr_ref, gkn_ref, gkr_ref, qone_ref, kbias_ref, tab_ref, wg_ref, bg_ref,
                 *outs, gpt, prompt):
    if prompt:
        lat_ref, krt_ref, q_ref, k_ref, v_ref, gq_ref, gk_ref, gv_ref, la_ref, gr_ref, dmin_ref = outs
    else:
        lat_ref, kr_ref, krt_ref, q_ref, gq_ref, gk_ref, gv_ref, la_ref, gr_ref, dmin_ref = outs
    tm, d = x_ref.shape
    x = x_ref[...]
    xn = x * lax.rsqrt(jnp.mean(x * x, axis=-1, keepdims=True) + EPS)
    h = (xn.reshape(gpt, tm // gpt, d) * (g1_ref[...] * (1.0 + sc_ref[...])) + sh_ref[...]).reshape(tm, d)
    hb = h.astype(BF16)

    def col(c):
        return _dot(hb, win_ref[:, c[0]:c[1]])

    tab = tab_ref[...]
    cos, sa, sb = tab[:, 0:LANES], tab[:, LANES:2 * LANES], tab[:, 2 * LANES:3 * LANES]

    qkr = col(C_QKR)
    krg = qkr[:, Q_LORA:]
    z = _dot(krg.astype(BF16), wg_ref[...]) + bg_ref[...]
    la = (jnp.minimum(z, 0.0) - jnp.log(1.0 + jnp.exp(-jnp.abs(z)))) * (1.0 / GLA_TAU)
    la_ref[...] = la
    dmin_ref[...] = jnp.min(la, axis=0, keepdims=True)
    r = col(C_GR)
    gr_ref[...] = (r * jax.nn.sigmoid(r)).astype(BF16)

    cq = qkr[:, 0:Q_LORA]
    cqn = cq * lax.rsqrt(jnp.mean(cq * cq, axis=-1, keepdims=True) + EPS) * gql_ref[...]
    qp = _dot(cqn.astype(BF16), wuq_ref[...])
    for hh in range(MLA_HEADS):
        nope = qp[:, hh * HEAD_PAD:hh * HEAD_PAD + QK_NOPE]
        rt = qp[:, hh * HEAD_PAD + QK_NOPE:(hh + 1) * HEAD_PAD]
        ss = jnp.sum(nope * nope, axis=-1, keepdims=True) + jnp.sum(rt * rt, axis=-1, keepdims=True)
        inv = lax.rsqrt(ss * (1.0 / QK_DIM) + EPS)
        q_ref[:, hh * HEAD_PAD:hh * HEAD_PAD + QK_NOPE] = (nope * inv * gqn_ref[...]).astype(BF16)
        rq = _rope_tile(rt * inv * gqr_ref[...], cos, sa, sb)
        q_ref[:, hh * HEAD_PAD + QK_NOPE:(hh + 1) * HEAD_PAD] = (rq + qone_ref[...]).astype(BF16)

    ckv = col(C_KV)
    lat = ckv * lax.rsqrt(jnp.mean(ckv * ckv, axis=-1, keepdims=True) + EPS) * gkv_ref[...]
    lat_ref[...] = lat
    krt_ref[...] = krg.T[0:QK_ROPE, :]
    if prompt:
        lane = lax.broadcasted_iota(jnp.int32, (1, LANES), 1)
        krm = jnp.where(lane < QK_ROPE, krg, 0.0)
        ssr = jnp.sum(krm * krm, axis=-1, keepdims=True)
        rk = _rope_tile(krm * gkr_ref[...], cos, sa, sb)
        kv = _dot(lat.astype(BF16), wukv_ref[...])
        for hh in range(MLA_HEADS):
            kn = kv[:, hh * QK_NOPE:(hh + 1) * QK_NOPE]
            inv = lax.rsqrt((jnp.sum(kn * kn, axis=-1, keepdims=True) + ssr) * (1.0 / QK_DIM) + EPS)
            k_ref[:, hh * HEAD_PAD:hh * HEAD_PAD + QK_NOPE] = (kn * inv * gkn_ref[...]).astype(BF16)
            k_ref[:, hh * HEAD_PAD + QK_NOPE:(hh + 1) * HEAD_PAD] = (rk * inv + kbias_ref[...]).astype(BF16)
        v_ref[...] = kv[:, MLA_HEADS * QK_NOPE:].astype(BF16)
    else:
        kr_ref[...] = krg[:, 0:QK_ROPE]

    gq_ref[...] = col(C_GQ)
    gk_ref[...] = col(C_GK)
    gv_ref[...] = col(C_GV).astype(BF16)


def _projection(x2, mod4, tab, w, *, rows_per_group, tm, prompt):
    n, d = x2.shape
    gpt = max(1, tm // rows_per_group)
    tpg = max(1, rows_per_group // tm)
    ntab = tab.shape[0] // tm

    def mod_spec(j):
        return pl.BlockSpec((gpt, None, 1, d), lambda i: ((i // tpg) if gpt == 1 else i, j, 0, 0))

    def rows(c):
        return pl.BlockSpec((tm, c), lambda i: (i, 0))

    def out(c, t):
        return rows(c), jax.ShapeDtypeStruct((n, c), t)

    dmin = (pl.BlockSpec((None, 1, GLA_QK), lambda i: (i, 0, 0)), jax.ShapeDtypeStruct((n // tm, 1, GLA_QK), F32))
    gla_outs = [out(GLA_QK, F32), out(GLA_QK, F32), out(GLA_WIDTH, BF16), out(GLA_QK, F32), out(GLA_WIDTH, BF16),
                dmin]
    qo = out(MLA_HEADS * HEAD_PAD, BF16)
    if prompt:
        krt = (pl.BlockSpec((None, QK_ROPE, tm), lambda i: (i // tpg, 0, i % tpg)),
               jax.ShapeDtypeStruct((n // rows_per_group, QK_ROPE, rows_per_group), F32))
        outs = [out(KV_LORA, F32), krt, qo, out(MLA_HEADS * HEAD_PAD, BF16), out(MLA_WIDTH, BF16)] + gla_outs
    else:
        krt = (pl.BlockSpec((QK_ROPE, tm), lambda i: (0, i)), jax.ShapeDtypeStruct((QK_ROPE, n), F32))
        outs = [out(KV_LORA, F32), out(QK_ROPE, F32), krt, qo] + gla_outs
    return pl.pallas_call(
        functools.partial(_proj_kernel, gpt=gpt, prompt=prompt),
        grid=(n // tm,),
        in_specs=[rows(d), mod_spec(0), mod_spec(1), _const((1, d)), _const(w["w_in"].shape),
                  _const((1, Q_LORA)), _const(w["w_uq"].shape), _const((1, KV_LORA)), _const(w["w_ukv"].shape),
                  _const((1, LANES)), _const((1, LANES)), _const((1, LANES)), _const((1, LANES)),
                  _const((1, LANES)), _const((1, LANES)),
                  pl.BlockSpec((tm, 3 * LANES), lambda i: (i % ntab, 0)),
                  _const(w["w_gate"].shape), _const((1, GLA_QK))],
        out_specs=[o[0] for o in outs],
        out_shape=[o[1] for o in outs],
        compiler_params=_params("arbitrary"),
        name="in_proj",
    )(x2, mod4, mod4, w["g_norm1"], w["w_in"], w["g_q_lora"], w["w_uq"], w["g_kv_lora"], w["w_ukv"],
      w["gqn"], w["gqr"], w["gkn"], w["gkr"], w["qone"], w["kbias"], tab, w["w_gate"], w["b_gate"])


def _attn_prompt_kernel(fast_ref, q_ref, k_ref, v_ref, o_ref, vx_ref, m_ref, acc_ref, *, tq):
    i = pl.program_id(1)
    heads = range(MLA_HEADS)

    @pl.when(i == 0)
    def _():
        for hh in heads:
            vx_ref[hh, :, 0:V_DIM] = v_ref[:, hh * V_DIM:(hh + 1) * V_DIM]
            vx_ref[hh, :, V_DIM:] = jnp.ones((v_ref.shape[0], V_DIM), BF16)

    def scores(j, hh, masked):
        hs = slice(hh * HEAD_PAD, (hh + 1) * HEAD_PAD)
        s = _dot_nt(q_ref[:, hs], k_ref[pl.ds(pl.multiple_of(j * tq, tq), tq), hs])
        if masked:
            qc = lax.broadcasted_iota(jnp.int32, (tq, tq), 0) // CHUNK
            kc = lax.broadcasted_iota(jnp.int32, (tq, tq), 1) // CHUNK
            s = jnp.where(kc <= qc, s, -jnp.inf)
        return s

    def fast_diagonal():
        hq = tq // 2
        k0 = pl.multiple_of(i * tq, tq)
        qc = lax.broadcasted_iota(jnp.int32, (hq, hq), 0) // CHUNK
        kc = lax.broadcasted_iota(jnp.int32, (hq, hq), 1) // CHUNK
        tri = kc <= qc
        for hh in heads:
            hs = slice(hh * HEAD_PAD, (hh + 1) * HEAD_PAD)
            top = jnp.where(tri, _dot_nt(q_ref[0:hq, hs], k_ref[pl.ds(k0, hq), hs]), -jnp.inf)
            acc_ref[hh, 0:hq] = _dot(jnp.exp2(top).astype(BF16), vx_ref[hh, pl.ds(k0, hq), :])
            bot = _dot_nt(q_ref[hq:tq, hs], k_ref[pl.ds(k0, tq), hs])
            bot = jnp.concatenate([bot[:, 0:hq], jnp.where(tri, bot[:, hq:tq], -jnp.inf)], axis=1)
            acc_ref[hh, hq:tq] = _dot(jnp.exp2(bot).astype(BF16), vx_ref[hh, pl.ds(k0, tq), :])

    def vblock(j, hh):
        return vx_ref[hh, pl.ds(pl.multiple_of(j * tq, tq), tq), :]

    def fast_block(j, masked):
        for hh in heads:
            acc_ref[hh] += _dot(jnp.exp2(scores(j, hh, masked)).astype(BF16), vblock(j, hh))

    def safe_block(j, masked):
        for hh in heads:
            s = scores(j, hh, masked)
            m = m_ref[hh]
            m_new = jnp.maximum(m, jnp.max(s, axis=1, keepdims=True))
            p = jnp.exp2(s - m_new).astype(BF16)
            acc_ref[hh] = jnp.exp2(m - m_new) * acc_ref[hh] + _dot(p, vblock(j, hh))
            m_ref[hh] = m_new

    def sweep(block):
        def body(jj, c):
            block(2 * jj, False)
            block(2 * jj + 1, False)
            return c

        lax.fori_loop(0, i // 2, body, 0)

        @pl.when(i % 2 == 1)
        def _():
            block(i - 1, False)

    @pl.when(fast_ref[0] == 1)
    def _():
        fast_diagonal()
        sweep(fast_block)

    @pl.when(fast_ref[0] != 1)
    def _():
        m_ref[...] = jnp.full(m_ref.shape, -jnp.inf, F32)
        acc_ref[...] = jnp.zeros(acc_ref.shape, F32)
        safe_block(i, True)
        sweep(safe_block)

    for hh in heads:
        acc = acc_ref[hh]
        o_ref[:, hh * V_DIM:(hh + 1) * V_DIM] = (acc[:, 0:V_DIM] / acc[:, V_DIM:]).astype(BF16)


def _attention_prompt(fast, q, k, v, *, batch, seq, tq):
    nq = seq // tq
    return pl.pallas_call(
        functools.partial(_attn_prompt_kernel, tq=tq),
        grid=(batch, nq),
        in_specs=[pl.BlockSpec(memory_space=pltpu.SMEM),
                  pl.BlockSpec((tq, MLA_HEADS * HEAD_PAD), lambda b, i: (b * nq + i, 0)),
                  pl.BlockSpec((seq, MLA_HEADS * HEAD_PAD), lambda b, i: (b, 0)),
                  pl.BlockSpec((seq, MLA_WIDTH), lambda b, i: (b, 0))],
        out_specs=pl.BlockSpec((tq, MLA_WIDTH), lambda b, i: (b * nq + i, 0)),
        out_shape=jax.ShapeDtypeStruct((batch * seq, MLA_WIDTH), BF16),
        scratch_shapes=[pltpu.VMEM((MLA_HEADS, seq, 2 * V_DIM), BF16), pltpu.VMEM((MLA_HEADS, tq, 1), F32),
                        pltpu.VMEM((MLA_HEADS, tq, 2 * V_DIM), F32)],
        compiler_params=_params("arbitrary", "arbitrary"),
        name="mla_attn_prompt",
    )(fast, q, k, v)


def _attn_sample_kernel(bound_ref, q_ref, lat_ref, krt_ref, latn_ref, krtn_ref, tab_ref, wukt_ref, wuv_ref,
                        gkn_ref, gkr_ref, o_ref, wq_ref, latb_ref, s_ref, *, tkb, fast):
    t = q_ref.shape[0] // 2
    past = lat_ref.shape[1]
    nk = MLA_HEADS * QK_NOPE
    half = QK_ROPE // 2
    streams = range(2)

    @pl.when(pl.program_id(0) == 0)
    def _():
        for e in streams:
            wq_ref[e, 0:nk, :] = wukt_ref[...]

    qr = []
    for e in streams:
        rows = slice(e * t, (e + 1) * t)
        parts = []
        for hh in range(MLA_HEADS):
            qn = (q_ref[rows, hh * HEAD_PAD:hh * HEAD_PAD + QK_NOPE].astype(F32) * gkn_ref[...]).astype(BF16)
            wq_ref[e, nk + hh * t:nk + (hh + 1) * t, :] = _dot(
                qn, wukt_ref[hh * QK_NOPE:(hh + 1) * QK_NOPE, :]).astype(BF16)
            parts.append(q_ref[rows, hh * HEAD_PAD + QK_NOPE:(hh + 1) * HEAD_PAD])
        qr.append(jnp.concatenate(parts, axis=0))

    def key_block(e, lat, krt, tab, width, valid):
        latb = lat.astype(BF16)
        g = _dot_nt(wq_ref[e], latb)
        ssr = jnp.sum(krt * krt, axis=0, keepdims=True)
        kg = krt * gkr_ref[...]
        x1, x2 = kg[0:half], kg[half:QK_ROPE]
        c, sn = tab[0:half], tab[half:QK_ROPE]
        rope = jnp.concatenate([x1 * c - x2 * sn, x2 * c + x1 * sn, jnp.zeros((LANES - QK_ROPE, width), F32)], axis=0)
        srope = _dot(qr[e], rope.astype(BF16))
        rows = []
        for hh in range(MLA_HEADS):
            kn = g[hh * QK_NOPE:(hh + 1) * QK_NOPE]
            inv = lax.rsqrt((jnp.sum(kn * kn, axis=0, keepdims=True) + ssr) * (1.0 / QK_DIM) + EPS)
            rows.append((g[nk + hh * t:nk + (hh + 1) * t] + srope[hh * t:(hh + 1) * t]) * inv)
        sc = jnp.concatenate(rows, axis=0) - bound_ref[0]
        if valid is not None:
            sc = jnp.where(valid, sc, -jnp.inf)
        return sc, latb

    def blocks():
        for blk in range(past // tkb):
            c0 = blk * tkb
            for e in streams:
                yield e, c0, tkb, key_block(e, lat_ref[e, c0:c0 + tkb, :], krt_ref[e, :, c0:c0 + tkb],
                                            tab_ref[:, c0:c0 + tkb], tkb, None)
        for e in streams:
            mine = lax.broadcasted_iota(jnp.int32, (1, 2 * t), 1) // t == e
            yield e, past, 2 * t, key_block(e, latn_ref[...], krtn_ref[...], tab_ref[:, past:past + 2 * t], 2 * t, mine)

    def finish(e, wlat, l):
        wlat = wlat.astype(BF16)
        for hh in range(MLA_HEADS):
            o = _dot(wlat[hh * t:(hh + 1) * t], wuv_ref[:, hh * V_DIM:(hh + 1) * V_DIM])
            o_ref[e * t:(e + 1) * t, hh * V_DIM:(hh + 1) * V_DIM] = (o / l[hh * t:(hh + 1) * t]).astype(BF16)

    if fast:
        wlat = [jnp.zeros((MLA_HEADS * t, KV_LORA), F32) for _ in streams]
        lsum = [jnp.zeros((MLA_HEADS * t, LANES), F32) for _ in streams]
        for e, _, width, (sc, latb) in blocks():
            p = jnp.exp2(sc)
            wlat[e] += _dot(p.astype(BF16), latb)
            for c in range(width // LANES):
                lsum[e] += p[:, c * LANES:(c + 1) * LANES]
        for e in streams:
            finish(e, wlat[e], jnp.sum(lsum[e], axis=1, keepdims=True))

    else:
        for e, c0, width, (sc, latb) in blocks():
            s_ref[e, :, c0:c0 + width] = sc
            latb_ref[e, c0:c0 + width, :] = latb
        for e in streams:
            s = s_ref[e]
            p = jnp.exp2(s - jnp.max(s, axis=1, keepdims=True))
            finish(e, _dot(p.astype(BF16), latb_ref[e]), jnp.sum(p, axis=1, keepdims=True))


def _attention_sample(q, past_lat, past_krt, lat_new, krt_new, w, *, tkb, fast):
    batch, past, _ = past_lat.shape
    t = q.shape[0] // batch
    half = QK_ROPE // 2
    inv = ROPE_THETA ** (-np.arange(half, dtype=np.float64) / half)
    pos = np.concatenate([np.arange(past), past + np.arange(t), past + np.arange(t)]).astype(np.float64)
    ang = inv[:, None] * pos[None, :]
    tab = jnp.asarray(np.concatenate([np.cos(ang), np.sin(ang)], axis=0).astype(np.float32))
    s_pad = past + 2 * t
    return pl.pallas_call(
        functools.partial(_attn_sample_kernel, tkb=tkb, fast=fast),
        grid=(batch // 2,),
        in_specs=[pl.BlockSpec(memory_space=pltpu.SMEM),
                  pl.BlockSpec((2 * t, MLA_HEADS * HEAD_PAD), lambda g: (g, 0)),
                  pl.BlockSpec((2, past, KV_LORA), lambda g: (g, 0, 0)),
                  pl.BlockSpec((2, QK_ROPE, past), lambda g: (g, 0, 0)),
                  pl.BlockSpec((2 * t, KV_LORA), lambda g: (g, 0)),
                  pl.BlockSpec((QK_ROPE, 2 * t), lambda g: (0, g)),
                  _const(tab.shape), _const(w["w_ukt"].shape), _const(w["w_uv"].shape),
                  _const((1, LANES)), _const((QK_ROPE, 1))],
        out_specs=pl.BlockSpec((2 * t, MLA_WIDTH), lambda g: (g, 0)),
        out_shape=jax.ShapeDtypeStruct((batch * t, MLA_WIDTH), BF16),
        scratch_shapes=[pltpu.VMEM((2, MLA_HEADS * (QK_NOPE + t), KV_LORA), BF16),
                        pltpu.VMEM((2, s_pad, KV_LORA), BF16),
                        pltpu.VMEM((2, MLA_HEADS * t, s_pad), F32)],
        compiler_params=_params("arbitrary"),
        name="mla_attn_sample",
    )(w["bound"], q, past_lat, past_krt, lat_new, krt_new, tab, w["w_ukt"], w["w_uv"], w["gkn"], w["gkr_col"])


def _gla_kernel(q_ref, k_ref, v_ref, la_ref, r_ref, s0_ref, g_ref, spread_ref, o_ref, sn_ref,
                st_ref, kp_ref, ap_ref, p_ref, on_ref, *, gpt, fast):
    t_idx = pl.program_id(1)
    L, W, P = CHUNK, GLA_QK, LANES
    R = q_ref.shape[0]
    n_chunks = R // L
    cpg = n_chunks // gpt
    n_pairs = GLA_HEADS // 2

    lane_p = lax.broadcasted_iota(jnp.int32, (1, P), 1)
    even = lane_p < GLA_DK
    bd_mask = (lax.broadcasted_iota(jnp.int32, (2 * GLA_DV, P), 0) // GLA_DV
               == lax.broadcasted_iota(jnp.int32, (2 * GLA_DV, P), 1) // GLA_DK)

    @pl.when(t_idx == 0)
    def _():
        kp_ref[0:SUB, :] = jnp.zeros((SUB, W), F32)
        ap_ref[0:SUB, :] = jnp.zeros((SUB, W), F32)
        for gi in range(gpt):
            for pr in range(n_pairs):
                tt = s0_ref[gi, 2 * pr:2 * pr + 2].reshape(2 * GLA_DK, GLA_DV).T
                st_ref[gi, pr] = jnp.where(bd_mask, jnp.concatenate([tt, tt], axis=0), 0.0)

    q = q_ref[...]
    k = k_ref[...]
    la = la_ref[...]

    tri = (lax.broadcasted_iota(jnp.int32, (L, L), 0) >= lax.broadcasted_iota(jnp.int32, (L, L), 1)).astype(BF16)
    la_hi = la.astype(BF16)
    la2 = jnp.concatenate([la_hi, (la - la_hi.astype(F32)).astype(BF16)], axis=1)
    bs = []
    for c in range(n_chunks):
        t2 = _dot(tri, la2[c * L:(c + 1) * L, :])
        bs.append(t2[:, 0:W] + t2[:, W:2 * W])
    b = (jnp.concatenate(bs, axis=0) if n_chunks > 1 else bs[0]) * LOG2E
    b3 = b.reshape(n_chunks, L, W)

    def chunk_row(r):
        return jnp.broadcast_to(b3[:, r:r + 1, :], (n_chunks, L, W)).reshape(R, W)

    b_sub = jnp.broadcast_to(b.reshape(R // SUB, SUB, W)[:, 0:1, :], (R // SUB, SUB, W)).reshape(R, W)
    sub = (lax.broadcasted_iota(jnp.int32, (R, W), 0) % L) // SUB

    qt = q * jnp.exp2(b - b_sub)
    zb = jnp.zeros((), BF16)
    zq = jnp.zeros((SUB, P), F32)
    n_sub = L // SUB
    first = 0 if fast else 1
    ktm = []
    for i in range(first, n_sub):
        kt = (k * jnp.exp2(chunk_row(i * SUB) - b)).astype(BF16)
        keep = (sub[:, 0:P] <= i) if fast else (sub[:, 0:P] < i)
        ktm.append([[jnp.where(keep & (even if e == 0 else ~even), kt[:, pr * P:(pr + 1) * P], zb)
                     for e in range(2)] for pr in range(n_pairs)])

    if fast:
        causal = (lax.broadcasted_iota(jnp.int32, (L, P), 0) >= lax.broadcasted_iota(jnp.int32, (L, P), 1) % L)
    else:
        a = jnp.exp(la)
        kp_ref[SUB:SUB + R, :] = k
        ap_ref[SUB:SUB + R, :] = a
        p_ref[:, 0:W] = (q * k).astype(BF16)
        e = a
        for d in range(1, SUB):
            if d > 1:
                e = e * ap_ref[SUB - d + 1:SUB - d + 1 + R, :]
            p_ref[:, d * W:(d + 1) * W] = (q * kp_ref[SUB - d:SUB - d + R, :] * e).astype(BF16)
        cband = _dot(p_ref[...], spread_ref[...])
        same_sub = (lax.broadcasted_iota(jnp.int32, (L, W), 0) // SUB
                    == (lax.broadcasted_iota(jnp.int32, (L, W), 1) % L) // SUB)

    qe = (q * jnp.exp2(b)).astype(BF16)
    kd = (k * jnp.exp2(chunk_row(L - 1) - b)).astype(BF16)
    zv = jnp.zeros((L, 2 * GLA_DV), BF16)

    o_intra, d_st, dec = {}, {}, []
    for c in range(n_chunks):
        rs = slice(c * L, (c + 1) * L)
        if not fast:
            a_band = jnp.where(same_sub, pltpu.roll(cband[rs], W - (SUB - 1), 1, stride=1, stride_axis=0), 0.0)
        dec.append(jnp.exp2(b[c * L + L - 1:c * L + L, :]))
        for pr in range(n_pairs):
            ls = slice(pr * P, (pr + 1) * P)
            lhs_c = jnp.concatenate(
                [jnp.concatenate([qt[c * L + r * SUB:c * L + (r + 1) * SUB, ls] if i == r else zq
                                  for i in range(first, n_sub)], axis=1) for r in range(n_sub)],
                axis=0).astype(BF16)
            rhs_c = jnp.concatenate([jnp.concatenate([m[pr][0][rs], m[pr][1][rs]], axis=0) for m in ktm], axis=1)
            scores = _dot_nt(lhs_c, rhs_c)
            a_tot = (jnp.where(causal, scores, 0.0) if fast else a_band[:, ls] + scores).astype(BF16)
            vp = v_ref[rs, 2 * pr * GLA_DV:(2 * pr + 2) * GLA_DV]
            v_bd = jnp.concatenate([jnp.concatenate([vp[:, 0:GLA_DV], zv[:, 0:GLA_DV]], axis=1),
                                    jnp.concatenate([zv[:, 0:GLA_DV], vp[:, GLA_DV:]], axis=1)], axis=0)
            o_intra[c, pr] = _dot(a_tot, v_bd)
            d_st[c, pr] = jnp.where(bd_mask, _dot_tn(vp, kd[rs, ls]), 0.0)
    st_in = {}
    for gi in range(gpt):
        for pr in range(n_pairs):
            st = st_ref[gi, pr]
            for c in range(gi * cpg, (gi + 1) * cpg):
                st_in[c, pr] = st.astype(BF16)
                st = st * dec[c][:, pr * P:(pr + 1) * P] + d_st[c, pr]
            st_ref[gi, pr] = st
    for c in range(n_chunks):
        rs = slice(c * L, (c + 1) * L)
        for pr in range(n_pairs):
            on_ref[rs, 2 * pr * GLA_DV:(2 * pr + 2) * GLA_DV] = (
                o_intra[c, pr] + _dot_nt(qe[rs, pr * P:(pr + 1) * P], st_in[c, pr]))

    for hh in range(GLA_HEADS):
        hs = slice(hh * GLA_DV, (hh + 1) * GLA_DV)
        o = on_ref[:, hs]
        on = o * lax.rsqrt(jnp.mean(o * o, axis=-1, keepdims=True) + EPS) * g_ref[:, hs]
        o_ref[:, hs] = (on * r_ref[:, hs].astype(F32)).astype(BF16)

    @pl.when(t_idx == pl.num_programs(1) - 1)
    def _():
        for gi in range(gpt):
            for pr in range(n_pairs):
                st = st_ref[gi, pr]
                tt = jnp.where(even, st[0:GLA_DV], st[GLA_DV:2 * GLA_DV])
                sn_ref[gi, 2 * pr:2 * pr + 2] = tt.T.reshape(2, GLA_DK, GLA_DV)


def _band_spread():
    m = np.zeros((SUB, GLA_HEADS, GLA_DK, GLA_QK), np.float32)
    for d in range(SUB):
        for h in range(GLA_HEADS):
            m[d, h, :, h * GLA_DK + SUB - 1 - d] = 1.0
    return jnp.asarray(m.reshape(SUB * GLA_QK, GLA_QK), BF16)


def _gla(gq, gk, gv, la, gr, s0, g_out, *, groups, rows_per_group, tc, fast):
    gpt = max(1, tc // rows_per_group)
    nt = max(1, rows_per_group // tc)

    def rows(c):
        return pl.BlockSpec((tc, c), lambda g, t: (g * nt + t, 0))

    state = pl.BlockSpec((gpt, GLA_HEADS, GLA_DK, GLA_DV), lambda g, t: (g, 0, 0, 0))
    spread = _band_spread()
    return pl.pallas_call(
        functools.partial(_gla_kernel, gpt=gpt, fast=fast),
        grid=(groups // gpt, nt),
        in_specs=[rows(GLA_QK), rows(GLA_QK), rows(GLA_WIDTH), rows(GLA_QK), rows(GLA_WIDTH), state,
                  _const((1, GLA_WIDTH)), _const(spread.shape)],
        out_specs=[rows(GLA_WIDTH), state],
        out_shape=[jax.ShapeDtypeStruct((groups * rows_per_group, GLA_WIDTH), BF16),
                   jax.ShapeDtypeStruct((groups, GLA_HEADS, GLA_DK, GLA_DV), F32)],
        scratch_shapes=[pltpu.VMEM((gpt, GLA_HEADS // 2, 2 * GLA_DV, LANES), F32),
                        pltpu.VMEM((SUB + tc, GLA_QK), F32), pltpu.VMEM((SUB + tc, GLA_QK), F32),
                        pltpu.VMEM((tc, SUB * GLA_QK), BF16), pltpu.VMEM((tc, GLA_WIDTH), F32)],
        compiler_params=_params("arbitrary", "arbitrary"),
        name="gla",
    )(gq, gk, gv, la, gr, s0, g_out, spread)


def _mlp_tile(x_ref, a_ref, b_ref, g1_ref, sh2_ref, sc2_ref, g2_ref, gn_ref, wo_ref, wu_ref, wd_ref, y_ref, *, gpt):
    tm, d = x_ref.shape

    def per_group(val, ref, scale_plus_one=False):
        m = ref[...]
        if scale_plus_one:
            m = 1.0 + m
        return (val.reshape(gpt, tm // gpt, d) * m).reshape(tm, d)

    mix = jnp.concatenate([a_ref[...], b_ref[...]], axis=1)
    x1 = x_ref[...] + per_group(_dot(mix, wo_ref[...]), g1_ref)
    xn = x1 * lax.rsqrt(jnp.mean(x1 * x1, axis=-1, keepdims=True) + EPS) * gn_ref[...]
    h2 = (per_group(xn, sc2_ref, True).reshape(gpt, tm // gpt, d) + sh2_ref[...]).reshape(tm, d).astype(BF16)
    acc = jnp.zeros((tm, d), F32)
    for j in range(wu_ref.shape[1] // FF_SLICE):
        u = jnp.maximum(_dot(h2, wu_ref[:, j * FF_SLICE:(j + 1) * FF_SLICE]), 0.0)
        acc += _dot((u * u).astype(BF16), wd_ref[j * FF_SLICE:(j + 1) * FF_SLICE, :])
    y_ref[...] = x1 + per_group(acc, g2_ref)


def _mlp_kernel(*refs, tiles, gpts):
    per_phase = 7
    gn_ref, wo_ref, wu_ref, wd_ref = refs[len(tiles) * per_phase:len(tiles) * per_phase + 4]
    y_refs = refs[len(tiles) * per_phase + 4:]
    i = pl.program_id(0)
    first = 0
    for p, (n_tiles, gpt) in enumerate(zip(tiles, gpts)):
        ins = refs[p * per_phase:(p + 1) * per_phase]

        @pl.when(jnp.logical_and(i >= first, i < first + n_tiles))
        def _(ins=ins, p=p, gpt=gpt):
            _mlp_tile(*ins, gn_ref, wo_ref, wu_ref, wd_ref, y_refs[p], gpt=gpt)

        first += n_tiles


def _mlp(phases, w, *, tm):
    d = phases[0][0].shape[1]
    tiles = [ph[0].shape[0] // tm for ph in phases]
    gpts = [max(1, tm // ph[4]) for ph in phases]
    in_specs, args, first = [], [], 0
    for (x2, a_out, b_out, mod4, rpg), n_tiles, gpt in zip(phases, tiles, gpts):
        tpg = max(1, rpg // tm)

        def tile(i, first=first, n_tiles=n_tiles):
            return jnp.clip(i - first, 0, n_tiles - 1)

        def rows(c, tile=tile):
            return pl.BlockSpec((tm, c), lambda i: (tile(i), 0))

        def mod_spec(j, tile=tile, gpt=gpt, tpg=tpg):
            return pl.BlockSpec((gpt, None, 1, d), lambda i: (tile(i) // tpg, j, 0, 0))

        in_specs += [rows(d), rows(MLA_WIDTH), rows(GLA_WIDTH), mod_spec(2), mod_spec(3), mod_spec(4), mod_spec(5)]
        args += [x2, a_out, b_out, mod4, mod4, mod4, mod4]
        first += n_tiles
    out_specs, first = [], 0
    for n_tiles in tiles:
        out_specs.append(pl.BlockSpec((tm, d), lambda i, first=first, n_tiles=n_tiles:
                                      (jnp.clip(i - first, 0, n_tiles - 1), 0)))
        first += n_tiles
    return pl.pallas_call(
        functools.partial(_mlp_kernel, tiles=tuple(tiles), gpts=tuple(gpts)),
        grid=(sum(tiles),),
        in_specs=in_specs + [_const((1, d)), _const(w["w_out"].shape), _const(w["w_up"].shape),
                             _const(w["w_down"].shape)],
        out_specs=out_specs,
        out_shape=[jax.ShapeDtypeStruct(ph[0].shape, F32) for ph in phases],
        compiler_params=_params("arbitrary"),
        name="out_proj_mlp",
    )(*args, w["g_norm2"], w["w_out"], w["w_up"], w["w_down"])


def _rope_table(start, count, repeat=1):
    half = QK_ROPE // 2
    inv = ROPE_THETA ** (-np.arange(half, dtype=np.float64) / half)
    ang = (start + np.arange(count, dtype=np.float64))[:, None] * inv[None, :]
    c, s, z = np.cos(ang), np.sin(ang), np.zeros_like(ang)
    tab = np.concatenate([c, c, z, z, -s, z, z, z, z, s, z, z], axis=1).astype(np.float32)
    return jnp.asarray(np.tile(tab, (repeat, 1)))


def _pad_gain(g_rope):
    return jnp.concatenate([g_rope, jnp.zeros((LANES - QK_ROPE,), F32)]).reshape(1, LANES)


def _relayout_kernel(wint_ref, wuq_ref, wukv_ref, wg_ref, win_o, wuq_o, wukv_o, wukt_o, wuv_o, wg_o):
    s = np.cumsum([0, Q_LORA, KV_LORA, QK_ROPE, GLA_QK, GLA_QK, GLA_WIDTH, GLA_GATE_RANK, GLA_WIDTH])

    def piece(i):
        return wint_ref[int(s[i]):int(s[i + 1]), :]

    d = wint_ref.shape[1]
    zeros = jnp.zeros((LANES - QK_ROPE - GLA_GATE_RANK, d), F32)
    win_o[:, 0:Q_LORA] = piece(0).T.astype(BF16)
    win_o[:, Q_LORA:C_QKR[1]] = jnp.concatenate([piece(2), piece(6), zeros], axis=0).T.astype(BF16)
    win_o[:, C_KV[0]:C_KV[1]] = piece(1).T.astype(BF16)
    win_o[:, C_GQ[0]:C_GQ[1]] = (piece(3).T * (GLA_DK ** -0.5)).astype(BF16)
    win_o[:, C_GK[0]:C_GK[1]] = piece(4).T.astype(BF16)
    win_o[:, C_GV[0]:C_GV[1]] = piece(5).T.astype(BF16)
    win_o[:, C_GR[0]:C_GR[1]] = piece(7).T.astype(BF16)

    zq = jnp.zeros((Q_LORA, HEAD_PAD - QK_DIM), BF16)
    kvw = QK_NOPE + V_DIM
    for hh in range(MLA_HEADS):
        wuq_o[:, hh * HEAD_PAD:hh * HEAD_PAD + QK_DIM] = wuq_ref[:, hh * QK_DIM:(hh + 1) * QK_DIM].astype(BF16)
        wuq_o[:, hh * HEAD_PAD + QK_DIM:(hh + 1) * HEAD_PAD] = zq
        uk = wukv_ref[:, hh * kvw:hh * kvw + QK_NOPE]
        uv = wukv_ref[:, hh * kvw + QK_NOPE:(hh + 1) * kvw].astype(BF16)
        wukv_o[:, hh * QK_NOPE:(hh + 1) * QK_NOPE] = uk.astype(BF16)
        wukv_o[:, (MLA_HEADS + hh) * V_DIM:(MLA_HEADS + hh + 1) * V_DIM] = uv
        wukt_o[hh * QK_NOPE:(hh + 1) * QK_NOPE, :] = uk.T.astype(BF16)
        wuv_o[:, hh * V_DIM:(hh + 1) * V_DIM] = uv

    wg_o[...] = jnp.zeros(wg_o.shape, BF16)
    wg_o[QK_ROPE:QK_ROPE + GLA_GATE_RANK, :] = wg_ref[...].astype(BF16)


def _relayout(w_in_t, w_uq, w_ukv, w_gate_up):
    d = w_in_t.shape[1]
    shapes = [(d, C_GR[1]), (Q_LORA, MLA_HEADS * HEAD_PAD), (KV_LORA, MLA_HEADS * (QK_NOPE + V_DIM)),
              (MLA_HEADS * QK_NOPE, KV_LORA), (KV_LORA, MLA_WIDTH), (LANES, GLA_QK)]
    return pl.pallas_call(
        _relayout_kernel,
        out_shape=[jax.ShapeDtypeStruct(sh, BF16) for sh in shapes],
        compiler_params=pltpu.CompilerParams(vmem_limit_bytes=VMEM_LIMIT),
        name="weight_relayout",
    )(w_in_t, w_uq, w_ukv, w_gate_up)


def _prep_weights(w_in, g_norm1, g_q_lora, w_uq, g_kv_lora, w_ukv, g_q_head, g_k_head,
                  w_gate_up, b_gate_up, g_gla_out, w_out, g_norm2, w_up, w_down):
    d = w_in.shape[0]
    w_in_p, w_uq_p, w_ukv_p, w_ukt, w_uv, w_gate = _relayout(w_in.T, w_uq, w_ukv, w_gate_up)
    qscale = QK_DIM ** -0.5 * LOG2E
    bound = 1.02 * QK_DIM ** 0.5 * LOG2E * jnp.max(jnp.abs(g_q_head)) * jnp.max(jnp.abs(g_k_head))
    lane = jnp.arange(LANES) == QK_ROPE
    return {
        "w_in": w_in_p, "g_norm1": g_norm1.reshape(1, d), "g_q_lora": g_q_lora.reshape(1, Q_LORA),
        "w_uq": w_uq_p, "g_kv_lora": g_kv_lora.reshape(1, KV_LORA), "w_ukv": w_ukv_p, "w_ukt": w_ukt, "w_uv": w_uv,
        "gkr_col": g_k_head[QK_NOPE:].reshape(QK_ROPE, 1),
        "qone": lane.astype(F32).reshape(1, LANES), "kbias": jnp.where(lane, -bound, 0.0).reshape(1, LANES),
        "fast_softmax": (bound <= MAX_FIXED_SHIFT).astype(jnp.int32).reshape(1), "bound": bound.reshape(1),
        "gqn": (g_q_head[:QK_NOPE] * qscale).reshape(1, LANES), "gqr": _pad_gain(g_q_head[QK_NOPE:] * qscale),
        "gkn": g_k_head[:QK_NOPE].reshape(1, LANES), "gkr": _pad_gain(g_k_head[QK_NOPE:]),
        "w_gate": w_gate, "b_gate": b_gate_up.reshape(1, GLA_QK),
        "g_gla_out": g_gla_out.reshape(1, GLA_WIDTH), "w_out": w_out.astype(BF16),
        "g_norm2": g_norm2.reshape(1, d), "w_up": w_up.astype(BF16), "w_down": w_down.astype(BF16),
    }


def _mixers(x, mod4, past_lat, past_kr, s0, w, *, tm):
    batch, seq, d = x.shape
    n = batch * seq
    past = 0 if past_lat is None else past_lat.shape[1]
    x2 = x.reshape(n, d)
    tm = min(tm, n)
    tab = _rope_table(past, seq, repeat=max(1, tm // seq))
    if past == 0:
        lat, krt, q, k, v, gq, gk, gv, la, gr, dmin = _projection(x2, mod4, tab, w, rows_per_group=seq, tm=tm,
                                                                  prompt=True)
        kr = jnp.swapaxes(krt, 1, 2)
        a_out = _attention_prompt(w["fast_softmax"], q, k, v, batch=batch, seq=seq, tq=min(ATTN_TILE, seq))
    else:
        assert seq == CHUNK and past % CHUNK == 0
        lat, kr, krt, q, gq, gk, gv, la, gr, dmin = _projection(x2, mod4, tab, w, rows_per_group=seq, tm=tm,
                                                                prompt=False)
        attend = functools.partial(_attention_sample, q, past_lat, jnp.swapaxes(past_kr, 1, 2), lat, krt, w,
                                   tkb=min(CACHE_BLOCK, past))
        a_out = lax.cond(w["fast_softmax"][0] == 1, functools.partial(attend, fast=True),
                         functools.partial(attend, fast=False))
    gap = -(SUB - 1) * LOG2E * jnp.min(dmin)
    gla = functools.partial(_gla, gq, gk, gv, la, gr, s0, w["g_gla_out"], groups=batch, rows_per_group=seq, tc=tm)
    b_out, s_new = lax.cond(gap <= MAX_SUB_DECAY, functools.partial(gla, fast=True), functools.partial(gla, fast=False))
    return (x2, a_out, b_out, mod4, seq), (lat.reshape(batch, seq, KV_LORA), kr.reshape(batch, seq, QK_ROPE), s_new)


def kernel(x_prompt, x_sample, cache_mla_latent, cache_mla_krope, state_gla, c_prompt, c_sample,
           w_ada, b_ada, g_norm1, w_in, g_q_lora, w_uq, g_kv_lora, w_ukv, g_q_head, g_k_head,
           w_gate_up, b_gate_up, g_gla_out, w_out, g_norm2, w_up, w_down):
    nb = x_prompt.shape[0]
    depth = w_ada.shape[0]
    y_p, y_s = x_prompt, x_sample
    outs = [[] for _ in range(6)]
    for l in range(depth):
        w = _prep_weights(w_in[l], g_norm1[l], g_q_lora[l], w_uq[l], g_kv_lora[l], w_ukv[l], g_q_head[l],
                          g_k_head[l], w_gate_up[l], b_gate_up[l], g_gla_out[l], w_out[l], g_norm2[l],
                          w_up[l], w_down[l])
        mod_p, mod_s = _modulation(c_prompt, c_sample, w_ada[l], b_ada[l])
        zero_state = jnp.zeros((nb, GLA_HEADS, GLA_DK, GLA_DV), x_prompt.dtype)
        mix_p, new_p = _mixers(y_p, mod_p, None, None, zero_state, w, tm=ROW_TILE)
        mix_s, new_s = _mixers(y_s, mod_s, cache_mla_latent[l], cache_mla_krope[l], state_gla[l], w, tm=ROW_TILE)
        y2_p, y2_s = _mlp([mix_p, mix_s], w, tm=ROW_TILE)
        y_p, y_s = y2_p.reshape(y_p.shape), y2_s.reshape(y_s.shape)
        for o, new in zip(outs, new_p + new_s):
            o.append(new)
    return (y_p, y_s) + tuple(jnp.stack(o) for o in outs)
```

```python
import functools

import jax
import jax.numpy as jnp
import numpy as np
from jax import lax
from jax.experimental import pallas as pl
from jax.experimental.pallas import tpu as pltpu

F32 = jnp.float32
BF16 = jnp.bfloat16

CHUNK = 64
EPS = 1e-6
MLA_HEADS = 4
Q_LORA = 384
KV_LORA = 256
QK_NOPE = 128
QK_ROPE = 64
QK_DIM = QK_NOPE + QK_ROPE
V_DIM = 128
ROPE_THETA = 10000.0
GLA_HEADS = 4
GLA_DK = 64
GLA_DV = 128
GLA_GATE_RANK = 16
GLA_TAU = 16.0
GLA_QK = GLA_HEADS * GLA_DK
GLA_WIDTH = GLA_HEADS * GLA_DV
MLA_WIDTH = MLA_HEADS * V_DIM
HEAD_PAD = 256
SUB = 8
LOG2E = 1.4426950408889634
MAX_FIXED_SHIFT = 48.0
MAX_SUB_DECAY = 60.0

LANES = 128
VMEM_LIMIT = 56 * 1024 * 1024
ROW_TILE = 512
ATTN_TILE = 512
CACHE_BLOCK = 1024
FF_SLICE = 1024

C_QKR = (0, 512)
C_KV = (512, 768)
C_GQ = (768, 1024)
C_GK = (1024, 1280)
C_GV = (1280, 1792)
C_GR = (1792, 2304)


def _dot(a, b):
    return jnp.dot(a, b, preferred_element_type=F32)


def _dot_nt(a, b):
    return lax.dot_general(a, b, (((1,), (1,)), ((), ())), preferred_element_type=F32)


def _dot_tn(a, b):
    return lax.dot_general(a, b, (((0,), (0,)), ((), ())), preferred_element_type=F32)


def _rope_tile(t, c, sa, sb):
    return t * c + pltpu.roll(t, 96, 1) * sa + pltpu.roll(t, 32, 1) * sb


def _params(*sem):
    return pltpu.CompilerParams(dimension_semantics=sem, vmem_limit_bytes=VMEM_LIMIT)


def _const(shape):
    return pl.BlockSpec(shape, lambda *_: (0,) * len(shape), pipeline_mode=pl.Buffered(1))


def _mod_kernel(cs_ref, cp_ref, w_ref, b_ref, os_ref, op_ref):
    ns = cs_ref.shape[0]
    c = jnp.concatenate([cs_ref[...], cp_ref[...]], axis=0)
    s = (c * jax.nn.sigmoid(c)).astype(BF16)
    mod = _dot(s, w_ref[...].astype(BF16)) + b_ref[...]
    for r in range(ns):
        os_ref[r] = mod[r:r + 1]
    for r in range(op_ref.shape[0]):
        op_ref[r] = mod[ns + r:ns + r + 1]


def _modulation(c_prompt, c_sample, w_ada, b_ada):
    d, n = w_ada.shape
    vectors = n // d

    def whole(c):
        return pl.BlockSpec(c.shape, lambda j: (0, 0))

    def vector(c):
        return pl.BlockSpec((c.shape[0], None, 1, d), lambda j: (0, j, 0, 0))

    def shape(c):
        return jax.ShapeDtypeStruct((c.shape[0], vectors, 1, d), F32)

    mod_s, mod_p = pl.pallas_call(
        _mod_kernel,
        grid=(vectors,),
        in_specs=[whole(c_sample), whole(c_prompt),
                  pl.BlockSpec((d, d), lambda j: (0, j)),
                  pl.BlockSpec((1, d), lambda j: (0, j))],
        out_specs=[vector(c_sample), vector(c_prompt)],
        out_shape=[shape(c_sample), shape(c_prompt)],
        compiler_params=_params("arbitrary"),
        name="adaln_mod",
    )(c_sample, c_prompt, w_ada, b_ada.reshape(1, n))
    return mod_p, mod_s


def _proj_kernel(x_ref, sh_ref, sc_ref, g1_ref, win_ref, gql_ref, wuq_ref, gkv_ref, wukv_ref,
                 gqn_ref, gqr_ref, gkn_ref, gkr_ref, qone_ref, kbias_ref, tab_ref, wg_ref, bg_ref,
                 *outs, gpt, prompt):
    if prompt:
        lat_ref, krt_ref, q_ref, k_ref, v_ref, gq_ref, gk_ref, gv_ref, la_ref, gr_ref, dmin_ref = outs
    else:
        lat_ref, kr_ref, krt_ref, q_ref, gq_ref, gk_ref, gv_ref, la_ref, gr_ref, dmin_ref = outs
    tm, d = x_ref.shape
    x = x_ref[...]
    xn = x * lax.rsqrt(jnp.mean(x * x, axis=-1, keepdims=True) + EPS)
    h = (xn.reshape(gpt, tm // gpt, d) * (g1_ref[...] * (1.0 + sc_ref[...])) + sh_ref[...]).reshape(tm, d)
    hb = h.astype(BF16)

    def col(c):
        return _dot(hb, win_ref[:, c[0]:c[1]])

    tab = tab_ref[...]
    cos, sa, sb = tab[:, 0:LANES], tab[:, LANES:2 * LANES], tab[:, 2 * LANES:3 * LANES]

    qkr = col(C_QKR)
    krg = qkr[:, Q_LORA:]
    z = _dot(krg.astype(BF16), wg_ref[...]) + bg_ref[...]
    la = (jnp.minimum(z, 0.0) - jnp.log(1.0 + jnp.exp(-jnp.abs(z)))) * (1.0 / GLA_TAU)
    la_ref[...] = la
    dmin_ref[...] = jnp.min(la, axis=0, keepdims=True)
    r = col(C_GR)
    gr_ref[...] = (r * jax.nn.sigmoid(r)).astype(BF16)

    cq = qkr[:, 0:Q_LORA]
    cqn = cq * lax.rsqrt(jnp.mean(cq * cq, axis=-1, keepdims=True) + EPS) * gql_ref[...]
    qp = _dot(cqn.astype(BF16), wuq_ref[...])
    for hh in range(MLA_HEADS):
        nope = qp[:, hh * HEAD_PAD:hh * HEAD_PAD + QK_NOPE]
        rt = qp[:, hh * HEAD_PAD + QK_NOPE:(hh + 1) * HEAD_PAD]
        ss = jnp.sum(nope * nope, axis=-1, keepdims=True) + jnp.sum(rt * rt, axis=-1, keepdims=True)
        inv = lax.rsqrt(ss * (1.0 / QK_DIM) + EPS)
        q_ref[:, hh * HEAD_PAD:hh * HEAD_PAD + QK_NOPE] = (nope * inv * gqn_ref[...]).astype(BF16)
        rq = _rope_tile(rt * inv * gqr_ref[...], cos, sa, sb)
        q_ref[:, hh * HEAD_PAD + QK_NOPE:(hh + 1) * HEAD_PAD] = (rq + qone_ref[...]).astype(BF16)

    ckv = col(C_KV)
    lat = ckv * lax.rsqrt(jnp.mean(ckv * ckv, axis=-1, keepdims=True) + EPS) * gkv_ref[...]
    lat_ref[...] = lat
    krt_ref[...] = krg.T[0:QK_ROPE, :]
    if prompt:
        lane = lax.broadcasted_iota(jnp.int32, (1, LANES), 1)
        krm = jnp.where(lane < QK_ROPE, krg, 0.0)
        ssr = jnp.sum(krm * krm, axis=-1, keepdims=True)
        rk = _rope_tile(krm * gkr_ref[...], cos, sa, sb)
        kv = _dot(lat.astype(BF16), wukv_ref[...])
        for hh in range(MLA_HEADS):
            kn = kv[:, hh * QK_NOPE:(hh + 1) * QK_NOPE]
            inv = lax.rsqrt((jnp.sum(kn * kn, axis=-1, keepdims=True) + ssr) * (1.0 / QK_DIM) + EPS)
            k_ref[:, hh * HEAD_PAD:hh * HEAD_PAD + QK_NOPE] = (kn * inv * gkn_ref[...]).astype(BF16)
            k_ref[:, hh * HEAD_PAD + QK_NOPE:(hh + 1) * HEAD_PAD] = (rk * inv + kbias_ref[...]).astype(BF16)
        v_ref[...] = kv[:, MLA_HEADS * QK_NOPE:].astype(BF16)
    else:
        kr_ref[...] = krg[:, 0:QK_ROPE]

    gq_ref[...] = col(C_GQ)
    gk_ref[...] = col(C_GK)
    gv_ref[...] = col(C_GV).astype(BF16)


def _projection(x2, mod4, tab, w, *, rows_per_group, tm, prompt):
    n, d = x2.shape
    gpt = max(1, tm // rows_per_group)
    tpg = max(1, rows_per_group // tm)
    ntab = tab.shape[0] // tm

    def mod_spec(j):
        return pl.BlockSpec((gpt, None, 1, d), lambda i: ((i // tpg) if gpt == 1 else i, j, 0, 0))

    def rows(c):
        return pl.BlockSpec((tm, c), lambda i: (i, 0))

    def out(c, t):
        return rows(c), jax.ShapeDtypeStruct((n, c), t)

    dmin = (pl.BlockSpec((None, 1, GLA_QK), lambda i: (i, 0, 0)), jax.ShapeDtypeStruct((n // tm, 1, GLA_QK), F32))
    gla_outs = [out(GLA_QK, F32), out(GLA_QK, F32), out(GLA_WIDTH, BF16), out(GLA_QK, F32), out(GLA_WIDTH, BF16),
                dmin]
    qo = out(MLA_HEADS * HEAD_PAD, BF16)
    if prompt:
        krt = (pl.BlockSpec((None, QK_ROPE, tm), lambda i: (i // tpg, 0, i % tpg)),
               jax.ShapeDtypeStruct((n // rows_per_group, QK_ROPE, rows_per_group), F32))
        outs = [out(KV_LORA, F32), krt, qo, out(MLA_HEADS * HEAD_PAD, BF16), out(MLA_WIDTH, BF16)] + gla_outs
    else:
        krt = (pl.BlockSpec((QK_ROPE, tm), lambda i: (0, i)), jax.ShapeDtypeStruct((QK_ROPE, n), F32))
        outs = [out(KV_LORA, F32), out(QK_ROPE, F32), krt, qo] + gla_outs
    return pl.pallas_call(
        functools.partial(_proj_kernel, gpt=gpt, prompt=prompt),
        grid=(n // tm,),
        in_specs=[rows(d), mod_spec(0), mod_spec(1), _const((1, d)), _const(w["w_in"].shape),
                  _const((1, Q_LORA)), _const(w["w_uq"].shape), _const((1, KV_LORA)), _const(w["w_ukv"].shape),
                  _const((1, LANES)), _const((1, LANES)), _const((1, LANES)), _const((1, LANES)),
                  _const((1, LANES)), _const((1, LANES)),
                  pl.BlockSpec((tm, 3 * LANES), lambda i: (i % ntab, 0)),
                  _const(w["w_gate"].shape), _const((1, GLA_QK))],
        out_specs=[o[0] for o in outs],
        out_shape=[o[1] for o in outs],
        compiler_params=_params("arbitrary"),
        name="in_proj",
    )(x2, mod4, mod4, w["g_norm1"], w["w_in"], w["g_q_lora"], w["w_uq"], w["g_kv_lora"], w["w_ukv"],
      w["gqn"], w["gqr"], w["gkn"], w["gkr"], w["qone"], w["kbias"], tab, w["w_gate"], w["b_gate"])


def _attn_prompt_kernel(fast_ref, q_ref, k_ref, v_ref, o_ref, vx_ref, m_ref, acc_ref, *, tq):
    i = pl.program_id(1)
    heads = range(MLA_HEADS)

    @pl.when(i == 0)
    def _():
        for hh in heads:
            vx_ref[hh, :, 0:V_DIM] = v_ref[:, hh * V_DIM:(hh + 1) * V_DIM]
            vx_ref[hh, :, V_DIM:] = jnp.ones((v_ref.shape[0], V_DIM), BF16)

    def scores(j, hh, masked):
        hs = slice(hh * HEAD_PAD, (hh + 1) * HEAD_PAD)
        s = _dot_nt(q_ref[:, hs], k_ref[pl.ds(pl.multiple_of(j * tq, tq), tq), hs])
        if masked:
            qc = lax.broadcasted_iota(jnp.int32, (tq, tq), 0) // CHUNK
            kc = lax.broadcasted_iota(jnp.int32, (tq, tq), 1) // CHUNK
            s = jnp.where(kc <= qc, s, -jnp.inf)
        return s

    def fast_diagonal():
        hq = tq // 2
        k0 = pl.multiple_of(i * tq, tq)
        qc = lax.broadcasted_iota(jnp.int32, (hq, hq), 0) // CHUNK
        kc = lax.broadcasted_iota(jnp.int32, (hq, hq), 1) // CHUNK
        tri = kc <= qc
        for hh in heads:
            hs = slice(hh * HEAD_PAD, (hh + 1) * HEAD_PAD)
            top = jnp.where(tri, _dot_nt(q_ref[0:hq, hs], k_ref[pl.ds(k0, hq), hs]), -jnp.inf)
            acc_ref[hh, 0:hq] = _dot(jnp.exp2(top).astype(BF16), vx_ref[hh, pl.ds(k0, hq), :])
            bot = _dot_nt(q_ref[hq:tq, hs], k_ref[pl.ds(k0, tq), hs])
            bot = jnp.concatenate([bot[:, 0:hq], jnp.where(tri, bot[:, hq:tq], -jnp.inf)], axis=1)
            acc_ref[hh, hq:tq] = _dot(jnp.exp2(bot).astype(BF16), vx_ref[hh, pl.ds(k0, tq), :])

    def vblock(j, hh):
        return vx_ref[hh, pl.ds(pl.multiple_of(j * tq, tq), tq), :]

    def fast_block(j, masked):
        for hh in heads:
            acc_ref[hh] += _dot(jnp.exp2(scores(j, hh, masked)).astype(BF16), vblock(j, hh))

    def safe_block(j, masked):
        for hh in heads:
            s = scores(j, hh, masked)
            m = m_ref[hh]
            m_new = jnp.maximum(m, jnp.max(s, axis=1, keepdims=True))
            p = jnp.exp2(s - m_new).astype(BF16)
            acc_ref[hh] = jnp.exp2(m - m_new) * acc_ref[hh] + _dot(p, vblock(j, hh))
            m_ref[hh] = m_new

    def sweep(block):
        def body(jj, c):
            block(2 * jj, False)
            block(2 * jj + 1, False)
            return c

        lax.fori_loop(0, i // 2, body, 0)

        @pl.when(i % 2 == 1)
        def _():
            block(i - 1, False)

    @pl.when(fast_ref[0] == 1)
    def _():
        fast_diagonal()
        sweep(fast_block)

    @pl.when(fast_ref[0] != 1)
    def _():
        m_ref[...] = jnp.full(m_ref.shape, -jnp.inf, F32)
        acc_ref[...] = jnp.zeros(acc_ref.shape, F32)
        safe_block(i, True)
        sweep(safe_block)

    for hh in heads:
        acc = acc_ref[hh]
        o_ref[:, hh * V_DIM:(hh + 1) * V_DIM] = (acc[:, 0:V_DIM] / acc[:, V_DIM:]).astype(BF16)


def _attention_prompt(fast, q, k, v, *, batch, seq, tq):
    nq = seq // tq
    return pl.pallas_call(
        functools.partial(_attn_prompt_kernel, tq=tq),
        grid=(batch, nq),
        in_specs=[pl.BlockSpec(memory_space=pltpu.SMEM),
                  pl.BlockSpec((tq, MLA_HEADS * HEAD_PAD), lambda b, i: (b * nq + i, 0)),
                  pl.BlockSpec((seq, MLA_HEADS * HEAD_PAD), lambda b, i: (b, 0)),
                  pl.BlockSpec((seq, MLA_WIDTH), lambda b, i: (b, 0))],
        out_specs=pl.BlockSpec((tq, MLA_WIDTH), lambda b, i: (b * nq + i, 0)),
        out_shape=jax.ShapeDtypeStruct((batch * seq, MLA_WIDTH), BF16),
        scratch_shapes=[pltpu.VMEM((MLA_HEADS, seq, 2 * V_DIM), BF16), pltpu.VMEM((MLA_HEADS, tq, 1), F32),
                        pltpu.VMEM((MLA_HEADS, tq, 2 * V_DIM), F32)],
        compiler_params=_params("arbitrary", "arbitrary"),
        name="mla_attn_prompt",
    )(fast, q, k, v)


def _attn_sample_kernel(bound_ref, q_ref, lat_ref, krt_ref, latn_ref, krtn_ref, tab_ref, wukt_ref, wuv_ref,
                        gkn_ref, gkr_ref, o_ref, wq_ref, latb_ref, s_ref, *, tkb, fast):
    t = q_ref.shape[0] // 2
    past = lat_ref.shape[1]
    nk = MLA_HEADS * QK_NOPE
    half = QK_ROPE // 2
    streams = range(2)

    @pl.when(pl.program_id(0) == 0)
    def _():
        for e in streams:
            wq_ref[e, 0:nk, :] = wukt_ref[...]

    qr = []
    for e in streams:
        rows = slice(e * t, (e + 1) * t)
        parts = []
        for hh in range(MLA_HEADS):
            qn = (q_ref[rows, hh * HEAD_PAD:hh * HEAD_PAD + QK_NOPE].astype(F32) * gkn_ref[...]).astype(BF16)
            wq_ref[e, nk + hh * t:nk + (hh + 1) * t, :] = _dot(
                qn, wukt_ref[hh * QK_NOPE:(hh + 1) * QK_NOPE, :]).astype(BF16)
            parts.append(q_ref[rows, hh * HEAD_PAD + QK_NOPE:(hh + 1) * HEAD_PAD])
        qr.append(jnp.concatenate(parts, axis=0))

    def key_block(e, lat, krt, tab, width, valid):
        latb = lat.astype(BF16)
        g = _dot_nt(wq_ref[e], latb)
        ssr = jnp.sum(krt * krt, axis=0, keepdims=True)
        kg = krt * gkr_ref[...]
        x1, x2 = kg[0:half], kg[half:QK_ROPE]
        c, sn = tab[0:half], tab[half:QK_ROPE]
        rope = jnp.concatenate([x1 * c - x2 * sn, x2 * c + x1 * sn, jnp.zeros((LANES - QK_ROPE, width), F32)], axis=0)
        srope = _dot(qr[e], rope.astype(BF16))
        rows = []
        for hh in range(MLA_HEADS):
            kn = g[hh * QK_NOPE:(hh + 1) * QK_NOPE]
            inv = lax.rsqrt((jnp.sum(kn * kn, axis=0, keepdims=True) + ssr) * (1.0 / QK_DIM) + EPS)
            rows.append((g[nk + hh * t:nk + (hh + 1) * t] + srope[hh * t:(hh + 1) * t]) * inv)
        sc = jnp.concatenate(rows, axis=0) - bound_ref[0]
        if valid is not None:
            sc = jnp.where(valid, sc, -jnp.inf)
        return sc, latb

    def blocks():
        for blk in range(past // tkb):
            c0 = blk * tkb
            for e in streams:
                yield e, c0, tkb, key_block(e, lat_ref[e, c0:c0 + tkb, :], krt_ref[e, :, c0:c0 + tkb],
                                            tab_ref[:, c0:c0 + tkb], tkb, None)
        for e in streams:
            mine = lax.broadcasted_iota(jnp.int32, (1, 2 * t), 1) // t == e
            yield e, past, 2 * t, key_block(e, latn_ref[...], krtn_ref[...], tab_ref[:, past:past + 2 * t], 2 * t, mine)

    def finish(e, wlat, l):
        wlat = wlat.astype(BF16)
        for hh in range(MLA_HEADS):
            o = _dot(wlat[hh * t:(hh + 1) * t], wuv_ref[:, hh * V_DIM:(hh + 1) * V_DIM])
            o_ref[e * t:(e + 1) * t, hh * V_DIM:(hh + 1) * V_DIM] = (o / l[hh * t:(hh + 1) * t]).astype(BF16)

    if fast:
        wlat = [jnp.zeros((MLA_HEADS * t, KV_LORA), F32) for _ in streams]
        lsum = [jnp.zeros((MLA_HEADS * t, LANES), F32) for _ in streams]
        for e, _, width, (sc, latb) in blocks():
            p = jnp.exp2(sc)
            wlat[e] += _dot(p.astype(BF16), latb)
            for c in range(width // LANES):
                lsum[e] += p[:, c * LANES:(c + 1) * LANES]
        for e in streams:
            finish(e, wlat[e], jnp.sum(lsum[e], axis=1, keepdims=True))

    else:
        for e, c0, width, (sc, latb) in blocks():
            s_ref[e, :, c0:c0 + width] = sc
            latb_ref[e, c0:c0 + width, :] = latb
        for e in streams:
            s = s_ref[e]
            p = jnp.exp2(s - jnp.max(s, axis=1, keepdims=True))
            finish(e, _dot(p.astype(BF16), latb_ref[e]), jnp.sum(p, axis=1, keepdims=True))


def _attention_sample(q, past_lat, past_krt, lat_new, krt_new, w, *, tkb, fast):
    batch, past, _ = past_lat.shape
    t = q.shape[0] // batch
    half = QK_ROPE // 2
    inv = ROPE_THETA ** (-np.arange(half, dtype=np.float64) / half)
    pos = np.concatenate([np.arange(past), past + np.arange(t), past + np.arange(t)]).astype(np.float64)
    ang = inv[:, None] * pos[None, :]
    tab = jnp.asarray(np.concatenate([np.cos(ang), np.sin(ang)], axis=0).astype(np.float32))
    s_pad = past + 2 * t
    return pl.pallas_call(
        functools.partial(_attn_sample_kernel, tkb=tkb, fast=fast),
        grid=(batch // 2,),
        in_specs=[pl.BlockSpec(memory_space=pltpu.SMEM),
                  pl.BlockSpec((2 * t, MLA_HEADS * HEAD_PAD), lambda g: (g, 0)),
                  pl.BlockSpec((2, past, KV_LORA), lambda g: (g, 0, 0)),
                  pl.BlockSpec((2, QK_ROPE, past), lambda g: (g, 0, 0)),
                  pl.BlockSpec((2 * t, KV_LORA), lambda g: (g, 0)),
                  pl.BlockSpec((QK_ROPE, 2 * t), lambda g: (0, g)),
                  _const(tab.shape), _const(w["w_ukt"].shape), _const(w["w_uv"].shape),
                  _const((1, LANES)), _const((QK_ROPE, 1))],
        out_specs=pl.BlockSpec((2 * t, MLA_WIDTH), lambda g: (g, 0)),
        out_shape=jax.ShapeDtypeStruct((batch * t, MLA_WIDTH), BF16),
        scratch_shapes=[pltpu.VMEM((2, MLA_HEADS * (QK_NOPE + t), KV_LORA), BF16),
                        pltpu.VMEM((2, s_pad, KV_LORA), BF16),
                        pltpu.VMEM((2, MLA_HEADS * t, s_pad), F32)],
        compiler_params=_params("arbitrary"),
        name="mla_attn_sample",
    )(w["bound"], q, past_lat, past_krt, lat_new, krt_new, tab, w["w_ukt"], w["w_uv"], w["gkn"], w["gkr_col"])


def _gla_kernel(q_ref, k_ref, v_ref, la_ref, r_ref, s0_ref, g_ref, spread_ref, o_ref, sn_ref,
                st_ref, kp_ref, ap_ref, p_ref, on_ref, *, gpt, fast):
    t_idx = pl.program_id(1)
    L, W, P = CHUNK, GLA_QK, LANES
    R = q_ref.shape[0]
    n_chunks = R // L
    cpg = n_chunks // gpt
    n_pairs = GLA_HEADS // 2

    lane_p = lax.broadcasted_iota(jnp.int32, (1, P), 1)
    even = lane_p < GLA_DK
    bd_mask = (lax.broadcasted_iota(jnp.int32, (2 * GLA_DV, P), 0) // GLA_DV
               == lax.broadcasted_iota(jnp.int32, (2 * GLA_DV, P), 1) // GLA_DK)

    @pl.when(t_idx == 0)
    def _():
        kp_ref[0:SUB, :] = jnp.zeros((SUB, W), F32)
        ap_ref[0:SUB, :] = jnp.zeros((SUB, W), F32)
        for gi in range(gpt):
            for pr in range(n_pairs):
                tt = s0_ref[gi, 2 * pr:2 * pr + 2].reshape(2 * GLA_DK, GLA_DV).T
                st_ref[gi, pr] = jnp.where(bd_mask, jnp.concatenate([tt, tt], axis=0), 0.0)

    q = q_ref[...]
    k = k_ref[...]
    la = la_ref[...]

    tri = (lax.broadcasted_iota(jnp.int32, (L, L), 0) >= lax.broadcasted_iota(jnp.int32, (L, L), 1)).astype(BF16)
    la_hi = la.astype(BF16)
    la2 = jnp.concatenate([la_hi, (la - la_hi.astype(F32)).astype(BF16)], axis=1)
    bs = []
    for c in range(n_chunks):
        t2 = _dot(tri, la2[c * L:(c + 1) * L, :])
        bs.append(t2[:, 0:W] + t2[:, W:2 * W])
    b = (jnp.concatenate(bs, axis=0) if n_chunks > 1 else bs[0]) * LOG2E
    b3 = b.reshape(n_chunks, L, W)

    def chunk_row(r):
        return jnp.broadcast_to(b3[:, r:r + 1, :], (n_chunks, L, W)).reshape(R, W)

    b_sub = jnp.broadcast_to(b.reshape(R // SUB, SUB, W)[:, 0:1, :], (R // SUB, SUB, W)).reshape(R, W)
    sub = (lax.broadcasted_iota(jnp.int32, (R, W), 0) % L) // SUB

    qt = q * jnp.exp2(b - b_sub)
    zb = jnp.zeros((), BF16)
    zq = jnp.zeros((SUB, P), F32)
    n_sub = L // SUB
    first = 0 if fast else 1
    ktm = []
    for i in range(first, n_sub):
        kt = (k * jnp.exp2(chunk_row(i * SUB) - b)).astype(BF16)
        keep = (sub[:, 0:P] <= i) if fast else (sub[:, 0:P] < i)
        ktm.append([[jnp.where(keep & (even if e == 0 else ~even), kt[:, pr * P:(pr + 1) * P], zb)
                     for e in range(2)] for pr in range(n_pairs)])

    if fast:
        causal = (lax.broadcasted_iota(jnp.int32, (L, P), 0) >= lax.broadcasted_iota(jnp.int32, (L, P), 1) % L)
    else:
        a = jnp.exp(la)
        kp_ref[SUB:SUB + R, :] = k
        ap_ref[SUB:SUB + R, :] = a
        p_ref[:, 0:W] = (q * k).astype(BF16)
        e = a
        for d in range(1, SUB):
            if d > 1:
                e = e * ap_ref[SUB - d + 1:SUB - d + 1 + R, :]
            p_ref[:, d * W:(d + 1) * W] = (q * kp_ref[SUB - d:SUB - d + R, :] * e).astype(BF16)
        cband = _dot(p_ref[...], spread_ref[...])
        same_sub = (lax.broadcasted_iota(jnp.int32, (L, W), 0) // SUB
                    == (lax.broadcasted_iota(jnp.int32, (L, W), 1) % L) // SUB)

    qe = (q * jnp.exp2(b)).astype(BF16)
    kd = (k * jnp.exp2(chunk_row(L - 1) - b)).astype(BF16)
    zv = jnp.zeros((L, 2 * GLA_DV), BF16)

    o_intra, d_st, dec = {}, {}, []
    for c in range(n_chunks):
        rs = slice(c * L, (c + 1) * L)
        if not fast:
            a_band = jnp.where(same_sub, pltpu.roll(cband[rs], W - (SUB - 1), 1, stride=1, stride_axis=0), 0.0)
        dec.append(jnp.exp2(b[c * L + L - 1:c * L + L, :]))
        for pr in range(n_pairs):
            ls = slice(pr * P, (pr + 1) * P)
            lhs_c = jnp.concatenate(
                [jnp.concatenate([qt[c * L + r * SUB:c * L + (r + 1) * SUB, ls] if i == r else zq
                                  for i in range(first, n_sub)], axis=1) for r in range(n_sub)],
                axis=0).astype(BF16)
            rhs_c = jnp.concatenate([jnp.concatenate([m[pr][0][rs], m[pr][1][rs]], axis=0) for m in ktm], axis=1)
            scores = _dot_nt(lhs_c, rhs_c)
            a_tot = (jnp.where(causal, scores, 0.0) if fast else a_band[:, ls] + scores).astype(BF16)
            vp = v_ref[rs, 2 * pr * GLA_DV:(2 * pr + 2) * GLA_DV]
            v_bd = jnp.concatenate([jnp.concatenate([vp[:, 0:GLA_DV], zv[:, 0:GLA_DV]], axis=1),
                                    jnp.concatenate([zv[:, 0:GLA_DV], vp[:, GLA_DV:]], axis=1)], axis=0)
            o_intra[c, pr] = _dot(a_tot, v_bd)
            d_st[c, pr] = jnp.where(bd_mask, _dot_tn(vp, kd[rs, ls]), 0.0)
    st_in = {}
    for gi in range(gpt):
        for pr in range(n_pairs):
            st = st_ref[gi, pr]
            for c in range(gi * cpg, (gi + 1) * cpg):
                st_in[c, pr] = st.astype(BF16)
                st = st * dec[c][:, pr * P:(pr + 1) * P] + d_st[c, pr]
            st_ref[gi, pr] = st
    for c in range(n_chunks):
        rs = slice(c * L, (c + 1) * L)
        for pr in range(n_pairs):
            on_ref[rs, 2 * pr * GLA_DV:(2 * pr + 2) * GLA_DV] = (
                o_intra[c, pr] + _dot_nt(qe[rs, pr * P:(pr + 1) * P], st_in[c, pr]))

    for hh in range(GLA_HEADS):
        hs = slice(hh * GLA_DV, (hh + 1) * GLA_DV)
        o = on_ref[:, hs]
        on = o * lax.rsqrt(jnp.mean(o * o, axis=-1, keepdims=True) + EPS) * g_ref[:, hs]
        o_ref[:, hs] = (on * r_ref[:, hs].astype(F32)).astype(BF16)

    @pl.when(t_idx == pl.num_programs(1) - 1)
    def _():
        for gi in range(gpt):
            for pr in range(n_pairs):
                st = st_ref[gi, pr]
                tt = jnp.where(even, st[0:GLA_DV], st[GLA_DV:2 * GLA_DV])
                sn_ref[gi, 2 * pr:2 * pr + 2] = tt.T.reshape(2, GLA_DK, GLA_DV)


def _band_spread():
    m = np.zeros((SUB, GLA_HEADS, GLA_DK, GLA_QK), np.float32)
    for d in range(SUB):
        for h in range(GLA_HEADS):
            m[d, h, :, h * GLA_DK + SUB - 1 - d] = 1.0
    return jnp.asarray(m.reshape(SUB * GLA_QK, GLA_QK), BF16)


def _gla(gq, gk, gv, la, gr, s0, g_out, *, groups, rows_per_group, tc, fast):
    gpt = max(1, tc // rows_per_group)
    nt = max(1, rows_per_group // tc)

    def rows(c):
        return pl.BlockSpec((tc, c), lambda g, t: (g * nt + t, 0))

    state = pl.BlockSpec((gpt, GLA_HEADS, GLA_DK, GLA_DV), lambda g, t: (g, 0, 0, 0))
    spread = _band_spread()
    return pl.pallas_call(
        functools.partial(_gla_kernel, gpt=gpt, fast=fast),
        grid=(groups // gpt, nt),
        in_specs=[rows(GLA_QK), rows(GLA_QK), rows(GLA_WIDTH), rows(GLA_QK), rows(GLA_WIDTH), state,
                  _const((1, GLA_WIDTH)), _const(spread.shape)],
        out_specs=[rows(GLA_WIDTH), state],
        out_shape=[jax.ShapeDtypeStruct((groups * rows_per_group, GLA_WIDTH), BF16),
                   jax.ShapeDtypeStruct((groups, GLA_HEADS, GLA_DK, GLA_DV), F32)],
        scratch_shapes=[pltpu.VMEM((gpt, GLA_HEADS // 2, 2 * GLA_DV, LANES), F32),
                        pltpu.VMEM((SUB + tc, GLA_QK), F32), pltpu.VMEM((SUB + tc, GLA_QK), F32),
                        pltpu.VMEM((tc, SUB * GLA_QK), BF16), pltpu.VMEM((tc, GLA_WIDTH), F32)],
        compiler_params=_params("arbitrary", "arbitrary"),
        name="gla",
    )(gq, gk, gv, la, gr, s0, g_out, spread)


def _mlp_tile(x_ref, a_ref, b_ref, g1_ref, sh2_ref, sc2_ref, g2_ref, gn_ref, wo_ref, wu_ref, wd_ref, y_ref, *, gpt):
    tm, d = x_ref.shape

    def per_group(val, ref, scale_plus_one=False):
        m = ref[...]
        if scale_plus_one:
            m = 1.0 + m
        return (val.reshape(gpt, tm // gpt, d) * m).reshape(tm, d)

    mix = jnp.concatenate([a_ref[...], b_ref[...]], axis=1)
    x1 = x_ref[...] + per_group(_dot(mix, wo_ref[...]), g1_ref)
    xn = x1 * lax.rsqrt(jnp.mean(x1 * x1, axis=-1, keepdims=True) + EPS) * gn_ref[...]
    h2 = (per_group(xn, sc2_ref, True).reshape(gpt, tm // gpt, d) + sh2_ref[...]).reshape(tm, d).astype(BF16)
    acc = jnp.zeros((tm, d), F32)
    for j in range(wu_ref.shape[1] // FF_SLICE):
        u = jnp.maximum(_dot(h2, wu_ref[:, j * FF_SLICE:(j + 1) * FF_SLICE]), 0.0)
        acc += _dot((u * u).astype(BF16), wd_ref[j * FF_SLICE:(j + 1) * FF_SLICE, :])
    y_ref[...] = x1 + per_group(acc, g2_ref)


def _mlp_kernel(*refs, tiles, gpts):
    per_phase = 7
    gn_ref, wo_ref, wu_ref, wd_ref = refs[len(tiles) * per_phase:len(tiles) * per_phase + 4]
    y_refs = refs[len(tiles) * per_phase + 4:]
    i = pl.program_id(0)
    first = 0
    for p, (n_tiles, gpt) in enumerate(zip(tiles, gpts)):
        ins = refs[p * per_phase:(p + 1) * per_phase]

        @pl.when(jnp.logical_and(i >= first, i < first + n_tiles))
        def _(ins=ins, p=p, gpt=gpt):
            _mlp_tile(*ins, gn_ref, wo_ref, wu_ref, wd_ref, y_refs[p], gpt=gpt)

        first += n_tiles


def _mlp(phases, w, *, tm):
    d = phases[0][0].shape[1]
    tiles = [ph[0].shape[0] // tm for ph in phases]
    gpts = [max(1, tm // ph[4]) for ph in phases]
    in_specs, args, first = [], [], 0
    for (x2, a_out, b_out, mod4, rpg), n_tiles, gpt in zip(phases, tiles, gpts):
        tpg = max(1, rpg // tm)

        def tile(i, first=first, n_tiles=n_tiles):
            return jnp.clip(i - first, 0, n_tiles - 1)

        def rows(c, tile=tile):
            return pl.BlockSpec((tm, c), lambda i: (tile(i), 0))

        def mod_spec(j, tile=tile, gpt=gpt, tpg=tpg):
            return pl.BlockSpec((gpt, None, 1, d), lambda i: (tile(i) // tpg, j, 0, 0))

        in_specs += [rows(d), rows(MLA_WIDTH), rows(GLA_WIDTH), mod_spec(2), mod_spec(3), mod_spec(4), mod_spec(5)]
        args += [x2, a_out, b_out, mod4, mod4, mod4, mod4]
        first += n_tiles
    out_specs, first = [], 0
    for n_tiles in tiles:
        out_specs.append(pl.BlockSpec((tm, d), lambda i, first=first, n_tiles=n_tiles:
                                      (jnp.clip(i - first, 0, n_tiles - 1), 0)))
        first += n_tiles
    return pl.pallas_call(
        functools.partial(_mlp_kernel, tiles=tuple(tiles), gpts=tuple(gpts)),
        grid=(sum(tiles),),
        in_specs=in_specs + [_const((1, d)), _const(w["w_out"].shape), _const(w["w_up"].shape),
                             _const(w["w_down"].shape)],
        out_specs=out_specs,
        out_shape=[jax.ShapeDtypeStruct(ph[0].shape, F32) for ph in phases],
        compiler_params=_params("arbitrary"),
        name="out_proj_mlp",
    )(*args, w["g_norm2"], w["w_out"], w["w_up"], w["w_down"])


def _rope_table(start, count, repeat=1):
    half = QK_ROPE // 2
    inv = ROPE_THETA ** (-np.arange(half, dtype=np.float64) / half)
    ang = (start + np.arange(count, dtype=np.float64))[:, None] * inv[None, :]
    c, s, z = np.cos(ang), np.sin(ang), np.zeros_like(ang)
    tab = np.concatenate([c, c, z, z, -s, z, z, z, z, s, z, z], axis=1).astype(np.float32)
    return jnp.asarray(np.tile(tab, (repeat, 1)))


def _pad_gain(g_rope):
    return jnp.concatenate([g_rope, jnp.zeros((LANES - QK_ROPE,), F32)]).reshape(1, LANES)


def _relayout_kernel(wint_ref, wuq_ref, wukv_ref, wg_ref, win_o, wuq_o, wukv_o, wukt_o, wuv_o, wg_o):
    s = np.cumsum([0, Q_LORA, KV_LORA, QK_ROPE, GLA_QK, GLA_QK, GLA_WIDTH, GLA_GATE_RANK, GLA_WIDTH])

    def piece(i):
        return wint_ref[int(s[i]):int(s[i + 1]), :]

    d = wint_ref.shape[1]
    zeros = jnp.zeros((LANES - QK_ROPE - GLA_GATE_RANK, d), F32)
    win_o[:, 0:Q_LORA] = piece(0).T.astype(BF16)
    win_o[:, Q_LORA:C_QKR[1]] = jnp.concatenate([piece(2), piece(6), zeros], axis=0).T.astype(BF16)
    win_o[:, C_KV[0]:C_KV[1]] = piece(1).T.astype(BF16)
    win_o[:, C_GQ[0]:C_GQ[1]] = (piece(3).T * (GLA_DK ** -0.5)).astype(BF16)
    win_o[:, C_GK[0]:C_GK[1]] = piece(4).T.astype(BF16)
    win_o[:, C_GV[0]:C_GV[1]] = piece(5).T.astype(BF16)
    win_o[:, C_GR[0]:C_GR[1]] = piece(7).T.astype(BF16)

    zq = jnp.zeros((Q_LORA, HEAD_PAD - QK_DIM), BF16)
    kvw = QK_NOPE + V_DIM
    for hh in range(MLA_HEADS):
        wuq_o[:, hh * HEAD_PAD:hh * HEAD_PAD + QK_DIM] = wuq_ref[:, hh * QK_DIM:(hh + 1) * QK_DIM].astype(BF16)
        wuq_o[:, hh * HEAD_PAD + QK_DIM:(hh + 1) * HEAD_PAD] = zq
        uk = wukv_ref[:, hh * kvw:hh * kvw + QK_NOPE]
        uv = wukv_ref[:, hh * kvw + QK_NOPE:(hh + 1) * kvw].astype(BF16)
        wukv_o[:, hh * QK_NOPE:(hh + 1) * QK_NOPE] = uk.astype(BF16)
        wukv_o[:, (MLA_HEADS + hh) * V_DIM:(MLA_HEADS + hh + 1) * V_DIM] = uv
        wukt_o[hh * QK_NOPE:(hh + 1) * QK_NOPE, :] = uk.T.astype(BF16)
        wuv_o[:, hh * V_DIM:(hh + 1) * V_DIM] = uv

    wg_o[...] = jnp.zeros(wg_o.shape, BF16)
    wg_o[QK_ROPE:QK_ROPE + GLA_GATE_RANK, :] = wg_ref[...].astype(BF16)


def _relayout(w_in_t, w_uq, w_ukv, w_gate_up):
    d = w_in_t.shape[1]
    shapes = [(d, C_GR[1]), (Q_LORA, MLA_HEADS * HEAD_PAD), (KV_LORA, MLA_HEADS * (QK_NOPE + V_DIM)),
              (MLA_HEADS * QK_NOPE, KV_LORA), (KV_LORA, MLA_WIDTH), (LANES, GLA_QK)]
    return pl.pallas_call(
        _relayout_kernel,
        out_shape=[jax.ShapeDtypeStruct(sh, BF16) for sh in shapes],
        compiler_params=pltpu.CompilerParams(vmem_limit_bytes=VMEM_LIMIT),
        name="weight_relayout",
    )(w_in_t, w_uq, w_ukv, w_gate_up)


def _prep_weights(w_in, g_norm1, g_q_lora, w_uq, g_kv_lora, w_ukv, g_q_head, g_k_head,
                  w_gate_up, b_gate_up, g_gla_out, w_out, g_norm2, w_up, w_down):
    d = w_in.shape[0]
    w_in_p, w_uq_p, w_ukv_p, w_ukt, w_uv, w_gate = _relayout(w_in.T, w_uq, w_ukv, w_gate_up)
    qscale = QK_DIM ** -0.5 * LOG2E
    bound = 1.02 * QK_DIM ** 0.5 * LOG2E * jnp.max(jnp.abs(g_q_head)) * jnp.max(jnp.abs(g_k_head))
    lane = jnp.arange(LANES) == QK_ROPE
    return {
        "w_in": w_in_p, "g_norm1": g_norm1.reshape(1, d), "g_q_lora": g_q_lora.reshape(1, Q_LORA),
        "w_uq": w_uq_p, "g_kv_lora": g_kv_lora.reshape(1, KV_LORA), "w_ukv": w_ukv_p, "w_ukt": w_ukt, "w_uv": w_uv,
        "gkr_col": g_k_head[QK_NOPE:].reshape(QK_ROPE, 1),
        "qone": lane.astype(F32).reshape(1, LANES), "kbias": jnp.where(lane, -bound, 0.0).reshape(1, LANES),
        "fast_softmax": (bound <= MAX_FIXED_SHIFT).astype(jnp.int32).reshape(1), "bound": bound.reshape(1),
        "gqn": (g_q_head[:QK_NOPE] * qscale).reshape(1, LANES), "gqr": _pad_gain(g_q_head[QK_NOPE:] * qscale),
        "gkn": g_k_head[:QK_NOPE].reshape(1, LANES), "gkr": _pad_gain(g_k_head[QK_NOPE:]),
        "w_gate": w_gate, "b_gate": b_gate_up.reshape(1, GLA_QK),
        "g_gla_out": g_gla_out.reshape(1, GLA_WIDTH), "w_out": w_out.astype(BF16),
        "g_norm2": g_norm2.reshape(1, d), "w_up": w_up.astype(BF16), "w_down": w_down.astype(BF16),
    }


def _mixers(x, mod4, past_lat, past_kr, s0, w, *, tm):
    batch, seq, d = x.shape
    n = batch * seq
    past = 0 if past_lat is None else past_lat.shape[1]
    x2 = x.reshape(n, d)
    tm = min(tm, n)
    tab = _rope_table(past, seq, repeat=max(1, tm // seq))
    if past == 0:
        lat, krt, q, k, v, gq, gk, gv, la, gr, dmin = _projection(x2, mod4, tab, w, rows_per_group=seq, tm=tm,
                                                                  prompt=True)
        kr = jnp.swapaxes(krt, 1, 2)
        a_out = _attention_prompt(w["fast_softmax"], q, k, v, batch=batch, seq=seq, tq=min(ATTN_TILE, seq))
    else:
        assert seq == CHUNK and past % CHUNK == 0
        lat, kr, krt, q, gq, gk, gv, la, gr, dmin = _projection(x2, mod4, tab, w, rows_per_group=seq, tm=tm,
                                                                prompt=False)
        attend = functools.partial(_attention_sample, q, past_lat, jnp.swapaxes(past_kr, 1, 2), lat, krt, w,
                                   tkb=min(CACHE_BLOCK, past))
        a_out = lax.cond(w["fast_softmax"][0] == 1, functools.partial(attend, fast=True),
                         functools.partial(attend, fast=False))
    gap = -(SUB - 1) * LOG2E * jnp.min(dmin)
    gla = functools.partial(_gla, gq, gk, gv, la, gr, s0, w["g_gla_out"], groups=batch, rows_per_group=seq, tc=tm)
    b_out, s_new = lax.cond(gap <= MAX_SUB_DECAY, functools.partial(gla, fast=True), functools.partial(gla, fast=False))
    return (x2, a_out, b_out, mod4, seq), (lat.reshape(batch, seq, KV_LORA), kr.reshape(batch, seq, QK_ROPE), s_new)


def kernel(x_prompt, x_sample, cache_mla_latent, cache_mla_krope, state_gla, c_prompt, c_sample,
           w_ada, b_ada, g_norm1, w_in, g_q_lora, w_uq, g_kv_lora, w_ukv, g_q_head, g_k_head,
           w_gate_up, b_gate_up, g_gla_out, w_out, g_norm2, w_up, w_down):
    nb = x_prompt.shape[0]
    depth = w_ada.shape[0]
    y_p, y_s = x_prompt, x_sample
    outs = [[] for _ in range(6)]
    for l in range(depth):
        w = _prep_weights(w_in[l], g_norm1[l], g_q_lora[l], w_uq[l], g_kv_lora[l], w_ukv[l], g_q_head[l],
                          g_k_head[l], w_gate_up[l], b_gate_up[l], g_gla_out[l], w_out[l], g_norm2[l],
                          w_up[l], w_down[l])
        mod_p, mod_s = _modulation(c_prompt, c_sample, w_ada[l], b_ada[l])
        zero_state = jnp.zeros((nb, GLA_HEADS, GLA_DK, GLA_DV), x_prompt.dtype)
        mix_p, new_p = _mixers(y_p, mod_p, None, None, zero_state, w, tm=ROW_TILE)
        mix_s, new_s = _mixers(y_s, mod_s, cache_mla_latent[l], cache_mla_krope[l], state_gla[l], w, tm=ROW_TILE)
        y2_p, y2_s = _mlp([mix_p, mix_s], w, tm=ROW_TILE)
        y_p, y_s = y2_p.reshape(y_p.shape), y2_s.reshape(y_s.shape)
        for o, new in zip(outs, new_p + new_s):
            o.append(new)
    return (y_p, y_s) + tuple(jnp.stack(o) for o in outs)
```

```python
import functools

import jax
import jax.numpy as jnp
import numpy as np
from jax import lax
from jax.experimental import pallas as pl
from jax.experimental.pallas import tpu as pltpu

F32 = jnp.float32
BF16 = jnp.bfloat16

CHUNK = 64
EPS = 1e-6
MLA_HEADS = 4
Q_LORA = 384
KV_LORA = 256
QK_NOPE = 128
QK_ROPE = 64
QK_DIM = QK_NOPE + QK_ROPE
V_DIM = 128
ROPE_THETA = 10000.0
GLA_HEADS = 4
GLA_DK = 64
GLA_DV = 128
GLA_GATE_RANK = 16
GLA_TAU = 16.0
GLA_QK = GLA_HEADS * GLA_DK
GLA_WIDTH = GLA_HEADS * GLA_DV
MLA_WIDTH = MLA_HEADS * V_DIM
HEAD_PAD = 256
SUB = 8
LOG2E = 1.4426950408889634
MAX_FIXED_SHIFT = 48.0
MAX_SUB_DECAY = 60.0

LANES = 128
VMEM_LIMIT = 56 * 1024 * 1024
ROW_TILE = 512
ATTN_TILE = 512
CACHE_BLOCK = 2048
FF_SLICE = 1024

C_QKR = (0, 512)
C_KV = (512, 768)
C_GQ = (768, 1024)
C_GK = (1024, 1280)
C_GV = (1280, 1792)
C_GR = (1792, 2304)


def _dot(a, b):
    return jnp.dot(a, b, preferred_element_type=F32)


def _dot_nt(a, b):
    return lax.dot_general(a, b, (((1,), (1,)), ((), ())), preferred_element_type=F32)


def _dot_tn(a, b):
    return lax.dot_general(a, b, (((0,), (0,)), ((), ())), preferred_element_type=F32)


def _rope_tile(t, c, sa, sb):
    return t * c + pltpu.roll(t, 96, 1) * sa + pltpu.roll(t, 32, 1) * sb


def _params(*sem):
    return pltpu.CompilerParams(dimension_semantics=sem, vmem_limit_bytes=VMEM_LIMIT)


def _const(shape):
    return pl.BlockSpec(shape, lambda *_: (0,) * len(shape), pipeline_mode=pl.Buffered(1))


def _mod_kernel(cs_ref, cp_ref, w_ref, b_ref, os_ref, op_ref):
    ns = cs_ref.shape[0]
    c = jnp.concatenate([cs_ref[...], cp_ref[...]], axis=0)
    s = (c * jax.nn.sigmoid(c)).astype(BF16)
    mod = _dot(s, w_ref[...].astype(BF16)) + b_ref[...]
    for r in range(ns):
        os_ref[r] = mod[r:r + 1]
    for r in range(op_ref.shape[0]):
        op_ref[r] = mod[ns + r:ns + r + 1]


def _modulation(c_prompt, c_sample, w_ada, b_ada):
    d, n = w_ada.shape
    vectors = n // d

    def whole(c):
        return pl.BlockSpec(c.shape, lambda j: (0, 0))

    def vector(c):
        return pl.BlockSpec((c.shape[0], None, 1, d), lambda j: (0, j, 0, 0))

    def shape(c):
        return jax.ShapeDtypeStruct((c.shape[0], vectors, 1, d), F32)

    mod_s, mod_p = pl.pallas_call(
        _mod_kernel,
        grid=(vectors,),
        in_specs=[whole(c_sample), whole(c_prompt),
                  pl.BlockSpec((d, d), lambda j: (0, j)),
                  pl.BlockSpec((1, d), lambda j: (0, j))],
        out_specs=[vector(c_sample), vector(c_prompt)],
        out_shape=[shape(c_sample), shape(c_prompt)],
        compiler_params=_params("arbitrary"),
        name="adaln_mod",
    )(c_sample, c_prompt, w_ada, b_ada.reshape(1, n))
    return mod_p, mod_s


def _proj_kernel(x_ref, sh_ref, sc_ref, g1_ref, win_ref, gql_ref, wuq_ref, gkv_ref, wukv_ref,
                 gqn_ref, gqr_ref, gkn_ref, gkr_ref, qone_ref, kbias_ref, tab_ref, wg_ref, bg_ref,
                 *outs, gpt, prompt):
    if prompt:
        lat_ref, krt_ref, q_ref, k_ref, v_ref, gq_ref, gk_ref, gv_ref, la_ref, gr_ref, dmin_ref = outs
    else:
        lat_ref, kr_ref, krt_ref, q_ref, gq_ref, gk_ref, gv_ref, la_ref, gr_ref, dmin_ref = outs
    tm, d = x_ref.shape
    x = x_ref[...]
    xn = x * lax.rsqrt(jnp.mean(x * x, axis=-1, keepdims=True) + EPS)
    h = (xn.reshape(gpt, tm // gpt, d) * (g1_ref[...] * (1.0 + sc_ref[...])) + sh_ref[...]).reshape(tm, d)
    hb = h.astype(BF16)

    def col(c):
        return _dot(hb, win_ref[:, c[0]:c[1]])

    tab = tab_ref[...]
    cos, sa, sb = tab[:, 0:LANES], tab[:, LANES:2 * LANES], tab[:, 2 * LANES:3 * LANES]

    qkr = col(C_QKR)
    krg = qkr[:, Q_LORA:]
    z = _dot(krg.astype(BF16), wg_ref[...]) + bg_ref[...]
    la = (jnp.minimum(z, 0.0) - jnp.log(1.0 + jnp.exp(-jnp.abs(z)))) * (1.0 / GLA_TAU)
    la_ref[...] = la
    dmin_ref[...] = jnp.min(la, axis=0, keepdims=True)
    r = col(C_GR)
    gr_ref[...] = (r * jax.nn.sigmoid(r)).astype(BF16)

    cq = qkr[:, 0:Q_LORA]
    cqn = cq * lax.rsqrt(jnp.mean(cq * cq, axis=-1, keepdims=True) + EPS) * gql_ref[...]
    qp = _dot(cqn.astype(BF16), wuq_ref[...])
    for hh in range(MLA_HEADS):
        nope = qp[:, hh * HEAD_PAD:hh * HEAD_PAD + QK_NOPE]
        rt = qp[:, hh * HEAD_PAD + QK_NOPE:(hh + 1) * HEAD_PAD]
        ss = jnp.sum(nope * nope, axis=-1, keepdims=True) + jnp.sum(rt * rt, axis=-1, keepdims=True)
        inv = lax.rsqrt(ss * (1.0 / QK_DIM) + EPS)
        q_ref[:, hh * HEAD_PAD:hh * HEAD_PAD + QK_NOPE] = (nope * inv * gqn_ref[...]).astype(BF16)
        rq = _rope_tile(rt * inv * gqr_ref[...], cos, sa, sb)
        q_ref[:, hh * HEAD_PAD + QK_NOPE:(hh + 1) * HEAD_PAD] = (rq + qone_ref[...]).astype(BF16)

    ckv = col(C_KV)
    lat = ckv * lax.rsqrt(jnp.mean(ckv * ckv, axis=-1, keepdims=True) + EPS) * gkv_ref[...]
    lat_ref[...] = lat
    krt_ref[...] = krg.T[0:QK_ROPE, :]
    if prompt:
        lane = lax.broadcasted_iota(jnp.int32, (1, LANES), 1)
        krm = jnp.where(lane < QK_ROPE, krg, 0.0)
        ssr = jnp.sum(krm * krm, axis=-1, keepdims=True)
        rk = _rope_tile(krm * gkr_ref[...], cos, sa, sb)
        kv = _dot(lat.astype(BF16), wukv_ref[...])
        for hh in range(MLA_HEADS):
            kn = kv[:, hh * QK_NOPE:(hh + 1) * QK_NOPE]
            inv = lax.rsqrt((jnp.sum(kn * kn, axis=-1, keepdims=True) + ssr) * (1.0 / QK_DIM) + EPS)
            k_ref[:, hh * HEAD_PAD:hh * HEAD_PAD + QK_NOPE] = (kn * inv * gkn_ref[...]).astype(BF16)
            k_ref[:, hh * HEAD_PAD + QK_NOPE:(hh + 1) * HEAD_PAD] = (rk * inv + kbias_ref[...]).astype(BF16)
        v_ref[...] = kv[:, MLA_HEADS * QK_NOPE:].astype(BF16)
    else:
        kr_ref[...] = krg[:, 0:QK_ROPE]

    gq_ref[...] = col(C_GQ)
    gk_ref[...] = col(C_GK)
    gv_ref[...] = col(C_GV).astype(BF16)


def _projection(x2, mod4, tab, w, *, rows_per_group, tm, prompt):
    n, d = x2.shape
    gpt = max(1, tm // rows_per_group)
    tpg = max(1, rows_per_group // tm)
    ntab = tab.shape[0] // tm

    def mod_spec(j):
        return pl.BlockSpec((gpt, None, 1, d), lambda i: ((i // tpg) if gpt == 1 else i, j, 0, 0))

    def rows(c):
        return pl.BlockSpec((tm, c), lambda i: (i, 0))

    def out(c, t):
        return rows(c), jax.ShapeDtypeStruct((n, c), t)

    dmin = (pl.BlockSpec((None, 1, GLA_QK), lambda i: (i, 0, 0)), jax.ShapeDtypeStruct((n // tm, 1, GLA_QK), F32))
    gla_outs = [out(GLA_QK, F32), out(GLA_QK, F32), out(GLA_WIDTH, BF16), out(GLA_QK, F32), out(GLA_WIDTH, BF16),
                dmin]
    qo = out(MLA_HEADS * HEAD_PAD, BF16)
    if prompt:
        krt = (pl.BlockSpec((None, QK_ROPE, tm), lambda i: (i // tpg, 0, i % tpg)),
               jax.ShapeDtypeStruct((n // rows_per_group, QK_ROPE, rows_per_group), F32))
        outs = [out(KV_LORA, F32), krt, qo, out(MLA_HEADS * HEAD_PAD, BF16), out(MLA_WIDTH, BF16)] + gla_outs
    else:
        krt = (pl.BlockSpec((QK_ROPE, tm), lambda i: (0, i)), jax.ShapeDtypeStruct((QK_ROPE, n), F32))
        outs = [out(KV_LORA, F32), out(QK_ROPE, F32), krt, qo] + gla_outs
    return pl.pallas_call(
        functools.partial(_proj_kernel, gpt=gpt, prompt=prompt),
        grid=(n // tm,),
        in_specs=[rows(d), mod_spec(0), mod_spec(1), _const((1, d)), _const(w["w_in"].shape),
                  _const((1, Q_LORA)), _const(w["w_uq"].shape), _const((1, KV_LORA)), _const(w["w_ukv"].shape),
                  _const((1, LANES)), _const((1, LANES)), _const((1, LANES)), _const((1, LANES)),
                  _const((1, LANES)), _const((1, LANES)),
                  pl.BlockSpec((tm, 3 * LANES), lambda i: (i % ntab, 0)),
                  _const(w["w_gate"].shape), _const((1, GLA_QK))],
        out_specs=[o[0] for o in outs],
        out_shape=[o[1] for o in outs],
        compiler_params=_params("arbitrary"),
        name="in_proj",
    )(x2, mod4, mod4, w["g_norm1"], w["w_in"], w["g_q_lora"], w["w_uq"], w["g_kv_lora"], w["w_ukv"],
      w["gqn"], w["gqr"], w["gkn"], w["gkr"], w["qone"], w["kbias"], tab, w["w_gate"], w["b_gate"])


def _attn_prompt_kernel(fast_ref, q_ref, k_ref, v_ref, o_ref, vx_ref, m_ref, acc_ref, *, tq):
    i = pl.program_id(1)
    heads = range(MLA_HEADS)

    @pl.when(i == 0)
    def _():
        for hh in heads:
            vx_ref[hh, :, 0:V_DIM] = v_ref[:, hh * V_DIM:(hh + 1) * V_DIM]
            vx_ref[hh, :, V_DIM:] = jnp.ones((v_ref.shape[0], V_DIM), BF16)

    def scores(j, hh, masked):
        hs = slice(hh * HEAD_PAD, (hh + 1) * HEAD_PAD)
        s = _dot_nt(q_ref[:, hs], k_ref[pl.ds(pl.multiple_of(j * tq, tq), tq), hs])
        if masked:
            qc = lax.broadcasted_iota(jnp.int32, (tq, tq), 0) // CHUNK
            kc = lax.broadcasted_iota(jnp.int32, (tq, tq), 1) // CHUNK
            s = jnp.where(kc <= qc, s, -jnp.inf)
        return s

    def fast_diagonal():
        hq = tq // 2
        k0 = pl.multiple_of(i * tq, tq)
        qc = lax.broadcasted_iota(jnp.int32, (hq, hq), 0) // CHUNK
        kc = lax.broadcasted_iota(jnp.int32, (hq, hq), 1) // CHUNK
        tri = kc <= qc
        for hh in heads:
            hs = slice(hh * HEAD_PAD, (hh + 1) * HEAD_PAD)
            top = jnp.where(tri, _dot_nt(q_ref[0:hq, hs], k_ref[pl.ds(k0, hq), hs]), -jnp.inf)
            acc_ref[hh, 0:hq] = _dot(jnp.exp2(top).astype(BF16), vx_ref[hh, pl.ds(k0, hq), :])
            bot = _dot_nt(q_ref[hq:tq, hs], k_ref[pl.ds(k0, tq), hs])
            bot = jnp.concatenate([bot[:, 0:hq], jnp.where(tri, bot[:, hq:tq], -jnp.inf)], axis=1)
            acc_ref[hh, hq:tq] = _dot(jnp.exp2(bot).astype(BF16), vx_ref[hh, pl.ds(k0, tq), :])

    def vblock(j, hh):
        return vx_ref[hh, pl.ds(pl.multiple_of(j * tq, tq), tq), :]

    def fast_block(j, masked):
        for hh in heads:
            acc_ref[hh] += _dot(jnp.exp2(scores(j, hh, masked)).astype(BF16), vblock(j, hh))

    def safe_block(j, masked):
        for hh in heads:
            s = scores(j, hh, masked)
            m = m_ref[hh]
            m_new = jnp.maximum(m, jnp.max(s, axis=1, keepdims=True))
            p = jnp.exp2(s - m_new).astype(BF16)
            acc_ref[hh] = jnp.exp2(m - m_new) * acc_ref[hh] + _dot(p, vblock(j, hh))
            m_ref[hh] = m_new

    def sweep(block):
        def body(jj, c):
            block(2 * jj, False)
            block(2 * jj + 1, False)
            return c

        lax.fori_loop(0, i // 2, body, 0)

        @pl.when(i % 2 == 1)
        def _():
            block(i - 1, False)

    @pl.when(fast_ref[0] == 1)
    def _():
        fast_diagonal()
        sweep(fast_block)

    @pl.when(fast_ref[0] != 1)
    def _():
        m_ref[...] = jnp.full(m_ref.shape, -jnp.inf, F32)
        acc_ref[...] = jnp.zeros(acc_ref.shape, F32)
        safe_block(i, True)
        sweep(safe_block)

    for hh in heads:
        acc = acc_ref[hh]
        o_ref[:, hh * V_DIM:(hh + 1) * V_DIM] = (acc[:, 0:V_DIM] / acc[:, V_DIM:]).astype(BF16)


def _attention_prompt(fast, q, k, v, *, batch, seq, tq):
    nq = seq // tq
    return pl.pallas_call(
        functools.partial(_attn_prompt_kernel, tq=tq),
        grid=(batch, nq),
        in_specs=[pl.BlockSpec(memory_space=pltpu.SMEM),
                  pl.BlockSpec((tq, MLA_HEADS * HEAD_PAD), lambda b, i: (b * nq + i, 0)),
                  pl.BlockSpec((seq, MLA_HEADS * HEAD_PAD), lambda b, i: (b, 0)),
                  pl.BlockSpec((seq, MLA_WIDTH), lambda b, i: (b, 0))],
        out_specs=pl.BlockSpec((tq, MLA_WIDTH), lambda b, i: (b * nq + i, 0)),
        out_shape=jax.ShapeDtypeStruct((batch * seq, MLA_WIDTH), BF16),
        scratch_shapes=[pltpu.VMEM((MLA_HEADS, seq, 2 * V_DIM), BF16), pltpu.VMEM((MLA_HEADS, tq, 1), F32),
                        pltpu.VMEM((MLA_HEADS, tq, 2 * V_DIM), F32)],
        compiler_params=_params("arbitrary", "arbitrary"),
        name="mla_attn_prompt",
    )(fast, q, k, v)


def _attn_sample_kernel(bound_ref, q_ref, lat_ref, krt_ref, latn_ref, krtn_ref, tab_ref, wukt_ref, wuv_ref,
                        gkn_ref, gkr_ref, o_ref, wq_ref, latb_ref, s_ref, *, tkb, fast):
    t = q_ref.shape[0] // 2
    past = lat_ref.shape[1]
    nk = MLA_HEADS * QK_NOPE
    half = QK_ROPE // 2
    streams = range(2)

    @pl.when(pl.program_id(0) == 0)
    def _():
        for e in streams:
            wq_ref[e, 0:nk, :] = wukt_ref[...]

    qr = []
    for e in streams:
        rows = slice(e * t, (e + 1) * t)
        parts = []
        for hh in range(MLA_HEADS):
            qn = (q_ref[rows, hh * HEAD_PAD:hh * HEAD_PAD + QK_NOPE].astype(F32) * gkn_ref[...]).astype(BF16)
            wq_ref[e, nk + hh * t:nk + (hh + 1) * t, :] = _dot(
                qn, wukt_ref[hh * QK_NOPE:(hh + 1) * QK_NOPE, :]).astype(BF16)
            parts.append(q_ref[rows, hh * HEAD_PAD + QK_NOPE:(hh + 1) * HEAD_PAD])
        qr.append(jnp.concatenate(parts, axis=0))

    def key_block(e, lat, krt, tab, width, valid):
        latb = lat.astype(BF16)
        g = _dot_nt(wq_ref[e], latb)
        ssr = jnp.sum(krt * krt, axis=0, keepdims=True)
        kg = krt * gkr_ref[...]
        x1, x2 = kg[0:half], kg[half:QK_ROPE]
        c, sn = tab[0:half], tab[half:QK_ROPE]
        rope = jnp.concatenate([x1 * c - x2 * sn, x2 * c + x1 * sn, jnp.zeros((LANES - QK_ROPE, width), F32)], axis=0)
        srope = _dot(qr[e], rope.astype(BF16))
        rows = []
        for hh in range(MLA_HEADS):
            kn = g[hh * QK_NOPE:(hh + 1) * QK_NOPE]
            inv = lax.rsqrt((jnp.sum(kn * kn, axis=0, keepdims=True) + ssr) * (1.0 / QK_DIM) + EPS)
            rows.append((g[nk + hh * t:nk + (hh + 1) * t] + srope[hh * t:(hh + 1) * t]) * inv)
        sc = jnp.concatenate(rows, axis=0) - bound_ref[0]
        if valid is not None:
            sc = jnp.where(valid, sc, -jnp.inf)
        return sc, latb

    def blocks():
        for blk in range(past // tkb):
            c0 = blk * tkb
            for e in streams:
                yield e, c0, tkb, key_block(e, lat_ref[e, c0:c0 + tkb, :], krt_ref[e, :, c0:c0 + tkb],
                                            tab_ref[:, c0:c0 + tkb], tkb, None)
        for e in streams:
            mine = lax.broadcasted_iota(jnp.int32, (1, 2 * t), 1) // t == e
            yield e, past, 2 * t, key_block(e, latn_ref[...], krtn_ref[...], tab_ref[:, past:past + 2 * t], 2 * t, mine)

    def finish(e, wlat, l):
        wlat = wlat.astype(BF16)
        for hh in range(MLA_HEADS):
            o = _dot(wlat[hh * t:(hh + 1) * t], wuv_ref[:, hh * V_DIM:(hh + 1) * V_DIM])
            o_ref[e * t:(e + 1) * t, hh * V_DIM:(hh + 1) * V_DIM] = (o / l[hh * t:(hh + 1) * t]).astype(BF16)

    if fast:
        wlat = [jnp.zeros((MLA_HEADS * t, KV_LORA), F32) for _ in streams]
        lsum = [jnp.zeros((MLA_HEADS * t, LANES), F32) for _ in streams]
        for e, _, width, (sc, latb) in blocks():
            p = jnp.exp2(sc)
            wlat[e] += _dot(p.astype(BF16), latb)
            for c in range(width // LANES):
                lsum[e] += p[:, c * LANES:(c + 1) * LANES]
        for e in streams:
            finish(e, wlat[e], jnp.sum(lsum[e], axis=1, keepdims=True))

    else:
        for e, c0, width, (sc, latb) in blocks():
            s_ref[e, :, c0:c0 + width] = sc
            latb_ref[e, c0:c0 + width, :] = latb
        for e in streams:
            s = s_ref[e]
            p = jnp.exp2(s - jnp.max(s, axis=1, keepdims=True))
            finish(e, _dot(p.astype(BF16), latb_ref[e]), jnp.sum(p, axis=1, keepdims=True))


def _attention_sample(q, past_lat, past_krt, lat_new, krt_new, w, *, tkb, fast):
    batch, past, _ = past_lat.shape
    t = q.shape[0] // batch
    half = QK_ROPE // 2
    inv = ROPE_THETA ** (-np.arange(half, dtype=np.float64) / half)
    pos = np.concatenate([np.arange(past), past + np.arange(t), past + np.arange(t)]).astype(np.float64)
    ang = inv[:, None] * pos[None, :]
    tab = jnp.asarray(np.concatenate([np.cos(ang), np.sin(ang)], axis=0).astype(np.float32))
    s_pad = past + 2 * t
    return pl.pallas_call(
        functools.partial(_attn_sample_kernel, tkb=tkb, fast=fast),
        grid=(batch // 2,),
        in_specs=[pl.BlockSpec(memory_space=pltpu.SMEM),
                  pl.BlockSpec((2 * t, MLA_HEADS * HEAD_PAD), lambda g: (g, 0)),
                  pl.BlockSpec((2, past, KV_LORA), lambda g: (g, 0, 0)),
                  pl.BlockSpec((2, QK_ROPE, past), lambda g: (g, 0, 0)),
                  pl.BlockSpec((2 * t, KV_LORA), lambda g: (g, 0)),
                  pl.BlockSpec((QK_ROPE, 2 * t), lambda g: (0, g)),
                  _const(tab.shape), _const(w["w_ukt"].shape), _const(w["w_uv"].shape),
                  _const((1, LANES)), _const((QK_ROPE, 1))],
        out_specs=pl.BlockSpec((2 * t, MLA_WIDTH), lambda g: (g, 0)),
        out_shape=jax.ShapeDtypeStruct((batch * t, MLA_WIDTH), BF16),
        scratch_shapes=[pltpu.VMEM((2, MLA_HEADS * (QK_NOPE + t), KV_LORA), BF16),
                        pltpu.VMEM((2, s_pad, KV_LORA), BF16),
                        pltpu.VMEM((2, MLA_HEADS * t, s_pad), F32)],
        compiler_params=_params("arbitrary"),
        name="mla_attn_sample",
    )(w["bound"], q, past_lat, past_krt, lat_new, krt_new, tab, w["w_ukt"], w["w_uv"], w["gkn"], w["gkr_col"])


def _gla_kernel(q_ref, k_ref, v_ref, la_ref, r_ref, s0_ref, g_ref, spread_ref, o_ref, sn_ref,
                st_ref, kp_ref, ap_ref, p_ref, on_ref, *, gpt, fast):
    t_idx = pl.program_id(1)
    L, W, P = CHUNK, GLA_QK, LANES
    R = q_ref.shape[0]
    n_chunks = R // L
    cpg = n_chunks // gpt
    n_pairs = GLA_HEADS // 2

    lane_p = lax.broadcasted_iota(jnp.int32, (1, P), 1)
    even = lane_p < GLA_DK
    bd_mask = (lax.broadcasted_iota(jnp.int32, (2 * GLA_DV, P), 0) // GLA_DV
               == lax.broadcasted_iota(jnp.int32, (2 * GLA_DV, P), 1) // GLA_DK)

    @pl.when(t_idx == 0)
    def _():
        kp_ref[0:SUB, :] = jnp.zeros((SUB, W), F32)
        ap_ref[0:SUB, :] = jnp.zeros((SUB, W), F32)
        for gi in range(gpt):
            for pr in range(n_pairs):
                tt = s0_ref[gi, 2 * pr:2 * pr + 2].reshape(2 * GLA_DK, GLA_DV).T
                st_ref[gi, pr] = jnp.where(bd_mask, jnp.concatenate([tt, tt], axis=0), 0.0)

    q = q_ref[...]
    k = k_ref[...]
    la = la_ref[...]

    tri = (lax.broadcasted_iota(jnp.int32, (L, L), 0) >= lax.broadcasted_iota(jnp.int32, (L, L), 1)).astype(BF16)
    la_hi = la.astype(BF16)
    la2 = jnp.concatenate([la_hi, (la - la_hi.astype(F32)).astype(BF16)], axis=1)
    bs = []
    for c in range(n_chunks):
        t2 = _dot(tri, la2[c * L:(c + 1) * L, :])
        bs.append(t2[:, 0:W] + t2[:, W:2 * W])
    b = (jnp.concatenate(bs, axis=0) if n_chunks > 1 else bs[0]) * LOG2E
    b3 = b.reshape(n_chunks, L, W)

    def chunk_row(r):
        return jnp.broadcast_to(b3[:, r:r + 1, :], (n_chunks, L, W)).reshape(R, W)

    b_sub = jnp.broadcast_to(b.reshape(R // SUB, SUB, W)[:, 0:1, :], (R // SUB, SUB, W)).reshape(R, W)
    sub = (lax.broadcasted_iota(jnp.int32, (R, W), 0) % L) // SUB

    qt = q * jnp.exp2(b - b_sub)
    zb = jnp.zeros((), BF16)
    zq = jnp.zeros((SUB, P), F32)
    n_sub = L // SUB
    first = 0 if fast else 1
    ktm = []
    for i in range(first, n_sub):
        kt = (k * jnp.exp2(chunk_row(i * SUB) - b)).astype(BF16)
        keep = (sub[:, 0:P] <= i) if fast else (sub[:, 0:P] < i)
        ktm.append([[jnp.where(keep & (even if e == 0 else ~even), kt[:, pr * P:(pr + 1) * P], zb)
                     for e in range(2)] for pr in range(n_pairs)])

    if fast:
        causal = (lax.broadcasted_iota(jnp.int32, (L, P), 0) >= lax.broadcasted_iota(jnp.int32, (L, P), 1) % L)
    else:
        a = jnp.exp(la)
        kp_ref[SUB:SUB + R, :] = k
        ap_ref[SUB:SUB + R, :] = a
        p_ref[:, 0:W] = (q * k).astype(BF16)
        e = a
        for d in range(1, SUB):
            if d > 1:
                e = e * ap_ref[SUB - d + 1:SUB - d + 1 + R, :]
            p_ref[:, d * W:(d + 1) * W] = (q * kp_ref[SUB - d:SUB - d + R, :] * e).astype(BF16)
        cband = _dot(p_ref[...], spread_ref[...])
        same_sub = (lax.broadcasted_iota(jnp.int32, (L, W), 0) // SUB
                    == (lax.broadcasted_iota(jnp.int32, (L, W), 1) % L) // SUB)

    qe = (q * jnp.exp2(b)).astype(BF16)
    kd = (k * jnp.exp2(chunk_row(L - 1) - b)).astype(BF16)
    zv = jnp.zeros((L, 2 * GLA_DV), BF16)

    o_intra, d_st, dec = {}, {}, []
    for c in range(n_chunks):
        rs = slice(c * L, (c + 1) * L)
        if not fast:
            a_band = jnp.where(same_sub, pltpu.roll(cband[rs], W - (SUB - 1), 1, stride=1, stride_axis=0), 0.0)
        dec.append(jnp.exp2(b[c * L + L - 1:c * L + L, :]))
        for pr in range(n_pairs):
            ls = slice(pr * P, (pr + 1) * P)
            lhs_c = jnp.concatenate(
                [jnp.concatenate([qt[c * L + r * SUB:c * L + (r + 1) * SUB, ls] if i == r else zq
                                  for i in range(first, n_sub)], axis=1) for r in range(n_sub)],
                axis=0).astype(BF16)
            rhs_c = jnp.concatenate([jnp.concatenate([m[pr][0][rs], m[pr][1][rs]], axis=0) for m in ktm], axis=1)
            scores = _dot_nt(lhs_c, rhs_c)
            a_tot = (jnp.where(causal, scores, 0.0) if fast else a_band[:, ls] + scores).astype(BF16)
            vp = v_ref[rs, 2 * pr * GLA_DV:(2 * pr + 2) * GLA_DV]
            v_bd = jnp.concatenate([jnp.concatenate([vp[:, 0:GLA_DV], zv[:, 0:GLA_DV]], axis=1),
                                    jnp.concatenate([zv[:, 0:GLA_DV], vp[:, GLA_DV:]], axis=1)], axis=0)
            o_intra[c, pr] = _dot(a_tot, v_bd)
            d_st[c, pr] = jnp.where(bd_mask, _dot_tn(vp, kd[rs, ls]), 0.0)
    st_in = {}
    for gi in range(gpt):
        for pr in range(n_pairs):
            st = st_ref[gi, pr]
            for c in range(gi * cpg, (gi + 1) * cpg):
                st_in[c, pr] = st.astype(BF16)
                st = st * dec[c][:, pr * P:(pr + 1) * P] + d_st[c, pr]
            st_ref[gi, pr] = st
    for c in range(n_chunks):
        rs = slice(c * L, (c + 1) * L)
        for pr in range(n_pairs):
            on_ref[rs, 2 * pr * GLA_DV:(2 * pr + 2) * GLA_DV] = (
                o_intra[c, pr] + _dot_nt(qe[rs, pr * P:(pr + 1) * P], st_in[c, pr]))

    for hh in range(GLA_HEADS):
        hs = slice(hh * GLA_DV, (hh + 1) * GLA_DV)
        o = on_ref[:, hs]
        on = o * lax.rsqrt(jnp.mean(o * o, axis=-1, keepdims=True) + EPS) * g_ref[:, hs]
        o_ref[:, hs] = (on * r_ref[:, hs].astype(F32)).astype(BF16)

    @pl.when(t_idx == pl.num_programs(1) - 1)
    def _():
        for gi in range(gpt):
            for pr in range(n_pairs):
                st = st_ref[gi, pr]
                tt = jnp.where(even, st[0:GLA_DV], st[GLA_DV:2 * GLA_DV])
                sn_ref[gi, 2 * pr:2 * pr + 2] = tt.T.reshape(2, GLA_DK, GLA_DV)


def _band_spread():
    m = np.zeros((SUB, GLA_HEADS, GLA_DK, GLA_QK), np.float32)
    for d in range(SUB):
        for h in range(GLA_HEADS):
            m[d, h, :, h * GLA_DK + SUB - 1 - d] = 1.0
    return jnp.asarray(m.reshape(SUB * GLA_QK, GLA_QK), BF16)


def _gla(gq, gk, gv, la, gr, s0, g_out, *, groups, rows_per_group, tc, fast):
    gpt = max(1, tc // rows_per_group)
    nt = max(1, rows_per_group // tc)

    def rows(c):
        return pl.BlockSpec((tc, c), lambda g, t: (g * nt + t, 0))

    state = pl.BlockSpec((gpt, GLA_HEADS, GLA_DK, GLA_DV), lambda g, t: (g, 0, 0, 0))
    spread = _band_spread()
    return pl.pallas_call(
        functools.partial(_gla_kernel, gpt=gpt, fast=fast),
        grid=(groups // gpt, nt),
        in_specs=[rows(GLA_QK), rows(GLA_QK), rows(GLA_WIDTH), rows(GLA_QK), rows(GLA_WIDTH), state,
                  _const((1, GLA_WIDTH)), _const(spread.shape)],
        out_specs=[rows(GLA_WIDTH), state],
        out_shape=[jax.ShapeDtypeStruct((groups * rows_per_group, GLA_WIDTH), BF16),
                   jax.ShapeDtypeStruct((groups, GLA_HEADS, GLA_DK, GLA_DV), F32)],
        scratch_shapes=[pltpu.VMEM((gpt, GLA_HEADS // 2, 2 * GLA_DV, LANES), F32),
                        pltpu.VMEM((SUB + tc, GLA_QK), F32), pltpu.VMEM((SUB + tc, GLA_QK), F32),
                        pltpu.VMEM((tc, SUB * GLA_QK), BF16), pltpu.VMEM((tc, GLA_WIDTH), F32)],
        compiler_params=_params("arbitrary", "arbitrary"),
        name="gla",
    )(gq, gk, gv, la, gr, s0, g_out, spread)


def _mlp_tile(x_ref, a_ref, b_ref, g1_ref, sh2_ref, sc2_ref, g2_ref, gn_ref, wo_ref, wu_ref, wd_ref, y_ref, *, gpt):
    tm, d = x_ref.shape

    def per_group(val, ref, scale_plus_one=False):
        m = ref[...]
        if scale_plus_one:
            m = 1.0 + m
        return (val.reshape(gpt, tm // gpt, d) * m).reshape(tm, d)

    mix = jnp.concatenate([a_ref[...], b_ref[...]], axis=1)
    x1 = x_ref[...] + per_group(_dot(mix, wo_ref[...]), g1_ref)
    xn = x1 * lax.rsqrt(jnp.mean(x1 * x1, axis=-1, keepdims=True) + EPS) * gn_ref[...]
    h2 = (per_group(xn, sc2_ref, True).reshape(gpt, tm // gpt, d) + sh2_ref[...]).reshape(tm, d).astype(BF16)
    acc = jnp.zeros((tm, d), F32)
    for j in range(wu_ref.shape[1] // FF_SLICE):
        u = jnp.maximum(_dot(h2, wu_ref[:, j * FF_SLICE:(j + 1) * FF_SLICE]), 0.0)
        acc += _dot((u * u).astype(BF16), wd_ref[j * FF_SLICE:(j + 1) * FF_SLICE, :])
    y_ref[...] = x1 + per_group(acc, g2_ref)


def _mlp_kernel(*refs, tiles, gpts):
    per_phase = 7
    gn_ref, wo_ref, wu_ref, wd_ref = refs[len(tiles) * per_phase:len(tiles) * per_phase + 4]
    y_refs = refs[len(tiles) * per_phase + 4:]
    i = pl.program_id(0)
    first = 0
    for p, (n_tiles, gpt) in enumerate(zip(tiles, gpts)):
        ins = refs[p * per_phase:(p + 1) * per_phase]

        @pl.when(jnp.logical_and(i >= first, i < first + n_tiles))
        def _(ins=ins, p=p, gpt=gpt):
            _mlp_tile(*ins, gn_ref, wo_ref, wu_ref, wd_ref, y_refs[p], gpt=gpt)

        first += n_tiles


def _mlp(phases, w, *, tm):
    d = phases[0][0].shape[1]
    tiles = [ph[0].shape[0] // tm for ph in phases]
    gpts = [max(1, tm // ph[4]) for ph in phases]
    in_specs, args, first = [], [], 0
    for (x2, a_out, b_out, mod4, rpg), n_tiles, gpt in zip(phases, tiles, gpts):
        tpg = max(1, rpg // tm)

        def tile(i, first=first, n_tiles=n_tiles):
            return jnp.clip(i - first, 0, n_tiles - 1)

        def rows(c, tile=tile):
            return pl.BlockSpec((tm, c), lambda i: (tile(i), 0))

        def mod_spec(j, tile=tile, gpt=gpt, tpg=tpg):
            return pl.BlockSpec((gpt, None, 1, d), lambda i: (tile(i) // tpg, j, 0, 0))

        in_specs += [rows(d), rows(MLA_WIDTH), rows(GLA_WIDTH), mod_spec(2), mod_spec(3), mod_spec(4), mod_spec(5)]
        args += [x2, a_out, b_out, mod4, mod4, mod4, mod4]
        first += n_tiles
    out_specs, first = [], 0
    for n_tiles in tiles:
        out_specs.append(pl.BlockSpec((tm, d), lambda i, first=first, n_tiles=n_tiles:
                                      (jnp.clip(i - first, 0, n_tiles - 1), 0)))
        first += n_tiles
    return pl.pallas_call(
        functools.partial(_mlp_kernel, tiles=tuple(tiles), gpts=tuple(gpts)),
        grid=(sum(tiles),),
        in_specs=in_specs + [_const((1, d)), _const(w["w_out"].shape), _const(w["w_up"].shape),
                             _const(w["w_down"].shape)],
        out_specs=out_specs,
        out_shape=[jax.ShapeDtypeStruct(ph[0].shape, F32) for ph in phases],
        compiler_params=_params("arbitrary"),
        name="out_proj_mlp",
    )(*args, w["g_norm2"], w["w_out"], w["w_up"], w["w_down"])


def _rope_table(start, count, repeat=1):
    half = QK_ROPE // 2
    inv = ROPE_THETA ** (-np.arange(half, dtype=np.float64) / half)
    ang = (start + np.arange(count, dtype=np.float64))[:, None] * inv[None, :]
    c, s, z = np.cos(ang), np.sin(ang), np.zeros_like(ang)
    tab = np.concatenate([c, c, z, z, -s, z, z, z, z, s, z, z], axis=1).astype(np.float32)
    return jnp.asarray(np.tile(tab, (repeat, 1)))


def _pad_gain(g_rope):
    return jnp.concatenate([g_rope, jnp.zeros((LANES - QK_ROPE,), F32)]).reshape(1, LANES)


def _relayout_kernel(wint_ref, wuq_ref, wukv_ref, wg_ref, win_o, wuq_o, wukv_o, wukt_o, wuv_o, wg_o):
    s = np.cumsum([0, Q_LORA, KV_LORA, QK_ROPE, GLA_QK, GLA_QK, GLA_WIDTH, GLA_GATE_RANK, GLA_WIDTH])

    def piece(i):
        return wint_ref[int(s[i]):int(s[i + 1]), :]

    d = wint_ref.shape[1]
    zeros = jnp.zeros((LANES - QK_ROPE - GLA_GATE_RANK, d), F32)
    win_o[:, 0:Q_LORA] = piece(0).T.astype(BF16)
    win_o[:, Q_LORA:C_QKR[1]] = jnp.concatenate([piece(2), piece(6), zeros], axis=0).T.astype(BF16)
    win_o[:, C_KV[0]:C_KV[1]] = piece(1).T.astype(BF16)
    win_o[:, C_GQ[0]:C_GQ[1]] = (piece(3).T * (GLA_DK ** -0.5)).astype(BF16)
    win_o[:, C_GK[0]:C_GK[1]] = piece(4).T.astype(BF16)
    win_o[:, C_GV[0]:C_GV[1]] = piece(5).T.astype(BF16)
    win_o[:, C_GR[0]:C_GR[1]] = piece(7).T.astype(BF16)

    zq = jnp.zeros((Q_LORA, HEAD_PAD - QK_DIM), BF16)
    kvw = QK_NOPE + V_DIM
    for hh in range(MLA_HEADS):
        wuq_o[:, hh * HEAD_PAD:hh * HEAD_PAD + QK_DIM] = wuq_ref[:, hh * QK_DIM:(hh + 1) * QK_DIM].astype(BF16)
        wuq_o[:, hh * HEAD_PAD + QK_DIM:(hh + 1) * HEAD_PAD] = zq
        uk = wukv_ref[:, hh * kvw:hh * kvw + QK_NOPE]
        uv = wukv_ref[:, hh * kvw + QK_NOPE:(hh + 1) * kvw].astype(BF16)
        wukv_o[:, hh * QK_NOPE:(hh + 1) * QK_NOPE] = uk.astype(BF16)
        wukv_o[:, (MLA_HEADS + hh) * V_DIM:(MLA_HEADS + hh + 1) * V_DIM] = uv
        wukt_o[hh * QK_NOPE:(hh + 1) * QK_NOPE, :] = uk.T.astype(BF16)
        wuv_o[:, hh * V_DIM:(hh + 1) * V_DIM] = uv

    wg_o[...] = jnp.zeros(wg_o.shape, BF16)
    wg_o[QK_ROPE:QK_ROPE + GLA_GATE_RANK, :] = wg_ref[...].astype(BF16)


def _relayout(w_in_t, w_uq, w_ukv, w_gate_up):
    d = w_in_t.shape[1]
    shapes = [(d, C_GR[1]), (Q_LORA, MLA_HEADS * HEAD_PAD), (KV_LORA, MLA_HEADS * (QK_NOPE + V_DIM)),
              (MLA_HEADS * QK_NOPE, KV_LORA), (KV_LORA, MLA_WIDTH), (LANES, GLA_QK)]
    return pl.pallas_call(
        _relayout_kernel,
        out_shape=[jax.ShapeDtypeStruct(sh, BF16) for sh in shapes],
        compiler_params=pltpu.CompilerParams(vmem_limit_bytes=VMEM_LIMIT),
        name="weight_relayout",
    )(w_in_t, w_uq, w_ukv, w_gate_up)


def _prep_weights(w_in, g_norm1, g_q_lora, w_uq, g_kv_lora, w_ukv, g_q_head, g_k_head,
                  w_gate_up, b_gate_up, g_gla_out, w_out, g_norm2, w_up, w_down):
    d = w_in.shape[0]
    w_in_p, w_uq_p, w_ukv_p, w_ukt, w_uv, w_gate = _relayout(w_in.T, w_uq, w_ukv, w_gate_up)
    qscale = QK_DIM ** -0.5 * LOG2E
    bound = 1.02 * QK_DIM ** 0.5 * LOG2E * jnp.max(jnp.abs(g_q_head)) * jnp.max(jnp.abs(g_k_head))
    lane = jnp.arange(LANES) == QK_ROPE
    return {
        "w_in": w_in_p, "g_norm1": g_norm1.reshape(1, d), "g_q_lora": g_q_lora.reshape(1, Q_LORA),
        "w_uq": w_uq_p, "g_kv_lora": g_kv_lora.reshape(1, KV_LORA), "w_ukv": w_ukv_p, "w_ukt": w_ukt, "w_uv": w_uv,
        "gkr_col": g_k_head[QK_NOPE:].reshape(QK_ROPE, 1),
        "qone": lane.astype(F32).reshape(1, LANES), "kbias": jnp.where(lane, -bound, 0.0).reshape(1, LANES),
        "fast_softmax": (bound <= MAX_FIXED_SHIFT).astype(jnp.int32).reshape(1), "bound": bound.reshape(1),
        "gqn": (g_q_head[:QK_NOPE] * qscale).reshape(1, LANES), "gqr": _pad_gain(g_q_head[QK_NOPE:] * qscale),
        "gkn": g_k_head[:QK_NOPE].reshape(1, LANES), "gkr": _pad_gain(g_k_head[QK_NOPE:]),
        "w_gate": w_gate, "b_gate": b_gate_up.reshape(1, GLA_QK),
        "g_gla_out": g_gla_out.reshape(1, GLA_WIDTH), "w_out": w_out.astype(BF16),
        "g_norm2": g_norm2.reshape(1, d), "w_up": w_up.astype(BF16), "w_down": w_down.astype(BF16),
    }


def _mixers(x, mod4, past_lat, past_kr, s0, w, *, tm):
    batch, seq, d = x.shape
    n = batch * seq
    past = 0 if past_lat is None else past_lat.shape[1]
    x2 = x.reshape(n, d)
    tm = min(tm, n)
    tab = _rope_table(past, seq, repeat=max(1, tm // seq))
    if past == 0:
        lat, krt, q, k, v, gq, gk, gv, la, gr, dmin = _projection(x2, mod4, tab, w, rows_per_group=seq, tm=tm,
                                                                  prompt=True)
        kr = jnp.swapaxes(krt, 1, 2)
        a_out = _attention_prompt(w["fast_softmax"], q, k, v, batch=batch, seq=seq, tq=min(ATTN_TILE, seq))
    else:
        assert seq == CHUNK and past % CHUNK == 0
        lat, kr, krt, q, gq, gk, gv, la, gr, dmin = _projection(x2, mod4, tab, w, rows_per_group=seq, tm=tm,
                                                                prompt=False)
        attend = functools.partial(_attention_sample, q, past_lat, jnp.swapaxes(past_kr, 1, 2), lat, krt, w,
                                   tkb=min(CACHE_BLOCK, past))
        a_out = lax.cond(w["fast_softmax"][0] == 1, functools.partial(attend, fast=True),
                         functools.partial(attend, fast=False))
    gap = -(SUB - 1) * LOG2E * jnp.min(dmin)
    gla = functools.partial(_gla, gq, gk, gv, la, gr, s0, w["g_gla_out"], groups=batch, rows_per_group=seq, tc=tm)
    b_out, s_new = lax.cond(gap <= MAX_SUB_DECAY, functools.partial(gla, fast=True), functools.partial(gla, fast=False))
    return (x2, a_out, b_out, mod4, seq), (lat.reshape(batch, seq, KV_LORA), kr.reshape(batch, seq, QK_ROPE), s_new)


def kernel(x_prompt, x_sample, cache_mla_latent, cache_mla_krope, state_gla, c_prompt, c_sample,
           w_ada, b_ada, g_norm1, w_in, g_q_lora, w_uq, g_kv_lora, w_ukv, g_q_head, g_k_head,
           w_gate_up, b_gate_up, g_gla_out, w_out, g_norm2, w_up, w_down):
    nb = x_prompt.shape[0]
    depth = w_ada.shape[0]
    y_p, y_s = x_prompt, x_sample
    outs = [[] for _ in range(6)]
    for l in range(depth):
        w = _prep_weights(w_in[l], g_norm1[l], g_q_lora[l], w_uq[l], g_kv_lora[l], w_ukv[l], g_q_head[l],
                          g_k_head[l], w_gate_up[l], b_gate_up[l], g_gla_out[l], w_out[l], g_norm2[l],
                          w_up[l], w_down[l])
        mod_p, mod_s = _modulation(c_prompt, c_sample, w_ada[l], b_ada[l])
        zero_state = jnp.zeros((nb, GLA_HEADS, GLA_DK, GLA_DV), x_prompt.dtype)
        mix_p, new_p = _mixers(y_p, mod_p, None, None, zero_state, w, tm=ROW_TILE)
        mix_s, new_s = _mixers(y_s, mod_s, cache_mla_latent[l], cache_mla_krope[l], state_gla[l], w, tm=ROW_TILE)
        y2_p, y2_s = _mlp([mix_p, mix_s], w, tm=ROW_TILE)
        y_p, y_s = y2_p.reshape(y_p.shape), y2_s.reshape(y_s.shape)
        for o, new in zip(outs, new_p + new_s):
            o.append(new)
    return (y_p, y_s) + tuple(jnp.stack(o) for o in outs)
```
